```python
import math
import jax
import jax.numpy as jnp
from jax import lax
import numpy as np

D_MODEL = 1024
BATCH = 8
SEQ = 4096
DEPTH = 2

EPS = 1e-6
D_ATT = D_MODEL // 2
D_HY = D_MODEL - D_ATT
ATT_HEADS = 4
ATT_V_DIM = D_ATT // ATT_HEADS
ATT_QK_DIM = ATT_V_DIM // 2
ROPE_DIM = ATT_QK_DIM // 4
ROPE_THETA = 500000.0
Q_BLOCK = 128
HY_SHORT_CONV = 3
HY_EMB_BANDS = 16
HY_EMB_DIM = 1 + 2 * HY_EMB_BANDS
HY_FILTER_FF = 64
HY_TARGET = 1e-2
HY_FAST_DECAY = 0.3
HY_SLOW_DECAY = 1.5
HY_MIN_DECAY = math.log(HY_TARGET) / HY_SLOW_DECAY
HY_MAX_DECAY = math.log(HY_TARGET) / HY_FAST_DECAY
IN_EVEN = 3 * D_ATT + 3 * D_HY
ML_HEADS = 4
ML_V_DIM = D_MODEL // ML_HEADS
ML_QK_DIM = ML_V_DIM // 2
ML_CHUNK = 128
SHORT_CONV = 3
ML_QK_W = 2 * ML_HEADS * ML_QK_DIM
IN_ODD = ML_QK_W + 2 * D_MODEL + 4 * ML_HEADS
N_GROUPS = 4
EXPERTS_PER_GROUP = 8
N_EXPERTS = N_GROUPS * EXPERTS_PER_GROUP
TOP_K = 2
D_EXPERT = D_MODEL // 2
MOE_BLOCK = 128
N_EVEN = (DEPTH + 1) // 2
N_ODD = DEPTH // 2

kernel_name = 'hybrid_diffattn_hyena_mlstm_hmoe_encoder'


def rms_norm(x, g):
    xf = x.astype(jnp.float32)
    y = xf * lax.rsqrt(jnp.mean(xf * xf, axis=-1, keepdims=True) + EPS)
    return (y * g.astype(jnp.float32)).astype(x.dtype)


def centered_dwconv(x, w, b):
    k = w.shape[0]
    pad = k // 2
    s = x.shape[1]
    xp = jnp.pad(x, ((0, 0), (pad, pad), (0, 0)))
    out = b
    for j in range(k):
        out = out + xp[:, j:j + s] * w[j]
    return out


def rope_tables(seq):
    inv_freq = 1.0 / (ROPE_THETA ** (jnp.arange(0, ROPE_DIM, 2, dtype=jnp.float32) / ROPE_DIM))
    ang = jnp.arange(seq, dtype=jnp.float32)[:, None] * inv_freq[None, :]
    return jnp.cos(ang), jnp.sin(ang)


def partial_rope(x, cos, sin):
    half = ROPE_DIM // 2
    x1 = x[..., :half]
    x2 = x[..., half:ROPE_DIM]
    c = cos.astype(x.dtype)
    s = sin.astype(x.dtype)
    return jnp.concatenate([x1 * c - x2 * s, x2 * c + x1 * s, x[..., ROPE_DIM:]], axis=-1)


def diff_attention(u, q_norm, k_norm, lam_q1, lam_k1, lam_q2, lam_k2, subln, lambda_init, cos, sin):
    b, s, _ = u.shape
    h, dk, dv = ATT_HEADS, ATT_QK_DIM, ATT_V_DIM
    q, k, v = jnp.split(u, 3, axis=-1)
    q = q.reshape(b, s, h, 2, dk).transpose(0, 2, 3, 1, 4)
    k = k.reshape(b, s, h, 2, dk).transpose(0, 2, 3, 1, 4)
    v = v.reshape(b, s, h, dv).transpose(0, 2, 1, 3)
    q = partial_rope(rms_norm(q, q_norm), cos, sin) * (dk ** -0.5)
    k = partial_rope(rms_norm(k, k_norm), cos, sin)
    f32 = jnp.float32
    lam = (jnp.exp(jnp.sum(lam_q1.astype(f32) * lam_k1.astype(f32)))
           - jnp.exp(jnp.sum(lam_q2.astype(f32) * lam_k2.astype(f32))) + lambda_init)
    nb = s // Q_BLOCK
    qb = q.reshape(b, h, 2, nb, Q_BLOCK, dk).transpose(3, 0, 1, 2, 4, 5)

    def attend(q_blk):
        sc = jnp.einsum('bhcqd,bhckd->bhcqk', q_blk, k).astype(f32)
        p = jax.nn.softmax(sc, axis=-1)
        w = p[:, :, 0] - lam * p[:, :, 1]
        return jnp.einsum('bhqk,bhkd->bhqd', w.astype(v.dtype), v)

    o = lax.map(attend, qb)
    o = o.transpose(1, 0, 3, 2, 4).reshape(b, s, h, dv)
    o = rms_norm(o, subln) * (1.0 - lambda_init)
    return o.reshape(b, s, h * dv)


def hyena_filter(length, w1, b1, freq, w2, b2, w3):
    f32 = jnp.float32
    w1, b1, freq, w2, b2, w3 = (a.astype(f32) for a in (w1, b1, freq, w2, b2, w3))
    t = jnp.linspace(0.0, 1.0, length, dtype=f32)[:, None]
    bands = jnp.linspace(1e-4, HY_EMB_BANDS - 1, HY_EMB_BANDS, dtype=f32)[None, :]
    ang = (2.0 * math.pi / length) * jnp.arange(length, dtype=f32)[:, None] * bands
    z = jnp.concatenate([t, jnp.cos(ang), -jnp.sin(ang)], axis=-1)
    hdn = jnp.sin(freq * (z @ w1 + b1))
    hdn = jnp.sin(freq * (hdn @ w2 + b2))
    filt = hdn @ w3
    deltas = jnp.abs(jnp.linspace(HY_MIN_DECAY, HY_MAX_DECAY, D_HY, dtype=f32))
    decay = jnp.exp(-t * deltas[None, :])
    h_fwd = filt[:, :D_HY] * decay
    h_bwd = filt[:, D_HY:] * decay
    kern = jnp.concatenate([h_fwd.at[0].add(h_bwd[0]), jnp.zeros((1, D_HY), f32), h_bwd[:0:-1]], axis=0)
    return kern / (jnp.sum(jnp.abs(kern), axis=0, keepdims=True) + EPS)


def hyena_mixer(u, conv_w, conv_b, f_w1, f_b1, f_freq, f_w2, f_b2, f_w3, skip):
    length = u.shape[1]
    uc = centered_dwconv(u, conv_w, conv_b)
    x1, x2, v = jnp.split(uc, 3, axis=-1)
    z = (v * x2).astype(jnp.float32)
    kern = hyena_filter(length, f_w1, f_b1, f_freq, f_w2, f_b2, f_w3)
    n = 2 * length
    zf = jnp.fft.rfft(z, n=n, axis=1)
    kf = jnp.fft.rfft(kern, n=n, axis=0)
    y = jnp.fft.irfft(zf * kf[None], n=n, axis=1)[:, :length]
    y = y + z * skip.astype(jnp.float32)
    return (y * x1.astype(jnp.float32)).astype(u.dtype)


def mlstm_chunkwise(q, k, v, log_i, log_f):
    f32 = jnp.float32
    b, g, s, dk = q.shape
    dv = v.shape[-1]
    lc = ML_CHUNK
    nc = s // lc
    qc = jnp.moveaxis(q.astype(f32).reshape(b, g, nc, lc, dk), 2, 0)
    kc = jnp.moveaxis(k.astype(f32).reshape(b, g, nc, lc, dk), 2, 0)
    vc = jnp.moveaxis(v.astype(f32).reshape(b, g, nc, lc, dv), 2, 0)
    ic = jnp.moveaxis(log_i.astype(f32).reshape(b, g, nc, lc), 2, 0)
    fc = jnp.moveaxis(log_f.astype(f32).reshape(b, g, nc, lc), 2, 0)
    mask = jnp.tril(jnp.ones((lc, lc), dtype=bool))

    def step(carry, inp):
        c_st, n_st, m_st = carry
        qj, kj, vj, lij, lfj = inp
        bcum = jnp.cumsum(lfj, axis=-1)
        dmat = bcum[..., :, None] - bcum[..., None, :] + lij[..., None, :]
        dmat = jnp.where(mask, dmat, -jnp.inf)
        inter = bcum + m_st[..., None]
        m_t = jnp.maximum(inter, jnp.max(dmat, axis=-1))
        w_intra = jnp.exp(dmat - m_t[..., None])
        w_inter = jnp.exp(inter - m_t)
        sc = jnp.einsum('bgtd,bgsd->bgts', qj, kj) * w_intra
        num = (w_inter[..., None] * jnp.einsum('bgtd,bgde->bgte', qj, c_st)
               + jnp.einsum('bgts,bgse->bgte', sc, vj))
        den = w_inter * jnp.einsum('bgtd,bgd->bgt', qj, n_st) + jnp.sum(sc, axis=-1)
        h = num / jnp.maximum(jnp.abs(den), jnp.exp(-m_t))[..., None]
        b_last = bcum[..., -1]
        g_s = b_last[..., None] - bcum + lij
        m_next = jnp.maximum(b_last + m_st, jnp.max(g_s, axis=-1))
        a_prev = jnp.exp(b_last + m_st - m_next)
        kw = kj * jnp.exp(g_s - m_next[..., None])[..., None]
        c_new = a_prev[..., None, None] * c_st + jnp.einsum('bgsd,bgse->bgde', kw, vj)
        n_new = a_prev[..., None] * n_st + jnp.sum(kw, axis=2)
        return (c_new, n_new, m_next), h

    init = (jnp.zeros((b, g, dk, dv), f32), jnp.zeros((b, g, dk), f32), jnp.zeros((b, g), f32))
    _, hs = lax.scan(step, init, (qc, kc, vc, ic, fc))
    return jnp.moveaxis(hs, 0, 2).reshape(b, g, s, dv)


def mlstm_mixer(u, conv_w, conv_b, gate_b, out_norm):
    b, s, _ = u.shape
    h, dk, dv = ML_HEADS, ML_QK_DIM, ML_V_DIM
    qk, v, o, gates = jnp.split(u, [ML_QK_W, ML_QK_W + D_MODEL, ML_QK_W + 2 * D_MODEL], axis=-1)
    qk = jax.nn.silu(centered_dwconv(qk, conv_w, conv_b))
    q, k = jnp.split(qk, 2, axis=-1)
    q = q.reshape(b, s, h, dk).transpose(0, 2, 1, 3) * (dk ** -0.5)
    k = k.reshape(b, s, h, dk).transpose(0, 2, 1, 3)
    v = v.reshape(b, s, h, dv).transpose(0, 2, 1, 3)
    gates = (gates.astype(jnp.float32) + gate_b.astype(jnp.float32)).reshape(b, s, 4, h).transpose(0, 2, 3, 1)

    def flip(a):
        return jnp.flip(a, axis=2)

    log_i = jnp.concatenate([gates[:, 0], flip(gates[:, 2])], axis=1)
    log_f = jax.nn.log_sigmoid(jnp.concatenate([gates[:, 1], flip(gates[:, 3])], axis=1))
    hs = mlstm_chunkwise(jnp.concatenate([q, flip(q)], axis=1), jnp.concatenate([k, flip(k)], axis=1),
                         jnp.concatenate([v, flip(v)], axis=1), log_i, log_f)
    hsum = hs[:, :h] + flip(hs[:, h:])
    hsum = hsum.transpose(0, 2, 1, 3).astype(u.dtype)
    hsum = rms_norm(hsum, out_norm.reshape(h, dv)).reshape(b, s, D_MODEL)
    return hsum * jax.nn.sigmoid(o)


def hier_moe(x, wg, bg, we, be, w1, w3, w2):
    f32 = jnp.float32
    b, s, d = x.shape
    t = b * s
    xt = x.reshape(t, d)
    g_logits = (xt @ wg).astype(f32) + bg.astype(f32)
    g_prob = jax.nn.softmax(g_logits, axis=-1)
    g_idx = jnp.argmax(g_logits, axis=-1)
    g_w = jnp.take_along_axis(g_prob, g_idx[:, None], axis=-1)[:, 0]
    e_logits = ((xt @ we).astype(f32) + be.astype(f32)).reshape(t, N_GROUPS, EXPERTS_PER_GROUP)
    e_logits = jnp.take_along_axis(e_logits, g_idx[:, None, None], axis=1)[:, 0]
    top_p, top_e = lax.top_k(jax.nn.softmax(e_logits, axis=-1), TOP_K)
    top_p = top_p / jnp.sum(top_p, axis=-1, keepdims=True)
    gate = (g_w[:, None] * top_p).reshape(-1)
    eid = (g_idx[:, None] * EXPERTS_PER_GROUP + top_e).reshape(-1).astype(jnp.int32)
    tok = jnp.arange(t * TOP_K, dtype=jnp.int32) // TOP_K
    order = jnp.argsort(eid)
    se = eid[order]
    counts = jnp.bincount(eid, length=N_EXPERTS)
    starts = jnp.cumsum(counts) - counts
    padded = (counts + MOE_BLOCK - 1) // MOE_BLOCK * MOE_BLOCK
    pad_end = jnp.cumsum(padded)
    pad_start = pad_end - padded
    dest = pad_start[se] + (jnp.arange(t * TOP_K) - starts[se])
    cap = t * TOP_K + N_EXPERTS * MOE_BLOCK
    nb = cap // MOE_BLOCK
    buf_tok = jnp.zeros((cap,), jnp.int32).at[dest].set(tok[order])
    buf_gate = jnp.zeros((cap,), f32).at[dest].set(gate[order])
    blk_e = jnp.minimum(jnp.searchsorted(pad_end, jnp.arange(nb) * MOE_BLOCK, side='right'), N_EXPERTS - 1)
    xb = xt[buf_tok].reshape(nb, MOE_BLOCK, d)

    def expert_block(args):
        xe, e = args
        hid = jax.nn.silu(xe @ w1[e]) * (xe @ w3[e])
        return hid @ w2[e]

    yb = lax.map(expert_block, (xb, blk_e)).reshape(cap, d)
    out = jax.ops.segment_sum(yb.astype(f32) * buf_gate[:, None], buf_tok, num_segments=t)
    return out.astype(x.dtype).reshape(b, s, d)


def setup_inputs(seed: int = 0) -> dict:
    key = jax.random.key(seed)
    keys = iter(jax.random.split(key, 48))
    f32 = jnp.float32

    def normal(shape, scale):
        return jax.random.normal(next(keys), shape, f32) * scale

    def gain(shape):
        return 1.0 + normal(shape, 0.02)

    ne, no, d = N_EVEN, N_ODD, D_MODEL
    h = ML_HEADS
    f_bias = jnp.linspace(3.0, 6.0, h, dtype=f32)
    gate_b = jnp.concatenate([normal((no, h), 0.1), f_bias + normal((no, h), 0.1),
                              normal((no, h), 0.1), f_bias + normal((no, h), 0.1)], axis=-1)
    return {
        'x': normal((BATCH, SEQ, d), 1.0),
        'mix_norm': gain((DEPTH, d)),
        'ffn_norm': gain((DEPTH, d)),
        'ev_w_in': normal((ne, d, IN_EVEN), d ** -0.5),
        'ev_q_norm': gain((ne, ATT_QK_DIM)),
        'ev_k_norm': gain((ne, ATT_QK_DIM)),
        'ev_lam_q1': normal((ne, ATT_QK_DIM), 0.1),
        'ev_lam_k1': normal((ne, ATT_QK_DIM), 0.1),
        'ev_lam_q2': normal((ne, ATT_QK_DIM), 0.1),
        'ev_lam_k2': normal((ne, ATT_QK_DIM), 0.1),
        'ev_subln': gain((ne, ATT_V_DIM)),
        'ev_hy_conv_w': normal((ne, HY_SHORT_CONV, 3 * D_HY), HY_SHORT_CONV ** -0.5),
        'ev_hy_conv_b': normal((ne, 3 * D_HY), 0.02),
        'ev_hy_f_w1': normal((ne, HY_EMB_DIM, HY_FILTER_FF), HY_EMB_DIM ** -0.5),
        'ev_hy_f_b1': normal((ne, HY_FILTER_FF), 0.02),
        'ev_hy_f_freq': 1.0 + normal((ne, HY_FILTER_FF), 0.1),
        'ev_hy_f_w2': normal((ne, HY_FILTER_FF, HY_FILTER_FF), HY_FILTER_FF ** -0.5),
        'ev_hy_f_b2': normal((ne, HY_FILTER_FF), 0.02),
        'ev_hy_f_w3': normal((ne, HY_FILTER_FF, 2 * D_HY), HY_FILTER_FF ** -0.5),
        'ev_hy_skip': normal((ne, D_HY), 0.5),
        'ev_w_out': normal((ne, d, d), d ** -0.5),
        'od_w_in': normal((no, d, IN_ODD), d ** -0.5),
        'od_conv_w': normal((no, SHORT_CONV, ML_QK_W), SHORT_CONV ** -0.5),
        'od_conv_b': normal((no, ML_QK_W), 0.02),
        'od_gate_b': gate_b,
        'od_out_norm': gain((no, d)),
        'od_w_out': normal((no, d, d), d ** -0.5),
        'moe_wg': normal((DEPTH, d, N_GROUPS), d ** -0.5),
        'moe_bg': normal((DEPTH, N_GROUPS), 0.01),
        'moe_we': normal((DEPTH, d, N_EXPERTS), d ** -0.5),
        'moe_be': normal((DEPTH, N_EXPERTS), 0.01),
        'moe_w1': normal((DEPTH, N_EXPERTS, d, D_EXPERT), d ** -0.5),
        'moe_w3': normal((DEPTH, N_EXPERTS, d, D_EXPERT), d ** -0.5),
        'moe_w2': normal((DEPTH, N_EXPERTS, D_EXPERT, d), D_EXPERT ** -0.5),
    }


def reference(x, mix_norm, ffn_norm, ev_w_in, ev_q_norm, ev_k_norm, ev_lam_q1, ev_lam_k1, ev_lam_q2,
              ev_lam_k2, ev_subln, ev_hy_conv_w, ev_hy_conv_b, ev_hy_f_w1, ev_hy_f_b1, ev_hy_f_freq,
              ev_hy_f_w2, ev_hy_f_b2, ev_hy_f_w3, ev_hy_skip, ev_w_out, od_w_in, od_conv_w, od_conv_b,
              od_gate_b, od_out_norm, od_w_out, moe_wg, moe_bg, moe_we, moe_be, moe_w1, moe_w3, moe_w2):
    cos, sin = rope_tables(x.shape[1])
    for layer in range(DEPTH):
        j = layer // 2
        hn = rms_norm(x, mix_norm[layer])
        if layer % 2 == 0:
            u = hn @ ev_w_in[j]
            lambda_init = 0.8 - 0.6 * math.exp(-0.3 * layer)
            y_att = diff_attention(u[..., :3 * D_ATT], ev_q_norm[j], ev_k_norm[j], ev_lam_q1[j],
                                   ev_lam_k1[j], ev_lam_q2[j], ev_lam_k2[j], ev_subln[j],
                                   lambda_init, cos, sin)
            y_hy = hyena_mixer(u[..., 3 * D_ATT:], ev_hy_conv_w[j], ev_hy_conv_b[j], ev_hy_f_w1[j],
                               ev_hy_f_b1[j], ev_hy_f_freq[j], ev_hy_f_w2[j], ev_hy_f_b2[j],
                               ev_hy_f_w3[j], ev_hy_skip[j])
            y = jnp.concatenate([y_att, y_hy], axis=-1) @ ev_w_out[j]
        else:
            u = hn @ od_w_in[j]
            y = mlstm_mixer(u, od_conv_w[j], od_conv_b[j], od_gate_b[j], od_out_norm[j]) @ od_w_out[j]
        x = x + y
        x = x + hier_moe(rms_norm(x, ffn_norm[layer]), moe_wg[layer], moe_bg[layer], moe_we[layer],
                         moe_be[layer], moe_w1[layer], moe_w3[layer], moe_w2[layer])
    return x
```

```python
import functools
import math

import jax
import jax.numpy as jnp
from jax import lax
from jax.experimental import pallas as pl
from jax.experimental.pallas import tpu as pltpu

F32 = jnp.float32
BF16 = jnp.bfloat16
I32 = jnp.int32

EPS = 1e-6
ROPE_THETA = 500000.0
ATT_HEADS = 4
ML_HEADS = 4
ML_CHUNK = 128
N_GROUPS = 4
EXPERTS_PER_GROUP = 8
N_EXPERTS = N_GROUPS * EXPERTS_PER_GROUP
HY_EMB_BANDS = 16
HY_MIN_DECAY = math.log(1e-2) / 1.5
HY_MAX_DECAY = math.log(1e-2) / 0.3

V7X_VMEM_BYTES = 64 * 1024 * 1024
VMEM_LIMIT = V7X_VMEM_BYTES - 8 * 1024 * 1024
NEG_BIG = -1e30


def _params(*sem, **kw):
    return pltpu.CompilerParams(dimension_semantics=sem, vmem_limit_bytes=VMEM_LIMIT, **kw)


def _sigmoid(x):
    return 1.0 / (1.0 + jnp.exp(-x))


def _norm_matmul_body(x_ref, g_ref, *refs, n_out, col_chunk):
    w_refs, o_refs = refs[:n_out], refs[n_out:]
    x = x_ref[...]
    ms = jnp.mean(x * x, axis=-1, keepdims=True)
    hn = (x * lax.rsqrt(ms + EPS) * g_ref[...]).astype(BF16)
    for w_ref, o_ref in zip(w_refs, o_refs):
        n = w_ref.shape[1]
        for c in range(0, n, col_chunk):
            ce = min(n, c + col_chunk)
            o_ref[:, c:ce] = jnp.dot(hn, w_ref[:, c:ce], preferred_element_type=F32).astype(o_ref.dtype)


def norm_matmul(x2d, g, ws, out_dtypes, tm=512):
    t, d = x2d.shape
    in_specs = [pl.BlockSpec((tm, d), lambda i: (i, 0)), pl.BlockSpec((1, d), lambda i: (0, 0))]
    in_specs += [pl.BlockSpec(w.shape, lambda i: (0, 0)) for w in ws]
    out_specs = [pl.BlockSpec((tm, w.shape[1]), lambda i: (i, 0)) for w in ws]
    out_shape = [jax.ShapeDtypeStruct((t, w.shape[1]), dt) for w, dt in zip(ws, out_dtypes)]
    return pl.pallas_call(
        functools.partial(_norm_matmul_body, n_out=len(ws), col_chunk=1024),
        grid=(t // tm,), in_specs=in_specs, out_specs=out_specs, out_shape=out_shape,
        compiler_params=_params("parallel"), name="norm_matmul",
    )(x2d, g.reshape(1, d), *ws)


def _matmul_res_body(res_ref, *refs, n_in):
    a_refs, w_refs, o_ref = refs[:n_in], refs[n_in:2 * n_in], refs[2 * n_in]
    acc = res_ref[...]
    for a_ref, w_ref in zip(a_refs, w_refs):
        acc = acc + jnp.dot(a_ref[...], w_ref[...], preferred_element_type=F32)
    o_ref[...] = acc


def matmul_residual(res, a_list, w_list, tm=512):
    t, d = res.shape
    in_specs = [pl.BlockSpec((tm, d), lambda i: (i, 0))]
    in_specs += [pl.BlockSpec((tm, a.shape[1]), lambda i: (i, 0)) for a in a_list]
    in_specs += [pl.BlockSpec(w.shape, lambda i: (0, 0)) for w in w_list]
    return pl.pallas_call(
        functools.partial(_matmul_res_body, n_in=len(a_list)),
        grid=(t // tm,), in_specs=in_specs, out_specs=pl.BlockSpec((tm, d), lambda i: (i, 0)),
        out_shape=jax.ShapeDtypeStruct((t, d), F32),
        compiler_params=_params("parallel"), name="matmul_residual",
    )(res, *a_list, *w_list)


def _attn_prep_body(u_ref, g_ref, c_ref, s1_ref, s2_ref, o_ref, *, dk):
    x = u_ref[...].astype(F32)
    lane = lax.broadcasted_iota(I32, x.shape, 1)
    lo = lane < dk
    x2 = x * x
    s_lo = jnp.sum(jnp.where(lo, x2, 0.0), axis=-1, keepdims=True)
    s_hi = jnp.sum(jnp.where(lo, 0.0, x2), axis=-1, keepdims=True)
    ms = jnp.where(lo, s_lo, s_hi) * (1.0 / dk)
    y = x * lax.rsqrt(ms + EPS) * g_ref[...]
    out = y * c_ref[...] + pltpu.roll(y, 120, 1) * s1_ref[...] + pltpu.roll(y, 8, 1) * s2_ref[...]
    o_ref[...] = out.astype(o_ref.dtype)


def _rope_lane_tables(seq, dk, rope_dim, q_scale):
    half = rope_dim // 2
    inv_freq = 1.0 / (ROPE_THETA ** (jnp.arange(0, rope_dim, 2, dtype=F32) / rope_dim))
    ang = jnp.arange(seq, dtype=F32)[:, None] * inv_freq[None, :]
    cos, sin = jnp.cos(ang), jnp.sin(ang)
    d = jnp.arange(2 * dk) % dk
    fi = d % half
    c_tab = jnp.where(d[None, :] < rope_dim, cos[:, fi], 1.0)
    s1_tab = jnp.where(d[None, :] < half, -sin[:, fi], 0.0)
    s2_tab = jnp.where((d[None, :] >= half) & (d[None, :] < rope_dim), sin[:, fi], 0.0)
    tabs = jnp.stack([c_tab, s1_tab, s2_tab])
    scale = jnp.array([q_scale, 1.0], F32)[None, :, None, None]
    return (tabs[:, None] * scale).astype(F32)


def attn_prep(u, q_norm, k_norm):
    b, s, _ = u.shape
    dk = q_norm.shape[0]
    nblk = 2 * ATT_HEADS
    assert 2 * dk == 128 and dk // 4 == 16, "rope roll shifts assume 64-wide components, 16 rotary dims"
    tabs = _rope_lane_tables(s, dk, dk // 4, dk ** -0.5)
    gains = jnp.stack([jnp.tile(q_norm, 2), jnp.tile(k_norm, 2)]).reshape(2, 1, 128).astype(F32)
    tab_spec = pl.BlockSpec((None, s, 128), lambda c, bi: (c // ATT_HEADS, 0, 0))
    return pl.pallas_call(
        functools.partial(_attn_prep_body, dk=dk),
        grid=(nblk, b),
        in_specs=[pl.BlockSpec((None, s, 128), lambda c, bi: (bi, 0, c)),
                  pl.BlockSpec((None, 1, 128), lambda c, bi: (c // ATT_HEADS, 0, 0)),
                  tab_spec, tab_spec, tab_spec],
        out_specs=pl.BlockSpec((None, s, 128), lambda c, bi: (bi, 0, c)),
        out_shape=jax.ShapeDtypeStruct((b, s, nblk * 128), BF16),
        compiler_params=_params("parallel", "parallel"), name="attn_prep",
    )(u, gains, tabs[0], tabs[1], tabs[2])


def _attn_body(lam_ref, q_ref, k_ref, v_ref, g_ref, o_ref, *, tq, dk, post_scale):
    q = q_ref[...]
    lane = lax.broadcasted_iota(I32, q.shape, 1)
    zero = jnp.zeros_like(q)
    qq = jnp.concatenate([jnp.where(lane < dk, q, zero), jnp.where(lane < dk, zero, q)], axis=0)
    s = lax.dot_general(qq, k_ref[...], (((1,), (1,)), ((), ())), preferred_element_type=F32)
    m = jnp.max(s, axis=-1, keepdims=True)
    p = jnp.exp(s - m)
    r = 1.0 / jnp.sum(p, axis=-1, keepdims=True)
    lam = lam_ref[0, 0]
    w = p[:tq] * r[:tq] - p[tq:] * (lam * r[tq:])
    o = jnp.dot(w.astype(BF16), v_ref[...], preferred_element_type=F32)
    ms = jnp.mean(o * o, axis=-1, keepdims=True)
    o_ref[...] = (o * lax.rsqrt(ms + EPS) * g_ref[...] * post_scale).astype(o_ref.dtype)


def diff_attention(qkp, u, v_blk0, lam, subln, lambda_init, tq=256):
    b, s, _ = qkp.shape
    h = ATT_HEADS
    return pl.pallas_call(
        functools.partial(_attn_body, tq=tq, dk=64, post_scale=1.0 - lambda_init),
        grid=(b, h, s // tq),
        in_specs=[pl.BlockSpec(memory_space=pltpu.SMEM),
                  pl.BlockSpec((None, tq, 128), lambda bi, hi, i: (bi, i, hi)),
                  pl.BlockSpec((None, s, 128), lambda bi, hi, i: (bi, 0, h + hi)),
                  pl.BlockSpec((None, s, 128), lambda bi, hi, i: (bi, 0, v_blk0 + hi)),
                  pl.BlockSpec((1, 128), lambda bi, hi, i: (0, 0))],
        out_specs=pl.BlockSpec((None, tq, 128), lambda bi, hi, i: (bi, i, hi)),
        out_shape=jax.ShapeDtypeStruct((b, s, h * 128), BF16),
        compiler_params=_params("parallel", "parallel", "parallel"), name="diff_attention",
    )(lam.reshape(1, 1).astype(F32), qkp, qkp, u, subln.reshape(1, 128).astype(F32))


def _conv3(u_ref, w_ref, b_ref):
    x = u_ref[...].astype(F32)
    s = x.shape[0]
    row = lax.broadcasted_iota(I32, x.shape, 0)
    x_prev = jnp.where(row == 0, 0.0, pltpu.roll(x, 1, 0))
    x_next = jnp.where(row == s - 1, 0.0, pltpu.roll(x, s - 1, 0))
    w = w_ref[...]
    return b_ref[...] + x_prev * w[0:1] + x * w[1:2] + x_next * w[2:3]


def _hy_prep_body(x1_ref, x2_ref, v_ref, w1_ref, w2_ref, wv_ref, b1_ref, b2_ref, bv_ref, z_ref, x1c_ref):
    x1c_ref[...] = _conv3(x1_ref, w1_ref, b1_ref).astype(x1c_ref.dtype)
    z = _conv3(v_ref, wv_ref, bv_ref) * _conv3(x2_ref, w2_ref, b2_ref)
    z_ref[...] = z.astype(z_ref.dtype)


def hy_prep(u, col0, conv_w, conv_b, tc=256):
    b, s, _ = u.shape
    d_hy = conv_w.shape[1] // 3
    nct = d_hy // tc
    blk0 = col0 // tc

    def uspec(part):
        return pl.BlockSpec((None, s, tc), lambda bi, c: (bi, 0, blk0 + part * nct + c))

    def wspec(part, rows):
        return pl.BlockSpec((rows, tc), lambda bi, c: (0, part * nct + c))

    ospec = pl.BlockSpec((None, s, tc), lambda bi, c: (bi, 0, c))
    return pl.pallas_call(
        _hy_prep_body, grid=(b, nct),
        in_specs=[uspec(0), uspec(1), uspec(2), wspec(0, 3), wspec(1, 3), wspec(2, 3),
                  wspec(0, 1), wspec(1, 1), wspec(2, 1)],
        out_specs=[ospec, ospec],
        out_shape=[jax.ShapeDtypeStruct((b, s, d_hy), BF16), jax.ShapeDtypeStruct((b, s, d_hy), BF16)],
        compiler_params=_params("parallel", "parallel"), name="hy_prep",
    )(u, u, u, conv_w, conv_w, conv_w, conv_b.reshape(1, -1), conv_b.reshape(1, -1), conv_b.reshape(1, -1))


def hyena_filter(length, w1, b1, freq, w2, b2, w3):
    d_hy = w3.shape[1] // 2
    t = jnp.linspace(0.0, 1.0, length, dtype=F32)[:, None]
    bands = jnp.linspace(1e-4, HY_EMB_BANDS - 1, HY_EMB_BANDS, dtype=F32)[None, :]
    ang = (2.0 * math.pi / length) * jnp.arange(length, dtype=F32)[:, None] * bands
    z = jnp.concatenate([t, jnp.cos(ang), -jnp.sin(ang)], axis=-1)
    hp = lax.Precision.HIGHEST
    hdn = jnp.sin(freq * (jnp.dot(z, w1, precision=hp) + b1))
    hdn = jnp.sin(freq * (jnp.dot(hdn, w2, precision=hp) + b2))
    filt = jnp.dot(hdn, w3, precision=hp)
    deltas = jnp.abs(jnp.linspace(HY_MIN_DECAY, HY_MAX_DECAY, d_hy, dtype=F32))
    decay = jnp.exp(-t * deltas[None, :])
    h_fwd = filt[:, :d_hy] * decay
    h_bwd = filt[:, d_hy:] * decay
    kern = jnp.concatenate([h_fwd.at[0].add(h_bwd[0]), jnp.zeros((1, d_hy), F32), h_bwd[:0:-1]], axis=0)
    return kern / (jnp.sum(jnp.abs(kern), axis=0, keepdims=True) + EPS)


def dft_tables(length, tk):
    n = 2 * length
    k = jnp.arange(length, dtype=I32)
    pos = jnp.arange(length, dtype=I32)
    r = 64
    unit = 2.0 * math.pi / n
    ang_a = ((k[:, None] * (jnp.arange(length // r, dtype=I32) * r)[None, :]) % n).astype(F32) * unit
    ang_b = ((k[:, None] * jnp.arange(r, dtype=I32)[None, :]) % n).astype(F32) * unit
    ca, sa = jnp.cos(ang_a)[:, :, None], jnp.sin(ang_a)[:, :, None]
    cb, sb = jnp.cos(ang_b)[:, None, :], jnp.sin(ang_b)[:, None, :]
    cos = (ca * cb - sa * sb).reshape(length, length)
    sin = (sa * cb + ca * sb).reshape(length, length)
    alt = jnp.where(pos % 2 == 0, 1.0, -1.0).astype(F32)
    fwd_re = cos
    fwd_im = jnp.where(k[:, None] == 0, alt[None, :], -sin)
    ck = jnp.where(k == 0, 1.0 / n, 2.0 / n).astype(F32)
    inv_re = cos * ck[:, None]
    inv_im = jnp.where(k[:, None] == 0, alt[None, :] / n, -sin * (2.0 / n))
    nblk = length // tk
    fwd = jnp.concatenate([fwd_re.reshape(nblk, tk, length), fwd_im.reshape(nblk, tk, length)], axis=1)
    inv = jnp.concatenate([inv_re.reshape(nblk, tk, length), inv_im.reshape(nblk, tk, length)], axis=1)
    return fwd.astype(BF16), jnp.swapaxes(inv, 1, 2).astype(BF16)


def _spectrum_body(fwd_ref, ab_ref, kf_ref, *, d_hy):
    p = jnp.dot(fwd_ref[...], ab_ref[...], preferred_element_type=F32)
    row = lax.broadcasted_iota(I32, (p.shape[0], d_hy), 0)
    sign = jnp.where((row & 1) == 0, 1.0, -1.0)
    kf_ref[...] = p[:, :d_hy] + sign * p[:, d_hy:]


def filter_spectrum(kern, fwd):
    nblk, tk2, length = fwd.shape
    d_hy = kern.shape[1]
    ab = jnp.concatenate([kern[:length], kern[length:]], axis=1).astype(BF16)
    return pl.pallas_call(
        functools.partial(_spectrum_body, d_hy=d_hy), grid=(nblk,),
        in_specs=[pl.BlockSpec((None, tk2, length), lambda j: (j, 0, 0)),
                  pl.BlockSpec((length, 2 * d_hy), lambda j: (0, 0))],
        out_specs=pl.BlockSpec((None, tk2, d_hy), lambda j: (j, 0, 0)),
        out_shape=jax.ShapeDtypeStruct((nblk, tk2, d_hy), F32),
        compiler_params=_params("parallel"), name="filter_spectrum",
    )(fwd, ab)


def _hy_conv_body(z_ref, x1c_ref, fwd_ref, inv_ref, kf_ref, skip_ref, o_ref, acc_ref, *, tk):
    j = pl.program_id(1)

    @pl.when(j == 0)
    def _():
        acc_ref[...] = jnp.zeros_like(acc_ref)

    zf = jnp.dot(fwd_ref[...], z_ref[...], preferred_element_type=F32)
    zr, zi = zf[:tk], zf[tk:]
    kf = kf_ref[...]
    kr, ki = kf[:tk], kf[tk:]
    row = lax.broadcasted_iota(I32, zr.shape, 0)
    packed = (row == 0) & (j == 0)
    yr = zr * kr - jnp.where(packed, 0.0, zi * ki)
    yi = jnp.where(packed, zi * ki, zr * ki + zi * kr)
    y = jnp.concatenate([yr, yi], axis=0).astype(BF16)
    acc_ref[...] += jnp.dot(inv_ref[...], y, preferred_element_type=F32)

    @pl.when(j == pl.num_programs(1) - 1)
    def _():
        z = z_ref[...].astype(F32)
        o_ref[...] = ((acc_ref[...] + z * skip_ref[...]) * x1c_ref[...].astype(F32)).astype(o_ref.dtype)


def hy_conv(z, x1c, fwd, inv, kf, skip):
    b, s, c = z.shape
    nblk, tk2, _ = fwd.shape
    return pl.pallas_call(
        functools.partial(_hy_conv_body, tk=tk2 // 2), grid=(b, nblk),
        in_specs=[pl.BlockSpec((None, s, c), lambda bi, j: (bi, 0, 0)),
                  pl.BlockSpec((None, s, c), lambda bi, j: (bi, 0, 0)),
                  pl.BlockSpec((None, tk2, s), lambda bi, j: (j, 0, 0)),
                  pl.BlockSpec((None, s, tk2), lambda bi, j: (j, 0, 0)),
                  pl.BlockSpec((None, tk2, c), lambda bi, j: (j, 0, 0)),
                  pl.BlockSpec((1, c), lambda bi, j: (0, 0))],
        out_specs=pl.BlockSpec((None, s, c), lambda bi, j: (bi, 0, 0)),
        out_shape=jax.ShapeDtypeStruct((b, s, c), BF16),
        scratch_shapes=[pltpu.VMEM((s, c), F32)],
        compiler_params=_params("parallel", "arbitrary"), name="hy_conv",
    )(z, x1c, fwd, inv, kf, skip.reshape(1, c).astype(F32))


def _ml_prep_body(u_ref, w_ref, b_ref, sc_ref, o_ref):
    y = _conv3(u_ref, w_ref, b_ref)
    o_ref[...] = (y * _sigmoid(y) * sc_ref[...]).astype(o_ref.dtype)


def ml_prep(u, conv_w, conv_b, col_scale, tc=256):
    b, s, _ = u.shape
    w = conv_w.shape[1]
    return pl.pallas_call(
        _ml_prep_body, grid=(b, w // tc),
        in_specs=[pl.BlockSpec((None, s, tc), lambda bi, c: (bi, 0, c)),
                  pl.BlockSpec((3, tc), lambda bi, c: (0, c)),
                  pl.BlockSpec((1, tc), lambda bi, c: (0, c)),
                  pl.BlockSpec((1, tc), lambda bi, c: (0, c))],
        out_specs=pl.BlockSpec((None, s, tc), lambda bi, c: (bi, 0, c)),
        out_shape=jax.ShapeDtypeStruct((b, s, w), BF16),
        compiler_params=_params("parallel", "parallel"), name="ml_prep",
    )(u, conv_w, conv_b.reshape(1, w), col_scale.reshape(1, w))


def _log_sigmoid(x):
    return jnp.minimum(x, 0.0) - jnp.log(1.0 + jnp.exp(-jnp.abs(x)))


def _dot_split(a, b, a_is_f32):
    x = a if a_is_f32 else b
    hi = x.astype(BF16)
    lo = (x - hi.astype(F32)).astype(BF16)
    if a_is_f32:
        return (jnp.dot(hi, b, preferred_element_type=F32) + jnp.dot(lo, b, preferred_element_type=F32))
    return (jnp.dot(a, hi, preferred_element_type=F32) + jnp.dot(a, lo, preferred_element_type=F32))


def _mlstm_chain(q, k, v, bc, br, li_r, li_c, total, mask, c_ref, n_ref, m_ref, idx):
    c_st = c_ref[idx]
    n_st = n_ref[idx:idx + 1, :]
    m_st = m_ref[idx:idx + 1, 0:1]
    dmat = jnp.where(mask, bc - br + li_r, NEG_BIG)
    inter = bc + m_st
    m_t = jnp.maximum(inter, jnp.max(dmat, axis=-1, keepdims=True))
    w_intra = jnp.exp(dmat - m_t)
    w_inter = jnp.exp(inter - m_t)
    sc = lax.dot_general(q, k, (((1,), (1,)), ((), ())), preferred_element_type=F32) * w_intra
    num = (w_inter * jnp.dot(q, c_st.astype(BF16), preferred_element_type=F32)
           + jnp.dot(sc.astype(BF16), v, preferred_element_type=F32))
    den = (w_inter * jnp.sum(q.astype(F32) * n_st, axis=-1, keepdims=True)
           + jnp.sum(sc, axis=-1, keepdims=True))
    h = num / jnp.maximum(jnp.abs(den), jnp.exp(-m_t))
    g_s = total - bc + li_c
    m_next = jnp.maximum(total + m_st, jnp.max(g_s, axis=0, keepdims=True))
    a_prev = jnp.exp(total + m_st - m_next)
    kw = k.astype(F32) * jnp.exp(g_s - m_next)
    c_ref[idx] = a_prev * c_st + lax.dot_general(kw.astype(BF16), v, (((0,), (0,)), ((), ())),
                                                 preferred_element_type=F32)
    n_ref[idx:idx + 1, :] = a_prev * n_st + jnp.sum(kw, axis=0, keepdims=True)
    m_ref[idx:idx + 1, :] = jnp.broadcast_to(m_next, (1, m_ref.shape[1]))
    return h


def _mlstm_body(qkf_ref, vf_ref, gcf_ref, grf_ref, qkb_ref, vb_ref, gcb_ref, grb_ref,
                hf_ref, hb_ref, c_ref, n_ref, m_ref, *, heads, dk, dv):
    @pl.when(pl.program_id(1) == 0)
    def _():
        c_ref[...] = jnp.zeros_like(c_ref)
        n_ref[...] = jnp.zeros_like(n_ref)
        m_ref[...] = jnp.zeros_like(m_ref)

    lc = qkf_ref.shape[0]
    t_i = lax.broadcasted_iota(I32, (lc, lc), 0)
    s_i = lax.broadcasted_iota(I32, (lc, lc), 1)
    lower = s_i <= t_i
    upper = s_i >= t_i
    ltri = jnp.where(lower, 1.0, 0.0).astype(BF16)
    utri = jnp.where(upper, 1.0, 0.0).astype(BF16)

    for direction, (qk_ref, v_ref, gc_ref, gr_ref, h_ref) in enumerate(
            ((qkf_ref, vf_ref, gcf_ref, grf_ref, hf_ref), (qkb_ref, vb_ref, gcb_ref, grb_ref, hb_ref))):
        fwd = direction == 0
        gc = gc_ref[...]
        gr = gr_ref[...]
        lfc, lfr = _log_sigmoid(gc), _log_sigmoid(gr)
        cum_c = _dot_split(ltri if fwd else utri, lfc, a_is_f32=False)
        cum_r = _dot_split(lfr, utri if fwd else ltri, a_is_f32=True)
        for hd in range(heads):
            gi = (0 if fwd else 2) * heads + hd
            gf = (1 if fwd else 3) * heads + hd
            bc, br = cum_c[:, gf:gf + 1], cum_r[gf:gf + 1, :]
            total = br[:, lc - 1:lc] if fwd else br[:, 0:1]
            q = qk_ref[:, hd * dk:(hd + 1) * dk]
            k = qk_ref[:, (heads + hd) * dk:(heads + hd + 1) * dk]
            v = v_ref[:, hd * dv:(hd + 1) * dv]
            h = _mlstm_chain(q, k, v, bc, br, gr[gi:gi + 1, :], gc[:, gi:gi + 1], total,
                             lower if fwd else upper, c_ref, n_ref, m_ref, direction * heads + hd)
            h_ref[:, hd * dv:(hd + 1) * dv] = h.astype(h_ref.dtype)


def mlstm_scan(qk, u, v_blk, gcol, grow):
    b, s, w = qk.shape
    heads = ML_HEADS
    dk = w // (2 * heads)
    dv = 2 * dk
    lc = ML_CHUNK
    nc = s // lc
    ng = gcol.shape[-1]

    def fw(bi, j):
        return j

    def bw(bi, j):
        return nc - 1 - j

    def specs(pos):
        return [pl.BlockSpec((None, lc, w), lambda bi, j: (bi, pos(bi, j), 0)),
                pl.BlockSpec((None, lc, heads * dv), lambda bi, j: (bi, pos(bi, j), v_blk)),
                pl.BlockSpec((None, lc, ng), lambda bi, j: (bi, pos(bi, j), 0)),
                pl.BlockSpec((None, ng, lc), lambda bi, j: (bi, 0, pos(bi, j)))]

    hshape = jax.ShapeDtypeStruct((b, s, heads * dv), BF16)
    return pl.pallas_call(
        functools.partial(_mlstm_body, heads=heads, dk=dk, dv=dv), grid=(b, nc),
        in_specs=specs(fw) + specs(bw),
        out_specs=[pl.BlockSpec((None, lc, heads * dv), lambda bi, j: (bi, j, 0)),
                   pl.BlockSpec((None, lc, heads * dv), lambda bi, j: (bi, nc - 1 - j, 0))],
        out_shape=[hshape, hshape],
        scratch_shapes=[pltpu.VMEM((2 * heads, dk, dv), F32), pltpu.VMEM((2 * heads, dk), F32),
                        pltpu.VMEM((2 * heads, 128), F32)],
        compiler_params=_params("parallel", "arbitrary"), name="mlstm_scan",
    )(qk, u, gcol, grow, qk, u, gcol, grow)


def _mlstm_out_body(res_ref, hf_ref, hb_ref, o_ref, g_ref, w_ref, out_ref, *, heads):
    hs = hf_ref[...].astype(F32) + hb_ref[...].astype(F32)
    dv = hs.shape[1] // heads
    g = g_ref[...]
    parts = []
    for hd in range(heads):
        seg = hs[:, hd * dv:(hd + 1) * dv]
        ms = jnp.mean(seg * seg, axis=-1, keepdims=True)
        parts.append(seg * lax.rsqrt(ms + EPS) * g[:, hd * dv:(hd + 1) * dv])
    a = jnp.concatenate(parts, axis=-1) * _sigmoid(o_ref[...].astype(F32))
    out_ref[...] = res_ref[...] + jnp.dot(a.astype(BF16), w_ref[...], preferred_element_type=F32)


def mlstm_out(res, hf, hb, u2d, o_blk, gain, w_out, tm=512):
    t, d = res.shape
    row = lambda i: (i, 0)
    return pl.pallas_call(
        functools.partial(_mlstm_out_body, heads=ML_HEADS), grid=(t // tm,),
        in_specs=[pl.BlockSpec((tm, d), row), pl.BlockSpec((tm, d), row), pl.BlockSpec((tm, d), row),
                  pl.BlockSpec((tm, d), lambda i: (i, o_blk)), pl.BlockSpec((1, d), lambda i: (0, 0)),
                  pl.BlockSpec((d, d), lambda i: (0, 0))],
        out_specs=pl.BlockSpec((tm, d), row), out_shape=jax.ShapeDtypeStruct((t, d), F32),
        compiler_params=_params("parallel"), name="mlstm_out",
    )(res, hf, hb, u2d, gain.reshape(1, d), w_out)


ROUTE_ROWS = 128
EXPERT_ROW0 = 8


def _router_body(x_ref, g_ref, wt_ref, b_ref, o_ref):
    x = x_ref[...]
    ms = jnp.mean(x * x, axis=-1, keepdims=True)
    xn = x * lax.rsqrt(ms + EPS) * g_ref[...]
    logit = lax.dot_general(wt_ref[...], xn, (((1,), (1,)), ((), ())), preferred_element_type=F32,
                            precision=lax.Precision.HIGHEST) + b_ref[...]
    rows = [logit[r:r + 1, :] for r in range(EXPERT_ROW0 + N_EXPERTS)]
    g_best, g_idx = rows[0], jnp.zeros_like(rows[0])
    for gi in range(1, N_GROUPS):
        better = rows[gi] > g_best
        g_best = jnp.where(better, rows[gi], g_best)
        g_idx = jnp.where(better, float(gi), g_idx)
    g_den = sum(jnp.exp(rows[gi] - g_best) for gi in range(N_GROUPS))
    g_w = 1.0 / g_den
    sel = []
    for e in range(EXPERTS_PER_GROUP):
        v = rows[EXPERT_ROW0 + e]
        for gi in range(1, N_GROUPS):
            v = jnp.where(g_idx == float(gi), rows[EXPERT_ROW0 + gi * EXPERTS_PER_GROUP + e], v)
        sel.append(v)
    v1, i1 = sel[0], jnp.zeros_like(sel[0])
    for e in range(1, EXPERTS_PER_GROUP):
        better = sel[e] > v1
        v1 = jnp.where(better, sel[e], v1)
        i1 = jnp.where(better, float(e), i1)
    v2, i2 = jnp.full_like(v1, -jnp.inf), jnp.zeros_like(v1)
    for e in range(EXPERTS_PER_GROUP):
        better = (sel[e] > v2) & (i1 != float(e))
        v2 = jnp.where(better, sel[e], v2)
        i2 = jnp.where(better, float(e), i2)
    e21 = jnp.exp(v2 - v1)
    gate1 = g_w / (1.0 + e21)
    gate2 = gate1 * e21
    base = g_idx * float(EXPERTS_PER_GROUP)
    zero = jnp.zeros_like(v1)
    o_ref[...] = jnp.concatenate([base + i1, base + i2, gate1, gate2, zero, zero, zero, zero], axis=0)


def moe_router(x2d, g, wg, bg, we, be, tm=512):
    t, d = x2d.shape
    wt = jnp.zeros((ROUTE_ROWS, d), F32).at[:N_GROUPS].set(wg.T).at[EXPERT_ROW0:EXPERT_ROW0 + N_EXPERTS].set(we.T)
    bias = jnp.zeros((ROUTE_ROWS, 1), F32).at[:N_GROUPS, 0].set(bg).at[EXPERT_ROW0:EXPERT_ROW0 + N_EXPERTS, 0].set(be)
    return pl.pallas_call(
        _router_body, grid=(t // tm,),
        in_specs=[pl.BlockSpec((tm, d), lambda i: (i, 0)), pl.BlockSpec((1, d), lambda i: (0, 0)),
                  pl.BlockSpec((ROUTE_ROWS, d), lambda i: (0, 0)), pl.BlockSpec((ROUTE_ROWS, 1), lambda i: (0, 0))],
        out_specs=pl.BlockSpec((8, tm), lambda i: (0, i)),
        out_shape=jax.ShapeDtypeStruct((8, t), F32),
        compiler_params=_params("parallel"), name="moe_router",
    )(x2d, g.reshape(1, d), wt, bias)


def _dispatch_body(dest_ref, x_ref, g_ref, xs_in_ref, xs_ref, buf_ref, sem_ref, *, td):
    del xs_in_ref
    i = pl.program_id(0)
    n = pl.num_programs(0)
    slot = i % 2

    def row_copy(step, sl, r, kk):
        return pltpu.make_async_copy(buf_ref.at[sl, pl.ds(r, 1), :],
                                     xs_ref.at[pl.ds(dest_ref[(step * td + r) * 2 + kk], 1), :],
                                     sem_ref.at[sl])

    def wait_step(step, sl):
        def body(r, c):
            row_copy(step, sl, r, 0).wait()
            row_copy(step, sl, r, 1).wait()
            return c
        lax.fori_loop(0, td, body, 0)

    @pl.when(i >= 2)
    def _():
        wait_step(i - 2, slot)

    x = x_ref[...]
    ms = jnp.mean(x * x, axis=-1, keepdims=True)
    buf_ref[slot] = x * lax.rsqrt(ms + EPS) * g_ref[...]

    def issue(r, c):
        row_copy(i, slot, r, 0).start()
        row_copy(i, slot, r, 1).start()
        return c
    lax.fori_loop(0, td, issue, 0)

    @pl.when(i == n - 1)
    def _():
        @pl.when(n >= 2)
        def _():
            wait_step(i - 1, 1 - slot)
        wait_step(i, slot)


def moe_dispatch(x2d, g, dest, cap, td=256):
    t, d = x2d.shape
    xs0 = jnp.zeros((cap, d), F32)
    grid_spec = pltpu.PrefetchScalarGridSpec(
        num_scalar_prefetch=1, grid=(t // td,),
        in_specs=[pl.BlockSpec((td, d), lambda i, dest: (i, 0)), pl.BlockSpec((1, d), lambda i, dest: (0, 0)),
                  pl.BlockSpec(memory_space=pl.ANY)],
        out_specs=pl.BlockSpec(memory_space=pl.ANY),
        scratch_shapes=[pltpu.VMEM((2, td, d), F32), pltpu.SemaphoreType.DMA((2,))])
    return pl.pallas_call(
        functools.partial(_dispatch_body, td=td), grid_spec=grid_spec,
        out_shape=jax.ShapeDtypeStruct((cap, d), F32), input_output_aliases={3: 0},
        compiler_params=_params("arbitrary", disable_bounds_checks=True),
        name="moe_dispatch",
    )(dest, x2d, g.reshape(1, d), xs0)


def _expert_body(blk_e_ref, xs_ref, w1_ref, w3_ref, w2_ref, ys_ref):
    del blk_e_ref
    x = xs_ref[...].astype(BF16)
    h1 = jnp.dot(x, w1_ref[...], preferred_element_type=F32)
    h3 = jnp.dot(x, w3_ref[...], preferred_element_type=F32)
    hid = (h1 * _sigmoid(h1) * h3).astype(BF16)
    ys_ref[...] = jnp.dot(hid, w2_ref[...], preferred_element_type=F32)


def moe_experts(xs, blk_e, w1, w3, w2, tm):
    cap, d = xs.shape
    de = w1.shape[2]
    grid_spec = pltpu.PrefetchScalarGridSpec(
        num_scalar_prefetch=1, grid=(cap // tm,),
        in_specs=[pl.BlockSpec((tm, d), lambda i, be: (i, 0)),
                  pl.BlockSpec((None, d, de), lambda i, be: (be[i], 0, 0)),
                  pl.BlockSpec((None, d, de), lambda i, be: (be[i], 0, 0)),
                  pl.BlockSpec((None, de, d), lambda i, be: (be[i], 0, 0))],
        out_specs=pl.BlockSpec((tm, d), lambda i, be: (i, 0)))
    return pl.pallas_call(
        _expert_body, grid_spec=grid_spec, out_shape=jax.ShapeDtypeStruct((cap, d), F32),
        compiler_params=_params("arbitrary"), name="moe_experts",
    )(blk_e, xs, w1, w3, w2)


def _combine_body(dest_ref, x_ref, gate_ref, ys_ref, o_ref, buf_ref, sem_ref, *, tc):
    i = pl.program_id(0)
    n = pl.num_programs(0)
    slot = i % 2

    def row_copy(step, sl, r, kk):
        return pltpu.make_async_copy(ys_ref.at[pl.ds(dest_ref[(step * tc + r) * 2 + kk], 1), :],
                                     buf_ref.at[sl, kk, pl.ds(r, 1), :], sem_ref.at[sl])

    def issue_step(step, sl):
        def body(r, c):
            row_copy(step, sl, r, 0).start()
            row_copy(step, sl, r, 1).start()
            return c
        lax.fori_loop(0, tc, body, 0)

    @pl.when(i == 0)
    def _():
        issue_step(0, 0)

    @pl.when(i + 1 < n)
    def _():
        issue_step(i + 1, 1 - slot)

    def wait_body(r, c):
        row_copy(i, slot, r, 0).wait()
        row_copy(i, slot, r, 1).wait()
        return c
    lax.fori_loop(0, tc, wait_body, 0)

    gate = gate_ref[...]
    o_ref[...] = x_ref[...] + gate[:, 0:1] * buf_ref[slot, 0] + gate[:, 1:2] * buf_ref[slot, 1]


def moe_combine(x2d, gates, ys, dest, tc=256):
    t, d = x2d.shape
    grid_spec = pltpu.PrefetchScalarGridSpec(
        num_scalar_prefetch=1, grid=(t // tc,),
        in_specs=[pl.BlockSpec((tc, d), lambda i, dest: (i, 0)), pl.BlockSpec((tc, 2), lambda i, dest: (i, 0)),
                  pl.BlockSpec(memory_space=pl.ANY)],
        out_specs=pl.BlockSpec((tc, d), lambda i, dest: (i, 0)),
        scratch_shapes=[pltpu.VMEM((2, 2, tc, d), F32), pltpu.SemaphoreType.DMA((2,))])
    return pl.pallas_call(
        functools.partial(_combine_body, tc=tc), grid_spec=grid_spec,
        out_shape=jax.ShapeDtypeStruct((t, d), F32),
        compiler_params=_params("arbitrary", disable_bounds_checks=True), name="moe_combine",
    )(dest, x2d, gates, ys)


def hier_moe_residual(x2d, g, wg, bg, we, be, w1, w3, w2, tm=256):
    t, d = x2d.shape
    route = moe_router(x2d, g, wg, bg, we, be)
    eid = route[0:2].T.astype(I32).reshape(-1)
    gates = route[2:4].T
    onehot = (eid[:, None] == jnp.arange(N_EXPERTS, dtype=I32)[None, :]).astype(I32)
    csum = jnp.cumsum(onehot, axis=0)
    rank = jnp.take_along_axis(csum, eid[:, None], axis=1)[:, 0] - 1
    counts = csum[-1]
    padded = (counts + tm - 1) // tm * tm
    pad_end = jnp.cumsum(padded)
    dest = ((pad_end - padded)[eid] + rank).astype(I32)
    cap = t * 2 + N_EXPERTS * tm
    blk_e = jnp.minimum(jnp.searchsorted(pad_end, jnp.arange(cap // tm, dtype=I32) * tm, side="right"),
                        N_EXPERTS - 1).astype(I32)
    xs = moe_dispatch(x2d, g, dest, cap)
    ys = moe_experts(xs, blk_e, w1, w3, w2, tm)
    return moe_combine(x2d, gates, ys, dest)


def _even_layer(x2d, b, s, layer, mix_g, w_in, q_norm, k_norm, lam_q1, lam_k1, lam_q2, lam_k2, subln,
                hy_conv_w, hy_conv_b, f_w1, f_b1, f_freq, f_w2, f_b2, f_w3, hy_skip, w_out):
    d = x2d.shape[1]
    d_att = d // 2
    (u2d,) = norm_matmul(x2d, mix_g, [w_in.astype(BF16)], [BF16])
    u = u2d.reshape(b, s, -1)
    lambda_init = 0.8 - 0.6 * math.exp(-0.3 * layer)
    lam = jnp.exp(jnp.sum(lam_q1 * lam_k1)) - jnp.exp(jnp.sum(lam_q2 * lam_k2)) + lambda_init
    qkp = attn_prep(u, q_norm, k_norm)
    y_att = diff_attention(qkp, u, 2 * ATT_HEADS, lam, subln, lambda_init)
    z, x1c = hy_prep(u, 3 * d_att, hy_conv_w, hy_conv_b)
    kern = hyena_filter(s, f_w1, f_b1, f_freq, f_w2, f_b2, f_w3)
    fwd, inv = dft_tables(s, 128)
    kf = filter_spectrum(kern, fwd)
    y_hy = hy_conv(z, x1c, fwd, inv, kf, hy_skip)
    w_out = w_out.astype(BF16)
    return matmul_residual(x2d, [y_att.reshape(b * s, -1), y_hy.reshape(b * s, -1)],
                           [w_out[:d_att], w_out[d_att:]])


def _odd_layer(x2d, b, s, mix_g, w_in, conv_w, conv_b, gate_b, out_norm, w_out):
    d = x2d.shape[1]
    qk_w = conv_w.shape[1]
    main_w = qk_w + 2 * d
    ng = 4 * ML_HEADS
    w_gate = jnp.zeros((d, 128), F32).at[:, :ng].set(w_in[:, main_w:]).astype(BF16)
    u2d, ug = norm_matmul(x2d, mix_g, [w_in[:, :main_w].astype(BF16), w_gate], [BF16, F32])
    u = u2d.reshape(b, s, main_w)
    gcol = (ug[:, :ng] + gate_b).reshape(b, s, ng)
    grow = jnp.swapaxes(gcol, 1, 2)
    dk = qk_w // (2 * ML_HEADS)
    col_scale = jnp.concatenate([jnp.full((qk_w // 2,), dk ** -0.5, F32), jnp.ones((qk_w // 2,), F32)])
    qk = ml_prep(u, conv_w, conv_b, col_scale)
    hf, hb = mlstm_scan(qk, u, qk_w // d, gcol, grow)
    return mlstm_out(x2d, hf.reshape(b * s, d), hb.reshape(b * s, d), u2d, (qk_w + d) // d, out_norm,
                     w_out.astype(BF16))


def kernel(x, mix_norm, ffn_norm, ev_w_in, ev_q_norm, ev_k_norm, ev_lam_q1, ev_lam_k1, ev_lam_q2, ev_lam_k2, ev_subln, ev_hy_conv_w, ev_hy_conv_b, ev_hy_f_w1, ev_hy_f_b1, ev_hy_f_freq, ev_hy_f_w2, ev_hy_f_b2, ev_hy_f_w3, ev_hy_skip, ev_w_out, od_w_in, od_conv_w, od_conv_b, od_gate_b, od_out_norm, od_w_out, moe_wg, moe_bg, moe_we, moe_be, moe_w1, moe_w3, moe_w2):
    b, s, d = x.shape
    depth = mix_norm.shape[0]
    x2d = x.reshape(b * s, d)
    for layer in range(depth):
        j = layer // 2
        if layer % 2 == 0:
            x2d = _even_layer(x2d, b, s, layer, mix_norm[layer], ev_w_in[j], ev_q_norm[j], ev_k_norm[j],
                              ev_lam_q1[j], ev_lam_k1[j], ev_lam_q2[j], ev_lam_k2[j], ev_subln[j],
                              ev_hy_conv_w[j], ev_hy_conv_b[j], ev_hy_f_w1[j], ev_hy_f_b1[j], ev_hy_f_freq[j],
                              ev_hy_f_w2[j], ev_hy_f_b2[j], ev_hy_f_w3[j], ev_hy_skip[j], ev_w_out[j])
        else:
            x2d = _odd_layer(x2d, b, s, mix_norm[layer], od_w_in[j], od_conv_w[j], od_conv_b[j], od_gate_b[j],
                             od_out_norm[j], od_w_out[j])
        x2d = hier_moe_residual(x2d, ffn_norm[layer], moe_wg[layer], moe_bg[layer], moe_we[layer],
                                moe_be[layer], moe_w1[layer].astype(BF16), moe_w3[layer].astype(BF16),
                                moe_w2[layer].astype(BF16))
    return x2d.reshape(b, s, d)
```

```python
import functools
import math

import jax
import jax.numpy as jnp
from jax import lax
from jax.experimental import pallas as pl
from jax.experimental.pallas import tpu as pltpu

F32 = jnp.float32
BF16 = jnp.bfloat16
I32 = jnp.int32

EPS = 1e-6
ROPE_THETA = 500000.0
ATT_HEADS = 4
ML_HEADS = 4
ML_CHUNK = 128
N_GROUPS = 4
EXPERTS_PER_GROUP = 8
N_EXPERTS = N_GROUPS * EXPERTS_PER_GROUP
HY_EMB_BANDS = 16
HY_MIN_DECAY = math.log(1e-2) / 1.5
HY_MAX_DECAY = math.log(1e-2) / 0.3

V7X_VMEM_BYTES = 64 * 1024 * 1024
VMEM_LIMIT = V7X_VMEM_BYTES - 8 * 1024 * 1024
NEG_BIG = -1e30


def _params(*sem, **kw):
    return pltpu.CompilerParams(dimension_semantics=sem, vmem_limit_bytes=VMEM_LIMIT, **kw)


def _sigmoid(x):
    return 1.0 / (1.0 + jnp.exp(-x))


def _norm_matmul_body(x_ref, g_ref, *refs, n_out, col_chunk):
    w_refs, o_refs = refs[:n_out], refs[n_out:]
    x = x_ref[...]
    ms = jnp.mean(x * x, axis=-1, keepdims=True)
    hn = (x * lax.rsqrt(ms + EPS) * g_ref[...]).astype(BF16)
    for w_ref, o_ref in zip(w_refs, o_refs):
        n = w_ref.shape[1]
        for c in range(0, n, col_chunk):
            ce = min(n, c + col_chunk)
            o_ref[:, c:ce] = jnp.dot(hn, w_ref[:, c:ce], preferred_element_type=F32).astype(o_ref.dtype)


def norm_matmul(x2d, g, ws, out_dtypes, tm=512):
    t, d = x2d.shape
    in_specs = [pl.BlockSpec((tm, d), lambda i: (i, 0)), pl.BlockSpec((1, d), lambda i: (0, 0))]
    in_specs += [pl.BlockSpec(w.shape, lambda i: (0, 0)) for w in ws]
    out_specs = [pl.BlockSpec((tm, w.shape[1]), lambda i: (i, 0)) for w in ws]
    out_shape = [jax.ShapeDtypeStruct((t, w.shape[1]), dt) for w, dt in zip(ws, out_dtypes)]
    return pl.pallas_call(
        functools.partial(_norm_matmul_body, n_out=len(ws), col_chunk=1024),
        grid=(t // tm,), in_specs=in_specs, out_specs=out_specs, out_shape=out_shape,
        compiler_params=_params("parallel"), name="norm_matmul",
    )(x2d, g.reshape(1, d), *ws)


def _matmul_res_body(res_ref, *refs, n_in):
    a_refs, w_refs, o_ref = refs[:n_in], refs[n_in:2 * n_in], refs[2 * n_in]
    acc = res_ref[...]
    for a_ref, w_ref in zip(a_refs, w_refs):
        acc = acc + jnp.dot(a_ref[...], w_ref[...], preferred_element_type=F32)
    o_ref[...] = acc


def matmul_residual(res, a_list, w_list, tm=512):
    t, d = res.shape
    in_specs = [pl.BlockSpec((tm, d), lambda i: (i, 0))]
    in_specs += [pl.BlockSpec((tm, a.shape[1]), lambda i: (i, 0)) for a in a_list]
    in_specs += [pl.BlockSpec(w.shape, lambda i: (0, 0)) for w in w_list]
    return pl.pallas_call(
        functools.partial(_matmul_res_body, n_in=len(a_list)),
        grid=(t // tm,), in_specs=in_specs, out_specs=pl.BlockSpec((tm, d), lambda i: (i, 0)),
        out_shape=jax.ShapeDtypeStruct((t, d), F32),
        compiler_params=_params("parallel"), name="matmul_residual",
    )(res, *a_list, *w_list)


def _head_prep_body(u_ref, *refs, dk, rope, transpose):
    o_ref = refs[-1]
    x = u_ref[...].astype(F32)
    if rope:
        g_ref, c_ref, s1_ref, s2_ref = refs[:4]
        lane = lax.broadcasted_iota(I32, x.shape, 1)
        lo = lane < dk
        x2 = x * x
        s_lo = jnp.sum(jnp.where(lo, x2, 0.0), axis=-1, keepdims=True)
        s_hi = jnp.sum(jnp.where(lo, 0.0, x2), axis=-1, keepdims=True)
        ms = jnp.where(lo, s_lo, s_hi) * (1.0 / dk)
        y = x * lax.rsqrt(ms + EPS) * g_ref[...]
        x = y * c_ref[...] + pltpu.roll(y, 120, 1) * s1_ref[...] + pltpu.roll(y, 8, 1) * s2_ref[...]
    if transpose:
        x = x.T
    o_ref[...] = x.astype(o_ref.dtype)


def _rope_lane_tables(seq, dk, rope_dim, scale):
    half = rope_dim // 2
    inv_freq = 1.0 / (ROPE_THETA ** (jnp.arange(0, rope_dim, 2, dtype=F32) / rope_dim))
    ang = jnp.arange(seq, dtype=F32)[:, None] * inv_freq[None, :]
    cos, sin = jnp.cos(ang), jnp.sin(ang)
    d = jnp.arange(2 * dk) % dk
    fi = d % half
    c_tab = jnp.where(d[None, :] < rope_dim, cos[:, fi], 1.0)
    s1_tab = jnp.where(d[None, :] < half, -sin[:, fi], 0.0)
    s2_tab = jnp.where((d[None, :] >= half) & (d[None, :] < rope_dim), sin[:, fi], 0.0)
    return (jnp.stack([c_tab, s1_tab, s2_tab]) * scale).astype(F32)


def head_prep(u, blk0, norm_gain=None, scale=1.0, transpose=False):
    b, s, _ = u.shape
    h = ATT_HEADS
    rope = norm_gain is not None
    in_specs = [pl.BlockSpec((None, s, 128), lambda c, bi: (bi, 0, blk0 + c))]
    args = [u]
    dk = 64
    if rope:
        dk = norm_gain.shape[0]
        assert 2 * dk == 128 and dk // 4 == 16, "rope roll shifts assume 64-wide components, 16 rotary dims"
        tabs = _rope_lane_tables(s, dk, dk // 4, scale)
        in_specs += [pl.BlockSpec((1, 128), lambda c, bi: (0, 0))] + [pl.BlockSpec((s, 128), lambda c, bi: (0, 0))] * 3
        args += [jnp.tile(norm_gain, 2).reshape(1, 128).astype(F32), tabs[0], tabs[1], tabs[2]]
    if transpose:
        out_spec = pl.BlockSpec((None, 128, s), lambda c, bi: (bi, c, 0))
        out_shape = jax.ShapeDtypeStruct((b, h * 128, s), BF16)
    else:
        out_spec = pl.BlockSpec((None, s, 128), lambda c, bi: (bi, 0, c))
        out_shape = jax.ShapeDtypeStruct((b, s, h * 128), BF16)
    return pl.pallas_call(
        functools.partial(_head_prep_body, dk=dk, rope=rope, transpose=transpose), grid=(h, b),
        in_specs=in_specs, out_specs=out_spec, out_shape=out_shape,
        compiler_params=_params("parallel", "parallel"), name="head_prep",
    )(*args)


def _attn_body(lam_ref, qt_ref, k_ref, vt_ref, g_ref, o_ref, *, tq, dk, post_scale):
    qt = qt_ref[...]
    row = lax.broadcasted_iota(I32, qt.shape, 0)
    zero = jnp.zeros_like(qt)
    qq = jnp.concatenate([jnp.where(row < dk, qt, zero), jnp.where(row < dk, zero, qt)], axis=1)
    st = jnp.dot(k_ref[...], qq, preferred_element_type=F32)
    m = jnp.max(st, axis=0, keepdims=True)
    p = jnp.exp2(st - m)
    r = 1.0 / jnp.sum(p, axis=0, keepdims=True)
    ot = jnp.dot(vt_ref[...], p.astype(BF16), preferred_element_type=F32)
    lam = lam_ref[0, 0]
    o = (ot[:, :tq] * r[:, :tq] - ot[:, tq:] * (lam * r[:, tq:])).T
    ms = jnp.mean(o * o, axis=-1, keepdims=True)
    o_ref[...] = (o * lax.rsqrt(ms + EPS) * g_ref[...] * post_scale).astype(o_ref.dtype)


def diff_attention(qt, k, vt, lam, subln, lambda_init, tq=256):
    b, s, _ = k.shape
    h = ATT_HEADS
    return pl.pallas_call(
        functools.partial(_attn_body, tq=tq, dk=64, post_scale=1.0 - lambda_init),
        grid=(b, h, s // tq),
        in_specs=[pl.BlockSpec(memory_space=pltpu.SMEM),
                  pl.BlockSpec((None, 128, tq), lambda bi, hi, i: (bi, hi, i)),
                  pl.BlockSpec((None, s, 128), lambda bi, hi, i: (bi, 0, hi)),
                  pl.BlockSpec((None, 128, s), lambda bi, hi, i: (bi, hi, 0)),
                  pl.BlockSpec((1, 128), lambda bi, hi, i: (0, 0))],
        out_specs=pl.BlockSpec((None, tq, 128), lambda bi, hi, i: (bi, i, hi)),
        out_shape=jax.ShapeDtypeStruct((b, s, h * 128), BF16),
        compiler_params=_params("parallel", "parallel", "parallel"), name="diff_attention",
    )(lam.reshape(1, 1).astype(F32), qt, k, vt, subln.reshape(1, 128).astype(F32))


def _conv3(u_ref, w_ref, b_ref):
    x = u_ref[...].astype(F32)
    s = x.shape[0]
    row = lax.broadcasted_iota(I32, x.shape, 0)
    x_prev = jnp.where(row == 0, 0.0, pltpu.roll(x, 1, 0))
    x_next = jnp.where(row == s - 1, 0.0, pltpu.roll(x, s - 1, 0))
    w = w_ref[...]
    return b_ref[...] + x_prev * w[0:1] + x * w[1:2] + x_next * w[2:3]


def _hy_prep_body(x1_ref, x2_ref, v_ref, w1_ref, w2_ref, wv_ref, b1_ref, b2_ref, bv_ref, z_ref, x1c_ref):
    x1c_ref[...] = _conv3(x1_ref, w1_ref, b1_ref).astype(x1c_ref.dtype)
    z = _conv3(v_ref, wv_ref, bv_ref) * _conv3(x2_ref, w2_ref, b2_ref)
    z_ref[...] = z.astype(z_ref.dtype)


def hy_prep(u, col0, conv_w, conv_b, tc=256):
    b, s, _ = u.shape
    d_hy = conv_w.shape[1] // 3
    nct = d_hy // tc
    blk0 = col0 // tc

    def uspec(part):
        return pl.BlockSpec((None, s, tc), lambda bi, c: (bi, 0, blk0 + part * nct + c))

    def wspec(part, rows):
        return pl.BlockSpec((rows, tc), lambda bi, c: (0, part * nct + c))

    ospec = pl.BlockSpec((None, s, tc), lambda bi, c: (bi, 0, c))
    return pl.pallas_call(
        _hy_prep_body, grid=(b, nct),
        in_specs=[uspec(0), uspec(1), uspec(2), wspec(0, 3), wspec(1, 3), wspec(2, 3),
                  wspec(0, 1), wspec(1, 1), wspec(2, 1)],
        out_specs=[ospec, ospec],
        out_shape=[jax.ShapeDtypeStruct((b, s, d_hy), BF16), jax.ShapeDtypeStruct((b, s, d_hy), BF16)],
        compiler_params=_params("parallel", "parallel"), name="hy_prep",
    )(u, u, u, conv_w, conv_w, conv_w, conv_b.reshape(1, -1), conv_b.reshape(1, -1), conv_b.reshape(1, -1))


def hyena_filter(length, w1, b1, freq, w2, b2, w3):
    d_hy = w3.shape[1] // 2
    t = jnp.linspace(0.0, 1.0, length, dtype=F32)[:, None]
    bands = jnp.linspace(1e-4, HY_EMB_BANDS - 1, HY_EMB_BANDS, dtype=F32)[None, :]
    ang = (2.0 * math.pi / length) * jnp.arange(length, dtype=F32)[:, None] * bands
    z = jnp.concatenate([t, jnp.cos(ang), -jnp.sin(ang)], axis=-1)
    hp = lax.Precision.HIGHEST
    hdn = jnp.sin(freq * (jnp.dot(z, w1, precision=hp) + b1))
    hdn = jnp.sin(freq * (jnp.dot(hdn, w2, precision=hp) + b2))
    filt = jnp.dot(hdn, w3, precision=hp)
    deltas = jnp.abs(jnp.linspace(HY_MIN_DECAY, HY_MAX_DECAY, d_hy, dtype=F32))
    decay = jnp.exp(-t * deltas[None, :])
    h_fwd = filt[:, :d_hy] * decay
    h_bwd = filt[:, d_hy:] * decay
    kern = jnp.concatenate([h_fwd.at[0].add(h_bwd[0]), jnp.zeros((1, d_hy), F32), h_bwd[:0:-1]], axis=0)
    return kern / (jnp.sum(jnp.abs(kern), axis=0, keepdims=True) + EPS)


def dft_tables(length, tk):
    n = 2 * length
    k = jnp.arange(length, dtype=I32)
    pos = jnp.arange(length, dtype=I32)
    r = 64
    unit = 2.0 * math.pi / n
    ang_a = ((k[:, None] * (jnp.arange(length // r, dtype=I32) * r)[None, :]) % n).astype(F32) * unit
    ang_b = ((k[:, None] * jnp.arange(r, dtype=I32)[None, :]) % n).astype(F32) * unit
    ca, sa = jnp.cos(ang_a)[:, :, None], jnp.sin(ang_a)[:, :, None]
    cb, sb = jnp.cos(ang_b)[:, None, :], jnp.sin(ang_b)[:, None, :]
    cos = (ca * cb - sa * sb).reshape(length, length)
    sin = (sa * cb + ca * sb).reshape(length, length)
    alt = jnp.where(pos % 2 == 0, 1.0, -1.0).astype(F32)
    fwd_re = cos
    fwd_im = jnp.where(k[:, None] == 0, alt[None, :], -sin)
    ck = jnp.where(k == 0, 1.0 / n, 2.0 / n).astype(F32)
    inv_re = cos * ck[:, None]
    inv_im = jnp.where(k[:, None] == 0, alt[None, :] / n, -sin * (2.0 / n))
    nblk = length // tk
    fwd = jnp.concatenate([fwd_re.reshape(nblk, tk, length), fwd_im.reshape(nblk, tk, length)], axis=1)
    inv = jnp.concatenate([inv_re.reshape(nblk, tk, length), inv_im.reshape(nblk, tk, length)], axis=1)
    return fwd.astype(BF16), jnp.swapaxes(inv, 1, 2).astype(BF16)


def _spectrum_body(fwd_ref, ab_ref, kf_ref, *, d_hy):
    p = jnp.dot(fwd_ref[...], ab_ref[...], preferred_element_type=F32)
    row = lax.broadcasted_iota(I32, (p.shape[0], d_hy), 0)
    sign = jnp.where((row & 1) == 0, 1.0, -1.0)
    kf_ref[...] = p[:, :d_hy] + sign * p[:, d_hy:]


def filter_spectrum(kern, fwd):
    nblk, tk2, length = fwd.shape
    d_hy = kern.shape[1]
    ab = jnp.concatenate([kern[:length], kern[length:]], axis=1).astype(BF16)
    return pl.pallas_call(
        functools.partial(_spectrum_body, d_hy=d_hy), grid=(nblk,),
        in_specs=[pl.BlockSpec((None, tk2, length), lambda j: (j, 0, 0)),
                  pl.BlockSpec((length, 2 * d_hy), lambda j: (0, 0))],
        out_specs=pl.BlockSpec((None, tk2, d_hy), lambda j: (j, 0, 0)),
        out_shape=jax.ShapeDtypeStruct((nblk, tk2, d_hy), F32),
        compiler_params=_params("parallel"), name="filter_spectrum",
    )(fwd, ab)


def _hy_conv_body(z_ref, x1c_ref, fwd_ref, inv_ref, kf_ref, skip_ref, o_ref, acc_ref, *, tk):
    j = pl.program_id(1)

    @pl.when(j == 0)
    def _():
        acc_ref[...] = jnp.zeros_like(acc_ref)

    zf = jnp.dot(fwd_ref[...], z_ref[...], preferred_element_type=F32)
    zr, zi = zf[:tk], zf[tk:]
    kf = kf_ref[...]
    kr, ki = kf[:tk], kf[tk:]
    row = lax.broadcasted_iota(I32, zr.shape, 0)
    packed = (row == 0) & (j == 0)
    yr = zr * kr - jnp.where(packed, 0.0, zi * ki)
    yi = jnp.where(packed, zi * ki, zr * ki + zi * kr)
    y = jnp.concatenate([yr, yi], axis=0).astype(BF16)
    acc_ref[...] += jnp.dot(inv_ref[...], y, preferred_element_type=F32)

    @pl.when(j == pl.num_programs(1) - 1)
    def _():
        z = z_ref[...].astype(F32)
        o_ref[...] = ((acc_ref[...] + z * skip_ref[...]) * x1c_ref[...].astype(F32)).astype(o_ref.dtype)


def hy_conv(z, x1c, fwd, inv, kf, skip):
    b, s, c = z.shape
    nblk, tk2, _ = fwd.shape
    return pl.pallas_call(
        functools.partial(_hy_conv_body, tk=tk2 // 2), grid=(b, nblk),
        in_specs=[pl.BlockSpec((None, s, c), lambda bi, j: (bi, 0, 0)),
                  pl.BlockSpec((None, s, c), lambda bi, j: (bi, 0, 0)),
                  pl.BlockSpec((None, tk2, s), lambda bi, j: (j, 0, 0)),
                  pl.BlockSpec((None, s, tk2), lambda bi, j: (j, 0, 0)),
                  pl.BlockSpec((None, tk2, c), lambda bi, j: (j, 0, 0)),
                  pl.BlockSpec((1, c), lambda bi, j: (0, 0))],
        out_specs=pl.BlockSpec((None, s, c), lambda bi, j: (bi, 0, 0)),
        out_shape=jax.ShapeDtypeStruct((b, s, c), BF16),
        scratch_shapes=[pltpu.VMEM((s, c), F32)],
        compiler_params=_params("parallel", "arbitrary"), name="hy_conv",
    )(z, x1c, fwd, inv, kf, skip.reshape(1, c).astype(F32))


def _ml_prep_body(u_ref, w_ref, b_ref, sc_ref, o_ref):
    y = _conv3(u_ref, w_ref, b_ref)
    o_ref[...] = (y * _sigmoid(y) * sc_ref[...]).astype(o_ref.dtype)


def ml_prep(u, conv_w, conv_b, col_scale, tc=256):
    b, s, _ = u.shape
    w = conv_w.shape[1]
    return pl.pallas_call(
        _ml_prep_body, grid=(b, w // tc),
        in_specs=[pl.BlockSpec((None, s, tc), lambda bi, c: (bi, 0, c)),
                  pl.BlockSpec((3, tc), lambda bi, c: (0, c)),
                  pl.BlockSpec((1, tc), lambda bi, c: (0, c)),
                  pl.BlockSpec((1, tc), lambda bi, c: (0, c))],
        out_specs=pl.BlockSpec((None, s, tc), lambda bi, c: (bi, 0, c)),
        out_shape=jax.ShapeDtypeStruct((b, s, w), BF16),
        compiler_params=_params("parallel", "parallel"), name="ml_prep",
    )(u, conv_w, conv_b.reshape(1, w), col_scale.reshape(1, w))


def _log_sigmoid(x):
    return jnp.minimum(x, 0.0) - jnp.log(1.0 + jnp.exp(-jnp.abs(x)))


def _dot_split(a, b, a_is_f32):
    x = a if a_is_f32 else b
    hi = x.astype(BF16)
    lo = (x - hi.astype(F32)).astype(BF16)
    if a_is_f32:
        return (jnp.dot(hi, b, preferred_element_type=F32) + jnp.dot(lo, b, preferred_element_type=F32))
    return (jnp.dot(a, hi, preferred_element_type=F32) + jnp.dot(a, lo, preferred_element_type=F32))


def _mlstm_chain(q, k, v, bc, br, li_r, li_c, total, mask, c_ref, n_ref, m_ref, idx):
    c_st = c_ref[idx]
    n_st = n_ref[idx:idx + 1, :]
    m_st = m_ref[idx:idx + 1, 0:1]
    dmat = jnp.where(mask, bc - br + li_r, NEG_BIG)
    inter = bc + m_st
    m_t = jnp.maximum(inter, jnp.max(dmat, axis=-1, keepdims=True))
    w_intra = jnp.exp(dmat - m_t)
    w_inter = jnp.exp(inter - m_t)
    sc = lax.dot_general(q, k, (((1,), (1,)), ((), ())), preferred_element_type=F32) * w_intra
    num = (w_inter * jnp.dot(q, c_st.astype(BF16), preferred_element_type=F32)
           + jnp.dot(sc.astype(BF16), v, preferred_element_type=F32))
    den = (w_inter * jnp.sum(q.astype(F32) * n_st, axis=-1, keepdims=True)
           + jnp.sum(sc, axis=-1, keepdims=True))
    h = num / jnp.maximum(jnp.abs(den), jnp.exp(-m_t))
    g_s = total - bc + li_c
    m_next = jnp.maximum(total + m_st, jnp.max(g_s, axis=0, keepdims=True))
    a_prev = jnp.exp(total + m_st - m_next)
    kw = k.astype(F32) * jnp.exp(g_s - m_next)
    c_ref[idx] = a_prev * c_st + lax.dot_general(kw.astype(BF16), v, (((0,), (0,)), ((), ())),
                                                 preferred_element_type=F32)
    n_ref[idx:idx + 1, :] = a_prev * n_st + jnp.sum(kw, axis=0, keepdims=True)
    m_ref[idx:idx + 1, :] = jnp.broadcast_to(m_next, (1, m_ref.shape[1]))
    return h


def _mlstm_body(qkf_ref, vf_ref, gcf_ref, grf_ref, qkb_ref, vb_ref, gcb_ref, grb_ref,
                hf_ref, hb_ref, c_ref, n_ref, m_ref, *, heads, dk, dv):
    @pl.when(pl.program_id(1) == 0)
    def _():
        c_ref[...] = jnp.zeros_like(c_ref)
        n_ref[...] = jnp.zeros_like(n_ref)
        m_ref[...] = jnp.zeros_like(m_ref)

    lc = qkf_ref.shape[0]
    t_i = lax.broadcasted_iota(I32, (lc, lc), 0)
    s_i = lax.broadcasted_iota(I32, (lc, lc), 1)
    lower = s_i <= t_i
    upper = s_i >= t_i
    ltri = jnp.where(lower, 1.0, 0.0).astype(BF16)
    utri = jnp.where(upper, 1.0, 0.0).astype(BF16)

    for direction, (qk_ref, v_ref, gc_ref, gr_ref, h_ref) in enumerate(
            ((qkf_ref, vf_ref, gcf_ref, grf_ref, hf_ref), (qkb_ref, vb_ref, gcb_ref, grb_ref, hb_ref))):
        fwd = direction == 0
        gc = gc_ref[...]
        gr = gr_ref[...]
        lfc, lfr = _log_sigmoid(gc), _log_sigmoid(gr)
        cum_c = _dot_split(ltri if fwd else utri, lfc, a_is_f32=False)
        cum_r = _dot_split(lfr, utri if fwd else ltri, a_is_f32=True)
        for hd in range(heads):
            gi = (0 if fwd else 2) * heads + hd
            gf = (1 if fwd else 3) * heads + hd
            bc, br = cum_c[:, gf:gf + 1], cum_r[gf:gf + 1, :]
            total = br[:, lc - 1:lc] if fwd else br[:, 0:1]
            q = qk_ref[:, hd * dk:(hd + 1) * dk]
            k = qk_ref[:, (heads + hd) * dk:(heads + hd + 1) * dk]
            v = v_ref[:, hd * dv:(hd + 1) * dv]
            h = _mlstm_chain(q, k, v, bc, br, gr[gi:gi + 1, :], gc[:, gi:gi + 1], total,
                             lower if fwd else upper, c_ref, n_ref, m_ref, direction * heads + hd)
            h_ref[:, hd * dv:(hd + 1) * dv] = h.astype(h_ref.dtype)


def mlstm_scan(qk, u, v_blk, gcol, grow):
    b, s, w = qk.shape
    heads = ML_HEADS
    dk = w // (2 * heads)
    dv = 2 * dk
    lc = ML_CHUNK
    nc = s // lc
    ng = gcol.shape[-1]

    def fw(bi, j):
        return j

    def bw(bi, j):
        return nc - 1 - j

    def specs(pos):
        return [pl.BlockSpec((None, lc, w), lambda bi, j: (bi, pos(bi, j), 0)),
                pl.BlockSpec((None, lc, heads * dv), lambda bi, j: (bi, pos(bi, j), v_blk)),
                pl.BlockSpec((None, lc, ng), lambda bi, j: (bi, pos(bi, j), 0)),
                pl.BlockSpec((None, ng, lc), lambda bi, j: (bi, 0, pos(bi, j)))]

    hshape = jax.ShapeDtypeStruct((b, s, heads * dv), BF16)
    return pl.pallas_call(
        functools.partial(_mlstm_body, heads=heads, dk=dk, dv=dv), grid=(b, nc),
        in_specs=specs(fw) + specs(bw),
        out_specs=[pl.BlockSpec((None, lc, heads * dv), lambda bi, j: (bi, j, 0)),
                   pl.BlockSpec((None, lc, heads * dv), lambda bi, j: (bi, nc - 1 - j, 0))],
        out_shape=[hshape, hshape],
        scratch_shapes=[pltpu.VMEM((2 * heads, dk, dv), F32), pltpu.VMEM((2 * heads, dk), F32),
                        pltpu.VMEM((2 * heads, 128), F32)],
        compiler_params=_params("parallel", "arbitrary"), name="mlstm_scan",
    )(qk, u, gcol, grow, qk, u, gcol, grow)


def _mlstm_out_body(res_ref, hf_ref, hb_ref, o_ref, g_ref, w_ref, out_ref, *, heads):
    hs = hf_ref[...].astype(F32) + hb_ref[...].astype(F32)
    dv = hs.shape[1] // heads
    g = g_ref[...]
    parts = []
    for hd in range(heads):
        seg = hs[:, hd * dv:(hd + 1) * dv]
        ms = jnp.mean(seg * seg, axis=-1, keepdims=True)
        parts.append(seg * lax.rsqrt(ms + EPS) * g[:, hd * dv:(hd + 1) * dv])
    a = jnp.concatenate(parts, axis=-1) * _sigmoid(o_ref[...].astype(F32))
    out_ref[...] = res_ref[...] + jnp.dot(a.astype(BF16), w_ref[...], preferred_element_type=F32)


def mlstm_out(res, hf, hb, u2d, o_blk, gain, w_out, tm=512):
    t, d = res.shape
    row = lambda i: (i, 0)
    return pl.pallas_call(
        functools.partial(_mlstm_out_body, heads=ML_HEADS), grid=(t // tm,),
        in_specs=[pl.BlockSpec((tm, d), row), pl.BlockSpec((tm, d), row), pl.BlockSpec((tm, d), row),
                  pl.BlockSpec((tm, d), lambda i: (i, o_blk)), pl.BlockSpec((1, d), lambda i: (0, 0)),
                  pl.BlockSpec((d, d), lambda i: (0, 0))],
        out_specs=pl.BlockSpec((tm, d), row), out_shape=jax.ShapeDtypeStruct((t, d), F32),
        compiler_params=_params("parallel"), name="mlstm_out",
    )(res, hf, hb, u2d, gain.reshape(1, d), w_out)


ROUTE_ROWS = 128
EXPERT_ROW0 = 8


def _router_body(x_ref, g_ref, wt_ref, b_ref, o_ref, cnt_ref, run_ref):
    @pl.when(pl.program_id(0) == 0)
    def _():
        run_ref[...] = jnp.zeros_like(run_ref)

    x = x_ref[...]
    ms = jnp.mean(x * x, axis=-1, keepdims=True)
    xn = x * lax.rsqrt(ms + EPS) * g_ref[...]
    logit = lax.dot_general(wt_ref[...], xn, (((1,), (1,)), ((), ())), preferred_element_type=F32,
                            precision=lax.Precision.HIGHEST) + b_ref[...]
    rows = [logit[r:r + 1, :] for r in range(EXPERT_ROW0 + N_EXPERTS)]
    g_best, g_idx = rows[0], jnp.zeros_like(rows[0])
    for gi in range(1, N_GROUPS):
        better = rows[gi] > g_best
        g_best = jnp.where(better, rows[gi], g_best)
        g_idx = jnp.where(better, float(gi), g_idx)
    g_den = sum(jnp.exp(rows[gi] - g_best) for gi in range(N_GROUPS))
    g_w = 1.0 / g_den
    sel = []
    for e in range(EXPERTS_PER_GROUP):
        v = rows[EXPERT_ROW0 + e]
        for gi in range(1, N_GROUPS):
            v = jnp.where(g_idx == float(gi), rows[EXPERT_ROW0 + gi * EXPERTS_PER_GROUP + e], v)
        sel.append(v)
    v1, i1 = sel[0], jnp.zeros_like(sel[0])
    for e in range(1, EXPERTS_PER_GROUP):
        better = sel[e] > v1
        v1 = jnp.where(better, sel[e], v1)
        i1 = jnp.where(better, float(e), i1)
    v2, i2 = jnp.full_like(v1, -jnp.inf), jnp.zeros_like(v1)
    for e in range(EXPERTS_PER_GROUP):
        better = (sel[e] > v2) & (i1 != float(e))
        v2 = jnp.where(better, sel[e], v2)
        i2 = jnp.where(better, float(e), i2)
    e21 = jnp.exp(v2 - v1)
    gate1 = g_w / (1.0 + e21)
    gate2 = gate1 * e21
    base = g_idx * float(EXPERTS_PER_GROUP)
    e1, e2 = base + i1, base + i2
    tm = e1.shape[1]
    erow = lax.broadcasted_iota(I32, (N_EXPERTS, tm), 0).astype(F32)
    oh1 = jnp.where(erow == e1, 1.0, 0.0)
    oh2 = jnp.where(erow == e2, 1.0, 0.0)
    cnt = oh1 + oh2
    earlier = jnp.where(lax.broadcasted_iota(I32, (tm, tm), 0) < lax.broadcasted_iota(I32, (tm, tm), 1),
                        1.0, 0.0).astype(BF16)
    run = run_ref[...]
    pos = run[:, 0:1] + jnp.dot(cnt.astype(BF16), earlier, preferred_element_type=F32)
    rank1 = jnp.sum(oh1 * pos, axis=0, keepdims=True)
    rank2 = jnp.sum(oh2 * pos, axis=0, keepdims=True)
    run = run + jnp.sum(cnt, axis=1, keepdims=True)
    run_ref[...] = run
    cnt_ref[...] = run
    zero = jnp.zeros_like(v1)
    o_ref[...] = jnp.concatenate([e1, e2, gate1, gate2, rank1, rank2, zero, zero], axis=0)


def moe_router(x2d, g, wg, bg, we, be, tm=512):
    t, d = x2d.shape
    wt = jnp.zeros((ROUTE_ROWS, d), F32).at[:N_GROUPS].set(wg.T).at[EXPERT_ROW0:EXPERT_ROW0 + N_EXPERTS].set(we.T)
    bias = jnp.zeros((ROUTE_ROWS, 1), F32).at[:N_GROUPS, 0].set(bg).at[EXPERT_ROW0:EXPERT_ROW0 + N_EXPERTS, 0].set(be)
    return pl.pallas_call(
        _router_body, grid=(t // tm,),
        in_specs=[pl.BlockSpec((tm, d), lambda i: (i, 0)), pl.BlockSpec((1, d), lambda i: (0, 0)),
                  pl.BlockSpec((ROUTE_ROWS, d), lambda i: (0, 0)), pl.BlockSpec((ROUTE_ROWS, 1), lambda i: (0, 0))],
        out_specs=[pl.BlockSpec((8, tm), lambda i: (0, i)), pl.BlockSpec((N_EXPERTS, 128), lambda i: (0, 0))],
        out_shape=[jax.ShapeDtypeStruct((8, t), F32), jax.ShapeDtypeStruct((N_EXPERTS, 128), F32)],
        scratch_shapes=[pltpu.VMEM((N_EXPERTS, 128), F32)],
        compiler_params=_params("arbitrary"), name="moe_router",
    )(x2d, g.reshape(1, d), wt, bias)


def _dispatch_body(dest_ref, zblk_ref, x_ref, g_ref, xs_ref, buf_ref, sem_ref, *, td):
    i = pl.program_id(0)
    n = pl.num_programs(0)
    slot = i % 2

    @pl.when(i == 0)
    def _():
        buf_ref[1] = jnp.zeros(buf_ref.shape[1:], buf_ref.dtype)

        def zero_copy(j):
            return pltpu.make_async_copy(buf_ref.at[1], xs_ref.at[pl.ds(zblk_ref[j] * td, td), :], sem_ref.at[1])

        def start(j, c):
            @pl.when(zblk_ref[j] >= 0)
            def _():
                zero_copy(j).start()
            return c

        def wait(j, c):
            @pl.when(zblk_ref[j] >= 0)
            def _():
                zero_copy(j).wait()
            return c
        lax.fori_loop(0, zblk_ref.shape[0], start, 0)
        lax.fori_loop(0, zblk_ref.shape[0], wait, 0)

    def row_copy(r, kk):
        return pltpu.make_async_copy(buf_ref.at[slot, pl.ds(r, 1), :],
                                     xs_ref.at[pl.ds(dest_ref[(i * td + r) * 2 + kk], 1), :],
                                     sem_ref.at[slot])

    def wait_buffer(sl):
        for _ in range(2):
            pltpu.make_async_copy(buf_ref.at[sl], xs_ref.at[pl.ds(0, td), :], sem_ref.at[sl]).wait()

    @pl.when(i >= 2)
    def _():
        wait_buffer(slot)

    x = x_ref[...]
    ms = jnp.mean(x * x, axis=-1, keepdims=True)
    buf_ref[slot] = x * lax.rsqrt(ms + EPS) * g_ref[...]

    def issue(r, c):
        row_copy(r, 0).start()
        row_copy(r, 1).start()
        return c
    lax.fori_loop(0, td, issue, 0, unroll=8)

    @pl.when(i == n - 1)
    def _():
        @pl.when(n >= 2)
        def _():
            wait_buffer(1 - slot)
        wait_buffer(slot)


def moe_dispatch(x2d, g, dest, zero_blk, cap, td):
    t, d = x2d.shape
    grid_spec = pltpu.PrefetchScalarGridSpec(
        num_scalar_prefetch=2, grid=(t // td,),
        in_specs=[pl.BlockSpec((td, d), lambda i, dest, zb: (i, 0)),
                  pl.BlockSpec((1, d), lambda i, dest, zb: (0, 0))],
        out_specs=pl.BlockSpec(memory_space=pl.ANY),
        scratch_shapes=[pltpu.VMEM((2, td, d), F32), pltpu.SemaphoreType.DMA((2,))])
    return pl.pallas_call(
        functools.partial(_dispatch_body, td=td), grid_spec=grid_spec,
        out_shape=jax.ShapeDtypeStruct((cap, d), F32),
        compiler_params=_params("arbitrary", disable_bounds_checks=True),
        name="moe_dispatch",
    )(dest, zero_blk, x2d, g.reshape(1, d))


def _expert_body(blk_e_ref, nused_ref, xs_ref, w1_ref, w3_ref, w2_ref, ys_ref, w1b_ref, w3b_ref, w2b_ref):
    i = pl.program_id(0)
    nused = nused_ref[0]
    last = nused - 1
    cur = blk_e_ref[jnp.minimum(i, last)]
    prev = blk_e_ref[jnp.minimum(jnp.maximum(i - 1, 0), last)]

    @pl.when((i == 0) | (cur != prev))
    def _():
        w1b_ref[...] = w1_ref[...].astype(BF16)
        w3b_ref[...] = w3_ref[...].astype(BF16)
        w2b_ref[...] = w2_ref[...].astype(BF16)

    @pl.when(i < nused)
    def _():
        x = xs_ref[...].astype(BF16)
        h1 = jnp.dot(x, w1b_ref[...], preferred_element_type=F32)
        h3 = jnp.dot(x, w3b_ref[...], preferred_element_type=F32)
        hid = (h1 * _sigmoid(h1) * h3).astype(BF16)
        ys_ref[...] = jnp.dot(hid, w2b_ref[...], preferred_element_type=F32)

    @pl.when(i >= nused)
    def _():
        ys_ref[...] = jnp.zeros_like(ys_ref)


def moe_experts(xs, blk_e, nused, w1, w3, w2, layer, tm):
    cap, d = xs.shape
    de = w1.shape[-1]

    def blk(i, be, nu):
        return jnp.minimum(i, nu[0] - 1)

    grid_spec = pltpu.PrefetchScalarGridSpec(
        num_scalar_prefetch=2, grid=(cap // tm,),
        in_specs=[pl.BlockSpec((tm, d), lambda i, be, nu: (blk(i, be, nu), 0)),
                  pl.BlockSpec((None, None, d, de), lambda i, be, nu: (layer, be[blk(i, be, nu)], 0, 0)),
                  pl.BlockSpec((None, None, d, de), lambda i, be, nu: (layer, be[blk(i, be, nu)], 0, 0)),
                  pl.BlockSpec((None, None, de, d), lambda i, be, nu: (layer, be[blk(i, be, nu)], 0, 0))],
        out_specs=pl.BlockSpec((tm, d), lambda i, be, nu: (i, 0)),
        scratch_shapes=[pltpu.VMEM((d, de), BF16), pltpu.VMEM((d, de), BF16), pltpu.VMEM((de, d), BF16)])
    return pl.pallas_call(
        _expert_body, grid_spec=grid_spec, out_shape=jax.ShapeDtypeStruct((cap, d), F32),
        compiler_params=_params("arbitrary"), name="moe_experts",
    )(blk_e, nused, xs, w1, w3, w2)


def _combine_body(dest_ref, x_ref, gate_ref, ys_ref, o_ref, buf_ref, sem_ref, *, tc):
    i = pl.program_id(0)
    n = pl.num_programs(0)
    slot = i % 2

    def row_copy(step, sl, r, kk):
        return pltpu.make_async_copy(ys_ref.at[pl.ds(dest_ref[(step * tc + r) * 2 + kk], 1), :],
                                     buf_ref.at[sl, kk, pl.ds(r, 1), :], sem_ref.at[sl])

    def issue_step(step, sl):
        def body(r, c):
            row_copy(step, sl, r, 0).start()
            row_copy(step, sl, r, 1).start()
            return c
        lax.fori_loop(0, tc, body, 0, unroll=8)

    @pl.when(i == 0)
    def _():
        issue_step(0, 0)

    @pl.when(i + 1 < n)
    def _():
        issue_step(i + 1, 1 - slot)

    for kk in range(2):
        pltpu.make_async_copy(ys_ref.at[pl.ds(0, tc), :], buf_ref.at[slot, kk], sem_ref.at[slot]).wait()

    gate = gate_ref[...]
    o_ref[...] = x_ref[...] + gate[:, 0:1] * buf_ref[slot, 0] + gate[:, 1:2] * buf_ref[slot, 1]


def moe_combine(x2d, gates, ys, dest, tc=256):
    t, d = x2d.shape
    grid_spec = pltpu.PrefetchScalarGridSpec(
        num_scalar_prefetch=1, grid=(t // tc,),
        in_specs=[pl.BlockSpec((tc, d), lambda i, dest: (i, 0)), pl.BlockSpec((tc, 2), lambda i, dest: (i, 0)),
                  pl.BlockSpec(memory_space=pl.ANY)],
        out_specs=pl.BlockSpec((tc, d), lambda i, dest: (i, 0)),
        scratch_shapes=[pltpu.VMEM((2, 2, tc, d), F32), pltpu.SemaphoreType.DMA((2,))])
    return pl.pallas_call(
        functools.partial(_combine_body, tc=tc), grid_spec=grid_spec,
        out_shape=jax.ShapeDtypeStruct((t, d), F32),
        compiler_params=_params("arbitrary", disable_bounds_checks=True), name="moe_combine",
    )(dest, x2d, gates, ys)


def hier_moe_residual(x2d, g, wg, bg, we, be, w1, w3, w2, layer, tm=256):
    t, d = x2d.shape
    route, counts_b = moe_router(x2d, g, wg, bg, we, be)
    eid = route[0:2].T.astype(I32).reshape(-1)
    rank = route[4:6].T.astype(I32).reshape(-1)
    gates = route[2:4].T
    counts = counts_b[:, 0].astype(I32)
    padded = (counts + tm - 1) // tm * tm
    pad_end = jnp.cumsum(padded)
    pad_start = pad_end - padded
    experts = jnp.arange(N_EXPERTS, dtype=I32)
    dest = rank + jnp.sum(jnp.where(eid[:, None] == experts[None, :], pad_start[None, :], 0), axis=1)
    cap = t * 2 + N_EXPERTS * tm
    nblk = cap // tm
    blk_e = jnp.minimum(jnp.searchsorted(pad_end, jnp.arange(nblk, dtype=I32) * tm, side="right"),
                        N_EXPERTS - 1).astype(I32)
    nused = (pad_end[-1:] // tm).astype(I32)
    last_blk = jnp.where(padded > 0, pad_end // tm - 1, -1)
    tail_blk = jnp.where(nused[0] + experts < nblk, nused[0] + experts, -1)
    zero_blk = jnp.concatenate([last_blk, tail_blk]).astype(I32)
    xs = moe_dispatch(x2d, g, dest.astype(I32), zero_blk, cap, tm)
    ys = moe_experts(xs, blk_e, nused, w1, w3, w2, layer, tm)
    return moe_combine(x2d, gates, ys, dest.astype(I32))


def _even_layer(x2d, b, s, layer, mix_g, w_in, q_norm, k_norm, lam_q1, lam_k1, lam_q2, lam_k2, subln,
                hy_conv_w, hy_conv_b, f_w1, f_b1, f_freq, f_w2, f_b2, f_w3, hy_skip, w_out):
    d = x2d.shape[1]
    d_att = d // 2
    (u2d,) = norm_matmul(x2d, mix_g, [w_in.astype(BF16)], [BF16])
    u = u2d.reshape(b, s, -1)
    lambda_init = 0.8 - 0.6 * math.exp(-0.3 * layer)
    lam = jnp.exp(jnp.sum(lam_q1 * lam_k1)) - jnp.exp(jnp.sum(lam_q2 * lam_k2)) + lambda_init
    dk = q_norm.shape[0]
    qt = head_prep(u, 0, q_norm, scale=dk ** -0.5 * math.log2(math.e), transpose=True)
    kp = head_prep(u, ATT_HEADS, k_norm)
    vt = head_prep(u, 2 * ATT_HEADS, transpose=True)
    y_att = diff_attention(qt, kp, vt, lam, subln, lambda_init)
    z, x1c = hy_prep(u, 3 * d_att, hy_conv_w, hy_conv_b)
    kern = hyena_filter(s, f_w1, f_b1, f_freq, f_w2, f_b2, f_w3)
    fwd, inv = dft_tables(s, 128)
    kf = filter_spectrum(kern, fwd)
    y_hy = hy_conv(z, x1c, fwd, inv, kf, hy_skip)
    w_out = w_out.astype(BF16)
    return matmul_residual(x2d, [y_att.reshape(b * s, -1), y_hy.reshape(b * s, -1)],
                           [w_out[:d_att], w_out[d_att:]])


def _odd_layer(x2d, b, s, mix_g, w_in, conv_w, conv_b, gate_b, out_norm, w_out):
    d = x2d.shape[1]
    qk_w = conv_w.shape[1]
    main_w = qk_w + 2 * d
    ng = 4 * ML_HEADS
    w_gate = jnp.zeros((d, 128), F32).at[:, :ng].set(w_in[:, main_w:]).astype(BF16)
    u2d, ug = norm_matmul(x2d, mix_g, [w_in[:, :main_w].astype(BF16), w_gate], [BF16, F32])
    u = u2d.reshape(b, s, main_w)
    gcol = (ug[:, :ng] + gate_b).reshape(b, s, ng)
    grow = jnp.swapaxes(gcol, 1, 2)
    dk = qk_w // (2 * ML_HEADS)
    col_scale = jnp.concatenate([jnp.full((qk_w // 2,), dk ** -0.5, F32), jnp.ones((qk_w // 2,), F32)])
    qk = ml_prep(u, conv_w, conv_b, col_scale)
    hf, hb = mlstm_scan(qk, u, qk_w // d, gcol, grow)
    return mlstm_out(x2d, hf.reshape(b * s, d), hb.reshape(b * s, d), u2d, (qk_w + d) // d, out_norm,
                     w_out.astype(BF16))


def kernel(x, mix_norm, ffn_norm, ev_w_in, ev_q_norm, ev_k_norm, ev_lam_q1, ev_lam_k1, ev_lam_q2, ev_lam_k2, ev_subln, ev_hy_conv_w, ev_hy_conv_b, ev_hy_f_w1, ev_hy_f_b1, ev_hy_f_freq, ev_hy_f_w2, ev_hy_f_b2, ev_hy_f_w3, ev_hy_skip, ev_w_out, od_w_in, od_conv_w, od_conv_b, od_gate_b, od_out_norm, od_w_out, moe_wg, moe_bg, moe_we, moe_be, moe_w1, moe_w3, moe_w2):
    b, s, d = x.shape
    depth = mix_norm.shape[0]
    x2d = x.reshape(b * s, d)
    for layer in range(depth):
        j = layer // 2
        if layer % 2 == 0:
            x2d = _even_layer(x2d, b, s, layer, mix_norm[layer], ev_w_in[j], ev_q_norm[j], ev_k_norm[j],
                              ev_lam_q1[j], ev_lam_k1[j], ev_lam_q2[j], ev_lam_k2[j], ev_subln[j],
                              ev_hy_conv_w[j], ev_hy_conv_b[j], ev_hy_f_w1[j], ev_hy_f_b1[j], ev_hy_f_freq[j],
                              ev_hy_f_w2[j], ev_hy_f_b2[j], ev_hy_f_w3[j], ev_hy_skip[j], ev_w_out[j])
        else:
            x2d = _odd_layer(x2d, b, s, mix_norm[layer], od_w_in[j], od_conv_w[j], od_conv_b[j], od_gate_b[j],
                             od_out_norm[j], od_w_out[j])
        x2d = hier_moe_residual(x2d, ffn_norm[layer], moe_wg[layer], moe_bg[layer], moe_we[layer],
                                moe_be[layer], moe_w1, moe_w3, moe_w2, layer)
    return x2d.reshape(b, s, d)
```

```python
import functools
import math

import jax
import jax.numpy as jnp
from jax import lax
from jax.experimental import pallas as pl
from jax.experimental.pallas import tpu as pltpu

F32 = jnp.float32
BF16 = jnp.bfloat16
I32 = jnp.int32

EPS = 1e-6
ROPE_THETA = 500000.0
ATT_HEADS = 4
ML_HEADS = 4
ML_CHUNK = 128
N_GROUPS = 4
EXPERTS_PER_GROUP = 8
N_EXPERTS = N_GROUPS * EXPERTS_PER_GROUP
HY_EMB_BANDS = 16
HY_MIN_DECAY = math.log(1e-2) / 1.5
HY_MAX_DECAY = math.log(1e-2) / 0.3

V7X_VMEM_BYTES = 64 * 1024 * 1024
VMEM_LIMIT = V7X_VMEM_BYTES - 8 * 1024 * 1024
NEG_BIG = -1e30


def _params(*sem, **kw):
    return pltpu.CompilerParams(dimension_semantics=sem, vmem_limit_bytes=VMEM_LIMIT, **kw)


def _sigmoid(x):
    return 1.0 / (1.0 + jnp.exp(-x))


def _norm_matmul_body(x_ref, g_ref, *refs, n_out, col_chunk):
    w_refs, o_refs = refs[:n_out], refs[n_out:]
    x = x_ref[...]
    ms = jnp.mean(x * x, axis=-1, keepdims=True)
    hn = (x * lax.rsqrt(ms + EPS) * g_ref[...]).astype(BF16)
    for w_ref, o_ref in zip(w_refs, o_refs):
        n = w_ref.shape[1]
        for c in range(0, n, col_chunk):
            ce = min(n, c + col_chunk)
            o_ref[:, c:ce] = jnp.dot(hn, w_ref[:, c:ce], preferred_element_type=F32).astype(o_ref.dtype)


def norm_matmul(x2d, g, ws, out_dtypes, tm=512):
    t, d = x2d.shape
    in_specs = [pl.BlockSpec((tm, d), lambda i: (i, 0)), pl.BlockSpec((1, d), lambda i: (0, 0))]
    in_specs += [pl.BlockSpec(w.shape, lambda i: (0, 0)) for w in ws]
    out_specs = [pl.BlockSpec((tm, w.shape[1]), lambda i: (i, 0)) for w in ws]
    out_shape = [jax.ShapeDtypeStruct((t, w.shape[1]), dt) for w, dt in zip(ws, out_dtypes)]
    return pl.pallas_call(
        functools.partial(_norm_matmul_body, n_out=len(ws), col_chunk=1024),
        grid=(t // tm,), in_specs=in_specs, out_specs=out_specs, out_shape=out_shape,
        compiler_params=_params("parallel"), name="norm_matmul",
    )(x2d, g.reshape(1, d), *ws)


def _matmul_res_body(res_ref, *refs, n_in):
    a_refs, w_refs, o_ref = refs[:n_in], refs[n_in:2 * n_in], refs[2 * n_in]
    acc = res_ref[...]
    for a_ref, w_ref in zip(a_refs, w_refs):
        acc = acc + jnp.dot(a_ref[...], w_ref[...], preferred_element_type=F32)
    o_ref[...] = acc


def matmul_residual(res, a_list, w_list, tm=512):
    t, d = res.shape
    in_specs = [pl.BlockSpec((tm, d), lambda i: (i, 0))]
    in_specs += [pl.BlockSpec((tm, a.shape[1]), lambda i: (i, 0)) for a in a_list]
    in_specs += [pl.BlockSpec(w.shape, lambda i: (0, 0)) for w in w_list]
    return pl.pallas_call(
        functools.partial(_matmul_res_body, n_in=len(a_list)),
        grid=(t // tm,), in_specs=in_specs, out_specs=pl.BlockSpec((tm, d), lambda i: (i, 0)),
        out_shape=jax.ShapeDtypeStruct((t, d), F32),
        compiler_params=_params("parallel"), name="matmul_residual",
    )(res, *a_list, *w_list)


def _head_prep_body(u_ref, *refs, dk, rope, transpose):
    o_ref = refs[-1]
    x = u_ref[...].astype(F32)
    if rope:
        g_ref, c_ref, s1_ref, s2_ref = refs[:4]
        lane = lax.broadcasted_iota(I32, x.shape, 1)
        lo = lane < dk
        x2 = x * x
        s_lo = jnp.sum(jnp.where(lo, x2, 0.0), axis=-1, keepdims=True)
        s_hi = jnp.sum(jnp.where(lo, 0.0, x2), axis=-1, keepdims=True)
        ms = jnp.where(lo, s_lo, s_hi) * (1.0 / dk)
        y = x * lax.rsqrt(ms + EPS) * g_ref[...]
        x = y * c_ref[...] + pltpu.roll(y, 120, 1) * s1_ref[...] + pltpu.roll(y, 8, 1) * s2_ref[...]
    if transpose:
        x = x.T
    o_ref[...] = x.astype(o_ref.dtype)


def _rope_lane_tables(seq, dk, rope_dim, scale):
    half = rope_dim // 2
    inv_freq = 1.0 / (ROPE_THETA ** (jnp.arange(0, rope_dim, 2, dtype=F32) / rope_dim))
    ang = jnp.arange(seq, dtype=F32)[:, None] * inv_freq[None, :]
    cos, sin = jnp.cos(ang), jnp.sin(ang)
    d = jnp.arange(2 * dk) % dk
    fi = d % half
    c_tab = jnp.where(d[None, :] < rope_dim, cos[:, fi], 1.0)
    s1_tab = jnp.where(d[None, :] < half, -sin[:, fi], 0.0)
    s2_tab = jnp.where((d[None, :] >= half) & (d[None, :] < rope_dim), sin[:, fi], 0.0)
    return (jnp.stack([c_tab, s1_tab, s2_tab]) * scale).astype(F32)


def head_prep(u, blk0, norm_gain=None, scale=1.0, transpose=False):
    b, s, _ = u.shape
    h = ATT_HEADS
    rope = norm_gain is not None
    in_specs = [pl.BlockSpec((None, s, 128), lambda c, bi: (bi, 0, blk0 + c))]
    args = [u]
    dk = 64
    if rope:
        dk = norm_gain.shape[0]
        assert 2 * dk == 128 and dk // 4 == 16, "rope roll shifts assume 64-wide components, 16 rotary dims"
        tabs = _rope_lane_tables(s, dk, dk // 4, scale)
        in_specs += [pl.BlockSpec((1, 128), lambda c, bi: (0, 0))] + [pl.BlockSpec((s, 128), lambda c, bi: (0, 0))] * 3
        args += [jnp.tile(norm_gain, 2).reshape(1, 128).astype(F32), tabs[0], tabs[1], tabs[2]]
    if transpose:
        out_spec = pl.BlockSpec((None, 128, s), lambda c, bi: (bi, c, 0))
        out_shape = jax.ShapeDtypeStruct((b, h * 128, s), BF16)
    else:
        out_spec = pl.BlockSpec((None, s, 128), lambda c, bi: (bi, 0, c))
        out_shape = jax.ShapeDtypeStruct((b, s, h * 128), BF16)
    return pl.pallas_call(
        functools.partial(_head_prep_body, dk=dk, rope=rope, transpose=transpose), grid=(h, b),
        in_specs=in_specs, out_specs=out_spec, out_shape=out_shape,
        compiler_params=_params("parallel", "parallel"), name="head_prep",
    )(*args)


def _attn_body(lam_ref, qt_ref, k_ref, vt_ref, g_ref, o_ref, *, tq, dk, post_scale):
    qt = qt_ref[...]
    row = lax.broadcasted_iota(I32, qt.shape, 0)
    zero = jnp.zeros_like(qt)
    qq = jnp.concatenate([jnp.where(row < dk, qt, zero), jnp.where(row < dk, zero, qt)], axis=1)
    st = jnp.dot(k_ref[...], qq, preferred_element_type=F32)
    m = jnp.max(st, axis=0, keepdims=True)
    p = jnp.exp2(st - m)
    r = 1.0 / jnp.sum(p, axis=0, keepdims=True)
    ot = jnp.dot(vt_ref[...], p.astype(BF16), preferred_element_type=F32)
    lam = lam_ref[0, 0]
    o = (ot[:, :tq] * r[:, :tq] - ot[:, tq:] * (lam * r[:, tq:])).T
    ms = jnp.mean(o * o, axis=-1, keepdims=True)
    o_ref[...] = (o * lax.rsqrt(ms + EPS) * g_ref[...] * post_scale).astype(o_ref.dtype)


def diff_attention(qt, k, vt, lam, subln, lambda_init, tq=256):
    b, s, _ = k.shape
    h = ATT_HEADS
    return pl.pallas_call(
        functools.partial(_attn_body, tq=tq, dk=64, post_scale=1.0 - lambda_init),
        grid=(b, h, s // tq),
        in_specs=[pl.BlockSpec(memory_space=pltpu.SMEM),
                  pl.BlockSpec((None, 128, tq), lambda bi, hi, i: (bi, hi, i)),
                  pl.BlockSpec((None, s, 128), lambda bi, hi, i: (bi, 0, hi)),
                  pl.BlockSpec((None, 128, s), lambda bi, hi, i: (bi, hi, 0)),
                  pl.BlockSpec((1, 128), lambda bi, hi, i: (0, 0))],
        out_specs=pl.BlockSpec((None, tq, 128), lambda bi, hi, i: (bi, i, hi)),
        out_shape=jax.ShapeDtypeStruct((b, s, h * 128), BF16),
        compiler_params=_params("parallel", "parallel", "parallel"), name="diff_attention",
    )(lam.reshape(1, 1).astype(F32), qt, k, vt, subln.reshape(1, 128).astype(F32))


def _conv3(u_ref, w_ref, b_ref):
    x = u_ref[...].astype(F32)
    s = x.shape[0]
    row = lax.broadcasted_iota(I32, x.shape, 0)
    x_prev = jnp.where(row == 0, 0.0, pltpu.roll(x, 1, 0))
    x_next = jnp.where(row == s - 1, 0.0, pltpu.roll(x, s - 1, 0))
    w = w_ref[...]
    return b_ref[...] + x_prev * w[0:1] + x * w[1:2] + x_next * w[2:3]


FFT_N1 = 64
FFT_UNROLL = 8
FFT_PAD = 8


def _hy_prep_body(x1_ref, x2_ref, v_ref, w1_ref, w2_ref, wv_ref, b1_ref, b2_ref, bv_ref, z_ref, x1c_ref, zp_ref):
    x1c_ref[...] = _conv3(x1_ref, w1_ref, b1_ref).astype(x1c_ref.dtype)
    z = _conv3(v_ref, wv_ref, bv_ref) * _conv3(x2_ref, w2_ref, b2_ref)
    z_ref[...] = z.astype(z_ref.dtype)
    nb = z.shape[0] // FFT_N1
    zp_ref[...] = jnp.zeros_like(zp_ref)
    for n2 in range(nb):
        for ci in range(z.shape[1] // 128):
            zp_ref[ci, pl.ds(n2, FFT_N1, stride=nb + FFT_PAD), :] = (
                z[n2 * FFT_N1:(n2 + 1) * FFT_N1, ci * 128:(ci + 1) * 128])


def hy_prep(u, col0, conv_w, conv_b, tc=256):
    b, s, _ = u.shape
    d_hy = conv_w.shape[1] // 3
    nct = d_hy // tc
    blk0 = col0 // tc
    sp = FFT_N1 * (s // FFT_N1 + FFT_PAD)

    def uspec(part):
        return pl.BlockSpec((None, s, tc), lambda bi, c: (bi, 0, blk0 + part * nct + c))

    def wspec(part, rows):
        return pl.BlockSpec((rows, tc), lambda bi, c: (0, part * nct + c))

    ospec = pl.BlockSpec((None, s, tc), lambda bi, c: (bi, 0, c))
    return pl.pallas_call(
        _hy_prep_body, grid=(b, nct),
        in_specs=[uspec(0), uspec(1), uspec(2), wspec(0, 3), wspec(1, 3), wspec(2, 3),
                  wspec(0, 1), wspec(1, 1), wspec(2, 1)],
        out_specs=[ospec, ospec, pl.BlockSpec((None, tc // 128, sp, 128), lambda bi, c: (bi, c, 0, 0))],
        out_shape=[jax.ShapeDtypeStruct((b, s, d_hy), BF16), jax.ShapeDtypeStruct((b, s, d_hy), BF16),
                   jax.ShapeDtypeStruct((b, d_hy // 128, sp, 128), F32)],
        compiler_params=_params("parallel", "parallel"), name="hy_prep",
    )(u, u, u, conv_w, conv_w, conv_w, conv_b.reshape(1, -1), conv_b.reshape(1, -1), conv_b.reshape(1, -1))


def hyena_filter(length, w1, b1, freq, w2, b2, w3):
    d_hy = w3.shape[1] // 2
    t = jnp.linspace(0.0, 1.0, length, dtype=F32)[:, None]
    bands = jnp.linspace(1e-4, HY_EMB_BANDS - 1, HY_EMB_BANDS, dtype=F32)[None, :]
    ang = (2.0 * math.pi / length) * jnp.arange(length, dtype=F32)[:, None] * bands
    z = jnp.concatenate([t, jnp.cos(ang), -jnp.sin(ang)], axis=-1)
    hp = lax.Precision.HIGHEST
    hdn = jnp.sin(freq * (jnp.dot(z, w1, precision=hp) + b1))
    hdn = jnp.sin(freq * (jnp.dot(hdn, w2, precision=hp) + b2))
    filt = jnp.dot(hdn, w3, precision=hp)
    deltas = jnp.abs(jnp.linspace(HY_MIN_DECAY, HY_MAX_DECAY, d_hy, dtype=F32))
    decay = jnp.exp(-t * deltas[None, :])
    h_fwd = filt[:, :d_hy] * decay
    h_bwd = filt[:, d_hy:] * decay
    h_fwd = h_fwd.at[0].add(h_bwd[0])
    h_bwd = h_bwd.at[0].set(0.0)
    norm = jnp.sum(jnp.abs(h_fwd), axis=0, keepdims=True) + jnp.sum(jnp.abs(h_bwd), axis=0, keepdims=True) + EPS
    return jnp.concatenate([h_fwd / norm, h_bwd / norm], axis=1)


def fft_tables(length):
    n = 2 * length
    n1c, n2c, nb = FFT_N1, n // FFT_N1, length // FFT_N1
    unit = 2.0 * math.pi / n
    i1 = jnp.arange(n1c, dtype=I32)
    i2 = jnp.arange(n2c, dtype=I32)
    ib = jnp.arange(nb, dtype=I32)
    samp = i1[:, None, None] + n1c * ib[None, None, :]
    ang = ((i2[None, :, None] * samp) % n).astype(F32) * unit
    m1 = jnp.concatenate([jnp.cos(ang), -jnp.sin(ang)], axis=1)
    ang = ((i1[:, None] * i1[None, :]) % n1c).astype(F32) * (2.0 * math.pi / n1c)
    c, s = jnp.cos(ang), jnp.sin(ang)
    f1 = jnp.concatenate([jnp.concatenate([c, s], axis=1), jnp.concatenate([-s, c], axis=1)], axis=0)
    freq = n2c * i1[None, None, :] + i2[:, None, None]
    ang = ((i1[None, :, None] * freq) % n).astype(F32) * unit
    c, s = jnp.cos(ang), jnp.sin(ang)
    g1 = jnp.concatenate([jnp.concatenate([c, -s], axis=2), jnp.concatenate([s, c], axis=2)], axis=1)
    ang = ((ib[:, None] * i2[None, :]) % n2c).astype(F32) * (2.0 * math.pi / n2c)
    g2 = jnp.concatenate([jnp.cos(ang), -jnp.sin(ang)], axis=1) * (1.0 / n)
    return m1.astype(BF16), f1.astype(BF16), g1.astype(BF16), g2.astype(BF16)


def _fft_stage1(xp_ref, m1_ref, p_ref):
    nb = m1_ref.shape[2]
    n2c = m1_ref.shape[1] // 2

    def body(n1, c):
        x = xp_ref[pl.ds(pl.multiple_of(n1 * (nb + FFT_PAD), 8), nb), :].astype(BF16)
        a = jnp.dot(m1_ref[n1], x, preferred_element_type=F32)
        p_ref[0, pl.ds(n1, n2c, stride=FFT_N1 + FFT_PAD), :] = a[:n2c]
        p_ref[1, pl.ds(n1, n2c, stride=FFT_N1 + FFT_PAD), :] = a[n2c:]
        return c
    lax.fori_loop(0, FFT_N1, body, 0, unroll=FFT_UNROLL)


def _fft_stage2(p_ref, f1_ref, k2):
    r0 = pl.multiple_of(k2 * (FFT_N1 + FFT_PAD), 8)
    slab = jnp.concatenate([p_ref[0, pl.ds(r0, FFT_N1), :], p_ref[1, pl.ds(r0, FFT_N1), :]], axis=0)
    return jnp.dot(f1_ref[...], slab.astype(BF16), preferred_element_type=F32)


def _spectrum_body(xp_ref, m1_ref, f1_ref, o_ref, p_ref):
    _fft_stage1(xp_ref, m1_ref, p_ref)

    def body(k2, c):
        o_ref[k2] = _fft_stage2(p_ref, f1_ref, k2)
        return c
    lax.fori_loop(0, o_ref.shape[0], body, 0, unroll=FFT_UNROLL)


def filter_spectrum(ab, m1, f1):
    length, c2 = ab.shape
    nb = length // FFT_N1
    n2c = m1.shape[1] // 2
    nch = c2 // 128
    abp = jnp.pad(ab.reshape(nb, FFT_N1, nch, 128).transpose(2, 1, 0, 3), ((0, 0), (0, 0), (0, FFT_PAD), (0, 0)))
    abp = abp.reshape(nch, FFT_N1 * (nb + FFT_PAD), 128)
    spec = pl.pallas_call(
        _spectrum_body, grid=(nch,),
        in_specs=[pl.BlockSpec((None,) + abp.shape[1:], lambda c: (c, 0, 0)),
                  pl.BlockSpec(m1.shape, lambda c: (0, 0, 0)), pl.BlockSpec(f1.shape, lambda c: (0, 0))],
        out_specs=pl.BlockSpec((None, n2c, 2 * FFT_N1, 128), lambda c: (c, 0, 0, 0)),
        out_shape=jax.ShapeDtypeStruct((nch, n2c, 2 * FFT_N1, 128), F32),
        scratch_shapes=[pltpu.VMEM((2, n2c * (FFT_N1 + FFT_PAD), 128), F32)],
        compiler_params=_params("parallel"), name="filter_spectrum",
    )(abp, m1, f1)
    fa, fb = spec[:nch // 2], spec[nch // 2:]
    h = FFT_N1
    return jnp.concatenate([fa[:, :, :h] + fb[:, :, :h], fa[:, :, h:] - fb[:, :, h:]], axis=2).astype(BF16)


def _hy_fft_body(zp_ref, z_ref, x1c_ref, kf_ref, m1_ref, f1_ref, g1_ref, g2_ref, skip_ref, o_ref,
                 p_ref, q_ref, y_ref):
    h = FFT_N1
    n2c = g1_ref.shape[0]
    nb = g2_ref.shape[0]
    _fft_stage1(zp_ref, m1_ref, p_ref)

    def mid(k2, c):
        xf = _fft_stage2(p_ref, f1_ref, k2)
        kf = kf_ref[k2].astype(F32)
        xr, xi, kr, ki = xf[:h], xf[h:], kf[:h], kf[h:]
        y = jnp.concatenate([xr * kr - xi * ki, xr * ki + xi * kr], axis=0).astype(BF16)
        d = jnp.dot(g1_ref[k2], y, preferred_element_type=F32)
        q_ref[0, pl.ds(k2, h, stride=n2c + FFT_PAD), :] = d[:h]
        q_ref[1, pl.ds(k2, h, stride=n2c + FFT_PAD), :] = d[h:]
        return c
    lax.fori_loop(0, n2c, mid, 0, unroll=FFT_UNROLL)

    def last(t1, c):
        r0 = pl.multiple_of(t1 * (n2c + FFT_PAD), 8)
        slab = jnp.concatenate([q_ref[0, pl.ds(r0, n2c), :], q_ref[1, pl.ds(r0, n2c), :]], axis=0)
        y_ref[pl.ds(t1, nb, stride=h), :] = jnp.dot(g2_ref[...], slab.astype(BF16), preferred_element_type=F32)
        return c
    lax.fori_loop(0, h, last, 0, unroll=FFT_UNROLL)

    z = z_ref[...].astype(F32)
    o_ref[...] = ((y_ref[...] + z * skip_ref[...]) * x1c_ref[...].astype(F32)).astype(o_ref.dtype)


def hy_fft_conv(zp, z, x1c, kf, tables, skip):
    m1, f1, g1, g2 = tables
    b, s, c = z.shape
    nch = c // 128

    def const(shape):
        return pl.BlockSpec(shape, lambda ci, bi: (0,) * len(shape))

    nat = pl.BlockSpec((None, s, 128), lambda ci, bi: (bi, 0, ci))
    return pl.pallas_call(
        _hy_fft_body, grid=(nch, b),
        in_specs=[pl.BlockSpec((None, None) + zp.shape[2:], lambda ci, bi: (bi, ci, 0, 0)), nat, nat,
                  pl.BlockSpec((None,) + kf.shape[1:], lambda ci, bi: (ci, 0, 0, 0)),
                  const(m1.shape), const(f1.shape), const(g1.shape), const(g2.shape),
                  pl.BlockSpec((1, 128), lambda ci, bi: (0, ci))],
        out_specs=nat,
        out_shape=jax.ShapeDtypeStruct((b, s, c), BF16),
        scratch_shapes=[pltpu.VMEM((2, g1.shape[0] * (FFT_N1 + FFT_PAD), 128), F32),
                        pltpu.VMEM((2, FFT_N1 * (g1.shape[0] + FFT_PAD), 128), F32),
                        pltpu.VMEM((s, 128), F32)],
        compiler_params=_params("parallel", "parallel"), name="hy_fft_conv",
    )(zp, z, x1c, kf, m1, f1, g1, g2, skip.reshape(1, c).astype(F32))


def _ml_prep_body(u_ref, w_ref, b_ref, sc_ref, o_ref):
    y = _conv3(u_ref, w_ref, b_ref)
    o_ref[...] = (y * _sigmoid(y) * sc_ref[...]).astype(o_ref.dtype)


def ml_prep(u, conv_w, conv_b, col_scale, tc=256):
    b, s, _ = u.shape
    w = conv_w.shape[1]
    return pl.pallas_call(
        _ml_prep_body, grid=(b, w // tc),
        in_specs=[pl.BlockSpec((None, s, tc), lambda bi, c: (bi, 0, c)),
                  pl.BlockSpec((3, tc), lambda bi, c: (0, c)),
                  pl.BlockSpec((1, tc), lambda bi, c: (0, c)),
                  pl.BlockSpec((1, tc), lambda bi, c: (0, c))],
        out_specs=pl.BlockSpec((None, s, tc), lambda bi, c: (bi, 0, c)),
        out_shape=jax.ShapeDtypeStruct((b, s, w), BF16),
        compiler_params=_params("parallel", "parallel"), name="ml_prep",
    )(u, conv_w, conv_b.reshape(1, w), col_scale.reshape(1, w))


def _log_sigmoid(x):
    return jnp.minimum(x, 0.0) - jnp.log(1.0 + jnp.exp(-jnp.abs(x)))


def _dot_split(a, b, a_is_f32):
    x = a if a_is_f32 else b
    hi = x.astype(BF16)
    lo = (x - hi.astype(F32)).astype(BF16)
    if a_is_f32:
        return (jnp.dot(hi, b, preferred_element_type=F32) + jnp.dot(lo, b, preferred_element_type=F32))
    return (jnp.dot(a, hi, preferred_element_type=F32) + jnp.dot(a, lo, preferred_element_type=F32))


def _mlstm_chain(q, k, v, bc, br, li_r, li_c, total, mask, c_ref, n_ref, m_ref, idx):
    c_st = c_ref[idx]
    n_st = n_ref[idx:idx + 1, :]
    m_st = m_ref[idx:idx + 1, 0:1]
    dmat = jnp.where(mask, bc - br + li_r, NEG_BIG)
    inter = bc + m_st
    m_t = jnp.maximum(inter, jnp.max(dmat, axis=-1, keepdims=True))
    w_intra = jnp.exp(dmat - m_t)
    w_inter = jnp.exp(inter - m_t)
    sc = lax.dot_general(q, k, (((1,), (1,)), ((), ())), preferred_element_type=F32) * w_intra
    num = (w_inter * jnp.dot(q, c_st.astype(BF16), preferred_element_type=F32)
           + jnp.dot(sc.astype(BF16), v, preferred_element_type=F32))
    den = (w_inter * jnp.sum(q.astype(F32) * n_st, axis=-1, keepdims=True)
           + jnp.sum(sc, axis=-1, keepdims=True))
    h = num / jnp.maximum(jnp.abs(den), jnp.exp(-m_t))
    g_s = total - bc + li_c
    m_next = jnp.maximum(total + m_st, jnp.max(g_s, axis=0, keepdims=True))
    a_prev = jnp.exp(total + m_st - m_next)
    kw = k.astype(F32) * jnp.exp(g_s - m_next)
    c_ref[idx] = a_prev * c_st + lax.dot_general(kw.astype(BF16), v, (((0,), (0,)), ((), ())),
                                                 preferred_element_type=F32)
    n_ref[idx:idx + 1, :] = a_prev * n_st + jnp.sum(kw, axis=0, keepdims=True)
    m_ref[idx:idx + 1, :] = jnp.broadcast_to(m_next, (1, m_ref.shape[1]))
    return h


def _mlstm_body(qkf_ref, vf_ref, gcf_ref, grf_ref, qkb_ref, vb_ref, gcb_ref, grb_ref,
                hf_ref, hb_ref, c_ref, n_ref, m_ref, *, heads, dk, dv):
    @pl.when(pl.program_id(1) == 0)
    def _():
        c_ref[...] = jnp.zeros_like(c_ref)
        n_ref[...] = jnp.zeros_like(n_ref)
        m_ref[...] = jnp.zeros_like(m_ref)

    lc = qkf_ref.shape[0]
    t_i = lax.broadcasted_iota(I32, (lc, lc), 0)
    s_i = lax.broadcasted_iota(I32, (lc, lc), 1)
    lower = s_i <= t_i
    upper = s_i >= t_i
    ltri = jnp.where(lower, 1.0, 0.0).astype(BF16)
    utri = jnp.where(upper, 1.0, 0.0).astype(BF16)

    for direction, (qk_ref, v_ref, gc_ref, gr_ref, h_ref) in enumerate(
            ((qkf_ref, vf_ref, gcf_ref, grf_ref, hf_ref), (qkb_ref, vb_ref, gcb_ref, grb_ref, hb_ref))):
        fwd = direction == 0
        gc = gc_ref[...]
        gr = gr_ref[...]
        lfc, lfr = _log_sigmoid(gc), _log_sigmoid(gr)
        cum_c = _dot_split(ltri if fwd else utri, lfc, a_is_f32=False)
        cum_r = _dot_split(lfr, utri if fwd else ltri, a_is_f32=True)
        for hd in range(heads):
            gi = (0 if fwd else 2) * heads + hd
            gf = (1 if fwd else 3) * heads + hd
            bc, br = cum_c[:, gf:gf + 1], cum_r[gf:gf + 1, :]
            total = br[:, lc - 1:lc] if fwd else br[:, 0:1]
            q = qk_ref[:, hd * dk:(hd + 1) * dk]
            k = qk_ref[:, (heads + hd) * dk:(heads + hd + 1) * dk]
            v = v_ref[:, hd * dv:(hd + 1) * dv]
            h = _mlstm_chain(q, k, v, bc, br, gr[gi:gi + 1, :], gc[:, gi:gi + 1], total,
                             lower if fwd else upper, c_ref, n_ref, m_ref, direction * heads + hd)
            h_ref[:, hd * dv:(hd + 1) * dv] = h.astype(h_ref.dtype)


def mlstm_scan(qk, u, v_blk, gcol, grow):
    b, s, w = qk.shape
    heads = ML_HEADS
    dk = w // (2 * heads)
    dv = 2 * dk
    lc = ML_CHUNK
    nc = s // lc
    ng = gcol.shape[-1]

    def fw(bi, j):
        return j

    def bw(bi, j):
        return nc - 1 - j

    def specs(pos):
        return [pl.BlockSpec((None, lc, w), lambda bi, j: (bi, pos(bi, j), 0)),
                pl.BlockSpec((None, lc, heads * dv), lambda bi, j: (bi, pos(bi, j), v_blk)),
                pl.BlockSpec((None, lc, ng), lambda bi, j: (bi, pos(bi, j), 0)),
                pl.BlockSpec((None, ng, lc), lambda bi, j: (bi, 0, pos(bi, j)))]

    hshape = jax.ShapeDtypeStruct((b, s, heads * dv), BF16)
    return pl.pallas_call(
        functools.partial(_mlstm_body, heads=heads, dk=dk, dv=dv), grid=(b, nc),
        in_specs=specs(fw) + specs(bw),
        out_specs=[pl.BlockSpec((None, lc, heads * dv), lambda bi, j: (bi, j, 0)),
                   pl.BlockSpec((None, lc, heads * dv), lambda bi, j: (bi, nc - 1 - j, 0))],
        out_shape=[hshape, hshape],
        scratch_shapes=[pltpu.VMEM((2 * heads, dk, dv), F32), pltpu.VMEM((2 * heads, dk), F32),
                        pltpu.VMEM((2 * heads, 128), F32)],
        compiler_params=_params("parallel", "arbitrary"), name="mlstm_scan",
    )(qk, u, gcol, grow, qk, u, gcol, grow)


def _mlstm_out_body(res_ref, hf_ref, hb_ref, o_ref, g_ref, w_ref, out_ref, *, heads):
    hs = hf_ref[...].astype(F32) + hb_ref[...].astype(F32)
    dv = hs.shape[1] // heads
    g = g_ref[...]
    parts = []
    for hd in range(heads):
        seg = hs[:, hd * dv:(hd + 1) * dv]
        ms = jnp.mean(seg * seg, axis=-1, keepdims=True)
        parts.append(seg * lax.rsqrt(ms + EPS) * g[:, hd * dv:(hd + 1) * dv])
    a = jnp.concatenate(parts, axis=-1) * _sigmoid(o_ref[...].astype(F32))
    out_ref[...] = res_ref[...] + jnp.dot(a.astype(BF16), w_ref[...], preferred_element_type=F32)


def mlstm_out(res, hf, hb, u2d, o_blk, gain, w_out, tm=512):
    t, d = res.shape
    row = lambda i: (i, 0)
    return pl.pallas_call(
        functools.partial(_mlstm_out_body, heads=ML_HEADS), grid=(t // tm,),
        in_specs=[pl.BlockSpec((tm, d), row), pl.BlockSpec((tm, d), row), pl.BlockSpec((tm, d), row),
                  pl.BlockSpec((tm, d), lambda i: (i, o_blk)), pl.BlockSpec((1, d), lambda i: (0, 0)),
                  pl.BlockSpec((d, d), lambda i: (0, 0))],
        out_specs=pl.BlockSpec((tm, d), row), out_shape=jax.ShapeDtypeStruct((t, d), F32),
        compiler_params=_params("parallel"), name="mlstm_out",
    )(res, hf, hb, u2d, gain.reshape(1, d), w_out)


ROUTE_ROWS = 128
EXPERT_ROW0 = 8


def _router_body(x_ref, g_ref, wt_ref, b_ref, o_ref, cnt_ref, run_ref):
    @pl.when(pl.program_id(0) == 0)
    def _():
        run_ref[...] = jnp.zeros_like(run_ref)

    x = x_ref[...]
    ms = jnp.mean(x * x, axis=-1, keepdims=True)
    xn = x * lax.rsqrt(ms + EPS) * g_ref[...]
    logit = lax.dot_general(wt_ref[...], xn, (((1,), (1,)), ((), ())), preferred_element_type=F32,
                            precision=lax.Precision.HIGHEST) + b_ref[...]
    rows = [logit[r:r + 1, :] for r in range(EXPERT_ROW0 + N_EXPERTS)]
    g_best, g_idx = rows[0], jnp.zeros_like(rows[0])
    for gi in range(1, N_GROUPS):
        better = rows[gi] > g_best
        g_best = jnp.where(better, rows[gi], g_best)
        g_idx = jnp.where(better, float(gi), g_idx)
    g_den = sum(jnp.exp(rows[gi] - g_best) for gi in range(N_GROUPS))
    g_w = 1.0 / g_den
    sel = []
    for e in range(EXPERTS_PER_GROUP):
        v = rows[EXPERT_ROW0 + e]
        for gi in range(1, N_GROUPS):
            v = jnp.where(g_idx == float(gi), rows[EXPERT_ROW0 + gi * EXPERTS_PER_GROUP + e], v)
        sel.append(v)
    v1, i1 = sel[0], jnp.zeros_like(sel[0])
    for e in range(1, EXPERTS_PER_GROUP):
        better = sel[e] > v1
        v1 = jnp.where(better, sel[e], v1)
        i1 = jnp.where(better, float(e), i1)
    v2, i2 = jnp.full_like(v1, -jnp.inf), jnp.zeros_like(v1)
    for e in range(EXPERTS_PER_GROUP):
        better = (sel[e] > v2) & (i1 != float(e))
        v2 = jnp.where(better, sel[e], v2)
        i2 = jnp.where(better, float(e), i2)
    e21 = jnp.exp(v2 - v1)
    gate1 = g_w / (1.0 + e21)
    gate2 = gate1 * e21
    base = g_idx * float(EXPERTS_PER_GROUP)
    e1, e2 = base + i1, base + i2
    tm = e1.shape[1]
    erow = lax.broadcasted_iota(I32, (N_EXPERTS, tm), 0).astype(F32)
    oh1 = jnp.where(erow == e1, 1.0, 0.0)
    oh2 = jnp.where(erow == e2, 1.0, 0.0)
    cnt = oh1 + oh2
    earlier = jnp.where(lax.broadcasted_iota(I32, (tm, tm), 0) < lax.broadcasted_iota(I32, (tm, tm), 1),
                        1.0, 0.0).astype(BF16)
    run = run_ref[...]
    pos = run[:, 0:1] + jnp.dot(cnt.astype(BF16), earlier, preferred_element_type=F32)
    rank1 = jnp.sum(oh1 * pos, axis=0, keepdims=True)
    rank2 = jnp.sum(oh2 * pos, axis=0, keepdims=True)
    run = run + jnp.sum(cnt, axis=1, keepdims=True)
    run_ref[...] = run
    cnt_ref[...] = run
    zero = jnp.zeros_like(v1)
    o_ref[...] = jnp.concatenate([e1, e2, gate1, gate2, rank1, rank2, zero, zero], axis=0)


def moe_router(x2d, g, wg, bg, we, be, tm=512):
    t, d = x2d.shape
    wt = jnp.zeros((ROUTE_ROWS, d), F32).at[:N_GROUPS].set(wg.T).at[EXPERT_ROW0:EXPERT_ROW0 + N_EXPERTS].set(we.T)
    bias = jnp.zeros((ROUTE_ROWS, 1), F32).at[:N_GROUPS, 0].set(bg).at[EXPERT_ROW0:EXPERT_ROW0 + N_EXPERTS, 0].set(be)
    return pl.pallas_call(
        _router_body, grid=(t // tm,),
        in_specs=[pl.BlockSpec((tm, d), lambda i: (i, 0)), pl.BlockSpec((1, d), lambda i: (0, 0)),
                  pl.BlockSpec((ROUTE_ROWS, d), lambda i: (0, 0)), pl.BlockSpec((ROUTE_ROWS, 1), lambda i: (0, 0))],
        out_specs=[pl.BlockSpec((8, tm), lambda i: (0, i)), pl.BlockSpec((N_EXPERTS, 128), lambda i: (0, 0))],
        out_shape=[jax.ShapeDtypeStruct((8, t), F32), jax.ShapeDtypeStruct((N_EXPERTS, 128), F32)],
        scratch_shapes=[pltpu.VMEM((N_EXPERTS, 128), F32)],
        compiler_params=_params("arbitrary"), name="moe_router",
    )(x2d, g.reshape(1, d), wt, bias)


def _dispatch_body(dest_ref, zblk_ref, x_ref, g_ref, xs_ref, buf_ref, sem_ref, *, td):
    i = pl.program_id(0)
    n = pl.num_programs(0)
    slot = i % 2

    @pl.when(i == 0)
    def _():
        buf_ref[1] = jnp.zeros(buf_ref.shape[1:], buf_ref.dtype)

        def zero_copy(j):
            return pltpu.make_async_copy(buf_ref.at[1], xs_ref.at[pl.ds(zblk_ref[j] * td, td), :], sem_ref.at[1])

        def start(j, c):
            @pl.when(zblk_ref[j] >= 0)
            def _():
                zero_copy(j).start()
            return c

        def wait(j, c):
            @pl.when(zblk_ref[j] >= 0)
            def _():
                zero_copy(j).wait()
            return c
        lax.fori_loop(0, zblk_ref.shape[0], start, 0)
        lax.fori_loop(0, zblk_ref.shape[0], wait, 0)

    def row_copy(r, kk):
        return pltpu.make_async_copy(buf_ref.at[slot, pl.ds(r, 1), :],
                                     xs_ref.at[pl.ds(dest_ref[(i * td + r) * 2 + kk], 1), :],
                                     sem_ref.at[slot])

    def wait_buffer(sl):
        for _ in range(2):
            pltpu.make_async_copy(buf_ref.at[sl], xs_ref.at[pl.ds(0, td), :], sem_ref.at[sl]).wait()

    @pl.when(i >= 2)
    def _():
        wait_buffer(slot)

    x = x_ref[...]
    ms = jnp.mean(x * x, axis=-1, keepdims=True)
    buf_ref[slot] = x * lax.rsqrt(ms + EPS) * g_ref[...]

    def issue(r, c):
        row_copy(r, 0).start()
        row_copy(r, 1).start()
        return c
    lax.fori_loop(0, td, issue, 0, unroll=8)

    @pl.when(i == n - 1)
    def _():
        @pl.when(n >= 2)
        def _():
            wait_buffer(1 - slot)
        wait_buffer(slot)


def moe_dispatch(x2d, g, dest, zero_blk, cap, td):
    t, d = x2d.shape
    grid_spec = pltpu.PrefetchScalarGridSpec(
        num_scalar_prefetch=2, grid=(t // td,),
        in_specs=[pl.BlockSpec((td, d), lambda i, dest, zb: (i, 0)),
                  pl.BlockSpec((1, d), lambda i, dest, zb: (0, 0))],
        out_specs=pl.BlockSpec(memory_space=pl.ANY),
        scratch_shapes=[pltpu.VMEM((2, td, d), F32), pltpu.SemaphoreType.DMA((2,))])
    return pl.pallas_call(
        functools.partial(_dispatch_body, td=td), grid_spec=grid_spec,
        out_shape=jax.ShapeDtypeStruct((cap, d), F32),
        compiler_params=_params("arbitrary", disable_bounds_checks=True),
        name="moe_dispatch",
    )(dest, zero_blk, x2d, g.reshape(1, d))


def _expert_body(blk_e_ref, nused_ref, xs_ref, w1_ref, w3_ref, w2_ref, ys_ref, w1b_ref, w3b_ref, w2b_ref):
    i = pl.program_id(0)
    nused = nused_ref[0]
    last = nused - 1
    cur = blk_e_ref[jnp.minimum(i, last)]
    prev = blk_e_ref[jnp.minimum(jnp.maximum(i - 1, 0), last)]

    @pl.when((i == 0) | (cur != prev))
    def _():
        w1b_ref[...] = w1_ref[...].astype(BF16)
        w3b_ref[...] = w3_ref[...].astype(BF16)
        w2b_ref[...] = w2_ref[...].astype(BF16)

    @pl.when(i < nused)
    def _():
        x = xs_ref[...].astype(BF16)
        h1 = jnp.dot(x, w1b_ref[...], preferred_element_type=F32)
        h3 = jnp.dot(x, w3b_ref[...], preferred_element_type=F32)
        hid = (h1 * _sigmoid(h1) * h3).astype(BF16)
        ys_ref[...] = jnp.dot(hid, w2b_ref[...], preferred_element_type=F32)

    @pl.when(i >= nused)
    def _():
        ys_ref[...] = jnp.zeros_like(ys_ref)


def moe_experts(xs, blk_e, nused, w1, w3, w2, layer, tm):
    cap, d = xs.shape
    de = w1.shape[-1]

    def blk(i, be, nu):
        return jnp.minimum(i, nu[0] - 1)

    grid_spec = pltpu.PrefetchScalarGridSpec(
        num_scalar_prefetch=2, grid=(cap // tm,),
        in_specs=[pl.BlockSpec((tm, d), lambda i, be, nu: (blk(i, be, nu), 0)),
                  pl.BlockSpec((None, None, d, de), lambda i, be, nu: (layer, be[blk(i, be, nu)], 0, 0)),
                  pl.BlockSpec((None, None, d, de), lambda i, be, nu: (layer, be[blk(i, be, nu)], 0, 0)),
                  pl.BlockSpec((None, None, de, d), lambda i, be, nu: (layer, be[blk(i, be, nu)], 0, 0))],
        out_specs=pl.BlockSpec((tm, d), lambda i, be, nu: (i, 0)),
        scratch_shapes=[pltpu.VMEM((d, de), BF16), pltpu.VMEM((d, de), BF16), pltpu.VMEM((de, d), BF16)])
    return pl.pallas_call(
        _expert_body, grid_spec=grid_spec, out_shape=jax.ShapeDtypeStruct((cap, d), F32),
        compiler_params=_params("arbitrary"), name="moe_experts",
    )(blk_e, nused, xs, w1, w3, w2)


def _combine_body(dest_ref, x_ref, gate_ref, ys_ref, o_ref, buf_ref, sem_ref, *, tc):
    i = pl.program_id(0)
    n = pl.num_programs(0)
    slot = i % 2

    def row_copy(step, sl, r, kk):
        return pltpu.make_async_copy(ys_ref.at[pl.ds(dest_ref[(step * tc + r) * 2 + kk], 1), :],
                                     buf_ref.at[sl, kk, pl.ds(r, 1), :], sem_ref.at[sl])

    def issue_step(step, sl):
        def body(r, c):
            row_copy(step, sl, r, 0).start()
            row_copy(step, sl, r, 1).start()
            return c
        lax.fori_loop(0, tc, body, 0, unroll=8)

    @pl.when(i == 0)
    def _():
        issue_step(0, 0)

    @pl.when(i + 1 < n)
    def _():
        issue_step(i + 1, 1 - slot)

    for kk in range(2):
        pltpu.make_async_copy(ys_ref.at[pl.ds(0, tc), :], buf_ref.at[slot, kk], sem_ref.at[slot]).wait()

    gate = gate_ref[...]
    o_ref[...] = x_ref[...] + gate[:, 0:1] * buf_ref[slot, 0] + gate[:, 1:2] * buf_ref[slot, 1]


def moe_combine(x2d, gates, ys, dest, tc=256):
    t, d = x2d.shape
    grid_spec = pltpu.PrefetchScalarGridSpec(
        num_scalar_prefetch=1, grid=(t // tc,),
        in_specs=[pl.BlockSpec((tc, d), lambda i, dest: (i, 0)), pl.BlockSpec((tc, 2), lambda i, dest: (i, 0)),
                  pl.BlockSpec(memory_space=pl.ANY)],
        out_specs=pl.BlockSpec((tc, d), lambda i, dest: (i, 0)),
        scratch_shapes=[pltpu.VMEM((2, 2, tc, d), F32), pltpu.SemaphoreType.DMA((2,))])
    return pl.pallas_call(
        functools.partial(_combine_body, tc=tc), grid_spec=grid_spec,
        out_shape=jax.ShapeDtypeStruct((t, d), F32),
        compiler_params=_params("arbitrary", disable_bounds_checks=True), name="moe_combine",
    )(dest, x2d, gates, ys)


def hier_moe_residual(x2d, g, wg, bg, we, be, w1, w3, w2, layer, tm=256):
    t, d = x2d.shape
    route, counts_b = moe_router(x2d, g, wg, bg, we, be)
    eid = route[0:2].T.astype(I32).reshape(-1)
    rank = route[4:6].T.astype(I32).reshape(-1)
    gates = route[2:4].T
    counts = counts_b[:, 0].astype(I32)
    padded = (counts + tm - 1) // tm * tm
    pad_end = jnp.cumsum(padded)
    pad_start = pad_end - padded
    experts = jnp.arange(N_EXPERTS, dtype=I32)
    dest = rank + jnp.sum(jnp.where(eid[:, None] == experts[None, :], pad_start[None, :], 0), axis=1)
    cap = t * 2 + N_EXPERTS * tm
    nblk = cap // tm
    blk_row0 = jnp.arange(nblk, dtype=I32) * tm
    blk_e = jnp.minimum(jnp.sum((pad_end[None, :] <= blk_row0[:, None]).astype(I32), axis=1), N_EXPERTS - 1)
    nused = (pad_end[-1:] // tm).astype(I32)
    last_blk = jnp.where(padded > 0, pad_end // tm - 1, -1)
    tail_blk = jnp.where(nused[0] + experts < nblk, nused[0] + experts, -1)
    zero_blk = jnp.concatenate([last_blk, tail_blk]).astype(I32)
    xs = moe_dispatch(x2d, g, dest.astype(I32), zero_blk, cap, tm)
    ys = moe_experts(xs, blk_e, nused, w1, w3, w2, layer, tm)
    return moe_combine(x2d, gates, ys, dest.astype(I32))


def _even_layer(x2d, b, s, layer, mix_g, w_in, q_norm, k_norm, lam_q1, lam_k1, lam_q2, lam_k2, subln,
                hy_conv_w, hy_conv_b, f_w1, f_b1, f_freq, f_w2, f_b2, f_w3, hy_skip, w_out):
    d = x2d.shape[1]
    d_att = d // 2
    (u2d,) = norm_matmul(x2d, mix_g, [w_in.astype(BF16)], [BF16])
    u = u2d.reshape(b, s, -1)
    lambda_init = 0.8 - 0.6 * math.exp(-0.3 * layer)
    lam = jnp.exp(jnp.sum(lam_q1 * lam_k1)) - jnp.exp(jnp.sum(lam_q2 * lam_k2)) + lambda_init
    dk = q_norm.shape[0]
    qt = head_prep(u, 0, q_norm, scale=dk ** -0.5 * math.log2(math.e), transpose=True)
    kp = head_prep(u, ATT_HEADS, k_norm)
    vt = head_prep(u, 2 * ATT_HEADS, transpose=True)
    y_att = diff_attention(qt, kp, vt, lam, subln, lambda_init)
    z, x1c, zp = hy_prep(u, 3 * d_att, hy_conv_w, hy_conv_b)
    tables = fft_tables(s)
    kf = filter_spectrum(hyena_filter(s, f_w1, f_b1, f_freq, f_w2, f_b2, f_w3), tables[0], tables[1])
    y_hy = hy_fft_conv(zp, z, x1c, kf, tables, hy_skip)
    w_out = w_out.astype(BF16)
    return matmul_residual(x2d, [y_att.reshape(b * s, -1), y_hy.reshape(b * s, -1)],
                           [w_out[:d_att], w_out[d_att:]])


def _odd_layer(x2d, b, s, mix_g, w_in, conv_w, conv_b, gate_b, out_norm, w_out):
    d = x2d.shape[1]
    qk_w = conv_w.shape[1]
    main_w = qk_w + 2 * d
    ng = 4 * ML_HEADS
    w_gate = jnp.zeros((d, 128), F32).at[:, :ng].set(w_in[:, main_w:]).astype(BF16)
    u2d, ug = norm_matmul(x2d, mix_g, [w_in[:, :main_w].astype(BF16), w_gate], [BF16, F32])
    u = u2d.reshape(b, s, main_w)
    gcol = (ug[:, :ng] + gate_b).reshape(b, s, ng)
    grow = jnp.swapaxes(gcol, 1, 2)
    dk = qk_w // (2 * ML_HEADS)
    col_scale = jnp.concatenate([jnp.full((qk_w // 2,), dk ** -0.5, F32), jnp.ones((qk_w // 2,), F32)])
    qk = ml_prep(u, conv_w, conv_b, col_scale)
    hf, hb = mlstm_scan(qk, u, qk_w // d, gcol, grow)
    return mlstm_out(x2d, hf.reshape(b * s, d), hb.reshape(b * s, d), u2d, (qk_w + d) // d, out_norm,
                     w_out.astype(BF16))


def kernel(x, mix_norm, ffn_norm, ev_w_in, ev_q_norm, ev_k_norm, ev_lam_q1, ev_lam_k1, ev_lam_q2, ev_lam_k2, ev_subln, ev_hy_conv_w, ev_hy_conv_b, ev_hy_f_w1, ev_hy_f_b1, ev_hy_f_freq, ev_hy_f_w2, ev_hy_f_b2, ev_hy_f_w3, ev_hy_skip, ev_w_out, od_w_in, od_conv_w, od_conv_b, od_gate_b, od_out_norm, od_w_out, moe_wg, moe_bg, moe_we, moe_be, moe_w1, moe_w3, moe_w2):
    b, s, d = x.shape
    depth = mix_norm.shape[0]
    x2d = x.reshape(b * s, d)
    for layer in range(depth):
        j = layer // 2
        if layer % 2 == 0:
            x2d = _even_layer(x2d, b, s, layer, mix_norm[layer], ev_w_in[j], ev_q_norm[j], ev_k_norm[j],
                              ev_lam_q1[j], ev_lam_k1[j], ev_lam_q2[j], ev_lam_k2[j], ev_subln[j],
                              ev_hy_conv_w[j], ev_hy_conv_b[j], ev_hy_f_w1[j], ev_hy_f_b1[j], ev_hy_f_freq[j],
                              ev_hy_f_w2[j], ev_hy_f_b2[j], ev_hy_f_w3[j], ev_hy_skip[j], ev_w_out[j])
        else:
            x2d = _odd_layer(x2d, b, s, mix_norm[layer], od_w_in[j], od_conv_w[j], od_conv_b[j], od_gate_b[j],
                             od_out_norm[j], od_w_out[j])
        x2d = hier_moe_residual(x2d, ffn_norm[layer], moe_wg[layer], moe_bg[layer], moe_we[layer],
                                moe_be[layer], moe_w1, moe_w3, moe_w2, layer)
    return x2d.reshape(b, s, d)
```

```python
import functools
import math

import jax
import jax.numpy as jnp
from jax import lax
from jax.experimental import pallas as pl
from jax.experimental.pallas import tpu as pltpu

F32 = jnp.float32
BF16 = jnp.bfloat16
I32 = jnp.int32

EPS = 1e-6
ROPE_THETA = 500000.0
ATT_HEADS = 4
ML_HEADS = 4
ML_CHUNK = 128
ML_BATCH_PER_STEP = 2
N_GROUPS = 4
EXPERTS_PER_GROUP = 8
N_EXPERTS = N_GROUPS * EXPERTS_PER_GROUP
HY_EMB_BANDS = 16
HY_MIN_DECAY = math.log(1e-2) / 1.5
HY_MAX_DECAY = math.log(1e-2) / 0.3

V7X_VMEM_BYTES = 64 * 1024 * 1024
VMEM_LIMIT = V7X_VMEM_BYTES - 8 * 1024 * 1024
NEG_BIG = -1e30


def _params(*sem, **kw):
    return pltpu.CompilerParams(dimension_semantics=sem, vmem_limit_bytes=VMEM_LIMIT, **kw)


def _sigmoid(x):
    return 1.0 / (1.0 + jnp.exp(-x))


def _norm_matmul_body(x_ref, g_ref, *refs, n_out, col_chunk):
    w_refs, o_refs = refs[:n_out], refs[n_out:]
    x = x_ref[...]
    ms = jnp.mean(x * x, axis=-1, keepdims=True)
    hn = (x * lax.rsqrt(ms + EPS) * g_ref[...]).astype(BF16)
    for w_ref, o_ref in zip(w_refs, o_refs):
        n = w_ref.shape[1]
        for c in range(0, n, col_chunk):
            ce = min(n, c + col_chunk)
            o_ref[:, c:ce] = jnp.dot(hn, w_ref[:, c:ce], preferred_element_type=F32).astype(o_ref.dtype)


def norm_matmul(x2d, g, ws, out_dtypes, tm=512):
    t, d = x2d.shape
    in_specs = [pl.BlockSpec((tm, d), lambda i: (i, 0)), pl.BlockSpec((1, d), lambda i: (0, 0))]
    in_specs += [pl.BlockSpec(w.shape, lambda i: (0, 0)) for w in ws]
    out_specs = [pl.BlockSpec((tm, w.shape[1]), lambda i: (i, 0)) for w in ws]
    out_shape = [jax.ShapeDtypeStruct((t, w.shape[1]), dt) for w, dt in zip(ws, out_dtypes)]
    return pl.pallas_call(
        functools.partial(_norm_matmul_body, n_out=len(ws), col_chunk=1024),
        grid=(t // tm,), in_specs=in_specs, out_specs=out_specs, out_shape=out_shape,
        compiler_params=_params("parallel"), name="norm_matmul",
    )(x2d, g.reshape(1, d), *ws)


def _matmul_res_body(res_ref, *refs, n_in):
    a_refs, w_refs, o_ref = refs[:n_in], refs[n_in:2 * n_in], refs[2 * n_in]
    acc = res_ref[...]
    for a_ref, w_ref in zip(a_refs, w_refs):
        acc = acc + jnp.dot(a_ref[...], w_ref[...], preferred_element_type=F32)
    o_ref[...] = acc


def matmul_residual(res, a_list, w_list, tm=512):
    t, d = res.shape
    in_specs = [pl.BlockSpec((tm, d), lambda i: (i, 0))]
    in_specs += [pl.BlockSpec((tm, a.shape[1]), lambda i: (i, 0)) for a in a_list]
    in_specs += [pl.BlockSpec(w.shape, lambda i: (0, 0)) for w in w_list]
    return pl.pallas_call(
        functools.partial(_matmul_res_body, n_in=len(a_list)),
        grid=(t // tm,), in_specs=in_specs, out_specs=pl.BlockSpec((tm, d), lambda i: (i, 0)),
        out_shape=jax.ShapeDtypeStruct((t, d), F32),
        compiler_params=_params("parallel"), name="matmul_residual",
    )(res, *a_list, *w_list)


def _head_prep_body(u_ref, *refs, dk, rope, transpose):
    o_ref = refs[-1]
    x = u_ref[...].astype(F32)
    if rope:
        g_ref, c_ref, s1_ref, s2_ref = refs[:4]
        lane = lax.broadcasted_iota(I32, x.shape, 1)
        lo = lane < dk
        x2 = x * x
        s_lo = jnp.sum(jnp.where(lo, x2, 0.0), axis=-1, keepdims=True)
        s_hi = jnp.sum(jnp.where(lo, 0.0, x2), axis=-1, keepdims=True)
        ms = jnp.where(lo, s_lo, s_hi) * (1.0 / dk)
        y = x * lax.rsqrt(ms + EPS) * g_ref[...]
        x = y * c_ref[...] + pltpu.roll(y, 120, 1) * s1_ref[...] + pltpu.roll(y, 8, 1) * s2_ref[...]
    if transpose:
        x = x.T
    o_ref[...] = x.astype(o_ref.dtype)


def _rope_lane_tables(seq, dk, rope_dim, scale):
    half = rope_dim // 2
    inv_freq = 1.0 / (ROPE_THETA ** (jnp.arange(0, rope_dim, 2, dtype=F32) / rope_dim))
    ang = jnp.arange(seq, dtype=F32)[:, None] * inv_freq[None, :]
    cos, sin = jnp.cos(ang), jnp.sin(ang)
    d = jnp.arange(2 * dk) % dk
    fi = d % half
    c_tab = jnp.where(d[None, :] < rope_dim, cos[:, fi], 1.0)
    s1_tab = jnp.where(d[None, :] < half, -sin[:, fi], 0.0)
    s2_tab = jnp.where((d[None, :] >= half) & (d[None, :] < rope_dim), sin[:, fi], 0.0)
    return (jnp.stack([c_tab, s1_tab, s2_tab]) * scale).astype(F32)


def head_prep(u, blk0, norm_gain=None, scale=1.0, transpose=False):
    b, s, _ = u.shape
    h = ATT_HEADS
    rope = norm_gain is not None
    in_specs = [pl.BlockSpec((None, s, 128), lambda c, bi: (bi, 0, blk0 + c))]
    args = [u]
    dk = 64
    if rope:
        dk = norm_gain.shape[0]
        assert 2 * dk == 128 and dk // 4 == 16, "rope roll shifts assume 64-wide components, 16 rotary dims"
        tabs = _rope_lane_tables(s, dk, dk // 4, scale)
        in_specs += [pl.BlockSpec((1, 128), lambda c, bi: (0, 0))] + [pl.BlockSpec((s, 128), lambda c, bi: (0, 0))] * 3
        args += [jnp.tile(norm_gain, 2).reshape(1, 128).astype(F32), tabs[0], tabs[1], tabs[2]]
    if transpose:
        out_spec = pl.BlockSpec((None, 128, s), lambda c, bi: (bi, c, 0))
        out_shape = jax.ShapeDtypeStruct((b, h * 128, s), BF16)
    else:
        out_spec = pl.BlockSpec((None, s, 128), lambda c, bi: (bi, 0, c))
        out_shape = jax.ShapeDtypeStruct((b, s, h * 128), BF16)
    return pl.pallas_call(
        functools.partial(_head_prep_body, dk=dk, rope=rope, transpose=transpose), grid=(h, b),
        in_specs=in_specs, out_specs=out_spec, out_shape=out_shape,
        compiler_params=_params("parallel", "parallel"), name="head_prep",
    )(*args)


def _attn_body(lam_ref, qt_ref, k_ref, vt_ref, g_ref, o_ref, *, tq, dk, post_scale, n_split):
    lam = lam_ref[0, 0]
    th = tq // n_split
    for part in range(n_split):
        qt = qt_ref[:, part * th:(part + 1) * th]
        row = lax.broadcasted_iota(I32, qt.shape, 0)
        zero = jnp.zeros_like(qt)
        qq = jnp.concatenate([jnp.where(row < dk, qt, zero), jnp.where(row < dk, zero, qt)], axis=1)
        st = jnp.dot(k_ref[...], qq, preferred_element_type=F32)
        m = jnp.max(st, axis=0, keepdims=True)
        p = jnp.exp2(st - m)
        r = 1.0 / jnp.sum(p, axis=0, keepdims=True)
        ot = jnp.dot(vt_ref[...], p.astype(BF16), preferred_element_type=F32)
        o = (ot[:, :th] * r[:, :th] - ot[:, th:] * (lam * r[:, th:])).T
        ms = jnp.mean(o * o, axis=-1, keepdims=True)
        o_ref[part * th:(part + 1) * th, :] = (o * lax.rsqrt(ms + EPS) * g_ref[...] * post_scale).astype(o_ref.dtype)


def diff_attention(qt, k, vt, lam, subln, lambda_init, tq=256):
    b, s, _ = k.shape
    h = ATT_HEADS
    return pl.pallas_call(
        functools.partial(_attn_body, tq=tq, dk=64, post_scale=1.0 - lambda_init, n_split=1),
        grid=(b, h, s // tq),
        in_specs=[pl.BlockSpec(memory_space=pltpu.SMEM),
                  pl.BlockSpec((None, 128, tq), lambda bi, hi, i: (bi, hi, i)),
                  pl.BlockSpec((None, s, 128), lambda bi, hi, i: (bi, 0, hi)),
                  pl.BlockSpec((None, 128, s), lambda bi, hi, i: (bi, hi, 0)),
                  pl.BlockSpec((1, 128), lambda bi, hi, i: (0, 0))],
        out_specs=pl.BlockSpec((None, tq, 128), lambda bi, hi, i: (bi, i, hi)),
        out_shape=jax.ShapeDtypeStruct((b, s, h * 128), BF16),
        compiler_params=_params("parallel", "parallel", "parallel"), name="diff_attention",
    )(lam.reshape(1, 1).astype(F32), qt, k, vt, subln.reshape(1, 128).astype(F32))


def _conv3(u_ref, w_ref, b_ref):
    x = u_ref[...].astype(F32)
    s = x.shape[0]
    row = lax.broadcasted_iota(I32, x.shape, 0)
    x_prev = jnp.where(row == 0, 0.0, pltpu.roll(x, 1, 0))
    x_next = jnp.where(row == s - 1, 0.0, pltpu.roll(x, s - 1, 0))
    w = w_ref[...]
    return b_ref[...] + x_prev * w[0:1] + x * w[1:2] + x_next * w[2:3]


FFT_N1 = 64
FFT_UNROLL = 8
FFT_PAD = 8


def _hy_prep_body(x1_ref, x2_ref, v_ref, w1_ref, w2_ref, wv_ref, b1_ref, b2_ref, bv_ref, z_ref, x1c_ref, zp_ref):
    x1c_ref[...] = _conv3(x1_ref, w1_ref, b1_ref).astype(x1c_ref.dtype)
    z = _conv3(v_ref, wv_ref, bv_ref) * _conv3(x2_ref, w2_ref, b2_ref)
    z_ref[...] = z.astype(z_ref.dtype)
    nb = z.shape[0] // FFT_N1
    zp_ref[...] = jnp.zeros_like(zp_ref)
    for n2 in range(nb):
        for ci in range(z.shape[1] // 128):
            zp_ref[ci, pl.ds(n2, FFT_N1, stride=nb + FFT_PAD), :] = (
                z[n2 * FFT_N1:(n2 + 1) * FFT_N1, ci * 128:(ci + 1) * 128])


def hy_prep(u, col0, conv_w, conv_b, tc=256):
    b, s, _ = u.shape
    d_hy = conv_w.shape[1] // 3
    nct = d_hy // tc
    blk0 = col0 // tc
    sp = FFT_N1 * (s // FFT_N1 + FFT_PAD)

    def uspec(part):
        return pl.BlockSpec((None, s, tc), lambda bi, c: (bi, 0, blk0 + part * nct + c))

    def wspec(part, rows):
        return pl.BlockSpec((rows, tc), lambda bi, c: (0, part * nct + c))

    ospec = pl.BlockSpec((None, s, tc), lambda bi, c: (bi, 0, c))
    return pl.pallas_call(
        _hy_prep_body, grid=(b, nct),
        in_specs=[uspec(0), uspec(1), uspec(2), wspec(0, 3), wspec(1, 3), wspec(2, 3),
                  wspec(0, 1), wspec(1, 1), wspec(2, 1)],
        out_specs=[ospec, ospec, pl.BlockSpec((None, tc // 128, sp, 128), lambda bi, c: (bi, c, 0, 0))],
        out_shape=[jax.ShapeDtypeStruct((b, s, d_hy), BF16), jax.ShapeDtypeStruct((b, s, d_hy), BF16),
                   jax.ShapeDtypeStruct((b, d_hy // 128, sp, 128), F32)],
        compiler_params=_params("parallel", "parallel"), name="hy_prep",
    )(u, u, u, conv_w, conv_w, conv_w, conv_b.reshape(1, -1), conv_b.reshape(1, -1), conv_b.reshape(1, -1))


def hyena_filter(length, w1, b1, freq, w2, b2, w3):
    d_hy = w3.shape[1] // 2
    t = jnp.linspace(0.0, 1.0, length, dtype=F32)[:, None]
    bands = jnp.linspace(1e-4, HY_EMB_BANDS - 1, HY_EMB_BANDS, dtype=F32)[None, :]
    ang = (2.0 * math.pi / length) * jnp.arange(length, dtype=F32)[:, None] * bands
    z = jnp.concatenate([t, jnp.cos(ang), -jnp.sin(ang)], axis=-1)
    hp = lax.Precision.HIGHEST
    hdn = jnp.sin(freq * (jnp.dot(z, w1, precision=hp) + b1))
    hdn = jnp.sin(freq * (jnp.dot(hdn, w2, precision=hp) + b2))
    filt = jnp.dot(hdn, w3, precision=hp)
    deltas = jnp.abs(jnp.linspace(HY_MIN_DECAY, HY_MAX_DECAY, d_hy, dtype=F32))
    decay = jnp.exp(-t * deltas[None, :])
    h_fwd = filt[:, :d_hy] * decay
    h_bwd = filt[:, d_hy:] * decay
    h_fwd = h_fwd.at[0].add(h_bwd[0])
    h_bwd = h_bwd.at[0].set(0.0)
    norm = jnp.sum(jnp.abs(h_fwd), axis=0, keepdims=True) + jnp.sum(jnp.abs(h_bwd), axis=0, keepdims=True) + EPS
    return jnp.concatenate([h_fwd / norm, h_bwd / norm], axis=1)


def fft_tables(length):
    n = 2 * length
    n1c, n2c, nb = FFT_N1, n // FFT_N1, length // FFT_N1
    unit = 2.0 * math.pi / n
    i1 = jnp.arange(n1c, dtype=I32)
    i2 = jnp.arange(n2c, dtype=I32)
    ib = jnp.arange(nb, dtype=I32)
    samp = i1[:, None, None] + n1c * ib[None, None, :]
    ang = ((i2[None, :, None] * samp) % n).astype(F32) * unit
    m1 = jnp.concatenate([jnp.cos(ang), -jnp.sin(ang)], axis=1)
    ang = ((i1[:, None] * i1[None, :]) % n1c).astype(F32) * (2.0 * math.pi / n1c)
    c, s = jnp.cos(ang), jnp.sin(ang)
    f1 = jnp.concatenate([jnp.concatenate([c, s], axis=1), jnp.concatenate([-s, c], axis=1)], axis=0)
    freq = n2c * i1[None, None, :] + i2[:, None, None]
    ang = ((i1[None, :, None] * freq) % n).astype(F32) * unit
    c, s = jnp.cos(ang), jnp.sin(ang)
    g1 = jnp.concatenate([jnp.concatenate([c, -s], axis=2), jnp.concatenate([s, c], axis=2)], axis=1)
    ang = ((ib[:, None] * i2[None, :]) % n2c).astype(F32) * (2.0 * math.pi / n2c)
    g2 = jnp.concatenate([jnp.cos(ang), -jnp.sin(ang)], axis=1) * (1.0 / n)
    return m1.astype(BF16), f1.astype(BF16), g1.astype(BF16), g2.astype(BF16)


def _fft_stage1(xp_ref, m1_ref, p_ref):
    nb = m1_ref.shape[2]
    n2c = m1_ref.shape[1] // 2

    def body(n1, c):
        x = xp_ref[pl.ds(pl.multiple_of(n1 * (nb + FFT_PAD), 8), nb), :].astype(BF16)
        a = jnp.dot(m1_ref[n1], x, preferred_element_type=F32)
        p_ref[0, pl.ds(n1, n2c, stride=FFT_N1 + FFT_PAD), :] = a[:n2c]
        p_ref[1, pl.ds(n1, n2c, stride=FFT_N1 + FFT_PAD), :] = a[n2c:]
        return c
    lax.fori_loop(0, FFT_N1, body, 0, unroll=FFT_UNROLL)


def _fft_stage2(p_ref, f1_ref, k2):
    r0 = pl.multiple_of(k2 * (FFT_N1 + FFT_PAD), 8)
    slab = jnp.concatenate([p_ref[0, pl.ds(r0, FFT_N1), :], p_ref[1, pl.ds(r0, FFT_N1), :]], axis=0)
    return jnp.dot(f1_ref[...], slab.astype(BF16), preferred_element_type=F32)


def _spectrum_body(xp_ref, m1_ref, f1_ref, o_ref, p_ref):
    _fft_stage1(xp_ref, m1_ref, p_ref)

    def body(k2, c):
        o_ref[k2] = _fft_stage2(p_ref, f1_ref, k2)
        return c
    lax.fori_loop(0, o_ref.shape[0], body, 0, unroll=FFT_UNROLL)


def filter_spectrum(ab, m1, f1):
    length, c2 = ab.shape
    nb = length // FFT_N1
    n2c = m1.shape[1] // 2
    nch = c2 // 128
    abp = jnp.pad(ab.reshape(nb, FFT_N1, nch, 128).transpose(2, 1, 0, 3), ((0, 0), (0, 0), (0, FFT_PAD), (0, 0)))
    abp = abp.reshape(nch, FFT_N1 * (nb + FFT_PAD), 128)
    spec = pl.pallas_call(
        _spectrum_body, grid=(nch,),
        in_specs=[pl.BlockSpec((None,) + abp.shape[1:], lambda c: (c, 0, 0)),
                  pl.BlockSpec(m1.shape, lambda c: (0, 0, 0)), pl.BlockSpec(f1.shape, lambda c: (0, 0))],
        out_specs=pl.BlockSpec((None, n2c, 2 * FFT_N1, 128), lambda c: (c, 0, 0, 0)),
        out_shape=jax.ShapeDtypeStruct((nch, n2c, 2 * FFT_N1, 128), F32),
        scratch_shapes=[pltpu.VMEM((2, n2c * (FFT_N1 + FFT_PAD), 128), F32)],
        compiler_params=_params("parallel"), name="filter_spectrum",
    )(abp, m1, f1)
    fa, fb = spec[:nch // 2], spec[nch // 2:]
    h = FFT_N1
    return jnp.concatenate([fa[:, :, :h] + fb[:, :, :h], fa[:, :, h:] - fb[:, :, h:]], axis=2).astype(BF16)


def _hy_fft_body(zp_ref, z_ref, x1c_ref, kf_ref, m1_ref, f1_ref, g1_ref, g2_ref, skip_ref, o_ref,
                 p_ref, q_ref, y_ref):
    h = FFT_N1
    n2c = g1_ref.shape[0]
    nb = g2_ref.shape[0]
    _fft_stage1(zp_ref, m1_ref, p_ref)

    def mid(k2, c):
        xf = _fft_stage2(p_ref, f1_ref, k2)
        kf = kf_ref[k2].astype(F32)
        xr, xi, kr, ki = xf[:h], xf[h:], kf[:h], kf[h:]
        y = jnp.concatenate([xr * kr - xi * ki, xr * ki + xi * kr], axis=0).astype(BF16)
        d = jnp.dot(g1_ref[k2], y, preferred_element_type=F32)
        q_ref[0, pl.ds(k2, h, stride=n2c + FFT_PAD), :] = d[:h]
        q_ref[1, pl.ds(k2, h, stride=n2c + FFT_PAD), :] = d[h:]
        return c
    lax.fori_loop(0, n2c, mid, 0, unroll=FFT_UNROLL)

    def last(t1, c):
        r0 = pl.multiple_of(t1 * (n2c + FFT_PAD), 8)
        slab = jnp.concatenate([q_ref[0, pl.ds(r0, n2c), :], q_ref[1, pl.ds(r0, n2c), :]], axis=0)
        y_ref[pl.ds(t1, nb, stride=h), :] = jnp.dot(g2_ref[...], slab.astype(BF16), preferred_element_type=F32)
        return c
    lax.fori_loop(0, h, last, 0, unroll=FFT_UNROLL)

    z = z_ref[...].astype(F32)
    o_ref[...] = ((y_ref[...] + z * skip_ref[...]) * x1c_ref[...].astype(F32)).astype(o_ref.dtype)


def hy_fft_conv(zp, z, x1c, kf, tables, skip):
    m1, f1, g1, g2 = tables
    b, s, c = z.shape
    nch = c // 128

    def const(shape):
        return pl.BlockSpec(shape, lambda ci, bi: (0,) * len(shape))

    nat = pl.BlockSpec((None, s, 128), lambda ci, bi: (bi, 0, ci))
    return pl.pallas_call(
        _hy_fft_body, grid=(nch, b),
        in_specs=[pl.BlockSpec((None, None) + zp.shape[2:], lambda ci, bi: (bi, ci, 0, 0)), nat, nat,
                  pl.BlockSpec((None,) + kf.shape[1:], lambda ci, bi: (ci, 0, 0, 0)),
                  const(m1.shape), const(f1.shape), const(g1.shape), const(g2.shape),
                  pl.BlockSpec((1, 128), lambda ci, bi: (0, ci))],
        out_specs=nat,
        out_shape=jax.ShapeDtypeStruct((b, s, c), BF16),
        scratch_shapes=[pltpu.VMEM((2, g1.shape[0] * (FFT_N1 + FFT_PAD), 128), F32),
                        pltpu.VMEM((2, FFT_N1 * (g1.shape[0] + FFT_PAD), 128), F32),
                        pltpu.VMEM((s, 128), F32)],
        compiler_params=_params("parallel", "parallel"), name="hy_fft_conv",
    )(zp, z, x1c, kf, m1, f1, g1, g2, skip.reshape(1, c).astype(F32))


def _ml_prep_body(u_ref, w_ref, b_ref, sc_ref, o_ref):
    y = _conv3(u_ref, w_ref, b_ref)
    o_ref[...] = (y * _sigmoid(y) * sc_ref[...]).astype(o_ref.dtype)


def ml_prep(u, conv_w, conv_b, col_scale, tc=256):
    b, s, _ = u.shape
    w = conv_w.shape[1]
    return pl.pallas_call(
        _ml_prep_body, grid=(b, w // tc),
        in_specs=[pl.BlockSpec((None, s, tc), lambda bi, c: (bi, 0, c)),
                  pl.BlockSpec((3, tc), lambda bi, c: (0, c)),
                  pl.BlockSpec((1, tc), lambda bi, c: (0, c)),
                  pl.BlockSpec((1, tc), lambda bi, c: (0, c))],
        out_specs=pl.BlockSpec((None, s, tc), lambda bi, c: (bi, 0, c)),
        out_shape=jax.ShapeDtypeStruct((b, s, w), BF16),
        compiler_params=_params("parallel", "parallel"), name="ml_prep",
    )(u, conv_w, conv_b.reshape(1, w), col_scale.reshape(1, w))


def _log_sigmoid(x):
    return jnp.minimum(x, 0.0) - jnp.log(1.0 + jnp.exp(-jnp.abs(x)))


def _dot_split(a, b, a_is_f32):
    x = a if a_is_f32 else b
    hi = x.astype(BF16)
    lo = (x - hi.astype(F32)).astype(BF16)
    if a_is_f32:
        return (jnp.dot(hi, b, preferred_element_type=F32) + jnp.dot(lo, b, preferred_element_type=F32))
    return (jnp.dot(a, hi, preferred_element_type=F32) + jnp.dot(a, lo, preferred_element_type=F32))


def _mlstm_chain(q, k, v, bc, br, li_r, li_c, total, mask, c_ref, m_ref, idx):
    dv = v.shape[1] - 128
    c_st = c_ref[idx]
    m_st = m_ref[idx:idx + 1, 0:1]
    dmat = jnp.where(mask, bc - br + li_r, NEG_BIG)
    inter = bc + m_st
    m_t = jnp.maximum(inter, jnp.max(dmat, axis=-1, keepdims=True))
    w_intra = jnp.exp(dmat - m_t)
    w_inter = jnp.exp(inter - m_t)
    sc = lax.dot_general(q, k, (((1,), (1,)), ((), ())), preferred_element_type=F32) * w_intra
    both = (w_inter * jnp.dot(q, c_st.astype(BF16), preferred_element_type=F32)
            + jnp.dot(sc.astype(BF16), v, preferred_element_type=F32))
    den = both[:, dv:dv + 1]
    h = both[:, :dv] / jnp.maximum(jnp.abs(den), jnp.exp(-m_t))
    g_s = total - bc + li_c
    m_next = jnp.maximum(total + m_st, jnp.max(g_s, axis=0, keepdims=True))
    a_prev = jnp.exp(total + m_st - m_next)
    kw = k.astype(F32) * jnp.exp(g_s - m_next)
    c_ref[idx] = a_prev * c_st + lax.dot_general(kw.astype(BF16), v, (((0,), (0,)), ((), ())),
                                                 preferred_element_type=F32)
    m_ref[idx:idx + 1, :] = jnp.broadcast_to(m_next, (1, m_ref.shape[1]))
    return h


def _mlstm_body(qkf_ref, vf_ref, gcf_ref, grf_ref, qkb_ref, vb_ref, gcb_ref, grb_ref,
                hf_ref, hb_ref, c_ref, m_ref, *, heads, dk, dv):
    @pl.when(pl.program_id(1) == 0)
    def _():
        c_ref[...] = jnp.zeros_like(c_ref)
        m_ref[...] = jnp.zeros_like(m_ref)

    lc = qkf_ref.shape[1]
    ones_blk = jnp.where(lax.broadcasted_iota(I32, (lc, 128), 1) == 0, 1.0, 0.0).astype(BF16)
    t_i = lax.broadcasted_iota(I32, (lc, lc), 0)
    s_i = lax.broadcasted_iota(I32, (lc, lc), 1)
    lower = s_i <= t_i
    upper = s_i >= t_i
    ltri = jnp.where(lower, 1.0, 0.0).astype(BF16)
    utri = jnp.where(upper, 1.0, 0.0).astype(BF16)

    for bb in range(qkf_ref.shape[0]):
        for direction, (qk_ref, v_ref, gc_ref, gr_ref, h_ref) in enumerate(
                ((qkf_ref, vf_ref, gcf_ref, grf_ref, hf_ref), (qkb_ref, vb_ref, gcb_ref, grb_ref, hb_ref))):
            fwd = direction == 0
            gc = gc_ref[bb]
            gr = gr_ref[bb]
            lfc, lfr = _log_sigmoid(gc), _log_sigmoid(gr)
            cum_c = _dot_split(ltri if fwd else utri, lfc, a_is_f32=False)
            cum_r = _dot_split(lfr, utri if fwd else ltri, a_is_f32=True)
            for hd in range(heads):
                gi = (0 if fwd else 2) * heads + hd
                gf = (1 if fwd else 3) * heads + hd
                bc, br = cum_c[:, gf:gf + 1], cum_r[gf:gf + 1, :]
                total = br[:, lc - 1:lc] if fwd else br[:, 0:1]
                q = qk_ref[bb, :, hd * dk:(hd + 1) * dk]
                k = qk_ref[bb, :, (heads + hd) * dk:(heads + hd + 1) * dk]
                v = jnp.concatenate([v_ref[bb, :, hd * dv:(hd + 1) * dv], ones_blk], axis=1)
                h = _mlstm_chain(q, k, v, bc, br, gr[gi:gi + 1, :], gc[:, gi:gi + 1], total,
                                 lower if fwd else upper, c_ref, m_ref, (bb * 2 + direction) * heads + hd)
                h_ref[bb, :, hd * dv:(hd + 1) * dv] = h.astype(h_ref.dtype)


def mlstm_scan(qk, u, v_blk, gcol, grow):
    b, s, w = qk.shape
    heads = ML_HEADS
    dk = w // (2 * heads)
    dv = 2 * dk
    lc = ML_CHUNK
    nc = s // lc
    ng = gcol.shape[-1]

    def fw(bi, j):
        return j

    def bw(bi, j):
        return nc - 1 - j

    nbs = ML_BATCH_PER_STEP if b % ML_BATCH_PER_STEP == 0 else 1

    def specs(pos):
        return [pl.BlockSpec((nbs, lc, w), lambda bi, j: (bi, pos(bi, j), 0)),
                pl.BlockSpec((nbs, lc, heads * dv), lambda bi, j: (bi, pos(bi, j), v_blk)),
                pl.BlockSpec((nbs, lc, ng), lambda bi, j: (bi, pos(bi, j), 0)),
                pl.BlockSpec((nbs, ng, lc), lambda bi, j: (bi, 0, pos(bi, j)))]

    hshape = jax.ShapeDtypeStruct((b, s, heads * dv), BF16)
    return pl.pallas_call(
        functools.partial(_mlstm_body, heads=heads, dk=dk, dv=dv), grid=(b // nbs, nc),
        in_specs=specs(fw) + specs(bw),
        out_specs=[pl.BlockSpec((nbs, lc, heads * dv), lambda bi, j: (bi, j, 0)),
                   pl.BlockSpec((nbs, lc, heads * dv), lambda bi, j: (bi, nc - 1 - j, 0))],
        out_shape=[hshape, hshape],
        scratch_shapes=[pltpu.VMEM((nbs * 2 * heads, dk, dv + 128), F32),
                        pltpu.VMEM((nbs * 2 * heads, 128), F32)],
        compiler_params=_params("parallel", "arbitrary"), name="mlstm_scan",
    )(qk, u, gcol, grow, qk, u, gcol, grow)


def _mlstm_out_body(res_ref, hf_ref, hb_ref, o_ref, g_ref, w_ref, out_ref, *, heads):
    hs = hf_ref[...].astype(F32) + hb_ref[...].astype(F32)
    dv = hs.shape[1] // heads
    g = g_ref[...]
    parts = []
    for hd in range(heads):
        seg = hs[:, hd * dv:(hd + 1) * dv]
        ms = jnp.mean(seg * seg, axis=-1, keepdims=True)
        parts.append(seg * lax.rsqrt(ms + EPS) * g[:, hd * dv:(hd + 1) * dv])
    a = jnp.concatenate(parts, axis=-1) * _sigmoid(o_ref[...].astype(F32))
    out_ref[...] = res_ref[...] + jnp.dot(a.astype(BF16), w_ref[...], preferred_element_type=F32)


def mlstm_out(res, hf, hb, u2d, o_blk, gain, w_out, tm=512):
    t, d = res.shape
    row = lambda i: (i, 0)
    return pl.pallas_call(
        functools.partial(_mlstm_out_body, heads=ML_HEADS), grid=(t // tm,),
        in_specs=[pl.BlockSpec((tm, d), row), pl.BlockSpec((tm, d), row), pl.BlockSpec((tm, d), row),
                  pl.BlockSpec((tm, d), lambda i: (i, o_blk)), pl.BlockSpec((1, d), lambda i: (0, 0)),
                  pl.BlockSpec((d, d), lambda i: (0, 0))],
        out_specs=pl.BlockSpec((tm, d), row), out_shape=jax.ShapeDtypeStruct((t, d), F32),
        compiler_params=_params("parallel"), name="mlstm_out",
    )(res, hf, hb, u2d, gain.reshape(1, d), w_out)


ROUTE_ROWS = 128
EXPERT_ROW0 = 8


def _router_body(x_ref, g_ref, wt_ref, b_ref, o_ref, cnt_ref, run_ref):
    @pl.when(pl.program_id(0) == 0)
    def _():
        run_ref[...] = jnp.zeros_like(run_ref)

    x = x_ref[...]
    ms = jnp.mean(x * x, axis=-1, keepdims=True)
    xn = x * lax.rsqrt(ms + EPS) * g_ref[...]
    logit = lax.dot_general(wt_ref[...], xn, (((1,), (1,)), ((), ())), preferred_element_type=F32,
                            precision=lax.Precision.HIGHEST) + b_ref[...]
    rows = [logit[r:r + 1, :] for r in range(EXPERT_ROW0 + N_EXPERTS)]
    g_best, g_idx = rows[0], jnp.zeros_like(rows[0])
    for gi in range(1, N_GROUPS):
        better = rows[gi] > g_best
        g_best = jnp.where(better, rows[gi], g_best)
        g_idx = jnp.where(better, float(gi), g_idx)
    g_den = sum(jnp.exp(rows[gi] - g_best) for gi in range(N_GROUPS))
    g_w = 1.0 / g_den
    sel = []
    for e in range(EXPERTS_PER_GROUP):
        v = rows[EXPERT_ROW0 + e]
        for gi in range(1, N_GROUPS):
            v = jnp.where(g_idx == float(gi), rows[EXPERT_ROW0 + gi * EXPERTS_PER_GROUP + e], v)
        sel.append(v)
    v1, i1 = sel[0], jnp.zeros_like(sel[0])
    for e in range(1, EXPERTS_PER_GROUP):
        better = sel[e] > v1
        v1 = jnp.where(better, sel[e], v1)
        i1 = jnp.where(better, float(e), i1)
    v2, i2 = jnp.full_like(v1, -jnp.inf), jnp.zeros_like(v1)
    for e in range(EXPERTS_PER_GROUP):
        better = (sel[e] > v2) & (i1 != float(e))
        v2 = jnp.where(better, sel[e], v2)
        i2 = jnp.where(better, float(e), i2)
    e21 = jnp.exp(v2 - v1)
    gate1 = g_w / (1.0 + e21)
    gate2 = gate1 * e21
    base = g_idx * float(EXPERTS_PER_GROUP)
    e1, e2 = base + i1, base + i2
    tm = e1.shape[1]
    erow = lax.broadcasted_iota(I32, (N_EXPERTS, tm), 0).astype(F32)
    oh1 = jnp.where(erow == e1, 1.0, 0.0)
    oh2 = jnp.where(erow == e2, 1.0, 0.0)
    cnt = oh1 + oh2
    earlier = jnp.where(lax.broadcasted_iota(I32, (tm, tm), 0) < lax.broadcasted_iota(I32, (tm, tm), 1),
                        1.0, 0.0).astype(BF16)
    run = run_ref[...]
    pos = run[:, 0:1] + jnp.dot(cnt.astype(BF16), earlier, preferred_element_type=F32)
    rank1 = jnp.sum(oh1 * pos, axis=0, keepdims=True)
    rank2 = jnp.sum(oh2 * pos, axis=0, keepdims=True)
    run = run + jnp.sum(cnt, axis=1, keepdims=True)
    run_ref[...] = run
    cnt_ref[...] = run
    zero = jnp.zeros_like(v1)
    o_ref[...] = jnp.concatenate([e1, e2, gate1, gate2, rank1, rank2, zero, zero], axis=0)


def moe_router(x2d, g, wg, bg, we, be, tm=512):
    t, d = x2d.shape
    wt = jnp.zeros((ROUTE_ROWS, d), F32).at[:N_GROUPS].set(wg.T).at[EXPERT_ROW0:EXPERT_ROW0 + N_EXPERTS].set(we.T)
    bias = jnp.zeros((ROUTE_ROWS, 1), F32).at[:N_GROUPS, 0].set(bg).at[EXPERT_ROW0:EXPERT_ROW0 + N_EXPERTS, 0].set(be)
    return pl.pallas_call(
        _router_body, grid=(t // tm,),
        in_specs=[pl.BlockSpec((tm, d), lambda i: (i, 0)), pl.BlockSpec((1, d), lambda i: (0, 0)),
                  pl.BlockSpec((ROUTE_ROWS, d), lambda i: (0, 0)), pl.BlockSpec((ROUTE_ROWS, 1), lambda i: (0, 0))],
        out_specs=[pl.BlockSpec((8, tm), lambda i: (0, i)), pl.BlockSpec((N_EXPERTS, 128), lambda i: (0, 0))],
        out_shape=[jax.ShapeDtypeStruct((8, t), F32), jax.ShapeDtypeStruct((N_EXPERTS, 128), F32)],
        scratch_shapes=[pltpu.VMEM((N_EXPERTS, 128), F32)],
        compiler_params=_params("arbitrary"), name="moe_router",
    )(x2d, g.reshape(1, d), wt, bias)


def _expert_body(blk_e_ref, nused_ref, tok_ref, x_ref, g_ref, w1_ref, w3_ref, w2_ref, ys_ref,
                 xbuf_ref, sem_ref, w1b_ref, w3b_ref, w2b_ref, *, tm):
    i = pl.program_id(0)
    nused = nused_ref[0]
    last = nused - 1
    slot = i % 2
    cur = blk_e_ref[jnp.minimum(i, last)]
    prev = blk_e_ref[jnp.minimum(jnp.maximum(i - 1, 0), last)]

    def row_copy(blk, sl, r):
        return pltpu.make_async_copy(x_ref.at[pl.ds(tok_ref[blk * tm + r], 1), :],
                                     xbuf_ref.at[sl, pl.ds(r, 1), :], sem_ref.at[sl])

    def wait_rows(sl):
        pltpu.make_async_copy(x_ref.at[pl.ds(0, tm), :], xbuf_ref.at[sl], sem_ref.at[sl]).wait()

    @pl.when(i == 0)
    def _():
        def first(r, c):
            row_copy(0, 0, r).start()
            return c
        lax.fori_loop(0, tm, first, 0, unroll=8)

    @pl.when((i == 0) | (cur != prev))
    def _():
        w1b_ref[...] = w1_ref[...].astype(BF16)
        w3b_ref[...] = w3_ref[...].astype(BF16)
        w2b_ref[...] = w2_ref[...].astype(BF16)

    @pl.when(i < nused)
    def _():
        for r in range(tm):
            row_copy(i + 1, 1 - slot, r).start()
        wait_rows(slot)
        x = xbuf_ref[slot]
        ms = jnp.mean(x * x, axis=-1, keepdims=True)
        xn = (x * lax.rsqrt(ms + EPS) * g_ref[...]).astype(BF16)
        h1 = jnp.dot(xn, w1b_ref[...], preferred_element_type=F32)
        h3 = jnp.dot(xn, w3b_ref[...], preferred_element_type=F32)
        hid = (h1 * _sigmoid(h1) * h3).astype(BF16)
        ys_ref[...] = jnp.dot(hid, w2b_ref[...], preferred_element_type=F32)

    @pl.when(i == nused)
    def _():
        wait_rows(slot)

    @pl.when(i >= nused)
    def _():
        ys_ref[...] = jnp.zeros_like(ys_ref)


def moe_experts(x2d, g, src_tok, blk_e, nused, w1, w3, w2, layer, tm):
    t, d = x2d.shape
    cap = src_tok.shape[0]
    de = w1.shape[-1]

    def wspec(shape):
        return pl.BlockSpec((None, None) + shape,
                            lambda i, be, nu, tok: (layer, be[jnp.minimum(i, nu[0] - 1)], 0, 0))

    grid_spec = pltpu.PrefetchScalarGridSpec(
        num_scalar_prefetch=3, grid=(cap // tm,),
        in_specs=[pl.BlockSpec(memory_space=pl.ANY), pl.BlockSpec((1, d), lambda i, be, nu, tok: (0, 0)),
                  wspec((d, de)), wspec((d, de)), wspec((de, d))],
        out_specs=pl.BlockSpec((tm, d), lambda i, be, nu, tok: (i, 0)),
        scratch_shapes=[pltpu.VMEM((2, tm, d), F32), pltpu.SemaphoreType.DMA((2,)),
                        pltpu.VMEM((d, de), BF16), pltpu.VMEM((d, de), BF16), pltpu.VMEM((de, d), BF16)])
    return pl.pallas_call(
        functools.partial(_expert_body, tm=tm), grid_spec=grid_spec,
        out_shape=jax.ShapeDtypeStruct((cap, d), F32),
        compiler_params=_params("arbitrary", disable_bounds_checks=True), name="moe_experts",
    )(blk_e, nused, src_tok, x2d, g.reshape(1, d), w1, w3, w2)


def _combine_body(dest_ref, x_ref, gate_ref, ys_ref, o_ref, buf_ref, sem_ref, *, tc):
    i = pl.program_id(0)
    n = pl.num_programs(0)
    slot = i % 2

    def row_copy(step, sl, r, kk):
        return pltpu.make_async_copy(ys_ref.at[pl.ds(dest_ref[(step * tc + r) * 2 + kk], 1), :],
                                     buf_ref.at[sl, kk, pl.ds(r, 1), :], sem_ref.at[sl])

    def issue_step(step, sl):
        def body(r, c):
            row_copy(step, sl, r, 0).start()
            row_copy(step, sl, r, 1).start()
            return c
        lax.fori_loop(0, tc, body, 0, unroll=8)

    @pl.when(i == 0)
    def _():
        issue_step(0, 0)

    @pl.when(i + 1 < n)
    def _():
        issue_step(i + 1, 1 - slot)

    for kk in range(2):
        pltpu.make_async_copy(ys_ref.at[pl.ds(0, tc), :], buf_ref.at[slot, kk], sem_ref.at[slot]).wait()

    gate = gate_ref[...]
    o_ref[...] = x_ref[...] + gate[:, 0:1] * buf_ref[slot, 0] + gate[:, 1:2] * buf_ref[slot, 1]


def moe_combine(x2d, gates, ys, dest, tc=256):
    t, d = x2d.shape
    grid_spec = pltpu.PrefetchScalarGridSpec(
        num_scalar_prefetch=1, grid=(t // tc,),
        in_specs=[pl.BlockSpec((tc, d), lambda i, dest: (i, 0)), pl.BlockSpec((tc, 2), lambda i, dest: (i, 0)),
                  pl.BlockSpec(memory_space=pl.ANY)],
        out_specs=pl.BlockSpec((tc, d), lambda i, dest: (i, 0)),
        scratch_shapes=[pltpu.VMEM((2, 2, tc, d), F32), pltpu.SemaphoreType.DMA((2,))])
    return pl.pallas_call(
        functools.partial(_combine_body, tc=tc), grid_spec=grid_spec,
        out_shape=jax.ShapeDtypeStruct((t, d), F32),
        compiler_params=_params("arbitrary", disable_bounds_checks=True), name="moe_combine",
    )(dest, x2d, gates, ys)


def hier_moe_residual(x2d, g, wg, bg, we, be, w1, w3, w2, layer, tm=256):
    t, d = x2d.shape
    route, counts_b = moe_router(x2d, g, wg, bg, we, be)
    eid = route[0:2].T.astype(I32).reshape(-1)
    rank = route[4:6].T.astype(I32).reshape(-1)
    gates = route[2:4].T
    counts = counts_b[:, 0].astype(I32)
    padded = (counts + tm - 1) // tm * tm
    pad_end = jnp.cumsum(padded)
    pad_start = pad_end - padded
    experts = jnp.arange(N_EXPERTS, dtype=I32)
    dest = rank + jnp.sum(jnp.where(eid[:, None] == experts[None, :], pad_start[None, :], 0), axis=1)
    cap = t * 2 + N_EXPERTS * tm
    nblk = cap // tm
    blk_row0 = jnp.arange(nblk, dtype=I32) * tm
    blk_e = jnp.minimum(jnp.sum((pad_end[None, :] <= blk_row0[:, None]).astype(I32), axis=1), N_EXPERTS - 1)
    nused = (pad_end[-1:] // tm).astype(I32)
    nslot = 2 * t
    slot_sorted = jnp.sort(eid * nslot + jnp.arange(nslot, dtype=I32)) % nslot
    starts = jnp.cumsum(counts) - counts
    blk_first = starts[blk_e] + blk_row0 - pad_start[blk_e]
    blk_last = starts[blk_e] + counts[blk_e] - 1
    pos = jnp.minimum(blk_first[:, None] + jnp.arange(tm, dtype=I32)[None, :], blk_last[:, None]).reshape(-1)
    src_tok = slot_sorted[jnp.clip(pos, 0, nslot - 1)] // 2
    ys = moe_experts(x2d, g, src_tok.astype(I32), blk_e, nused, w1, w3, w2, layer, tm)
    return moe_combine(x2d, gates, ys, dest.astype(I32))


def _even_layer(x2d, b, s, layer, mix_g, w_in, q_norm, k_norm, lam_q1, lam_k1, lam_q2, lam_k2, subln,
                hy_conv_w, hy_conv_b, f_w1, f_b1, f_freq, f_w2, f_b2, f_w3, hy_skip, w_out):
    d = x2d.shape[1]
    d_att = d // 2
    (u2d,) = norm_matmul(x2d, mix_g, [w_in.astype(BF16)], [BF16])
    u = u2d.reshape(b, s, -1)
    lambda_init = 0.8 - 0.6 * math.exp(-0.3 * layer)
    lam = jnp.exp(jnp.sum(lam_q1 * lam_k1)) - jnp.exp(jnp.sum(lam_q2 * lam_k2)) + lambda_init
    dk = q_norm.shape[0]
    qt = head_prep(u, 0, q_norm, scale=dk ** -0.5 * math.log2(math.e), transpose=True)
    kp = head_prep(u, ATT_HEADS, k_norm)
    vt = head_prep(u, 2 * ATT_HEADS, transpose=True)
    y_att = diff_attention(qt, kp, vt, lam, subln, lambda_init)
    z, x1c, zp = hy_prep(u, 3 * d_att, hy_conv_w, hy_conv_b)
    tables = fft_tables(s)
    kf = filter_spectrum(hyena_filter(s, f_w1, f_b1, f_freq, f_w2, f_b2, f_w3), tables[0], tables[1])
    y_hy = hy_fft_conv(zp, z, x1c, kf, tables, hy_skip)
    w_out = w_out.astype(BF16)
    return matmul_residual(x2d, [y_att.reshape(b * s, -1), y_hy.reshape(b * s, -1)],
                           [w_out[:d_att], w_out[d_att:]])


def _odd_layer(x2d, b, s, mix_g, w_in, conv_w, conv_b, gate_b, out_norm, w_out):
    d = x2d.shape[1]
    qk_w = conv_w.shape[1]
    main_w = qk_w + 2 * d
    ng = 4 * ML_HEADS
    w_gate = jnp.zeros((d, 128), F32).at[:, :ng].set(w_in[:, main_w:]).astype(BF16)
    u2d, ug = norm_matmul(x2d, mix_g, [w_in[:, :main_w].astype(BF16), w_gate], [BF16, F32])
    u = u2d.reshape(b, s, main_w)
    gcol = (ug[:, :ng] + gate_b).reshape(b, s, ng)
    grow = jnp.swapaxes(gcol, 1, 2)
    dk = qk_w // (2 * ML_HEADS)
    col_scale = jnp.concatenate([jnp.full((qk_w // 2,), dk ** -0.5, F32), jnp.ones((qk_w // 2,), F32)])
    qk = ml_prep(u, conv_w, conv_b, col_scale)
    hf, hb = mlstm_scan(qk, u, qk_w // d, gcol, grow)
    return mlstm_out(x2d, hf.reshape(b * s, d), hb.reshape(b * s, d), u2d, (qk_w + d) // d, out_norm,
                     w_out.astype(BF16))


def kernel(x, mix_norm, ffn_norm, ev_w_in, ev_q_norm, ev_k_norm, ev_lam_q1, ev_lam_k1, ev_lam_q2, ev_lam_k2, ev_subln, ev_hy_conv_w, ev_hy_conv_b, ev_hy_f_w1, ev_hy_f_b1, ev_hy_f_freq, ev_hy_f_w2, ev_hy_f_b2, ev_hy_f_w3, ev_hy_skip, ev_w_out, od_w_in, od_conv_w, od_conv_b, od_gate_b, od_out_norm, od_w_out, moe_wg, moe_bg, moe_we, moe_be, moe_w1, moe_w3, moe_w2):
    b, s, d = x.shape
    depth = mix_norm.shape[0]
    x2d = x.reshape(b * s, d)
    for layer in range(depth):
        j = layer // 2
        if layer % 2 == 0:
            x2d = _even_layer(x2d, b, s, layer, mix_norm[layer], ev_w_in[j], ev_q_norm[j], ev_k_norm[j],
                              ev_lam_q1[j], ev_lam_k1[j], ev_lam_q2[j], ev_lam_k2[j], ev_subln[j],
                              ev_hy_conv_w[j], ev_hy_conv_b[j], ev_hy_f_w1[j], ev_hy_f_b1[j], ev_hy_f_freq[j],
                              ev_hy_f_w2[j], ev_hy_f_b2[j], ev_hy_f_w3[j], ev_hy_skip[j], ev_w_out[j])
        else:
            x2d = _odd_layer(x2d, b, s, mix_norm[layer], od_w_in[j], od_conv_w[j], od_conv_b[j], od_gate_b[j],
                             od_out_norm[j], od_w_out[j])
        x2d = hier_moe_residual(x2d, ffn_norm[layer], moe_wg[layer], moe_bg[layer], moe_we[layer],
                                moe_be[layer], moe_w1, moe_w3, moe_w2, layer)
    return x2d.reshape(b, s, d)
```

```python
import functools
import math

import jax
import jax.numpy as jnp
from jax import lax
from jax.experimental import pallas as pl
from jax.experimental.pallas import tpu as pltpu

F32 = jnp.float32
BF16 = jnp.bfloat16
I32 = jnp.int32

EPS = 1e-6
ROPE_THETA = 500000.0
ATT_HEADS = 4
ML_HEADS = 4
ML_CHUNK = 128
ML_BATCH_PER_STEP = 2
N_GROUPS = 4
EXPERTS_PER_GROUP = 8
N_EXPERTS = N_GROUPS * EXPERTS_PER_GROUP
HY_EMB_BANDS = 16
HY_MIN_DECAY = math.log(1e-2) / 1.5
HY_MAX_DECAY = math.log(1e-2) / 0.3

V7X_VMEM_BYTES = 64 * 1024 * 1024
VMEM_LIMIT = V7X_VMEM_BYTES - 8 * 1024 * 1024
NEG_BIG = -1e30


def _params(*sem, **kw):
    return pltpu.CompilerParams(dimension_semantics=sem, vmem_limit_bytes=VMEM_LIMIT, **kw)


def _sigmoid(x):
    return 1.0 / (1.0 + jnp.exp(-x))


def _norm_matmul_body(x_ref, g_ref, *refs, n_out, col_chunk):
    w_refs, o_refs = refs[:n_out], refs[n_out:]
    x = x_ref[...]
    ms = jnp.mean(x * x, axis=-1, keepdims=True)
    hn = (x * lax.rsqrt(ms + EPS) * g_ref[...]).astype(BF16)
    for w_ref, o_ref in zip(w_refs, o_refs):
        n = w_ref.shape[1]
        for c in range(0, n, col_chunk):
            ce = min(n, c + col_chunk)
            o_ref[:, c:ce] = jnp.dot(hn, w_ref[:, c:ce], preferred_element_type=F32).astype(o_ref.dtype)


def norm_matmul(x2d, g, ws, out_dtypes, tm=512):
    t, d = x2d.shape
    in_specs = [pl.BlockSpec((tm, d), lambda i: (i, 0)), pl.BlockSpec((1, d), lambda i: (0, 0))]
    in_specs += [pl.BlockSpec(w.shape, lambda i: (0, 0)) for w in ws]
    out_specs = [pl.BlockSpec((tm, w.shape[1]), lambda i: (i, 0)) for w in ws]
    out_shape = [jax.ShapeDtypeStruct((t, w.shape[1]), dt) for w, dt in zip(ws, out_dtypes)]
    return pl.pallas_call(
        functools.partial(_norm_matmul_body, n_out=len(ws), col_chunk=1024),
        grid=(t // tm,), in_specs=in_specs, out_specs=out_specs, out_shape=out_shape,
        compiler_params=_params("parallel"), name="norm_matmul",
    )(x2d, g.reshape(1, d), *ws)


def _matmul_res_body(res_ref, *refs, n_in):
    a_refs, w_refs, o_ref = refs[:n_in], refs[n_in:2 * n_in], refs[2 * n_in]
    acc = res_ref[...]
    for a_ref, w_ref in zip(a_refs, w_refs):
        acc = acc + jnp.dot(a_ref[...], w_ref[...], preferred_element_type=F32)
    o_ref[...] = acc


def matmul_residual(res, a_list, w_list, tm=512):
    t, d = res.shape
    in_specs = [pl.BlockSpec((tm, d), lambda i: (i, 0))]
    in_specs += [pl.BlockSpec((tm, a.shape[1]), lambda i: (i, 0)) for a in a_list]
    in_specs += [pl.BlockSpec(w.shape, lambda i: (0, 0)) for w in w_list]
    return pl.pallas_call(
        functools.partial(_matmul_res_body, n_in=len(a_list)),
        grid=(t // tm,), in_specs=in_specs, out_specs=pl.BlockSpec((tm, d), lambda i: (i, 0)),
        out_shape=jax.ShapeDtypeStruct((t, d), F32),
        compiler_params=_params("parallel"), name="matmul_residual",
    )(res, *a_list, *w_list)


def _head_prep_body(u_ref, *refs, dk, rope, transpose):
    o_ref = refs[-1]
    x = u_ref[...].astype(F32)
    if rope:
        g_ref, c_ref, s1_ref, s2_ref = refs[:4]
        lane = lax.broadcasted_iota(I32, x.shape, 1)
        lo = lane < dk
        x2 = x * x
        s_lo = jnp.sum(jnp.where(lo, x2, 0.0), axis=-1, keepdims=True)
        s_hi = jnp.sum(jnp.where(lo, 0.0, x2), axis=-1, keepdims=True)
        ms = jnp.where(lo, s_lo, s_hi) * (1.0 / dk)
        y = x * lax.rsqrt(ms + EPS) * g_ref[...]
        x = y * c_ref[...] + pltpu.roll(y, 120, 1) * s1_ref[...] + pltpu.roll(y, 8, 1) * s2_ref[...]
    if transpose:
        x = x.T
    o_ref[...] = x.astype(o_ref.dtype)


def _rope_lane_tables(seq, dk, rope_dim, scale):
    half = rope_dim // 2
    inv_freq = 1.0 / (ROPE_THETA ** (jnp.arange(0, rope_dim, 2, dtype=F32) / rope_dim))
    ang = jnp.arange(seq, dtype=F32)[:, None] * inv_freq[None, :]
    cos, sin = jnp.cos(ang), jnp.sin(ang)
    d = jnp.arange(2 * dk) % dk
    fi = d % half
    c_tab = jnp.where(d[None, :] < rope_dim, cos[:, fi], 1.0)
    s1_tab = jnp.where(d[None, :] < half, -sin[:, fi], 0.0)
    s2_tab = jnp.where((d[None, :] >= half) & (d[None, :] < rope_dim), sin[:, fi], 0.0)
    return (jnp.stack([c_tab, s1_tab, s2_tab]) * scale).astype(F32)


def head_prep(u, blk0, norm_gain=None, scale=1.0, transpose=False):
    b, s, _ = u.shape
    h = ATT_HEADS
    rope = norm_gain is not None
    in_specs = [pl.BlockSpec((None, s, 128), lambda c, bi: (bi, 0, blk0 + c))]
    args = [u]
    dk = 64
    if rope:
        dk = norm_gain.shape[0]
        assert 2 * dk == 128 and dk // 4 == 16, "rope roll shifts assume 64-wide components, 16 rotary dims"
        tabs = _rope_lane_tables(s, dk, dk // 4, scale)
        in_specs += [pl.BlockSpec((1, 128), lambda c, bi: (0, 0))] + [pl.BlockSpec((s, 128), lambda c, bi: (0, 0))] * 3
        args += [jnp.tile(norm_gain, 2).reshape(1, 128).astype(F32), tabs[0], tabs[1], tabs[2]]
    if transpose:
        out_spec = pl.BlockSpec((None, 128, s), lambda c, bi: (bi, c, 0))
        out_shape = jax.ShapeDtypeStruct((b, h * 128, s), BF16)
    else:
        out_spec = pl.BlockSpec((None, s, 128), lambda c, bi: (bi, 0, c))
        out_shape = jax.ShapeDtypeStruct((b, s, h * 128), BF16)
    return pl.pallas_call(
        functools.partial(_head_prep_body, dk=dk, rope=rope, transpose=transpose), grid=(h, b),
        in_specs=in_specs, out_specs=out_spec, out_shape=out_shape,
        compiler_params=_params("parallel", "parallel"), name="head_prep",
    )(*args)


def _attn_body(lam_ref, qt_ref, k_ref, vt_ref, g_ref, o_ref, *, tq, dk, post_scale, n_split):
    lam = lam_ref[0, 0]
    th = tq // n_split
    for part in range(n_split):
        qt = qt_ref[:, part * th:(part + 1) * th]
        row = lax.broadcasted_iota(I32, qt.shape, 0)
        zero = jnp.zeros_like(qt)
        qq = jnp.concatenate([jnp.where(row < dk, qt, zero), jnp.where(row < dk, zero, qt)], axis=1)
        st = jnp.dot(k_ref[...], qq, preferred_element_type=F32)
        m = jnp.max(st, axis=0, keepdims=True)
        p = jnp.exp2(st - m)
        r = 1.0 / jnp.sum(p, axis=0, keepdims=True)
        ot = jnp.dot(vt_ref[...], p.astype(BF16), preferred_element_type=F32)
        o = (ot[:, :th] * r[:, :th] - ot[:, th:] * (lam * r[:, th:])).T
        ms = jnp.mean(o * o, axis=-1, keepdims=True)
        o_ref[part * th:(part + 1) * th, :] = (o * lax.rsqrt(ms + EPS) * g_ref[...] * post_scale).astype(o_ref.dtype)


def diff_attention(qt, k, vt, lam, subln, lambda_init, tq=256):
    b, s, _ = k.shape
    h = ATT_HEADS
    return pl.pallas_call(
        functools.partial(_attn_body, tq=tq, dk=64, post_scale=1.0 - lambda_init, n_split=1),
        grid=(b, h, s // tq),
        in_specs=[pl.BlockSpec(memory_space=pltpu.SMEM),
                  pl.BlockSpec((None, 128, tq), lambda bi, hi, i: (bi, hi, i)),
                  pl.BlockSpec((None, s, 128), lambda bi, hi, i: (bi, 0, hi)),
                  pl.BlockSpec((None, 128, s), lambda bi, hi, i: (bi, hi, 0)),
                  pl.BlockSpec((1, 128), lambda bi, hi, i: (0, 0))],
        out_specs=pl.BlockSpec((None, tq, 128), lambda bi, hi, i: (bi, i, hi)),
        out_shape=jax.ShapeDtypeStruct((b, s, h * 128), BF16),
        compiler_params=_params("parallel", "parallel", "parallel"), name="diff_attention",
    )(lam.reshape(1, 1).astype(F32), qt, k, vt, subln.reshape(1, 128).astype(F32))


def _conv3(u_ref, w_ref, b_ref):
    x = u_ref[...].astype(F32)
    s = x.shape[0]
    row = lax.broadcasted_iota(I32, x.shape, 0)
    x_prev = jnp.where(row == 0, 0.0, pltpu.roll(x, 1, 0))
    x_next = jnp.where(row == s - 1, 0.0, pltpu.roll(x, s - 1, 0))
    w = w_ref[...]
    return b_ref[...] + x_prev * w[0:1] + x * w[1:2] + x_next * w[2:3]


FFT_N1 = 64
FFT_UNROLL = 8
FFT_PAD = 8


def _hy_prep_body(x1_ref, x2_ref, v_ref, w1_ref, w2_ref, wv_ref, b1_ref, b2_ref, bv_ref, z_ref, x1c_ref, zp_ref):
    x1c_ref[...] = _conv3(x1_ref, w1_ref, b1_ref).astype(x1c_ref.dtype)
    z = _conv3(v_ref, wv_ref, bv_ref) * _conv3(x2_ref, w2_ref, b2_ref)
    z_ref[...] = z.astype(z_ref.dtype)
    nb = z.shape[0] // FFT_N1
    zp_ref[...] = jnp.zeros_like(zp_ref)
    for n2 in range(nb):
        for ci in range(z.shape[1] // 128):
            zp_ref[ci, pl.ds(n2, FFT_N1, stride=nb + FFT_PAD), :] = (
                z[n2 * FFT_N1:(n2 + 1) * FFT_N1, ci * 128:(ci + 1) * 128])


def hy_prep(u, col0, conv_w, conv_b, tc=256):
    b, s, _ = u.shape
    d_hy = conv_w.shape[1] // 3
    nct = d_hy // tc
    blk0 = col0 // tc
    sp = FFT_N1 * (s // FFT_N1 + FFT_PAD)

    def uspec(part):
        return pl.BlockSpec((None, s, tc), lambda bi, c: (bi, 0, blk0 + part * nct + c))

    def wspec(part, rows):
        return pl.BlockSpec((rows, tc), lambda bi, c: (0, part * nct + c))

    ospec = pl.BlockSpec((None, s, tc), lambda bi, c: (bi, 0, c))
    return pl.pallas_call(
        _hy_prep_body, grid=(b, nct),
        in_specs=[uspec(0), uspec(1), uspec(2), wspec(0, 3), wspec(1, 3), wspec(2, 3),
                  wspec(0, 1), wspec(1, 1), wspec(2, 1)],
        out_specs=[ospec, ospec, pl.BlockSpec((None, tc // 128, sp, 128), lambda bi, c: (bi, c, 0, 0))],
        out_shape=[jax.ShapeDtypeStruct((b, s, d_hy), BF16), jax.ShapeDtypeStruct((b, s, d_hy), BF16),
                   jax.ShapeDtypeStruct((b, d_hy // 128, sp, 128), F32)],
        compiler_params=_params("parallel", "parallel"), name="hy_prep",
    )(u, u, u, conv_w, conv_w, conv_w, conv_b.reshape(1, -1), conv_b.reshape(1, -1), conv_b.reshape(1, -1))


def hyena_filter(length, w1, b1, freq, w2, b2, w3):
    d_hy = w3.shape[1] // 2
    t = jnp.linspace(0.0, 1.0, length, dtype=F32)[:, None]
    bands = jnp.linspace(1e-4, HY_EMB_BANDS - 1, HY_EMB_BANDS, dtype=F32)[None, :]
    ang = (2.0 * math.pi / length) * jnp.arange(length, dtype=F32)[:, None] * bands
    z = jnp.concatenate([t, jnp.cos(ang), -jnp.sin(ang)], axis=-1)
    hp = lax.Precision.HIGHEST
    hdn = jnp.sin(freq * (jnp.dot(z, w1, precision=hp) + b1))
    hdn = jnp.sin(freq * (jnp.dot(hdn, w2, precision=hp) + b2))
    filt = jnp.dot(hdn, w3, precision=hp)
    deltas = jnp.abs(jnp.linspace(HY_MIN_DECAY, HY_MAX_DECAY, d_hy, dtype=F32))
    decay = jnp.exp(-t * deltas[None, :])
    h_fwd = filt[:, :d_hy] * decay
    h_bwd = filt[:, d_hy:] * decay
    h_fwd = h_fwd.at[0].add(h_bwd[0])
    h_bwd = h_bwd.at[0].set(0.0)
    norm = jnp.sum(jnp.abs(h_fwd), axis=0, keepdims=True) + jnp.sum(jnp.abs(h_bwd), axis=0, keepdims=True) + EPS
    return jnp.concatenate([h_fwd / norm, h_bwd / norm], axis=1)


def fft_tables(length):
    n = 2 * length
    n1c, n2c, nb = FFT_N1, n // FFT_N1, length // FFT_N1
    unit = 2.0 * math.pi / n
    i1 = jnp.arange(n1c, dtype=I32)
    i2 = jnp.arange(n2c, dtype=I32)
    ib = jnp.arange(nb, dtype=I32)
    samp = i1[:, None, None] + n1c * ib[None, None, :]
    ang = ((i2[None, :, None] * samp) % n).astype(F32) * unit
    m1 = jnp.concatenate([jnp.cos(ang), -jnp.sin(ang)], axis=1)
    ang = ((i1[:, None] * i1[None, :]) % n1c).astype(F32) * (2.0 * math.pi / n1c)
    c, s = jnp.cos(ang), jnp.sin(ang)
    f1 = jnp.concatenate([jnp.concatenate([c, s], axis=1), jnp.concatenate([-s, c], axis=1)], axis=0)
    freq = n2c * i1[None, None, :] + i2[:, None, None]
    ang = ((i1[None, :, None] * freq) % n).astype(F32) * unit
    c, s = jnp.cos(ang), jnp.sin(ang)
    g1 = jnp.concatenate([jnp.concatenate([c, -s], axis=2), jnp.concatenate([s, c], axis=2)], axis=1)
    ang = ((ib[:, None] * i2[None, :]) % n2c).astype(F32) * (2.0 * math.pi / n2c)
    g2 = jnp.concatenate([jnp.cos(ang), -jnp.sin(ang)], axis=1) * (1.0 / n)
    return m1.astype(BF16), f1.astype(BF16), g1.astype(BF16), g2.astype(BF16)


def _fft_stage1(xp_ref, m1_ref, p_ref):
    nb = m1_ref.shape[2]
    n2c = m1_ref.shape[1] // 2

    def body(n1, c):
        x = xp_ref[pl.ds(pl.multiple_of(n1 * (nb + FFT_PAD), 8), nb), :].astype(BF16)
        a = jnp.dot(m1_ref[n1], x, preferred_element_type=F32)
        p_ref[0, pl.ds(n1, n2c, stride=FFT_N1 + FFT_PAD), :] = a[:n2c]
        p_ref[1, pl.ds(n1, n2c, stride=FFT_N1 + FFT_PAD), :] = a[n2c:]
        return c
    lax.fori_loop(0, FFT_N1, body, 0, unroll=FFT_UNROLL)


def _fft_stage2(p_ref, f1_ref, k2):
    r0 = pl.multiple_of(k2 * (FFT_N1 + FFT_PAD), 8)
    slab = jnp.concatenate([p_ref[0, pl.ds(r0, FFT_N1), :], p_ref[1, pl.ds(r0, FFT_N1), :]], axis=0)
    return jnp.dot(f1_ref[...], slab.astype(BF16), preferred_element_type=F32)


def _spectrum_body(xp_ref, m1_ref, f1_ref, o_ref, p_ref):
    _fft_stage1(xp_ref, m1_ref, p_ref)

    def body(k2, c):
        o_ref[k2] = _fft_stage2(p_ref, f1_ref, k2)
        return c
    lax.fori_loop(0, o_ref.shape[0], body, 0, unroll=FFT_UNROLL)


def filter_spectrum(ab, m1, f1):
    length, c2 = ab.shape
    nb = length // FFT_N1
    n2c = m1.shape[1] // 2
    nch = c2 // 128
    abp = jnp.pad(ab.reshape(nb, FFT_N1, nch, 128).transpose(2, 1, 0, 3), ((0, 0), (0, 0), (0, FFT_PAD), (0, 0)))
    abp = abp.reshape(nch, FFT_N1 * (nb + FFT_PAD), 128)
    spec = pl.pallas_call(
        _spectrum_body, grid=(nch,),
        in_specs=[pl.BlockSpec((None,) + abp.shape[1:], lambda c: (c, 0, 0)),
                  pl.BlockSpec(m1.shape, lambda c: (0, 0, 0)), pl.BlockSpec(f1.shape, lambda c: (0, 0))],
        out_specs=pl.BlockSpec((None, n2c, 2 * FFT_N1, 128), lambda c: (c, 0, 0, 0)),
        out_shape=jax.ShapeDtypeStruct((nch, n2c, 2 * FFT_N1, 128), F32),
        scratch_shapes=[pltpu.VMEM((2, n2c * (FFT_N1 + FFT_PAD), 128), F32)],
        compiler_params=_params("parallel"), name="filter_spectrum",
    )(abp, m1, f1)
    fa, fb = spec[:nch // 2], spec[nch // 2:]
    h = FFT_N1
    return jnp.concatenate([fa[:, :, :h] + fb[:, :, :h], fa[:, :, h:] - fb[:, :, h:]], axis=2).astype(BF16)


def _hy_fft_body(zp_ref, z_ref, x1c_ref, kf_ref, m1_ref, f1_ref, g1_ref, g2_ref, skip_ref, o_ref,
                 p_ref, q_ref, y_ref):
    h = FFT_N1
    n2c = g1_ref.shape[0]
    nb = g2_ref.shape[0]
    _fft_stage1(zp_ref, m1_ref, p_ref)

    def mid(k2, c):
        xf = _fft_stage2(p_ref, f1_ref, k2)
        kf = kf_ref[k2].astype(F32)
        xr, xi, kr, ki = xf[:h], xf[h:], kf[:h], kf[h:]
        y = jnp.concatenate([xr * kr - xi * ki, xr * ki + xi * kr], axis=0).astype(BF16)
        d = jnp.dot(g1_ref[k2], y, preferred_element_type=F32)
        q_ref[0, pl.ds(k2, h, stride=n2c + FFT_PAD), :] = d[:h]
        q_ref[1, pl.ds(k2, h, stride=n2c + FFT_PAD), :] = d[h:]
        return c
    lax.fori_loop(0, n2c, mid, 0, unroll=FFT_UNROLL)

    def last(t1, c):
        r0 = pl.multiple_of(t1 * (n2c + FFT_PAD), 8)
        slab = jnp.concatenate([q_ref[0, pl.ds(r0, n2c), :], q_ref[1, pl.ds(r0, n2c), :]], axis=0)
        y_ref[pl.ds(t1, nb, stride=h), :] = jnp.dot(g2_ref[...], slab.astype(BF16), preferred_element_type=F32)
        return c
    lax.fori_loop(0, h, last, 0, unroll=FFT_UNROLL)

    z = z_ref[...].astype(F32)
    o_ref[...] = ((y_ref[...] + z * skip_ref[...]) * x1c_ref[...].astype(F32)).astype(o_ref.dtype)


def hy_fft_conv(zp, z, x1c, kf, tables, skip):
    m1, f1, g1, g2 = tables
    b, s, c = z.shape
    nch = c // 128

    def const(shape):
        return pl.BlockSpec(shape, lambda ci, bi: (0,) * len(shape))

    nat = pl.BlockSpec((None, s, 128), lambda ci, bi: (bi, 0, ci))
    return pl.pallas_call(
        _hy_fft_body, grid=(nch, b),
        in_specs=[pl.BlockSpec((None, None) + zp.shape[2:], lambda ci, bi: (bi, ci, 0, 0)), nat, nat,
                  pl.BlockSpec((None,) + kf.shape[1:], lambda ci, bi: (ci, 0, 0, 0)),
                  const(m1.shape), const(f1.shape), const(g1.shape), const(g2.shape),
                  pl.BlockSpec((1, 128), lambda ci, bi: (0, ci))],
        out_specs=nat,
        out_shape=jax.ShapeDtypeStruct((b, s, c), BF16),
        scratch_shapes=[pltpu.VMEM((2, g1.shape[0] * (FFT_N1 + FFT_PAD), 128), F32),
                        pltpu.VMEM((2, FFT_N1 * (g1.shape[0] + FFT_PAD), 128), F32),
                        pltpu.VMEM((s, 128), F32)],
        compiler_params=_params("parallel", "parallel"), name="hy_fft_conv",
    )(zp, z, x1c, kf, m1, f1, g1, g2, skip.reshape(1, c).astype(F32))


def _ml_prep_body(u_ref, w_ref, b_ref, sc_ref, o_ref):
    y = _conv3(u_ref, w_ref, b_ref)
    o_ref[...] = (y * _sigmoid(y) * sc_ref[...]).astype(o_ref.dtype)


def ml_prep(u, conv_w, conv_b, col_scale, tc=256):
    b, s, _ = u.shape
    w = conv_w.shape[1]
    return pl.pallas_call(
        _ml_prep_body, grid=(b, w // tc),
        in_specs=[pl.BlockSpec((None, s, tc), lambda bi, c: (bi, 0, c)),
                  pl.BlockSpec((3, tc), lambda bi, c: (0, c)),
                  pl.BlockSpec((1, tc), lambda bi, c: (0, c)),
                  pl.BlockSpec((1, tc), lambda bi, c: (0, c))],
        out_specs=pl.BlockSpec((None, s, tc), lambda bi, c: (bi, 0, c)),
        out_shape=jax.ShapeDtypeStruct((b, s, w), BF16),
        compiler_params=_params("parallel", "parallel"), name="ml_prep",
    )(u, conv_w, conv_b.reshape(1, w), col_scale.reshape(1, w))


def _log_sigmoid(x):
    return jnp.minimum(x, 0.0) - jnp.log(1.0 + jnp.exp(-jnp.abs(x)))


def _dot_split(a, b, a_is_f32):
    x = a if a_is_f32 else b
    hi = x.astype(BF16)
    lo = (x - hi.astype(F32)).astype(BF16)
    if a_is_f32:
        return (jnp.dot(hi, b, preferred_element_type=F32) + jnp.dot(lo, b, preferred_element_type=F32))
    return (jnp.dot(a, hi, preferred_element_type=F32) + jnp.dot(a, lo, preferred_element_type=F32))


def _mlstm_chain(q, k, v, bc, br, li_r, li_c, total, mask, c_ref, m_ref, idx):
    dv = v.shape[1] - 128
    c_st = c_ref[idx]
    m_st = m_ref[idx:idx + 1, 0:1]
    dmat = jnp.where(mask, bc - br + li_r, NEG_BIG)
    inter = bc + m_st
    m_t = jnp.maximum(inter, jnp.max(dmat, axis=-1, keepdims=True))
    w_intra = jnp.exp(dmat - m_t)
    w_inter = jnp.exp(inter - m_t)
    sc = lax.dot_general(q, k, (((1,), (1,)), ((), ())), preferred_element_type=F32) * w_intra
    both = (w_inter * jnp.dot(q, c_st.astype(BF16), preferred_element_type=F32)
            + jnp.dot(sc.astype(BF16), v, preferred_element_type=F32))
    den = both[:, dv:dv + 1]
    h = both[:, :dv] / jnp.maximum(jnp.abs(den), jnp.exp(-m_t))
    g_s = total - bc + li_c
    m_next = jnp.maximum(total + m_st, jnp.max(g_s, axis=0, keepdims=True))
    a_prev = jnp.exp(total + m_st - m_next)
    kw = k.astype(F32) * jnp.exp(g_s - m_next)
    c_ref[idx] = a_prev * c_st + lax.dot_general(kw.astype(BF16), v, (((0,), (0,)), ((), ())),
                                                 preferred_element_type=F32)
    m_ref[idx:idx + 1, :] = jnp.broadcast_to(m_next, (1, m_ref.shape[1]))
    return h


def _mlstm_body(qkf_ref, vf_ref, gcf_ref, grf_ref, qkb_ref, vb_ref, gcb_ref, grb_ref,
                hf_ref, hb_ref, c_ref, m_ref, *, heads, dk, dv):
    @pl.when(pl.program_id(1) == 0)
    def _():
        c_ref[...] = jnp.zeros_like(c_ref)
        m_ref[...] = jnp.zeros_like(m_ref)

    lc = qkf_ref.shape[1]
    ones_blk = jnp.where(lax.broadcasted_iota(I32, (lc, 128), 1) == 0, 1.0, 0.0).astype(BF16)
    t_i = lax.broadcasted_iota(I32, (lc, lc), 0)
    s_i = lax.broadcasted_iota(I32, (lc, lc), 1)
    lower = s_i <= t_i
    upper = s_i >= t_i
    ltri = jnp.where(lower, 1.0, 0.0).astype(BF16)
    utri = jnp.where(upper, 1.0, 0.0).astype(BF16)

    for bb in range(qkf_ref.shape[0]):
        for direction, (qk_ref, v_ref, gc_ref, gr_ref, h_ref) in enumerate(
                ((qkf_ref, vf_ref, gcf_ref, grf_ref, hf_ref), (qkb_ref, vb_ref, gcb_ref, grb_ref, hb_ref))):
            fwd = direction == 0
            gc = gc_ref[bb]
            gr = gr_ref[bb]
            lfc, lfr = _log_sigmoid(gc), _log_sigmoid(gr)
            cum_c = _dot_split(ltri if fwd else utri, lfc, a_is_f32=False)
            cum_r = _dot_split(lfr, utri if fwd else ltri, a_is_f32=True)
            for hd in range(heads):
                gi = (0 if fwd else 2) * heads + hd
                gf = (1 if fwd else 3) * heads + hd
                bc, br = cum_c[:, gf:gf + 1], cum_r[gf:gf + 1, :]
                total = br[:, lc - 1:lc] if fwd else br[:, 0:1]
                q = qk_ref[bb, :, hd * dk:(hd + 1) * dk]
                k = qk_ref[bb, :, (heads + hd) * dk:(heads + hd + 1) * dk]
                v = jnp.concatenate([v_ref[bb, :, hd * dv:(hd + 1) * dv], ones_blk], axis=1)
                h = _mlstm_chain(q, k, v, bc, br, gr[gi:gi + 1, :], gc[:, gi:gi + 1], total,
                                 lower if fwd else upper, c_ref, m_ref, (bb * 2 + direction) * heads + hd)
                h_ref[bb, :, hd * dv:(hd + 1) * dv] = h.astype(h_ref.dtype)


def mlstm_scan(qk, u, v_blk, gcol, grow):
    b, s, w = qk.shape
    heads = ML_HEADS
    dk = w // (2 * heads)
    dv = 2 * dk
    lc = ML_CHUNK
    nc = s // lc
    ng = gcol.shape[-1]

    def fw(bi, j):
        return j

    def bw(bi, j):
        return nc - 1 - j

    nbs = ML_BATCH_PER_STEP if b % ML_BATCH_PER_STEP == 0 else 1

    def specs(pos):
        return [pl.BlockSpec((nbs, lc, w), lambda bi, j: (bi, pos(bi, j), 0)),
                pl.BlockSpec((nbs, lc, heads * dv), lambda bi, j: (bi, pos(bi, j), v_blk)),
                pl.BlockSpec((nbs, lc, ng), lambda bi, j: (bi, pos(bi, j), 0)),
                pl.BlockSpec((nbs, ng, lc), lambda bi, j: (bi, 0, pos(bi, j)))]

    hshape = jax.ShapeDtypeStruct((b, s, heads * dv), BF16)
    return pl.pallas_call(
        functools.partial(_mlstm_body, heads=heads, dk=dk, dv=dv), grid=(b // nbs, nc),
        in_specs=specs(fw) + specs(bw),
        out_specs=[pl.BlockSpec((nbs, lc, heads * dv), lambda bi, j: (bi, j, 0)),
                   pl.BlockSpec((nbs, lc, heads * dv), lambda bi, j: (bi, nc - 1 - j, 0))],
        out_shape=[hshape, hshape],
        scratch_shapes=[pltpu.VMEM((nbs * 2 * heads, dk, dv + 128), F32),
                        pltpu.VMEM((nbs * 2 * heads, 128), F32)],
        compiler_params=_params("parallel", "arbitrary"), name="mlstm_scan",
    )(qk, u, gcol, grow, qk, u, gcol, grow)


def _mlstm_out_body(res_ref, hf_ref, hb_ref, o_ref, g_ref, w_ref, out_ref, *, heads):
    hs = hf_ref[...].astype(F32) + hb_ref[...].astype(F32)
    dv = hs.shape[1] // heads
    g = g_ref[...]
    parts = []
    for hd in range(heads):
        seg = hs[:, hd * dv:(hd + 1) * dv]
        ms = jnp.mean(seg * seg, axis=-1, keepdims=True)
        parts.append(seg * lax.rsqrt(ms + EPS) * g[:, hd * dv:(hd + 1) * dv])
    a = jnp.concatenate(parts, axis=-1) * _sigmoid(o_ref[...].astype(F32))
    out_ref[...] = res_ref[...] + jnp.dot(a.astype(BF16), w_ref[...], preferred_element_type=F32)


def mlstm_out(res, hf, hb, u2d, o_blk, gain, w_out, tm=512):
    t, d = res.shape
    row = lambda i: (i, 0)
    return pl.pallas_call(
        functools.partial(_mlstm_out_body, heads=ML_HEADS), grid=(t // tm,),
        in_specs=[pl.BlockSpec((tm, d), row), pl.BlockSpec((tm, d), row), pl.BlockSpec((tm, d), row),
                  pl.BlockSpec((tm, d), lambda i: (i, o_blk)), pl.BlockSpec((1, d), lambda i: (0, 0)),
                  pl.BlockSpec((d, d), lambda i: (0, 0))],
        out_specs=pl.BlockSpec((tm, d), row), out_shape=jax.ShapeDtypeStruct((t, d), F32),
        compiler_params=_params("parallel"), name="mlstm_out",
    )(res, hf, hb, u2d, gain.reshape(1, d), w_out)


MOE_GATHER_BUFS = 3
ROUTE_ROWS = 128
EXPERT_ROW0 = 8


def _router_body(x_ref, g_ref, wt_ref, b_ref, o_ref, cnt_ref, run_ref):
    @pl.when(pl.program_id(0) == 0)
    def _():
        run_ref[...] = jnp.zeros_like(run_ref)

    x = x_ref[...]
    ms = jnp.mean(x * x, axis=-1, keepdims=True)
    xn = x * lax.rsqrt(ms + EPS) * g_ref[...]
    logit = lax.dot_general(wt_ref[...], xn, (((1,), (1,)), ((), ())), preferred_element_type=F32,
                            precision=lax.Precision.HIGHEST) + b_ref[...]
    rows = [logit[r:r + 1, :] for r in range(EXPERT_ROW0 + N_EXPERTS)]
    g_best, g_idx = rows[0], jnp.zeros_like(rows[0])
    for gi in range(1, N_GROUPS):
        better = rows[gi] > g_best
        g_best = jnp.where(better, rows[gi], g_best)
        g_idx = jnp.where(better, float(gi), g_idx)
    g_den = sum(jnp.exp(rows[gi] - g_best) for gi in range(N_GROUPS))
    g_w = 1.0 / g_den
    sel = []
    for e in range(EXPERTS_PER_GROUP):
        v = rows[EXPERT_ROW0 + e]
        for gi in range(1, N_GROUPS):
            v = jnp.where(g_idx == float(gi), rows[EXPERT_ROW0 + gi * EXPERTS_PER_GROUP + e], v)
        sel.append(v)
    v1, i1 = sel[0], jnp.zeros_like(sel[0])
    for e in range(1, EXPERTS_PER_GROUP):
        better = sel[e] > v1
        v1 = jnp.where(better, sel[e], v1)
        i1 = jnp.where(better, float(e), i1)
    v2, i2 = jnp.full_like(v1, -jnp.inf), jnp.zeros_like(v1)
    for e in range(EXPERTS_PER_GROUP):
        better = (sel[e] > v2) & (i1 != float(e))
        v2 = jnp.where(better, sel[e], v2)
        i2 = jnp.where(better, float(e), i2)
    e21 = jnp.exp(v2 - v1)
    gate1 = g_w / (1.0 + e21)
    gate2 = gate1 * e21
    base = g_idx * float(EXPERTS_PER_GROUP)
    e1, e2 = base + i1, base + i2
    tm = e1.shape[1]
    erow = lax.broadcasted_iota(I32, (N_EXPERTS, tm), 0).astype(F32)
    oh1 = jnp.where(erow == e1, 1.0, 0.0)
    oh2 = jnp.where(erow == e2, 1.0, 0.0)
    cnt = oh1 + oh2
    earlier = jnp.where(lax.broadcasted_iota(I32, (tm, tm), 0) < lax.broadcasted_iota(I32, (tm, tm), 1),
                        1.0, 0.0).astype(BF16)
    run = run_ref[...]
    pos = run[:, 0:1] + jnp.dot(cnt.astype(BF16), earlier, preferred_element_type=F32)
    rank1 = jnp.sum(oh1 * pos, axis=0, keepdims=True)
    rank2 = jnp.sum(oh2 * pos, axis=0, keepdims=True)
    run = run + jnp.sum(cnt, axis=1, keepdims=True)
    run_ref[...] = run
    cnt_ref[...] = run
    zero = jnp.zeros_like(v1)
    o_ref[...] = jnp.concatenate([e1, e2, gate1, gate2, rank1, rank2, zero, zero], axis=0)


def moe_router(x2d, g, wg, bg, we, be, tm=512):
    t, d = x2d.shape
    wt = jnp.zeros((ROUTE_ROWS, d), F32).at[:N_GROUPS].set(wg.T).at[EXPERT_ROW0:EXPERT_ROW0 + N_EXPERTS].set(we.T)
    bias = jnp.zeros((ROUTE_ROWS, 1), F32).at[:N_GROUPS, 0].set(bg).at[EXPERT_ROW0:EXPERT_ROW0 + N_EXPERTS, 0].set(be)
    return pl.pallas_call(
        _router_body, grid=(t // tm,),
        in_specs=[pl.BlockSpec((tm, d), lambda i: (i, 0)), pl.BlockSpec((1, d), lambda i: (0, 0)),
                  pl.BlockSpec((ROUTE_ROWS, d), lambda i: (0, 0)), pl.BlockSpec((ROUTE_ROWS, 1), lambda i: (0, 0))],
        out_specs=[pl.BlockSpec((8, tm), lambda i: (0, i)), pl.BlockSpec((N_EXPERTS, 128), lambda i: (0, 0))],
        out_shape=[jax.ShapeDtypeStruct((8, t), F32), jax.ShapeDtypeStruct((N_EXPERTS, 128), F32)],
        scratch_shapes=[pltpu.VMEM((N_EXPERTS, 128), F32)],
        compiler_params=_params("arbitrary"), name="moe_router",
    )(x2d, g.reshape(1, d), wt, bias)


def _expert_body(blk_e_ref, nused_ref, tok_ref, x_ref, g_ref, w1_ref, w3_ref, w2_ref, ys_ref,
                 xbuf_ref, sem_ref, w1b_ref, w3b_ref, w2b_ref, *, tm):
    i = pl.program_id(0)
    nused = nused_ref[0]
    last = nused - 1
    nbuf = xbuf_ref.shape[0]
    ahead = nbuf - 1
    slot = i % nbuf
    cur = blk_e_ref[jnp.minimum(i, last)]
    prev = blk_e_ref[jnp.minimum(jnp.maximum(i - 1, 0), last)]

    def row_copy(blk, sl, r):
        return pltpu.make_async_copy(x_ref.at[pl.ds(tok_ref[blk * tm + r], 1), :],
                                     xbuf_ref.at[sl, pl.ds(r, 1), :], sem_ref.at[sl])

    def wait_rows(sl):
        pltpu.make_async_copy(x_ref.at[pl.ds(0, tm), :], xbuf_ref.at[sl], sem_ref.at[sl]).wait()

    @pl.when(i == 0)
    def _():
        for blk in range(ahead):
            def first(r, c, blk=blk):
                row_copy(blk, blk, r).start()
                return c
            lax.fori_loop(0, tm, first, 0, unroll=8)

    @pl.when((i == 0) | (cur != prev))
    def _():
        w1b_ref[...] = w1_ref[...].astype(BF16)
        w3b_ref[...] = w3_ref[...].astype(BF16)
        w2b_ref[...] = w2_ref[...].astype(BF16)

    @pl.when(i < nused)
    def _():
        nxt = (i + ahead) % nbuf
        for r in range(tm):
            row_copy(i + ahead, nxt, r).start()
        wait_rows(slot)
        x = xbuf_ref[slot]
        ms = jnp.mean(x * x, axis=-1, keepdims=True)
        xn = (x * lax.rsqrt(ms + EPS) * g_ref[...]).astype(BF16)
        h1 = jnp.dot(xn, w1b_ref[...], preferred_element_type=F32)
        h3 = jnp.dot(xn, w3b_ref[...], preferred_element_type=F32)
        hid = (h1 * _sigmoid(h1) * h3).astype(BF16)
        ys_ref[...] = jnp.dot(hid, w2b_ref[...], preferred_element_type=F32)

    @pl.when((i >= nused) & (i < nused + ahead))
    def _():
        wait_rows(slot)

    @pl.when(i >= nused)
    def _():
        ys_ref[...] = jnp.zeros_like(ys_ref)


def moe_experts(x2d, g, src_tok, blk_e, nused, w1, w3, w2, layer, tm):
    t, d = x2d.shape
    cap = src_tok.shape[0]
    de = w1.shape[-1]

    def wspec(shape):
        return pl.BlockSpec((None, None) + shape,
                            lambda i, be, nu, tok: (layer, be[jnp.minimum(i, nu[0] - 1)], 0, 0))

    grid_spec = pltpu.PrefetchScalarGridSpec(
        num_scalar_prefetch=3, grid=(cap // tm,),
        in_specs=[pl.BlockSpec(memory_space=pl.ANY), pl.BlockSpec((1, d), lambda i, be, nu, tok: (0, 0)),
                  wspec((d, de)), wspec((d, de)), wspec((de, d))],
        out_specs=pl.BlockSpec((tm, d), lambda i, be, nu, tok: (i, 0)),
        scratch_shapes=[pltpu.VMEM((MOE_GATHER_BUFS, tm, d), F32), pltpu.SemaphoreType.DMA((MOE_GATHER_BUFS,)),
                        pltpu.VMEM((d, de), BF16), pltpu.VMEM((d, de), BF16), pltpu.VMEM((de, d), BF16)])
    return pl.pallas_call(
        functools.partial(_expert_body, tm=tm), grid_spec=grid_spec,
        out_shape=jax.ShapeDtypeStruct((cap, d), F32),
        compiler_params=_params("arbitrary", disable_bounds_checks=True), name="moe_experts",
    )(blk_e, nused, src_tok, x2d, g.reshape(1, d), w1, w3, w2)


def _combine_body(dest_ref, x_ref, gate_ref, ys_ref, o_ref, buf_ref, sem_ref, *, tc):
    i = pl.program_id(0)
    n = pl.num_programs(0)
    slot = i % 2

    def row_copy(step, sl, r, kk):
        return pltpu.make_async_copy(ys_ref.at[pl.ds(dest_ref[(step * tc + r) * 2 + kk], 1), :],
                                     buf_ref.at[sl, kk, pl.ds(r, 1), :], sem_ref.at[sl])

    def issue_step(step, sl):
        def body(r, c):
            row_copy(step, sl, r, 0).start()
            row_copy(step, sl, r, 1).start()
            return c
        lax.fori_loop(0, tc, body, 0, unroll=8)

    @pl.when(i == 0)
    def _():
        issue_step(0, 0)

    @pl.when(i + 1 < n)
    def _():
        issue_step(i + 1, 1 - slot)

    for kk in range(2):
        pltpu.make_async_copy(ys_ref.at[pl.ds(0, tc), :], buf_ref.at[slot, kk], sem_ref.at[slot]).wait()

    gate = gate_ref[...]
    o_ref[...] = x_ref[...] + gate[:, 0:1] * buf_ref[slot, 0] + gate[:, 1:2] * buf_ref[slot, 1]


def moe_combine(x2d, gates, ys, dest, tc=256):
    t, d = x2d.shape
    grid_spec = pltpu.PrefetchScalarGridSpec(
        num_scalar_prefetch=1, grid=(t // tc,),
        in_specs=[pl.BlockSpec((tc, d), lambda i, dest: (i, 0)), pl.BlockSpec((tc, 2), lambda i, dest: (i, 0)),
                  pl.BlockSpec(memory_space=pl.ANY)],
        out_specs=pl.BlockSpec((tc, d), lambda i, dest: (i, 0)),
        scratch_shapes=[pltpu.VMEM((2, 2, tc, d), F32), pltpu.SemaphoreType.DMA((2,))])
    return pl.pallas_call(
        functools.partial(_combine_body, tc=tc), grid_spec=grid_spec,
        out_shape=jax.ShapeDtypeStruct((t, d), F32),
        compiler_params=_params("arbitrary", disable_bounds_checks=True), name="moe_combine",
    )(dest, x2d, gates, ys)


def hier_moe_residual(x2d, g, wg, bg, we, be, w1, w3, w2, layer, tm=256):
    t, d = x2d.shape
    route, counts_b = moe_router(x2d, g, wg, bg, we, be)
    eid = route[0:2].T.astype(I32).reshape(-1)
    rank = route[4:6].T.astype(I32).reshape(-1)
    gates = route[2:4].T
    counts = counts_b[:, 0].astype(I32)
    padded = (counts + tm - 1) // tm * tm
    pad_end = jnp.cumsum(padded)
    pad_start = pad_end - padded
    experts = jnp.arange(N_EXPERTS, dtype=I32)
    dest = rank + jnp.sum(jnp.where(eid[:, None] == experts[None, :], pad_start[None, :], 0), axis=1)
    cap = t * 2 + (N_EXPERTS + MOE_GATHER_BUFS - 2) * tm
    nblk = cap // tm
    blk_row0 = jnp.arange(nblk, dtype=I32) * tm
    blk_e = jnp.minimum(jnp.sum((pad_end[None, :] <= blk_row0[:, None]).astype(I32), axis=1), N_EXPERTS - 1)
    nused = (pad_end[-1:] // tm).astype(I32)
    nslot = 2 * t
    slot_sorted = jnp.sort(eid * nslot + jnp.arange(nslot, dtype=I32)) % nslot
    starts = jnp.cumsum(counts) - counts
    blk_first = starts[blk_e] + blk_row0 - pad_start[blk_e]
    blk_last = starts[blk_e] + counts[blk_e] - 1
    pos = jnp.minimum(blk_first[:, None] + jnp.arange(tm, dtype=I32)[None, :], blk_last[:, None]).reshape(-1)
    src_tok = slot_sorted[jnp.clip(pos, 0, nslot - 1)] // 2
    ys = moe_experts(x2d, g, src_tok.astype(I32), blk_e, nused, w1, w3, w2, layer, tm)
    return moe_combine(x2d, gates, ys, dest.astype(I32))


def _even_layer(x2d, b, s, layer, mix_g, w_in, q_norm, k_norm, lam_q1, lam_k1, lam_q2, lam_k2, subln,
                hy_conv_w, hy_conv_b, f_w1, f_b1, f_freq, f_w2, f_b2, f_w3, hy_skip, w_out):
    d = x2d.shape[1]
    d_att = d // 2
    (u2d,) = norm_matmul(x2d, mix_g, [w_in.astype(BF16)], [BF16])
    u = u2d.reshape(b, s, -1)
    lambda_init = 0.8 - 0.6 * math.exp(-0.3 * layer)
    lam = jnp.exp(jnp.sum(lam_q1 * lam_k1)) - jnp.exp(jnp.sum(lam_q2 * lam_k2)) + lambda_init
    dk = q_norm.shape[0]
    qt = head_prep(u, 0, q_norm, scale=dk ** -0.5 * math.log2(math.e), transpose=True)
    kp = head_prep(u, ATT_HEADS, k_norm)
    vt = head_prep(u, 2 * ATT_HEADS, transpose=True)
    y_att = diff_attention(qt, kp, vt, lam, subln, lambda_init)
    z, x1c, zp = hy_prep(u, 3 * d_att, hy_conv_w, hy_conv_b)
    tables = fft_tables(s)
    kf = filter_spectrum(hyena_filter(s, f_w1, f_b1, f_freq, f_w2, f_b2, f_w3), tables[0], tables[1])
    y_hy = hy_fft_conv(zp, z, x1c, kf, tables, hy_skip)
    w_out = w_out.astype(BF16)
    return matmul_residual(x2d, [y_att.reshape(b * s, -1), y_hy.reshape(b * s, -1)],
                           [w_out[:d_att], w_out[d_att:]])


def _odd_layer(x2d, b, s, mix_g, w_in, conv_w, conv_b, gate_b, out_norm, w_out):
    d = x2d.shape[1]
    qk_w = conv_w.shape[1]
    main_w = qk_w + 2 * d
    ng = 4 * ML_HEADS
    w_gate = jnp.zeros((d, 128), F32).at[:, :ng].set(w_in[:, main_w:]).astype(BF16)
    u2d, ug = norm_matmul(x2d, mix_g, [w_in[:, :main_w].astype(BF16), w_gate], [BF16, F32])
    u = u2d.reshape(b, s, main_w)
    gcol = (ug[:, :ng] + gate_b).reshape(b, s, ng)
    grow = jnp.swapaxes(gcol, 1, 2)
    dk = qk_w // (2 * ML_HEADS)
    col_scale = jnp.concatenate([jnp.full((qk_w // 2,), dk ** -0.5, F32), jnp.ones((qk_w // 2,), F32)])
    qk = ml_prep(u, conv_w, conv_b, col_scale)
    hf, hb = mlstm_scan(qk, u, qk_w // d, gcol, grow)
    return mlstm_out(x2d, hf.reshape(b * s, d), hb.reshape(b * s, d), u2d, (qk_w + d) // d, out_norm,
                     w_out.astype(BF16))


def kernel(x, mix_norm, ffn_norm, ev_w_in, ev_q_norm, ev_k_norm, ev_lam_q1, ev_lam_k1, ev_lam_q2, ev_lam_k2, ev_subln, ev_hy_conv_w, ev_hy_conv_b, ev_hy_f_w1, ev_hy_f_b1, ev_hy_f_freq, ev_hy_f_w2, ev_hy_f_b2, ev_hy_f_w3, ev_hy_skip, ev_w_out, od_w_in, od_conv_w, od_conv_b, od_gate_b, od_out_norm, od_w_out, moe_wg, moe_bg, moe_we, moe_be, moe_w1, moe_w3, moe_w2):
    b, s, d = x.shape
    depth = mix_norm.shape[0]
    x2d = x.reshape(b * s, d)
    for layer in range(depth):
        j = layer // 2
        if layer % 2 == 0:
            x2d = _even_layer(x2d, b, s, layer, mix_norm[layer], ev_w_in[j], ev_q_norm[j], ev_k_norm[j],
                              ev_lam_q1[j], ev_lam_k1[j], ev_lam_q2[j], ev_lam_k2[j], ev_subln[j],
                              ev_hy_conv_w[j], ev_hy_conv_b[j], ev_hy_f_w1[j], ev_hy_f_b1[j], ev_hy_f_freq[j],
                              ev_hy_f_w2[j], ev_hy_f_b2[j], ev_hy_f_w3[j], ev_hy_skip[j], ev_w_out[j])
        else:
            x2d = _odd_layer(x2d, b, s, mix_norm[layer], od_w_in[j], od_conv_w[j], od_conv_b[j], od_gate_b[j],
                             od_out_norm[j], od_w_out[j])
        x2d = hier_moe_residual(x2d, ffn_norm[layer], moe_wg[layer], moe_bg[layer], moe_we[layer],
                                moe_be[layer], moe_w1, moe_w3, moe_w2, layer)
    return x2d.reshape(b, s, d)
```

```python
import functools
import math

import jax
import jax.numpy as jnp
from jax import lax
from jax.experimental import pallas as pl
from jax.experimental.pallas import tpu as pltpu

F32 = jnp.float32
BF16 = jnp.bfloat16
I32 = jnp.int32

EPS = 1e-6
ROPE_THETA = 500000.0
ATT_HEADS = 4
ML_HEADS = 4
ML_CHUNK = 128
ML_BATCH_PER_STEP = 2
N_GROUPS = 4
EXPERTS_PER_GROUP = 8
N_EXPERTS = N_GROUPS * EXPERTS_PER_GROUP
HY_EMB_BANDS = 16
HY_MIN_DECAY = math.log(1e-2) / 1.5
HY_MAX_DECAY = math.log(1e-2) / 0.3

V7X_VMEM_BYTES = 64 * 1024 * 1024
VMEM_LIMIT = V7X_VMEM_BYTES - 8 * 1024 * 1024
NEG_BIG = -1e30


def _params(*sem, **kw):
    return pltpu.CompilerParams(dimension_semantics=sem, vmem_limit_bytes=VMEM_LIMIT, **kw)


def _sigmoid(x):
    return 1.0 / (1.0 + jnp.exp(-x))


def _norm_matmul_body(x_ref, g_ref, *refs, n_out, col_chunk):
    w_refs, o_refs = refs[:n_out], refs[n_out:]
    x = x_ref[...]
    ms = jnp.mean(x * x, axis=-1, keepdims=True)
    hn = (x * lax.rsqrt(ms + EPS) * g_ref[...]).astype(BF16)
    for w_ref, o_ref in zip(w_refs, o_refs):
        n = w_ref.shape[1]
        for c in range(0, n, col_chunk):
            ce = min(n, c + col_chunk)
            o_ref[:, c:ce] = jnp.dot(hn, w_ref[:, c:ce], preferred_element_type=F32).astype(o_ref.dtype)


def norm_matmul(x2d, g, ws, out_dtypes, tm=512):
    t, d = x2d.shape
    in_specs = [pl.BlockSpec((tm, d), lambda i: (i, 0)), pl.BlockSpec((1, d), lambda i: (0, 0))]
    in_specs += [pl.BlockSpec(w.shape, lambda i: (0, 0)) for w in ws]
    out_specs = [pl.BlockSpec((tm, w.shape[1]), lambda i: (i, 0)) for w in ws]
    out_shape = [jax.ShapeDtypeStruct((t, w.shape[1]), dt) for w, dt in zip(ws, out_dtypes)]
    return pl.pallas_call(
        functools.partial(_norm_matmul_body, n_out=len(ws), col_chunk=1024),
        grid=(t // tm,), in_specs=in_specs, out_specs=out_specs, out_shape=out_shape,
        compiler_params=_params("parallel"), name="norm_matmul",
    )(x2d, g.reshape(1, d), *ws)


def _matmul_res_body(res_ref, *refs, n_in):
    a_refs, w_refs, o_ref = refs[:n_in], refs[n_in:2 * n_in], refs[2 * n_in]
    acc = res_ref[...]
    for a_ref, w_ref in zip(a_refs, w_refs):
        acc = acc + jnp.dot(a_ref[...], w_ref[...], preferred_element_type=F32)
    o_ref[...] = acc


def matmul_residual(res, a_list, w_list, tm=512):
    t, d = res.shape
    in_specs = [pl.BlockSpec((tm, d), lambda i: (i, 0))]
    in_specs += [pl.BlockSpec((tm, a.shape[1]), lambda i: (i, 0)) for a in a_list]
    in_specs += [pl.BlockSpec(w.shape, lambda i: (0, 0)) for w in w_list]
    return pl.pallas_call(
        functools.partial(_matmul_res_body, n_in=len(a_list)),
        grid=(t // tm,), in_specs=in_specs, out_specs=pl.BlockSpec((tm, d), lambda i: (i, 0)),
        out_shape=jax.ShapeDtypeStruct((t, d), F32),
        compiler_params=_params("parallel"), name="matmul_residual",
    )(res, *a_list, *w_list)


def _head_prep_body(u_ref, *refs, dk, rope, transpose):
    o_ref = refs[-1]
    x = u_ref[...].astype(F32)
    if rope:
        g_ref, c_ref, s1_ref, s2_ref = refs[:4]
        lane = lax.broadcasted_iota(I32, x.shape, 1)
        lo = lane < dk
        x2 = x * x
        s_lo = jnp.sum(jnp.where(lo, x2, 0.0), axis=-1, keepdims=True)
        s_hi = jnp.sum(jnp.where(lo, 0.0, x2), axis=-1, keepdims=True)
        ms = jnp.where(lo, s_lo, s_hi) * (1.0 / dk)
        y = x * lax.rsqrt(ms + EPS) * g_ref[...]
        x = y * c_ref[...] + pltpu.roll(y, 120, 1) * s1_ref[...] + pltpu.roll(y, 8, 1) * s2_ref[...]
    if transpose:
        x = x.T
    o_ref[...] = x.astype(o_ref.dtype)


def _rope_lane_tables(seq, dk, rope_dim, scale):
    half = rope_dim // 2
    inv_freq = 1.0 / (ROPE_THETA ** (jnp.arange(0, rope_dim, 2, dtype=F32) / rope_dim))
    ang = jnp.arange(seq, dtype=F32)[:, None] * inv_freq[None, :]
    cos, sin = jnp.cos(ang), jnp.sin(ang)
    d = jnp.arange(2 * dk) % dk
    fi = d % half
    c_tab = jnp.where(d[None, :] < rope_dim, cos[:, fi], 1.0)
    s1_tab = jnp.where(d[None, :] < half, -sin[:, fi], 0.0)
    s2_tab = jnp.where((d[None, :] >= half) & (d[None, :] < rope_dim), sin[:, fi], 0.0)
    return (jnp.stack([c_tab, s1_tab, s2_tab]) * scale).astype(F32)


def head_prep(u, blk0, norm_gain=None, scale=1.0, transpose=False):
    b, s, _ = u.shape
    h = ATT_HEADS
    rope = norm_gain is not None
    in_specs = [pl.BlockSpec((None, s, 128), lambda c, bi: (bi, 0, blk0 + c))]
    args = [u]
    dk = 64
    if rope:
        dk = norm_gain.shape[0]
        assert 2 * dk == 128 and dk // 4 == 16, "rope roll shifts assume 64-wide components, 16 rotary dims"
        tabs = _rope_lane_tables(s, dk, dk // 4, scale)
        in_specs += [pl.BlockSpec((1, 128), lambda c, bi: (0, 0))] + [pl.BlockSpec((s, 128), lambda c, bi: (0, 0))] * 3
        args += [jnp.tile(norm_gain, 2).reshape(1, 128).astype(F32), tabs[0], tabs[1], tabs[2]]
    if transpose:
        out_spec = pl.BlockSpec((None, 128, s), lambda c, bi: (bi, c, 0))
        out_shape = jax.ShapeDtypeStruct((b, h * 128, s), BF16)
    else:
        out_spec = pl.BlockSpec((None, s, 128), lambda c, bi: (bi, 0, c))
        out_shape = jax.ShapeDtypeStruct((b, s, h * 128), BF16)
    return pl.pallas_call(
        functools.partial(_head_prep_body, dk=dk, rope=rope, transpose=transpose), grid=(h, b),
        in_specs=in_specs, out_specs=out_spec, out_shape=out_shape,
        compiler_params=_params("parallel", "parallel"), name="head_prep",
    )(*args)


def _attn_body(lam_ref, qt_ref, k_ref, vt_ref, g_ref, o_ref, *, tq, dk, post_scale, n_split):
    lam = lam_ref[0, 0]
    th = tq // n_split
    for part in range(n_split):
        qt = qt_ref[:, part * th:(part + 1) * th]
        row = lax.broadcasted_iota(I32, qt.shape, 0)
        zero = jnp.zeros_like(qt)
        qq = jnp.concatenate([jnp.where(row < dk, qt, zero), jnp.where(row < dk, zero, qt)], axis=1)
        st = jnp.dot(k_ref[...], qq, preferred_element_type=F32)
        m = jnp.max(st, axis=0, keepdims=True)
        p = jnp.exp2(st - m)
        r = 1.0 / jnp.sum(p, axis=0, keepdims=True)
        ot = jnp.dot(vt_ref[...], p.astype(BF16), preferred_element_type=F32)
        o = (ot[:, :th] * r[:, :th] - ot[:, th:] * (lam * r[:, th:])).T
        ms = jnp.mean(o * o, axis=-1, keepdims=True)
        o_ref[part * th:(part + 1) * th, :] = (o * lax.rsqrt(ms + EPS) * g_ref[...] * post_scale).astype(o_ref.dtype)


def diff_attention(qt, k, vt, lam, subln, lambda_init, tq=256):
    b, s, _ = k.shape
    h = ATT_HEADS
    return pl.pallas_call(
        functools.partial(_attn_body, tq=tq, dk=64, post_scale=1.0 - lambda_init, n_split=1),
        grid=(b, h, s // tq),
        in_specs=[pl.BlockSpec(memory_space=pltpu.SMEM),
                  pl.BlockSpec((None, 128, tq), lambda bi, hi, i: (bi, hi, i)),
                  pl.BlockSpec((None, s, 128), lambda bi, hi, i: (bi, 0, hi)),
                  pl.BlockSpec((None, 128, s), lambda bi, hi, i: (bi, hi, 0)),
                  pl.BlockSpec((1, 128), lambda bi, hi, i: (0, 0))],
        out_specs=pl.BlockSpec((None, tq, 128), lambda bi, hi, i: (bi, i, hi)),
        out_shape=jax.ShapeDtypeStruct((b, s, h * 128), BF16),
        compiler_params=_params("parallel", "parallel", "parallel"), name="diff_attention",
    )(lam.reshape(1, 1).astype(F32), qt, k, vt, subln.reshape(1, 128).astype(F32))


def _conv3(u_ref, w_ref, b_ref):
    x = u_ref[...].astype(F32)
    s = x.shape[0]
    row = lax.broadcasted_iota(I32, x.shape, 0)
    x_prev = jnp.where(row == 0, 0.0, pltpu.roll(x, 1, 0))
    x_next = jnp.where(row == s - 1, 0.0, pltpu.roll(x, s - 1, 0))
    w = w_ref[...]
    return b_ref[...] + x_prev * w[0:1] + x * w[1:2] + x_next * w[2:3]


FFT_N1 = 64
FFT_UNROLL = 8
FFT_PAD = 8


def _hy_prep_body(x1_ref, x2_ref, v_ref, w1_ref, w2_ref, wv_ref, b1_ref, b2_ref, bv_ref, z_ref, x1c_ref, zp_ref):
    x1c_ref[...] = _conv3(x1_ref, w1_ref, b1_ref).astype(x1c_ref.dtype)
    z = _conv3(v_ref, wv_ref, bv_ref) * _conv3(x2_ref, w2_ref, b2_ref)
    z_ref[...] = z.astype(z_ref.dtype)
    nb = z.shape[0] // FFT_N1
    zp_ref[...] = jnp.zeros_like(zp_ref)
    for n2 in range(nb):
        for ci in range(z.shape[1] // 128):
            zp_ref[ci, pl.ds(n2, FFT_N1, stride=nb + FFT_PAD), :] = (
                z[n2 * FFT_N1:(n2 + 1) * FFT_N1, ci * 128:(ci + 1) * 128])


def hy_prep(u, col0, conv_w, conv_b, tc=256):
    b, s, _ = u.shape
    d_hy = conv_w.shape[1] // 3
    nct = d_hy // tc
    blk0 = col0 // tc
    sp = FFT_N1 * (s // FFT_N1 + FFT_PAD)

    def uspec(part):
        return pl.BlockSpec((None, s, tc), lambda bi, c: (bi, 0, blk0 + part * nct + c))

    def wspec(part, rows):
        return pl.BlockSpec((rows, tc), lambda bi, c: (0, part * nct + c))

    ospec = pl.BlockSpec((None, s, tc), lambda bi, c: (bi, 0, c))
    return pl.pallas_call(
        _hy_prep_body, grid=(b, nct),
        in_specs=[uspec(0), uspec(1), uspec(2), wspec(0, 3), wspec(1, 3), wspec(2, 3),
                  wspec(0, 1), wspec(1, 1), wspec(2, 1)],
        out_specs=[ospec, ospec, pl.BlockSpec((None, tc // 128, sp, 128), lambda bi, c: (bi, c, 0, 0))],
        out_shape=[jax.ShapeDtypeStruct((b, s, d_hy), BF16), jax.ShapeDtypeStruct((b, s, d_hy), BF16),
                   jax.ShapeDtypeStruct((b, d_hy // 128, sp, 128), F32)],
        compiler_params=_params("parallel", "parallel"), name="hy_prep",
    )(u, u, u, conv_w, conv_w, conv_w, conv_b.reshape(1, -1), conv_b.reshape(1, -1), conv_b.reshape(1, -1))


def hyena_filter(length, w1, b1, freq, w2, b2, w3):
    d_hy = w3.shape[1] // 2
    t = jnp.linspace(0.0, 1.0, length, dtype=F32)[:, None]
    bands = jnp.linspace(1e-4, HY_EMB_BANDS - 1, HY_EMB_BANDS, dtype=F32)[None, :]
    ang = (2.0 * math.pi / length) * jnp.arange(length, dtype=F32)[:, None] * bands
    z = jnp.concatenate([t, jnp.cos(ang), -jnp.sin(ang)], axis=-1)
    hp = lax.Precision.HIGHEST
    hdn = jnp.sin(freq * (jnp.dot(z, w1, precision=hp) + b1))
    hdn = jnp.sin(freq * (jnp.dot(hdn, w2, precision=hp) + b2))
    filt = jnp.dot(hdn, w3, precision=hp)
    deltas = jnp.abs(jnp.linspace(HY_MIN_DECAY, HY_MAX_DECAY, d_hy, dtype=F32))
    decay = jnp.exp(-t * deltas[None, :])
    h_fwd = filt[:, :d_hy] * decay
    h_bwd = filt[:, d_hy:] * decay
    h_fwd = h_fwd.at[0].add(h_bwd[0])
    h_bwd = h_bwd.at[0].set(0.0)
    norm = jnp.sum(jnp.abs(h_fwd), axis=0, keepdims=True) + jnp.sum(jnp.abs(h_bwd), axis=0, keepdims=True) + EPS
    return jnp.concatenate([h_fwd / norm, h_bwd / norm], axis=1)


def fft_tables(length):
    n = 2 * length
    n1c, n2c, nb = FFT_N1, n // FFT_N1, length // FFT_N1
    unit = 2.0 * math.pi / n
    i1 = jnp.arange(n1c, dtype=I32)
    i2 = jnp.arange(n2c, dtype=I32)
    ib = jnp.arange(nb, dtype=I32)
    samp = i1[:, None, None] + n1c * ib[None, None, :]
    ang = ((i2[None, :, None] * samp) % n).astype(F32) * unit
    m1 = jnp.concatenate([jnp.cos(ang), -jnp.sin(ang)], axis=1)
    ang = ((i1[:, None] * i1[None, :]) % n1c).astype(F32) * (2.0 * math.pi / n1c)
    c, s = jnp.cos(ang), jnp.sin(ang)
    f1 = jnp.concatenate([jnp.concatenate([c, s], axis=1), jnp.concatenate([-s, c], axis=1)], axis=0)
    freq = n2c * i1[None, None, :] + i2[:, None, None]
    ang = ((i1[None, :, None] * freq) % n).astype(F32) * unit
    c, s = jnp.cos(ang), jnp.sin(ang)
    g1 = jnp.concatenate([jnp.concatenate([c, -s], axis=2), jnp.concatenate([s, c], axis=2)], axis=1)
    ang = ((ib[:, None] * i2[None, :]) % n2c).astype(F32) * (2.0 * math.pi / n2c)
    g2 = jnp.concatenate([jnp.cos(ang), -jnp.sin(ang)], axis=1) * (1.0 / n)
    return m1.astype(BF16), f1.astype(BF16), g1.astype(BF16), g2.astype(BF16)


def _fft_stage1(xp_ref, m1_ref, p_ref):
    nb = m1_ref.shape[2]
    n2c = m1_ref.shape[1] // 2

    def body(n1, c):
        x = xp_ref[pl.ds(pl.multiple_of(n1 * (nb + FFT_PAD), 8), nb), :].astype(BF16)
        a = jnp.dot(m1_ref[n1], x, preferred_element_type=F32)
        p_ref[0, pl.ds(n1, n2c, stride=FFT_N1 + FFT_PAD), :] = a[:n2c]
        p_ref[1, pl.ds(n1, n2c, stride=FFT_N1 + FFT_PAD), :] = a[n2c:]
        return c
    lax.fori_loop(0, FFT_N1, body, 0, unroll=FFT_UNROLL)


def _fft_stage2(p_ref, f1_ref, k2):
    r0 = pl.multiple_of(k2 * (FFT_N1 + FFT_PAD), 8)
    slab = jnp.concatenate([p_ref[0, pl.ds(r0, FFT_N1), :], p_ref[1, pl.ds(r0, FFT_N1), :]], axis=0)
    return jnp.dot(f1_ref[...], slab.astype(BF16), preferred_element_type=F32)


def _spectrum_body(xp_ref, m1_ref, f1_ref, o_ref, p_ref):
    _fft_stage1(xp_ref, m1_ref, p_ref)

    def body(k2, c):
        o_ref[k2] = _fft_stage2(p_ref, f1_ref, k2)
        return c
    lax.fori_loop(0, o_ref.shape[0], body, 0, unroll=FFT_UNROLL)


def filter_spectrum(ab, m1, f1):
    length, c2 = ab.shape
    nb = length // FFT_N1
    n2c = m1.shape[1] // 2
    nch = c2 // 128
    abp = jnp.pad(ab.reshape(nb, FFT_N1, nch, 128).transpose(2, 1, 0, 3), ((0, 0), (0, 0), (0, FFT_PAD), (0, 0)))
    abp = abp.reshape(nch, FFT_N1 * (nb + FFT_PAD), 128)
    spec = pl.pallas_call(
        _spectrum_body, grid=(nch,),
        in_specs=[pl.BlockSpec((None,) + abp.shape[1:], lambda c: (c, 0, 0)),
                  pl.BlockSpec(m1.shape, lambda c: (0, 0, 0)), pl.BlockSpec(f1.shape, lambda c: (0, 0))],
        out_specs=pl.BlockSpec((None, n2c, 2 * FFT_N1, 128), lambda c: (c, 0, 0, 0)),
        out_shape=jax.ShapeDtypeStruct((nch, n2c, 2 * FFT_N1, 128), F32),
        scratch_shapes=[pltpu.VMEM((2, n2c * (FFT_N1 + FFT_PAD), 128), F32)],
        compiler_params=_params("parallel"), name="filter_spectrum",
    )(abp, m1, f1)
    fa, fb = spec[:nch // 2], spec[nch // 2:]
    h = FFT_N1
    return jnp.concatenate([fa[:, :, :h] + fb[:, :, :h], fa[:, :, h:] - fb[:, :, h:]], axis=2).astype(BF16)


def _hy_fft_body(zp_ref, z_ref, x1c_ref, kf_ref, m1_ref, f1_ref, g1_ref, g2_ref, skip_ref, o_ref,
                 p_ref, q_ref, y_ref):
    h = FFT_N1
    n2c = g1_ref.shape[0]
    nb = g2_ref.shape[0]
    _fft_stage1(zp_ref, m1_ref, p_ref)

    def mid(k2, c):
        xf = _fft_stage2(p_ref, f1_ref, k2)
        kf = kf_ref[k2].astype(F32)
        xr, xi, kr, ki = xf[:h], xf[h:], kf[:h], kf[h:]
        y = jnp.concatenate([xr * kr - xi * ki, xr * ki + xi * kr], axis=0).astype(BF16)
        d = jnp.dot(g1_ref[k2], y, preferred_element_type=F32)
        q_ref[0, pl.ds(k2, h, stride=n2c + FFT_PAD), :] = d[:h]
        q_ref[1, pl.ds(k2, h, stride=n2c + FFT_PAD), :] = d[h:]
        return c
    lax.fori_loop(0, n2c, mid, 0, unroll=FFT_UNROLL)

    def last(t1, c):
        r0 = pl.multiple_of(t1 * (n2c + FFT_PAD), 8)
        slab = jnp.concatenate([q_ref[0, pl.ds(r0, n2c), :], q_ref[1, pl.ds(r0, n2c), :]], axis=0)
        y_ref[pl.ds(t1, nb, stride=h), :] = jnp.dot(g2_ref[...], slab.astype(BF16), preferred_element_type=F32)
        return c
    lax.fori_loop(0, h, last, 0, unroll=FFT_UNROLL)

    z = z_ref[...].astype(F32)
    o_ref[...] = ((y_ref[...] + z * skip_ref[...]) * x1c_ref[...].astype(F32)).astype(o_ref.dtype)


def hy_fft_conv(zp, z, x1c, kf, tables, skip):
    m1, f1, g1, g2 = tables
    b, s, c = z.shape
    nch = c // 128

    def const(shape):
        return pl.BlockSpec(shape, lambda ci, bi: (0,) * len(shape))

    nat = pl.BlockSpec((None, s, 128), lambda ci, bi: (bi, 0, ci))
    return pl.pallas_call(
        _hy_fft_body, grid=(nch, b),
        in_specs=[pl.BlockSpec((None, None) + zp.shape[2:], lambda ci, bi: (bi, ci, 0, 0)), nat, nat,
                  pl.BlockSpec((None,) + kf.shape[1:], lambda ci, bi: (ci, 0, 0, 0)),
                  const(m1.shape), const(f1.shape), const(g1.shape), const(g2.shape),
                  pl.BlockSpec((1, 128), lambda ci, bi: (0, ci))],
        out_specs=nat,
        out_shape=jax.ShapeDtypeStruct((b, s, c), BF16),
        scratch_shapes=[pltpu.VMEM((2, g1.shape[0] * (FFT_N1 + FFT_PAD), 128), F32),
                        pltpu.VMEM((2, FFT_N1 * (g1.shape[0] + FFT_PAD), 128), F32),
                        pltpu.VMEM((s, 128), F32)],
        compiler_params=_params("parallel", "parallel"), name="hy_fft_conv",
    )(zp, z, x1c, kf, m1, f1, g1, g2, skip.reshape(1, c).astype(F32))


def _ml_prep_body(u_ref, w_ref, b_ref, sc_ref, o_ref):
    y = _conv3(u_ref, w_ref, b_ref)
    o_ref[...] = (y * _sigmoid(y) * sc_ref[...]).astype(o_ref.dtype)


def ml_prep(u, conv_w, conv_b, col_scale, tc=256):
    b, s, _ = u.shape
    w = conv_w.shape[1]
    return pl.pallas_call(
        _ml_prep_body, grid=(b, w // tc),
        in_specs=[pl.BlockSpec((None, s, tc), lambda bi, c: (bi, 0, c)),
                  pl.BlockSpec((3, tc), lambda bi, c: (0, c)),
                  pl.BlockSpec((1, tc), lambda bi, c: (0, c)),
                  pl.BlockSpec((1, tc), lambda bi, c: (0, c))],
        out_specs=pl.BlockSpec((None, s, tc), lambda bi, c: (bi, 0, c)),
        out_shape=jax.ShapeDtypeStruct((b, s, w), BF16),
        compiler_params=_params("parallel", "parallel"), name="ml_prep",
    )(u, conv_w, conv_b.reshape(1, w), col_scale.reshape(1, w))


def _log_sigmoid(x):
    return jnp.minimum(x, 0.0) - jnp.log(1.0 + jnp.exp(-jnp.abs(x)))


def _dot_split(a, b, a_is_f32):
    x = a if a_is_f32 else b
    hi = x.astype(BF16)
    lo = (x - hi.astype(F32)).astype(BF16)
    if a_is_f32:
        return (jnp.dot(hi, b, preferred_element_type=F32) + jnp.dot(lo, b, preferred_element_type=F32))
    return (jnp.dot(a, hi, preferred_element_type=F32) + jnp.dot(a, lo, preferred_element_type=F32))


def _mlstm_chain(q, k, v, bc, br, li_r, li_c, total, mask, c_ref, m_ref, idx):
    dv = v.shape[1] - 128
    c_st = c_ref[idx]
    m_st = m_ref[idx:idx + 1, 0:1]
    dmat = jnp.where(mask, bc - br + li_r, NEG_BIG)
    inter = bc + m_st
    m_t = jnp.maximum(inter, jnp.max(dmat, axis=-1, keepdims=True))
    w_intra = jnp.exp(dmat - m_t)
    w_inter = jnp.exp(inter - m_t)
    sc = lax.dot_general(q, k, (((1,), (1,)), ((), ())), preferred_element_type=F32) * w_intra
    both = (w_inter * jnp.dot(q, c_st.astype(BF16), preferred_element_type=F32)
            + jnp.dot(sc.astype(BF16), v, preferred_element_type=F32))
    den = both[:, dv:dv + 1]
    h = both[:, :dv] / jnp.maximum(jnp.abs(den), jnp.exp(-m_t))
    g_s = total - bc + li_c
    m_next = jnp.maximum(total + m_st, jnp.max(g_s, axis=0, keepdims=True))
    a_prev = jnp.exp(total + m_st - m_next)
    kw = k.astype(F32) * jnp.exp(g_s - m_next)
    c_ref[idx] = a_prev * c_st + lax.dot_general(kw.astype(BF16), v, (((0,), (0,)), ((), ())),
                                                 preferred_element_type=F32)
    m_ref[idx:idx + 1, :] = jnp.broadcast_to(m_next, (1, m_ref.shape[1]))
    return h


def _mlstm_body(qkf_ref, vf_ref, gcf_ref, grf_ref, qkb_ref, vb_ref, gcb_ref, grb_ref,
                hf_ref, hb_ref, c_ref, m_ref, *, heads, dk, dv):
    @pl.when(pl.program_id(1) == 0)
    def _():
        c_ref[...] = jnp.zeros_like(c_ref)
        m_ref[...] = jnp.zeros_like(m_ref)

    lc = qkf_ref.shape[1]
    ones_blk = jnp.where(lax.broadcasted_iota(I32, (lc, 128), 1) == 0, 1.0, 0.0).astype(BF16)
    t_i = lax.broadcasted_iota(I32, (lc, lc), 0)
    s_i = lax.broadcasted_iota(I32, (lc, lc), 1)
    lower = s_i <= t_i
    upper = s_i >= t_i
    ltri = jnp.where(lower, 1.0, 0.0).astype(BF16)
    utri = jnp.where(upper, 1.0, 0.0).astype(BF16)

    for bb in range(qkf_ref.shape[0]):
        for direction, (qk_ref, v_ref, gc_ref, gr_ref, h_ref) in enumerate(
                ((qkf_ref, vf_ref, gcf_ref, grf_ref, hf_ref), (qkb_ref, vb_ref, gcb_ref, grb_ref, hb_ref))):
            fwd = direction == 0
            gc = gc_ref[bb]
            gr = gr_ref[bb]
            lfc, lfr = _log_sigmoid(gc), _log_sigmoid(gr)
            cum_c = _dot_split(ltri if fwd else utri, lfc, a_is_f32=False)
            cum_r = _dot_split(lfr, utri if fwd else ltri, a_is_f32=True)
            for hd in range(heads):
                gi = (0 if fwd else 2) * heads + hd
                gf = (1 if fwd else 3) * heads + hd
                bc, br = cum_c[:, gf:gf + 1], cum_r[gf:gf + 1, :]
                total = br[:, lc - 1:lc] if fwd else br[:, 0:1]
                q = qk_ref[bb, :, hd * dk:(hd + 1) * dk]
                k = qk_ref[bb, :, (heads + hd) * dk:(heads + hd + 1) * dk]
                v = jnp.concatenate([v_ref[bb, :, hd * dv:(hd + 1) * dv], ones_blk], axis=1)
                h = _mlstm_chain(q, k, v, bc, br, gr[gi:gi + 1, :], gc[:, gi:gi + 1], total,
                                 lower if fwd else upper, c_ref, m_ref, (bb * 2 + direction) * heads + hd)
                h_ref[bb, :, hd * dv:(hd + 1) * dv] = h.astype(h_ref.dtype)


def mlstm_scan(qk, u, v_blk, gcol, grow):
    b, s, w = qk.shape
    heads = ML_HEADS
    dk = w // (2 * heads)
    dv = 2 * dk
    lc = ML_CHUNK
    nc = s // lc
    ng = gcol.shape[-1]

    def fw(bi, j):
        return j

    def bw(bi, j):
        return nc - 1 - j

    nbs = ML_BATCH_PER_STEP if b % ML_BATCH_PER_STEP == 0 else 1

    def specs(pos):
        return [pl.BlockSpec((nbs, lc, w), lambda bi, j: (bi, pos(bi, j), 0)),
                pl.BlockSpec((nbs, lc, heads * dv), lambda bi, j: (bi, pos(bi, j), v_blk)),
                pl.BlockSpec((nbs, lc, ng), lambda bi, j: (bi, pos(bi, j), 0)),
                pl.BlockSpec((nbs, ng, lc), lambda bi, j: (bi, 0, pos(bi, j)))]

    hshape = jax.ShapeDtypeStruct((b, s, heads * dv), BF16)
    return pl.pallas_call(
        functools.partial(_mlstm_body, heads=heads, dk=dk, dv=dv), grid=(b // nbs, nc),
        in_specs=specs(fw) + specs(bw),
        out_specs=[pl.BlockSpec((nbs, lc, heads * dv), lambda bi, j: (bi, j, 0)),
                   pl.BlockSpec((nbs, lc, heads * dv), lambda bi, j: (bi, nc - 1 - j, 0))],
        out_shape=[hshape, hshape],
        scratch_shapes=[pltpu.VMEM((nbs * 2 * heads, dk, dv + 128), F32),
                        pltpu.VMEM((nbs * 2 * heads, 128), F32)],
        compiler_params=_params("parallel", "arbitrary"), name="mlstm_scan",
    )(qk, u, gcol, grow, qk, u, gcol, grow)


def _mlstm_out_body(res_ref, hf_ref, hb_ref, o_ref, g_ref, w_ref, out_ref, *, heads):
    hs = hf_ref[...].astype(F32) + hb_ref[...].astype(F32)
    dv = hs.shape[1] // heads
    g = g_ref[...]
    parts = []
    for hd in range(heads):
        seg = hs[:, hd * dv:(hd + 1) * dv]
        ms = jnp.mean(seg * seg, axis=-1, keepdims=True)
        parts.append(seg * lax.rsqrt(ms + EPS) * g[:, hd * dv:(hd + 1) * dv])
    a = jnp.concatenate(parts, axis=-1) * _sigmoid(o_ref[...].astype(F32))
    out_ref[...] = res_ref[...] + jnp.dot(a.astype(BF16), w_ref[...], preferred_element_type=F32)


def mlstm_out(res, hf, hb, u2d, o_blk, gain, w_out, tm=512):
    t, d = res.shape
    row = lambda i: (i, 0)
    return pl.pallas_call(
        functools.partial(_mlstm_out_body, heads=ML_HEADS), grid=(t // tm,),
        in_specs=[pl.BlockSpec((tm, d), row), pl.BlockSpec((tm, d), row), pl.BlockSpec((tm, d), row),
                  pl.BlockSpec((tm, d), lambda i: (i, o_blk)), pl.BlockSpec((1, d), lambda i: (0, 0)),
                  pl.BlockSpec((d, d), lambda i: (0, 0))],
        out_specs=pl.BlockSpec((tm, d), row), out_shape=jax.ShapeDtypeStruct((t, d), F32),
        compiler_params=_params("parallel"), name="mlstm_out",
    )(res, hf, hb, u2d, gain.reshape(1, d), w_out)


MOE_GATHER_BUFS = 3
ROUTE_ROWS = 128
EXPERT_ROW0 = 8


def _router_body(x_ref, g_ref, wt_ref, b_ref, o_ref, cnt_ref, run_ref):
    @pl.when(pl.program_id(0) == 0)
    def _():
        run_ref[...] = jnp.zeros_like(run_ref)

    x = x_ref[...]
    ms = jnp.mean(x * x, axis=-1, keepdims=True)
    xn = x * lax.rsqrt(ms + EPS) * g_ref[...]
    logit = lax.dot_general(wt_ref[...], xn, (((1,), (1,)), ((), ())), preferred_element_type=F32,
                            precision=lax.Precision.HIGHEST) + b_ref[...]
    rows = [logit[r:r + 1, :] for r in range(EXPERT_ROW0 + N_EXPERTS)]
    g_best, g_idx = rows[0], jnp.zeros_like(rows[0])
    for gi in range(1, N_GROUPS):
        better = rows[gi] > g_best
        g_best = jnp.where(better, rows[gi], g_best)
        g_idx = jnp.where(better, float(gi), g_idx)
    g_den = sum(jnp.exp(rows[gi] - g_best) for gi in range(N_GROUPS))
    g_w = 1.0 / g_den
    sel = []
    for e in range(EXPERTS_PER_GROUP):
        v = rows[EXPERT_ROW0 + e]
        for gi in range(1, N_GROUPS):
            v = jnp.where(g_idx == float(gi), rows[EXPERT_ROW0 + gi * EXPERTS_PER_GROUP + e], v)
        sel.append(v)
    v1, i1 = sel[0], jnp.zeros_like(sel[0])
    for e in range(1, EXPERTS_PER_GROUP):
        better = sel[e] > v1
        v1 = jnp.where(better, sel[e], v1)
        i1 = jnp.where(better, float(e), i1)
    v2, i2 = jnp.full_like(v1, -jnp.inf), jnp.zeros_like(v1)
    for e in range(EXPERTS_PER_GROUP):
        better = (sel[e] > v2) & (i1 != float(e))
        v2 = jnp.where(better, sel[e], v2)
        i2 = jnp.where(better, float(e), i2)
    e21 = jnp.exp(v2 - v1)
    gate1 = g_w / (1.0 + e21)
    gate2 = gate1 * e21
    base = g_idx * float(EXPERTS_PER_GROUP)
    e1, e2 = base + i1, base + i2
    tm = e1.shape[1]
    erow = lax.broadcasted_iota(I32, (N_EXPERTS, tm), 0).astype(F32)
    oh1 = jnp.where(erow == e1, 1.0, 0.0)
    oh2 = jnp.where(erow == e2, 1.0, 0.0)
    cnt = oh1 + oh2
    earlier = jnp.where(lax.broadcasted_iota(I32, (tm, tm), 0) < lax.broadcasted_iota(I32, (tm, tm), 1),
                        1.0, 0.0).astype(BF16)
    run = run_ref[...]
    pos = run[:, 0:1] + jnp.dot(cnt.astype(BF16), earlier, preferred_element_type=F32)
    rank1 = jnp.sum(oh1 * pos, axis=0, keepdims=True)
    rank2 = jnp.sum(oh2 * pos, axis=0, keepdims=True)
    run = run + jnp.sum(cnt, axis=1, keepdims=True)
    run_ref[...] = run
    cnt_ref[...] = run
    zero = jnp.zeros_like(v1)
    o_ref[...] = jnp.concatenate([e1, e2, gate1, gate2, rank1, rank2, zero, zero], axis=0)


def moe_router(x2d, g, wg, bg, we, be, tm=512):
    t, d = x2d.shape
    wt = jnp.zeros((ROUTE_ROWS, d), F32).at[:N_GROUPS].set(wg.T).at[EXPERT_ROW0:EXPERT_ROW0 + N_EXPERTS].set(we.T)
    bias = jnp.zeros((ROUTE_ROWS, 1), F32).at[:N_GROUPS, 0].set(bg).at[EXPERT_ROW0:EXPERT_ROW0 + N_EXPERTS, 0].set(be)
    return pl.pallas_call(
        _router_body, grid=(t // tm,),
        in_specs=[pl.BlockSpec((tm, d), lambda i: (i, 0)), pl.BlockSpec((1, d), lambda i: (0, 0)),
                  pl.BlockSpec((ROUTE_ROWS, d), lambda i: (0, 0)), pl.BlockSpec((ROUTE_ROWS, 1), lambda i: (0, 0))],
        out_specs=[pl.BlockSpec((8, tm), lambda i: (0, i)), pl.BlockSpec((N_EXPERTS, 128), lambda i: (0, 0))],
        out_shape=[jax.ShapeDtypeStruct((8, t), F32), jax.ShapeDtypeStruct((N_EXPERTS, 128), F32)],
        scratch_shapes=[pltpu.VMEM((N_EXPERTS, 128), F32)],
        compiler_params=_params("arbitrary"), name="moe_router",
    )(x2d, g.reshape(1, d), wt, bias)


def _expert_body(blk_e_ref, nused_ref, tok_ref, x_ref, g_ref, w1_ref, w3_ref, w2_ref, ys_ref,
                 xbuf_ref, sem_ref, w1b_ref, w3b_ref, w2b_ref, *, tm):
    i = pl.program_id(0)
    nused = nused_ref[0]
    last = nused - 1
    nbuf = xbuf_ref.shape[0]
    ahead = nbuf - 1
    slot = i % nbuf
    cur = blk_e_ref[jnp.minimum(i, last)]
    prev = blk_e_ref[jnp.minimum(jnp.maximum(i - 1, 0), last)]

    def row_copy(blk, sl, r):
        return pltpu.make_async_copy(x_ref.at[pl.ds(tok_ref[blk * tm + r], 1), :],
                                     xbuf_ref.at[sl, pl.ds(r, 1), :], sem_ref.at[sl])

    def wait_rows(sl):
        pltpu.make_async_copy(x_ref.at[pl.ds(0, tm), :], xbuf_ref.at[sl], sem_ref.at[sl]).wait()

    @pl.when(i == 0)
    def _():
        for blk in range(ahead):
            def first(r2, c, blk=blk):
                row_copy(blk, blk, 2 * r2).start(priority=0)
                row_copy(blk, blk, 2 * r2 + 1).start(priority=1)
                return c
            lax.fori_loop(0, tm // 2, first, 0, unroll=4)

    @pl.when((i == 0) | (cur != prev))
    def _():
        w1b_ref[...] = w1_ref[...].astype(BF16)
        w3b_ref[...] = w3_ref[...].astype(BF16)
        w2b_ref[...] = w2_ref[...].astype(BF16)

    @pl.when(i < nused)
    def _():
        nxt = (i + ahead) % nbuf
        for r in range(tm):
            row_copy(i + ahead, nxt, r).start(priority=r % 2)
        wait_rows(slot)
        x = xbuf_ref[slot]
        ms = jnp.mean(x * x, axis=-1, keepdims=True)
        xn = (x * lax.rsqrt(ms + EPS) * g_ref[...]).astype(BF16)
        h1 = jnp.dot(xn, w1b_ref[...], preferred_element_type=F32)
        h3 = jnp.dot(xn, w3b_ref[...], preferred_element_type=F32)
        hid = (h1 * _sigmoid(h1) * h3).astype(BF16)
        ys_ref[...] = jnp.dot(hid, w2b_ref[...], preferred_element_type=F32)

    @pl.when((i >= nused) & (i < nused + ahead))
    def _():
        wait_rows(slot)

    @pl.when(i >= nused)
    def _():
        ys_ref[...] = jnp.zeros_like(ys_ref)


def moe_experts(x2d, g, src_tok, blk_e, nused, w1, w3, w2, layer, tm):
    t, d = x2d.shape
    cap = src_tok.shape[0]
    de = w1.shape[-1]

    def wspec(shape):
        return pl.BlockSpec((None, None) + shape,
                            lambda i, be, nu, tok: (layer, be[jnp.minimum(i, nu[0] - 1)], 0, 0))

    grid_spec = pltpu.PrefetchScalarGridSpec(
        num_scalar_prefetch=3, grid=(cap // tm,),
        in_specs=[pl.BlockSpec(memory_space=pl.ANY), pl.BlockSpec((1, d), lambda i, be, nu, tok: (0, 0)),
                  wspec((d, de)), wspec((d, de)), wspec((de, d))],
        out_specs=pl.BlockSpec((tm, d), lambda i, be, nu, tok: (i, 0)),
        scratch_shapes=[pltpu.VMEM((MOE_GATHER_BUFS, tm, d), F32), pltpu.SemaphoreType.DMA((MOE_GATHER_BUFS,)),
                        pltpu.VMEM((d, de), BF16), pltpu.VMEM((d, de), BF16), pltpu.VMEM((de, d), BF16)])
    return pl.pallas_call(
        functools.partial(_expert_body, tm=tm), grid_spec=grid_spec,
        out_shape=jax.ShapeDtypeStruct((cap, d), F32),
        compiler_params=_params("arbitrary", disable_bounds_checks=True), name="moe_experts",
    )(blk_e, nused, src_tok, x2d, g.reshape(1, d), w1, w3, w2)


def _combine_body(dest_ref, x_ref, gate_ref, ys_ref, o_ref, buf_ref, sem_ref, *, tc):
    i = pl.program_id(0)
    n = pl.num_programs(0)
    slot = i % 2

    def row_copy(step, sl, r, kk):
        return pltpu.make_async_copy(ys_ref.at[pl.ds(dest_ref[(step * tc + r) * 2 + kk], 1), :],
                                     buf_ref.at[sl, kk, pl.ds(r, 1), :], sem_ref.at[sl])

    def issue_step(step, sl):
        def body(r, c):
            row_copy(step, sl, r, 0).start(priority=0)
            row_copy(step, sl, r, 1).start(priority=1)
            return c
        lax.fori_loop(0, tc, body, 0, unroll=8)

    @pl.when(i == 0)
    def _():
        issue_step(0, 0)

    @pl.when(i + 1 < n)
    def _():
        issue_step(i + 1, 1 - slot)

    for kk in range(2):
        pltpu.make_async_copy(ys_ref.at[pl.ds(0, tc), :], buf_ref.at[slot, kk], sem_ref.at[slot]).wait()

    gate = gate_ref[...]
    o_ref[...] = x_ref[...] + gate[:, 0:1] * buf_ref[slot, 0] + gate[:, 1:2] * buf_ref[slot, 1]


def moe_combine(x2d, gates, ys, dest, tc=256):
    t, d = x2d.shape
    grid_spec = pltpu.PrefetchScalarGridSpec(
        num_scalar_prefetch=1, grid=(t // tc,),
        in_specs=[pl.BlockSpec((tc, d), lambda i, dest: (i, 0)), pl.BlockSpec((tc, 2), lambda i, dest: (i, 0)),
                  pl.BlockSpec(memory_space=pl.ANY)],
        out_specs=pl.BlockSpec((tc, d), lambda i, dest: (i, 0)),
        scratch_shapes=[pltpu.VMEM((2, 2, tc, d), F32), pltpu.SemaphoreType.DMA((2,))])
    return pl.pallas_call(
        functools.partial(_combine_body, tc=tc), grid_spec=grid_spec,
        out_shape=jax.ShapeDtypeStruct((t, d), F32),
        compiler_params=_params("arbitrary", disable_bounds_checks=True), name="moe_combine",
    )(dest, x2d, gates, ys)


def hier_moe_residual(x2d, g, wg, bg, we, be, w1, w3, w2, layer, tm=256):
    t, d = x2d.shape
    route, counts_b = moe_router(x2d, g, wg, bg, we, be)
    eid = route[0:2].T.astype(I32).reshape(-1)
    rank = route[4:6].T.astype(I32).reshape(-1)
    gates = route[2:4].T
    counts = counts_b[:, 0].astype(I32)
    padded = (counts + tm - 1) // tm * tm
    pad_end = jnp.cumsum(padded)
    pad_start = pad_end - padded
    experts = jnp.arange(N_EXPERTS, dtype=I32)
    dest = rank + jnp.sum(jnp.where(eid[:, None] == experts[None, :], pad_start[None, :], 0), axis=1)
    cap = t * 2 + (N_EXPERTS + MOE_GATHER_BUFS - 2) * tm
    nblk = cap // tm
    blk_row0 = jnp.arange(nblk, dtype=I32) * tm
    blk_e = jnp.minimum(jnp.sum((pad_end[None, :] <= blk_row0[:, None]).astype(I32), axis=1), N_EXPERTS - 1)
    nused = (pad_end[-1:] // tm).astype(I32)
    nslot = 2 * t
    slot_sorted = jnp.sort(eid * nslot + jnp.arange(nslot, dtype=I32)) % nslot
    starts = jnp.cumsum(counts) - counts
    blk_first = starts[blk_e] + blk_row0 - pad_start[blk_e]
    blk_last = starts[blk_e] + counts[blk_e] - 1
    pos = jnp.minimum(blk_first[:, None] + jnp.arange(tm, dtype=I32)[None, :], blk_last[:, None]).reshape(-1)
    src_tok = slot_sorted[jnp.clip(pos, 0, nslot - 1)] // 2
    ys = moe_experts(x2d, g, src_tok.astype(I32), blk_e, nused, w1, w3, w2, layer, tm)
    return moe_combine(x2d, gates, ys, dest.astype(I32))


def _even_layer(x2d, b, s, layer, mix_g, w_in, q_norm, k_norm, lam_q1, lam_k1, lam_q2, lam_k2, subln,
                hy_conv_w, hy_conv_b, f_w1, f_b1, f_freq, f_w2, f_b2, f_w3, hy_skip, w_out):
    d = x2d.shape[1]
    d_att = d // 2
    (u2d,) = norm_matmul(x2d, mix_g, [w_in.astype(BF16)], [BF16])
    u = u2d.reshape(b, s, -1)
    lambda_init = 0.8 - 0.6 * math.exp(-0.3 * layer)
    lam = jnp.exp(jnp.sum(lam_q1 * lam_k1)) - jnp.exp(jnp.sum(lam_q2 * lam_k2)) + lambda_init
    dk = q_norm.shape[0]
    qt = head_prep(u, 0, q_norm, scale=dk ** -0.5 * math.log2(math.e), transpose=True)
    kp = head_prep(u, ATT_HEADS, k_norm)
    vt = head_prep(u, 2 * ATT_HEADS, transpose=True)
    y_att = diff_attention(qt, kp, vt, lam, subln, lambda_init)
    z, x1c, zp = hy_prep(u, 3 * d_att, hy_conv_w, hy_conv_b)
    tables = fft_tables(s)
    kf = filter_spectrum(hyena_filter(s, f_w1, f_b1, f_freq, f_w2, f_b2, f_w3), tables[0], tables[1])
    y_hy = hy_fft_conv(zp, z, x1c, kf, tables, hy_skip)
    w_out = w_out.astype(BF16)
    return matmul_residual(x2d, [y_att.reshape(b * s, -1), y_hy.reshape(b * s, -1)],
                           [w_out[:d_att], w_out[d_att:]])


def _odd_layer(x2d, b, s, mix_g, w_in, conv_w, conv_b, gate_b, out_norm, w_out):
    d = x2d.shape[1]
    qk_w = conv_w.shape[1]
    main_w = qk_w + 2 * d
    ng = 4 * ML_HEADS
    w_gate = jnp.zeros((d, 128), F32).at[:, :ng].set(w_in[:, main_w:]).astype(BF16)
    u2d, ug = norm_matmul(x2d, mix_g, [w_in[:, :main_w].astype(BF16), w_gate], [BF16, F32])
    u = u2d.reshape(b, s, main_w)
    gcol = (ug[:, :ng] + gate_b).reshape(b, s, ng)
    grow = jnp.swapaxes(gcol, 1, 2)
    dk = qk_w // (2 * ML_HEADS)
    col_scale = jnp.concatenate([jnp.full((qk_w // 2,), dk ** -0.5, F32), jnp.ones((qk_w // 2,), F32)])
    qk = ml_prep(u, conv_w, conv_b, col_scale)
    hf, hb = mlstm_scan(qk, u, qk_w // d, gcol, grow)
    return mlstm_out(x2d, hf.reshape(b * s, d), hb.reshape(b * s, d), u2d, (qk_w + d) // d, out_norm,
                     w_out.astype(BF16))


def kernel(x, mix_norm, ffn_norm, ev_w_in, ev_q_norm, ev_k_norm, ev_lam_q1, ev_lam_k1, ev_lam_q2, ev_lam_k2, ev_subln, ev_hy_conv_w, ev_hy_conv_b, ev_hy_f_w1, ev_hy_f_b1, ev_hy_f_freq, ev_hy_f_w2, ev_hy_f_b2, ev_hy_f_w3, ev_hy_skip, ev_w_out, od_w_in, od_conv_w, od_conv_b, od_gate_b, od_out_norm, od_w_out, moe_wg, moe_bg, moe_we, moe_be, moe_w1, moe_w3, moe_w2):
    b, s, d = x.shape
    depth = mix_norm.shape[0]
    x2d = x.reshape(b * s, d)
    for layer in range(depth):
        j = layer // 2
        if layer % 2 == 0:
            x2d = _even_layer(x2d, b, s, layer, mix_norm[layer], ev_w_in[j], ev_q_norm[j], ev_k_norm[j],
                              ev_lam_q1[j], ev_lam_k1[j], ev_lam_q2[j], ev_lam_k2[j], ev_subln[j],
                              ev_hy_conv_w[j], ev_hy_conv_b[j], ev_hy_f_w1[j], ev_hy_f_b1[j], ev_hy_f_freq[j],
                              ev_hy_f_w2[j], ev_hy_f_b2[j], ev_hy_f_w3[j], ev_hy_skip[j], ev_w_out[j])
        else:
            x2d = _odd_layer(x2d, b, s, mix_norm[layer], od_w_in[j], od_conv_w[j], od_conv_b[j], od_gate_b[j],
                             od_out_norm[j], od_w_out[j])
        x2d = hier_moe_residual(x2d, ffn_norm[layer], moe_wg[layer], moe_bg[layer], moe_we[layer],
                                moe_be[layer], moe_w1, moe_w3, moe_w2, layer)
    return x2d.reshape(b, s, d)
```

```python
import functools
import math

import jax
import jax.numpy as jnp
from jax import lax
from jax.experimental import pallas as pl
from jax.experimental.pallas import tpu as pltpu

F32 = jnp.float32
BF16 = jnp.bfloat16
I32 = jnp.int32

EPS = 1e-6
ROPE_THETA = 500000.0
ATT_HEADS = 4
ML_HEADS = 4
ML_CHUNK = 128
ML_BATCH_PER_STEP = 2
N_GROUPS = 4
EXPERTS_PER_GROUP = 8
N_EXPERTS = N_GROUPS * EXPERTS_PER_GROUP
HY_EMB_BANDS = 16
HY_MIN_DECAY = math.log(1e-2) / 1.5
HY_MAX_DECAY = math.log(1e-2) / 0.3

V7X_VMEM_BYTES = 64 * 1024 * 1024
VMEM_LIMIT = V7X_VMEM_BYTES - 8 * 1024 * 1024
NEG_BIG = -1e30


def _params(*sem, **kw):
    return pltpu.CompilerParams(dimension_semantics=sem, vmem_limit_bytes=VMEM_LIMIT, **kw)


def _sigmoid(x):
    return 1.0 / (1.0 + jnp.exp(-x))


def _norm_matmul_body(x_ref, g_ref, *refs, n_out, col_chunk):
    w_refs, o_refs = refs[:n_out], refs[n_out:]
    x = x_ref[...]
    ms = jnp.mean(x * x, axis=-1, keepdims=True)
    hn = (x * lax.rsqrt(ms + EPS) * g_ref[...]).astype(BF16)
    for w_ref, o_ref in zip(w_refs, o_refs):
        n = w_ref.shape[1]
        for c in range(0, n, col_chunk):
            ce = min(n, c + col_chunk)
            o_ref[:, c:ce] = jnp.dot(hn, w_ref[:, c:ce], preferred_element_type=F32).astype(o_ref.dtype)


def norm_matmul(x2d, g, ws, out_dtypes, tm=512):
    t, d = x2d.shape
    in_specs = [pl.BlockSpec((tm, d), lambda i: (i, 0)), pl.BlockSpec((1, d), lambda i: (0, 0))]
    in_specs += [pl.BlockSpec(w.shape, lambda i: (0, 0)) for w in ws]
    out_specs = [pl.BlockSpec((tm, w.shape[1]), lambda i: (i, 0)) for w in ws]
    out_shape = [jax.ShapeDtypeStruct((t, w.shape[1]), dt) for w, dt in zip(ws, out_dtypes)]
    return pl.pallas_call(
        functools.partial(_norm_matmul_body, n_out=len(ws), col_chunk=1024),
        grid=(t // tm,), in_specs=in_specs, out_specs=out_specs, out_shape=out_shape,
        compiler_params=_params("parallel"), name="norm_matmul",
    )(x2d, g.reshape(1, d), *ws)


def _matmul_res_body(res_ref, *refs, n_in):
    a_refs, w_refs, o_ref = refs[:n_in], refs[n_in:2 * n_in], refs[2 * n_in]
    acc = res_ref[...]
    for a_ref, w_ref in zip(a_refs, w_refs):
        acc = acc + jnp.dot(a_ref[...], w_ref[...], preferred_element_type=F32)
    o_ref[...] = acc


def matmul_residual(res, a_list, w_list, tm=512):
    t, d = res.shape
    in_specs = [pl.BlockSpec((tm, d), lambda i: (i, 0))]
    in_specs += [pl.BlockSpec((tm, a.shape[1]), lambda i: (i, 0)) for a in a_list]
    in_specs += [pl.BlockSpec(w.shape, lambda i: (0, 0)) for w in w_list]
    return pl.pallas_call(
        functools.partial(_matmul_res_body, n_in=len(a_list)),
        grid=(t // tm,), in_specs=in_specs, out_specs=pl.BlockSpec((tm, d), lambda i: (i, 0)),
        out_shape=jax.ShapeDtypeStruct((t, d), F32),
        compiler_params=_params("parallel"), name="matmul_residual",
    )(res, *a_list, *w_list)


def _head_prep_body(u_ref, *refs, dk, rope, transpose):
    o_ref = refs[-1]
    x = u_ref[...].astype(F32)
    if rope:
        g_ref, c_ref, s1_ref, s2_ref = refs[:4]
        lane = lax.broadcasted_iota(I32, x.shape, 1)
        lo = lane < dk
        x2 = x * x
        s_lo = jnp.sum(jnp.where(lo, x2, 0.0), axis=-1, keepdims=True)
        s_hi = jnp.sum(jnp.where(lo, 0.0, x2), axis=-1, keepdims=True)
        ms = jnp.where(lo, s_lo, s_hi) * (1.0 / dk)
        y = x * lax.rsqrt(ms + EPS) * g_ref[...]
        x = y * c_ref[...] + pltpu.roll(y, 120, 1) * s1_ref[...] + pltpu.roll(y, 8, 1) * s2_ref[...]
    if transpose:
        x = x.T
    o_ref[...] = x.astype(o_ref.dtype)


def _rope_lane_tables(seq, dk, rope_dim, scale):
    half = rope_dim // 2
    inv_freq = 1.0 / (ROPE_THETA ** (jnp.arange(0, rope_dim, 2, dtype=F32) / rope_dim))
    ang = jnp.arange(seq, dtype=F32)[:, None] * inv_freq[None, :]
    cos, sin = jnp.cos(ang), jnp.sin(ang)
    d = jnp.arange(2 * dk) % dk
    fi = d % half
    c_tab = jnp.where(d[None, :] < rope_dim, cos[:, fi], 1.0)
    s1_tab = jnp.where(d[None, :] < half, -sin[:, fi], 0.0)
    s2_tab = jnp.where((d[None, :] >= half) & (d[None, :] < rope_dim), sin[:, fi], 0.0)
    return (jnp.stack([c_tab, s1_tab, s2_tab]) * scale).astype(F32)


def head_prep(u, blk0, norm_gain=None, scale=1.0, transpose=False):
    b, s, _ = u.shape
    h = ATT_HEADS
    rope = norm_gain is not None
    in_specs = [pl.BlockSpec((None, s, 128), lambda c, bi: (bi, 0, blk0 + c))]
    args = [u]
    dk = 64
    if rope:
        dk = norm_gain.shape[0]
        assert 2 * dk == 128 and dk // 4 == 16, "rope roll shifts assume 64-wide components, 16 rotary dims"
        tabs = _rope_lane_tables(s, dk, dk // 4, scale)
        in_specs += [pl.BlockSpec((1, 128), lambda c, bi: (0, 0))] + [pl.BlockSpec((s, 128), lambda c, bi: (0, 0))] * 3
        args += [jnp.tile(norm_gain, 2).reshape(1, 128).astype(F32), tabs[0], tabs[1], tabs[2]]
    if transpose:
        out_spec = pl.BlockSpec((None, 128, s), lambda c, bi: (bi, c, 0))
        out_shape = jax.ShapeDtypeStruct((b, h * 128, s), BF16)
    else:
        out_spec = pl.BlockSpec((None, s, 128), lambda c, bi: (bi, 0, c))
        out_shape = jax.ShapeDtypeStruct((b, s, h * 128), BF16)
    return pl.pallas_call(
        functools.partial(_head_prep_body, dk=dk, rope=rope, transpose=transpose), grid=(h, b),
        in_specs=in_specs, out_specs=out_spec, out_shape=out_shape,
        compiler_params=_params("parallel", "parallel"), name="head_prep",
    )(*args)


def _attn_body(lam_ref, qt_ref, k_ref, vt_ref, g_ref, o_ref, *, tq, dk, post_scale, n_split):
    lam = lam_ref[0, 0]
    th = tq // n_split
    for part in range(n_split):
        qt = qt_ref[:, part * th:(part + 1) * th]
        row = lax.broadcasted_iota(I32, qt.shape, 0)
        zero = jnp.zeros_like(qt)
        qq = jnp.concatenate([jnp.where(row < dk, qt, zero), jnp.where(row < dk, zero, qt)], axis=1)
        st = jnp.dot(k_ref[...], qq, preferred_element_type=F32)
        m = jnp.max(st, axis=0, keepdims=True)
        p = jnp.exp2(st - m)
        r = 1.0 / jnp.sum(p, axis=0, keepdims=True)
        ot = jnp.dot(vt_ref[...], p.astype(BF16), preferred_element_type=F32)
        o = (ot[:, :th] * r[:, :th] - ot[:, th:] * (lam * r[:, th:])).T
        ms = jnp.mean(o * o, axis=-1, keepdims=True)
        o_ref[part * th:(part + 1) * th, :] = (o * lax.rsqrt(ms + EPS) * g_ref[...] * post_scale).astype(o_ref.dtype)


def diff_attention(qt, k, vt, lam, subln, lambda_init, tq=256):
    b, s, _ = k.shape
    h = ATT_HEADS
    return pl.pallas_call(
        functools.partial(_attn_body, tq=tq, dk=64, post_scale=1.0 - lambda_init, n_split=1),
        grid=(b, h, s // tq),
        in_specs=[pl.BlockSpec(memory_space=pltpu.SMEM),
                  pl.BlockSpec((None, 128, tq), lambda bi, hi, i: (bi, hi, i)),
                  pl.BlockSpec((None, s, 128), lambda bi, hi, i: (bi, 0, hi)),
                  pl.BlockSpec((None, 128, s), lambda bi, hi, i: (bi, hi, 0)),
                  pl.BlockSpec((1, 128), lambda bi, hi, i: (0, 0))],
        out_specs=pl.BlockSpec((None, tq, 128), lambda bi, hi, i: (bi, i, hi)),
        out_shape=jax.ShapeDtypeStruct((b, s, h * 128), BF16),
        compiler_params=_params("parallel", "parallel", "parallel"), name="diff_attention",
    )(lam.reshape(1, 1).astype(F32), qt, k, vt, subln.reshape(1, 128).astype(F32))


def _conv3(u_ref, w_ref, b_ref):
    x = u_ref[...].astype(F32)
    s = x.shape[0]
    row = lax.broadcasted_iota(I32, x.shape, 0)
    x_prev = jnp.where(row == 0, 0.0, pltpu.roll(x, 1, 0))
    x_next = jnp.where(row == s - 1, 0.0, pltpu.roll(x, s - 1, 0))
    w = w_ref[...]
    return b_ref[...] + x_prev * w[0:1] + x * w[1:2] + x_next * w[2:3]


FFT_N1 = 64
FFT_UNROLL = 8
FFT_PAD = 8


def _hy_prep_body(x1_ref, x2_ref, v_ref, w1_ref, w2_ref, wv_ref, b1_ref, b2_ref, bv_ref, z_ref, x1c_ref, zp_ref):
    x1c_ref[...] = _conv3(x1_ref, w1_ref, b1_ref).astype(x1c_ref.dtype)
    z = _conv3(v_ref, wv_ref, bv_ref) * _conv3(x2_ref, w2_ref, b2_ref)
    z_ref[...] = z.astype(z_ref.dtype)
    nb = z.shape[0] // FFT_N1
    zp_ref[...] = jnp.zeros_like(zp_ref)
    for n2 in range(nb):
        for ci in range(z.shape[1] // 128):
            zp_ref[ci, pl.ds(n2, FFT_N1, stride=nb + FFT_PAD), :] = (
                z[n2 * FFT_N1:(n2 + 1) * FFT_N1, ci * 128:(ci + 1) * 128])


def hy_prep(u, col0, conv_w, conv_b, tc=256):
    b, s, _ = u.shape
    d_hy = conv_w.shape[1] // 3
    nct = d_hy // tc
    blk0 = col0 // tc
    sp = FFT_N1 * (s // FFT_N1 + FFT_PAD)

    def uspec(part):
        return pl.BlockSpec((None, s, tc), lambda bi, c: (bi, 0, blk0 + part * nct + c))

    def wspec(part, rows):
        return pl.BlockSpec((rows, tc), lambda bi, c: (0, part * nct + c))

    ospec = pl.BlockSpec((None, s, tc), lambda bi, c: (bi, 0, c))
    return pl.pallas_call(
        _hy_prep_body, grid=(b, nct),
        in_specs=[uspec(0), uspec(1), uspec(2), wspec(0, 3), wspec(1, 3), wspec(2, 3),
                  wspec(0, 1), wspec(1, 1), wspec(2, 1)],
        out_specs=[ospec, ospec, pl.BlockSpec((None, tc // 128, sp, 128), lambda bi, c: (bi, c, 0, 0))],
        out_shape=[jax.ShapeDtypeStruct((b, s, d_hy), BF16), jax.ShapeDtypeStruct((b, s, d_hy), BF16),
                   jax.ShapeDtypeStruct((b, d_hy // 128, sp, 128), F32)],
        compiler_params=_params("parallel", "parallel"), name="hy_prep",
    )(u, u, u, conv_w, conv_w, conv_w, conv_b.reshape(1, -1), conv_b.reshape(1, -1), conv_b.reshape(1, -1))


def hyena_filter(length, w1, b1, freq, w2, b2, w3):
    d_hy = w3.shape[1] // 2
    t = jnp.linspace(0.0, 1.0, length, dtype=F32)[:, None]
    bands = jnp.linspace(1e-4, HY_EMB_BANDS - 1, HY_EMB_BANDS, dtype=F32)[None, :]
    ang = (2.0 * math.pi / length) * jnp.arange(length, dtype=F32)[:, None] * bands
    z = jnp.concatenate([t, jnp.cos(ang), -jnp.sin(ang)], axis=-1)
    hp = lax.Precision.HIGHEST
    hdn = jnp.sin(freq * (jnp.dot(z, w1, precision=hp) + b1))
    hdn = jnp.sin(freq * (jnp.dot(hdn, w2, precision=hp) + b2))
    filt = jnp.dot(hdn, w3, precision=hp)
    deltas = jnp.abs(jnp.linspace(HY_MIN_DECAY, HY_MAX_DECAY, d_hy, dtype=F32))
    decay = jnp.exp(-t * deltas[None, :])
    h_fwd = filt[:, :d_hy] * decay
    h_bwd = filt[:, d_hy:] * decay
    h_fwd = h_fwd.at[0].add(h_bwd[0])
    h_bwd = h_bwd.at[0].set(0.0)
    norm = jnp.sum(jnp.abs(h_fwd), axis=0, keepdims=True) + jnp.sum(jnp.abs(h_bwd), axis=0, keepdims=True) + EPS
    return jnp.concatenate([h_fwd / norm, h_bwd / norm], axis=1)


def fft_tables(length):
    n = 2 * length
    n1c, n2c, nb = FFT_N1, n // FFT_N1, length // FFT_N1
    unit = 2.0 * math.pi / n
    i1 = jnp.arange(n1c, dtype=I32)
    i2 = jnp.arange(n2c, dtype=I32)
    ib = jnp.arange(nb, dtype=I32)
    samp = i1[:, None, None] + n1c * ib[None, None, :]
    ang = ((i2[None, :, None] * samp) % n).astype(F32) * unit
    m1 = jnp.concatenate([jnp.cos(ang), -jnp.sin(ang)], axis=1)
    ang = ((i1[:, None] * i1[None, :]) % n1c).astype(F32) * (2.0 * math.pi / n1c)
    c, s = jnp.cos(ang), jnp.sin(ang)
    f1 = jnp.concatenate([jnp.concatenate([c, s], axis=1), jnp.concatenate([-s, c], axis=1)], axis=0)
    freq = n2c * i1[None, None, :] + i2[:, None, None]
    ang = ((i1[None, :, None] * freq) % n).astype(F32) * unit
    c, s = jnp.cos(ang), jnp.sin(ang)
    g1 = jnp.concatenate([jnp.concatenate([c, -s], axis=2), jnp.concatenate([s, c], axis=2)], axis=1)
    ang = ((ib[:, None] * i2[None, :]) % n2c).astype(F32) * (2.0 * math.pi / n2c)
    g2 = jnp.concatenate([jnp.cos(ang), -jnp.sin(ang)], axis=1) * (1.0 / n)
    return m1.astype(BF16), f1.astype(BF16), g1.astype(BF16), g2.astype(BF16)


def _fft_stage1(xp_ref, m1_ref, p_ref):
    nb = m1_ref.shape[2]
    n2c = m1_ref.shape[1] // 2

    def body(n1, c):
        x = xp_ref[pl.ds(pl.multiple_of(n1 * (nb + FFT_PAD), 8), nb), :].astype(BF16)
        a = jnp.dot(m1_ref[n1], x, preferred_element_type=F32)
        p_ref[0, pl.ds(n1, n2c, stride=FFT_N1 + FFT_PAD), :] = a[:n2c]
        p_ref[1, pl.ds(n1, n2c, stride=FFT_N1 + FFT_PAD), :] = a[n2c:]
        return c
    lax.fori_loop(0, FFT_N1, body, 0, unroll=FFT_UNROLL)


def _fft_stage2(p_ref, f1_ref, k2):
    r0 = pl.multiple_of(k2 * (FFT_N1 + FFT_PAD), 8)
    slab = jnp.concatenate([p_ref[0, pl.ds(r0, FFT_N1), :], p_ref[1, pl.ds(r0, FFT_N1), :]], axis=0)
    return jnp.dot(f1_ref[...], slab.astype(BF16), preferred_element_type=F32)


def _spectrum_body(xp_ref, m1_ref, f1_ref, o_ref, p_ref):
    _fft_stage1(xp_ref, m1_ref, p_ref)

    def body(k2, c):
        o_ref[k2] = _fft_stage2(p_ref, f1_ref, k2)
        return c
    lax.fori_loop(0, o_ref.shape[0], body, 0, unroll=FFT_UNROLL)


def filter_spectrum(ab, m1, f1):
    length, c2 = ab.shape
    nb = length // FFT_N1
    n2c = m1.shape[1] // 2
    nch = c2 // 128
    abp = jnp.pad(ab.reshape(nb, FFT_N1, nch, 128).transpose(2, 1, 0, 3), ((0, 0), (0, 0), (0, FFT_PAD), (0, 0)))
    abp = abp.reshape(nch, FFT_N1 * (nb + FFT_PAD), 128)
    spec = pl.pallas_call(
        _spectrum_body, grid=(nch,),
        in_specs=[pl.BlockSpec((None,) + abp.shape[1:], lambda c: (c, 0, 0)),
                  pl.BlockSpec(m1.shape, lambda c: (0, 0, 0)), pl.BlockSpec(f1.shape, lambda c: (0, 0))],
        out_specs=pl.BlockSpec((None, n2c, 2 * FFT_N1, 128), lambda c: (c, 0, 0, 0)),
        out_shape=jax.ShapeDtypeStruct((nch, n2c, 2 * FFT_N1, 128), F32),
        scratch_shapes=[pltpu.VMEM((2, n2c * (FFT_N1 + FFT_PAD), 128), F32)],
        compiler_params=_params("parallel"), name="filter_spectrum",
    )(abp, m1, f1)
    fa, fb = spec[:nch // 2], spec[nch // 2:]
    h = FFT_N1
    return jnp.concatenate([fa[:, :, :h] + fb[:, :, :h], fa[:, :, h:] - fb[:, :, h:]], axis=2).astype(BF16)


def _hy_fft_body(zp_ref, z_ref, x1c_ref, kf_ref, m1_ref, f1_ref, g1_ref, g2_ref, skip_ref, o_ref,
                 p_ref, q_ref, y_ref):
    h = FFT_N1
    n2c = g1_ref.shape[0]
    nb = g2_ref.shape[0]
    _fft_stage1(zp_ref, m1_ref, p_ref)

    def mid(k2, c):
        xf = _fft_stage2(p_ref, f1_ref, k2)
        kf = kf_ref[k2].astype(F32)
        xr, xi, kr, ki = xf[:h], xf[h:], kf[:h], kf[h:]
        y = jnp.concatenate([xr * kr - xi * ki, xr * ki + xi * kr], axis=0).astype(BF16)
        d = jnp.dot(g1_ref[k2], y, preferred_element_type=F32)
        q_ref[0, pl.ds(k2, h, stride=n2c + FFT_PAD), :] = d[:h]
        q_ref[1, pl.ds(k2, h, stride=n2c + FFT_PAD), :] = d[h:]
        return c
    lax.fori_loop(0, n2c, mid, 0, unroll=FFT_UNROLL)

    def last(t1, c):
        r0 = pl.multiple_of(t1 * (n2c + FFT_PAD), 8)
        slab = jnp.concatenate([q_ref[0, pl.ds(r0, n2c), :], q_ref[1, pl.ds(r0, n2c), :]], axis=0)
        y_ref[pl.ds(t1, nb, stride=h), :] = jnp.dot(g2_ref[...], slab.astype(BF16), preferred_element_type=F32)
        return c
    lax.fori_loop(0, h, last, 0, unroll=FFT_UNROLL)

    z = z_ref[...].astype(F32)
    o_ref[...] = ((y_ref[...] + z * skip_ref[...]) * x1c_ref[...].astype(F32)).astype(o_ref.dtype)


def hy_fft_conv(zp, z, x1c, kf, tables, skip):
    m1, f1, g1, g2 = tables
    b, s, c = z.shape
    nch = c // 128

    def const(shape):
        return pl.BlockSpec(shape, lambda ci, bi: (0,) * len(shape))

    nat = pl.BlockSpec((None, s, 128), lambda ci, bi: (bi, 0, ci))
    return pl.pallas_call(
        _hy_fft_body, grid=(nch, b),
        in_specs=[pl.BlockSpec((None, None) + zp.shape[2:], lambda ci, bi: (bi, ci, 0, 0)), nat, nat,
                  pl.BlockSpec((None,) + kf.shape[1:], lambda ci, bi: (ci, 0, 0, 0)),
                  const(m1.shape), const(f1.shape), const(g1.shape), const(g2.shape),
                  pl.BlockSpec((1, 128), lambda ci, bi: (0, ci))],
        out_specs=nat,
        out_shape=jax.ShapeDtypeStruct((b, s, c), BF16),
        scratch_shapes=[pltpu.VMEM((2, g1.shape[0] * (FFT_N1 + FFT_PAD), 128), F32),
                        pltpu.VMEM((2, FFT_N1 * (g1.shape[0] + FFT_PAD), 128), F32),
                        pltpu.VMEM((s, 128), F32)],
        compiler_params=_params("parallel", "parallel"), name="hy_fft_conv",
    )(zp, z, x1c, kf, m1, f1, g1, g2, skip.reshape(1, c).astype(F32))


def _ml_prep_body(u_ref, w_ref, b_ref, sc_ref, o_ref):
    y = _conv3(u_ref, w_ref, b_ref)
    o_ref[...] = (y * _sigmoid(y) * sc_ref[...]).astype(o_ref.dtype)


def ml_prep(u, conv_w, conv_b, col_scale, tc=256):
    b, s, _ = u.shape
    w = conv_w.shape[1]
    return pl.pallas_call(
        _ml_prep_body, grid=(b, w // tc),
        in_specs=[pl.BlockSpec((None, s, tc), lambda bi, c: (bi, 0, c)),
                  pl.BlockSpec((3, tc), lambda bi, c: (0, c)),
                  pl.BlockSpec((1, tc), lambda bi, c: (0, c)),
                  pl.BlockSpec((1, tc), lambda bi, c: (0, c))],
        out_specs=pl.BlockSpec((None, s, tc), lambda bi, c: (bi, 0, c)),
        out_shape=jax.ShapeDtypeStruct((b, s, w), BF16),
        compiler_params=_params("parallel", "parallel"), name="ml_prep",
    )(u, conv_w, conv_b.reshape(1, w), col_scale.reshape(1, w))


def _log_sigmoid(x):
    return jnp.minimum(x, 0.0) - jnp.log(1.0 + jnp.exp(-jnp.abs(x)))


def _dot_split(a, b, a_is_f32):
    x = a if a_is_f32 else b
    hi = x.astype(BF16)
    lo = (x - hi.astype(F32)).astype(BF16)
    if a_is_f32:
        return (jnp.dot(hi, b, preferred_element_type=F32) + jnp.dot(lo, b, preferred_element_type=F32))
    return (jnp.dot(a, hi, preferred_element_type=F32) + jnp.dot(a, lo, preferred_element_type=F32))


def _mlstm_chain(q, k, v, bc, br, li_r, li_c, total, mask, c_ref, m_ref, idx):
    dv = v.shape[1] - 128
    c_st = c_ref[idx]
    m_st = m_ref[idx:idx + 1, 0:1]
    dmat = jnp.where(mask, bc - br + li_r, NEG_BIG)
    inter = bc + m_st
    m_t = jnp.maximum(inter, jnp.max(dmat, axis=-1, keepdims=True))
    w_intra = jnp.exp(dmat - m_t)
    w_inter = jnp.exp(inter - m_t)
    sc = lax.dot_general(q, k, (((1,), (1,)), ((), ())), preferred_element_type=F32) * w_intra
    both = (w_inter * jnp.dot(q, c_st.astype(BF16), preferred_element_type=F32)
            + jnp.dot(sc.astype(BF16), v, preferred_element_type=F32))
    den = both[:, dv:dv + 1]
    h = both[:, :dv] / jnp.maximum(jnp.abs(den), jnp.exp(-m_t))
    g_s = total - bc + li_c
    m_next = jnp.maximum(total + m_st, jnp.max(g_s, axis=0, keepdims=True))
    a_prev = jnp.exp(total + m_st - m_next)
    kw = k.astype(F32) * jnp.exp(g_s - m_next)
    c_ref[idx] = a_prev * c_st + lax.dot_general(kw.astype(BF16), v, (((0,), (0,)), ((), ())),
                                                 preferred_element_type=F32)
    m_ref[idx:idx + 1, :] = jnp.broadcast_to(m_next, (1, m_ref.shape[1]))
    return h


def _mlstm_body(qkf_ref, vf_ref, gcf_ref, grf_ref, qkb_ref, vb_ref, gcb_ref, grb_ref,
                hf_ref, hb_ref, c_ref, m_ref, *, heads, dk, dv):
    @pl.when(pl.program_id(1) == 0)
    def _():
        c_ref[...] = jnp.zeros_like(c_ref)
        m_ref[...] = jnp.zeros_like(m_ref)

    lc = qkf_ref.shape[1]
    ones_blk = jnp.where(lax.broadcasted_iota(I32, (lc, 128), 1) == 0, 1.0, 0.0).astype(BF16)
    t_i = lax.broadcasted_iota(I32, (lc, lc), 0)
    s_i = lax.broadcasted_iota(I32, (lc, lc), 1)
    lower = s_i <= t_i
    upper = s_i >= t_i
    ltri = jnp.where(lower, 1.0, 0.0).astype(BF16)
    utri = jnp.where(upper, 1.0, 0.0).astype(BF16)

    for bb in range(qkf_ref.shape[0]):
        for direction, (qk_ref, v_ref, gc_ref, gr_ref, h_ref) in enumerate(
                ((qkf_ref, vf_ref, gcf_ref, grf_ref, hf_ref), (qkb_ref, vb_ref, gcb_ref, grb_ref, hb_ref))):
            fwd = direction == 0
            gc = gc_ref[bb]
            gr = gr_ref[bb]
            lfc, lfr = _log_sigmoid(gc), _log_sigmoid(gr)
            cum_c = _dot_split(ltri if fwd else utri, lfc, a_is_f32=False)
            cum_r = _dot_split(lfr, utri if fwd else ltri, a_is_f32=True)
            for hd in range(heads):
                gi = (0 if fwd else 2) * heads + hd
                gf = (1 if fwd else 3) * heads + hd
                bc, br = cum_c[:, gf:gf + 1], cum_r[gf:gf + 1, :]
                total = br[:, lc - 1:lc] if fwd else br[:, 0:1]
                q = qk_ref[bb, :, hd * dk:(hd + 1) * dk]
                k = qk_ref[bb, :, (heads + hd) * dk:(heads + hd + 1) * dk]
                v = jnp.concatenate([v_ref[bb, :, hd * dv:(hd + 1) * dv], ones_blk], axis=1)
                h = _mlstm_chain(q, k, v, bc, br, gr[gi:gi + 1, :], gc[:, gi:gi + 1], total,
                                 lower if fwd else upper, c_ref, m_ref, (bb * 2 + direction) * heads + hd)
                h_ref[bb, :, hd * dv:(hd + 1) * dv] = h.astype(h_ref.dtype)


def mlstm_scan(qk, u, v_blk, gcol, grow):
    b, s, w = qk.shape
    heads = ML_HEADS
    dk = w // (2 * heads)
    dv = 2 * dk
    lc = ML_CHUNK
    nc = s // lc
    ng = gcol.shape[-1]

    def fw(bi, j):
        return j

    def bw(bi, j):
        return nc - 1 - j

    nbs = ML_BATCH_PER_STEP if b % ML_BATCH_PER_STEP == 0 else 1

    def specs(pos):
        return [pl.BlockSpec((nbs, lc, w), lambda bi, j: (bi, pos(bi, j), 0)),
                pl.BlockSpec((nbs, lc, heads * dv), lambda bi, j: (bi, pos(bi, j), v_blk)),
                pl.BlockSpec((nbs, lc, ng), lambda bi, j: (bi, pos(bi, j), 0)),
                pl.BlockSpec((nbs, ng, lc), lambda bi, j: (bi, 0, pos(bi, j)))]

    hshape = jax.ShapeDtypeStruct((b, s, heads * dv), BF16)
    return pl.pallas_call(
        functools.partial(_mlstm_body, heads=heads, dk=dk, dv=dv), grid=(b // nbs, nc),
        in_specs=specs(fw) + specs(bw),
        out_specs=[pl.BlockSpec((nbs, lc, heads * dv), lambda bi, j: (bi, j, 0)),
                   pl.BlockSpec((nbs, lc, heads * dv), lambda bi, j: (bi, nc - 1 - j, 0))],
        out_shape=[hshape, hshape],
        scratch_shapes=[pltpu.VMEM((nbs * 2 * heads, dk, dv + 128), F32),
                        pltpu.VMEM((nbs * 2 * heads, 128), F32)],
        compiler_params=_params("parallel", "arbitrary"), name="mlstm_scan",
    )(qk, u, gcol, grow, qk, u, gcol, grow)


def _mlstm_out_body(res_ref, hf_ref, hb_ref, o_ref, g_ref, w_ref, out_ref, *, heads):
    hs = hf_ref[...].astype(F32) + hb_ref[...].astype(F32)
    dv = hs.shape[1] // heads
    g = g_ref[...]
    parts = []
    for hd in range(heads):
        seg = hs[:, hd * dv:(hd + 1) * dv]
        ms = jnp.mean(seg * seg, axis=-1, keepdims=True)
        parts.append(seg * lax.rsqrt(ms + EPS) * g[:, hd * dv:(hd + 1) * dv])
    a = jnp.concatenate(parts, axis=-1) * _sigmoid(o_ref[...].astype(F32))
    out_ref[...] = res_ref[...] + jnp.dot(a.astype(BF16), w_ref[...], preferred_element_type=F32)


def mlstm_out(res, hf, hb, u2d, o_blk, gain, w_out, tm=512):
    t, d = res.shape
    row = lambda i: (i, 0)
    return pl.pallas_call(
        functools.partial(_mlstm_out_body, heads=ML_HEADS), grid=(t // tm,),
        in_specs=[pl.BlockSpec((tm, d), row), pl.BlockSpec((tm, d), row), pl.BlockSpec((tm, d), row),
                  pl.BlockSpec((tm, d), lambda i: (i, o_blk)), pl.BlockSpec((1, d), lambda i: (0, 0)),
                  pl.BlockSpec((d, d), lambda i: (0, 0))],
        out_specs=pl.BlockSpec((tm, d), row), out_shape=jax.ShapeDtypeStruct((t, d), F32),
        compiler_params=_params("parallel"), name="mlstm_out",
    )(res, hf, hb, u2d, gain.reshape(1, d), w_out)


ROUTE_ROWS = 128
EXPERT_ROW0 = 8


def _router_body(x_ref, g_ref, wt_ref, b_ref, o_ref, cnt_ref, run_ref):
    @pl.when(pl.program_id(0) == 0)
    def _():
        run_ref[...] = jnp.zeros_like(run_ref)

    x = x_ref[...]
    ms = jnp.mean(x * x, axis=-1, keepdims=True)
    xn = x * lax.rsqrt(ms + EPS) * g_ref[...]
    logit = lax.dot_general(wt_ref[...], xn, (((1,), (1,)), ((), ())), preferred_element_type=F32,
                            precision=lax.Precision.HIGHEST) + b_ref[...]
    rows = [logit[r:r + 1, :] for r in range(EXPERT_ROW0 + N_EXPERTS)]
    g_best, g_idx = rows[0], jnp.zeros_like(rows[0])
    for gi in range(1, N_GROUPS):
        better = rows[gi] > g_best
        g_best = jnp.where(better, rows[gi], g_best)
        g_idx = jnp.where(better, float(gi), g_idx)
    g_den = sum(jnp.exp(rows[gi] - g_best) for gi in range(N_GROUPS))
    g_w = 1.0 / g_den
    sel = []
    for e in range(EXPERTS_PER_GROUP):
        v = rows[EXPERT_ROW0 + e]
        for gi in range(1, N_GROUPS):
            v = jnp.where(g_idx == float(gi), rows[EXPERT_ROW0 + gi * EXPERTS_PER_GROUP + e], v)
        sel.append(v)
    v1, i1 = sel[0], jnp.zeros_like(sel[0])
    for e in range(1, EXPERTS_PER_GROUP):
        better = sel[e] > v1
        v1 = jnp.where(better, sel[e], v1)
        i1 = jnp.where(better, float(e), i1)
    v2, i2 = jnp.full_like(v1, -jnp.inf), jnp.zeros_like(v1)
    for e in range(EXPERTS_PER_GROUP):
        better = (sel[e] > v2) & (i1 != float(e))
        v2 = jnp.where(better, sel[e], v2)
        i2 = jnp.where(better, float(e), i2)
    e21 = jnp.exp(v2 - v1)
    gate1 = g_w / (1.0 + e21)
    gate2 = gate1 * e21
    base = g_idx * float(EXPERTS_PER_GROUP)
    e1, e2 = base + i1, base + i2
    tm = e1.shape[1]
    erow = lax.broadcasted_iota(I32, (N_EXPERTS, tm), 0).astype(F32)
    oh1 = jnp.where(erow == e1, 1.0, 0.0)
    oh2 = jnp.where(erow == e2, 1.0, 0.0)
    cnt = oh1 + oh2
    earlier = jnp.where(lax.broadcasted_iota(I32, (tm, tm), 0) < lax.broadcasted_iota(I32, (tm, tm), 1),
                        1.0, 0.0).astype(BF16)
    run = run_ref[...]
    pos = run[:, 0:1] + jnp.dot(cnt.astype(BF16), earlier, preferred_element_type=F32)
    rank1 = jnp.sum(oh1 * pos, axis=0, keepdims=True)
    rank2 = jnp.sum(oh2 * pos, axis=0, keepdims=True)
    run = run + jnp.sum(cnt, axis=1, keepdims=True)
    run_ref[...] = run
    cnt_ref[...] = run
    zero = jnp.zeros_like(v1)
    o_ref[...] = jnp.concatenate([e1, e2, gate1, gate2, rank1, rank2, zero, zero], axis=0)


def moe_router(x2d, g, wg, bg, we, be, tm=512):
    t, d = x2d.shape
    wt = jnp.zeros((ROUTE_ROWS, d), F32).at[:N_GROUPS].set(wg.T).at[EXPERT_ROW0:EXPERT_ROW0 + N_EXPERTS].set(we.T)
    bias = jnp.zeros((ROUTE_ROWS, 1), F32).at[:N_GROUPS, 0].set(bg).at[EXPERT_ROW0:EXPERT_ROW0 + N_EXPERTS, 0].set(be)
    return pl.pallas_call(
        _router_body, grid=(t // tm,),
        in_specs=[pl.BlockSpec((tm, d), lambda i: (i, 0)), pl.BlockSpec((1, d), lambda i: (0, 0)),
                  pl.BlockSpec((ROUTE_ROWS, d), lambda i: (0, 0)), pl.BlockSpec((ROUTE_ROWS, 1), lambda i: (0, 0))],
        out_specs=[pl.BlockSpec((8, tm), lambda i: (0, i)), pl.BlockSpec((N_EXPERTS, 128), lambda i: (0, 0))],
        out_shape=[jax.ShapeDtypeStruct((8, t), F32), jax.ShapeDtypeStruct((N_EXPERTS, 128), F32)],
        scratch_shapes=[pltpu.VMEM((N_EXPERTS, 128), F32)],
        compiler_params=_params("arbitrary"), name="moe_router",
    )(x2d, g.reshape(1, d), wt, bias)


def _dispatch_body(dest_ref, zblk_ref, x_ref, g_ref, xs_ref, buf_ref, sem_ref, *, td):
    i = pl.program_id(0)
    n = pl.num_programs(0)
    slot = i % 2

    @pl.when(i == 0)
    def _():
        buf_ref[1] = jnp.zeros(buf_ref.shape[1:], buf_ref.dtype)

        def zero_copy(j):
            return pltpu.make_async_copy(buf_ref.at[1], xs_ref.at[pl.ds(zblk_ref[j] * td, td), :], sem_ref.at[1])

        def start(j, c):
            @pl.when(zblk_ref[j] >= 0)
            def _():
                zero_copy(j).start()
            return c

        def wait(j, c):
            @pl.when(zblk_ref[j] >= 0)
            def _():
                zero_copy(j).wait()
            return c
        lax.fori_loop(0, zblk_ref.shape[0], start, 0)
        lax.fori_loop(0, zblk_ref.shape[0], wait, 0)

    def row_copy(r, kk):
        return pltpu.make_async_copy(buf_ref.at[slot, pl.ds(r, 1), :],
                                     xs_ref.at[pl.ds(dest_ref[(i * td + r) * 2 + kk], 1), :],
                                     sem_ref.at[slot])

    def wait_buffer(sl):
        for _ in range(2):
            pltpu.make_async_copy(buf_ref.at[sl], xs_ref.at[pl.ds(0, td), :], sem_ref.at[sl]).wait()

    @pl.when(i >= 2)
    def _():
        wait_buffer(slot)

    x = x_ref[...]
    ms = jnp.mean(x * x, axis=-1, keepdims=True)
    buf_ref[slot] = x * lax.rsqrt(ms + EPS) * g_ref[...]

    def issue(r, c):
        row_copy(r, 0).start()
        row_copy(r, 1).start()
        return c
    lax.fori_loop(0, td, issue, 0, unroll=8)

    @pl.when(i == n - 1)
    def _():
        @pl.when(n >= 2)
        def _():
            wait_buffer(1 - slot)
        wait_buffer(slot)


def moe_dispatch(x2d, g, dest, zero_blk, cap, td):
    t, d = x2d.shape
    grid_spec = pltpu.PrefetchScalarGridSpec(
        num_scalar_prefetch=2, grid=(t // td,),
        in_specs=[pl.BlockSpec((td, d), lambda i, dest, zb: (i, 0)),
                  pl.BlockSpec((1, d), lambda i, dest, zb: (0, 0))],
        out_specs=pl.BlockSpec(memory_space=pl.ANY),
        scratch_shapes=[pltpu.VMEM((2, td, d), F32), pltpu.SemaphoreType.DMA((2,))])
    return pl.pallas_call(
        functools.partial(_dispatch_body, td=td), grid_spec=grid_spec,
        out_shape=jax.ShapeDtypeStruct((cap, d), F32),
        compiler_params=_params("arbitrary", disable_bounds_checks=True),
        name="moe_dispatch",
    )(dest, zero_blk, x2d, g.reshape(1, d))


def _expert_body(blk_e_ref, nused_ref, xs_ref, w1_ref, w3_ref, w2_ref, ys_ref, w1b_ref, w3b_ref, w2b_ref):
    i = pl.program_id(0)
    nused = nused_ref[0]
    last = nused - 1
    cur = blk_e_ref[jnp.minimum(i, last)]
    prev = blk_e_ref[jnp.minimum(jnp.maximum(i - 1, 0), last)]

    @pl.when((i == 0) | (cur != prev))
    def _():
        w1b_ref[...] = w1_ref[...].astype(BF16)
        w3b_ref[...] = w3_ref[...].astype(BF16)
        w2b_ref[...] = w2_ref[...].astype(BF16)

    @pl.when(i < nused)
    def _():
        x = xs_ref[...].astype(BF16)
        h1 = jnp.dot(x, w1b_ref[...], preferred_element_type=F32)
        h3 = jnp.dot(x, w3b_ref[...], preferred_element_type=F32)
        hid = (h1 * _sigmoid(h1) * h3).astype(BF16)
        ys_ref[...] = jnp.dot(hid, w2b_ref[...], preferred_element_type=F32)

    @pl.when(i >= nused)
    def _():
        ys_ref[...] = jnp.zeros_like(ys_ref)


def moe_experts(xs, blk_e, nused, w1, w3, w2, layer, tm):
    cap, d = xs.shape
    de = w1.shape[-1]

    def blk(i, be, nu):
        return jnp.minimum(i, nu[0] - 1)

    grid_spec = pltpu.PrefetchScalarGridSpec(
        num_scalar_prefetch=2, grid=(cap // tm,),
        in_specs=[pl.BlockSpec((tm, d), lambda i, be, nu: (blk(i, be, nu), 0)),
                  pl.BlockSpec((None, None, d, de), lambda i, be, nu: (layer, be[blk(i, be, nu)], 0, 0)),
                  pl.BlockSpec((None, None, d, de), lambda i, be, nu: (layer, be[blk(i, be, nu)], 0, 0)),
                  pl.BlockSpec((None, None, de, d), lambda i, be, nu: (layer, be[blk(i, be, nu)], 0, 0))],
        out_specs=pl.BlockSpec((tm, d), lambda i, be, nu: (i, 0)),
        scratch_shapes=[pltpu.VMEM((d, de), BF16), pltpu.VMEM((d, de), BF16), pltpu.VMEM((de, d), BF16)])
    return pl.pallas_call(
        _expert_body, grid_spec=grid_spec, out_shape=jax.ShapeDtypeStruct((cap, d), F32),
        compiler_params=_params("arbitrary"), name="moe_experts",
    )(blk_e, nused, xs, w1, w3, w2)


def _combine_body(dest_ref, x_ref, gate_ref, ys_ref, o_ref, buf_ref, sem_ref, *, tc):
    i = pl.program_id(0)
    n = pl.num_programs(0)
    slot = i % 2

    def row_copy(step, sl, r, kk):
        return pltpu.make_async_copy(ys_ref.at[pl.ds(dest_ref[(step * tc + r) * 2 + kk], 1), :],
                                     buf_ref.at[sl, kk, pl.ds(r, 1), :], sem_ref.at[sl])

    def issue_step(step, sl):
        def body(r, c):
            row_copy(step, sl, r, 0).start()
            row_copy(step, sl, r, 1).start()
            return c
        lax.fori_loop(0, tc, body, 0, unroll=8)

    @pl.when(i == 0)
    def _():
        issue_step(0, 0)

    @pl.when(i + 1 < n)
    def _():
        issue_step(i + 1, 1 - slot)

    for kk in range(2):
        pltpu.make_async_copy(ys_ref.at[pl.ds(0, tc), :], buf_ref.at[slot, kk], sem_ref.at[slot]).wait()

    gate = gate_ref[...]
    o_ref[...] = x_ref[...] + gate[:, 0:1] * buf_ref[slot, 0] + gate[:, 1:2] * buf_ref[slot, 1]


def moe_combine(x2d, gates, ys, dest, tc=256):
    t, d = x2d.shape
    grid_spec = pltpu.PrefetchScalarGridSpec(
        num_scalar_prefetch=1, grid=(t // tc,),
        in_specs=[pl.BlockSpec((tc, d), lambda i, dest: (i, 0)), pl.BlockSpec((tc, 2), lambda i, dest: (i, 0)),
                  pl.BlockSpec(memory_space=pl.ANY)],
        out_specs=pl.BlockSpec((tc, d), lambda i, dest: (i, 0)),
        scratch_shapes=[pltpu.VMEM((2, 2, tc, d), F32), pltpu.SemaphoreType.DMA((2,))])
    return pl.pallas_call(
        functools.partial(_combine_body, tc=tc), grid_spec=grid_spec,
        out_shape=jax.ShapeDtypeStruct((t, d), F32),
        compiler_params=_params("arbitrary", disable_bounds_checks=True), name="moe_combine",
    )(dest, x2d, gates, ys)


def hier_moe_residual(x2d, g, wg, bg, we, be, w1, w3, w2, layer, tm=512):
    t, d = x2d.shape
    route, counts_b = moe_router(x2d, g, wg, bg, we, be)
    eid = route[0:2].T.astype(I32).reshape(-1)
    rank = route[4:6].T.astype(I32).reshape(-1)
    gates = route[2:4].T
    counts = counts_b[:, 0].astype(I32)
    padded = (counts + tm - 1) // tm * tm
    pad_end = jnp.cumsum(padded)
    pad_start = pad_end - padded
    experts = jnp.arange(N_EXPERTS, dtype=I32)
    dest = rank + jnp.sum(jnp.where(eid[:, None] == experts[None, :], pad_start[None, :], 0), axis=1)
    cap = t * 2 + N_EXPERTS * tm
    nblk = cap // tm
    blk_row0 = jnp.arange(nblk, dtype=I32) * tm
    blk_e = jnp.minimum(jnp.sum((pad_end[None, :] <= blk_row0[:, None]).astype(I32), axis=1), N_EXPERTS - 1)
    nused = (pad_end[-1:] // tm).astype(I32)
    last_blk = jnp.where(padded > 0, pad_end // tm - 1, -1)
    tail_blk = jnp.where(nused[0] + experts < nblk, nused[0] + experts, -1)
    zero_blk = jnp.concatenate([last_blk, tail_blk]).astype(I32)
    xs = moe_dispatch(x2d, g, dest.astype(I32), zero_blk, cap, tm)
    ys = moe_experts(xs, blk_e, nused, w1, w3, w2, layer, tm)
    return moe_combine(x2d, gates, ys, dest.astype(I32))


def _even_layer(x2d, b, s, layer, mix_g, w_in, q_norm, k_norm, lam_q1, lam_k1, lam_q2, lam_k2, subln,
                hy_conv_w, hy_conv_b, f_w1, f_b1, f_freq, f_w2, f_b2, f_w3, hy_skip, w_out):
    d = x2d.shape[1]
    d_att = d // 2
    (u2d,) = norm_matmul(x2d, mix_g, [w_in.astype(BF16)], [BF16])
    u = u2d.reshape(b, s, -1)
    lambda_init = 0.8 - 0.6 * math.exp(-0.3 * layer)
    lam = jnp.exp(jnp.sum(lam_q1 * lam_k1)) - jnp.exp(jnp.sum(lam_q2 * lam_k2)) + lambda_init
    dk = q_norm.shape[0]
    qt = head_prep(u, 0, q_norm, scale=dk ** -0.5 * math.log2(math.e), transpose=True)
    kp = head_prep(u, ATT_HEADS, k_norm)
    vt = head_prep(u, 2 * ATT_HEADS, transpose=True)
    y_att = diff_attention(qt, kp, vt, lam, subln, lambda_init)
    z, x1c, zp = hy_prep(u, 3 * d_att, hy_conv_w, hy_conv_b)
    tables = fft_tables(s)
    kf = filter_spectrum(hyena_filter(s, f_w1, f_b1, f_freq, f_w2, f_b2, f_w3), tables[0], tables[1])
    y_hy = hy_fft_conv(zp, z, x1c, kf, tables, hy_skip)
    w_out = w_out.astype(BF16)
    return matmul_residual(x2d, [y_att.reshape(b * s, -1), y_hy.reshape(b * s, -1)],
                           [w_out[:d_att], w_out[d_att:]])


def _odd_layer(x2d, b, s, mix_g, w_in, conv_w, conv_b, gate_b, out_norm, w_out):
    d = x2d.shape[1]
    qk_w = conv_w.shape[1]
    main_w = qk_w + 2 * d
    ng = 4 * ML_HEADS
    w_gate = jnp.zeros((d, 128), F32).at[:, :ng].set(w_in[:, main_w:]).astype(BF16)
    u2d, ug = norm_matmul(x2d, mix_g, [w_in[:, :main_w].astype(BF16), w_gate], [BF16, F32])
    u = u2d.reshape(b, s, main_w)
    gcol = (ug[:, :ng] + gate_b).reshape(b, s, ng)
    grow = jnp.swapaxes(gcol, 1, 2)
    dk = qk_w // (2 * ML_HEADS)
    col_scale = jnp.concatenate([jnp.full((qk_w // 2,), dk ** -0.5, F32), jnp.ones((qk_w // 2,), F32)])
    qk = ml_prep(u, conv_w, conv_b, col_scale)
    hf, hb = mlstm_scan(qk, u, qk_w // d, gcol, grow)
    return mlstm_out(x2d, hf.reshape(b * s, d), hb.reshape(b * s, d), u2d, (qk_w + d) // d, out_norm,
                     w_out.astype(BF16))


def kernel(x, mix_norm, ffn_norm, ev_w_in, ev_q_norm, ev_k_norm, ev_lam_q1, ev_lam_k1, ev_lam_q2, ev_lam_k2, ev_subln, ev_hy_conv_w, ev_hy_conv_b, ev_hy_f_w1, ev_hy_f_b1, ev_hy_f_freq, ev_hy_f_w2, ev_hy_f_b2, ev_hy_f_w3, ev_hy_skip, ev_w_out, od_w_in, od_conv_w, od_conv_b, od_gate_b, od_out_norm, od_w_out, moe_wg, moe_bg, moe_we, moe_be, moe_w1, moe_w3, moe_w2):
    b, s, d = x.shape
    depth = mix_norm.shape[0]
    x2d = x.reshape(b * s, d)
    for layer in range(depth):
        j = layer // 2
        if layer % 2 == 0:
            x2d = _even_layer(x2d, b, s, layer, mix_norm[layer], ev_w_in[j], ev_q_norm[j], ev_k_norm[j],
                              ev_lam_q1[j], ev_lam_k1[j], ev_lam_q2[j], ev_lam_k2[j], ev_subln[j],
                              ev_hy_conv_w[j], ev_hy_conv_b[j], ev_hy_f_w1[j], ev_hy_f_b1[j], ev_hy_f_freq[j],
                              ev_hy_f_w2[j], ev_hy_f_b2[j], ev_hy_f_w3[j], ev_hy_skip[j], ev_w_out[j])
        else:
            x2d = _odd_layer(x2d, b, s, mix_norm[layer], od_w_in[j], od_conv_w[j], od_conv_b[j], od_gate_b[j],
                             od_out_norm[j], od_w_out[j])
        x2d = hier_moe_residual(x2d, ffn_norm[layer], moe_wg[layer], moe_bg[layer], moe_we[layer],
                                moe_be[layer], moe_w1, moe_w3, moe_w2, layer)
    return x2d.reshape(b, s, d)
```

```python
import functools
import math

import jax
import jax.numpy as jnp
from jax import lax
from jax.experimental import pallas as pl
from jax.experimental.pallas import tpu as pltpu

F32 = jnp.float32
BF16 = jnp.bfloat16
I32 = jnp.int32

EPS = 1e-6
ROPE_THETA = 500000.0
ATT_HEADS = 4
ML_HEADS = 4
ML_CHUNK = 128
ML_BATCH_PER_STEP = 2
N_GROUPS = 4
EXPERTS_PER_GROUP = 8
N_EXPERTS = N_GROUPS * EXPERTS_PER_GROUP
HY_EMB_BANDS = 16
HY_MIN_DECAY = math.log(1e-2) / 1.5
HY_MAX_DECAY = math.log(1e-2) / 0.3

V7X_VMEM_BYTES = 64 * 1024 * 1024
VMEM_LIMIT = V7X_VMEM_BYTES - 8 * 1024 * 1024
NEG_BIG = -1e30


def _params(*sem, **kw):
    return pltpu.CompilerParams(dimension_semantics=sem, vmem_limit_bytes=VMEM_LIMIT, **kw)


def _sigmoid(x):
    return 1.0 / (1.0 + jnp.exp(-x))


def _norm_matmul_body(x_ref, g_ref, *refs, n_out, col_chunk):
    w_refs, o_refs = refs[:n_out], refs[n_out:]
    x = x_ref[...]
    ms = jnp.mean(x * x, axis=-1, keepdims=True)
    hn = (x * lax.rsqrt(ms + EPS) * g_ref[...]).astype(BF16)
    for w_ref, o_ref in zip(w_refs, o_refs):
        n = w_ref.shape[1]
        for c in range(0, n, col_chunk):
            ce = min(n, c + col_chunk)
            o_ref[:, c:ce] = jnp.dot(hn, w_ref[:, c:ce], preferred_element_type=F32).astype(o_ref.dtype)


def norm_matmul(x2d, g, ws, out_dtypes, tm=512):
    t, d = x2d.shape
    in_specs = [pl.BlockSpec((tm, d), lambda i: (i, 0)), pl.BlockSpec((1, d), lambda i: (0, 0))]
    in_specs += [pl.BlockSpec(w.shape, lambda i: (0, 0)) for w in ws]
    out_specs = [pl.BlockSpec((tm, w.shape[1]), lambda i: (i, 0)) for w in ws]
    out_shape = [jax.ShapeDtypeStruct((t, w.shape[1]), dt) for w, dt in zip(ws, out_dtypes)]
    return pl.pallas_call(
        functools.partial(_norm_matmul_body, n_out=len(ws), col_chunk=1024),
        grid=(t // tm,), in_specs=in_specs, out_specs=out_specs, out_shape=out_shape,
        compiler_params=_params("parallel"), name="norm_matmul",
    )(x2d, g.reshape(1, d), *ws)


def _matmul_res_body(res_ref, *refs, n_in):
    a_refs, w_refs, o_ref = refs[:n_in], refs[n_in:2 * n_in], refs[2 * n_in]
    acc = res_ref[...]
    for a_ref, w_ref in zip(a_refs, w_refs):
        acc = acc + jnp.dot(a_ref[...], w_ref[...], preferred_element_type=F32)
    o_ref[...] = acc


def matmul_residual(res, a_list, w_list, tm=512):
    t, d = res.shape
    in_specs = [pl.BlockSpec((tm, d), lambda i: (i, 0))]
    in_specs += [pl.BlockSpec((tm, a.shape[1]), lambda i: (i, 0)) for a in a_list]
    in_specs += [pl.BlockSpec(w.shape, lambda i: (0, 0)) for w in w_list]
    return pl.pallas_call(
        functools.partial(_matmul_res_body, n_in=len(a_list)),
        grid=(t // tm,), in_specs=in_specs, out_specs=pl.BlockSpec((tm, d), lambda i: (i, 0)),
        out_shape=jax.ShapeDtypeStruct((t, d), F32),
        compiler_params=_params("parallel"), name="matmul_residual",
    )(res, *a_list, *w_list)


def _head_prep_body(u_ref, *refs, dk, rope, transpose):
    o_ref = refs[-1]
    x = u_ref[...].astype(F32)
    if rope:
        g_ref, c_ref, s1_ref, s2_ref = refs[:4]
        lane = lax.broadcasted_iota(I32, x.shape, 1)
        lo = lane < dk
        x2 = x * x
        s_lo = jnp.sum(jnp.where(lo, x2, 0.0), axis=-1, keepdims=True)
        s_hi = jnp.sum(jnp.where(lo, 0.0, x2), axis=-1, keepdims=True)
        ms = jnp.where(lo, s_lo, s_hi) * (1.0 / dk)
        y = x * lax.rsqrt(ms + EPS) * g_ref[...]
        x = y * c_ref[...] + pltpu.roll(y, 120, 1) * s1_ref[...] + pltpu.roll(y, 8, 1) * s2_ref[...]
    if transpose:
        x = x.T
    o_ref[...] = x.astype(o_ref.dtype)


def _rope_lane_tables(seq, dk, rope_dim, scale):
    half = rope_dim // 2
    inv_freq = 1.0 / (ROPE_THETA ** (jnp.arange(0, rope_dim, 2, dtype=F32) / rope_dim))
    ang = jnp.arange(seq, dtype=F32)[:, None] * inv_freq[None, :]
    cos, sin = jnp.cos(ang), jnp.sin(ang)
    d = jnp.arange(2 * dk) % dk
    fi = d % half
    c_tab = jnp.where(d[None, :] < rope_dim, cos[:, fi], 1.0)
    s1_tab = jnp.where(d[None, :] < half, -sin[:, fi], 0.0)
    s2_tab = jnp.where((d[None, :] >= half) & (d[None, :] < rope_dim), sin[:, fi], 0.0)
    return (jnp.stack([c_tab, s1_tab, s2_tab]) * scale).astype(F32)


def head_prep(u, blk0, norm_gain=None, scale=1.0, transpose=False):
    b, s, _ = u.shape
    h = ATT_HEADS
    rope = norm_gain is not None
    in_specs = [pl.BlockSpec((None, s, 128), lambda c, bi: (bi, 0, blk0 + c))]
    args = [u]
    dk = 64
    if rope:
        dk = norm_gain.shape[0]
        assert 2 * dk == 128 and dk // 4 == 16, "rope roll shifts assume 64-wide components, 16 rotary dims"
        tabs = _rope_lane_tables(s, dk, dk // 4, scale)
        in_specs += [pl.BlockSpec((1, 128), lambda c, bi: (0, 0))] + [pl.BlockSpec((s, 128), lambda c, bi: (0, 0))] * 3
        args += [jnp.tile(norm_gain, 2).reshape(1, 128).astype(F32), tabs[0], tabs[1], tabs[2]]
    if transpose:
        out_spec = pl.BlockSpec((None, 128, s), lambda c, bi: (bi, c, 0))
        out_shape = jax.ShapeDtypeStruct((b, h * 128, s), BF16)
    else:
        out_spec = pl.BlockSpec((None, s, 128), lambda c, bi: (bi, 0, c))
        out_shape = jax.ShapeDtypeStruct((b, s, h * 128), BF16)
    return pl.pallas_call(
        functools.partial(_head_prep_body, dk=dk, rope=rope, transpose=transpose), grid=(h, b),
        in_specs=in_specs, out_specs=out_spec, out_shape=out_shape,
        compiler_params=_params("parallel", "parallel"), name="head_prep",
    )(*args)


def _attn_body(lam_ref, qt_ref, k_ref, vt_ref, g_ref, o_ref, *, tq, dk, post_scale, n_split):
    lam = lam_ref[0, 0]
    th = tq // n_split
    for part in range(n_split):
        qt = qt_ref[:, part * th:(part + 1) * th]
        row = lax.broadcasted_iota(I32, qt.shape, 0)
        zero = jnp.zeros_like(qt)
        qq = jnp.concatenate([jnp.where(row < dk, qt, zero), jnp.where(row < dk, zero, qt)], axis=1)
        st = jnp.dot(k_ref[...], qq, preferred_element_type=F32)
        m = jnp.max(st, axis=0, keepdims=True)
        p = jnp.exp2(st - m)
        r = 1.0 / jnp.sum(p, axis=0, keepdims=True)
        ot = jnp.dot(vt_ref[...], p.astype(BF16), preferred_element_type=F32)
        o = (ot[:, :th] * r[:, :th] - ot[:, th:] * (lam * r[:, th:])).T
        ms = jnp.mean(o * o, axis=-1, keepdims=True)
        o_ref[part * th:(part + 1) * th, :] = (o * lax.rsqrt(ms + EPS) * g_ref[...] * post_scale).astype(o_ref.dtype)


def diff_attention(qt, k, vt, lam, subln, lambda_init, tq=256):
    b, s, _ = k.shape
    h = ATT_HEADS
    return pl.pallas_call(
        functools.partial(_attn_body, tq=tq, dk=64, post_scale=1.0 - lambda_init, n_split=1),
        grid=(b, h, s // tq),
        in_specs=[pl.BlockSpec(memory_space=pltpu.SMEM),
                  pl.BlockSpec((None, 128, tq), lambda bi, hi, i: (bi, hi, i)),
                  pl.BlockSpec((None, s, 128), lambda bi, hi, i: (bi, 0, hi)),
                  pl.BlockSpec((None, 128, s), lambda bi, hi, i: (bi, hi, 0)),
                  pl.BlockSpec((1, 128), lambda bi, hi, i: (0, 0))],
        out_specs=pl.BlockSpec((None, tq, 128), lambda bi, hi, i: (bi, i, hi)),
        out_shape=jax.ShapeDtypeStruct((b, s, h * 128), BF16),
        compiler_params=_params("parallel", "parallel", "parallel"), name="diff_attention",
    )(lam.reshape(1, 1).astype(F32), qt, k, vt, subln.reshape(1, 128).astype(F32))


def _conv3(u_ref, w_ref, b_ref):
    x = u_ref[...].astype(F32)
    s = x.shape[0]
    row = lax.broadcasted_iota(I32, x.shape, 0)
    x_prev = jnp.where(row == 0, 0.0, pltpu.roll(x, 1, 0))
    x_next = jnp.where(row == s - 1, 0.0, pltpu.roll(x, s - 1, 0))
    w = w_ref[...]
    return b_ref[...] + x_prev * w[0:1] + x * w[1:2] + x_next * w[2:3]


FFT_N1 = 64
FFT_UNROLL = 8
FFT_PAD = 8


def _hy_prep_body(x1_ref, x2_ref, v_ref, w1_ref, w2_ref, wv_ref, b1_ref, b2_ref, bv_ref, z_ref, x1c_ref, zp_ref):
    x1c_ref[...] = _conv3(x1_ref, w1_ref, b1_ref).astype(x1c_ref.dtype)
    z = _conv3(v_ref, wv_ref, bv_ref) * _conv3(x2_ref, w2_ref, b2_ref)
    z_ref[...] = z.astype(z_ref.dtype)
    nb = z.shape[0] // FFT_N1
    zp_ref[...] = jnp.zeros_like(zp_ref)
    for n2 in range(nb):
        for ci in range(z.shape[1] // 128):
            zp_ref[ci, pl.ds(n2, FFT_N1, stride=nb + FFT_PAD), :] = (
                z[n2 * FFT_N1:(n2 + 1) * FFT_N1, ci * 128:(ci + 1) * 128])


def hy_prep(u, col0, conv_w, conv_b, tc=256):
    b, s, _ = u.shape
    d_hy = conv_w.shape[1] // 3
    nct = d_hy // tc
    blk0 = col0 // tc
    sp = FFT_N1 * (s // FFT_N1 + FFT_PAD)

    def uspec(part):
        return pl.BlockSpec((None, s, tc), lambda bi, c: (bi, 0, blk0 + part * nct + c))

    def wspec(part, rows):
        return pl.BlockSpec((rows, tc), lambda bi, c: (0, part * nct + c))

    ospec = pl.BlockSpec((None, s, tc), lambda bi, c: (bi, 0, c))
    return pl.pallas_call(
        _hy_prep_body, grid=(b, nct),
        in_specs=[uspec(0), uspec(1), uspec(2), wspec(0, 3), wspec(1, 3), wspec(2, 3),
                  wspec(0, 1), wspec(1, 1), wspec(2, 1)],
        out_specs=[ospec, ospec, pl.BlockSpec((None, tc // 128, sp, 128), lambda bi, c: (bi, c, 0, 0))],
        out_shape=[jax.ShapeDtypeStruct((b, s, d_hy), BF16), jax.ShapeDtypeStruct((b, s, d_hy), BF16),
                   jax.ShapeDtypeStruct((b, d_hy // 128, sp, 128), F32)],
        compiler_params=_params("parallel", "parallel"), name="hy_prep",
    )(u, u, u, conv_w, conv_w, conv_w, conv_b.reshape(1, -1), conv_b.reshape(1, -1), conv_b.reshape(1, -1))


def hyena_filter(length, w1, b1, freq, w2, b2, w3):
    d_hy = w3.shape[1] // 2
    t = jnp.linspace(0.0, 1.0, length, dtype=F32)[:, None]
    bands = jnp.linspace(1e-4, HY_EMB_BANDS - 1, HY_EMB_BANDS, dtype=F32)[None, :]
    ang = (2.0 * math.pi / length) * jnp.arange(length, dtype=F32)[:, None] * bands
    z = jnp.concatenate([t, jnp.cos(ang), -jnp.sin(ang)], axis=-1)
    hp = lax.Precision.HIGHEST
    hdn = jnp.sin(freq * (jnp.dot(z, w1, precision=hp) + b1))
    hdn = jnp.sin(freq * (jnp.dot(hdn, w2, precision=hp) + b2))
    filt = jnp.dot(hdn, w3, precision=hp)
    deltas = jnp.abs(jnp.linspace(HY_MIN_DECAY, HY_MAX_DECAY, d_hy, dtype=F32))
    decay = jnp.exp(-t * deltas[None, :])
    h_fwd = filt[:, :d_hy] * decay
    h_bwd = filt[:, d_hy:] * decay
    h_fwd = h_fwd.at[0].add(h_bwd[0])
    h_bwd = h_bwd.at[0].set(0.0)
    norm = jnp.sum(jnp.abs(h_fwd), axis=0, keepdims=True) + jnp.sum(jnp.abs(h_bwd), axis=0, keepdims=True) + EPS
    return jnp.concatenate([h_fwd / norm, h_bwd / norm], axis=1)


def fft_tables(length):
    n = 2 * length
    n1c, n2c, nb = FFT_N1, n // FFT_N1, length // FFT_N1
    unit = 2.0 * math.pi / n
    i1 = jnp.arange(n1c, dtype=I32)
    i2 = jnp.arange(n2c, dtype=I32)
    ib = jnp.arange(nb, dtype=I32)
    samp = i1[:, None, None] + n1c * ib[None, None, :]
    ang = ((i2[None, :, None] * samp) % n).astype(F32) * unit
    m1 = jnp.concatenate([jnp.cos(ang), -jnp.sin(ang)], axis=1)
    ang = ((i1[:, None] * i1[None, :]) % n1c).astype(F32) * (2.0 * math.pi / n1c)
    c, s = jnp.cos(ang), jnp.sin(ang)
    f1 = jnp.concatenate([jnp.concatenate([c, s], axis=1), jnp.concatenate([-s, c], axis=1)], axis=0)
    freq = n2c * i1[None, None, :] + i2[:, None, None]
    ang = ((i1[None, :, None] * freq) % n).astype(F32) * unit
    c, s = jnp.cos(ang), jnp.sin(ang)
    g1 = jnp.concatenate([jnp.concatenate([c, -s], axis=2), jnp.concatenate([s, c], axis=2)], axis=1)
    ang = ((ib[:, None] * i2[None, :]) % n2c).astype(F32) * (2.0 * math.pi / n2c)
    g2 = jnp.concatenate([jnp.cos(ang), -jnp.sin(ang)], axis=1) * (1.0 / n)
    return m1.astype(BF16), f1.astype(BF16), g1.astype(BF16), g2.astype(BF16)


def _fft_stage1(xp_ref, m1_ref, p_ref):
    nb = m1_ref.shape[2]
    n2c = m1_ref.shape[1] // 2

    def body(n1, c):
        x = xp_ref[pl.ds(pl.multiple_of(n1 * (nb + FFT_PAD), 8), nb), :].astype(BF16)
        a = jnp.dot(m1_ref[n1], x, preferred_element_type=F32)
        p_ref[0, pl.ds(n1, n2c, stride=FFT_N1 + FFT_PAD), :] = a[:n2c]
        p_ref[1, pl.ds(n1, n2c, stride=FFT_N1 + FFT_PAD), :] = a[n2c:]
        return c
    lax.fori_loop(0, FFT_N1, body, 0, unroll=FFT_UNROLL)


def _fft_stage2(p_ref, f1_ref, k2):
    r0 = pl.multiple_of(k2 * (FFT_N1 + FFT_PAD), 8)
    slab = jnp.concatenate([p_ref[0, pl.ds(r0, FFT_N1), :], p_ref[1, pl.ds(r0, FFT_N1), :]], axis=0)
    return jnp.dot(f1_ref[...], slab.astype(BF16), preferred_element_type=F32)


def _spectrum_body(xp_ref, m1_ref, f1_ref, o_ref, p_ref):
    _fft_stage1(xp_ref, m1_ref, p_ref)

    def body(k2, c):
        o_ref[k2] = _fft_stage2(p_ref, f1_ref, k2)
        return c
    lax.fori_loop(0, o_ref.shape[0], body, 0, unroll=FFT_UNROLL)


def filter_spectrum(ab, m1, f1):
    length, c2 = ab.shape
    nb = length // FFT_N1
    n2c = m1.shape[1] // 2
    nch = c2 // 128
    abp = jnp.pad(ab.reshape(nb, FFT_N1, nch, 128).transpose(2, 1, 0, 3), ((0, 0), (0, 0), (0, FFT_PAD), (0, 0)))
    abp = abp.reshape(nch, FFT_N1 * (nb + FFT_PAD), 128)
    spec = pl.pallas_call(
        _spectrum_body, grid=(nch,),
        in_specs=[pl.BlockSpec((None,) + abp.shape[1:], lambda c: (c, 0, 0)),
                  pl.BlockSpec(m1.shape, lambda c: (0, 0, 0)), pl.BlockSpec(f1.shape, lambda c: (0, 0))],
        out_specs=pl.BlockSpec((None, n2c, 2 * FFT_N1, 128), lambda c: (c, 0, 0, 0)),
        out_shape=jax.ShapeDtypeStruct((nch, n2c, 2 * FFT_N1, 128), F32),
        scratch_shapes=[pltpu.VMEM((2, n2c * (FFT_N1 + FFT_PAD), 128), F32)],
        compiler_params=_params("parallel"), name="filter_spectrum",
    )(abp, m1, f1)
    fa, fb = spec[:nch // 2], spec[nch // 2:]
    h = FFT_N1
    return jnp.concatenate([fa[:, :, :h] + fb[:, :, :h], fa[:, :, h:] - fb[:, :, h:]], axis=2).astype(BF16)


def _hy_fft_body(zp_ref, z_ref, x1c_ref, kf_ref, m1_ref, f1_ref, g1_ref, g2_ref, skip_ref, o_ref,
                 p_ref, q_ref, y_ref):
    h = FFT_N1
    n2c = g1_ref.shape[0]
    nb = g2_ref.shape[0]
    _fft_stage1(zp_ref, m1_ref, p_ref)

    def mid(k2, c):
        xf = _fft_stage2(p_ref, f1_ref, k2)
        kf = kf_ref[k2].astype(F32)
        xr, xi, kr, ki = xf[:h], xf[h:], kf[:h], kf[h:]
        y = jnp.concatenate([xr * kr - xi * ki, xr * ki + xi * kr], axis=0).astype(BF16)
        d = jnp.dot(g1_ref[k2], y, preferred_element_type=F32)
        q_ref[0, pl.ds(k2, h, stride=n2c + FFT_PAD), :] = d[:h]
        q_ref[1, pl.ds(k2, h, stride=n2c + FFT_PAD), :] = d[h:]
        return c
    lax.fori_loop(0, n2c, mid, 0, unroll=FFT_UNROLL)

    def last(t1, c):
        r0 = pl.multiple_of(t1 * (n2c + FFT_PAD), 8)
        slab = jnp.concatenate([q_ref[0, pl.ds(r0, n2c), :], q_ref[1, pl.ds(r0, n2c), :]], axis=0)
        y_ref[pl.ds(t1, nb, stride=h), :] = jnp.dot(g2_ref[...], slab.astype(BF16), preferred_element_type=F32)
        return c
    lax.fori_loop(0, h, last, 0, unroll=FFT_UNROLL)

    z = z_ref[...].astype(F32)
    o_ref[...] = ((y_ref[...] + z * skip_ref[...]) * x1c_ref[...].astype(F32)).astype(o_ref.dtype)


def hy_fft_conv(zp, z, x1c, kf, tables, skip):
    m1, f1, g1, g2 = tables
    b, s, c = z.shape
    nch = c // 128

    def const(shape):
        return pl.BlockSpec(shape, lambda ci, bi: (0,) * len(shape))

    nat = pl.BlockSpec((None, s, 128), lambda ci, bi: (bi, 0, ci))
    return pl.pallas_call(
        _hy_fft_body, grid=(nch, b),
        in_specs=[pl.BlockSpec((None, None) + zp.shape[2:], lambda ci, bi: (bi, ci, 0, 0)), nat, nat,
                  pl.BlockSpec((None,) + kf.shape[1:], lambda ci, bi: (ci, 0, 0, 0)),
                  const(m1.shape), const(f1.shape), const(g1.shape), const(g2.shape),
                  pl.BlockSpec((1, 128), lambda ci, bi: (0, ci))],
        out_specs=nat,
        out_shape=jax.ShapeDtypeStruct((b, s, c), BF16),
        scratch_shapes=[pltpu.VMEM((2, g1.shape[0] * (FFT_N1 + FFT_PAD), 128), F32),
                        pltpu.VMEM((2, FFT_N1 * (g1.shape[0] + FFT_PAD), 128), F32),
                        pltpu.VMEM((s, 128), F32)],
        compiler_params=_params("parallel", "parallel"), name="hy_fft_conv",
    )(zp, z, x1c, kf, m1, f1, g1, g2, skip.reshape(1, c).astype(F32))


def _ml_prep_body(u_ref, w_ref, b_ref, sc_ref, o_ref):
    y = _conv3(u_ref, w_ref, b_ref)
    o_ref[...] = (y * _sigmoid(y) * sc_ref[...]).astype(o_ref.dtype)


def ml_prep(u, conv_w, conv_b, col_scale, tc=256):
    b, s, _ = u.shape
    w = conv_w.shape[1]
    return pl.pallas_call(
        _ml_prep_body, grid=(b, w // tc),
        in_specs=[pl.BlockSpec((None, s, tc), lambda bi, c: (bi, 0, c)),
                  pl.BlockSpec((3, tc), lambda bi, c: (0, c)),
                  pl.BlockSpec((1, tc), lambda bi, c: (0, c)),
                  pl.BlockSpec((1, tc), lambda bi, c: (0, c))],
        out_specs=pl.BlockSpec((None, s, tc), lambda bi, c: (bi, 0, c)),
        out_shape=jax.ShapeDtypeStruct((b, s, w), BF16),
        compiler_params=_params("parallel", "parallel"), name="ml_prep",
    )(u, conv_w, conv_b.reshape(1, w), col_scale.reshape(1, w))


def _log_sigmoid(x):
    return jnp.minimum(x, 0.0) - jnp.log(1.0 + jnp.exp(-jnp.abs(x)))


def _dot_split(a, b, a_is_f32):
    x = a if a_is_f32 else b
    hi = x.astype(BF16)
    lo = (x - hi.astype(F32)).astype(BF16)
    if a_is_f32:
        return (jnp.dot(hi, b, preferred_element_type=F32) + jnp.dot(lo, b, preferred_element_type=F32))
    return (jnp.dot(a, hi, preferred_element_type=F32) + jnp.dot(a, lo, preferred_element_type=F32))


def _mlstm_chain(q, k, v, bc, br, li_r, li_c, total, mask, c_ref, m_ref, idx):
    dv = v.shape[1] - 128
    c_st = c_ref[idx]
    m_st = m_ref[idx:idx + 1, 0:1]
    dmat = jnp.where(mask, bc - br + li_r, NEG_BIG)
    inter = bc + m_st
    m_t = jnp.maximum(inter, jnp.max(dmat, axis=-1, keepdims=True))
    w_intra = jnp.exp(dmat - m_t)
    w_inter = jnp.exp(inter - m_t)
    sc = lax.dot_general(q, k, (((1,), (1,)), ((), ())), preferred_element_type=F32) * w_intra
    both = (w_inter * jnp.dot(q, c_st.astype(BF16), preferred_element_type=F32)
            + jnp.dot(sc.astype(BF16), v, preferred_element_type=F32))
    den = both[:, dv:dv + 1]
    h = both[:, :dv] / jnp.maximum(jnp.abs(den), jnp.exp(-m_t))
    g_s = total - bc + li_c
    m_next = jnp.maximum(total + m_st, jnp.max(g_s, axis=0, keepdims=True))
    a_prev = jnp.exp(total + m_st - m_next)
    kw = k.astype(F32) * jnp.exp(g_s - m_next)
    c_ref[idx] = a_prev * c_st + lax.dot_general(kw.astype(BF16), v, (((0,), (0,)), ((), ())),
                                                 preferred_element_type=F32)
    m_ref[idx:idx + 1, :] = jnp.broadcast_to(m_next, (1, m_ref.shape[1]))
    return h


def _mlstm_body(qkf_ref, vf_ref, gcf_ref, grf_ref, qkb_ref, vb_ref, gcb_ref, grb_ref,
                hf_ref, hb_ref, c_ref, m_ref, *, heads, dk, dv):
    @pl.when(pl.program_id(1) == 0)
    def _():
        c_ref[...] = jnp.zeros_like(c_ref)
        m_ref[...] = jnp.zeros_like(m_ref)

    lc = qkf_ref.shape[1]
    ones_blk = jnp.where(lax.broadcasted_iota(I32, (lc, 128), 1) == 0, 1.0, 0.0).astype(BF16)
    t_i = lax.broadcasted_iota(I32, (lc, lc), 0)
    s_i = lax.broadcasted_iota(I32, (lc, lc), 1)
    lower = s_i <= t_i
    upper = s_i >= t_i
    ltri = jnp.where(lower, 1.0, 0.0).astype(BF16)
    utri = jnp.where(upper, 1.0, 0.0).astype(BF16)

    for bb in range(qkf_ref.shape[0]):
        for direction, (qk_ref, v_ref, gc_ref, gr_ref, h_ref) in enumerate(
                ((qkf_ref, vf_ref, gcf_ref, grf_ref, hf_ref), (qkb_ref, vb_ref, gcb_ref, grb_ref, hb_ref))):
            fwd = direction == 0
            gc = gc_ref[bb]
            gr = gr_ref[bb]
            lfc, lfr = _log_sigmoid(gc), _log_sigmoid(gr)
            cum_c = _dot_split(ltri if fwd else utri, lfc, a_is_f32=False)
            cum_r = _dot_split(lfr, utri if fwd else ltri, a_is_f32=True)
            for hd in range(heads):
                gi = (0 if fwd else 2) * heads + hd
                gf = (1 if fwd else 3) * heads + hd
                bc, br = cum_c[:, gf:gf + 1], cum_r[gf:gf + 1, :]
                total = br[:, lc - 1:lc] if fwd else br[:, 0:1]
                q = qk_ref[bb, :, hd * dk:(hd + 1) * dk]
                k = qk_ref[bb, :, (heads + hd) * dk:(heads + hd + 1) * dk]
                v = jnp.concatenate([v_ref[bb, :, hd * dv:(hd + 1) * dv], ones_blk], axis=1)
                h = _mlstm_chain(q, k, v, bc, br, gr[gi:gi + 1, :], gc[:, gi:gi + 1], total,
                                 lower if fwd else upper, c_ref, m_ref, (bb * 2 + direction) * heads + hd)
                h_ref[bb, :, hd * dv:(hd + 1) * dv] = h.astype(h_ref.dtype)


def mlstm_scan(qk, u, v_blk, gcol, grow):
    b, s, w = qk.shape
    heads = ML_HEADS
    dk = w // (2 * heads)
    dv = 2 * dk
    lc = ML_CHUNK
    nc = s // lc
    ng = gcol.shape[-1]

    def fw(bi, j):
        return j

    def bw(bi, j):
        return nc - 1 - j

    nbs = ML_BATCH_PER_STEP if b % ML_BATCH_PER_STEP == 0 else 1

    def specs(pos):
        return [pl.BlockSpec((nbs, lc, w), lambda bi, j: (bi, pos(bi, j), 0)),
                pl.BlockSpec((nbs, lc, heads * dv), lambda bi, j: (bi, pos(bi, j), v_blk)),
                pl.BlockSpec((nbs, lc, ng), lambda bi, j: (bi, pos(bi, j), 0)),
                pl.BlockSpec((nbs, ng, lc), lambda bi, j: (bi, 0, pos(bi, j)))]

    hshape = jax.ShapeDtypeStruct((b, s, heads * dv), BF16)
    return pl.pallas_call(
        functools.partial(_mlstm_body, heads=heads, dk=dk, dv=dv), grid=(b // nbs, nc),
        in_specs=specs(fw) + specs(bw),
        out_specs=[pl.BlockSpec((nbs, lc, heads * dv), lambda bi, j: (bi, j, 0)),
                   pl.BlockSpec((nbs, lc, heads * dv), lambda bi, j: (bi, nc - 1 - j, 0))],
        out_shape=[hshape, hshape],
        scratch_shapes=[pltpu.VMEM((nbs * 2 * heads, dk, dv + 128), F32),
                        pltpu.VMEM((nbs * 2 * heads, 128), F32)],
        compiler_params=_params("parallel", "arbitrary"), name="mlstm_scan",
    )(qk, u, gcol, grow, qk, u, gcol, grow)


def _mlstm_out_body(res_ref, hf_ref, hb_ref, o_ref, g_ref, w_ref, out_ref, *, heads):
    hs = hf_ref[...].astype(F32) + hb_ref[...].astype(F32)
    dv = hs.shape[1] // heads
    g = g_ref[...]
    parts = []
    for hd in range(heads):
        seg = hs[:, hd * dv:(hd + 1) * dv]
        ms = jnp.mean(seg * seg, axis=-1, keepdims=True)
        parts.append(seg * lax.rsqrt(ms + EPS) * g[:, hd * dv:(hd + 1) * dv])
    a = jnp.concatenate(parts, axis=-1) * _sigmoid(o_ref[...].astype(F32))
    out_ref[...] = res_ref[...] + jnp.dot(a.astype(BF16), w_ref[...], preferred_element_type=F32)


def mlstm_out(res, hf, hb, u2d, o_blk, gain, w_out, tm=512):
    t, d = res.shape
    row = lambda i: (i, 0)
    return pl.pallas_call(
        functools.partial(_mlstm_out_body, heads=ML_HEADS), grid=(t // tm,),
        in_specs=[pl.BlockSpec((tm, d), row), pl.BlockSpec((tm, d), row), pl.BlockSpec((tm, d), row),
                  pl.BlockSpec((tm, d), lambda i: (i, o_blk)), pl.BlockSpec((1, d), lambda i: (0, 0)),
                  pl.BlockSpec((d, d), lambda i: (0, 0))],
        out_specs=pl.BlockSpec((tm, d), row), out_shape=jax.ShapeDtypeStruct((t, d), F32),
        compiler_params=_params("parallel"), name="mlstm_out",
    )(res, hf, hb, u2d, gain.reshape(1, d), w_out)


ROUTE_ROWS = 128
EXPERT_ROW0 = 8


def _router_body(x_ref, g_ref, wt_ref, b_ref, o_ref, cnt_ref, run_ref):
    @pl.when(pl.program_id(0) == 0)
    def _():
        run_ref[...] = jnp.zeros_like(run_ref)

    x = x_ref[...]
    ms = jnp.mean(x * x, axis=-1, keepdims=True)
    xn = x * lax.rsqrt(ms + EPS) * g_ref[...]
    logit = lax.dot_general(wt_ref[...], xn, (((1,), (1,)), ((), ())), preferred_element_type=F32,
                            precision=lax.Precision.HIGHEST) + b_ref[...]
    rows = [logit[r:r + 1, :] for r in range(EXPERT_ROW0 + N_EXPERTS)]
    g_best, g_idx = rows[0], jnp.zeros_like(rows[0])
    for gi in range(1, N_GROUPS):
        better = rows[gi] > g_best
        g_best = jnp.where(better, rows[gi], g_best)
        g_idx = jnp.where(better, float(gi), g_idx)
    g_den = sum(jnp.exp(rows[gi] - g_best) for gi in range(N_GROUPS))
    g_w = 1.0 / g_den
    sel = []
    for e in range(EXPERTS_PER_GROUP):
        v = rows[EXPERT_ROW0 + e]
        for gi in range(1, N_GROUPS):
            v = jnp.where(g_idx == float(gi), rows[EXPERT_ROW0 + gi * EXPERTS_PER_GROUP + e], v)
        sel.append(v)
    v1, i1 = sel[0], jnp.zeros_like(sel[0])
    for e in range(1, EXPERTS_PER_GROUP):
        better = sel[e] > v1
        v1 = jnp.where(better, sel[e], v1)
        i1 = jnp.where(better, float(e), i1)
    v2, i2 = jnp.full_like(v1, -jnp.inf), jnp.zeros_like(v1)
    for e in range(EXPERTS_PER_GROUP):
        better = (sel[e] > v2) & (i1 != float(e))
        v2 = jnp.where(better, sel[e], v2)
        i2 = jnp.where(better, float(e), i2)
    e21 = jnp.exp(v2 - v1)
    gate1 = g_w / (1.0 + e21)
    gate2 = gate1 * e21
    base = g_idx * float(EXPERTS_PER_GROUP)
    e1, e2 = base + i1, base + i2
    tm = e1.shape[1]
    erow = lax.broadcasted_iota(I32, (N_EXPERTS, tm), 0).astype(F32)
    oh1 = jnp.where(erow == e1, 1.0, 0.0)
    oh2 = jnp.where(erow == e2, 1.0, 0.0)
    cnt = oh1 + oh2
    earlier = jnp.where(lax.broadcasted_iota(I32, (tm, tm), 0) < lax.broadcasted_iota(I32, (tm, tm), 1),
                        1.0, 0.0).astype(BF16)
    run = run_ref[...]
    pos = run[:, 0:1] + jnp.dot(cnt.astype(BF16), earlier, preferred_element_type=F32)
    rank1 = jnp.sum(oh1 * pos, axis=0, keepdims=True)
    rank2 = jnp.sum(oh2 * pos, axis=0, keepdims=True)
    run = run + jnp.sum(cnt, axis=1, keepdims=True)
    run_ref[...] = run
    cnt_ref[...] = run
    zero = jnp.zeros_like(v1)
    o_ref[...] = jnp.concatenate([e1, e2, gate1, gate2, rank1, rank2, zero, zero], axis=0)


def moe_router(x2d, g, wg, bg, we, be, tm=512):
    t, d = x2d.shape
    wt = jnp.zeros((ROUTE_ROWS, d), F32).at[:N_GROUPS].set(wg.T).at[EXPERT_ROW0:EXPERT_ROW0 + N_EXPERTS].set(we.T)
    bias = jnp.zeros((ROUTE_ROWS, 1), F32).at[:N_GROUPS, 0].set(bg).at[EXPERT_ROW0:EXPERT_ROW0 + N_EXPERTS, 0].set(be)
    return pl.pallas_call(
        _router_body, grid=(t // tm,),
        in_specs=[pl.BlockSpec((tm, d), lambda i: (i, 0)), pl.BlockSpec((1, d), lambda i: (0, 0)),
                  pl.BlockSpec((ROUTE_ROWS, d), lambda i: (0, 0)), pl.BlockSpec((ROUTE_ROWS, 1), lambda i: (0, 0))],
        out_specs=[pl.BlockSpec((8, tm), lambda i: (0, i)), pl.BlockSpec((N_EXPERTS, 128), lambda i: (0, 0))],
        out_shape=[jax.ShapeDtypeStruct((8, t), F32), jax.ShapeDtypeStruct((N_EXPERTS, 128), F32)],
        scratch_shapes=[pltpu.VMEM((N_EXPERTS, 128), F32)],
        compiler_params=_params("arbitrary"), name="moe_router",
    )(x2d, g.reshape(1, d), wt, bias)


def _dispatch_body(dest_ref, zblk_ref, x_ref, g_ref, xs_ref, buf_ref, sem_ref, *, td):
    i = pl.program_id(0)
    n = pl.num_programs(0)
    slot = i % 2

    @pl.when(i == 0)
    def _():
        buf_ref[1] = jnp.zeros(buf_ref.shape[1:], buf_ref.dtype)

        def zero_copy(j):
            return pltpu.make_async_copy(buf_ref.at[1], xs_ref.at[pl.ds(zblk_ref[j] * td, td), :], sem_ref.at[1])

        def start(j, c):
            @pl.when(zblk_ref[j] >= 0)
            def _():
                zero_copy(j).start()
            return c

        def wait(j, c):
            @pl.when(zblk_ref[j] >= 0)
            def _():
                zero_copy(j).wait()
            return c
        lax.fori_loop(0, zblk_ref.shape[0], start, 0)
        lax.fori_loop(0, zblk_ref.shape[0], wait, 0)

    def row_copy(r, kk):
        return pltpu.make_async_copy(buf_ref.at[slot, pl.ds(r, 1), :],
                                     xs_ref.at[pl.ds(dest_ref[(i * td + r) * 2 + kk], 1), :],
                                     sem_ref.at[slot])

    def wait_buffer(sl):
        for _ in range(2):
            pltpu.make_async_copy(buf_ref.at[sl], xs_ref.at[pl.ds(0, td), :], sem_ref.at[sl]).wait()

    @pl.when(i >= 2)
    def _():
        wait_buffer(slot)

    x = x_ref[...]
    ms = jnp.mean(x * x, axis=-1, keepdims=True)
    buf_ref[slot] = x * lax.rsqrt(ms + EPS) * g_ref[...]

    for r in range(td):
        row_copy(r, 0).start(priority=0)
        row_copy(r, 1).start(priority=1)

    @pl.when(i == n - 1)
    def _():
        @pl.when(n >= 2)
        def _():
            wait_buffer(1 - slot)
        wait_buffer(slot)


def moe_dispatch(x2d, g, dest, zero_blk, cap, td):
    t, d = x2d.shape
    grid_spec = pltpu.PrefetchScalarGridSpec(
        num_scalar_prefetch=2, grid=(t // td,),
        in_specs=[pl.BlockSpec((td, d), lambda i, dest, zb: (i, 0)),
                  pl.BlockSpec((1, d), lambda i, dest, zb: (0, 0))],
        out_specs=pl.BlockSpec(memory_space=pl.ANY),
        scratch_shapes=[pltpu.VMEM((2, td, d), F32), pltpu.SemaphoreType.DMA((2,))])
    return pl.pallas_call(
        functools.partial(_dispatch_body, td=td), grid_spec=grid_spec,
        out_shape=jax.ShapeDtypeStruct((cap, d), F32),
        compiler_params=_params("arbitrary", disable_bounds_checks=True),
        name="moe_dispatch",
    )(dest, zero_blk, x2d, g.reshape(1, d))


def _expert_body(blk_e_ref, nused_ref, xs_ref, w1_ref, w3_ref, w2_ref, ys_ref, w1b_ref, w3b_ref, w2b_ref):
    i = pl.program_id(0)
    nused = nused_ref[0]
    last = nused - 1
    cur = blk_e_ref[jnp.minimum(i, last)]
    prev = blk_e_ref[jnp.minimum(jnp.maximum(i - 1, 0), last)]

    @pl.when((i == 0) | (cur != prev))
    def _():
        w1b_ref[...] = w1_ref[...].astype(BF16)
        w3b_ref[...] = w3_ref[...].astype(BF16)
        w2b_ref[...] = w2_ref[...].astype(BF16)

    @pl.when(i < nused)
    def _():
        x = xs_ref[...].astype(BF16)
        h1 = jnp.dot(x, w1b_ref[...], preferred_element_type=F32)
        h3 = jnp.dot(x, w3b_ref[...], preferred_element_type=F32)
        hid = (h1 * _sigmoid(h1) * h3).astype(BF16)
        ys_ref[...] = jnp.dot(hid, w2b_ref[...], preferred_element_type=F32)

    @pl.when(i >= nused)
    def _():
        ys_ref[...] = jnp.zeros_like(ys_ref)


def moe_experts(xs, blk_e, nused, w1, w3, w2, layer, tm):
    cap, d = xs.shape
    de = w1.shape[-1]

    def blk(i, be, nu):
        return jnp.minimum(i, nu[0] - 1)

    grid_spec = pltpu.PrefetchScalarGridSpec(
        num_scalar_prefetch=2, grid=(cap // tm,),
        in_specs=[pl.BlockSpec((tm, d), lambda i, be, nu: (blk(i, be, nu), 0)),
                  pl.BlockSpec((None, None, d, de), lambda i, be, nu: (layer, be[blk(i, be, nu)], 0, 0)),
                  pl.BlockSpec((None, None, d, de), lambda i, be, nu: (layer, be[blk(i, be, nu)], 0, 0)),
                  pl.BlockSpec((None, None, de, d), lambda i, be, nu: (layer, be[blk(i, be, nu)], 0, 0))],
        out_specs=pl.BlockSpec((tm, d), lambda i, be, nu: (i, 0)),
        scratch_shapes=[pltpu.VMEM((d, de), BF16), pltpu.VMEM((d, de), BF16), pltpu.VMEM((de, d), BF16)])
    return pl.pallas_call(
        _expert_body, grid_spec=grid_spec, out_shape=jax.ShapeDtypeStruct((cap, d), F32),
        compiler_params=_params("arbitrary"), name="moe_experts",
    )(blk_e, nused, xs, w1, w3, w2)


def _combine_body(dest_ref, x_ref, gate_ref, ys_ref, o_ref, buf_ref, sem_ref, *, tc):
    i = pl.program_id(0)
    n = pl.num_programs(0)
    slot = i % 2

    def row_copy(step, sl, r, kk):
        return pltpu.make_async_copy(ys_ref.at[pl.ds(dest_ref[(step * tc + r) * 2 + kk], 1), :],
                                     buf_ref.at[sl, kk, pl.ds(r, 1), :], sem_ref.at[sl])

    def issue_step(step, sl):
        for r in range(tc):
            row_copy(step, sl, r, 0).start(priority=0)
            row_copy(step, sl, r, 1).start(priority=1)

    @pl.when(i == 0)
    def _():
        issue_step(0, 0)

    @pl.when(i + 1 < n)
    def _():
        issue_step(i + 1, 1 - slot)

    for kk in range(2):
        pltpu.make_async_copy(ys_ref.at[pl.ds(0, tc), :], buf_ref.at[slot, kk], sem_ref.at[slot]).wait()

    gate = gate_ref[...]
    o_ref[...] = x_ref[...] + gate[:, 0:1] * buf_ref[slot, 0] + gate[:, 1:2] * buf_ref[slot, 1]


def moe_combine(x2d, gates, ys, dest, tc=256):
    t, d = x2d.shape
    grid_spec = pltpu.PrefetchScalarGridSpec(
        num_scalar_prefetch=1, grid=(t // tc,),
        in_specs=[pl.BlockSpec((tc, d), lambda i, dest: (i, 0)), pl.BlockSpec((tc, 2), lambda i, dest: (i, 0)),
                  pl.BlockSpec(memory_space=pl.ANY)],
        out_specs=pl.BlockSpec((tc, d), lambda i, dest: (i, 0)),
        scratch_shapes=[pltpu.VMEM((2, 2, tc, d), F32), pltpu.SemaphoreType.DMA((2,))])
    return pl.pallas_call(
        functools.partial(_combine_body, tc=tc), grid_spec=grid_spec,
        out_shape=jax.ShapeDtypeStruct((t, d), F32),
        compiler_params=_params("arbitrary", disable_bounds_checks=True), name="moe_combine",
    )(dest, x2d, gates, ys)


def hier_moe_residual(x2d, g, wg, bg, we, be, w1, w3, w2, layer, tm=512):
    t, d = x2d.shape
    route, counts_b = moe_router(x2d, g, wg, bg, we, be)
    eid = route[0:2].T.astype(I32).reshape(-1)
    rank = route[4:6].T.astype(I32).reshape(-1)
    gates = route[2:4].T
    counts = counts_b[:, 0].astype(I32)
    padded = (counts + tm - 1) // tm * tm
    pad_end = jnp.cumsum(padded)
    pad_start = pad_end - padded
    experts = jnp.arange(N_EXPERTS, dtype=I32)
    dest = rank + jnp.sum(jnp.where(eid[:, None] == experts[None, :], pad_start[None, :], 0), axis=1)
    cap = t * 2 + N_EXPERTS * tm
    nblk = cap // tm
    blk_row0 = jnp.arange(nblk, dtype=I32) * tm
    blk_e = jnp.minimum(jnp.sum((pad_end[None, :] <= blk_row0[:, None]).astype(I32), axis=1), N_EXPERTS - 1)
    nused = (pad_end[-1:] // tm).astype(I32)
    last_blk = jnp.where(padded > 0, pad_end // tm - 1, -1)
    tail_blk = jnp.where(nused[0] + experts < nblk, nused[0] + experts, -1)
    zero_blk = jnp.concatenate([last_blk, tail_blk]).astype(I32)
    xs = moe_dispatch(x2d, g, dest.astype(I32), zero_blk, cap, tm)
    ys = moe_experts(xs, blk_e, nused, w1, w3, w2, layer, tm)
    return moe_combine(x2d, gates, ys, dest.astype(I32))


def _even_layer(x2d, b, s, layer, mix_g, w_in, q_norm, k_norm, lam_q1, lam_k1, lam_q2, lam_k2, subln,
                hy_conv_w, hy_conv_b, f_w1, f_b1, f_freq, f_w2, f_b2, f_w3, hy_skip, w_out):
    d = x2d.shape[1]
    d_att = d // 2
    (u2d,) = norm_matmul(x2d, mix_g, [w_in.astype(BF16)], [BF16])
    u = u2d.reshape(b, s, -1)
    lambda_init = 0.8 - 0.6 * math.exp(-0.3 * layer)
    lam = jnp.exp(jnp.sum(lam_q1 * lam_k1)) - jnp.exp(jnp.sum(lam_q2 * lam_k2)) + lambda_init
    dk = q_norm.shape[0]
    qt = head_prep(u, 0, q_norm, scale=dk ** -0.5 * math.log2(math.e), transpose=True)
    kp = head_prep(u, ATT_HEADS, k_norm)
    vt = head_prep(u, 2 * ATT_HEADS, transpose=True)
    y_att = diff_attention(qt, kp, vt, lam, subln, lambda_init)
    z, x1c, zp = hy_prep(u, 3 * d_att, hy_conv_w, hy_conv_b)
    tables = fft_tables(s)
    kf = filter_spectrum(hyena_filter(s, f_w1, f_b1, f_freq, f_w2, f_b2, f_w3), tables[0], tables[1])
    y_hy = hy_fft_conv(zp, z, x1c, kf, tables, hy_skip)
    w_out = w_out.astype(BF16)
    return matmul_residual(x2d, [y_att.reshape(b * s, -1), y_hy.reshape(b * s, -1)],
                           [w_out[:d_att], w_out[d_att:]])


def _odd_layer(x2d, b, s, mix_g, w_in, conv_w, conv_b, gate_b, out_norm, w_out):
    d = x2d.shape[1]
    qk_w = conv_w.shape[1]
    main_w = qk_w + 2 * d
    ng = 4 * ML_HEADS
    w_gate = jnp.zeros((d, 128), F32).at[:, :ng].set(w_in[:, main_w:]).astype(BF16)
    u2d, ug = norm_matmul(x2d, mix_g, [w_in[:, :main_w].astype(BF16), w_gate], [BF16, F32])
    u = u2d.reshape(b, s, main_w)
    gcol = (ug[:, :ng] + gate_b).reshape(b, s, ng)
    grow = jnp.swapaxes(gcol, 1, 2)
    dk = qk_w // (2 * ML_HEADS)
    col_scale = jnp.concatenate([jnp.full((qk_w // 2,), dk ** -0.5, F32), jnp.ones((qk_w // 2,), F32)])
    qk = ml_prep(u, conv_w, conv_b, col_scale)
    hf, hb = mlstm_scan(qk, u, qk_w // d, gcol, grow)
    return mlstm_out(x2d, hf.reshape(b * s, d), hb.reshape(b * s, d), u2d, (qk_w + d) // d, out_norm,
                     w_out.astype(BF16))


def kernel(x, mix_norm, ffn_norm, ev_w_in, ev_q_norm, ev_k_norm, ev_lam_q1, ev_lam_k1, ev_lam_q2, ev_lam_k2, ev_subln, ev_hy_conv_w, ev_hy_conv_b, ev_hy_f_w1, ev_hy_f_b1, ev_hy_f_freq, ev_hy_f_w2, ev_hy_f_b2, ev_hy_f_w3, ev_hy_skip, ev_w_out, od_w_in, od_conv_w, od_conv_b, od_gate_b, od_out_norm, od_w_out, moe_wg, moe_bg, moe_we, moe_be, moe_w1, moe_w3, moe_w2):
    b, s, d = x.shape
    depth = mix_norm.shape[0]
    x2d = x.reshape(b * s, d)
    for layer in range(depth):
        j = layer // 2
        if layer % 2 == 0:
            x2d = _even_layer(x2d, b, s, layer, mix_norm[layer], ev_w_in[j], ev_q_norm[j], ev_k_norm[j],
                              ev_lam_q1[j], ev_lam_k1[j], ev_lam_q2[j], ev_lam_k2[j], ev_subln[j],
                              ev_hy_conv_w[j], ev_hy_conv_b[j], ev_hy_f_w1[j], ev_hy_f_b1[j], ev_hy_f_freq[j],
                              ev_hy_f_w2[j], ev_hy_f_b2[j], ev_hy_f_w3[j], ev_hy_skip[j], ev_w_out[j])
        else:
            x2d = _odd_layer(x2d, b, s, mix_norm[layer], od_w_in[j], od_conv_w[j], od_conv_b[j], od_gate_b[j],
                             od_out_norm[j], od_w_out[j])
        x2d = hier_moe_residual(x2d, ffn_norm[layer], moe_wg[layer], moe_bg[layer], moe_we[layer],
                                moe_be[layer], moe_w1, moe_w3, moe_w2, layer)
    return x2d.reshape(b, s, d)
```

```python
import functools
import math

import jax
import jax.numpy as jnp
from jax import lax
from jax.experimental import pallas as pl
from jax.experimental.pallas import tpu as pltpu

F32 = jnp.float32
BF16 = jnp.bfloat16
I32 = jnp.int32

EPS = 1e-6
ROPE_THETA = 500000.0
ATT_HEADS = 4
ML_HEADS = 4
ML_CHUNK = 128
ML_BATCH_PER_STEP = 2
N_GROUPS = 4
EXPERTS_PER_GROUP = 8
N_EXPERTS = N_GROUPS * EXPERTS_PER_GROUP
HY_EMB_BANDS = 16
HY_MIN_DECAY = math.log(1e-2) / 1.5
HY_MAX_DECAY = math.log(1e-2) / 0.3

V7X_VMEM_BYTES = 64 * 1024 * 1024
VMEM_LIMIT = V7X_VMEM_BYTES - 8 * 1024 * 1024
NEG_BIG = -1e30


def _params(*sem, **kw):
    return pltpu.CompilerParams(dimension_semantics=sem, vmem_limit_bytes=VMEM_LIMIT, **kw)


def _sigmoid(x):
    return 1.0 / (1.0 + jnp.exp(-x))


def _norm_matmul_body(x_ref, g_ref, *refs, n_out, col_chunk):
    w_refs, o_refs = refs[:n_out], refs[n_out:]
    x = x_ref[...]
    ms = jnp.mean(x * x, axis=-1, keepdims=True)
    hn = (x * lax.rsqrt(ms + EPS) * g_ref[...]).astype(BF16)
    for w_ref, o_ref in zip(w_refs, o_refs):
        n = w_ref.shape[1]
        for c in range(0, n, col_chunk):
            ce = min(n, c + col_chunk)
            o_ref[:, c:ce] = jnp.dot(hn, w_ref[:, c:ce], preferred_element_type=F32).astype(o_ref.dtype)


def norm_matmul(x2d, g, ws, out_dtypes, tm=1024):
    t, d = x2d.shape
    in_specs = [pl.BlockSpec((tm, d), lambda i: (i, 0)), pl.BlockSpec((1, d), lambda i: (0, 0))]
    in_specs += [pl.BlockSpec(w.shape, lambda i: (0, 0)) for w in ws]
    out_specs = [pl.BlockSpec((tm, w.shape[1]), lambda i: (i, 0)) for w in ws]
    out_shape = [jax.ShapeDtypeStruct((t, w.shape[1]), dt) for w, dt in zip(ws, out_dtypes)]
    return pl.pallas_call(
        functools.partial(_norm_matmul_body, n_out=len(ws), col_chunk=1024),
        grid=(t // tm,), in_specs=in_specs, out_specs=out_specs, out_shape=out_shape,
        compiler_params=_params("parallel"), name="norm_matmul",
    )(x2d, g.reshape(1, d), *ws)


def _matmul_res_body(res_ref, *refs, n_in):
    a_refs, w_refs, o_ref = refs[:n_in], refs[n_in:2 * n_in], refs[2 * n_in]
    acc = res_ref[...]
    for a_ref, w_ref in zip(a_refs, w_refs):
        acc = acc + jnp.dot(a_ref[...], w_ref[...], preferred_element_type=F32)
    o_ref[...] = acc


def matmul_residual(res, a_list, w_list, tm=1024):
    t, d = res.shape
    in_specs = [pl.BlockSpec((tm, d), lambda i: (i, 0))]
    in_specs += [pl.BlockSpec((tm, a.shape[1]), lambda i: (i, 0)) for a in a_list]
    in_specs += [pl.BlockSpec(w.shape, lambda i: (0, 0)) for w in w_list]
    return pl.pallas_call(
        functools.partial(_matmul_res_body, n_in=len(a_list)),
        grid=(t // tm,), in_specs=in_specs, out_specs=pl.BlockSpec((tm, d), lambda i: (i, 0)),
        out_shape=jax.ShapeDtypeStruct((t, d), F32),
        compiler_params=_params("parallel"), name="matmul_residual",
    )(res, *a_list, *w_list)


def _head_prep_body(u_ref, *refs, dk, rope, transpose):
    o_ref = refs[-1]
    x = u_ref[...].astype(F32)
    if rope:
        g_ref, c_ref, s1_ref, s2_ref = refs[:4]
        lane = lax.broadcasted_iota(I32, x.shape, 1)
        lo = lane < dk
        x2 = x * x
        s_lo = jnp.sum(jnp.where(lo, x2, 0.0), axis=-1, keepdims=True)
        s_hi = jnp.sum(jnp.where(lo, 0.0, x2), axis=-1, keepdims=True)
        ms = jnp.where(lo, s_lo, s_hi) * (1.0 / dk)
        y = x * lax.rsqrt(ms + EPS) * g_ref[...]
        x = y * c_ref[...] + pltpu.roll(y, 120, 1) * s1_ref[...] + pltpu.roll(y, 8, 1) * s2_ref[...]
    if transpose:
        x = x.T
    o_ref[...] = x.astype(o_ref.dtype)


def _rope_lane_tables(seq, dk, rope_dim, scale):
    half = rope_dim // 2
    inv_freq = 1.0 / (ROPE_THETA ** (jnp.arange(0, rope_dim, 2, dtype=F32) / rope_dim))
    ang = jnp.arange(seq, dtype=F32)[:, None] * inv_freq[None, :]
    cos, sin = jnp.cos(ang), jnp.sin(ang)
    d = jnp.arange(2 * dk) % dk
    fi = d % half
    c_tab = jnp.where(d[None, :] < rope_dim, cos[:, fi], 1.0)
    s1_tab = jnp.where(d[None, :] < half, -sin[:, fi], 0.0)
    s2_tab = jnp.where((d[None, :] >= half) & (d[None, :] < rope_dim), sin[:, fi], 0.0)
    return (jnp.stack([c_tab, s1_tab, s2_tab]) * scale).astype(F32)


def head_prep(u, blk0, norm_gain=None, scale=1.0, transpose=False):
    b, s, _ = u.shape
    h = ATT_HEADS
    rope = norm_gain is not None
    in_specs = [pl.BlockSpec((None, s, 128), lambda c, bi: (bi, 0, blk0 + c))]
    args = [u]
    dk = 64
    if rope:
        dk = norm_gain.shape[0]
        assert 2 * dk == 128 and dk // 4 == 16, "rope roll shifts assume 64-wide components, 16 rotary dims"
        tabs = _rope_lane_tables(s, dk, dk // 4, scale)
        in_specs += [pl.BlockSpec((1, 128), lambda c, bi: (0, 0))] + [pl.BlockSpec((s, 128), lambda c, bi: (0, 0))] * 3
        args += [jnp.tile(norm_gain, 2).reshape(1, 128).astype(F32), tabs[0], tabs[1], tabs[2]]
    if transpose:
        out_spec = pl.BlockSpec((None, 128, s), lambda c, bi: (bi, c, 0))
        out_shape = jax.ShapeDtypeStruct((b, h * 128, s), BF16)
    else:
        out_spec = pl.BlockSpec((None, s, 128), lambda c, bi: (bi, 0, c))
        out_shape = jax.ShapeDtypeStruct((b, s, h * 128), BF16)
    return pl.pallas_call(
        functools.partial(_head_prep_body, dk=dk, rope=rope, transpose=transpose), grid=(h, b),
        in_specs=in_specs, out_specs=out_spec, out_shape=out_shape,
        compiler_params=_params("parallel", "parallel"), name="head_prep",
    )(*args)


def _attn_body(lam_ref, qt_ref, k_ref, vt_ref, g_ref, o_ref, *, tq, dk, post_scale, n_split):
    lam = lam_ref[0, 0]
    th = tq // n_split
    for part in range(n_split):
        qt = qt_ref[:, part * th:(part + 1) * th]
        row = lax.broadcasted_iota(I32, qt.shape, 0)
        zero = jnp.zeros_like(qt)
        qq = jnp.concatenate([jnp.where(row < dk, qt, zero), jnp.where(row < dk, zero, qt)], axis=1)
        st = jnp.dot(k_ref[...], qq, preferred_element_type=F32)
        m = jnp.max(st, axis=0, keepdims=True)
        p = jnp.exp2(st - m)
        r = 1.0 / jnp.sum(p, axis=0, keepdims=True)
        ot = jnp.dot(vt_ref[...], p.astype(BF16), preferred_element_type=F32)
        o = (ot[:, :th] * r[:, :th] - ot[:, th:] * (lam * r[:, th:])).T
        ms = jnp.mean(o * o, axis=-1, keepdims=True)
        o_ref[part * th:(part + 1) * th, :] = (o * lax.rsqrt(ms + EPS) * g_ref[...] * post_scale).astype(o_ref.dtype)


def diff_attention(qt, k, vt, lam, subln, lambda_init, tq=512):
    b, s, _ = k.shape
    h = ATT_HEADS
    return pl.pallas_call(
        functools.partial(_attn_body, tq=tq, dk=64, post_scale=1.0 - lambda_init, n_split=1),
        grid=(b, h, s // tq),
        in_specs=[pl.BlockSpec(memory_space=pltpu.SMEM),
                  pl.BlockSpec((None, 128, tq), lambda bi, hi, i: (bi, hi, i)),
                  pl.BlockSpec((None, s, 128), lambda bi, hi, i: (bi, 0, hi)),
                  pl.BlockSpec((None, 128, s), lambda bi, hi, i: (bi, hi, 0)),
                  pl.BlockSpec((1, 128), lambda bi, hi, i: (0, 0))],
        out_specs=pl.BlockSpec((None, tq, 128), lambda bi, hi, i: (bi, i, hi)),
        out_shape=jax.ShapeDtypeStruct((b, s, h * 128), BF16),
        compiler_params=_params("parallel", "parallel", "parallel"), name="diff_attention",
    )(lam.reshape(1, 1).astype(F32), qt, k, vt, subln.reshape(1, 128).astype(F32))


def _conv3(u_ref, w_ref, b_ref):
    x = u_ref[...].astype(F32)
    s = x.shape[0]
    row = lax.broadcasted_iota(I32, x.shape, 0)
    x_prev = jnp.where(row == 0, 0.0, pltpu.roll(x, 1, 0))
    x_next = jnp.where(row == s - 1, 0.0, pltpu.roll(x, s - 1, 0))
    w = w_ref[...]
    return b_ref[...] + x_prev * w[0:1] + x * w[1:2] + x_next * w[2:3]


FFT_N1 = 64
FFT_UNROLL = 64
FFT_PAD = 8


def _hy_prep_body(x1_ref, x2_ref, v_ref, w1_ref, w2_ref, wv_ref, b1_ref, b2_ref, bv_ref, z_ref, x1c_ref, zp_ref):
    x1c_ref[...] = _conv3(x1_ref, w1_ref, b1_ref).astype(x1c_ref.dtype)
    z = _conv3(v_ref, wv_ref, bv_ref) * _conv3(x2_ref, w2_ref, b2_ref)
    z_ref[...] = z.astype(z_ref.dtype)
    nb = z.shape[0] // FFT_N1
    zp_ref[...] = jnp.zeros_like(zp_ref)
    for n2 in range(nb):
        for ci in range(z.shape[1] // 128):
            zp_ref[ci, pl.ds(n2, FFT_N1, stride=nb + FFT_PAD), :] = (
                z[n2 * FFT_N1:(n2 + 1) * FFT_N1, ci * 128:(ci + 1) * 128])


def hy_prep(u, col0, conv_w, conv_b, tc=256):
    b, s, _ = u.shape
    d_hy = conv_w.shape[1] // 3
    nct = d_hy // tc
    blk0 = col0 // tc
    sp = FFT_N1 * (s // FFT_N1 + FFT_PAD)

    def uspec(part):
        return pl.BlockSpec((None, s, tc), lambda bi, c: (bi, 0, blk0 + part * nct + c))

    def wspec(part, rows):
        return pl.BlockSpec((rows, tc), lambda bi, c: (0, part * nct + c))

    ospec = pl.BlockSpec((None, s, tc), lambda bi, c: (bi, 0, c))
    return pl.pallas_call(
        _hy_prep_body, grid=(b, nct),
        in_specs=[uspec(0), uspec(1), uspec(2), wspec(0, 3), wspec(1, 3), wspec(2, 3),
                  wspec(0, 1), wspec(1, 1), wspec(2, 1)],
        out_specs=[ospec, ospec, pl.BlockSpec((None, tc // 128, sp, 128), lambda bi, c: (bi, c, 0, 0))],
        out_shape=[jax.ShapeDtypeStruct((b, s, d_hy), BF16), jax.ShapeDtypeStruct((b, s, d_hy), BF16),
                   jax.ShapeDtypeStruct((b, d_hy // 128, sp, 128), F32)],
        compiler_params=_params("parallel", "parallel"), name="hy_prep",
    )(u, u, u, conv_w, conv_w, conv_w, conv_b.reshape(1, -1), conv_b.reshape(1, -1), conv_b.reshape(1, -1))


def hyena_filter(length, w1, b1, freq, w2, b2, w3):
    d_hy = w3.shape[1] // 2
    t = jnp.linspace(0.0, 1.0, length, dtype=F32)[:, None]
    bands = jnp.linspace(1e-4, HY_EMB_BANDS - 1, HY_EMB_BANDS, dtype=F32)[None, :]
    ang = (2.0 * math.pi / length) * jnp.arange(length, dtype=F32)[:, None] * bands
    z = jnp.concatenate([t, jnp.cos(ang), -jnp.sin(ang)], axis=-1)
    hp = lax.Precision.HIGHEST
    hdn = jnp.sin(freq * (jnp.dot(z, w1, precision=hp) + b1))
    hdn = jnp.sin(freq * (jnp.dot(hdn, w2, precision=hp) + b2))
    filt = jnp.dot(hdn, w3, precision=hp)
    deltas = jnp.abs(jnp.linspace(HY_MIN_DECAY, HY_MAX_DECAY, d_hy, dtype=F32))
    decay = jnp.exp(-t * deltas[None, :])
    h_fwd = filt[:, :d_hy] * decay
    h_bwd = filt[:, d_hy:] * decay
    h_fwd = h_fwd.at[0].add(h_bwd[0])
    h_bwd = h_bwd.at[0].set(0.0)
    norm = jnp.sum(jnp.abs(h_fwd), axis=0, keepdims=True) + jnp.sum(jnp.abs(h_bwd), axis=0, keepdims=True) + EPS
    return jnp.concatenate([h_fwd / norm, h_bwd / norm], axis=1)


def fft_tables(length):
    n = 2 * length
    n1c, n2c, nb = FFT_N1, n // FFT_N1, length // FFT_N1
    unit = 2.0 * math.pi / n
    i1 = jnp.arange(n1c, dtype=I32)
    i2 = jnp.arange(n2c, dtype=I32)
    ib = jnp.arange(nb, dtype=I32)
    samp = i1[:, None, None] + n1c * ib[None, None, :]
    ang = ((i2[None, :, None] * samp) % n).astype(F32) * unit
    m1 = jnp.concatenate([jnp.cos(ang), -jnp.sin(ang)], axis=1)
    ang = ((i1[:, None] * i1[None, :]) % n1c).astype(F32) * (2.0 * math.pi / n1c)
    c, s = jnp.cos(ang), jnp.sin(ang)
    f1 = jnp.concatenate([jnp.concatenate([c, s], axis=1), jnp.concatenate([-s, c], axis=1)], axis=0)
    freq = n2c * i1[None, None, :] + i2[:, None, None]
    ang = ((i1[None, :, None] * freq) % n).astype(F32) * unit
    c, s = jnp.cos(ang), jnp.sin(ang)
    g1 = jnp.concatenate([jnp.concatenate([c, -s], axis=2), jnp.concatenate([s, c], axis=2)], axis=1)
    ang = ((ib[:, None] * i2[None, :]) % n2c).astype(F32) * (2.0 * math.pi / n2c)
    g2 = jnp.concatenate([jnp.cos(ang), -jnp.sin(ang)], axis=1) * (1.0 / n)
    return m1.astype(BF16), f1.astype(BF16), g1.astype(BF16), g2.astype(BF16)


def _fft_stage1(xp_ref, m1_ref, p_ref):
    nb = m1_ref.shape[2]
    n2c = m1_ref.shape[1] // 2

    def body(n1, c):
        x = xp_ref[pl.ds(pl.multiple_of(n1 * (nb + FFT_PAD), 8), nb), :].astype(BF16)
        a = jnp.dot(m1_ref[n1], x, preferred_element_type=F32)
        p_ref[0, pl.ds(n1, n2c, stride=FFT_N1 + FFT_PAD), :] = a[:n2c]
        p_ref[1, pl.ds(n1, n2c, stride=FFT_N1 + FFT_PAD), :] = a[n2c:]
        return c
    lax.fori_loop(0, FFT_N1, body, 0, unroll=FFT_UNROLL)


def _fft_stage2(p_ref, f1_ref, k2):
    r0 = pl.multiple_of(k2 * (FFT_N1 + FFT_PAD), 8)
    slab = jnp.concatenate([p_ref[0, pl.ds(r0, FFT_N1), :], p_ref[1, pl.ds(r0, FFT_N1), :]], axis=0)
    return jnp.dot(f1_ref[...], slab.astype(BF16), preferred_element_type=F32)


def _spectrum_body(xp_ref, m1_ref, f1_ref, o_ref, p_ref):
    _fft_stage1(xp_ref, m1_ref, p_ref)

    def body(k2, c):
        o_ref[k2] = _fft_stage2(p_ref, f1_ref, k2)
        return c
    lax.fori_loop(0, o_ref.shape[0], body, 0, unroll=FFT_UNROLL)


def filter_spectrum(ab, m1, f1):
    length, c2 = ab.shape
    nb = length // FFT_N1
    n2c = m1.shape[1] // 2
    nch = c2 // 128
    abp = jnp.pad(ab.reshape(nb, FFT_N1, nch, 128).transpose(2, 1, 0, 3), ((0, 0), (0, 0), (0, FFT_PAD), (0, 0)))
    abp = abp.reshape(nch, FFT_N1 * (nb + FFT_PAD), 128)
    spec = pl.pallas_call(
        _spectrum_body, grid=(nch,),
        in_specs=[pl.BlockSpec((None,) + abp.shape[1:], lambda c: (c, 0, 0)),
                  pl.BlockSpec(m1.shape, lambda c: (0, 0, 0)), pl.BlockSpec(f1.shape, lambda c: (0, 0))],
        out_specs=pl.BlockSpec((None, n2c, 2 * FFT_N1, 128), lambda c: (c, 0, 0, 0)),
        out_shape=jax.ShapeDtypeStruct((nch, n2c, 2 * FFT_N1, 128), F32),
        scratch_shapes=[pltpu.VMEM((2, n2c * (FFT_N1 + FFT_PAD), 128), F32)],
        compiler_params=_params("parallel"), name="filter_spectrum",
    )(abp, m1, f1)
    fa, fb = spec[:nch // 2], spec[nch // 2:]
    h = FFT_N1
    return jnp.concatenate([fa[:, :, :h] + fb[:, :, :h], fa[:, :, h:] - fb[:, :, h:]], axis=2).astype(BF16)


def _hy_fft_body(zp_ref, z_ref, x1c_ref, kf_ref, m1_ref, f1_ref, g1_ref, g2_ref, skip_ref, o_ref,
                 p_ref, q_ref, y_ref):
    h = FFT_N1
    n2c = g1_ref.shape[0]
    nb = g2_ref.shape[0]
    _fft_stage1(zp_ref, m1_ref, p_ref)

    def mid(k2, c):
        xf = _fft_stage2(p_ref, f1_ref, k2)
        kf = kf_ref[k2].astype(F32)
        xr, xi, kr, ki = xf[:h], xf[h:], kf[:h], kf[h:]
        y = jnp.concatenate([xr * kr - xi * ki, xr * ki + xi * kr], axis=0).astype(BF16)
        d = jnp.dot(g1_ref[k2], y, preferred_element_type=F32)
        q_ref[0, pl.ds(k2, h, stride=n2c + FFT_PAD), :] = d[:h]
        q_ref[1, pl.ds(k2, h, stride=n2c + FFT_PAD), :] = d[h:]
        return c
    lax.fori_loop(0, n2c, mid, 0, unroll=FFT_UNROLL)

    def last(t1, c):
        r0 = pl.multiple_of(t1 * (n2c + FFT_PAD), 8)
        slab = jnp.concatenate([q_ref[0, pl.ds(r0, n2c), :], q_ref[1, pl.ds(r0, n2c), :]], axis=0)
        y_ref[pl.ds(t1, nb, stride=h), :] = jnp.dot(g2_ref[...], slab.astype(BF16), preferred_element_type=F32)
        return c
    lax.fori_loop(0, h, last, 0, unroll=FFT_UNROLL)

    z = z_ref[...].astype(F32)
    o_ref[...] = ((y_ref[...] + z * skip_ref[...]) * x1c_ref[...].astype(F32)).astype(o_ref.dtype)


def hy_fft_conv(zp, z, x1c, kf, tables, skip):
    m1, f1, g1, g2 = tables
    b, s, c = z.shape
    nch = c // 128

    def const(shape):
        return pl.BlockSpec(shape, lambda ci, bi: (0,) * len(shape))

    nat = pl.BlockSpec((None, s, 128), lambda ci, bi: (bi, 0, ci))
    return pl.pallas_call(
        _hy_fft_body, grid=(nch, b),
        in_specs=[pl.BlockSpec((None, None) + zp.shape[2:], lambda ci, bi: (bi, ci, 0, 0)), nat, nat,
                  pl.BlockSpec((None,) + kf.shape[1:], lambda ci, bi: (ci, 0, 0, 0)),
                  const(m1.shape), const(f1.shape), const(g1.shape), const(g2.shape),
                  pl.BlockSpec((1, 128), lambda ci, bi: (0, ci))],
        out_specs=nat,
        out_shape=jax.ShapeDtypeStruct((b, s, c), BF16),
        scratch_shapes=[pltpu.VMEM((2, g1.shape[0] * (FFT_N1 + FFT_PAD), 128), F32),
                        pltpu.VMEM((2, FFT_N1 * (g1.shape[0] + FFT_PAD), 128), F32),
                        pltpu.VMEM((s, 128), F32)],
        compiler_params=_params("parallel", "parallel"), name="hy_fft_conv",
    )(zp, z, x1c, kf, m1, f1, g1, g2, skip.reshape(1, c).astype(F32))


def _ml_prep_body(u_ref, w_ref, b_ref, sc_ref, o_ref):
    y = _conv3(u_ref, w_ref, b_ref)
    o_ref[...] = (y * _sigmoid(y) * sc_ref[...]).astype(o_ref.dtype)


def ml_prep(u, conv_w, conv_b, col_scale, tc=256):
    b, s, _ = u.shape
    w = conv_w.shape[1]
    return pl.pallas_call(
        _ml_prep_body, grid=(b, w // tc),
        in_specs=[pl.BlockSpec((None, s, tc), lambda bi, c: (bi, 0, c)),
                  pl.BlockSpec((3, tc), lambda bi, c: (0, c)),
                  pl.BlockSpec((1, tc), lambda bi, c: (0, c)),
                  pl.BlockSpec((1, tc), lambda bi, c: (0, c))],
        out_specs=pl.BlockSpec((None, s, tc), lambda bi, c: (bi, 0, c)),
        out_shape=jax.ShapeDtypeStruct((b, s, w), BF16),
        compiler_params=_params("parallel", "parallel"), name="ml_prep",
    )(u, conv_w, conv_b.reshape(1, w), col_scale.reshape(1, w))


def _log_sigmoid(x):
    return jnp.minimum(x, 0.0) - jnp.log(1.0 + jnp.exp(-jnp.abs(x)))


def _dot_split(a, b, a_is_f32):
    x = a if a_is_f32 else b
    hi = x.astype(BF16)
    lo = (x - hi.astype(F32)).astype(BF16)
    if a_is_f32:
        return (jnp.dot(hi, b, preferred_element_type=F32) + jnp.dot(lo, b, preferred_element_type=F32))
    return (jnp.dot(a, hi, preferred_element_type=F32) + jnp.dot(a, lo, preferred_element_type=F32))


def _mlstm_chain(q, k, v, bc, br, li_r, li_c, total, mask, c_ref, m_ref, idx):
    dv = v.shape[1] - 128
    c_st = c_ref[idx]
    m_st = m_ref[idx:idx + 1, 0:1]
    dmat = jnp.where(mask, bc - br + li_r, NEG_BIG)
    inter = bc + m_st
    m_t = jnp.maximum(inter, jnp.max(dmat, axis=-1, keepdims=True))
    w_intra = jnp.exp(dmat - m_t)
    w_inter = jnp.exp(inter - m_t)
    sc = lax.dot_general(q, k, (((1,), (1,)), ((), ())), preferred_element_type=F32) * w_intra
    both = (w_inter * jnp.dot(q, c_st.astype(BF16), preferred_element_type=F32)
            + jnp.dot(sc.astype(BF16), v, preferred_element_type=F32))
    den = both[:, dv:dv + 1]
    h = both[:, :dv] / jnp.maximum(jnp.abs(den), jnp.exp(-m_t))
    g_s = total - bc + li_c
    m_next = jnp.maximum(total + m_st, jnp.max(g_s, axis=0, keepdims=True))
    a_prev = jnp.exp(total + m_st - m_next)
    kw = k.astype(F32) * jnp.exp(g_s - m_next)
    c_ref[idx] = a_prev * c_st + lax.dot_general(kw.astype(BF16), v, (((0,), (0,)), ((), ())),
                                                 preferred_element_type=F32)
    m_ref[idx:idx + 1, :] = jnp.broadcast_to(m_next, (1, m_ref.shape[1]))
    return h


def _mlstm_body(qkf_ref, vf_ref, gcf_ref, grf_ref, qkb_ref, vb_ref, gcb_ref, grb_ref,
                hf_ref, hb_ref, c_ref, m_ref, *, heads, dk, dv):
    @pl.when(pl.program_id(1) == 0)
    def _():
        c_ref[...] = jnp.zeros_like(c_ref)
        m_ref[...] = jnp.zeros_like(m_ref)

    lc = qkf_ref.shape[1]
    ones_blk = jnp.where(lax.broadcasted_iota(I32, (lc, 128), 1) == 0, 1.0, 0.0).astype(BF16)
    t_i = lax.broadcasted_iota(I32, (lc, lc), 0)
    s_i = lax.broadcasted_iota(I32, (lc, lc), 1)
    lower = s_i <= t_i
    upper = s_i >= t_i
    ltri = jnp.where(lower, 1.0, 0.0).astype(BF16)
    utri = jnp.where(upper, 1.0, 0.0).astype(BF16)

    for bb in range(qkf_ref.shape[0]):
        for direction, (qk_ref, v_ref, gc_ref, gr_ref, h_ref) in enumerate(
                ((qkf_ref, vf_ref, gcf_ref, grf_ref, hf_ref), (qkb_ref, vb_ref, gcb_ref, grb_ref, hb_ref))):
            fwd = direction == 0
            gc = gc_ref[bb]
            gr = gr_ref[bb]
            lfc, lfr = _log_sigmoid(gc), _log_sigmoid(gr)
            cum_c = _dot_split(ltri if fwd else utri, lfc, a_is_f32=False)
            cum_r = _dot_split(lfr, utri if fwd else ltri, a_is_f32=True)
            for hd in range(heads):
                gi = (0 if fwd else 2) * heads + hd
                gf = (1 if fwd else 3) * heads + hd
                bc, br = cum_c[:, gf:gf + 1], cum_r[gf:gf + 1, :]
                total = br[:, lc - 1:lc] if fwd else br[:, 0:1]
                q = qk_ref[bb, :, hd * dk:(hd + 1) * dk]
                k = qk_ref[bb, :, (heads + hd) * dk:(heads + hd + 1) * dk]
                v = jnp.concatenate([v_ref[bb, :, hd * dv:(hd + 1) * dv], ones_blk], axis=1)
                h = _mlstm_chain(q, k, v, bc, br, gr[gi:gi + 1, :], gc[:, gi:gi + 1], total,
                                 lower if fwd else upper, c_ref, m_ref, (bb * 2 + direction) * heads + hd)
                h_ref[bb, :, hd * dv:(hd + 1) * dv] = h.astype(h_ref.dtype)


def mlstm_scan(qk, u, v_blk, gcol, grow):
    b, s, w = qk.shape
    heads = ML_HEADS
    dk = w // (2 * heads)
    dv = 2 * dk
    lc = ML_CHUNK
    nc = s // lc
    ng = gcol.shape[-1]

    def fw(bi, j):
        return j

    def bw(bi, j):
        return nc - 1 - j

    nbs = ML_BATCH_PER_STEP if b % ML_BATCH_PER_STEP == 0 else 1

    def specs(pos):
        return [pl.BlockSpec((nbs, lc, w), lambda bi, j: (bi, pos(bi, j), 0)),
                pl.BlockSpec((nbs, lc, heads * dv), lambda bi, j: (bi, pos(bi, j), v_blk)),
                pl.BlockSpec((nbs, lc, ng), lambda bi, j: (bi, pos(bi, j), 0)),
                pl.BlockSpec((nbs, ng, lc), lambda bi, j: (bi, 0, pos(bi, j)))]

    hshape = jax.ShapeDtypeStruct((b, s, heads * dv), BF16)
    return pl.pallas_call(
        functools.partial(_mlstm_body, heads=heads, dk=dk, dv=dv), grid=(b // nbs, nc),
        in_specs=specs(fw) + specs(bw),
        out_specs=[pl.BlockSpec((nbs, lc, heads * dv), lambda bi, j: (bi, j, 0)),
                   pl.BlockSpec((nbs, lc, heads * dv), lambda bi, j: (bi, nc - 1 - j, 0))],
        out_shape=[hshape, hshape],
        scratch_shapes=[pltpu.VMEM((nbs * 2 * heads, dk, dv + 128), F32),
                        pltpu.VMEM((nbs * 2 * heads, 128), F32)],
        compiler_params=_params("parallel", "arbitrary"), name="mlstm_scan",
    )(qk, u, gcol, grow, qk, u, gcol, grow)


def _mlstm_out_body(res_ref, hf_ref, hb_ref, o_ref, g_ref, w_ref, out_ref, *, heads):
    hs = hf_ref[...].astype(F32) + hb_ref[...].astype(F32)
    dv = hs.shape[1] // heads
    g = g_ref[...]
    parts = []
    for hd in range(heads):
        seg = hs[:, hd * dv:(hd + 1) * dv]
        ms = jnp.mean(seg * seg, axis=-1, keepdims=True)
        parts.append(seg * lax.rsqrt(ms + EPS) * g[:, hd * dv:(hd + 1) * dv])
    a = jnp.concatenate(parts, axis=-1) * _sigmoid(o_ref[...].astype(F32))
    out_ref[...] = res_ref[...] + jnp.dot(a.astype(BF16), w_ref[...], preferred_element_type=F32)


def mlstm_out(res, hf, hb, u2d, o_blk, gain, w_out, tm=1024):
    t, d = res.shape
    row = lambda i: (i, 0)
    return pl.pallas_call(
        functools.partial(_mlstm_out_body, heads=ML_HEADS), grid=(t // tm,),
        in_specs=[pl.BlockSpec((tm, d), row), pl.BlockSpec((tm, d), row), pl.BlockSpec((tm, d), row),
                  pl.BlockSpec((tm, d), lambda i: (i, o_blk)), pl.BlockSpec((1, d), lambda i: (0, 0)),
                  pl.BlockSpec((d, d), lambda i: (0, 0))],
        out_specs=pl.BlockSpec((tm, d), row), out_shape=jax.ShapeDtypeStruct((t, d), F32),
        compiler_params=_params("parallel"), name="mlstm_out",
    )(res, hf, hb, u2d, gain.reshape(1, d), w_out)


ROUTE_ROWS = 128
EXPERT_ROW0 = 8


def _router_body(x_ref, g_ref, wt_ref, b_ref, o_ref, cnt_ref, run_ref):
    @pl.when(pl.program_id(0) == 0)
    def _():
        run_ref[...] = jnp.zeros_like(run_ref)

    x = x_ref[...]
    ms = jnp.mean(x * x, axis=-1, keepdims=True)
    xn = x * lax.rsqrt(ms + EPS) * g_ref[...]
    wt = wt_ref[...]
    w_hi, x_hi = wt.astype(BF16), xn.astype(BF16)
    w_lo, x_lo = (wt - w_hi.astype(F32)).astype(BF16), (xn - x_hi.astype(F32)).astype(BF16)
    nt = (((1,), (1,)), ((), ()))
    logit = (lax.dot_general(w_hi, x_hi, nt, preferred_element_type=F32)
             + lax.dot_general(w_lo, x_hi, nt, preferred_element_type=F32)
             + lax.dot_general(w_hi, x_lo, nt, preferred_element_type=F32)) + b_ref[...]
    rows = [logit[r:r + 1, :] for r in range(EXPERT_ROW0 + N_EXPERTS)]
    g_best, g_idx = rows[0], jnp.zeros_like(rows[0])
    for gi in range(1, N_GROUPS):
        better = rows[gi] > g_best
        g_best = jnp.where(better, rows[gi], g_best)
        g_idx = jnp.where(better, float(gi), g_idx)
    g_den = sum(jnp.exp(rows[gi] - g_best) for gi in range(N_GROUPS))
    g_w = 1.0 / g_den
    sel = []
    for e in range(EXPERTS_PER_GROUP):
        v = rows[EXPERT_ROW0 + e]
        for gi in range(1, N_GROUPS):
            v = jnp.where(g_idx == float(gi), rows[EXPERT_ROW0 + gi * EXPERTS_PER_GROUP + e], v)
        sel.append(v)
    v1, i1 = sel[0], jnp.zeros_like(sel[0])
    for e in range(1, EXPERTS_PER_GROUP):
        better = sel[e] > v1
        v1 = jnp.where(better, sel[e], v1)
        i1 = jnp.where(better, float(e), i1)
    v2, i2 = jnp.full_like(v1, -jnp.inf), jnp.zeros_like(v1)
    for e in range(EXPERTS_PER_GROUP):
        better = (sel[e] > v2) & (i1 != float(e))
        v2 = jnp.where(better, sel[e], v2)
        i2 = jnp.where(better, float(e), i2)
    e21 = jnp.exp(v2 - v1)
    gate1 = g_w / (1.0 + e21)
    gate2 = gate1 * e21
    base = g_idx * float(EXPERTS_PER_GROUP)
    e1, e2 = base + i1, base + i2
    tm = e1.shape[1]
    erow = lax.broadcasted_iota(I32, (N_EXPERTS, tm), 0).astype(F32)
    oh1 = jnp.where(erow == e1, 1.0, 0.0)
    oh2 = jnp.where(erow == e2, 1.0, 0.0)
    cnt = oh1 + oh2
    earlier = jnp.where(lax.broadcasted_iota(I32, (tm, tm), 0) < lax.broadcasted_iota(I32, (tm, tm), 1),
                        1.0, 0.0).astype(BF16)
    run = run_ref[...]
    pos = run[:, 0:1] + jnp.dot(cnt.astype(BF16), earlier, preferred_element_type=F32)
    rank1 = jnp.sum(oh1 * pos, axis=0, keepdims=True)
    rank2 = jnp.sum(oh2 * pos, axis=0, keepdims=True)
    run = run + jnp.sum(cnt, axis=1, keepdims=True)
    run_ref[...] = run
    cnt_ref[...] = run
    zero = jnp.zeros_like(v1)
    o_ref[...] = jnp.concatenate([e1, e2, gate1, gate2, rank1, rank2, zero, zero], axis=0)


def moe_router(x2d, g, wg, bg, we, be, tm=512):
    t, d = x2d.shape
    wt = jnp.zeros((ROUTE_ROWS, d), F32).at[:N_GROUPS].set(wg.T).at[EXPERT_ROW0:EXPERT_ROW0 + N_EXPERTS].set(we.T)
    bias = jnp.zeros((ROUTE_ROWS, 1), F32).at[:N_GROUPS, 0].set(bg).at[EXPERT_ROW0:EXPERT_ROW0 + N_EXPERTS, 0].set(be)
    return pl.pallas_call(
        _router_body, grid=(t // tm,),
        in_specs=[pl.BlockSpec((tm, d), lambda i: (i, 0)), pl.BlockSpec((1, d), lambda i: (0, 0)),
                  pl.BlockSpec((ROUTE_ROWS, d), lambda i: (0, 0)), pl.BlockSpec((ROUTE_ROWS, 1), lambda i: (0, 0))],
        out_specs=[pl.BlockSpec((8, tm), lambda i: (0, i)), pl.BlockSpec((N_EXPERTS, 128), lambda i: (0, 0))],
        out_shape=[jax.ShapeDtypeStruct((8, t), F32), jax.ShapeDtypeStruct((N_EXPERTS, 128), F32)],
        scratch_shapes=[pltpu.VMEM((N_EXPERTS, 128), F32)],
        compiler_params=_params("arbitrary"), name="moe_router",
    )(x2d, g.reshape(1, d), wt, bias)


def _dispatch_body(dest_ref, zblk_ref, x_ref, g_ref, xs_ref, buf_ref, sem_ref, *, td):
    i = pl.program_id(0)
    n = pl.num_programs(0)
    slot = i % 2

    @pl.when(i == 0)
    def _():
        buf_ref[1] = jnp.zeros(buf_ref.shape[1:], buf_ref.dtype)

        def zero_copy(j):
            return pltpu.make_async_copy(buf_ref.at[1], xs_ref.at[pl.ds(zblk_ref[j] * td, td), :], sem_ref.at[1])

        def start(j, c):
            @pl.when(zblk_ref[j] >= 0)
            def _():
                zero_copy(j).start()
            return c

        def wait(j, c):
            @pl.when(zblk_ref[j] >= 0)
            def _():
                zero_copy(j).wait()
            return c
        lax.fori_loop(0, zblk_ref.shape[0], start, 0)
        lax.fori_loop(0, zblk_ref.shape[0], wait, 0)

    def row_copy(r, kk):
        return pltpu.make_async_copy(buf_ref.at[slot, pl.ds(r, 1), :],
                                     xs_ref.at[pl.ds(dest_ref[(i * td + r) * 2 + kk], 1), :],
                                     sem_ref.at[slot])

    def wait_buffer(sl):
        for _ in range(2):
            pltpu.make_async_copy(buf_ref.at[sl], xs_ref.at[pl.ds(0, td), :], sem_ref.at[sl]).wait()

    @pl.when(i >= 2)
    def _():
        wait_buffer(slot)

    x = x_ref[...]
    ms = jnp.mean(x * x, axis=-1, keepdims=True)
    buf_ref[slot] = x * lax.rsqrt(ms + EPS) * g_ref[...]

    for r in range(td):
        row_copy(r, 0).start(priority=0)
        row_copy(r, 1).start(priority=1)

    @pl.when(i == n - 1)
    def _():
        @pl.when(n >= 2)
        def _():
            wait_buffer(1 - slot)
        wait_buffer(slot)


def moe_dispatch(x2d, g, dest, zero_blk, cap, td):
    t, d = x2d.shape
    grid_spec = pltpu.PrefetchScalarGridSpec(
        num_scalar_prefetch=2, grid=(t // td,),
        in_specs=[pl.BlockSpec((td, d), lambda i, dest, zb: (i, 0)),
                  pl.BlockSpec((1, d), lambda i, dest, zb: (0, 0))],
        out_specs=pl.BlockSpec(memory_space=pl.ANY),
        scratch_shapes=[pltpu.VMEM((2, td, d), F32), pltpu.SemaphoreType.DMA((2,))])
    return pl.pallas_call(
        functools.partial(_dispatch_body, td=td), grid_spec=grid_spec,
        out_shape=jax.ShapeDtypeStruct((cap, d), F32),
        compiler_params=_params("arbitrary", disable_bounds_checks=True),
        name="moe_dispatch",
    )(dest, zero_blk, x2d, g.reshape(1, d))


def _expert_body(blk_e_ref, nused_ref, xs_ref, w1_ref, w3_ref, w2_ref, ys_ref, w1b_ref, w3b_ref, w2b_ref):
    i = pl.program_id(0)
    nused = nused_ref[0]
    last = nused - 1
    cur = blk_e_ref[jnp.minimum(i, last)]
    prev = blk_e_ref[jnp.minimum(jnp.maximum(i - 1, 0), last)]

    @pl.when((i == 0) | (cur != prev))
    def _():
        w1b_ref[...] = w1_ref[...].astype(BF16)
        w3b_ref[...] = w3_ref[...].astype(BF16)
        w2b_ref[...] = w2_ref[...].astype(BF16)

    @pl.when(i < nused)
    def _():
        x = xs_ref[...].astype(BF16)
        h1 = jnp.dot(x, w1b_ref[...], preferred_element_type=F32)
        h3 = jnp.dot(x, w3b_ref[...], preferred_element_type=F32)
        hid = (h1 * _sigmoid(h1) * h3).astype(BF16)
        ys_ref[...] = jnp.dot(hid, w2b_ref[...], preferred_element_type=F32)

    @pl.when(i >= nused)
    def _():
        ys_ref[...] = jnp.zeros_like(ys_ref)


def moe_experts(xs, blk_e, nused, w1, w3, w2, layer, tm):
    cap, d = xs.shape
    de = w1.shape[-1]

    def blk(i, be, nu):
        return jnp.minimum(i, nu[0] - 1)

    grid_spec = pltpu.PrefetchScalarGridSpec(
        num_scalar_prefetch=2, grid=(cap // tm,),
        in_specs=[pl.BlockSpec((tm, d), lambda i, be, nu: (blk(i, be, nu), 0)),
                  pl.BlockSpec((None, None, d, de), lambda i, be, nu: (layer, be[blk(i, be, nu)], 0, 0)),
                  pl.BlockSpec((None, None, d, de), lambda i, be, nu: (layer, be[blk(i, be, nu)], 0, 0)),
                  pl.BlockSpec((None, None, de, d), lambda i, be, nu: (layer, be[blk(i, be, nu)], 0, 0))],
        out_specs=pl.BlockSpec((tm, d), lambda i, be, nu: (i, 0)),
        scratch_shapes=[pltpu.VMEM((d, de), BF16), pltpu.VMEM((d, de), BF16), pltpu.VMEM((de, d), BF16)])
    return pl.pallas_call(
        _expert_body, grid_spec=grid_spec, out_shape=jax.ShapeDtypeStruct((cap, d), F32),
        compiler_params=_params("arbitrary"), name="moe_experts",
    )(blk_e, nused, xs, w1, w3, w2)


def _combine_body(dest_ref, x_ref, gate_ref, ys_ref, o_ref, buf_ref, sem_ref, *, tc):
    i = pl.program_id(0)
    n = pl.num_programs(0)
    slot = i % 2

    def row_copy(step, sl, r, kk):
        return pltpu.make_async_copy(ys_ref.at[pl.ds(dest_ref[(step * tc + r) * 2 + kk], 1), :],
                                     buf_ref.at[sl, kk, pl.ds(r, 1), :], sem_ref.at[sl])

    def issue_step(step, sl):
        for r in range(tc):
            row_copy(step, sl, r, 0).start(priority=0)
            row_copy(step, sl, r, 1).start(priority=1)

    @pl.when(i == 0)
    def _():
        issue_step(0, 0)

    @pl.when(i + 1 < n)
    def _():
        issue_step(i + 1, 1 - slot)

    for kk in range(2):
        pltpu.make_async_copy(ys_ref.at[pl.ds(0, tc), :], buf_ref.at[slot, kk], sem_ref.at[slot]).wait()

    gate = gate_ref[...]
    o_ref[...] = x_ref[...] + gate[:, 0:1] * buf_ref[slot, 0] + gate[:, 1:2] * buf_ref[slot, 1]


def moe_combine(x2d, gates, ys, dest, tc=256):
    t, d = x2d.shape
    grid_spec = pltpu.PrefetchScalarGridSpec(
        num_scalar_prefetch=1, grid=(t // tc,),
        in_specs=[pl.BlockSpec((tc, d), lambda i, dest: (i, 0)), pl.BlockSpec((tc, 2), lambda i, dest: (i, 0)),
                  pl.BlockSpec(memory_space=pl.ANY)],
        out_specs=pl.BlockSpec((tc, d), lambda i, dest: (i, 0)),
        scratch_shapes=[pltpu.VMEM((2, 2, tc, d), F32), pltpu.SemaphoreType.DMA((2,))])
    return pl.pallas_call(
        functools.partial(_combine_body, tc=tc), grid_spec=grid_spec,
        out_shape=jax.ShapeDtypeStruct((t, d), F32),
        compiler_params=_params("arbitrary", disable_bounds_checks=True), name="moe_combine",
    )(dest, x2d, gates, ys)


def hier_moe_residual(x2d, g, wg, bg, we, be, w1, w3, w2, layer, tm=512):
    t, d = x2d.shape
    route, counts_b = moe_router(x2d, g, wg, bg, we, be)
    eid = route[0:2].T.astype(I32).reshape(-1)
    rank = route[4:6].T.astype(I32).reshape(-1)
    gates = route[2:4].T
    counts = counts_b[:, 0].astype(I32)
    padded = (counts + tm - 1) // tm * tm
    pad_end = jnp.cumsum(padded)
    pad_start = pad_end - padded
    experts = jnp.arange(N_EXPERTS, dtype=I32)
    dest = rank + jnp.sum(jnp.where(eid[:, None] == experts[None, :], pad_start[None, :], 0), axis=1)
    cap = t * 2 + N_EXPERTS * tm
    nblk = cap // tm
    blk_row0 = jnp.arange(nblk, dtype=I32) * tm
    blk_e = jnp.minimum(jnp.sum((pad_end[None, :] <= blk_row0[:, None]).astype(I32), axis=1), N_EXPERTS - 1)
    nused = (pad_end[-1:] // tm).astype(I32)
    last_blk = jnp.where(padded > 0, pad_end // tm - 1, -1)
    tail_blk = jnp.where(nused[0] + experts < nblk, nused[0] + experts, -1)
    zero_blk = jnp.concatenate([last_blk, tail_blk]).astype(I32)
    xs = moe_dispatch(x2d, g, dest.astype(I32), zero_blk, cap, tm)
    ys = moe_experts(xs, blk_e, nused, w1, w3, w2, layer, tm)
    return moe_combine(x2d, gates, ys, dest.astype(I32))


def _even_layer(x2d, b, s, layer, mix_g, w_in, q_norm, k_norm, lam_q1, lam_k1, lam_q2, lam_k2, subln,
                hy_conv_w, hy_conv_b, f_w1, f_b1, f_freq, f_w2, f_b2, f_w3, hy_skip, w_out):
    d = x2d.shape[1]
    d_att = d // 2
    (u2d,) = norm_matmul(x2d, mix_g, [w_in.astype(BF16)], [BF16])
    u = u2d.reshape(b, s, -1)
    lambda_init = 0.8 - 0.6 * math.exp(-0.3 * layer)
    lam = jnp.exp(jnp.sum(lam_q1 * lam_k1)) - jnp.exp(jnp.sum(lam_q2 * lam_k2)) + lambda_init
    dk = q_norm.shape[0]
    qt = head_prep(u, 0, q_norm, scale=dk ** -0.5 * math.log2(math.e), transpose=True)
    kp = head_prep(u, ATT_HEADS, k_norm)
    vt = head_prep(u, 2 * ATT_HEADS, transpose=True)
    y_att = diff_attention(qt, kp, vt, lam, subln, lambda_init)
    z, x1c, zp = hy_prep(u, 3 * d_att, hy_conv_w, hy_conv_b)
    tables = fft_tables(s)
    kf = filter_spectrum(hyena_filter(s, f_w1, f_b1, f_freq, f_w2, f_b2, f_w3), tables[0], tables[1])
    y_hy = hy_fft_conv(zp, z, x1c, kf, tables, hy_skip)
    w_out = w_out.astype(BF16)
    return matmul_residual(x2d, [y_att.reshape(b * s, -1), y_hy.reshape(b * s, -1)],
                           [w_out[:d_att], w_out[d_att:]])


def _odd_layer(x2d, b, s, mix_g, w_in, conv_w, conv_b, gate_b, out_norm, w_out):
    d = x2d.shape[1]
    qk_w = conv_w.shape[1]
    main_w = qk_w + 2 * d
    ng = 4 * ML_HEADS
    w_gate = jnp.zeros((d, 128), F32).at[:, :ng].set(w_in[:, main_w:]).astype(BF16)
    u2d, ug = norm_matmul(x2d, mix_g, [w_in[:, :main_w].astype(BF16), w_gate], [BF16, F32])
    u = u2d.reshape(b, s, main_w)
    gcol = (ug[:, :ng] + gate_b).reshape(b, s, ng)
    grow = jnp.swapaxes(gcol, 1, 2)
    dk = qk_w // (2 * ML_HEADS)
    col_scale = jnp.concatenate([jnp.full((qk_w // 2,), dk ** -0.5, F32), jnp.ones((qk_w // 2,), F32)])
    qk = ml_prep(u, conv_w, conv_b, col_scale)
    hf, hb = mlstm_scan(qk, u, qk_w // d, gcol, grow)
    return mlstm_out(x2d, hf.reshape(b * s, d), hb.reshape(b * s, d), u2d, (qk_w + d) // d, out_norm,
                     w_out.astype(BF16))


def kernel(x, mix_norm, ffn_norm, ev_w_in, ev_q_norm, ev_k_norm, ev_lam_q1, ev_lam_k1, ev_lam_q2, ev_lam_k2, ev_subln, ev_hy_conv_w, ev_hy_conv_b, ev_hy_f_w1, ev_hy_f_b1, ev_hy_f_freq, ev_hy_f_w2, ev_hy_f_b2, ev_hy_f_w3, ev_hy_skip, ev_w_out, od_w_in, od_conv_w, od_conv_b, od_gate_b, od_out_norm, od_w_out, moe_wg, moe_bg, moe_we, moe_be, moe_w1, moe_w3, moe_w2):
    b, s, d = x.shape
    depth = mix_norm.shape[0]
    x2d = x.reshape(b * s, d)
    for layer in range(depth):
        j = layer // 2
        if layer % 2 == 0:
            x2d = _even_layer(x2d, b, s, layer, mix_norm[layer], ev_w_in[j], ev_q_norm[j], ev_k_norm[j],
                              ev_lam_q1[j], ev_lam_k1[j], ev_lam_q2[j], ev_lam_k2[j], ev_subln[j],
                              ev_hy_conv_w[j], ev_hy_conv_b[j], ev_hy_f_w1[j], ev_hy_f_b1[j], ev_hy_f_freq[j],
                              ev_hy_f_w2[j], ev_hy_f_b2[j], ev_hy_f_w3[j], ev_hy_skip[j], ev_w_out[j])
        else:
            x2d = _odd_layer(x2d, b, s, mix_norm[layer], od_w_in[j], od_conv_w[j], od_conv_b[j], od_gate_b[j],
                             od_out_norm[j], od_w_out[j])
        x2d = hier_moe_residual(x2d, ffn_norm[layer], moe_wg[layer], moe_bg[layer], moe_we[layer],
                                moe_be[layer], moe_w1, moe_w3, moe_w2, layer)
    return x2d.reshape(b, s, d)
```

```python
import functools
import math

import jax
import jax.numpy as jnp
from jax import lax
from jax.experimental import pallas as pl
from jax.experimental.pallas import tpu as pltpu

F32 = jnp.float32
BF16 = jnp.bfloat16
I32 = jnp.int32

EPS = 1e-6
ROPE_THETA = 500000.0
ATT_HEADS = 4
ML_HEADS = 4
ML_CHUNK = 128
ML_BATCH_PER_STEP = 2
N_GROUPS = 4
EXPERTS_PER_GROUP = 8
N_EXPERTS = N_GROUPS * EXPERTS_PER_GROUP
HY_EMB_BANDS = 16
HY_MIN_DECAY = math.log(1e-2) / 1.5
HY_MAX_DECAY = math.log(1e-2) / 0.3

V7X_VMEM_BYTES = 64 * 1024 * 1024
VMEM_LIMIT = V7X_VMEM_BYTES - 8 * 1024 * 1024
NEG_BIG = -1e30


def _params(*sem, **kw):
    return pltpu.CompilerParams(dimension_semantics=sem, vmem_limit_bytes=VMEM_LIMIT, **kw)


def _sigmoid(x):
    return 1.0 / (1.0 + jnp.exp(-x))


def _norm_matmul_body(x_ref, g_ref, *refs, n_out, col_chunk):
    w_refs, o_refs = refs[:n_out], refs[n_out:]
    x = x_ref[...]
    ms = jnp.mean(x * x, axis=-1, keepdims=True)
    hn = (x * lax.rsqrt(ms + EPS) * g_ref[...]).astype(BF16)
    for w_ref, o_ref in zip(w_refs, o_refs):
        n = w_ref.shape[1]
        for c in range(0, n, col_chunk):
            ce = min(n, c + col_chunk)
            o_ref[:, c:ce] = jnp.dot(hn, w_ref[:, c:ce], preferred_element_type=F32).astype(o_ref.dtype)


def norm_matmul(x2d, g, ws, out_dtypes, tm=1024):
    t, d = x2d.shape
    in_specs = [pl.BlockSpec((tm, d), lambda i: (i, 0)), pl.BlockSpec((1, d), lambda i: (0, 0))]
    in_specs += [pl.BlockSpec(w.shape, lambda i: (0, 0)) for w in ws]
    out_specs = [pl.BlockSpec((tm, w.shape[1]), lambda i: (i, 0)) for w in ws]
    out_shape = [jax.ShapeDtypeStruct((t, w.shape[1]), dt) for w, dt in zip(ws, out_dtypes)]
    return pl.pallas_call(
        functools.partial(_norm_matmul_body, n_out=len(ws), col_chunk=1024),
        grid=(t // tm,), in_specs=in_specs, out_specs=out_specs, out_shape=out_shape,
        compiler_params=_params("parallel"), name="norm_matmul",
    )(x2d, g.reshape(1, d), *ws)


def _matmul_res_body(res_ref, *refs, n_in):
    a_refs, w_refs, o_ref = refs[:n_in], refs[n_in:2 * n_in], refs[2 * n_in]
    acc = res_ref[...]
    for a_ref, w_ref in zip(a_refs, w_refs):
        acc = acc + jnp.dot(a_ref[...], w_ref[...], preferred_element_type=F32)
    o_ref[...] = acc


def matmul_residual(res, a_list, w_list, tm=1024):
    t, d = res.shape
    in_specs = [pl.BlockSpec((tm, d), lambda i: (i, 0))]
    in_specs += [pl.BlockSpec((tm, a.shape[1]), lambda i: (i, 0)) for a in a_list]
    in_specs += [pl.BlockSpec(w.shape, lambda i: (0, 0)) for w in w_list]
    return pl.pallas_call(
        functools.partial(_matmul_res_body, n_in=len(a_list)),
        grid=(t // tm,), in_specs=in_specs, out_specs=pl.BlockSpec((tm, d), lambda i: (i, 0)),
        out_shape=jax.ShapeDtypeStruct((t, d), F32),
        compiler_params=_params("parallel"), name="matmul_residual",
    )(res, *a_list, *w_list)


def _head_prep_body(u_ref, *refs, dk, rope, transpose):
    o_ref = refs[-1]
    x = u_ref[...].astype(F32)
    if rope:
        g_ref, c_ref, s1_ref, s2_ref = refs[:4]
        lane = lax.broadcasted_iota(I32, x.shape, 1)
        lo = lane < dk
        x2 = x * x
        s_lo = jnp.sum(jnp.where(lo, x2, 0.0), axis=-1, keepdims=True)
        s_hi = jnp.sum(jnp.where(lo, 0.0, x2), axis=-1, keepdims=True)
        ms = jnp.where(lo, s_lo, s_hi) * (1.0 / dk)
        y = x * lax.rsqrt(ms + EPS) * g_ref[...]
        x = y * c_ref[...] + pltpu.roll(y, 120, 1) * s1_ref[...] + pltpu.roll(y, 8, 1) * s2_ref[...]
    if transpose:
        x = x.T
    o_ref[...] = x.astype(o_ref.dtype)


def _rope_lane_tables(seq, dk, rope_dim, scale):
    half = rope_dim // 2
    inv_freq = 1.0 / (ROPE_THETA ** (jnp.arange(0, rope_dim, 2, dtype=F32) / rope_dim))
    ang = jnp.arange(seq, dtype=F32)[:, None] * inv_freq[None, :]
    cos, sin = jnp.cos(ang), jnp.sin(ang)
    d = jnp.arange(2 * dk) % dk
    fi = d % half
    c_tab = jnp.where(d[None, :] < rope_dim, cos[:, fi], 1.0)
    s1_tab = jnp.where(d[None, :] < half, -sin[:, fi], 0.0)
    s2_tab = jnp.where((d[None, :] >= half) & (d[None, :] < rope_dim), sin[:, fi], 0.0)
    return (jnp.stack([c_tab, s1_tab, s2_tab]) * scale).astype(F32)


def head_prep(u, blk0, norm_gain=None, scale=1.0, transpose=False):
    b, s, _ = u.shape
    h = ATT_HEADS
    rope = norm_gain is not None
    in_specs = [pl.BlockSpec((None, s, 128), lambda c, bi: (bi, 0, blk0 + c))]
    args = [u]
    dk = 64
    if rope:
        dk = norm_gain.shape[0]
        assert 2 * dk == 128 and dk // 4 == 16, "rope roll shifts assume 64-wide components, 16 rotary dims"
        tabs = _rope_lane_tables(s, dk, dk // 4, scale)
        in_specs += [pl.BlockSpec((1, 128), lambda c, bi: (0, 0))] + [pl.BlockSpec((s, 128), lambda c, bi: (0, 0))] * 3
        args += [jnp.tile(norm_gain, 2).reshape(1, 128).astype(F32), tabs[0], tabs[1], tabs[2]]
    if transpose:
        out_spec = pl.BlockSpec((None, 128, s), lambda c, bi: (bi, c, 0))
        out_shape = jax.ShapeDtypeStruct((b, h * 128, s), BF16)
    else:
        out_spec = pl.BlockSpec((None, s, 128), lambda c, bi: (bi, 0, c))
        out_shape = jax.ShapeDtypeStruct((b, s, h * 128), BF16)
    return pl.pallas_call(
        functools.partial(_head_prep_body, dk=dk, rope=rope, transpose=transpose), grid=(h, b),
        in_specs=in_specs, out_specs=out_spec, out_shape=out_shape,
        compiler_params=_params("parallel", "parallel"), name="head_prep",
    )(*args)


def _attn_body(lam_ref, qt_ref, k_ref, vt_ref, g_ref, o_ref, *, tq, dk, post_scale, n_split):
    lam = lam_ref[0, 0]
    th = tq // n_split
    for part in range(n_split):
        qt = qt_ref[:, part * th:(part + 1) * th]
        row = lax.broadcasted_iota(I32, qt.shape, 0)
        zero = jnp.zeros_like(qt)
        qq = jnp.concatenate([jnp.where(row < dk, qt, zero), jnp.where(row < dk, zero, qt)], axis=1)
        st = jnp.dot(k_ref[...], qq, preferred_element_type=F32)
        m = jnp.max(st, axis=0, keepdims=True)
        p = jnp.exp2(st - m)
        r = 1.0 / jnp.sum(p, axis=0, keepdims=True)
        ot = jnp.dot(vt_ref[...], p.astype(BF16), preferred_element_type=F32)
        o = (ot[:, :th] * r[:, :th] - ot[:, th:] * (lam * r[:, th:])).T
        ms = jnp.mean(o * o, axis=-1, keepdims=True)
        o_ref[part * th:(part + 1) * th, :] = (o * lax.rsqrt(ms + EPS) * g_ref[...] * post_scale).astype(o_ref.dtype)


def diff_attention(qt, k, vt, lam, subln, lambda_init, tq=512):
    b, s, _ = k.shape
    h = ATT_HEADS
    return pl.pallas_call(
        functools.partial(_attn_body, tq=tq, dk=64, post_scale=1.0 - lambda_init, n_split=1),
        grid=(b, h, s // tq),
        in_specs=[pl.BlockSpec(memory_space=pltpu.SMEM),
                  pl.BlockSpec((None, 128, tq), lambda bi, hi, i: (bi, hi, i)),
                  pl.BlockSpec((None, s, 128), lambda bi, hi, i: (bi, 0, hi)),
                  pl.BlockSpec((None, 128, s), lambda bi, hi, i: (bi, hi, 0)),
                  pl.BlockSpec((1, 128), lambda bi, hi, i: (0, 0))],
        out_specs=pl.BlockSpec((None, tq, 128), lambda bi, hi, i: (bi, i, hi)),
        out_shape=jax.ShapeDtypeStruct((b, s, h * 128), BF16),
        compiler_params=_params("parallel", "parallel", "parallel"), name="diff_attention",
    )(lam.reshape(1, 1).astype(F32), qt, k, vt, subln.reshape(1, 128).astype(F32))


def _conv3(u_ref, w_ref, b_ref):
    x = u_ref[...].astype(F32)
    s = x.shape[0]
    row = lax.broadcasted_iota(I32, x.shape, 0)
    x_prev = jnp.where(row == 0, 0.0, pltpu.roll(x, 1, 0))
    x_next = jnp.where(row == s - 1, 0.0, pltpu.roll(x, s - 1, 0))
    w = w_ref[...]
    return b_ref[...] + x_prev * w[0:1] + x * w[1:2] + x_next * w[2:3]


FFT_N1 = 64
FFT_UNROLL = 64
FFT_PAD = 8


def _hy_prep_body(x1_ref, x2_ref, v_ref, w1_ref, w2_ref, wv_ref, b1_ref, b2_ref, bv_ref, z_ref, x1c_ref, zp_ref):
    x1c_ref[...] = _conv3(x1_ref, w1_ref, b1_ref).astype(x1c_ref.dtype)
    z = _conv3(v_ref, wv_ref, bv_ref) * _conv3(x2_ref, w2_ref, b2_ref)
    z_ref[...] = z.astype(z_ref.dtype)
    nb = z.shape[0] // FFT_N1
    zp_ref[...] = jnp.zeros_like(zp_ref)
    for n2 in range(nb):
        for ci in range(z.shape[1] // 128):
            zp_ref[ci, pl.ds(n2, FFT_N1, stride=nb + FFT_PAD), :] = (
                z[n2 * FFT_N1:(n2 + 1) * FFT_N1, ci * 128:(ci + 1) * 128])


def hy_prep(u, col0, conv_w, conv_b, tc=256):
    b, s, _ = u.shape
    d_hy = conv_w.shape[1] // 3
    nct = d_hy // tc
    blk0 = col0 // tc
    sp = FFT_N1 * (s // FFT_N1 + FFT_PAD)

    def uspec(part):
        return pl.BlockSpec((None, s, tc), lambda bi, c: (bi, 0, blk0 + part * nct + c))

    def wspec(part, rows):
        return pl.BlockSpec((rows, tc), lambda bi, c: (0, part * nct + c))

    ospec = pl.BlockSpec((None, s, tc), lambda bi, c: (bi, 0, c))
    return pl.pallas_call(
        _hy_prep_body, grid=(b, nct),
        in_specs=[uspec(0), uspec(1), uspec(2), wspec(0, 3), wspec(1, 3), wspec(2, 3),
                  wspec(0, 1), wspec(1, 1), wspec(2, 1)],
        out_specs=[ospec, ospec, pl.BlockSpec((None, tc // 128, sp, 128), lambda bi, c: (bi, c, 0, 0))],
        out_shape=[jax.ShapeDtypeStruct((b, s, d_hy), BF16), jax.ShapeDtypeStruct((b, s, d_hy), BF16),
                   jax.ShapeDtypeStruct((b, d_hy // 128, sp, 128), F32)],
        compiler_params=_params("parallel", "parallel"), name="hy_prep",
    )(u, u, u, conv_w, conv_w, conv_w, conv_b.reshape(1, -1), conv_b.reshape(1, -1), conv_b.reshape(1, -1))


def hyena_filter(length, w1, b1, freq, w2, b2, w3):
    d_hy = w3.shape[1] // 2
    t = jnp.linspace(0.0, 1.0, length, dtype=F32)[:, None]
    bands = jnp.linspace(1e-4, HY_EMB_BANDS - 1, HY_EMB_BANDS, dtype=F32)[None, :]
    ang = (2.0 * math.pi / length) * jnp.arange(length, dtype=F32)[:, None] * bands
    z = jnp.concatenate([t, jnp.cos(ang), -jnp.sin(ang)], axis=-1)
    hp = lax.Precision.HIGHEST
    hdn = jnp.sin(freq * (jnp.dot(z, w1, precision=hp) + b1))
    hdn = jnp.sin(freq * (jnp.dot(hdn, w2, precision=hp) + b2))
    filt = jnp.dot(hdn, w3, precision=hp)
    deltas = jnp.abs(jnp.linspace(HY_MIN_DECAY, HY_MAX_DECAY, d_hy, dtype=F32))
    decay = jnp.exp(-t * deltas[None, :])
    h_fwd = filt[:, :d_hy] * decay
    h_bwd = filt[:, d_hy:] * decay
    h_fwd = h_fwd.at[0].add(h_bwd[0])
    h_bwd = h_bwd.at[0].set(0.0)
    norm = jnp.sum(jnp.abs(h_fwd), axis=0, keepdims=True) + jnp.sum(jnp.abs(h_bwd), axis=0, keepdims=True) + EPS
    return jnp.concatenate([h_fwd / norm, h_bwd / norm], axis=1)


def fft_tables(length):
    n = 2 * length
    n1c, n2c, nb = FFT_N1, n // FFT_N1, length // FFT_N1
    unit = 2.0 * math.pi / n
    i1 = jnp.arange(n1c, dtype=I32)
    i2 = jnp.arange(n2c, dtype=I32)
    ib = jnp.arange(nb, dtype=I32)
    samp = i1[:, None, None] + n1c * ib[None, None, :]
    ang = ((i2[None, :, None] * samp) % n).astype(F32) * unit
    m1 = jnp.concatenate([jnp.cos(ang), -jnp.sin(ang)], axis=1)
    ang = ((i1[:, None] * i1[None, :]) % n1c).astype(F32) * (2.0 * math.pi / n1c)
    c, s = jnp.cos(ang), jnp.sin(ang)
    f1 = jnp.concatenate([jnp.concatenate([c, s], axis=1), jnp.concatenate([-s, c], axis=1)], axis=0)
    freq = n2c * i1[None, None, :] + i2[:, None, None]
    ang = ((i1[None, :, None] * freq) % n).astype(F32) * unit
    c, s = jnp.cos(ang), jnp.sin(ang)
    g1 = jnp.concatenate([jnp.concatenate([c, -s], axis=2), jnp.concatenate([s, c], axis=2)], axis=1)
    ang = ((ib[:, None] * i2[None, :]) % n2c).astype(F32) * (2.0 * math.pi / n2c)
    g2 = jnp.concatenate([jnp.cos(ang), -jnp.sin(ang)], axis=1) * (1.0 / n)
    return m1.astype(BF16), f1.astype(BF16), g1.astype(BF16), g2.astype(BF16)


def _fft_stage1(xp_ref, m1_ref, p_ref):
    nb = m1_ref.shape[2]
    n2c = m1_ref.shape[1] // 2

    def body(n1, c):
        x = xp_ref[pl.ds(pl.multiple_of(n1 * (nb + FFT_PAD), 8), nb), :].astype(BF16)
        a = jnp.dot(m1_ref[n1], x, preferred_element_type=F32)
        p_ref[0, pl.ds(n1, n2c, stride=FFT_N1 + FFT_PAD), :] = a[:n2c]
        p_ref[1, pl.ds(n1, n2c, stride=FFT_N1 + FFT_PAD), :] = a[n2c:]
        return c
    lax.fori_loop(0, FFT_N1, body, 0, unroll=FFT_UNROLL)


def _fft_stage2(p_ref, f1_ref, k2):
    r0 = pl.multiple_of(k2 * (FFT_N1 + FFT_PAD), 8)
    slab = jnp.concatenate([p_ref[0, pl.ds(r0, FFT_N1), :], p_ref[1, pl.ds(r0, FFT_N1), :]], axis=0)
    return jnp.dot(f1_ref[...], slab.astype(BF16), preferred_element_type=F32)


def _spectrum_body(xp_ref, m1_ref, f1_ref, o_ref, p_ref):
    _fft_stage1(xp_ref, m1_ref, p_ref)

    def body(k2, c):
        o_ref[k2] = _fft_stage2(p_ref, f1_ref, k2)
        return c
    lax.fori_loop(0, o_ref.shape[0], body, 0, unroll=FFT_UNROLL)


def filter_spectrum(ab, m1, f1):
    length, c2 = ab.shape
    nb = length // FFT_N1
    n2c = m1.shape[1] // 2
    nch = c2 // 128
    abp = jnp.pad(ab.reshape(nb, FFT_N1, nch, 128).transpose(2, 1, 0, 3), ((0, 0), (0, 0), (0, FFT_PAD), (0, 0)))
    abp = abp.reshape(nch, FFT_N1 * (nb + FFT_PAD), 128)
    spec = pl.pallas_call(
        _spectrum_body, grid=(nch,),
        in_specs=[pl.BlockSpec((None,) + abp.shape[1:], lambda c: (c, 0, 0)),
                  pl.BlockSpec(m1.shape, lambda c: (0, 0, 0)), pl.BlockSpec(f1.shape, lambda c: (0, 0))],
        out_specs=pl.BlockSpec((None, n2c, 2 * FFT_N1, 128), lambda c: (c, 0, 0, 0)),
        out_shape=jax.ShapeDtypeStruct((nch, n2c, 2 * FFT_N1, 128), F32),
        scratch_shapes=[pltpu.VMEM((2, n2c * (FFT_N1 + FFT_PAD), 128), F32)],
        compiler_params=_params("parallel"), name="filter_spectrum",
    )(abp, m1, f1)
    fa, fb = spec[:nch // 2], spec[nch // 2:]
    h = FFT_N1
    return jnp.concatenate([fa[:, :, :h] + fb[:, :, :h], fa[:, :, h:] - fb[:, :, h:]], axis=2).astype(BF16)


def _hy_fft_body(zp_ref, z_ref, x1c_ref, kf_ref, m1_ref, f1_ref, g1_ref, g2_ref, skip_ref, o_ref,
                 p_ref, q_ref, y_ref):
    h = FFT_N1
    n2c = g1_ref.shape[0]
    nb = g2_ref.shape[0]
    _fft_stage1(zp_ref, m1_ref, p_ref)

    def mid(k2, c):
        xf = _fft_stage2(p_ref, f1_ref, k2)
        kf = kf_ref[k2].astype(F32)
        xr, xi, kr, ki = xf[:h], xf[h:], kf[:h], kf[h:]
        y = jnp.concatenate([xr * kr - xi * ki, xr * ki + xi * kr], axis=0).astype(BF16)
        d = jnp.dot(g1_ref[k2], y, preferred_element_type=F32)
        q_ref[0, pl.ds(k2, h, stride=n2c + FFT_PAD), :] = d[:h]
        q_ref[1, pl.ds(k2, h, stride=n2c + FFT_PAD), :] = d[h:]
        return c
    lax.fori_loop(0, n2c, mid, 0, unroll=FFT_UNROLL)

    def last(t1, c):
        r0 = pl.multiple_of(t1 * (n2c + FFT_PAD), 8)
        slab = jnp.concatenate([q_ref[0, pl.ds(r0, n2c), :], q_ref[1, pl.ds(r0, n2c), :]], axis=0)
        y_ref[pl.ds(t1, nb, stride=h), :] = jnp.dot(g2_ref[...], slab.astype(BF16), preferred_element_type=F32)
        return c
    lax.fori_loop(0, h, last, 0, unroll=FFT_UNROLL)

    z = z_ref[...].astype(F32)
    o_ref[...] = ((y_ref[...] + z * skip_ref[...]) * x1c_ref[...].astype(F32)).astype(o_ref.dtype)


def hy_fft_conv(zp, z, x1c, kf, tables, skip):
    m1, f1, g1, g2 = tables
    b, s, c = z.shape
    nch = c // 128

    def const(shape):
        return pl.BlockSpec(shape, lambda ci, bi: (0,) * len(shape))

    nat = pl.BlockSpec((None, s, 128), lambda ci, bi: (bi, 0, ci))
    return pl.pallas_call(
        _hy_fft_body, grid=(nch, b),
        in_specs=[pl.BlockSpec((None, None) + zp.shape[2:], lambda ci, bi: (bi, ci, 0, 0)), nat, nat,
                  pl.BlockSpec((None,) + kf.shape[1:], lambda ci, bi: (ci, 0, 0, 0)),
                  const(m1.shape), const(f1.shape), const(g1.shape), const(g2.shape),
                  pl.BlockSpec((1, 128), lambda ci, bi: (0, ci))],
        out_specs=nat,
        out_shape=jax.ShapeDtypeStruct((b, s, c), BF16),
        scratch_shapes=[pltpu.VMEM((2, g1.shape[0] * (FFT_N1 + FFT_PAD), 128), F32),
                        pltpu.VMEM((2, FFT_N1 * (g1.shape[0] + FFT_PAD), 128), F32),
                        pltpu.VMEM((s, 128), F32)],
        compiler_params=_params("parallel", "parallel"), name="hy_fft_conv",
    )(zp, z, x1c, kf, m1, f1, g1, g2, skip.reshape(1, c).astype(F32))


def _ml_prep_body(u_ref, w_ref, b_ref, o_ref, *, scale, transpose):
    y = _conv3(u_ref, w_ref, b_ref)
    y = y * _sigmoid(y) * scale
    o_ref[...] = (y.T if transpose else y).astype(o_ref.dtype)


def ml_prep(u, conv_w, conv_b, col0, ncols, scale, transpose, tc=256):
    b, s, _ = u.shape
    w = conv_w.shape[1]
    c0 = col0 // tc
    if transpose:
        out_spec = pl.BlockSpec((None, tc, s), lambda bi, c: (bi, c, 0))
        out_shape = jax.ShapeDtypeStruct((b, ncols, s), BF16)
    else:
        out_spec = pl.BlockSpec((None, s, tc), lambda bi, c: (bi, 0, c))
        out_shape = jax.ShapeDtypeStruct((b, s, ncols), BF16)
    return pl.pallas_call(
        functools.partial(_ml_prep_body, scale=scale, transpose=transpose), grid=(b, ncols // tc),
        in_specs=[pl.BlockSpec((None, s, tc), lambda bi, c: (bi, 0, c0 + c)),
                  pl.BlockSpec((3, tc), lambda bi, c: (0, c0 + c)),
                  pl.BlockSpec((1, tc), lambda bi, c: (0, c0 + c))],
        out_specs=out_spec, out_shape=out_shape,
        compiler_params=_params("parallel", "parallel"), name="ml_prep",
    )(u, conv_w, conv_b.reshape(1, w))


def _log_sigmoid(x):
    return jnp.minimum(x, 0.0) - jnp.log(1.0 + jnp.exp(-jnp.abs(x)))


def _dot_split(a, b, a_is_f32):
    x = a if a_is_f32 else b
    hi = x.astype(BF16)
    lo = (x - hi.astype(F32)).astype(BF16)
    if a_is_f32:
        return (jnp.dot(hi, b, preferred_element_type=F32) + jnp.dot(lo, b, preferred_element_type=F32))
    return (jnp.dot(a, hi, preferred_element_type=F32) + jnp.dot(a, lo, preferred_element_type=F32))


def _mlstm_chain(q, kt, v, bc, br, li_r, total, mask, c_ref, m_ref, idx):
    dv = v.shape[1] - 128
    c_st = c_ref[idx]
    m_st = m_ref[idx:idx + 1, 0:1]
    key_row = li_r - br
    inter = bc + m_st
    m_t = jnp.maximum(inter, bc + jnp.max(jnp.where(mask, key_row, NEG_BIG), axis=-1, keepdims=True))
    w_intra = jnp.exp(jnp.where(mask, (bc - m_t) + key_row, NEG_BIG))
    w_inter = jnp.exp(inter - m_t)
    sc = jnp.dot(q, kt, preferred_element_type=F32) * w_intra
    q_inter = (q.astype(F32) * w_inter).astype(BF16)
    both = (jnp.dot(q_inter, c_st.astype(BF16), preferred_element_type=F32)
            + jnp.dot(sc.astype(BF16), v, preferred_element_type=F32))
    den = both[:, dv:dv + 1]
    h = both[:, :dv] / jnp.maximum(jnp.abs(den), jnp.exp(-m_t))
    g_row = total + key_row
    m_next = jnp.maximum(total + m_st, jnp.max(g_row, axis=-1, keepdims=True))
    a_prev = jnp.exp(total + m_st - m_next)
    kwt = (kt.astype(F32) * jnp.exp(g_row - m_next)).astype(BF16)
    c_ref[idx] = a_prev * c_st + jnp.dot(kwt, v, preferred_element_type=F32)
    m_ref[idx:idx + 1, :] = jnp.broadcast_to(m_next, (1, m_ref.shape[1]))
    return h


def _mlstm_body(qf_ref, ktf_ref, vf_ref, gcf_ref, grf_ref, qb_ref, ktb_ref, vb_ref, gcb_ref, grb_ref,
                hf_ref, hb_ref, c_ref, m_ref, *, heads, dk, dv):
    @pl.when(pl.program_id(1) == 0)
    def _():
        c_ref[...] = jnp.zeros_like(c_ref)
        m_ref[...] = jnp.zeros_like(m_ref)

    lc = qf_ref.shape[1]
    ones_blk = jnp.where(lax.broadcasted_iota(I32, (lc, 128), 1) == 0, 1.0, 0.0).astype(BF16)
    t_i = lax.broadcasted_iota(I32, (lc, lc), 0)
    s_i = lax.broadcasted_iota(I32, (lc, lc), 1)
    lower = s_i <= t_i
    upper = s_i >= t_i
    ltri = jnp.where(lower, 1.0, 0.0).astype(BF16)
    utri = jnp.where(upper, 1.0, 0.0).astype(BF16)

    for bb in range(qf_ref.shape[0]):
        for direction, (q_ref, kt_ref, v_ref, gc_ref, gr_ref, h_ref) in enumerate(
                ((qf_ref, ktf_ref, vf_ref, gcf_ref, grf_ref, hf_ref),
                 (qb_ref, ktb_ref, vb_ref, gcb_ref, grb_ref, hb_ref))):
            fwd = direction == 0
            gc = gc_ref[bb]
            gr = gr_ref[bb]
            lfc, lfr = _log_sigmoid(gc), _log_sigmoid(gr)
            cum_c = _dot_split(ltri if fwd else utri, lfc, a_is_f32=False)
            cum_r = _dot_split(lfr, utri if fwd else ltri, a_is_f32=True)
            for hd in range(heads):
                gi = (0 if fwd else 2) * heads + hd
                gf = (1 if fwd else 3) * heads + hd
                bc, br = cum_c[:, gf:gf + 1], cum_r[gf:gf + 1, :]
                total = br[:, lc - 1:lc] if fwd else br[:, 0:1]
                q = q_ref[bb, :, hd * dk:(hd + 1) * dk]
                kt = kt_ref[bb, hd * dk:(hd + 1) * dk, :]
                v = jnp.concatenate([v_ref[bb, :, hd * dv:(hd + 1) * dv], ones_blk], axis=1)
                h = _mlstm_chain(q, kt, v, bc, br, gr[gi:gi + 1, :], total,
                                 lower if fwd else upper, c_ref, m_ref, (bb * 2 + direction) * heads + hd)
                h_ref[bb, :, hd * dv:(hd + 1) * dv] = h.astype(h_ref.dtype)


def mlstm_scan(q, kt, u, v_blk, gcol, grow):
    b, s, w = q.shape
    heads = ML_HEADS
    dk = w // heads
    dv = 2 * dk
    lc = ML_CHUNK
    nc = s // lc
    ng = gcol.shape[-1]

    def fw(bi, j):
        return j

    def bw(bi, j):
        return nc - 1 - j

    nbs = ML_BATCH_PER_STEP if b % ML_BATCH_PER_STEP == 0 else 1

    def specs(pos):
        return [pl.BlockSpec((nbs, lc, w), lambda bi, j: (bi, pos(bi, j), 0)),
                pl.BlockSpec((nbs, w, lc), lambda bi, j: (bi, 0, pos(bi, j))),
                pl.BlockSpec((nbs, lc, heads * dv), lambda bi, j: (bi, pos(bi, j), v_blk)),
                pl.BlockSpec((nbs, lc, ng), lambda bi, j: (bi, pos(bi, j), 0)),
                pl.BlockSpec((nbs, ng, lc), lambda bi, j: (bi, 0, pos(bi, j)))]

    hshape = jax.ShapeDtypeStruct((b, s, heads * dv), BF16)
    return pl.pallas_call(
        functools.partial(_mlstm_body, heads=heads, dk=dk, dv=dv), grid=(b // nbs, nc),
        in_specs=specs(fw) + specs(bw),
        out_specs=[pl.BlockSpec((nbs, lc, heads * dv), lambda bi, j: (bi, j, 0)),
                   pl.BlockSpec((nbs, lc, heads * dv), lambda bi, j: (bi, nc - 1 - j, 0))],
        out_shape=[hshape, hshape],
        scratch_shapes=[pltpu.VMEM((nbs * 2 * heads, dk, dv + 128), F32),
                        pltpu.VMEM((nbs * 2 * heads, 128), F32)],
        compiler_params=_params("parallel", "arbitrary"), name="mlstm_scan",
    )(q, kt, u, gcol, grow, q, kt, u, gcol, grow)


def _mlstm_out_body(res_ref, hf_ref, hb_ref, o_ref, g_ref, w_ref, out_ref, *, heads):
    hs = hf_ref[...].astype(F32) + hb_ref[...].astype(F32)
    dv = hs.shape[1] // heads
    g = g_ref[...]
    parts = []
    for hd in range(heads):
        seg = hs[:, hd * dv:(hd + 1) * dv]
        ms = jnp.mean(seg * seg, axis=-1, keepdims=True)
        parts.append(seg * lax.rsqrt(ms + EPS) * g[:, hd * dv:(hd + 1) * dv])
    a = jnp.concatenate(parts, axis=-1) * _sigmoid(o_ref[...].astype(F32))
    out_ref[...] = res_ref[...] + jnp.dot(a.astype(BF16), w_ref[...], preferred_element_type=F32)


def mlstm_out(res, hf, hb, u2d, o_blk, gain, w_out, tm=1024):
    t, d = res.shape
    row = lambda i: (i, 0)
    return pl.pallas_call(
        functools.partial(_mlstm_out_body, heads=ML_HEADS), grid=(t // tm,),
        in_specs=[pl.BlockSpec((tm, d), row), pl.BlockSpec((tm, d), row), pl.BlockSpec((tm, d), row),
                  pl.BlockSpec((tm, d), lambda i: (i, o_blk)), pl.BlockSpec((1, d), lambda i: (0, 0)),
                  pl.BlockSpec((d, d), lambda i: (0, 0))],
        out_specs=pl.BlockSpec((tm, d), row), out_shape=jax.ShapeDtypeStruct((t, d), F32),
        compiler_params=_params("parallel"), name="mlstm_out",
    )(res, hf, hb, u2d, gain.reshape(1, d), w_out)


ROUTE_ROWS = 128
EXPERT_ROW0 = 8


def _router_body(x_ref, g_ref, wt_ref, b_ref, o_ref, cnt_ref, run_ref):
    @pl.when(pl.program_id(0) == 0)
    def _():
        run_ref[...] = jnp.zeros_like(run_ref)

    x = x_ref[...]
    ms = jnp.mean(x * x, axis=-1, keepdims=True)
    xn = x * lax.rsqrt(ms + EPS) * g_ref[...]
    wt = wt_ref[...]
    w_hi, x_hi = wt.astype(BF16), xn.astype(BF16)
    w_lo, x_lo = (wt - w_hi.astype(F32)).astype(BF16), (xn - x_hi.astype(F32)).astype(BF16)
    nt = (((1,), (1,)), ((), ()))
    logit = (lax.dot_general(w_hi, x_hi, nt, preferred_element_type=F32)
             + lax.dot_general(w_lo, x_hi, nt, preferred_element_type=F32)
             + lax.dot_general(w_hi, x_lo, nt, preferred_element_type=F32)) + b_ref[...]
    rows = [logit[r:r + 1, :] for r in range(EXPERT_ROW0 + N_EXPERTS)]
    g_best, g_idx = rows[0], jnp.zeros_like(rows[0])
    for gi in range(1, N_GROUPS):
        better = rows[gi] > g_best
        g_best = jnp.where(better, rows[gi], g_best)
        g_idx = jnp.where(better, float(gi), g_idx)
    g_den = sum(jnp.exp(rows[gi] - g_best) for gi in range(N_GROUPS))
    g_w = 1.0 / g_den
    sel = []
    for e in range(EXPERTS_PER_GROUP):
        v = rows[EXPERT_ROW0 + e]
        for gi in range(1, N_GROUPS):
            v = jnp.where(g_idx == float(gi), rows[EXPERT_ROW0 + gi * EXPERTS_PER_GROUP + e], v)
        sel.append(v)
    v1, i1 = sel[0], jnp.zeros_like(sel[0])
    for e in range(1, EXPERTS_PER_GROUP):
        better = sel[e] > v1
        v1 = jnp.where(better, sel[e], v1)
        i1 = jnp.where(better, float(e), i1)
    v2, i2 = jnp.full_like(v1, -jnp.inf), jnp.zeros_like(v1)
    for e in range(EXPERTS_PER_GROUP):
        better = (sel[e] > v2) & (i1 != float(e))
        v2 = jnp.where(better, sel[e], v2)
        i2 = jnp.where(better, float(e), i2)
    e21 = jnp.exp(v2 - v1)
    gate1 = g_w / (1.0 + e21)
    gate2 = gate1 * e21
    base = g_idx * float(EXPERTS_PER_GROUP)
    e1, e2 = base + i1, base + i2
    tm = e1.shape[1]
    erow = lax.broadcasted_iota(I32, (N_EXPERTS, tm), 0).astype(F32)
    oh1 = jnp.where(erow == e1, 1.0, 0.0)
    oh2 = jnp.where(erow == e2, 1.0, 0.0)
    cnt = oh1 + oh2
    earlier = jnp.where(lax.broadcasted_iota(I32, (tm, tm), 0) < lax.broadcasted_iota(I32, (tm, tm), 1),
                        1.0, 0.0).astype(BF16)
    run = run_ref[...]
    pos = run[:, 0:1] + jnp.dot(cnt.astype(BF16), earlier, preferred_element_type=F32)
    rank1 = jnp.sum(oh1 * pos, axis=0, keepdims=True)
    rank2 = jnp.sum(oh2 * pos, axis=0, keepdims=True)
    run = run + jnp.sum(cnt, axis=1, keepdims=True)
    run_ref[...] = run
    cnt_ref[...] = run
    zero = jnp.zeros_like(v1)
    o_ref[...] = jnp.concatenate([e1, e2, gate1, gate2, rank1, rank2, zero, zero], axis=0)


def moe_router(x2d, g, wg, bg, we, be, tm=512):
    t, d = x2d.shape
    wt = jnp.zeros((ROUTE_ROWS, d), F32).at[:N_GROUPS].set(wg.T).at[EXPERT_ROW0:EXPERT_ROW0 + N_EXPERTS].set(we.T)
    bias = jnp.zeros((ROUTE_ROWS, 1), F32).at[:N_GROUPS, 0].set(bg).at[EXPERT_ROW0:EXPERT_ROW0 + N_EXPERTS, 0].set(be)
    return pl.pallas_call(
        _router_body, grid=(t // tm,),
        in_specs=[pl.BlockSpec((tm, d), lambda i: (i, 0)), pl.BlockSpec((1, d), lambda i: (0, 0)),
                  pl.BlockSpec((ROUTE_ROWS, d), lambda i: (0, 0)), pl.BlockSpec((ROUTE_ROWS, 1), lambda i: (0, 0))],
        out_specs=[pl.BlockSpec((8, tm), lambda i: (0, i)), pl.BlockSpec((N_EXPERTS, 128), lambda i: (0, 0))],
        out_shape=[jax.ShapeDtypeStruct((8, t), F32), jax.ShapeDtypeStruct((N_EXPERTS, 128), F32)],
        scratch_shapes=[pltpu.VMEM((N_EXPERTS, 128), F32)],
        compiler_params=_params("arbitrary"), name="moe_router",
    )(x2d, g.reshape(1, d), wt, bias)


def _dispatch_body(dest_ref, zblk_ref, x_ref, g_ref, xs_ref, buf_ref, sem_ref, *, td):
    i = pl.program_id(0)
    n = pl.num_programs(0)
    slot = i % 2

    @pl.when(i == 0)
    def _():
        buf_ref[1] = jnp.zeros(buf_ref.shape[1:], buf_ref.dtype)

        def zero_copy(j):
            return pltpu.make_async_copy(buf_ref.at[1], xs_ref.at[pl.ds(zblk_ref[j] * td, td), :], sem_ref.at[1])

        def start(j, c):
            @pl.when(zblk_ref[j] >= 0)
            def _():
                zero_copy(j).start()
            return c

        def wait(j, c):
            @pl.when(zblk_ref[j] >= 0)
            def _():
                zero_copy(j).wait()
            return c
        lax.fori_loop(0, zblk_ref.shape[0], start, 0)
        lax.fori_loop(0, zblk_ref.shape[0], wait, 0)

    def row_copy(r, kk):
        return pltpu.make_async_copy(buf_ref.at[slot, pl.ds(r, 1), :],
                                     xs_ref.at[pl.ds(dest_ref[(i * td + r) * 2 + kk], 1), :],
                                     sem_ref.at[slot])

    def wait_buffer(sl):
        for _ in range(2):
            pltpu.make_async_copy(buf_ref.at[sl], xs_ref.at[pl.ds(0, td), :], sem_ref.at[sl]).wait()

    @pl.when(i >= 2)
    def _():
        wait_buffer(slot)

    x = x_ref[...]
    ms = jnp.mean(x * x, axis=-1, keepdims=True)
    buf_ref[slot] = x * lax.rsqrt(ms + EPS) * g_ref[...]

    for r in range(td):
        row_copy(r, 0).start(priority=0)
        row_copy(r, 1).start(priority=1)

    @pl.when(i == n - 1)
    def _():
        @pl.when(n >= 2)
        def _():
            wait_buffer(1 - slot)
        wait_buffer(slot)


def moe_dispatch(x2d, g, dest, zero_blk, cap, td):
    t, d = x2d.shape
    grid_spec = pltpu.PrefetchScalarGridSpec(
        num_scalar_prefetch=2, grid=(t // td,),
        in_specs=[pl.BlockSpec((td, d), lambda i, dest, zb: (i, 0)),
                  pl.BlockSpec((1, d), lambda i, dest, zb: (0, 0))],
        out_specs=pl.BlockSpec(memory_space=pl.ANY),
        scratch_shapes=[pltpu.VMEM((2, td, d), F32), pltpu.SemaphoreType.DMA((2,))])
    return pl.pallas_call(
        functools.partial(_dispatch_body, td=td), grid_spec=grid_spec,
        out_shape=jax.ShapeDtypeStruct((cap, d), F32),
        compiler_params=_params("arbitrary", disable_bounds_checks=True),
        name="moe_dispatch",
    )(dest, zero_blk, x2d, g.reshape(1, d))


def _expert_body(blk_e_ref, nused_ref, xs_ref, w1_ref, w3_ref, w2_ref, ys_ref, w1b_ref, w3b_ref, w2b_ref):
    i = pl.program_id(0)
    nused = nused_ref[0]
    last = nused - 1
    cur = blk_e_ref[jnp.minimum(i, last)]
    prev = blk_e_ref[jnp.minimum(jnp.maximum(i - 1, 0), last)]

    @pl.when((i == 0) | (cur != prev))
    def _():
        w1b_ref[...] = w1_ref[...].astype(BF16)
        w3b_ref[...] = w3_ref[...].astype(BF16)
        w2b_ref[...] = w2_ref[...].astype(BF16)

    @pl.when(i < nused)
    def _():
        x = xs_ref[...].astype(BF16)
        h1 = jnp.dot(x, w1b_ref[...], preferred_element_type=F32)
        h3 = jnp.dot(x, w3b_ref[...], preferred_element_type=F32)
        hid = (h1 * _sigmoid(h1) * h3).astype(BF16)
        ys_ref[...] = jnp.dot(hid, w2b_ref[...], preferred_element_type=F32)

    @pl.when(i >= nused)
    def _():
        ys_ref[...] = jnp.zeros_like(ys_ref)


def moe_experts(xs, blk_e, nused, w1, w3, w2, layer, tm):
    cap, d = xs.shape
    de = w1.shape[-1]

    def blk(i, be, nu):
        return jnp.minimum(i, nu[0] - 1)

    grid_spec = pltpu.PrefetchScalarGridSpec(
        num_scalar_prefetch=2, grid=(cap // tm,),
        in_specs=[pl.BlockSpec((tm, d), lambda i, be, nu: (blk(i, be, nu), 0)),
                  pl.BlockSpec((None, None, d, de), lambda i, be, nu: (layer, be[blk(i, be, nu)], 0, 0)),
                  pl.BlockSpec((None, None, d, de), lambda i, be, nu: (layer, be[blk(i, be, nu)], 0, 0)),
                  pl.BlockSpec((None, None, de, d), lambda i, be, nu: (layer, be[blk(i, be, nu)], 0, 0))],
        out_specs=pl.BlockSpec((tm, d), lambda i, be, nu: (i, 0)),
        scratch_shapes=[pltpu.VMEM((d, de), BF16), pltpu.VMEM((d, de), BF16), pltpu.VMEM((de, d), BF16)])
    return pl.pallas_call(
        _expert_body, grid_spec=grid_spec, out_shape=jax.ShapeDtypeStruct((cap, d), F32),
        compiler_params=_params("arbitrary"), name="moe_experts",
    )(blk_e, nused, xs, w1, w3, w2)


def _combine_body(dest_ref, x_ref, gate_ref, ys_ref, o_ref, buf_ref, sem_ref, *, tc):
    i = pl.program_id(0)
    n = pl.num_programs(0)
    slot = i % 2

    def row_copy(step, sl, r, kk):
        return pltpu.make_async_copy(ys_ref.at[pl.ds(dest_ref[(step * tc + r) * 2 + kk], 1), :],
                                     buf_ref.at[sl, kk, pl.ds(r, 1), :], sem_ref.at[sl])

    def issue_step(step, sl):
        for r in range(tc):
            row_copy(step, sl, r, 0).start(priority=0)
            row_copy(step, sl, r, 1).start(priority=1)

    @pl.when(i == 0)
    def _():
        issue_step(0, 0)

    @pl.when(i + 1 < n)
    def _():
        issue_step(i + 1, 1 - slot)

    for kk in range(2):
        pltpu.make_async_copy(ys_ref.at[pl.ds(0, tc), :], buf_ref.at[slot, kk], sem_ref.at[slot]).wait()

    gate = gate_ref[...]
    o_ref[...] = x_ref[...] + gate[:, 0:1] * buf_ref[slot, 0] + gate[:, 1:2] * buf_ref[slot, 1]


def moe_combine(x2d, gates, ys, dest, tc=256):
    t, d = x2d.shape
    grid_spec = pltpu.PrefetchScalarGridSpec(
        num_scalar_prefetch=1, grid=(t // tc,),
        in_specs=[pl.BlockSpec((tc, d), lambda i, dest: (i, 0)), pl.BlockSpec((tc, 2), lambda i, dest: (i, 0)),
                  pl.BlockSpec(memory_space=pl.ANY)],
        out_specs=pl.BlockSpec((tc, d), lambda i, dest: (i, 0)),
        scratch_shapes=[pltpu.VMEM((2, 2, tc, d), F32), pltpu.SemaphoreType.DMA((2,))])
    return pl.pallas_call(
        functools.partial(_combine_body, tc=tc), grid_spec=grid_spec,
        out_shape=jax.ShapeDtypeStruct((t, d), F32),
        compiler_params=_params("arbitrary", disable_bounds_checks=True), name="moe_combine",
    )(dest, x2d, gates, ys)


def hier_moe_residual(x2d, g, wg, bg, we, be, w1, w3, w2, layer, tm=512):
    t, d = x2d.shape
    route, counts_b = moe_router(x2d, g, wg, bg, we, be)
    eid = route[0:2].T.astype(I32).reshape(-1)
    rank = route[4:6].T.astype(I32).reshape(-1)
    gates = route[2:4].T
    counts = counts_b[:, 0].astype(I32)
    padded = (counts + tm - 1) // tm * tm
    pad_end = jnp.cumsum(padded)
    pad_start = pad_end - padded
    experts = jnp.arange(N_EXPERTS, dtype=I32)
    dest = rank + jnp.sum(jnp.where(eid[:, None] == experts[None, :], pad_start[None, :], 0), axis=1)
    cap = t * 2 + N_EXPERTS * tm
    nblk = cap // tm
    blk_row0 = jnp.arange(nblk, dtype=I32) * tm
    blk_e = jnp.minimum(jnp.sum((pad_end[None, :] <= blk_row0[:, None]).astype(I32), axis=1), N_EXPERTS - 1)
    nused = (pad_end[-1:] // tm).astype(I32)
    last_blk = jnp.where(padded > 0, pad_end // tm - 1, -1)
    tail_blk = jnp.where(nused[0] + experts < nblk, nused[0] + experts, -1)
    zero_blk = jnp.concatenate([last_blk, tail_blk]).astype(I32)
    xs = moe_dispatch(x2d, g, dest.astype(I32), zero_blk, cap, tm)
    ys = moe_experts(xs, blk_e, nused, w1, w3, w2, layer, tm)
    return moe_combine(x2d, gates, ys, dest.astype(I32))


def _even_layer(x2d, b, s, layer, mix_g, w_in, q_norm, k_norm, lam_q1, lam_k1, lam_q2, lam_k2, subln,
                hy_conv_w, hy_conv_b, f_w1, f_b1, f_freq, f_w2, f_b2, f_w3, hy_skip, w_out):
    d = x2d.shape[1]
    d_att = d // 2
    (u2d,) = norm_matmul(x2d, mix_g, [w_in.astype(BF16)], [BF16])
    u = u2d.reshape(b, s, -1)
    lambda_init = 0.8 - 0.6 * math.exp(-0.3 * layer)
    lam = jnp.exp(jnp.sum(lam_q1 * lam_k1)) - jnp.exp(jnp.sum(lam_q2 * lam_k2)) + lambda_init
    dk = q_norm.shape[0]
    qt = head_prep(u, 0, q_norm, scale=dk ** -0.5 * math.log2(math.e), transpose=True)
    kp = head_prep(u, ATT_HEADS, k_norm)
    vt = head_prep(u, 2 * ATT_HEADS, transpose=True)
    y_att = diff_attention(qt, kp, vt, lam, subln, lambda_init)
    z, x1c, zp = hy_prep(u, 3 * d_att, hy_conv_w, hy_conv_b)
    tables = fft_tables(s)
    kf = filter_spectrum(hyena_filter(s, f_w1, f_b1, f_freq, f_w2, f_b2, f_w3), tables[0], tables[1])
    y_hy = hy_fft_conv(zp, z, x1c, kf, tables, hy_skip)
    w_out = w_out.astype(BF16)
    return matmul_residual(x2d, [y_att.reshape(b * s, -1), y_hy.reshape(b * s, -1)],
                           [w_out[:d_att], w_out[d_att:]])


def _odd_layer(x2d, b, s, mix_g, w_in, conv_w, conv_b, gate_b, out_norm, w_out):
    d = x2d.shape[1]
    qk_w = conv_w.shape[1]
    main_w = qk_w + 2 * d
    ng = 4 * ML_HEADS
    w_gate = jnp.zeros((d, 128), F32).at[:, :ng].set(w_in[:, main_w:]).astype(BF16)
    u2d, ug = norm_matmul(x2d, mix_g, [w_in[:, :main_w].astype(BF16), w_gate], [BF16, F32])
    u = u2d.reshape(b, s, main_w)
    gcol = (ug[:, :ng] + gate_b).reshape(b, s, ng)
    grow = jnp.swapaxes(gcol, 1, 2)
    dk = qk_w // (2 * ML_HEADS)
    q = ml_prep(u, conv_w, conv_b, 0, qk_w // 2, dk ** -0.5, transpose=False)
    kt = ml_prep(u, conv_w, conv_b, qk_w // 2, qk_w // 2, 1.0, transpose=True)
    hf, hb = mlstm_scan(q, kt, u, qk_w // d, gcol, grow)
    return mlstm_out(x2d, hf.reshape(b * s, d), hb.reshape(b * s, d), u2d, (qk_w + d) // d, out_norm,
                     w_out.astype(BF16))


def kernel(x, mix_norm, ffn_norm, ev_w_in, ev_q_norm, ev_k_norm, ev_lam_q1, ev_lam_k1, ev_lam_q2, ev_lam_k2, ev_subln, ev_hy_conv_w, ev_hy_conv_b, ev_hy_f_w1, ev_hy_f_b1, ev_hy_f_freq, ev_hy_f_w2, ev_hy_f_b2, ev_hy_f_w3, ev_hy_skip, ev_w_out, od_w_in, od_conv_w, od_conv_b, od_gate_b, od_out_norm, od_w_out, moe_wg, moe_bg, moe_we, moe_be, moe_w1, moe_w3, moe_w2):
    b, s, d = x.shape
    depth = mix_norm.shape[0]
    x2d = x.reshape(b * s, d)
    for layer in range(depth):
        j = layer // 2
        if layer % 2 == 0:
            x2d = _even_layer(x2d, b, s, layer, mix_norm[layer], ev_w_in[j], ev_q_norm[j], ev_k_norm[j],
                              ev_lam_q1[j], ev_lam_k1[j], ev_lam_q2[j], ev_lam_k2[j], ev_subln[j],
                              ev_hy_conv_w[j], ev_hy_conv_b[j], ev_hy_f_w1[j], ev_hy_f_b1[j], ev_hy_f_freq[j],
                              ev_hy_f_w2[j], ev_hy_f_b2[j], ev_hy_f_w3[j], ev_hy_skip[j], ev_w_out[j])
        else:
            x2d = _odd_layer(x2d, b, s, mix_norm[layer], od_w_in[j], od_conv_w[j], od_conv_b[j], od_gate_b[j],
                             od_out_norm[j], od_w_out[j])
        x2d = hier_moe_residual(x2d, ffn_norm[layer], moe_wg[layer], moe_bg[layer], moe_we[layer],
                                moe_be[layer], moe_w1, moe_w3, moe_w2, layer)
    return x2d.reshape(b, s, d)
```

```python
import functools
import math

import jax
import jax.numpy as jnp
from jax import lax
from jax.experimental import pallas as pl
from jax.experimental.pallas import tpu as pltpu

F32 = jnp.float32
BF16 = jnp.bfloat16
I32 = jnp.int32

EPS = 1e-6
ROPE_THETA = 500000.0
ATT_HEADS = 4
ML_HEADS = 4
ML_CHUNK = 128
ML_BATCH_PER_STEP = 2
N_GROUPS = 4
EXPERTS_PER_GROUP = 8
N_EXPERTS = N_GROUPS * EXPERTS_PER_GROUP
HY_EMB_BANDS = 16
HY_MIN_DECAY = math.log(1e-2) / 1.5
HY_MAX_DECAY = math.log(1e-2) / 0.3

V7X_VMEM_BYTES = 64 * 1024 * 1024
VMEM_LIMIT = V7X_VMEM_BYTES - 8 * 1024 * 1024
NEG_BIG = -1e30


def _params(*sem, **kw):
    return pltpu.CompilerParams(dimension_semantics=sem, vmem_limit_bytes=VMEM_LIMIT, **kw)


def _sigmoid(x):
    return 1.0 / (1.0 + jnp.exp(-x))


def _norm_matmul_body(x_ref, g_ref, *refs, n_out, col_chunk):
    w_refs, o_refs = refs[:n_out], refs[n_out:]
    x = x_ref[...]
    ms = jnp.mean(x * x, axis=-1, keepdims=True)
    hn = (x * lax.rsqrt(ms + EPS) * g_ref[...]).astype(BF16)
    for w_ref, o_ref in zip(w_refs, o_refs):
        n = w_ref.shape[1]
        for c in range(0, n, col_chunk):
            ce = min(n, c + col_chunk)
            o_ref[:, c:ce] = jnp.dot(hn, w_ref[:, c:ce], preferred_element_type=F32).astype(o_ref.dtype)


def norm_matmul(x2d, g, ws, out_dtypes, tm=1024):
    t, d = x2d.shape
    in_specs = [pl.BlockSpec((tm, d), lambda i: (i, 0)), pl.BlockSpec((1, d), lambda i: (0, 0))]
    in_specs += [pl.BlockSpec(w.shape, lambda i: (0, 0)) for w in ws]
    out_specs = [pl.BlockSpec((tm, w.shape[1]), lambda i: (i, 0)) for w in ws]
    out_shape = [jax.ShapeDtypeStruct((t, w.shape[1]), dt) for w, dt in zip(ws, out_dtypes)]
    return pl.pallas_call(
        functools.partial(_norm_matmul_body, n_out=len(ws), col_chunk=1024),
        grid=(t // tm,), in_specs=in_specs, out_specs=out_specs, out_shape=out_shape,
        compiler_params=_params("parallel"), name="norm_matmul",
    )(x2d, g.reshape(1, d), *ws)


def _matmul_res_body(res_ref, *refs, n_in):
    a_refs, w_refs, o_ref = refs[:n_in], refs[n_in:2 * n_in], refs[2 * n_in]
    acc = res_ref[...]
    for a_ref, w_ref in zip(a_refs, w_refs):
        acc = acc + jnp.dot(a_ref[...], w_ref[...], preferred_element_type=F32)
    o_ref[...] = acc


def matmul_residual(res, a_list, w_list, tm=1024):
    t, d = res.shape
    in_specs = [pl.BlockSpec((tm, d), lambda i: (i, 0))]
    in_specs += [pl.BlockSpec((tm, a.shape[1]), lambda i: (i, 0)) for a in a_list]
    in_specs += [pl.BlockSpec(w.shape, lambda i: (0, 0)) for w in w_list]
    return pl.pallas_call(
        functools.partial(_matmul_res_body, n_in=len(a_list)),
        grid=(t // tm,), in_specs=in_specs, out_specs=pl.BlockSpec((tm, d), lambda i: (i, 0)),
        out_shape=jax.ShapeDtypeStruct((t, d), F32),
        compiler_params=_params("parallel"), name="matmul_residual",
    )(res, *a_list, *w_list)


def _head_prep_body(u_ref, *refs, dk, rope, transpose):
    o_ref = refs[-1]
    x = u_ref[...].astype(F32)
    if rope:
        g_ref, c_ref, s1_ref, s2_ref = refs[:4]
        lane = lax.broadcasted_iota(I32, x.shape, 1)
        lo = lane < dk
        x2 = x * x
        s_lo = jnp.sum(jnp.where(lo, x2, 0.0), axis=-1, keepdims=True)
        s_hi = jnp.sum(jnp.where(lo, 0.0, x2), axis=-1, keepdims=True)
        ms = jnp.where(lo, s_lo, s_hi) * (1.0 / dk)
        y = x * lax.rsqrt(ms + EPS) * g_ref[...]
        x = y * c_ref[...] + pltpu.roll(y, 120, 1) * s1_ref[...] + pltpu.roll(y, 8, 1) * s2_ref[...]
    if transpose:
        x = x.T
    o_ref[...] = x.astype(o_ref.dtype)


def _rope_lane_tables(seq, dk, rope_dim, scale):
    half = rope_dim // 2
    inv_freq = 1.0 / (ROPE_THETA ** (jnp.arange(0, rope_dim, 2, dtype=F32) / rope_dim))
    ang = jnp.arange(seq, dtype=F32)[:, None] * inv_freq[None, :]
    cos, sin = jnp.cos(ang), jnp.sin(ang)
    d = jnp.arange(2 * dk) % dk
    fi = d % half
    c_tab = jnp.where(d[None, :] < rope_dim, cos[:, fi], 1.0)
    s1_tab = jnp.where(d[None, :] < half, -sin[:, fi], 0.0)
    s2_tab = jnp.where((d[None, :] >= half) & (d[None, :] < rope_dim), sin[:, fi], 0.0)
    return (jnp.stack([c_tab, s1_tab, s2_tab]) * scale).astype(F32)


def head_prep(u, blk0, norm_gain=None, scale=1.0, transpose=False):
    b, s, _ = u.shape
    h = ATT_HEADS
    rope = norm_gain is not None
    in_specs = [pl.BlockSpec((None, s, 128), lambda c, bi: (bi, 0, blk0 + c))]
    args = [u]
    dk = 64
    if rope:
        dk = norm_gain.shape[0]
        assert 2 * dk == 128 and dk // 4 == 16, "rope roll shifts assume 64-wide components, 16 rotary dims"
        tabs = _rope_lane_tables(s, dk, dk // 4, scale)
        in_specs += [pl.BlockSpec((1, 128), lambda c, bi: (0, 0))] + [pl.BlockSpec((s, 128), lambda c, bi: (0, 0))] * 3
        args += [jnp.tile(norm_gain, 2).reshape(1, 128).astype(F32), tabs[0], tabs[1], tabs[2]]
    if transpose:
        out_spec = pl.BlockSpec((None, 128, s), lambda c, bi: (bi, c, 0))
        out_shape = jax.ShapeDtypeStruct((b, h * 128, s), BF16)
    else:
        out_spec = pl.BlockSpec((None, s, 128), lambda c, bi: (bi, 0, c))
        out_shape = jax.ShapeDtypeStruct((b, s, h * 128), BF16)
    return pl.pallas_call(
        functools.partial(_head_prep_body, dk=dk, rope=rope, transpose=transpose), grid=(h, b),
        in_specs=in_specs, out_specs=out_spec, out_shape=out_shape,
        compiler_params=_params("parallel", "parallel"), name="head_prep",
    )(*args)


def _attn_body(lam_ref, qt_ref, k_ref, vt_ref, g_ref, o_ref, *, tq, dk, post_scale, n_split):
    lam = lam_ref[0, 0]
    th = tq // n_split
    for part in range(n_split):
        qt = qt_ref[:, part * th:(part + 1) * th]
        row = lax.broadcasted_iota(I32, qt.shape, 0)
        zero = jnp.zeros_like(qt)
        qq = jnp.concatenate([jnp.where(row < dk, qt, zero), jnp.where(row < dk, zero, qt)], axis=1)
        st = jnp.dot(k_ref[...], qq, preferred_element_type=F32)
        m = jnp.max(st, axis=0, keepdims=True)
        p = jnp.exp2(st - m)
        r = 1.0 / jnp.sum(p, axis=0, keepdims=True)
        ot = jnp.dot(vt_ref[...], p.astype(BF16), preferred_element_type=F32)
        o = (ot[:, :th] * r[:, :th] - ot[:, th:] * (lam * r[:, th:])).T
        ms = jnp.mean(o * o, axis=-1, keepdims=True)
        o_ref[part * th:(part + 1) * th, :] = (o * lax.rsqrt(ms + EPS) * g_ref[...] * post_scale).astype(o_ref.dtype)


def diff_attention(qt, k, vt, lam, subln, lambda_init, tq=512):
    b, s, _ = k.shape
    h = ATT_HEADS
    return pl.pallas_call(
        functools.partial(_attn_body, tq=tq, dk=64, post_scale=1.0 - lambda_init, n_split=1),
        grid=(b, h, s // tq),
        in_specs=[pl.BlockSpec(memory_space=pltpu.SMEM),
                  pl.BlockSpec((None, 128, tq), lambda bi, hi, i: (bi, hi, i)),
                  pl.BlockSpec((None, s, 128), lambda bi, hi, i: (bi, 0, hi)),
                  pl.BlockSpec((None, 128, s), lambda bi, hi, i: (bi, hi, 0)),
                  pl.BlockSpec((1, 128), lambda bi, hi, i: (0, 0))],
        out_specs=pl.BlockSpec((None, tq, 128), lambda bi, hi, i: (bi, i, hi)),
        out_shape=jax.ShapeDtypeStruct((b, s, h * 128), BF16),
        compiler_params=_params("parallel", "parallel", "parallel"), name="diff_attention",
    )(lam.reshape(1, 1).astype(F32), qt, k, vt, subln.reshape(1, 128).astype(F32))


def _conv3(u_ref, w_ref, b_ref):
    x = u_ref[...].astype(F32)
    s = x.shape[0]
    row = lax.broadcasted_iota(I32, x.shape, 0)
    x_prev = jnp.where(row == 0, 0.0, pltpu.roll(x, 1, 0))
    x_next = jnp.where(row == s - 1, 0.0, pltpu.roll(x, s - 1, 0))
    w = w_ref[...]
    return b_ref[...] + x_prev * w[0:1] + x * w[1:2] + x_next * w[2:3]


FFT_N1 = 64
FFT_UNROLL = 64
FFT_PAD = 8


def _hy_prep_body(x1_ref, x2_ref, v_ref, w1_ref, w2_ref, wv_ref, b1_ref, b2_ref, bv_ref, z_ref, x1c_ref, zp_ref):
    x1c_ref[...] = _conv3(x1_ref, w1_ref, b1_ref).astype(x1c_ref.dtype)
    z = _conv3(v_ref, wv_ref, bv_ref) * _conv3(x2_ref, w2_ref, b2_ref)
    z_ref[...] = z.astype(z_ref.dtype)
    nb = z.shape[0] // FFT_N1
    zp_ref[...] = jnp.zeros_like(zp_ref)
    for n2 in range(nb):
        for ci in range(z.shape[1] // 128):
            zp_ref[ci, pl.ds(n2, FFT_N1, stride=nb + FFT_PAD), :] = (
                z[n2 * FFT_N1:(n2 + 1) * FFT_N1, ci * 128:(ci + 1) * 128])


def hy_prep(u, col0, conv_w, conv_b, tc=256):
    b, s, _ = u.shape
    d_hy = conv_w.shape[1] // 3
    nct = d_hy // tc
    blk0 = col0 // tc
    sp = FFT_N1 * (s // FFT_N1 + FFT_PAD)

    def uspec(part):
        return pl.BlockSpec((None, s, tc), lambda bi, c: (bi, 0, blk0 + part * nct + c))

    def wspec(part, rows):
        return pl.BlockSpec((rows, tc), lambda bi, c: (0, part * nct + c))

    ospec = pl.BlockSpec((None, s, tc), lambda bi, c: (bi, 0, c))
    return pl.pallas_call(
        _hy_prep_body, grid=(b, nct),
        in_specs=[uspec(0), uspec(1), uspec(2), wspec(0, 3), wspec(1, 3), wspec(2, 3),
                  wspec(0, 1), wspec(1, 1), wspec(2, 1)],
        out_specs=[ospec, ospec, pl.BlockSpec((None, tc // 128, sp, 128), lambda bi, c: (bi, c, 0, 0))],
        out_shape=[jax.ShapeDtypeStruct((b, s, d_hy), BF16), jax.ShapeDtypeStruct((b, s, d_hy), BF16),
                   jax.ShapeDtypeStruct((b, d_hy // 128, sp, 128), F32)],
        compiler_params=_params("parallel", "parallel"), name="hy_prep",
    )(u, u, u, conv_w, conv_w, conv_w, conv_b.reshape(1, -1), conv_b.reshape(1, -1), conv_b.reshape(1, -1))


def hyena_filter(length, w1, b1, freq, w2, b2, w3):
    d_hy = w3.shape[1] // 2
    t = jnp.linspace(0.0, 1.0, length, dtype=F32)[:, None]
    bands = jnp.linspace(1e-4, HY_EMB_BANDS - 1, HY_EMB_BANDS, dtype=F32)[None, :]
    ang = (2.0 * math.pi / length) * jnp.arange(length, dtype=F32)[:, None] * bands
    z = jnp.concatenate([t, jnp.cos(ang), -jnp.sin(ang)], axis=-1)
    hp = lax.Precision.HIGHEST
    hdn = jnp.sin(freq * (jnp.dot(z, w1, precision=hp) + b1))
    hdn = jnp.sin(freq * (jnp.dot(hdn, w2, precision=hp) + b2))
    filt = jnp.dot(hdn, w3, precision=hp)
    deltas = jnp.abs(jnp.linspace(HY_MIN_DECAY, HY_MAX_DECAY, d_hy, dtype=F32))
    decay = jnp.exp(-t * deltas[None, :])
    h_fwd = filt[:, :d_hy] * decay
    h_bwd = filt[:, d_hy:] * decay
    h_fwd = h_fwd.at[0].add(h_bwd[0])
    h_bwd = h_bwd.at[0].set(0.0)
    norm = jnp.sum(jnp.abs(h_fwd), axis=0, keepdims=True) + jnp.sum(jnp.abs(h_bwd), axis=0, keepdims=True) + EPS
    return jnp.concatenate([h_fwd / norm, h_bwd / norm], axis=1)


def fft_tables(length):
    n = 2 * length
    n1c, n2c, nb = FFT_N1, n // FFT_N1, length // FFT_N1
    unit = 2.0 * math.pi / n
    i1 = jnp.arange(n1c, dtype=I32)
    i2 = jnp.arange(n2c, dtype=I32)
    ib = jnp.arange(nb, dtype=I32)
    samp = i1[:, None, None] + n1c * ib[None, None, :]
    ang = ((i2[None, :, None] * samp) % n).astype(F32) * unit
    m1 = jnp.concatenate([jnp.cos(ang), -jnp.sin(ang)], axis=1)
    ang = ((i1[:, None] * i1[None, :]) % n1c).astype(F32) * (2.0 * math.pi / n1c)
    c, s = jnp.cos(ang), jnp.sin(ang)
    f1 = jnp.concatenate([jnp.concatenate([c, s], axis=1), jnp.concatenate([-s, c], axis=1)], axis=0)
    freq = n2c * i1[None, None, :] + i2[:, None, None]
    ang = ((i1[None, :, None] * freq) % n).astype(F32) * unit
    c, s = jnp.cos(ang), jnp.sin(ang)
    g1 = jnp.concatenate([jnp.concatenate([c, -s], axis=2), jnp.concatenate([s, c], axis=2)], axis=1)
    ang = ((ib[:, None] * i2[None, :]) % n2c).astype(F32) * (2.0 * math.pi / n2c)
    g2 = jnp.concatenate([jnp.cos(ang), -jnp.sin(ang)], axis=1) * (1.0 / n)
    return m1.astype(BF16), f1.astype(BF16), g1.astype(BF16), g2.astype(BF16)


def _fft_stage1(xp_ref, m1_ref, p_ref):
    nb = m1_ref.shape[2]
    n2c = m1_ref.shape[1] // 2

    def body(n1, c):
        x = xp_ref[pl.ds(pl.multiple_of(n1 * (nb + FFT_PAD), 8), nb), :].astype(BF16)
        a = jnp.dot(m1_ref[n1], x, preferred_element_type=F32)
        p_ref[0, pl.ds(n1, n2c, stride=FFT_N1 + FFT_PAD), :] = a[:n2c]
        p_ref[1, pl.ds(n1, n2c, stride=FFT_N1 + FFT_PAD), :] = a[n2c:]
        return c
    lax.fori_loop(0, FFT_N1, body, 0, unroll=FFT_UNROLL)


def _fft_stage2(p_ref, f1_ref, k2):
    r0 = pl.multiple_of(k2 * (FFT_N1 + FFT_PAD), 8)
    slab = jnp.concatenate([p_ref[0, pl.ds(r0, FFT_N1), :], p_ref[1, pl.ds(r0, FFT_N1), :]], axis=0)
    return jnp.dot(f1_ref[...], slab.astype(BF16), preferred_element_type=F32)


def _spectrum_body(xp_ref, m1_ref, f1_ref, o_ref, p_ref):
    _fft_stage1(xp_ref, m1_ref, p_ref)

    def body(k2, c):
        o_ref[k2] = _fft_stage2(p_ref, f1_ref, k2)
        return c
    lax.fori_loop(0, o_ref.shape[0], body, 0, unroll=FFT_UNROLL)


def filter_spectrum(ab, m1, f1):
    length, c2 = ab.shape
    nb = length // FFT_N1
    n2c = m1.shape[1] // 2
    nch = c2 // 128
    abp = jnp.pad(ab.reshape(nb, FFT_N1, nch, 128).transpose(2, 1, 0, 3), ((0, 0), (0, 0), (0, FFT_PAD), (0, 0)))
    abp = abp.reshape(nch, FFT_N1 * (nb + FFT_PAD), 128)
    spec = pl.pallas_call(
        _spectrum_body, grid=(nch,),
        in_specs=[pl.BlockSpec((None,) + abp.shape[1:], lambda c: (c, 0, 0)),
                  pl.BlockSpec(m1.shape, lambda c: (0, 0, 0)), pl.BlockSpec(f1.shape, lambda c: (0, 0))],
        out_specs=pl.BlockSpec((None, n2c, 2 * FFT_N1, 128), lambda c: (c, 0, 0, 0)),
        out_shape=jax.ShapeDtypeStruct((nch, n2c, 2 * FFT_N1, 128), F32),
        scratch_shapes=[pltpu.VMEM((2, n2c * (FFT_N1 + FFT_PAD), 128), F32)],
        compiler_params=_params("parallel"), name="filter_spectrum",
    )(abp, m1, f1)
    fa, fb = spec[:nch // 2], spec[nch // 2:]
    h = FFT_N1
    return jnp.concatenate([fa[:, :, :h] + fb[:, :, :h], fa[:, :, h:] - fb[:, :, h:]], axis=2).astype(BF16)


def _hy_fft_body(zp_ref, z_ref, x1c_ref, kf_ref, m1_ref, f1_ref, g1_ref, g2_ref, skip_ref, o_ref,
                 p_ref, q_ref, y_ref):
    h = FFT_N1
    n2c = g1_ref.shape[0]
    nb = g2_ref.shape[0]
    _fft_stage1(zp_ref, m1_ref, p_ref)

    def mid(k2, c):
        xf = _fft_stage2(p_ref, f1_ref, k2)
        kf = kf_ref[k2].astype(F32)
        xr, xi, kr, ki = xf[:h], xf[h:], kf[:h], kf[h:]
        y = jnp.concatenate([xr * kr - xi * ki, xr * ki + xi * kr], axis=0).astype(BF16)
        d = jnp.dot(g1_ref[k2], y, preferred_element_type=F32)
        q_ref[0, pl.ds(k2, h, stride=n2c + FFT_PAD), :] = d[:h]
        q_ref[1, pl.ds(k2, h, stride=n2c + FFT_PAD), :] = d[h:]
        return c
    lax.fori_loop(0, n2c, mid, 0, unroll=FFT_UNROLL)

    def last(t1, c):
        r0 = pl.multiple_of(t1 * (n2c + FFT_PAD), 8)
        slab = jnp.concatenate([q_ref[0, pl.ds(r0, n2c), :], q_ref[1, pl.ds(r0, n2c), :]], axis=0)
        y_ref[pl.ds(t1, nb, stride=h), :] = jnp.dot(g2_ref[...], slab.astype(BF16), preferred_element_type=F32)
        return c
    lax.fori_loop(0, h, last, 0, unroll=FFT_UNROLL)

    z = z_ref[...].astype(F32)
    o_ref[...] = ((y_ref[...] + z * skip_ref[...]) * x1c_ref[...].astype(F32)).astype(o_ref.dtype)


def hy_fft_conv(zp, z, x1c, kf, tables, skip):
    m1, f1, g1, g2 = tables
    b, s, c = z.shape
    nch = c // 128

    def const(shape):
        return pl.BlockSpec(shape, lambda ci, bi: (0,) * len(shape))

    nat = pl.BlockSpec((None, s, 128), lambda ci, bi: (bi, 0, ci))
    return pl.pallas_call(
        _hy_fft_body, grid=(nch, b),
        in_specs=[pl.BlockSpec((None, None) + zp.shape[2:], lambda ci, bi: (bi, ci, 0, 0)), nat, nat,
                  pl.BlockSpec((None,) + kf.shape[1:], lambda ci, bi: (ci, 0, 0, 0)),
                  const(m1.shape), const(f1.shape), const(g1.shape), const(g2.shape),
                  pl.BlockSpec((1, 128), lambda ci, bi: (0, ci))],
        out_specs=nat,
        out_shape=jax.ShapeDtypeStruct((b, s, c), BF16),
        scratch_shapes=[pltpu.VMEM((2, g1.shape[0] * (FFT_N1 + FFT_PAD), 128), F32),
                        pltpu.VMEM((2, FFT_N1 * (g1.shape[0] + FFT_PAD), 128), F32),
                        pltpu.VMEM((s, 128), F32)],
        compiler_params=_params("parallel", "parallel"), name="hy_fft_conv",
    )(zp, z, x1c, kf, m1, f1, g1, g2, skip.reshape(1, c).astype(F32))


def _ml_prep_body(u_ref, w_ref, b_ref, o_ref, *, scale, transpose):
    y = _conv3(u_ref, w_ref, b_ref)
    y = y * _sigmoid(y) * scale
    o_ref[...] = (y.T if transpose else y).astype(o_ref.dtype)


def ml_prep(u, conv_w, conv_b, col0, ncols, scale, transpose, tc=256):
    b, s, _ = u.shape
    w = conv_w.shape[1]
    c0 = col0 // tc
    if transpose:
        out_spec = pl.BlockSpec((None, tc, s), lambda bi, c: (bi, c, 0))
        out_shape = jax.ShapeDtypeStruct((b, ncols, s), BF16)
    else:
        out_spec = pl.BlockSpec((None, s, tc), lambda bi, c: (bi, 0, c))
        out_shape = jax.ShapeDtypeStruct((b, s, ncols), BF16)
    return pl.pallas_call(
        functools.partial(_ml_prep_body, scale=scale, transpose=transpose), grid=(b, ncols // tc),
        in_specs=[pl.BlockSpec((None, s, tc), lambda bi, c: (bi, 0, c0 + c)),
                  pl.BlockSpec((3, tc), lambda bi, c: (0, c0 + c)),
                  pl.BlockSpec((1, tc), lambda bi, c: (0, c0 + c))],
        out_specs=out_spec, out_shape=out_shape,
        compiler_params=_params("parallel", "parallel"), name="ml_prep",
    )(u, conv_w, conv_b.reshape(1, w))


def _log_sigmoid(x):
    return jnp.minimum(x, 0.0) - jnp.log(1.0 + jnp.exp(-jnp.abs(x)))


def _dot_split(a, b, a_is_f32):
    x = a if a_is_f32 else b
    hi = x.astype(BF16)
    lo = (x - hi.astype(F32)).astype(BF16)
    if a_is_f32:
        return (jnp.dot(hi, b, preferred_element_type=F32) + jnp.dot(lo, b, preferred_element_type=F32))
    return (jnp.dot(a, hi, preferred_element_type=F32) + jnp.dot(a, lo, preferred_element_type=F32))


def _mlstm_chain(q, kt, v, bc, br, li_r, total, mask, c_ref, m_ref, idx):
    dv = v.shape[1] - 128
    c_st = c_ref[idx]
    m_st = m_ref[idx:idx + 1, 0:1]
    key_row = li_r - br
    inter = bc + m_st
    m_t = jnp.maximum(inter, bc + jnp.max(jnp.where(mask, key_row, NEG_BIG), axis=-1, keepdims=True))
    w_intra = jnp.exp(jnp.where(mask, (bc - m_t) + key_row, NEG_BIG))
    w_inter = jnp.exp(inter - m_t)
    sc = jnp.dot(q, kt, preferred_element_type=F32) * w_intra
    q_inter = (q.astype(F32) * w_inter).astype(BF16)
    both = jnp.dot(jnp.concatenate([q_inter, sc.astype(BF16)], axis=1),
                   jnp.concatenate([c_st.astype(BF16), v], axis=0),
                   preferred_element_type=F32)
    den = both[:, dv:dv + 1]
    h = both[:, :dv] / jnp.maximum(jnp.abs(den), jnp.exp(-m_t))
    g_row = total + key_row
    m_next = jnp.maximum(total + m_st, jnp.max(g_row, axis=-1, keepdims=True))
    a_prev = jnp.exp(total + m_st - m_next)
    kwt = (kt.astype(F32) * jnp.exp(g_row - m_next)).astype(BF16)
    c_ref[idx] = a_prev * c_st + jnp.dot(kwt, v, preferred_element_type=F32)
    m_ref[idx:idx + 1, :] = jnp.broadcast_to(m_next, (1, m_ref.shape[1]))
    return h


def _mlstm_body(qf_ref, ktf_ref, vf_ref, gcf_ref, grf_ref, qb_ref, ktb_ref, vb_ref, gcb_ref, grb_ref,
                hf_ref, hb_ref, c_ref, m_ref, *, heads, dk, dv):
    @pl.when(pl.program_id(1) == 0)
    def _():
        c_ref[...] = jnp.zeros_like(c_ref)
        m_ref[...] = jnp.zeros_like(m_ref)

    lc = qf_ref.shape[1]
    ones_blk = jnp.where(lax.broadcasted_iota(I32, (lc, 128), 1) == 0, 1.0, 0.0).astype(BF16)
    t_i = lax.broadcasted_iota(I32, (lc, lc), 0)
    s_i = lax.broadcasted_iota(I32, (lc, lc), 1)
    lower = s_i <= t_i
    upper = s_i >= t_i
    ltri = jnp.where(lower, 1.0, 0.0).astype(BF16)
    utri = jnp.where(upper, 1.0, 0.0).astype(BF16)

    for bb in range(qf_ref.shape[0]):
        for direction, (q_ref, kt_ref, v_ref, gc_ref, gr_ref, h_ref) in enumerate(
                ((qf_ref, ktf_ref, vf_ref, gcf_ref, grf_ref, hf_ref),
                 (qb_ref, ktb_ref, vb_ref, gcb_ref, grb_ref, hb_ref))):
            fwd = direction == 0
            gc = gc_ref[bb]
            gr = gr_ref[bb]
            lfc, lfr = _log_sigmoid(gc), _log_sigmoid(gr)
            cum_c = _dot_split(ltri if fwd else utri, lfc, a_is_f32=False)
            cum_r = _dot_split(lfr, utri if fwd else ltri, a_is_f32=True)
            for hd in range(heads):
                gi = (0 if fwd else 2) * heads + hd
                gf = (1 if fwd else 3) * heads + hd
                bc, br = cum_c[:, gf:gf + 1], cum_r[gf:gf + 1, :]
                total = br[:, lc - 1:lc] if fwd else br[:, 0:1]
                q = q_ref[bb, :, hd * dk:(hd + 1) * dk]
                kt = kt_ref[bb, hd * dk:(hd + 1) * dk, :]
                v = jnp.concatenate([v_ref[bb, :, hd * dv:(hd + 1) * dv], ones_blk], axis=1)
                h = _mlstm_chain(q, kt, v, bc, br, gr[gi:gi + 1, :], total,
                                 lower if fwd else upper, c_ref, m_ref, (bb * 2 + direction) * heads + hd)
                h_ref[bb, :, hd * dv:(hd + 1) * dv] = h.astype(h_ref.dtype)


def mlstm_scan(q, kt, u, v_blk, gcol, grow):
    b, s, w = q.shape
    heads = ML_HEADS
    dk = w // heads
    dv = 2 * dk
    lc = ML_CHUNK
    nc = s // lc
    ng = gcol.shape[-1]

    def fw(bi, j):
        return j

    def bw(bi, j):
        return nc - 1 - j

    nbs = ML_BATCH_PER_STEP if b % ML_BATCH_PER_STEP == 0 else 1

    def specs(pos):
        return [pl.BlockSpec((nbs, lc, w), lambda bi, j: (bi, pos(bi, j), 0)),
                pl.BlockSpec((nbs, w, lc), lambda bi, j: (bi, 0, pos(bi, j))),
                pl.BlockSpec((nbs, lc, heads * dv), lambda bi, j: (bi, pos(bi, j), v_blk)),
                pl.BlockSpec((nbs, lc, ng), lambda bi, j: (bi, pos(bi, j), 0)),
                pl.BlockSpec((nbs, ng, lc), lambda bi, j: (bi, 0, pos(bi, j)))]

    hshape = jax.ShapeDtypeStruct((b, s, heads * dv), BF16)
    return pl.pallas_call(
        functools.partial(_mlstm_body, heads=heads, dk=dk, dv=dv), grid=(b // nbs, nc),
        in_specs=specs(fw) + specs(bw),
        out_specs=[pl.BlockSpec((nbs, lc, heads * dv), lambda bi, j: (bi, j, 0)),
                   pl.BlockSpec((nbs, lc, heads * dv), lambda bi, j: (bi, nc - 1 - j, 0))],
        out_shape=[hshape, hshape],
        scratch_shapes=[pltpu.VMEM((nbs * 2 * heads, dk, dv + 128), F32),
                        pltpu.VMEM((nbs * 2 * heads, 128), F32)],
        compiler_params=_params("parallel", "arbitrary"), name="mlstm_scan",
    )(q, kt, u, gcol, grow, q, kt, u, gcol, grow)


def _mlstm_out_body(res_ref, hf_ref, hb_ref, o_ref, g_ref, w_ref, out_ref, *, heads):
    hs = hf_ref[...].astype(F32) + hb_ref[...].astype(F32)
    dv = hs.shape[1] // heads
    g = g_ref[...]
    parts = []
    for hd in range(heads):
        seg = hs[:, hd * dv:(hd + 1) * dv]
        ms = jnp.mean(seg * seg, axis=-1, keepdims=True)
        parts.append(seg * lax.rsqrt(ms + EPS) * g[:, hd * dv:(hd + 1) * dv])
    a = jnp.concatenate(parts, axis=-1) * _sigmoid(o_ref[...].astype(F32))
    out_ref[...] = res_ref[...] + jnp.dot(a.astype(BF16), w_ref[...], preferred_element_type=F32)


def mlstm_out(res, hf, hb, u2d, o_blk, gain, w_out, tm=1024):
    t, d = res.shape
    row = lambda i: (i, 0)
    return pl.pallas_call(
        functools.partial(_mlstm_out_body, heads=ML_HEADS), grid=(t // tm,),
        in_specs=[pl.BlockSpec((tm, d), row), pl.BlockSpec((tm, d), row), pl.BlockSpec((tm, d), row),
                  pl.BlockSpec((tm, d), lambda i: (i, o_blk)), pl.BlockSpec((1, d), lambda i: (0, 0)),
                  pl.BlockSpec((d, d), lambda i: (0, 0))],
        out_specs=pl.BlockSpec((tm, d), row), out_shape=jax.ShapeDtypeStruct((t, d), F32),
        compiler_params=_params("parallel"), name="mlstm_out",
    )(res, hf, hb, u2d, gain.reshape(1, d), w_out)


ROUTE_ROWS = 128
EXPERT_ROW0 = 8


def _router_body(x_ref, g_ref, wt_ref, b_ref, o_ref, cnt_ref, run_ref):
    @pl.when(pl.program_id(0) == 0)
    def _():
        run_ref[...] = jnp.zeros_like(run_ref)

    x = x_ref[...]
    ms = jnp.mean(x * x, axis=-1, keepdims=True)
    xn = x * lax.rsqrt(ms + EPS) * g_ref[...]
    wt = wt_ref[...]
    w_hi, x_hi = wt.astype(BF16), xn.astype(BF16)
    w_lo, x_lo = (wt - w_hi.astype(F32)).astype(BF16), (xn - x_hi.astype(F32)).astype(BF16)
    nt = (((1,), (1,)), ((), ()))
    logit = (lax.dot_general(w_hi, x_hi, nt, preferred_element_type=F32)
             + lax.dot_general(w_lo, x_hi, nt, preferred_element_type=F32)
             + lax.dot_general(w_hi, x_lo, nt, preferred_element_type=F32)) + b_ref[...]
    rows = [logit[r:r + 1, :] for r in range(EXPERT_ROW0 + N_EXPERTS)]
    g_best, g_idx = rows[0], jnp.zeros_like(rows[0])
    for gi in range(1, N_GROUPS):
        better = rows[gi] > g_best
        g_best = jnp.where(better, rows[gi], g_best)
        g_idx = jnp.where(better, float(gi), g_idx)
    g_den = sum(jnp.exp(rows[gi] - g_best) for gi in range(N_GROUPS))
    g_w = 1.0 / g_den
    sel = []
    for e in range(EXPERTS_PER_GROUP):
        v = rows[EXPERT_ROW0 + e]
        for gi in range(1, N_GROUPS):
            v = jnp.where(g_idx == float(gi), rows[EXPERT_ROW0 + gi * EXPERTS_PER_GROUP + e], v)
        sel.append(v)
    v1, i1 = sel[0], jnp.zeros_like(sel[0])
    for e in range(1, EXPERTS_PER_GROUP):
        better = sel[e] > v1
        v1 = jnp.where(better, sel[e], v1)
        i1 = jnp.where(better, float(e), i1)
    v2, i2 = jnp.full_like(v1, -jnp.inf), jnp.zeros_like(v1)
    for e in range(EXPERTS_PER_GROUP):
        better = (sel[e] > v2) & (i1 != float(e))
        v2 = jnp.where(better, sel[e], v2)
        i2 = jnp.where(better, float(e), i2)
    e21 = jnp.exp(v2 - v1)
    gate1 = g_w / (1.0 + e21)
    gate2 = gate1 * e21
    base = g_idx * float(EXPERTS_PER_GROUP)
    e1, e2 = base + i1, base + i2
    tm = e1.shape[1]
    erow = lax.broadcasted_iota(I32, (N_EXPERTS, tm), 0).astype(F32)
    oh1 = jnp.where(erow == e1, 1.0, 0.0)
    oh2 = jnp.where(erow == e2, 1.0, 0.0)
    cnt = oh1 + oh2
    earlier = jnp.where(lax.broadcasted_iota(I32, (tm, tm), 0) < lax.broadcasted_iota(I32, (tm, tm), 1),
                        1.0, 0.0).astype(BF16)
    run = run_ref[...]
    pos = run[:, 0:1] + jnp.dot(cnt.astype(BF16), earlier, preferred_element_type=F32)
    rank1 = jnp.sum(oh1 * pos, axis=0, keepdims=True)
    rank2 = jnp.sum(oh2 * pos, axis=0, keepdims=True)
    run = run + jnp.sum(cnt, axis=1, keepdims=True)
    run_ref[...] = run
    cnt_ref[...] = run
    zero = jnp.zeros_like(v1)
    o_ref[...] = jnp.concatenate([e1, e2, gate1, gate2, rank1, rank2, zero, zero], axis=0)


def moe_router(x2d, g, wg, bg, we, be, tm=512):
    t, d = x2d.shape
    wt = jnp.zeros((ROUTE_ROWS, d), F32).at[:N_GROUPS].set(wg.T).at[EXPERT_ROW0:EXPERT_ROW0 + N_EXPERTS].set(we.T)
    bias = jnp.zeros((ROUTE_ROWS, 1), F32).at[:N_GROUPS, 0].set(bg).at[EXPERT_ROW0:EXPERT_ROW0 + N_EXPERTS, 0].set(be)
    return pl.pallas_call(
        _router_body, grid=(t // tm,),
        in_specs=[pl.BlockSpec((tm, d), lambda i: (i, 0)), pl.BlockSpec((1, d), lambda i: (0, 0)),
                  pl.BlockSpec((ROUTE_ROWS, d), lambda i: (0, 0)), pl.BlockSpec((ROUTE_ROWS, 1), lambda i: (0, 0))],
        out_specs=[pl.BlockSpec((8, tm), lambda i: (0, i)), pl.BlockSpec((N_EXPERTS, 128), lambda i: (0, 0))],
        out_shape=[jax.ShapeDtypeStruct((8, t), F32), jax.ShapeDtypeStruct((N_EXPERTS, 128), F32)],
        scratch_shapes=[pltpu.VMEM((N_EXPERTS, 128), F32)],
        compiler_params=_params("arbitrary"), name="moe_router",
    )(x2d, g.reshape(1, d), wt, bias)


def _dispatch_body(dest_ref, zblk_ref, x_ref, g_ref, xs_ref, buf_ref, sem_ref, *, td):
    i = pl.program_id(0)
    n = pl.num_programs(0)
    slot = i % 2

    @pl.when(i == 0)
    def _():
        buf_ref[1] = jnp.zeros(buf_ref.shape[1:], buf_ref.dtype)

        def zero_copy(j):
            return pltpu.make_async_copy(buf_ref.at[1], xs_ref.at[pl.ds(zblk_ref[j] * td, td), :], sem_ref.at[1])

        def start(j, c):
            @pl.when(zblk_ref[j] >= 0)
            def _():
                zero_copy(j).start()
            return c

        def wait(j, c):
            @pl.when(zblk_ref[j] >= 0)
            def _():
                zero_copy(j).wait()
            return c
        lax.fori_loop(0, zblk_ref.shape[0], start, 0)
        lax.fori_loop(0, zblk_ref.shape[0], wait, 0)

    def row_copy(r, kk):
        return pltpu.make_async_copy(buf_ref.at[slot, pl.ds(r, 1), :],
                                     xs_ref.at[pl.ds(dest_ref[(i * td + r) * 2 + kk], 1), :],
                                     sem_ref.at[slot])

    def wait_buffer(sl):
        for _ in range(2):
            pltpu.make_async_copy(buf_ref.at[sl], xs_ref.at[pl.ds(0, td), :], sem_ref.at[sl]).wait()

    @pl.when(i >= 2)
    def _():
        wait_buffer(slot)

    x = x_ref[...]
    ms = jnp.mean(x * x, axis=-1, keepdims=True)
    buf_ref[slot] = x * lax.rsqrt(ms + EPS) * g_ref[...]

    for r in range(td):
        row_copy(r, 0).start(priority=0)
        row_copy(r, 1).start(priority=1)

    @pl.when(i == n - 1)
    def _():
        @pl.when(n >= 2)
        def _():
            wait_buffer(1 - slot)
        wait_buffer(slot)


def moe_dispatch(x2d, g, dest, zero_blk, cap, td):
    t, d = x2d.shape
    grid_spec = pltpu.PrefetchScalarGridSpec(
        num_scalar_prefetch=2, grid=(t // td,),
        in_specs=[pl.BlockSpec((td, d), lambda i, dest, zb: (i, 0)),
                  pl.BlockSpec((1, d), lambda i, dest, zb: (0, 0))],
        out_specs=pl.BlockSpec(memory_space=pl.ANY),
        scratch_shapes=[pltpu.VMEM((2, td, d), F32), pltpu.SemaphoreType.DMA((2,))])
    return pl.pallas_call(
        functools.partial(_dispatch_body, td=td), grid_spec=grid_spec,
        out_shape=jax.ShapeDtypeStruct((cap, d), F32),
        compiler_params=_params("arbitrary", disable_bounds_checks=True),
        name="moe_dispatch",
    )(dest, zero_blk, x2d, g.reshape(1, d))


def _expert_body(blk_e_ref, nused_ref, xs_ref, w1_ref, w3_ref, w2_ref, ys_ref, w1b_ref, w3b_ref, w2b_ref):
    i = pl.program_id(0)
    nused = nused_ref[0]
    last = nused - 1
    cur = blk_e_ref[jnp.minimum(i, last)]
    prev = blk_e_ref[jnp.minimum(jnp.maximum(i - 1, 0), last)]

    @pl.when((i == 0) | (cur != prev))
    def _():
        w1b_ref[...] = w1_ref[...].astype(BF16)
        w3b_ref[...] = w3_ref[...].astype(BF16)
        w2b_ref[...] = w2_ref[...].astype(BF16)

    @pl.when(i < nused)
    def _():
        x = xs_ref[...].astype(BF16)
        h1 = jnp.dot(x, w1b_ref[...], preferred_element_type=F32)
        h3 = jnp.dot(x, w3b_ref[...], preferred_element_type=F32)
        hid = (h1 * _sigmoid(h1) * h3).astype(BF16)
        ys_ref[...] = jnp.dot(hid, w2b_ref[...], preferred_element_type=F32)

    @pl.when(i >= nused)
    def _():
        ys_ref[...] = jnp.zeros_like(ys_ref)


def moe_experts(xs, blk_e, nused, w1, w3, w2, layer, tm):
    cap, d = xs.shape
    de = w1.shape[-1]

    def blk(i, be, nu):
        return jnp.minimum(i, nu[0] - 1)

    grid_spec = pltpu.PrefetchScalarGridSpec(
        num_scalar_prefetch=2, grid=(cap // tm,),
        in_specs=[pl.BlockSpec((tm, d), lambda i, be, nu: (blk(i, be, nu), 0)),
                  pl.BlockSpec((None, None, d, de), lambda i, be, nu: (layer, be[blk(i, be, nu)], 0, 0)),
                  pl.BlockSpec((None, None, d, de), lambda i, be, nu: (layer, be[blk(i, be, nu)], 0, 0)),
                  pl.BlockSpec((None, None, de, d), lambda i, be, nu: (layer, be[blk(i, be, nu)], 0, 0))],
        out_specs=pl.BlockSpec((tm, d), lambda i, be, nu: (i, 0)),
        scratch_shapes=[pltpu.VMEM((d, de), BF16), pltpu.VMEM((d, de), BF16), pltpu.VMEM((de, d), BF16)])
    return pl.pallas_call(
        _expert_body, grid_spec=grid_spec, out_shape=jax.ShapeDtypeStruct((cap, d), F32),
        compiler_params=_params("arbitrary"), name="moe_experts",
    )(blk_e, nused, xs, w1, w3, w2)


def _combine_body(dest_ref, x_ref, gate_ref, ys_ref, o_ref, buf_ref, sem_ref, *, tc):
    i = pl.program_id(0)
    n = pl.num_programs(0)
    slot = i % 2

    def row_copy(step, sl, r, kk):
        return pltpu.make_async_copy(ys_ref.at[pl.ds(dest_ref[(step * tc + r) * 2 + kk], 1), :],
                                     buf_ref.at[sl, kk, pl.ds(r, 1), :], sem_ref.at[sl])

    def issue_step(step, sl):
        for r in range(tc):
            row_copy(step, sl, r, 0).start(priority=0)
            row_copy(step, sl, r, 1).start(priority=1)

    @pl.when(i == 0)
    def _():
        issue_step(0, 0)

    @pl.when(i + 1 < n)
    def _():
        issue_step(i + 1, 1 - slot)

    for kk in range(2):
        pltpu.make_async_copy(ys_ref.at[pl.ds(0, tc), :], buf_ref.at[slot, kk], sem_ref.at[slot]).wait()

    gate = gate_ref[...]
    o_ref[...] = x_ref[...] + gate[:, 0:1] * buf_ref[slot, 0] + gate[:, 1:2] * buf_ref[slot, 1]


def moe_combine(x2d, gates, ys, dest, tc=256):
    t, d = x2d.shape
    grid_spec = pltpu.PrefetchScalarGridSpec(
        num_scalar_prefetch=1, grid=(t // tc,),
        in_specs=[pl.BlockSpec((tc, d), lambda i, dest: (i, 0)), pl.BlockSpec((tc, 2), lambda i, dest: (i, 0)),
                  pl.BlockSpec(memory_space=pl.ANY)],
        out_specs=pl.BlockSpec((tc, d), lambda i, dest: (i, 0)),
        scratch_shapes=[pltpu.VMEM((2, 2, tc, d), F32), pltpu.SemaphoreType.DMA((2,))])
    return pl.pallas_call(
        functools.partial(_combine_body, tc=tc), grid_spec=grid_spec,
        out_shape=jax.ShapeDtypeStruct((t, d), F32),
        compiler_params=_params("arbitrary", disable_bounds_checks=True), name="moe_combine",
    )(dest, x2d, gates, ys)


def hier_moe_residual(x2d, g, wg, bg, we, be, w1, w3, w2, layer, tm=512):
    t, d = x2d.shape
    route, counts_b = moe_router(x2d, g, wg, bg, we, be)
    eid = route[0:2].T.astype(I32).reshape(-1)
    rank = route[4:6].T.astype(I32).reshape(-1)
    gates = route[2:4].T
    counts = counts_b[:, 0].astype(I32)
    padded = (counts + tm - 1) // tm * tm
    pad_end = jnp.cumsum(padded)
    pad_start = pad_end - padded
    experts = jnp.arange(N_EXPERTS, dtype=I32)
    dest = rank + jnp.sum(jnp.where(eid[:, None] == experts[None, :], pad_start[None, :], 0), axis=1)
    cap = t * 2 + N_EXPERTS * tm
    nblk = cap // tm
    blk_row0 = jnp.arange(nblk, dtype=I32) * tm
    blk_e = jnp.minimum(jnp.sum((pad_end[None, :] <= blk_row0[:, None]).astype(I32), axis=1), N_EXPERTS - 1)
    nused = (pad_end[-1:] // tm).astype(I32)
    last_blk = jnp.where(padded > 0, pad_end // tm - 1, -1)
    tail_blk = jnp.where(nused[0] + experts < nblk, nused[0] + experts, -1)
    zero_blk = jnp.concatenate([last_blk, tail_blk]).astype(I32)
    xs = moe_dispatch(x2d, g, dest.astype(I32), zero_blk, cap, tm)
    ys = moe_experts(xs, blk_e, nused, w1, w3, w2, layer, tm)
    return moe_combine(x2d, gates, ys, dest.astype(I32))


def _even_layer(x2d, b, s, layer, mix_g, w_in, q_norm, k_norm, lam_q1, lam_k1, lam_q2, lam_k2, subln,
                hy_conv_w, hy_conv_b, f_w1, f_b1, f_freq, f_w2, f_b2, f_w3, hy_skip, w_out):
    d = x2d.shape[1]
    d_att = d // 2
    (u2d,) = norm_matmul(x2d, mix_g, [w_in.astype(BF16)], [BF16])
    u = u2d.reshape(b, s, -1)
    lambda_init = 0.8 - 0.6 * math.exp(-0.3 * layer)
    lam = jnp.exp(jnp.sum(lam_q1 * lam_k1)) - jnp.exp(jnp.sum(lam_q2 * lam_k2)) + lambda_init
    dk = q_norm.shape[0]
    qt = head_prep(u, 0, q_norm, scale=dk ** -0.5 * math.log2(math.e), transpose=True)
    kp = head_prep(u, ATT_HEADS, k_norm)
    vt = head_prep(u, 2 * ATT_HEADS, transpose=True)
    y_att = diff_attention(qt, kp, vt, lam, subln, lambda_init)
    z, x1c, zp = hy_prep(u, 3 * d_att, hy_conv_w, hy_conv_b)
    tables = fft_tables(s)
    kf = filter_spectrum(hyena_filter(s, f_w1, f_b1, f_freq, f_w2, f_b2, f_w3), tables[0], tables[1])
    y_hy = hy_fft_conv(zp, z, x1c, kf, tables, hy_skip)
    w_out = w_out.astype(BF16)
    return matmul_residual(x2d, [y_att.reshape(b * s, -1), y_hy.reshape(b * s, -1)],
                           [w_out[:d_att], w_out[d_att:]])


def _odd_layer(x2d, b, s, mix_g, w_in, conv_w, conv_b, gate_b, out_norm, w_out):
    d = x2d.shape[1]
    qk_w = conv_w.shape[1]
    main_w = qk_w + 2 * d
    ng = 4 * ML_HEADS
    w_gate = jnp.zeros((d, 128), F32).at[:, :ng].set(w_in[:, main_w:]).astype(BF16)
    u2d, ug = norm_matmul(x2d, mix_g, [w_in[:, :main_w].astype(BF16), w_gate], [BF16, F32])
    u = u2d.reshape(b, s, main_w)
    gcol = (ug[:, :ng] + gate_b).reshape(b, s, ng)
    grow = jnp.swapaxes(gcol, 1, 2)
    dk = qk_w // (2 * ML_HEADS)
    q = ml_prep(u, conv_w, conv_b, 0, qk_w // 2, dk ** -0.5, transpose=False)
    kt = ml_prep(u, conv_w, conv_b, qk_w // 2, qk_w // 2, 1.0, transpose=True)
    hf, hb = mlstm_scan(q, kt, u, qk_w // d, gcol, grow)
    return mlstm_out(x2d, hf.reshape(b * s, d), hb.reshape(b * s, d), u2d, (qk_w + d) // d, out_norm,
                     w_out.astype(BF16))


def kernel(x, mix_norm, ffn_norm, ev_w_in, ev_q_norm, ev_k_norm, ev_lam_q1, ev_lam_k1, ev_lam_q2, ev_lam_k2, ev_subln, ev_hy_conv_w, ev_hy_conv_b, ev_hy_f_w1, ev_hy_f_b1, ev_hy_f_freq, ev_hy_f_w2, ev_hy_f_b2, ev_hy_f_w3, ev_hy_skip, ev_w_out, od_w_in, od_conv_w, od_conv_b, od_gate_b, od_out_norm, od_w_out, moe_wg, moe_bg, moe_we, moe_be, moe_w1, moe_w3, moe_w2):
    b, s, d = x.shape
    depth = mix_norm.shape[0]
    x2d = x.reshape(b * s, d)
    for layer in range(depth):
        j = layer // 2
        if layer % 2 == 0:
            x2d = _even_layer(x2d, b, s, layer, mix_norm[layer], ev_w_in[j], ev_q_norm[j], ev_k_norm[j],
                              ev_lam_q1[j], ev_lam_k1[j], ev_lam_q2[j], ev_lam_k2[j], ev_subln[j],
                              ev_hy_conv_w[j], ev_hy_conv_b[j], ev_hy_f_w1[j], ev_hy_f_b1[j], ev_hy_f_freq[j],
                              ev_hy_f_w2[j], ev_hy_f_b2[j], ev_hy_f_w3[j], ev_hy_skip[j], ev_w_out[j])
        else:
            x2d = _odd_layer(x2d, b, s, mix_norm[layer], od_w_in[j], od_conv_w[j], od_conv_b[j], od_gate_b[j],
                             od_out_norm[j], od_w_out[j])
        x2d = hier_moe_residual(x2d, ffn_norm[layer], moe_wg[layer], moe_bg[layer], moe_we[layer],
                                moe_be[layer], moe_w1, moe_w3, moe_w2, layer)
    return x2d.reshape(b, s, d)
```

```python
import functools
import math

import jax
import jax.numpy as jnp
import numpy as np
from jax import lax
from jax.experimental import pallas as pl
from jax.experimental.pallas import tpu as pltpu

F32 = jnp.float32
BF16 = jnp.bfloat16
I32 = jnp.int32

EPS = 1e-6
ROPE_THETA = 500000.0
ATT_HEADS = 4
ML_HEADS = 4
ML_CHUNK = 128
ML_BATCH_PER_STEP = 2
N_GROUPS = 4
EXPERTS_PER_GROUP = 8
N_EXPERTS = N_GROUPS * EXPERTS_PER_GROUP
HY_EMB_BANDS = 16
HY_MIN_DECAY = math.log(1e-2) / 1.5
HY_MAX_DECAY = math.log(1e-2) / 0.3

V7X_VMEM_BYTES = 64 * 1024 * 1024
VMEM_LIMIT = V7X_VMEM_BYTES - 8 * 1024 * 1024
NEG_BIG = -1e30


def _params(*sem, **kw):
    return pltpu.CompilerParams(dimension_semantics=sem, vmem_limit_bytes=VMEM_LIMIT, **kw)


def _sigmoid(x):
    return 1.0 / (1.0 + jnp.exp(-x))


def _norm_matmul_body(x_ref, g_ref, *refs, n_out, col_chunk):
    w_refs, o_refs = refs[:n_out], refs[n_out:]
    x = x_ref[...]
    ms = jnp.mean(x * x, axis=-1, keepdims=True)
    hn = (x * lax.rsqrt(ms + EPS) * g_ref[...]).astype(BF16)
    for w_ref, o_ref in zip(w_refs, o_refs):
        n = w_ref.shape[1]
        for c in range(0, n, col_chunk):
            ce = min(n, c + col_chunk)
            o_ref[:, c:ce] = jnp.dot(hn, w_ref[:, c:ce], preferred_element_type=F32).astype(o_ref.dtype)


def norm_matmul(x2d, g, ws, out_dtypes, tm=1024):
    t, d = x2d.shape
    in_specs = [pl.BlockSpec((tm, d), lambda i: (i, 0)), pl.BlockSpec((1, d), lambda i: (0, 0))]
    in_specs += [pl.BlockSpec(w.shape, lambda i: (0, 0)) for w in ws]
    out_specs = [pl.BlockSpec((tm, w.shape[1]), lambda i: (i, 0)) for w in ws]
    out_shape = [jax.ShapeDtypeStruct((t, w.shape[1]), dt) for w, dt in zip(ws, out_dtypes)]
    return pl.pallas_call(
        functools.partial(_norm_matmul_body, n_out=len(ws), col_chunk=1024),
        grid=(t // tm,), in_specs=in_specs, out_specs=out_specs, out_shape=out_shape,
        compiler_params=_params("parallel"), name="norm_matmul",
    )(x2d, g.reshape(1, d), *ws)


def _matmul_res_body(res_ref, *refs, n_in):
    a_refs, w_refs, o_ref = refs[:n_in], refs[n_in:2 * n_in], refs[2 * n_in]
    acc = res_ref[...]
    for a_ref, w_ref in zip(a_refs, w_refs):
        acc = acc + jnp.dot(a_ref[...], w_ref[...], preferred_element_type=F32)
    o_ref[...] = acc


def matmul_residual(res, a_list, w_list, tm=1024):
    t, d = res.shape
    in_specs = [pl.BlockSpec((tm, d), lambda i: (i, 0))]
    in_specs += [pl.BlockSpec((tm, a.shape[1]), lambda i: (i, 0)) for a in a_list]
    in_specs += [pl.BlockSpec(w.shape, lambda i: (0, 0)) for w in w_list]
    return pl.pallas_call(
        functools.partial(_matmul_res_body, n_in=len(a_list)),
        grid=(t // tm,), in_specs=in_specs, out_specs=pl.BlockSpec((tm, d), lambda i: (i, 0)),
        out_shape=jax.ShapeDtypeStruct((t, d), F32),
        compiler_params=_params("parallel"), name="matmul_residual",
    )(res, *a_list, *w_list)


def _head_prep_body(u_ref, *refs, dk, rope, transpose):
    o_ref = refs[-1]
    x = u_ref[...].astype(F32)
    if rope:
        g_ref, c_ref, s1_ref, s2_ref = refs[:4]
        lane = lax.broadcasted_iota(I32, x.shape, 1)
        lo = lane < dk
        x2 = x * x
        s_lo = jnp.sum(jnp.where(lo, x2, 0.0), axis=-1, keepdims=True)
        s_hi = jnp.sum(jnp.where(lo, 0.0, x2), axis=-1, keepdims=True)
        ms = jnp.where(lo, s_lo, s_hi) * (1.0 / dk)
        y = x * lax.rsqrt(ms + EPS) * g_ref[...]
        x = y * c_ref[...] + pltpu.roll(y, 120, 1) * s1_ref[...] + pltpu.roll(y, 8, 1) * s2_ref[...]
    if transpose:
        x = x.T
    o_ref[...] = x.astype(o_ref.dtype)


def _rope_lane_tables(seq, dk, rope_dim, scale):
    half = rope_dim // 2
    f32 = np.float32
    inv_freq = (f32(1.0) / (f32(ROPE_THETA) ** (np.arange(0, rope_dim, 2, dtype=f32) / f32(rope_dim)))).astype(f32)
    ang = np.arange(seq, dtype=f32)[:, None] * inv_freq[None, :]
    cos, sin = np.cos(ang), np.sin(ang)
    d = np.arange(2 * dk) % dk
    fi = d % half
    c_tab = np.where(d[None, :] < rope_dim, cos[:, fi], 1.0)
    s1_tab = np.where(d[None, :] < half, -sin[:, fi], 0.0)
    s2_tab = np.where((d[None, :] >= half) & (d[None, :] < rope_dim), sin[:, fi], 0.0)
    return jnp.asarray((np.stack([c_tab, s1_tab, s2_tab]) * scale).astype(f32))


def head_prep(u, blk0, norm_gain=None, scale=1.0, transpose=False):
    b, s, _ = u.shape
    h = ATT_HEADS
    rope = norm_gain is not None
    in_specs = [pl.BlockSpec((None, s, 128), lambda c, bi: (bi, 0, blk0 + c))]
    args = [u]
    dk = 64
    if rope:
        dk = norm_gain.shape[0]
        assert 2 * dk == 128 and dk // 4 == 16, "rope roll shifts assume 64-wide components, 16 rotary dims"
        tabs = _rope_lane_tables(s, dk, dk // 4, scale)
        in_specs += [pl.BlockSpec((1, 128), lambda c, bi: (0, 0))] + [pl.BlockSpec((s, 128), lambda c, bi: (0, 0))] * 3
        args += [jnp.tile(norm_gain, 2).reshape(1, 128).astype(F32), tabs[0], tabs[1], tabs[2]]
    if transpose:
        out_spec = pl.BlockSpec((None, 128, s), lambda c, bi: (bi, c, 0))
        out_shape = jax.ShapeDtypeStruct((b, h * 128, s), BF16)
    else:
        out_spec = pl.BlockSpec((None, s, 128), lambda c, bi: (bi, 0, c))
        out_shape = jax.ShapeDtypeStruct((b, s, h * 128), BF16)
    return pl.pallas_call(
        functools.partial(_head_prep_body, dk=dk, rope=rope, transpose=transpose), grid=(h, b),
        in_specs=in_specs, out_specs=out_spec, out_shape=out_shape,
        compiler_params=_params("parallel", "parallel"), name="head_prep",
    )(*args)


def _attn_body(lam_ref, qt_ref, k_ref, vt_ref, g_ref, o_ref, *, tq, dk, post_scale, n_split):
    lam = lam_ref[0, 0]
    th = tq // n_split
    for part in range(n_split):
        qt = qt_ref[:, part * th:(part + 1) * th]
        row = lax.broadcasted_iota(I32, qt.shape, 0)
        zero = jnp.zeros_like(qt)
        qq = jnp.concatenate([jnp.where(row < dk, qt, zero), jnp.where(row < dk, zero, qt)], axis=1)
        st = jnp.dot(k_ref[...], qq, preferred_element_type=F32)
        m = jnp.max(st, axis=0, keepdims=True)
        p = jnp.exp2(st - m)
        r = 1.0 / jnp.sum(p, axis=0, keepdims=True)
        ot = jnp.dot(vt_ref[...], p.astype(BF16), preferred_element_type=F32)
        o = (ot[:, :th] * r[:, :th] - ot[:, th:] * (lam * r[:, th:])).T
        ms = jnp.mean(o * o, axis=-1, keepdims=True)
        o_ref[part * th:(part + 1) * th, :] = (o * lax.rsqrt(ms + EPS) * g_ref[...] * post_scale).astype(o_ref.dtype)


def diff_attention(qt, k, vt, lam, subln, lambda_init, tq=512):
    b, s, _ = k.shape
    h = ATT_HEADS
    return pl.pallas_call(
        functools.partial(_attn_body, tq=tq, dk=64, post_scale=1.0 - lambda_init, n_split=1),
        grid=(b, h, s // tq),
        in_specs=[pl.BlockSpec(memory_space=pltpu.SMEM),
                  pl.BlockSpec((None, 128, tq), lambda bi, hi, i: (bi, hi, i)),
                  pl.BlockSpec((None, s, 128), lambda bi, hi, i: (bi, 0, hi)),
                  pl.BlockSpec((None, 128, s), lambda bi, hi, i: (bi, hi, 0)),
                  pl.BlockSpec((1, 128), lambda bi, hi, i: (0, 0))],
        out_specs=pl.BlockSpec((None, tq, 128), lambda bi, hi, i: (bi, i, hi)),
        out_shape=jax.ShapeDtypeStruct((b, s, h * 128), BF16),
        compiler_params=_params("parallel", "parallel", "parallel"), name="diff_attention",
    )(lam.reshape(1, 1).astype(F32), qt, k, vt, subln.reshape(1, 128).astype(F32))


def _conv3(u_ref, w_ref, b_ref):
    x = u_ref[...].astype(F32)
    s = x.shape[0]
    row = lax.broadcasted_iota(I32, x.shape, 0)
    x_prev = jnp.where(row == 0, 0.0, pltpu.roll(x, 1, 0))
    x_next = jnp.where(row == s - 1, 0.0, pltpu.roll(x, s - 1, 0))
    w = w_ref[...]
    return b_ref[...] + x_prev * w[0:1] + x * w[1:2] + x_next * w[2:3]


FFT_N1 = 64
FFT_UNROLL = 64
FFT_PAD = 8


def _hy_prep_body(x1_ref, x2_ref, v_ref, w1_ref, w2_ref, wv_ref, b1_ref, b2_ref, bv_ref, z_ref, x1c_ref, zp_ref):
    x1c_ref[...] = _conv3(x1_ref, w1_ref, b1_ref).astype(x1c_ref.dtype)
    z = _conv3(v_ref, wv_ref, bv_ref) * _conv3(x2_ref, w2_ref, b2_ref)
    z_ref[...] = z.astype(z_ref.dtype)
    nb = z.shape[0] // FFT_N1
    zp_ref[...] = jnp.zeros_like(zp_ref)
    for n2 in range(nb):
        for ci in range(z.shape[1] // 128):
            zp_ref[ci, pl.ds(n2, FFT_N1, stride=nb + FFT_PAD), :] = (
                z[n2 * FFT_N1:(n2 + 1) * FFT_N1, ci * 128:(ci + 1) * 128])


def hy_prep(u, col0, conv_w, conv_b, tc=256):
    b, s, _ = u.shape
    d_hy = conv_w.shape[1] // 3
    nct = d_hy // tc
    blk0 = col0 // tc
    sp = FFT_N1 * (s // FFT_N1 + FFT_PAD)

    def uspec(part):
        return pl.BlockSpec((None, s, tc), lambda bi, c: (bi, 0, blk0 + part * nct + c))

    def wspec(part, rows):
        return pl.BlockSpec((rows, tc), lambda bi, c: (0, part * nct + c))

    ospec = pl.BlockSpec((None, s, tc), lambda bi, c: (bi, 0, c))
    return pl.pallas_call(
        _hy_prep_body, grid=(b, nct),
        in_specs=[uspec(0), uspec(1), uspec(2), wspec(0, 3), wspec(1, 3), wspec(2, 3),
                  wspec(0, 1), wspec(1, 1), wspec(2, 1)],
        out_specs=[ospec, ospec, pl.BlockSpec((None, tc // 128, sp, 128), lambda bi, c: (bi, c, 0, 0))],
        out_shape=[jax.ShapeDtypeStruct((b, s, d_hy), BF16), jax.ShapeDtypeStruct((b, s, d_hy), BF16),
                   jax.ShapeDtypeStruct((b, d_hy // 128, sp, 128), F32)],
        compiler_params=_params("parallel", "parallel"), name="hy_prep",
    )(u, u, u, conv_w, conv_w, conv_w, conv_b.reshape(1, -1), conv_b.reshape(1, -1), conv_b.reshape(1, -1))


def hyena_filter(length, w1, b1, freq, w2, b2, w3):
    d_hy = w3.shape[1] // 2
    f32 = np.float32
    t = np.linspace(0.0, 1.0, length, dtype=f32)[:, None]
    bands = np.linspace(1e-4, HY_EMB_BANDS - 1, HY_EMB_BANDS, dtype=f32)[None, :]
    ang = (f32(2.0 * math.pi / length) * np.arange(length, dtype=f32)[:, None] * bands).astype(f32)
    z = jnp.asarray(np.concatenate([t, np.cos(ang), -np.sin(ang)], axis=-1).astype(f32))
    deltas = np.abs(np.linspace(HY_MIN_DECAY, HY_MAX_DECAY, d_hy, dtype=f32))
    decay = jnp.asarray(np.exp(-t * deltas[None, :]).astype(f32))
    hp = lax.Precision.HIGHEST
    hdn = jnp.sin(freq * (jnp.dot(z, w1, precision=hp) + b1))
    hdn = jnp.sin(freq * (jnp.dot(hdn, w2, precision=hp) + b2))
    filt = jnp.dot(hdn, w3, precision=hp)
    h_fwd = filt[:, :d_hy] * decay
    h_bwd = filt[:, d_hy:] * decay
    h_fwd = h_fwd.at[0].add(h_bwd[0])
    h_bwd = h_bwd.at[0].set(0.0)
    norm = jnp.sum(jnp.abs(h_fwd), axis=0, keepdims=True) + jnp.sum(jnp.abs(h_bwd), axis=0, keepdims=True) + EPS
    return jnp.concatenate([h_fwd / norm, h_bwd / norm], axis=1)


def fft_tables(length):
    n = 2 * length
    n1c, n2c, nb = FFT_N1, n // FFT_N1, length // FFT_N1
    unit = 2.0 * math.pi / n
    i1 = np.arange(n1c, dtype=np.int64)
    i2 = np.arange(n2c, dtype=np.int64)
    ib = np.arange(nb, dtype=np.int64)
    samp = i1[:, None, None] + n1c * ib[None, None, :]
    ang = ((i2[None, :, None] * samp) % n) * unit
    m1 = np.concatenate([np.cos(ang), -np.sin(ang)], axis=1)
    ang = ((i1[:, None] * i1[None, :]) % n1c) * (2.0 * math.pi / n1c)
    c, s = np.cos(ang), np.sin(ang)
    f1 = np.concatenate([np.concatenate([c, s], axis=1), np.concatenate([-s, c], axis=1)], axis=0)
    freq = n2c * i1[None, None, :] + i2[:, None, None]
    ang = ((i1[None, :, None] * freq) % n) * unit
    c, s = np.cos(ang), np.sin(ang)
    g1 = np.concatenate([np.concatenate([c, -s], axis=2), np.concatenate([s, c], axis=2)], axis=1)
    ang = ((ib[:, None] * i2[None, :]) % n2c) * (2.0 * math.pi / n2c)
    g2 = np.concatenate([np.cos(ang), -np.sin(ang)], axis=1) * (1.0 / n)
    return tuple(jnp.asarray(t.astype(np.float32).astype(BF16)) for t in (m1, f1, g1, g2))


def _fft_stage1(xp_ref, m1_ref, p_ref):
    nb = m1_ref.shape[2]
    n2c = m1_ref.shape[1] // 2

    def body(n1, c):
        x = xp_ref[pl.ds(pl.multiple_of(n1 * (nb + FFT_PAD), 8), nb), :].astype(BF16)
        a = jnp.dot(m1_ref[n1], x, preferred_element_type=F32)
        p_ref[0, pl.ds(n1, n2c, stride=FFT_N1 + FFT_PAD), :] = a[:n2c]
        p_ref[1, pl.ds(n1, n2c, stride=FFT_N1 + FFT_PAD), :] = a[n2c:]
        return c
    lax.fori_loop(0, FFT_N1, body, 0, unroll=FFT_UNROLL)


def _fft_stage2(p_ref, f1_ref, k2):
    r0 = pl.multiple_of(k2 * (FFT_N1 + FFT_PAD), 8)
    slab = jnp.concatenate([p_ref[0, pl.ds(r0, FFT_N1), :], p_ref[1, pl.ds(r0, FFT_N1), :]], axis=0)
    return jnp.dot(f1_ref[...], slab.astype(BF16), preferred_element_type=F32)


def _spectrum_body(xp_ref, m1_ref, f1_ref, o_ref, p_ref):
    _fft_stage1(xp_ref, m1_ref, p_ref)

    def body(k2, c):
        o_ref[k2] = _fft_stage2(p_ref, f1_ref, k2)
        return c
    lax.fori_loop(0, o_ref.shape[0], body, 0, unroll=FFT_UNROLL)


def filter_spectrum(ab, m1, f1):
    length, c2 = ab.shape
    nb = length // FFT_N1
    n2c = m1.shape[1] // 2
    nch = c2 // 128
    abp = jnp.pad(ab.reshape(nb, FFT_N1, nch, 128).transpose(2, 1, 0, 3), ((0, 0), (0, 0), (0, FFT_PAD), (0, 0)))
    abp = abp.reshape(nch, FFT_N1 * (nb + FFT_PAD), 128)
    spec = pl.pallas_call(
        _spectrum_body, grid=(nch,),
        in_specs=[pl.BlockSpec((None,) + abp.shape[1:], lambda c: (c, 0, 0)),
                  pl.BlockSpec(m1.shape, lambda c: (0, 0, 0)), pl.BlockSpec(f1.shape, lambda c: (0, 0))],
        out_specs=pl.BlockSpec((None, n2c, 2 * FFT_N1, 128), lambda c: (c, 0, 0, 0)),
        out_shape=jax.ShapeDtypeStruct((nch, n2c, 2 * FFT_N1, 128), F32),
        scratch_shapes=[pltpu.VMEM((2, n2c * (FFT_N1 + FFT_PAD), 128), F32)],
        compiler_params=_params("parallel"), name="filter_spectrum",
    )(abp, m1, f1)
    fa, fb = spec[:nch // 2], spec[nch // 2:]
    h = FFT_N1
    return jnp.concatenate([fa[:, :, :h] + fb[:, :, :h], fa[:, :, h:] - fb[:, :, h:]], axis=2).astype(BF16)


def _hy_fft_body(zp_ref, z_ref, x1c_ref, kf_ref, m1_ref, f1_ref, g1_ref, g2_ref, skip_ref, o_ref,
                 p_ref, q_ref, y_ref):
    h = FFT_N1
    n2c = g1_ref.shape[0]
    nb = g2_ref.shape[0]
    _fft_stage1(zp_ref, m1_ref, p_ref)

    def mid(k2, c):
        xf = _fft_stage2(p_ref, f1_ref, k2)
        kf = kf_ref[k2].astype(F32)
        xr, xi, kr, ki = xf[:h], xf[h:], kf[:h], kf[h:]
        y = jnp.concatenate([xr * kr - xi * ki, xr * ki + xi * kr], axis=0).astype(BF16)
        d = jnp.dot(g1_ref[k2], y, preferred_element_type=F32)
        q_ref[0, pl.ds(k2, h, stride=n2c + FFT_PAD), :] = d[:h]
        q_ref[1, pl.ds(k2, h, stride=n2c + FFT_PAD), :] = d[h:]
        return c
    lax.fori_loop(0, n2c, mid, 0, unroll=FFT_UNROLL)

    def last(t1, c):
        r0 = pl.multiple_of(t1 * (n2c + FFT_PAD), 8)
        slab = jnp.concatenate([q_ref[0, pl.ds(r0, n2c), :], q_ref[1, pl.ds(r0, n2c), :]], axis=0)
        y_ref[pl.ds(t1, nb, stride=h), :] = jnp.dot(g2_ref[...], slab.astype(BF16), preferred_element_type=F32)
        return c
    lax.fori_loop(0, h, last, 0, unroll=FFT_UNROLL)

    z = z_ref[...].astype(F32)
    o_ref[...] = ((y_ref[...] + z * skip_ref[...]) * x1c_ref[...].astype(F32)).astype(o_ref.dtype)


def hy_fft_conv(zp, z, x1c, kf, tables, skip):
    m1, f1, g1, g2 = tables
    b, s, c = z.shape
    nch = c // 128

    def const(shape):
        return pl.BlockSpec(shape, lambda ci, bi: (0,) * len(shape))

    nat = pl.BlockSpec((None, s, 128), lambda ci, bi: (bi, 0, ci))
    return pl.pallas_call(
        _hy_fft_body, grid=(nch, b),
        in_specs=[pl.BlockSpec((None, None) + zp.shape[2:], lambda ci, bi: (bi, ci, 0, 0)), nat, nat,
                  pl.BlockSpec((None,) + kf.shape[1:], lambda ci, bi: (ci, 0, 0, 0)),
                  const(m1.shape), const(f1.shape), const(g1.shape), const(g2.shape),
                  pl.BlockSpec((1, 128), lambda ci, bi: (0, ci))],
        out_specs=nat,
        out_shape=jax.ShapeDtypeStruct((b, s, c), BF16),
        scratch_shapes=[pltpu.VMEM((2, g1.shape[0] * (FFT_N1 + FFT_PAD), 128), F32),
                        pltpu.VMEM((2, FFT_N1 * (g1.shape[0] + FFT_PAD), 128), F32),
                        pltpu.VMEM((s, 128), F32)],
        compiler_params=_params("parallel", "parallel"), name="hy_fft_conv",
    )(zp, z, x1c, kf, m1, f1, g1, g2, skip.reshape(1, c).astype(F32))


def _ml_prep_body(u_ref, w_ref, b_ref, o_ref, *, scale, transpose):
    y = _conv3(u_ref, w_ref, b_ref)
    y = y * _sigmoid(y) * scale
    o_ref[...] = (y.T if transpose else y).astype(o_ref.dtype)


def ml_prep(u, conv_w, conv_b, col0, ncols, scale, transpose, tc=256):
    b, s, _ = u.shape
    w = conv_w.shape[1]
    c0 = col0 // tc
    if transpose:
        out_spec = pl.BlockSpec((None, tc, s), lambda bi, c: (bi, c, 0))
        out_shape = jax.ShapeDtypeStruct((b, ncols, s), BF16)
    else:
        out_spec = pl.BlockSpec((None, s, tc), lambda bi, c: (bi, 0, c))
        out_shape = jax.ShapeDtypeStruct((b, s, ncols), BF16)
    return pl.pallas_call(
        functools.partial(_ml_prep_body, scale=scale, transpose=transpose), grid=(b, ncols // tc),
        in_specs=[pl.BlockSpec((None, s, tc), lambda bi, c: (bi, 0, c0 + c)),
                  pl.BlockSpec((3, tc), lambda bi, c: (0, c0 + c)),
                  pl.BlockSpec((1, tc), lambda bi, c: (0, c0 + c))],
        out_specs=out_spec, out_shape=out_shape,
        compiler_params=_params("parallel", "parallel"), name="ml_prep",
    )(u, conv_w, conv_b.reshape(1, w))


def _log_sigmoid(x):
    return jnp.minimum(x, 0.0) - jnp.log(1.0 + jnp.exp(-jnp.abs(x)))


def _dot_split(a, b, a_is_f32):
    x = a if a_is_f32 else b
    hi = x.astype(BF16)
    lo = (x - hi.astype(F32)).astype(BF16)
    if a_is_f32:
        return (jnp.dot(hi, b, preferred_element_type=F32) + jnp.dot(lo, b, preferred_element_type=F32))
    return (jnp.dot(a, hi, preferred_element_type=F32) + jnp.dot(a, lo, preferred_element_type=F32))


def _mlstm_chain(q, kt, v, bc, br, li_r, total, mask, c_ref, m_ref, idx):
    dv = v.shape[1] - 128
    c_st = c_ref[idx]
    m_st = m_ref[idx:idx + 1, 0:1]
    key_row = li_r - br
    inter = bc + m_st
    m_t = jnp.maximum(inter, bc + jnp.max(jnp.where(mask, key_row, NEG_BIG), axis=-1, keepdims=True))
    w_intra = jnp.exp(jnp.where(mask, (bc - m_t) + key_row, NEG_BIG))
    w_inter = jnp.exp(inter - m_t)
    sc = jnp.dot(q, kt, preferred_element_type=F32) * w_intra
    q_inter = (q.astype(F32) * w_inter).astype(BF16)
    both = jnp.dot(jnp.concatenate([q_inter, sc.astype(BF16)], axis=1),
                   jnp.concatenate([c_st.astype(BF16), v], axis=0),
                   preferred_element_type=F32)
    den = both[:, dv:dv + 1]
    h = both[:, :dv] / jnp.maximum(jnp.abs(den), jnp.exp(-m_t))
    g_row = total + key_row
    m_next = jnp.maximum(total + m_st, jnp.max(g_row, axis=-1, keepdims=True))
    a_prev = jnp.exp(total + m_st - m_next)
    kwt = (kt.astype(F32) * jnp.exp(g_row - m_next)).astype(BF16)
    c_ref[idx] = a_prev * c_st + jnp.dot(kwt, v, preferred_element_type=F32)
    m_ref[idx:idx + 1, :] = jnp.broadcast_to(m_next, (1, m_ref.shape[1]))
    return h


def _mlstm_body(qf_ref, ktf_ref, vf_ref, gcf_ref, grf_ref, qb_ref, ktb_ref, vb_ref, gcb_ref, grb_ref,
                hf_ref, hb_ref, c_ref, m_ref, *, heads, dk, dv):
    @pl.when(pl.program_id(1) == 0)
    def _():
        c_ref[...] = jnp.zeros_like(c_ref)
        m_ref[...] = jnp.zeros_like(m_ref)

    lc = qf_ref.shape[1]
    ones_blk = jnp.where(lax.broadcasted_iota(I32, (lc, 128), 1) == 0, 1.0, 0.0).astype(BF16)
    t_i = lax.broadcasted_iota(I32, (lc, lc), 0)
    s_i = lax.broadcasted_iota(I32, (lc, lc), 1)
    lower = s_i <= t_i
    upper = s_i >= t_i
    ltri = jnp.where(lower, 1.0, 0.0).astype(BF16)
    utri = jnp.where(upper, 1.0, 0.0).astype(BF16)

    for bb in range(qf_ref.shape[0]):
        for direction, (q_ref, kt_ref, v_ref, gc_ref, gr_ref, h_ref) in enumerate(
                ((qf_ref, ktf_ref, vf_ref, gcf_ref, grf_ref, hf_ref),
                 (qb_ref, ktb_ref, vb_ref, gcb_ref, grb_ref, hb_ref))):
            fwd = direction == 0
            gc = gc_ref[bb]
            gr = gr_ref[bb]
            lfc, lfr = _log_sigmoid(gc), _log_sigmoid(gr)
            cum_c = _dot_split(ltri if fwd else utri, lfc, a_is_f32=False)
            cum_r = _dot_split(lfr, utri if fwd else ltri, a_is_f32=True)
            for hd in range(heads):
                gi = (0 if fwd else 2) * heads + hd
                gf = (1 if fwd else 3) * heads + hd
                bc, br = cum_c[:, gf:gf + 1], cum_r[gf:gf + 1, :]
                total = br[:, lc - 1:lc] if fwd else br[:, 0:1]
                q = q_ref[bb, :, hd * dk:(hd + 1) * dk]
                kt = kt_ref[bb, hd * dk:(hd + 1) * dk, :]
                v = jnp.concatenate([v_ref[bb, :, hd * dv:(hd + 1) * dv], ones_blk], axis=1)
                h = _mlstm_chain(q, kt, v, bc, br, gr[gi:gi + 1, :], total,
                                 lower if fwd else upper, c_ref, m_ref, (bb * 2 + direction) * heads + hd)
                h_ref[bb, :, hd * dv:(hd + 1) * dv] = h.astype(h_ref.dtype)


def mlstm_scan(q, kt, u, v_blk, gcol, grow):
    b, s, w = q.shape
    heads = ML_HEADS
    dk = w // heads
    dv = 2 * dk
    lc = ML_CHUNK
    nc = s // lc
    ng = gcol.shape[-1]

    def fw(bi, j):
        return j

    def bw(bi, j):
        return nc - 1 - j

    nbs = ML_BATCH_PER_STEP if b % ML_BATCH_PER_STEP == 0 else 1

    def specs(pos):
        return [pl.BlockSpec((nbs, lc, w), lambda bi, j: (bi, pos(bi, j), 0)),
                pl.BlockSpec((nbs, w, lc), lambda bi, j: (bi, 0, pos(bi, j))),
                pl.BlockSpec((nbs, lc, heads * dv), lambda bi, j: (bi, pos(bi, j), v_blk)),
                pl.BlockSpec((nbs, lc, ng), lambda bi, j: (bi, pos(bi, j), 0)),
                pl.BlockSpec((nbs, ng, lc), lambda bi, j: (bi, 0, pos(bi, j)))]

    hshape = jax.ShapeDtypeStruct((b, s, heads * dv), BF16)
    return pl.pallas_call(
        functools.partial(_mlstm_body, heads=heads, dk=dk, dv=dv), grid=(b // nbs, nc),
        in_specs=specs(fw) + specs(bw),
        out_specs=[pl.BlockSpec((nbs, lc, heads * dv), lambda bi, j: (bi, j, 0)),
                   pl.BlockSpec((nbs, lc, heads * dv), lambda bi, j: (bi, nc - 1 - j, 0))],
        out_shape=[hshape, hshape],
        scratch_shapes=[pltpu.VMEM((nbs * 2 * heads, dk, dv + 128), F32),
                        pltpu.VMEM((nbs * 2 * heads, 128), F32)],
        compiler_params=_params("parallel", "arbitrary"), name="mlstm_scan",
    )(q, kt, u, gcol, grow, q, kt, u, gcol, grow)


def _mlstm_out_body(res_ref, hf_ref, hb_ref, o_ref, g_ref, w_ref, out_ref, *, heads):
    hs = hf_ref[...].astype(F32) + hb_ref[...].astype(F32)
    dv = hs.shape[1] // heads
    g = g_ref[...]
    parts = []
    for hd in range(heads):
        seg = hs[:, hd * dv:(hd + 1) * dv]
        ms = jnp.mean(seg * seg, axis=-1, keepdims=True)
        parts.append(seg * lax.rsqrt(ms + EPS) * g[:, hd * dv:(hd + 1) * dv])
    a = jnp.concatenate(parts, axis=-1) * _sigmoid(o_ref[...].astype(F32))
    out_ref[...] = res_ref[...] + jnp.dot(a.astype(BF16), w_ref[...], preferred_element_type=F32)


def mlstm_out(res, hf, hb, u2d, o_blk, gain, w_out, tm=1024):
    t, d = res.shape
    row = lambda i: (i, 0)
    return pl.pallas_call(
        functools.partial(_mlstm_out_body, heads=ML_HEADS), grid=(t // tm,),
        in_specs=[pl.BlockSpec((tm, d), row), pl.BlockSpec((tm, d), row), pl.BlockSpec((tm, d), row),
                  pl.BlockSpec((tm, d), lambda i: (i, o_blk)), pl.BlockSpec((1, d), lambda i: (0, 0)),
                  pl.BlockSpec((d, d), lambda i: (0, 0))],
        out_specs=pl.BlockSpec((tm, d), row), out_shape=jax.ShapeDtypeStruct((t, d), F32),
        compiler_params=_params("parallel"), name="mlstm_out",
    )(res, hf, hb, u2d, gain.reshape(1, d), w_out)


ROUTE_ROWS = 128
EXPERT_ROW0 = 8


def _router_body(x_ref, g_ref, wt_ref, b_ref, o_ref, cnt_ref, run_ref):
    @pl.when(pl.program_id(0) == 0)
    def _():
        run_ref[...] = jnp.zeros_like(run_ref)

    x = x_ref[...]
    ms = jnp.mean(x * x, axis=-1, keepdims=True)
    xn = x * lax.rsqrt(ms + EPS) * g_ref[...]
    wt = wt_ref[...]
    w_hi, x_hi = wt.astype(BF16), xn.astype(BF16)
    w_lo, x_lo = (wt - w_hi.astype(F32)).astype(BF16), (xn - x_hi.astype(F32)).astype(BF16)
    nt = (((1,), (1,)), ((), ()))
    logit = (lax.dot_general(w_hi, x_hi, nt, preferred_element_type=F32)
             + lax.dot_general(w_lo, x_hi, nt, preferred_element_type=F32)
             + lax.dot_general(w_hi, x_lo, nt, preferred_element_type=F32)) + b_ref[...]
    rows = [logit[r:r + 1, :] for r in range(EXPERT_ROW0 + N_EXPERTS)]
    g_best, g_idx = rows[0], jnp.zeros_like(rows[0])
    for gi in range(1, N_GROUPS):
        better = rows[gi] > g_best
        g_best = jnp.where(better, rows[gi], g_best)
        g_idx = jnp.where(better, float(gi), g_idx)
    g_den = sum(jnp.exp(rows[gi] - g_best) for gi in range(N_GROUPS))
    g_w = 1.0 / g_den
    sel = []
    for e in range(EXPERTS_PER_GROUP):
        v = rows[EXPERT_ROW0 + e]
        for gi in range(1, N_GROUPS):
            v = jnp.where(g_idx == float(gi), rows[EXPERT_ROW0 + gi * EXPERTS_PER_GROUP + e], v)
        sel.append(v)
    v1, i1 = sel[0], jnp.zeros_like(sel[0])
    for e in range(1, EXPERTS_PER_GROUP):
        better = sel[e] > v1
        v1 = jnp.where(better, sel[e], v1)
        i1 = jnp.where(better, float(e), i1)
    v2, i2 = jnp.full_like(v1, -jnp.inf), jnp.zeros_like(v1)
    for e in range(EXPERTS_PER_GROUP):
        better = (sel[e] > v2) & (i1 != float(e))
        v2 = jnp.where(better, sel[e], v2)
        i2 = jnp.where(better, float(e), i2)
    e21 = jnp.exp(v2 - v1)
    gate1 = g_w / (1.0 + e21)
    gate2 = gate1 * e21
    base = g_idx * float(EXPERTS_PER_GROUP)
    e1, e2 = base + i1, base + i2
    tm = e1.shape[1]
    erow = lax.broadcasted_iota(I32, (N_EXPERTS, tm), 0).astype(F32)
    oh1 = jnp.where(erow == e1, 1.0, 0.0)
    oh2 = jnp.where(erow == e2, 1.0, 0.0)
    cnt = oh1 + oh2
    earlier = jnp.where(lax.broadcasted_iota(I32, (tm, tm), 0) < lax.broadcasted_iota(I32, (tm, tm), 1),
                        1.0, 0.0).astype(BF16)
    run = run_ref[...]
    pos = run[:, 0:1] + jnp.dot(cnt.astype(BF16), earlier, preferred_element_type=F32)
    rank1 = jnp.sum(oh1 * pos, axis=0, keepdims=True)
    rank2 = jnp.sum(oh2 * pos, axis=0, keepdims=True)
    run = run + jnp.sum(cnt, axis=1, keepdims=True)
    run_ref[...] = run
    cnt_ref[...] = run
    zero = jnp.zeros_like(v1)
    o_ref[...] = jnp.concatenate([e1, e2, gate1, gate2, rank1, rank2, zero, zero], axis=0)


def moe_router(x2d, g, wg, bg, we, be, tm=512):
    t, d = x2d.shape
    wt = jnp.zeros((ROUTE_ROWS, d), F32).at[:N_GROUPS].set(wg.T).at[EXPERT_ROW0:EXPERT_ROW0 + N_EXPERTS].set(we.T)
    bias = jnp.zeros((ROUTE_ROWS, 1), F32).at[:N_GROUPS, 0].set(bg).at[EXPERT_ROW0:EXPERT_ROW0 + N_EXPERTS, 0].set(be)
    return pl.pallas_call(
        _router_body, grid=(t // tm,),
        in_specs=[pl.BlockSpec((tm, d), lambda i: (i, 0)), pl.BlockSpec((1, d), lambda i: (0, 0)),
                  pl.BlockSpec((ROUTE_ROWS, d), lambda i: (0, 0)), pl.BlockSpec((ROUTE_ROWS, 1), lambda i: (0, 0))],
        out_specs=[pl.BlockSpec((8, tm), lambda i: (0, i)), pl.BlockSpec((N_EXPERTS, 128), lambda i: (0, 0))],
        out_shape=[jax.ShapeDtypeStruct((8, t), F32), jax.ShapeDtypeStruct((N_EXPERTS, 128), F32)],
        scratch_shapes=[pltpu.VMEM((N_EXPERTS, 128), F32)],
        compiler_params=_params("arbitrary"), name="moe_router",
    )(x2d, g.reshape(1, d), wt, bias)


def _dispatch_body(dest_ref, zblk_ref, x_ref, g_ref, xs_ref, buf_ref, sem_ref, *, td):
    i = pl.program_id(0)
    n = pl.num_programs(0)
    slot = i % 2

    @pl.when(i == 0)
    def _():
        buf_ref[1] = jnp.zeros(buf_ref.shape[1:], buf_ref.dtype)

        def zero_copy(j):
            return pltpu.make_async_copy(buf_ref.at[1], xs_ref.at[pl.ds(zblk_ref[j] * td, td), :], sem_ref.at[1])

        def start(j, c):
            @pl.when(zblk_ref[j] >= 0)
            def _():
                zero_copy(j).start()
            return c

        def wait(j, c):
            @pl.when(zblk_ref[j] >= 0)
            def _():
                zero_copy(j).wait()
            return c
        lax.fori_loop(0, zblk_ref.shape[0], start, 0)
        lax.fori_loop(0, zblk_ref.shape[0], wait, 0)

    def row_copy(r, kk):
        return pltpu.make_async_copy(buf_ref.at[slot, pl.ds(r, 1), :],
                                     xs_ref.at[pl.ds(dest_ref[(i * td + r) * 2 + kk], 1), :],
                                     sem_ref.at[slot])

    def wait_buffer(sl):
        for _ in range(2):
            pltpu.make_async_copy(buf_ref.at[sl], xs_ref.at[pl.ds(0, td), :], sem_ref.at[sl]).wait()

    @pl.when(i >= 2)
    def _():
        wait_buffer(slot)

    x = x_ref[...]
    ms = jnp.mean(x * x, axis=-1, keepdims=True)
    buf_ref[slot] = x * lax.rsqrt(ms + EPS) * g_ref[...]

    for r in range(td):
        row_copy(r, 0).start(priority=0)
        row_copy(r, 1).start(priority=1)

    @pl.when(i == n - 1)
    def _():
        @pl.when(n >= 2)
        def _():
            wait_buffer(1 - slot)
        wait_buffer(slot)


def moe_dispatch(x2d, g, dest, zero_blk, cap, td):
    t, d = x2d.shape
    grid_spec = pltpu.PrefetchScalarGridSpec(
        num_scalar_prefetch=2, grid=(t // td,),
        in_specs=[pl.BlockSpec((td, d), lambda i, dest, zb: (i, 0)),
                  pl.BlockSpec((1, d), lambda i, dest, zb: (0, 0))],
        out_specs=pl.BlockSpec(memory_space=pl.ANY),
        scratch_shapes=[pltpu.VMEM((2, td, d), F32), pltpu.SemaphoreType.DMA((2,))])
    return pl.pallas_call(
        functools.partial(_dispatch_body, td=td), grid_spec=grid_spec,
        out_shape=jax.ShapeDtypeStruct((cap, d), F32),
        compiler_params=_params("arbitrary", disable_bounds_checks=True),
        name="moe_dispatch",
    )(dest, zero_blk, x2d, g.reshape(1, d))


def _expert_body(blk_e_ref, nused_ref, xs_ref, w1_ref, w3_ref, w2_ref, ys_ref, w1b_ref, w3b_ref, w2b_ref):
    i = pl.program_id(0)
    nused = nused_ref[0]
    last = nused - 1
    cur = blk_e_ref[jnp.minimum(i, last)]
    prev = blk_e_ref[jnp.minimum(jnp.maximum(i - 1, 0), last)]

    @pl.when((i == 0) | (cur != prev))
    def _():
        w1b_ref[...] = w1_ref[...].astype(BF16)
        w3b_ref[...] = w3_ref[...].astype(BF16)
        w2b_ref[...] = w2_ref[...].astype(BF16)

    @pl.when(i < nused)
    def _():
        x = xs_ref[...].astype(BF16)
        h1 = jnp.dot(x, w1b_ref[...], preferred_element_type=F32)
        h3 = jnp.dot(x, w3b_ref[...], preferred_element_type=F32)
        hid = (h1 * _sigmoid(h1) * h3).astype(BF16)
        ys_ref[...] = jnp.dot(hid, w2b_ref[...], preferred_element_type=F32)

    @pl.when(i >= nused)
    def _():
        ys_ref[...] = jnp.zeros_like(ys_ref)


def moe_experts(xs, blk_e, nused, w1, w3, w2, layer, tm):
    cap, d = xs.shape
    de = w1.shape[-1]

    def blk(i, be, nu):
        return jnp.minimum(i, nu[0] - 1)

    grid_spec = pltpu.PrefetchScalarGridSpec(
        num_scalar_prefetch=2, grid=(cap // tm,),
        in_specs=[pl.BlockSpec((tm, d), lambda i, be, nu: (blk(i, be, nu), 0)),
                  pl.BlockSpec((None, None, d, de), lambda i, be, nu: (layer, be[blk(i, be, nu)], 0, 0)),
                  pl.BlockSpec((None, None, d, de), lambda i, be, nu: (layer, be[blk(i, be, nu)], 0, 0)),
                  pl.BlockSpec((None, None, de, d), lambda i, be, nu: (layer, be[blk(i, be, nu)], 0, 0))],
        out_specs=pl.BlockSpec((tm, d), lambda i, be, nu: (i, 0)),
        scratch_shapes=[pltpu.VMEM((d, de), BF16), pltpu.VMEM((d, de), BF16), pltpu.VMEM((de, d), BF16)])
    return pl.pallas_call(
        _expert_body, grid_spec=grid_spec, out_shape=jax.ShapeDtypeStruct((cap, d), F32),
        compiler_params=_params("arbitrary"), name="moe_experts",
    )(blk_e, nused, xs, w1, w3, w2)


def _combine_body(dest_ref, x_ref, gate_ref, ys_ref, o_ref, buf_ref, sem_ref, *, tc):
    i = pl.program_id(0)
    n = pl.num_programs(0)
    slot = i % 2

    def row_copy(step, sl, r, kk):
        return pltpu.make_async_copy(ys_ref.at[pl.ds(dest_ref[(step * tc + r) * 2 + kk], 1), :],
                                     buf_ref.at[sl, kk, pl.ds(r, 1), :], sem_ref.at[sl])

    def issue_step(step, sl):
        for r in range(tc):
            row_copy(step, sl, r, 0).start(priority=0)
            row_copy(step, sl, r, 1).start(priority=1)

    @pl.when(i == 0)
    def _():
        issue_step(0, 0)

    @pl.when(i + 1 < n)
    def _():
        issue_step(i + 1, 1 - slot)

    for kk in range(2):
        pltpu.make_async_copy(ys_ref.at[pl.ds(0, tc), :], buf_ref.at[slot, kk], sem_ref.at[slot]).wait()

    gate = gate_ref[...]
    o_ref[...] = x_ref[...] + gate[:, 0:1] * buf_ref[slot, 0] + gate[:, 1:2] * buf_ref[slot, 1]


def moe_combine(x2d, gates, ys, dest, tc=256):
    t, d = x2d.shape
    grid_spec = pltpu.PrefetchScalarGridSpec(
        num_scalar_prefetch=1, grid=(t // tc,),
        in_specs=[pl.BlockSpec((tc, d), lambda i, dest: (i, 0)), pl.BlockSpec((tc, 2), lambda i, dest: (i, 0)),
                  pl.BlockSpec(memory_space=pl.ANY)],
        out_specs=pl.BlockSpec((tc, d), lambda i, dest: (i, 0)),
        scratch_shapes=[pltpu.VMEM((2, 2, tc, d), F32), pltpu.SemaphoreType.DMA((2,))])
    return pl.pallas_call(
        functools.partial(_combine_body, tc=tc), grid_spec=grid_spec,
        out_shape=jax.ShapeDtypeStruct((t, d), F32),
        compiler_params=_params("arbitrary", disable_bounds_checks=True), name="moe_combine",
    )(dest, x2d, gates, ys)


def hier_moe_residual(x2d, g, wg, bg, we, be, w1, w3, w2, layer, tm=512):
    t, d = x2d.shape
    route, counts_b = moe_router(x2d, g, wg, bg, we, be)
    eid = route[0:2].T.astype(I32).reshape(-1)
    rank = route[4:6].T.astype(I32).reshape(-1)
    gates = route[2:4].T
    counts = counts_b[:, 0].astype(I32)
    padded = (counts + tm - 1) // tm * tm
    pad_end = jnp.cumsum(padded)
    pad_start = pad_end - padded
    experts = jnp.arange(N_EXPERTS, dtype=I32)
    dest = rank + jnp.sum(jnp.where(eid[:, None] == experts[None, :], pad_start[None, :], 0), axis=1)
    cap = t * 2 + N_EXPERTS * tm
    nblk = cap // tm
    blk_row0 = jnp.arange(nblk, dtype=I32) * tm
    blk_e = jnp.minimum(jnp.sum((pad_end[None, :] <= blk_row0[:, None]).astype(I32), axis=1), N_EXPERTS - 1)
    nused = (pad_end[-1:] // tm).astype(I32)
    last_blk = jnp.where(padded > 0, pad_end // tm - 1, -1)
    tail_blk = jnp.where(nused[0] + experts < nblk, nused[0] + experts, -1)
    zero_blk = jnp.concatenate([last_blk, tail_blk]).astype(I32)
    xs = moe_dispatch(x2d, g, dest.astype(I32), zero_blk, cap, tm)
    ys = moe_experts(xs, blk_e, nused, w1, w3, w2, layer, tm)
    return moe_combine(x2d, gates, ys, dest.astype(I32))


def _even_layer(x2d, b, s, layer, mix_g, w_in, q_norm, k_norm, lam_q1, lam_k1, lam_q2, lam_k2, subln,
                hy_conv_w, hy_conv_b, f_w1, f_b1, f_freq, f_w2, f_b2, f_w3, hy_skip, w_out):
    d = x2d.shape[1]
    d_att = d // 2
    (u2d,) = norm_matmul(x2d, mix_g, [w_in.astype(BF16)], [BF16])
    u = u2d.reshape(b, s, -1)
    lambda_init = 0.8 - 0.6 * math.exp(-0.3 * layer)
    lam = jnp.exp(jnp.sum(lam_q1 * lam_k1)) - jnp.exp(jnp.sum(lam_q2 * lam_k2)) + lambda_init
    dk = q_norm.shape[0]
    qt = head_prep(u, 0, q_norm, scale=dk ** -0.5 * math.log2(math.e), transpose=True)
    kp = head_prep(u, ATT_HEADS, k_norm)
    vt = head_prep(u, 2 * ATT_HEADS, transpose=True)
    y_att = diff_attention(qt, kp, vt, lam, subln, lambda_init)
    z, x1c, zp = hy_prep(u, 3 * d_att, hy_conv_w, hy_conv_b)
    tables = fft_tables(s)
    kf = filter_spectrum(hyena_filter(s, f_w1, f_b1, f_freq, f_w2, f_b2, f_w3), tables[0], tables[1])
    y_hy = hy_fft_conv(zp, z, x1c, kf, tables, hy_skip)
    w_out = w_out.astype(BF16)
    return matmul_residual(x2d, [y_att.reshape(b * s, -1), y_hy.reshape(b * s, -1)],
                           [w_out[:d_att], w_out[d_att:]])


def _odd_layer(x2d, b, s, mix_g, w_in, conv_w, conv_b, gate_b, out_norm, w_out):
    d = x2d.shape[1]
    qk_w = conv_w.shape[1]
    main_w = qk_w + 2 * d
    ng = 4 * ML_HEADS
    w_gate = jnp.zeros((d, 128), F32).at[:, :ng].set(w_in[:, main_w:]).astype(BF16)
    u2d, ug = norm_matmul(x2d, mix_g, [w_in[:, :main_w].astype(BF16), w_gate], [BF16, F32])
    u = u2d.reshape(b, s, main_w)
    gcol = (ug[:, :ng] + gate_b).reshape(b, s, ng)
    grow = jnp.swapaxes(gcol, 1, 2)
    dk = qk_w // (2 * ML_HEADS)
    q = ml_prep(u, conv_w, conv_b, 0, qk_w // 2, dk ** -0.5, transpose=False)
    kt = ml_prep(u, conv_w, conv_b, qk_w // 2, qk_w // 2, 1.0, transpose=True)
    hf, hb = mlstm_scan(q, kt, u, qk_w // d, gcol, grow)
    return mlstm_out(x2d, hf.reshape(b * s, d), hb.reshape(b * s, d), u2d, (qk_w + d) // d, out_norm,
                     w_out.astype(BF16))


def kernel(x, mix_norm, ffn_norm, ev_w_in, ev_q_norm, ev_k_norm, ev_lam_q1, ev_lam_k1, ev_lam_q2, ev_lam_k2, ev_subln, ev_hy_conv_w, ev_hy_conv_b, ev_hy_f_w1, ev_hy_f_b1, ev_hy_f_freq, ev_hy_f_w2, ev_hy_f_b2, ev_hy_f_w3, ev_hy_skip, ev_w_out, od_w_in, od_conv_w, od_conv_b, od_gate_b, od_out_norm, od_w_out, moe_wg, moe_bg, moe_we, moe_be, moe_w1, moe_w3, moe_w2):
    b, s, d = x.shape
    depth = mix_norm.shape[0]
    x2d = x.reshape(b * s, d)
    for layer in range(depth):
        j = layer // 2
        if layer % 2 == 0:
            x2d = _even_layer(x2d, b, s, layer, mix_norm[layer], ev_w_in[j], ev_q_norm[j], ev_k_norm[j],
                              ev_lam_q1[j], ev_lam_k1[j], ev_lam_q2[j], ev_lam_k2[j], ev_subln[j],
                              ev_hy_conv_w[j], ev_hy_conv_b[j], ev_hy_f_w1[j], ev_hy_f_b1[j], ev_hy_f_freq[j],
                              ev_hy_f_w2[j], ev_hy_f_b2[j], ev_hy_f_w3[j], ev_hy_skip[j], ev_w_out[j])
        else:
            x2d = _odd_layer(x2d, b, s, mix_norm[layer], od_w_in[j], od_conv_w[j], od_conv_b[j], od_gate_b[j],
                             od_out_norm[j], od_w_out[j])
        x2d = hier_moe_residual(x2d, ffn_norm[layer], moe_wg[layer], moe_bg[layer], moe_we[layer],
                                moe_be[layer], moe_w1, moe_w3, moe_w2, layer)
    return x2d.reshape(b, s, d)
```

```python
import functools
import math

import jax
import jax.numpy as jnp
import numpy as np
from jax import lax
from jax.experimental import pallas as pl
from jax.experimental.pallas import tpu as pltpu

F32 = jnp.float32
BF16 = jnp.bfloat16
I32 = jnp.int32

EPS = 1e-6
ROPE_THETA = 500000.0
ATT_HEADS = 4
ML_HEADS = 4
ML_CHUNK = 128
ML_BATCH_PER_STEP = 2
N_GROUPS = 4
EXPERTS_PER_GROUP = 8
N_EXPERTS = N_GROUPS * EXPERTS_PER_GROUP
HY_EMB_BANDS = 16
HY_MIN_DECAY = math.log(1e-2) / 1.5
HY_MAX_DECAY = math.log(1e-2) / 0.3

V7X_VMEM_BYTES = 64 * 1024 * 1024
VMEM_LIMIT = V7X_VMEM_BYTES - 8 * 1024 * 1024
NEG_BIG = -1e30


def _params(*sem, **kw):
    return pltpu.CompilerParams(dimension_semantics=sem, vmem_limit_bytes=VMEM_LIMIT, **kw)


def _sigmoid(x):
    return 1.0 / (1.0 + jnp.exp(-x))


def _norm_matmul_body(x_ref, g_ref, *refs, n_out, col_chunk):
    w_refs, o_refs = refs[:n_out], refs[n_out:]
    x = x_ref[...]
    ms = jnp.mean(x * x, axis=-1, keepdims=True)
    hn = (x * lax.rsqrt(ms + EPS) * g_ref[...]).astype(BF16)
    for w_ref, o_ref in zip(w_refs, o_refs):
        n = w_ref.shape[1]
        for c in range(0, n, col_chunk):
            ce = min(n, c + col_chunk)
            o_ref[:, c:ce] = jnp.dot(hn, w_ref[:, c:ce], preferred_element_type=F32).astype(o_ref.dtype)


def norm_matmul(x2d, g, ws, out_dtypes, tm=1024):
    t, d = x2d.shape
    in_specs = [pl.BlockSpec((tm, d), lambda i: (i, 0)), pl.BlockSpec((1, d), lambda i: (0, 0))]
    in_specs += [pl.BlockSpec(w.shape, lambda i: (0, 0)) for w in ws]
    out_specs = [pl.BlockSpec((tm, w.shape[1]), lambda i: (i, 0)) for w in ws]
    out_shape = [jax.ShapeDtypeStruct((t, w.shape[1]), dt) for w, dt in zip(ws, out_dtypes)]
    return pl.pallas_call(
        functools.partial(_norm_matmul_body, n_out=len(ws), col_chunk=1024),
        grid=(t // tm,), in_specs=in_specs, out_specs=out_specs, out_shape=out_shape,
        compiler_params=_params("parallel"), name="norm_matmul",
    )(x2d, g.reshape(1, d), *ws)


def _matmul_res_body(res_ref, *refs, n_in):
    a_refs, w_refs, o_ref = refs[:n_in], refs[n_in:2 * n_in], refs[2 * n_in]
    acc = res_ref[...]
    for a_ref, w_ref in zip(a_refs, w_refs):
        acc = acc + jnp.dot(a_ref[...], w_ref[...], preferred_element_type=F32)
    o_ref[...] = acc


def matmul_residual(res, a_list, w_list, tm=1024):
    t, d = res.shape
    in_specs = [pl.BlockSpec((tm, d), lambda i: (i, 0))]
    in_specs += [pl.BlockSpec((tm, a.shape[1]), lambda i: (i, 0)) for a in a_list]
    in_specs += [pl.BlockSpec(w.shape, lambda i: (0, 0)) for w in w_list]
    return pl.pallas_call(
        functools.partial(_matmul_res_body, n_in=len(a_list)),
        grid=(t // tm,), in_specs=in_specs, out_specs=pl.BlockSpec((tm, d), lambda i: (i, 0)),
        out_shape=jax.ShapeDtypeStruct((t, d), F32),
        compiler_params=_params("parallel"), name="matmul_residual",
    )(res, *a_list, *w_list)


def _head_prep_body(u_ref, *refs, dk, rope, transpose):
    o_ref = refs[-1]
    x = u_ref[...].astype(F32)
    if rope:
        g_ref, c_ref, s1_ref, s2_ref = refs[:4]
        lane = lax.broadcasted_iota(I32, x.shape, 1)
        lo = lane < dk
        x2 = x * x
        s_lo = jnp.sum(jnp.where(lo, x2, 0.0), axis=-1, keepdims=True)
        s_hi = jnp.sum(jnp.where(lo, 0.0, x2), axis=-1, keepdims=True)
        ms = jnp.where(lo, s_lo, s_hi) * (1.0 / dk)
        y = x * lax.rsqrt(ms + EPS) * g_ref[...]
        x = y * c_ref[...] + pltpu.roll(y, 120, 1) * s1_ref[...] + pltpu.roll(y, 8, 1) * s2_ref[...]
    if transpose:
        x = x.T
    o_ref[...] = x.astype(o_ref.dtype)


def _rope_lane_tables(seq, dk, rope_dim, scale):
    half = rope_dim // 2
    f32 = np.float32
    inv_freq = (f32(1.0) / (f32(ROPE_THETA) ** (np.arange(0, rope_dim, 2, dtype=f32) / f32(rope_dim)))).astype(f32)
    ang = np.arange(seq, dtype=f32)[:, None] * inv_freq[None, :]
    cos, sin = np.cos(ang), np.sin(ang)
    d = np.arange(2 * dk) % dk
    fi = d % half
    c_tab = np.where(d[None, :] < rope_dim, cos[:, fi], 1.0)
    s1_tab = np.where(d[None, :] < half, -sin[:, fi], 0.0)
    s2_tab = np.where((d[None, :] >= half) & (d[None, :] < rope_dim), sin[:, fi], 0.0)
    return jnp.asarray((np.stack([c_tab, s1_tab, s2_tab]) * scale).astype(f32))


def head_prep(u, blk0, norm_gain=None, scale=1.0, transpose=False):
    b, s, _ = u.shape
    h = ATT_HEADS
    rope = norm_gain is not None
    in_specs = [pl.BlockSpec((None, s, 128), lambda c, bi: (bi, 0, blk0 + c))]
    args = [u]
    dk = 64
    if rope:
        dk = norm_gain.shape[0]
        assert 2 * dk == 128 and dk // 4 == 16, "rope roll shifts assume 64-wide components, 16 rotary dims"
        tabs = _rope_lane_tables(s, dk, dk // 4, scale)
        in_specs += [pl.BlockSpec((1, 128), lambda c, bi: (0, 0))] + [pl.BlockSpec((s, 128), lambda c, bi: (0, 0))] * 3
        args += [jnp.tile(norm_gain, 2).reshape(1, 128).astype(F32), tabs[0], tabs[1], tabs[2]]
    if transpose:
        out_spec = pl.BlockSpec((None, 128, s), lambda c, bi: (bi, c, 0))
        out_shape = jax.ShapeDtypeStruct((b, h * 128, s), BF16)
    else:
        out_spec = pl.BlockSpec((None, s, 128), lambda c, bi: (bi, 0, c))
        out_shape = jax.ShapeDtypeStruct((b, s, h * 128), BF16)
    return pl.pallas_call(
        functools.partial(_head_prep_body, dk=dk, rope=rope, transpose=transpose), grid=(h, b),
        in_specs=in_specs, out_specs=out_spec, out_shape=out_shape,
        compiler_params=_params("parallel", "parallel"), name="head_prep",
    )(*args)


def _attn_body(lam_ref, qt_ref, k_ref, vt_ref, g_ref, o_ref, *, tq, dk, post_scale):
    lam = lam_ref[0, 0]
    qt = qt_ref[...]
    row = lax.broadcasted_iota(I32, qt.shape, 0)
    zero = jnp.zeros_like(qt)
    qq = jnp.concatenate([jnp.where(row < dk, qt, zero), jnp.where(row < dk, zero, qt)], axis=1)
    st = jnp.dot(k_ref[...], qq, preferred_element_type=F32)
    m = jnp.max(st, axis=0, keepdims=True)
    p = jnp.exp2(st - m)
    r = 1.0 / jnp.sum(p, axis=0, keepdims=True)
    ot = jnp.dot(vt_ref[...], p.astype(BF16), preferred_element_type=F32)
    o = (ot[:, :tq] * r[:, :tq] - ot[:, tq:] * (lam * r[:, tq:])).T
    ms = jnp.mean(o * o, axis=-1, keepdims=True)
    o_ref[...] = (o * lax.rsqrt(ms + EPS) * g_ref[...] * post_scale).astype(o_ref.dtype)


def diff_attention(qt, k, vt, lam, subln, lambda_init, tq=512):
    b, s, _ = k.shape
    h = ATT_HEADS
    return pl.pallas_call(
        functools.partial(_attn_body, tq=tq, dk=64, post_scale=1.0 - lambda_init),
        grid=(b, h, s // tq),
        in_specs=[pl.BlockSpec(memory_space=pltpu.SMEM),
                  pl.BlockSpec((None, 128, tq), lambda bi, hi, i: (bi, hi, i)),
                  pl.BlockSpec((None, s, 128), lambda bi, hi, i: (bi, 0, hi)),
                  pl.BlockSpec((None, 128, s), lambda bi, hi, i: (bi, hi, 0)),
                  pl.BlockSpec((1, 128), lambda bi, hi, i: (0, 0))],
        out_specs=pl.BlockSpec((None, tq, 128), lambda bi, hi, i: (bi, i, hi)),
        out_shape=jax.ShapeDtypeStruct((b, s, h * 128), BF16),
        compiler_params=_params("parallel", "parallel", "parallel"), name="diff_attention",
    )(lam.reshape(1, 1).astype(F32), qt, k, vt, subln.reshape(1, 128).astype(F32))


def _conv3(u_ref, w_ref, b_ref):
    x = u_ref[...].astype(F32)
    s = x.shape[0]
    row = lax.broadcasted_iota(I32, x.shape, 0)
    x_prev = jnp.where(row == 0, 0.0, pltpu.roll(x, 1, 0))
    x_next = jnp.where(row == s - 1, 0.0, pltpu.roll(x, s - 1, 0))
    w = w_ref[...]
    return b_ref[...] + x_prev * w[0:1] + x * w[1:2] + x_next * w[2:3]


FFT_N1 = 64
FFT_UNROLL = 64
FFT_PAD = 8


def _hy_prep_body(x1_ref, x2_ref, v_ref, w1_ref, w2_ref, wv_ref, b1_ref, b2_ref, bv_ref, z_ref, x1c_ref, zp_ref):
    x1c_ref[...] = _conv3(x1_ref, w1_ref, b1_ref).astype(x1c_ref.dtype)
    z = _conv3(v_ref, wv_ref, bv_ref) * _conv3(x2_ref, w2_ref, b2_ref)
    z_ref[...] = z.astype(z_ref.dtype)
    nb = z.shape[0] // FFT_N1
    zp_ref[...] = jnp.zeros_like(zp_ref)
    for n2 in range(nb):
        for ci in range(z.shape[1] // 128):
            zp_ref[ci, pl.ds(n2, FFT_N1, stride=nb + FFT_PAD), :] = (
                z[n2 * FFT_N1:(n2 + 1) * FFT_N1, ci * 128:(ci + 1) * 128])


def hy_prep(u, col0, conv_w, conv_b, tc=256):
    b, s, _ = u.shape
    d_hy = conv_w.shape[1] // 3
    nct = d_hy // tc
    blk0 = col0 // tc
    sp = FFT_N1 * (s // FFT_N1 + FFT_PAD)

    def uspec(part):
        return pl.BlockSpec((None, s, tc), lambda bi, c: (bi, 0, blk0 + part * nct + c))

    def wspec(part, rows):
        return pl.BlockSpec((rows, tc), lambda bi, c: (0, part * nct + c))

    ospec = pl.BlockSpec((None, s, tc), lambda bi, c: (bi, 0, c))
    return pl.pallas_call(
        _hy_prep_body, grid=(b, nct),
        in_specs=[uspec(0), uspec(1), uspec(2), wspec(0, 3), wspec(1, 3), wspec(2, 3),
                  wspec(0, 1), wspec(1, 1), wspec(2, 1)],
        out_specs=[ospec, ospec, pl.BlockSpec((None, tc // 128, sp, 128), lambda bi, c: (bi, c, 0, 0))],
        out_shape=[jax.ShapeDtypeStruct((b, s, d_hy), BF16), jax.ShapeDtypeStruct((b, s, d_hy), BF16),
                   jax.ShapeDtypeStruct((b, d_hy // 128, sp, 128), F32)],
        compiler_params=_params("parallel", "parallel"), name="hy_prep",
    )(u, u, u, conv_w, conv_w, conv_w, conv_b.reshape(1, -1), conv_b.reshape(1, -1), conv_b.reshape(1, -1))


def hyena_filter(length, w1, b1, freq, w2, b2, w3):
    d_hy = w3.shape[1] // 2
    f32 = np.float32
    t = np.linspace(0.0, 1.0, length, dtype=f32)[:, None]
    bands = np.linspace(1e-4, HY_EMB_BANDS - 1, HY_EMB_BANDS, dtype=f32)[None, :]
    ang = (f32(2.0 * math.pi / length) * np.arange(length, dtype=f32)[:, None] * bands).astype(f32)
    z = jnp.asarray(np.concatenate([t, np.cos(ang), -np.sin(ang)], axis=-1).astype(f32))
    deltas = np.abs(np.linspace(HY_MIN_DECAY, HY_MAX_DECAY, d_hy, dtype=f32))
    decay = jnp.asarray(np.exp(-t * deltas[None, :]).astype(f32))
    hp = lax.Precision.HIGHEST
    hdn = jnp.sin(freq * (jnp.dot(z, w1, precision=hp) + b1))
    hdn = jnp.sin(freq * (jnp.dot(hdn, w2, precision=hp) + b2))
    filt = jnp.dot(hdn, w3, precision=hp)
    h_fwd = filt[:, :d_hy] * decay
    h_bwd = filt[:, d_hy:] * decay
    h_fwd = h_fwd.at[0].add(h_bwd[0])
    h_bwd = h_bwd.at[0].set(0.0)
    norm = jnp.sum(jnp.abs(h_fwd), axis=0, keepdims=True) + jnp.sum(jnp.abs(h_bwd), axis=0, keepdims=True) + EPS
    return jnp.concatenate([h_fwd / norm, h_bwd / norm], axis=1)


def fft_tables(length):
    n = 2 * length
    n1c, n2c, nb = FFT_N1, n // FFT_N1, length // FFT_N1
    unit = 2.0 * math.pi / n
    i1 = np.arange(n1c, dtype=np.int64)
    i2 = np.arange(n2c, dtype=np.int64)
    ib = np.arange(nb, dtype=np.int64)
    samp = i1[:, None, None] + n1c * ib[None, None, :]
    ang = ((i2[None, :, None] * samp) % n) * unit
    m1 = np.concatenate([np.cos(ang), -np.sin(ang)], axis=1)
    ang = ((i1[:, None] * i1[None, :]) % n1c) * (2.0 * math.pi / n1c)
    c, s = np.cos(ang), np.sin(ang)
    f1 = np.concatenate([np.concatenate([c, s], axis=1), np.concatenate([-s, c], axis=1)], axis=0)
    freq = n2c * i1[None, None, :] + i2[:, None, None]
    ang = ((i1[None, :, None] * freq) % n) * unit
    c, s = np.cos(ang), np.sin(ang)
    g1 = np.concatenate([np.concatenate([c, -s], axis=2), np.concatenate([s, c], axis=2)], axis=1)
    ang = ((ib[:, None] * i2[None, :]) % n2c) * (2.0 * math.pi / n2c)
    g2 = np.concatenate([np.cos(ang), -np.sin(ang)], axis=1) * (1.0 / n)
    return tuple(jnp.asarray(t.astype(np.float32).astype(BF16)) for t in (m1, f1, g1, g2))


def _fft_stage1(xp_ref, m1_ref, p_ref):
    nb = m1_ref.shape[2]
    n2c = m1_ref.shape[1] // 2

    def body(n1, c):
        x = xp_ref[pl.ds(pl.multiple_of(n1 * (nb + FFT_PAD), 8), nb), :].astype(BF16)
        a = jnp.dot(m1_ref[n1], x, preferred_element_type=F32)
        p_ref[0, pl.ds(n1, n2c, stride=FFT_N1 + FFT_PAD), :] = a[:n2c]
        p_ref[1, pl.ds(n1, n2c, stride=FFT_N1 + FFT_PAD), :] = a[n2c:]
        return c
    lax.fori_loop(0, FFT_N1, body, 0, unroll=FFT_UNROLL)


def _fft_stage2(p_ref, f1_ref, k2):
    r0 = pl.multiple_of(k2 * (FFT_N1 + FFT_PAD), 8)
    slab = jnp.concatenate([p_ref[0, pl.ds(r0, FFT_N1), :], p_ref[1, pl.ds(r0, FFT_N1), :]], axis=0)
    return jnp.dot(f1_ref[...], slab.astype(BF16), preferred_element_type=F32)


def _spectrum_body(xp_ref, m1_ref, f1_ref, o_ref, p_ref):
    _fft_stage1(xp_ref, m1_ref, p_ref)

    def body(k2, c):
        o_ref[k2] = _fft_stage2(p_ref, f1_ref, k2)
        return c
    lax.fori_loop(0, o_ref.shape[0], body, 0, unroll=FFT_UNROLL)


def filter_spectrum(ab, m1, f1):
    length, c2 = ab.shape
    nb = length // FFT_N1
    n2c = m1.shape[1] // 2
    nch = c2 // 128
    abp = jnp.pad(ab.reshape(nb, FFT_N1, nch, 128).transpose(2, 1, 0, 3), ((0, 0), (0, 0), (0, FFT_PAD), (0, 0)))
    abp = abp.reshape(nch, FFT_N1 * (nb + FFT_PAD), 128)
    spec = pl.pallas_call(
        _spectrum_body, grid=(nch,),
        in_specs=[pl.BlockSpec((None,) + abp.shape[1:], lambda c: (c, 0, 0)),
                  pl.BlockSpec(m1.shape, lambda c: (0, 0, 0)), pl.BlockSpec(f1.shape, lambda c: (0, 0))],
        out_specs=pl.BlockSpec((None, n2c, 2 * FFT_N1, 128), lambda c: (c, 0, 0, 0)),
        out_shape=jax.ShapeDtypeStruct((nch, n2c, 2 * FFT_N1, 128), F32),
        scratch_shapes=[pltpu.VMEM((2, n2c * (FFT_N1 + FFT_PAD), 128), F32)],
        compiler_params=_params("parallel"), name="filter_spectrum",
    )(abp, m1, f1)
    fa, fb = spec[:nch // 2], spec[nch // 2:]
    h = FFT_N1
    return jnp.concatenate([fa[:, :, :h] + fb[:, :, :h], fa[:, :, h:] - fb[:, :, h:]], axis=2).astype(BF16)


def _hy_fft_body(zp_ref, z_ref, x1c_ref, kf_ref, m1_ref, f1_ref, g1_ref, g2_ref, skip_ref, o_ref,
                 p_ref, q_ref, y_ref):
    h = FFT_N1
    n2c = g1_ref.shape[0]
    nb = g2_ref.shape[0]
    _fft_stage1(zp_ref, m1_ref, p_ref)

    def mid(k2, c):
        xf = _fft_stage2(p_ref, f1_ref, k2)
        kf = kf_ref[k2].astype(F32)
        xr, xi, kr, ki = xf[:h], xf[h:], kf[:h], kf[h:]
        y = jnp.concatenate([xr * kr - xi * ki, xr * ki + xi * kr], axis=0).astype(BF16)
        d = jnp.dot(g1_ref[k2], y, preferred_element_type=F32)
        q_ref[0, pl.ds(k2, h, stride=n2c + FFT_PAD), :] = d[:h]
        q_ref[1, pl.ds(k2, h, stride=n2c + FFT_PAD), :] = d[h:]
        return c
    lax.fori_loop(0, n2c, mid, 0, unroll=FFT_UNROLL)

    def last(t1, c):
        r0 = pl.multiple_of(t1 * (n2c + FFT_PAD), 8)
        slab = jnp.concatenate([q_ref[0, pl.ds(r0, n2c), :], q_ref[1, pl.ds(r0, n2c), :]], axis=0)
        y_ref[pl.ds(t1, nb, stride=h), :] = jnp.dot(g2_ref[...], slab.astype(BF16), preferred_element_type=F32)
        return c
    lax.fori_loop(0, h, last, 0, unroll=FFT_UNROLL)

    z = z_ref[...].astype(F32)
    o_ref[...] = ((y_ref[...] + z * skip_ref[...]) * x1c_ref[...].astype(F32)).astype(o_ref.dtype)


def hy_fft_conv(zp, z, x1c, kf, tables, skip):
    m1, f1, g1, g2 = tables
    b, s, c = z.shape
    nch = c // 128

    def const(shape):
        return pl.BlockSpec(shape, lambda ci, bi: (0,) * len(shape))

    nat = pl.BlockSpec((None, s, 128), lambda ci, bi: (bi, 0, ci))
    return pl.pallas_call(
        _hy_fft_body, grid=(nch, b),
        in_specs=[pl.BlockSpec((None, None) + zp.shape[2:], lambda ci, bi: (bi, ci, 0, 0)), nat, nat,
                  pl.BlockSpec((None,) + kf.shape[1:], lambda ci, bi: (ci, 0, 0, 0)),
                  const(m1.shape), const(f1.shape), const(g1.shape), const(g2.shape),
                  pl.BlockSpec((1, 128), lambda ci, bi: (0, ci))],
        out_specs=nat,
        out_shape=jax.ShapeDtypeStruct((b, s, c), BF16),
        scratch_shapes=[pltpu.VMEM((2, g1.shape[0] * (FFT_N1 + FFT_PAD), 128), F32),
                        pltpu.VMEM((2, FFT_N1 * (g1.shape[0] + FFT_PAD), 128), F32),
                        pltpu.VMEM((s, 128), F32)],
        compiler_params=_params("parallel", "parallel"), name="hy_fft_conv",
    )(zp, z, x1c, kf, m1, f1, g1, g2, skip.reshape(1, c).astype(F32))


def _ml_prep_body(u_ref, w_ref, b_ref, o_ref, *, scale, transpose):
    y = _conv3(u_ref, w_ref, b_ref)
    y = y * _sigmoid(y) * scale
    o_ref[...] = (y.T if transpose else y).astype(o_ref.dtype)


def ml_prep(u, conv_w, conv_b, col0, ncols, scale, transpose, tc=256):
    b, s, _ = u.shape
    w = conv_w.shape[1]
    c0 = col0 // tc
    if transpose:
        out_spec = pl.BlockSpec((None, tc, s), lambda bi, c: (bi, c, 0))
        out_shape = jax.ShapeDtypeStruct((b, ncols, s), BF16)
    else:
        out_spec = pl.BlockSpec((None, s, tc), lambda bi, c: (bi, 0, c))
        out_shape = jax.ShapeDtypeStruct((b, s, ncols), BF16)
    return pl.pallas_call(
        functools.partial(_ml_prep_body, scale=scale, transpose=transpose), grid=(b, ncols // tc),
        in_specs=[pl.BlockSpec((None, s, tc), lambda bi, c: (bi, 0, c0 + c)),
                  pl.BlockSpec((3, tc), lambda bi, c: (0, c0 + c)),
                  pl.BlockSpec((1, tc), lambda bi, c: (0, c0 + c))],
        out_specs=out_spec, out_shape=out_shape,
        compiler_params=_params("parallel", "parallel"), name="ml_prep",
    )(u, conv_w, conv_b.reshape(1, w))


def _log_sigmoid(x):
    return jnp.minimum(x, 0.0) - jnp.log(1.0 + jnp.exp(-jnp.abs(x)))


def _dot_split(a, b, a_is_f32):
    x = a if a_is_f32 else b
    hi = x.astype(BF16)
    lo = (x - hi.astype(F32)).astype(BF16)
    if a_is_f32:
        return (jnp.dot(hi, b, preferred_element_type=F32) + jnp.dot(lo, b, preferred_element_type=F32))
    return (jnp.dot(a, hi, preferred_element_type=F32) + jnp.dot(a, lo, preferred_element_type=F32))


def _mlstm_chain(q, kt, v, bc, br, li_r, total, mask, c_ref, m_ref, idx):
    dv = v.shape[1] - 128
    c_st = c_ref[idx]
    m_st = m_ref[idx:idx + 1, 0:1]
    key_row = li_r - br
    inter = bc + m_st
    m_t = jnp.maximum(inter, bc + jnp.max(jnp.where(mask, key_row, NEG_BIG), axis=-1, keepdims=True))
    w_intra = jnp.exp(jnp.where(mask, (bc - m_t) + key_row, NEG_BIG))
    w_inter = jnp.exp(inter - m_t)
    sc = jnp.dot(q, kt, preferred_element_type=F32) * w_intra
    q_inter = (q.astype(F32) * w_inter).astype(BF16)
    both = jnp.dot(jnp.concatenate([q_inter, sc.astype(BF16)], axis=1),
                   jnp.concatenate([c_st.astype(BF16), v], axis=0),
                   preferred_element_type=F32)
    den = both[:, dv:dv + 1]
    h = both[:, :dv] / jnp.maximum(jnp.abs(den), jnp.exp(-m_t))
    g_row = total + key_row
    m_next = jnp.maximum(total + m_st, jnp.max(g_row, axis=-1, keepdims=True))
    a_prev = jnp.exp(total + m_st - m_next)
    kwt = (kt.astype(F32) * jnp.exp(g_row - m_next)).astype(BF16)
    c_ref[idx] = a_prev * c_st + jnp.dot(kwt, v, preferred_element_type=F32)
    m_ref[idx:idx + 1, :] = jnp.broadcast_to(m_next, (1, m_ref.shape[1]))
    return h


def _mlstm_body(qf_ref, ktf_ref, vf_ref, gcf_ref, grf_ref, qb_ref, ktb_ref, vb_ref, gcb_ref, grb_ref,
                hf_ref, hb_ref, c_ref, m_ref, *, heads, dk, dv):
    @pl.when(pl.program_id(1) == 0)
    def _():
        c_ref[...] = jnp.zeros_like(c_ref)
        m_ref[...] = jnp.zeros_like(m_ref)

    lc = qf_ref.shape[1]
    ones_blk = jnp.where(lax.broadcasted_iota(I32, (lc, 128), 1) == 0, 1.0, 0.0).astype(BF16)
    t_i = lax.broadcasted_iota(I32, (lc, lc), 0)
    s_i = lax.broadcasted_iota(I32, (lc, lc), 1)
    lower = s_i <= t_i
    upper = s_i >= t_i
    ltri = jnp.where(lower, 1.0, 0.0).astype(BF16)
    utri = jnp.where(upper, 1.0, 0.0).astype(BF16)

    for bb in range(qf_ref.shape[0]):
        for direction, (q_ref, kt_ref, v_ref, gc_ref, gr_ref, h_ref) in enumerate(
                ((qf_ref, ktf_ref, vf_ref, gcf_ref, grf_ref, hf_ref),
                 (qb_ref, ktb_ref, vb_ref, gcb_ref, grb_ref, hb_ref))):
            fwd = direction == 0
            gc = gc_ref[bb]
            gr = gr_ref[bb]
            lfc, lfr = _log_sigmoid(gc), _log_sigmoid(gr)
            cum_c = _dot_split(ltri if fwd else utri, lfc, a_is_f32=False)
            cum_r = _dot_split(lfr, utri if fwd else ltri, a_is_f32=True)
            for hd in range(heads):
                gi = (0 if fwd else 2) * heads + hd
                gf = (1 if fwd else 3) * heads + hd
                bc, br = cum_c[:, gf:gf + 1], cum_r[gf:gf + 1, :]
                total = br[:, lc - 1:lc] if fwd else br[:, 0:1]
                q = q_ref[bb, :, hd * dk:(hd + 1) * dk]
                kt = kt_ref[bb, hd * dk:(hd + 1) * dk, :]
                v = jnp.concatenate([v_ref[bb, :, hd * dv:(hd + 1) * dv], ones_blk], axis=1)
                h = _mlstm_chain(q, kt, v, bc, br, gr[gi:gi + 1, :], total,
                                 lower if fwd else upper, c_ref, m_ref, (bb * 2 + direction) * heads + hd)
                h_ref[bb, :, hd * dv:(hd + 1) * dv] = h.astype(h_ref.dtype)


def mlstm_scan(q, kt, u, v_blk, gcol, grow):
    b, s, w = q.shape
    heads = ML_HEADS
    dk = w // heads
    dv = 2 * dk
    lc = ML_CHUNK
    nc = s // lc
    ng = gcol.shape[-1]

    def fw(bi, j):
        return j

    def bw(bi, j):
        return nc - 1 - j

    nbs = ML_BATCH_PER_STEP if b % ML_BATCH_PER_STEP == 0 else 1

    def specs(pos):
        return [pl.BlockSpec((nbs, lc, w), lambda bi, j: (bi, pos(bi, j), 0)),
                pl.BlockSpec((nbs, w, lc), lambda bi, j: (bi, 0, pos(bi, j))),
                pl.BlockSpec((nbs, lc, heads * dv), lambda bi, j: (bi, pos(bi, j), v_blk)),
                pl.BlockSpec((nbs, lc, ng), lambda bi, j: (bi, pos(bi, j), 0)),
                pl.BlockSpec((nbs, ng, lc), lambda bi, j: (bi, 0, pos(bi, j)))]

    hshape = jax.ShapeDtypeStruct((b, s, heads * dv), BF16)
    return pl.pallas_call(
        functools.partial(_mlstm_body, heads=heads, dk=dk, dv=dv), grid=(b // nbs, nc),
        in_specs=specs(fw) + specs(bw),
        out_specs=[pl.BlockSpec((nbs, lc, heads * dv), lambda bi, j: (bi, j, 0)),
                   pl.BlockSpec((nbs, lc, heads * dv), lambda bi, j: (bi, nc - 1 - j, 0))],
        out_shape=[hshape, hshape],
        scratch_shapes=[pltpu.VMEM((nbs * 2 * heads, dk, dv + 128), F32),
                        pltpu.VMEM((nbs * 2 * heads, 128), F32)],
        compiler_params=_params("parallel", "arbitrary"), name="mlstm_scan",
    )(q, kt, u, gcol, grow, q, kt, u, gcol, grow)


def _mlstm_out_body(res_ref, hf_ref, hb_ref, o_ref, g_ref, w_ref, out_ref, *, heads):
    hs = hf_ref[...].astype(F32) + hb_ref[...].astype(F32)
    dv = hs.shape[1] // heads
    g = g_ref[...]
    parts = []
    for hd in range(heads):
        seg = hs[:, hd * dv:(hd + 1) * dv]
        ms = jnp.mean(seg * seg, axis=-1, keepdims=True)
        parts.append(seg * lax.rsqrt(ms + EPS) * g[:, hd * dv:(hd + 1) * dv])
    a = jnp.concatenate(parts, axis=-1) * _sigmoid(o_ref[...].astype(F32))
    out_ref[...] = res_ref[...] + jnp.dot(a.astype(BF16), w_ref[...], preferred_element_type=F32)


def mlstm_out(res, hf, hb, u2d, o_blk, gain, w_out, tm=1024):
    t, d = res.shape
    row = lambda i: (i, 0)
    return pl.pallas_call(
        functools.partial(_mlstm_out_body, heads=ML_HEADS), grid=(t // tm,),
        in_specs=[pl.BlockSpec((tm, d), row), pl.BlockSpec((tm, d), row), pl.BlockSpec((tm, d), row),
                  pl.BlockSpec((tm, d), lambda i: (i, o_blk)), pl.BlockSpec((1, d), lambda i: (0, 0)),
                  pl.BlockSpec((d, d), lambda i: (0, 0))],
        out_specs=pl.BlockSpec((tm, d), row), out_shape=jax.ShapeDtypeStruct((t, d), F32),
        compiler_params=_params("parallel"), name="mlstm_out",
    )(res, hf, hb, u2d, gain.reshape(1, d), w_out)


ROUTE_ROWS = 128
EXPERT_ROW0 = 8


def _router_body(x_ref, g_ref, wt_ref, b_ref, o_ref, cnt_ref, run_ref):
    @pl.when(pl.program_id(0) == 0)
    def _():
        run_ref[...] = jnp.zeros_like(run_ref)

    x = x_ref[...]
    ms = jnp.mean(x * x, axis=-1, keepdims=True)
    xn = x * lax.rsqrt(ms + EPS) * g_ref[...]
    wt = wt_ref[...]
    w_hi, x_hi = wt.astype(BF16), xn.astype(BF16)
    w_lo, x_lo = (wt - w_hi.astype(F32)).astype(BF16), (xn - x_hi.astype(F32)).astype(BF16)
    nt = (((1,), (1,)), ((), ()))
    logit = (lax.dot_general(w_hi, x_hi, nt, preferred_element_type=F32)
             + lax.dot_general(w_lo, x_hi, nt, preferred_element_type=F32)
             + lax.dot_general(w_hi, x_lo, nt, preferred_element_type=F32)) + b_ref[...]
    rows = [logit[r:r + 1, :] for r in range(EXPERT_ROW0 + N_EXPERTS)]
    g_best, g_idx = rows[0], jnp.zeros_like(rows[0])
    for gi in range(1, N_GROUPS):
        better = rows[gi] > g_best
        g_best = jnp.where(better, rows[gi], g_best)
        g_idx = jnp.where(better, float(gi), g_idx)
    g_den = sum(jnp.exp(rows[gi] - g_best) for gi in range(N_GROUPS))
    g_w = 1.0 / g_den
    sel = []
    for e in range(EXPERTS_PER_GROUP):
        v = rows[EXPERT_ROW0 + e]
        for gi in range(1, N_GROUPS):
            v = jnp.where(g_idx == float(gi), rows[EXPERT_ROW0 + gi * EXPERTS_PER_GROUP + e], v)
        sel.append(v)
    v1, i1 = sel[0], jnp.zeros_like(sel[0])
    for e in range(1, EXPERTS_PER_GROUP):
        better = sel[e] > v1
        v1 = jnp.where(better, sel[e], v1)
        i1 = jnp.where(better, float(e), i1)
    v2, i2 = jnp.full_like(v1, -jnp.inf), jnp.zeros_like(v1)
    for e in range(EXPERTS_PER_GROUP):
        better = (sel[e] > v2) & (i1 != float(e))
        v2 = jnp.where(better, sel[e], v2)
        i2 = jnp.where(better, float(e), i2)
    e21 = jnp.exp(v2 - v1)
    gate1 = g_w / (1.0 + e21)
    gate2 = gate1 * e21
    base = g_idx * float(EXPERTS_PER_GROUP)
    e1, e2 = base + i1, base + i2
    tm = e1.shape[1]
    erow = lax.broadcasted_iota(I32, (N_EXPERTS, tm), 0).astype(F32)
    oh1 = jnp.where(erow == e1, 1.0, 0.0)
    oh2 = jnp.where(erow == e2, 1.0, 0.0)
    cnt = oh1 + oh2
    earlier = jnp.where(lax.broadcasted_iota(I32, (tm, tm), 0) < lax.broadcasted_iota(I32, (tm, tm), 1),
                        1.0, 0.0).astype(BF16)
    run = run_ref[...]
    pos = run[:, 0:1] + jnp.dot(cnt.astype(BF16), earlier, preferred_element_type=F32)
    rank1 = jnp.sum(oh1 * pos, axis=0, keepdims=True)
    rank2 = jnp.sum(oh2 * pos, axis=0, keepdims=True)
    run = run + jnp.sum(cnt, axis=1, keepdims=True)
    run_ref[...] = run
    cnt_ref[...] = run
    zero = jnp.zeros_like(v1)
    o_ref[...] = jnp.concatenate([e1, e2, gate1, gate2, rank1, rank2, zero, zero], axis=0)


def moe_router(x2d, g, wg, bg, we, be, tm=512):
    t, d = x2d.shape
    wt = jnp.zeros((ROUTE_ROWS, d), F32).at[:N_GROUPS].set(wg.T).at[EXPERT_ROW0:EXPERT_ROW0 + N_EXPERTS].set(we.T)
    bias = jnp.zeros((ROUTE_ROWS, 1), F32).at[:N_GROUPS, 0].set(bg).at[EXPERT_ROW0:EXPERT_ROW0 + N_EXPERTS, 0].set(be)
    return pl.pallas_call(
        _router_body, grid=(t // tm,),
        in_specs=[pl.BlockSpec((tm, d), lambda i: (i, 0)), pl.BlockSpec((1, d), lambda i: (0, 0)),
                  pl.BlockSpec((ROUTE_ROWS, d), lambda i: (0, 0)), pl.BlockSpec((ROUTE_ROWS, 1), lambda i: (0, 0))],
        out_specs=[pl.BlockSpec((8, tm), lambda i: (0, i)), pl.BlockSpec((N_EXPERTS, 128), lambda i: (0, 0))],
        out_shape=[jax.ShapeDtypeStruct((8, t), F32), jax.ShapeDtypeStruct((N_EXPERTS, 128), F32)],
        scratch_shapes=[pltpu.VMEM((N_EXPERTS, 128), F32)],
        compiler_params=_params("arbitrary"), name="moe_router",
    )(x2d, g.reshape(1, d), wt, bias)


def _dispatch_body(dest_ref, zblk_ref, x_ref, g_ref, xs_ref, buf_ref, sem_ref, *, td):
    i = pl.program_id(0)
    n = pl.num_programs(0)
    slot = i % 2

    @pl.when(i == 0)
    def _():
        buf_ref[1] = jnp.zeros(buf_ref.shape[1:], buf_ref.dtype)

        def zero_copy(j):
            return pltpu.make_async_copy(buf_ref.at[1], xs_ref.at[pl.ds(zblk_ref[j] * td, td), :], sem_ref.at[1])

        def start(j, c):
            @pl.when(zblk_ref[j] >= 0)
            def _():
                zero_copy(j).start()
            return c

        def wait(j, c):
            @pl.when(zblk_ref[j] >= 0)
            def _():
                zero_copy(j).wait()
            return c
        lax.fori_loop(0, zblk_ref.shape[0], start, 0)
        lax.fori_loop(0, zblk_ref.shape[0], wait, 0)

    def row_copy(r, kk):
        return pltpu.make_async_copy(buf_ref.at[slot, pl.ds(r, 1), :],
                                     xs_ref.at[pl.ds(dest_ref[(i * td + r) * 2 + kk], 1), :],
                                     sem_ref.at[slot])

    def wait_buffer(sl):
        for _ in range(2):
            pltpu.make_async_copy(buf_ref.at[sl], xs_ref.at[pl.ds(0, td), :], sem_ref.at[sl]).wait()

    @pl.when(i >= 2)
    def _():
        wait_buffer(slot)

    x = x_ref[...]
    ms = jnp.mean(x * x, axis=-1, keepdims=True)
    buf_ref[slot] = x * lax.rsqrt(ms + EPS) * g_ref[...]

    for r in range(td):
        row_copy(r, 0).start(priority=0)
        row_copy(r, 1).start(priority=1)

    @pl.when(i == n - 1)
    def _():
        @pl.when(n >= 2)
        def _():
            wait_buffer(1 - slot)
        wait_buffer(slot)


def moe_dispatch(x2d, g, dest, zero_blk, cap, td):
    t, d = x2d.shape
    grid_spec = pltpu.PrefetchScalarGridSpec(
        num_scalar_prefetch=2, grid=(t // td,),
        in_specs=[pl.BlockSpec((td, d), lambda i, dest, zb: (i, 0)),
                  pl.BlockSpec((1, d), lambda i, dest, zb: (0, 0))],
        out_specs=pl.BlockSpec(memory_space=pl.ANY),
        scratch_shapes=[pltpu.VMEM((2, td, d), F32), pltpu.SemaphoreType.DMA((2,))])
    return pl.pallas_call(
        functools.partial(_dispatch_body, td=td), grid_spec=grid_spec,
        out_shape=jax.ShapeDtypeStruct((cap, d), F32),
        compiler_params=_params("arbitrary", disable_bounds_checks=True),
        name="moe_dispatch",
    )(dest, zero_blk, x2d, g.reshape(1, d))


def _expert_body(blk_e_ref, nused_ref, xs_ref, w1_ref, w3_ref, w2_ref, ys_ref, w1b_ref, w3b_ref, w2b_ref):
    i = pl.program_id(0)
    nused = nused_ref[0]
    last = nused - 1
    cur = blk_e_ref[jnp.minimum(i, last)]
    prev = blk_e_ref[jnp.minimum(jnp.maximum(i - 1, 0), last)]

    @pl.when((i == 0) | (cur != prev))
    def _():
        w1b_ref[...] = w1_ref[...].astype(BF16)
        w3b_ref[...] = w3_ref[...].astype(BF16)
        w2b_ref[...] = w2_ref[...].astype(BF16)

    @pl.when(i < nused)
    def _():
        x = xs_ref[...].astype(BF16)
        h1 = jnp.dot(x, w1b_ref[...], preferred_element_type=F32)
        h3 = jnp.dot(x, w3b_ref[...], preferred_element_type=F32)
        hid = (h1 * _sigmoid(h1) * h3).astype(BF16)
        ys_ref[...] = jnp.dot(hid, w2b_ref[...], preferred_element_type=F32)

    @pl.when(i >= nused)
    def _():
        ys_ref[...] = jnp.zeros_like(ys_ref)


def moe_experts(xs, blk_e, nused, w1, w3, w2, layer, tm):
    cap, d = xs.shape
    de = w1.shape[-1]

    def blk(i, be, nu):
        return jnp.minimum(i, nu[0] - 1)

    grid_spec = pltpu.PrefetchScalarGridSpec(
        num_scalar_prefetch=2, grid=(cap // tm,),
        in_specs=[pl.BlockSpec((tm, d), lambda i, be, nu: (blk(i, be, nu), 0)),
                  pl.BlockSpec((None, None, d, de), lambda i, be, nu: (layer, be[blk(i, be, nu)], 0, 0)),
                  pl.BlockSpec((None, None, d, de), lambda i, be, nu: (layer, be[blk(i, be, nu)], 0, 0)),
                  pl.BlockSpec((None, None, de, d), lambda i, be, nu: (layer, be[blk(i, be, nu)], 0, 0))],
        out_specs=pl.BlockSpec((tm, d), lambda i, be, nu: (i, 0)),
        scratch_shapes=[pltpu.VMEM((d, de), BF16), pltpu.VMEM((d, de), BF16), pltpu.VMEM((de, d), BF16)])
    return pl.pallas_call(
        _expert_body, grid_spec=grid_spec, out_shape=jax.ShapeDtypeStruct((cap, d), F32),
        compiler_params=_params("arbitrary"), name="moe_experts",
    )(blk_e, nused, xs, w1, w3, w2)


def _combine_body(dest_ref, x_ref, gate_ref, ys_ref, o_ref, buf_ref, sem_ref, *, tc):
    i = pl.program_id(0)
    n = pl.num_programs(0)
    slot = i % 2

    def row_copy(step, sl, r, kk):
        return pltpu.make_async_copy(ys_ref.at[pl.ds(dest_ref[(step * tc + r) * 2 + kk], 1), :],
                                     buf_ref.at[sl, kk, pl.ds(r, 1), :], sem_ref.at[sl])

    def issue_step(step, sl):
        for r in range(tc):
            row_copy(step, sl, r, 0).start(priority=0)
            row_copy(step, sl, r, 1).start(priority=1)

    @pl.when(i == 0)
    def _():
        issue_step(0, 0)

    @pl.when(i + 1 < n)
    def _():
        issue_step(i + 1, 1 - slot)

    for kk in range(2):
        pltpu.make_async_copy(ys_ref.at[pl.ds(0, tc), :], buf_ref.at[slot, kk], sem_ref.at[slot]).wait()

    gate = gate_ref[...]
    o_ref[...] = x_ref[...] + gate[:, 0:1] * buf_ref[slot, 0] + gate[:, 1:2] * buf_ref[slot, 1]


def moe_combine(x2d, gates, ys, dest, tc=256):
    t, d = x2d.shape
    grid_spec = pltpu.PrefetchScalarGridSpec(
        num_scalar_prefetch=1, grid=(t // tc,),
        in_specs=[pl.BlockSpec((tc, d), lambda i, dest: (i, 0)), pl.BlockSpec((tc, 2), lambda i, dest: (i, 0)),
                  pl.BlockSpec(memory_space=pl.ANY)],
        out_specs=pl.BlockSpec((tc, d), lambda i, dest: (i, 0)),
        scratch_shapes=[pltpu.VMEM((2, 2, tc, d), F32), pltpu.SemaphoreType.DMA((2,))])
    return pl.pallas_call(
        functools.partial(_combine_body, tc=tc), grid_spec=grid_spec,
        out_shape=jax.ShapeDtypeStruct((t, d), F32),
        compiler_params=_params("arbitrary", disable_bounds_checks=True), name="moe_combine",
    )(dest, x2d, gates, ys)


def hier_moe_residual(x2d, g, wg, bg, we, be, w1, w3, w2, layer, tm=512):
    t, d = x2d.shape
    route, counts_b = moe_router(x2d, g, wg, bg, we, be)
    eid = route[0:2].T.astype(I32).reshape(-1)
    rank = route[4:6].T.astype(I32).reshape(-1)
    gates = route[2:4].T
    counts = counts_b[:, 0].astype(I32)
    padded = (counts + tm - 1) // tm * tm
    pad_end = jnp.cumsum(padded)
    pad_start = pad_end - padded
    experts = jnp.arange(N_EXPERTS, dtype=I32)
    dest = rank + jnp.sum(jnp.where(eid[:, None] == experts[None, :], pad_start[None, :], 0), axis=1)
    cap = t * 2 + N_EXPERTS * tm
    nblk = cap // tm
    blk_row0 = jnp.arange(nblk, dtype=I32) * tm
    blk_e = jnp.minimum(jnp.sum((pad_end[None, :] <= blk_row0[:, None]).astype(I32), axis=1), N_EXPERTS - 1)
    nused = (pad_end[-1:] // tm).astype(I32)
    last_blk = jnp.where(padded > 0, pad_end // tm - 1, -1)
    tail_blk = jnp.where(nused[0] + experts < nblk, nused[0] + experts, -1)
    zero_blk = jnp.concatenate([last_blk, tail_blk]).astype(I32)
    xs = moe_dispatch(x2d, g, dest.astype(I32), zero_blk, cap, tm)
    ys = moe_experts(xs, blk_e, nused, w1, w3, w2, layer, tm)
    return moe_combine(x2d, gates, ys, dest.astype(I32))


def _even_layer(x2d, b, s, layer, mix_g, w_in, q_norm, k_norm, lam_q1, lam_k1, lam_q2, lam_k2, subln,
                hy_conv_w, hy_conv_b, f_w1, f_b1, f_freq, f_w2, f_b2, f_w3, hy_skip, w_out):
    d = x2d.shape[1]
    d_att = d // 2
    (u2d,) = norm_matmul(x2d, mix_g, [w_in.astype(BF16)], [BF16])
    u = u2d.reshape(b, s, -1)
    lambda_init = 0.8 - 0.6 * math.exp(-0.3 * layer)
    lam = jnp.exp(jnp.sum(lam_q1 * lam_k1)) - jnp.exp(jnp.sum(lam_q2 * lam_k2)) + lambda_init
    dk = q_norm.shape[0]
    qt = head_prep(u, 0, q_norm, scale=dk ** -0.5 * math.log2(math.e), transpose=True)
    kp = head_prep(u, ATT_HEADS, k_norm)
    vt = head_prep(u, 2 * ATT_HEADS, transpose=True)
    y_att = diff_attention(qt, kp, vt, lam, subln, lambda_init)
    z, x1c, zp = hy_prep(u, 3 * d_att, hy_conv_w, hy_conv_b)
    tables = fft_tables(s)
    kf = filter_spectrum(hyena_filter(s, f_w1, f_b1, f_freq, f_w2, f_b2, f_w3), tables[0], tables[1])
    y_hy = hy_fft_conv(zp, z, x1c, kf, tables, hy_skip)
    w_out = w_out.astype(BF16)
    return matmul_residual(x2d, [y_att.reshape(b * s, -1), y_hy.reshape(b * s, -1)],
                           [w_out[:d_att], w_out[d_att:]])


def _odd_layer(x2d, b, s, mix_g, w_in, conv_w, conv_b, gate_b, out_norm, w_out):
    d = x2d.shape[1]
    qk_w = conv_w.shape[1]
    main_w = qk_w + 2 * d
    ng = 4 * ML_HEADS
    w_gate = jnp.zeros((d, 128), F32).at[:, :ng].set(w_in[:, main_w:]).astype(BF16)
    u2d, ug = norm_matmul(x2d, mix_g, [w_in[:, :main_w].astype(BF16), w_gate], [BF16, F32])
    u = u2d.reshape(b, s, main_w)
    gcol = (ug[:, :ng] + gate_b).reshape(b, s, ng)
    grow = jnp.swapaxes(gcol, 1, 2)
    dk = qk_w // (2 * ML_HEADS)
    q = ml_prep(u, conv_w, conv_b, 0, qk_w // 2, dk ** -0.5, transpose=False)
    kt = ml_prep(u, conv_w, conv_b, qk_w // 2, qk_w // 2, 1.0, transpose=True)
    hf, hb = mlstm_scan(q, kt, u, qk_w // d, gcol, grow)
    return mlstm_out(x2d, hf.reshape(b * s, d), hb.reshape(b * s, d), u2d, (qk_w + d) // d, out_norm,
                     w_out.astype(BF16))


def kernel(x, mix_norm, ffn_norm, ev_w_in, ev_q_norm, ev_k_norm, ev_lam_q1, ev_lam_k1, ev_lam_q2, ev_lam_k2, ev_subln, ev_hy_conv_w, ev_hy_conv_b, ev_hy_f_w1, ev_hy_f_b1, ev_hy_f_freq, ev_hy_f_w2, ev_hy_f_b2, ev_hy_f_w3, ev_hy_skip, ev_w_out, od_w_in, od_conv_w, od_conv_b, od_gate_b, od_out_norm, od_w_out, moe_wg, moe_bg, moe_we, moe_be, moe_w1, moe_w3, moe_w2):
    b, s, d = x.shape
    depth = mix_norm.shape[0]
    x2d = x.reshape(b * s, d)
    for layer in range(depth):
        j = layer // 2
        if layer % 2 == 0:
            x2d = _even_layer(x2d, b, s, layer, mix_norm[layer], ev_w_in[j], ev_q_norm[j], ev_k_norm[j],
                              ev_lam_q1[j], ev_lam_k1[j], ev_lam_q2[j], ev_lam_k2[j], ev_subln[j],
                              ev_hy_conv_w[j], ev_hy_conv_b[j], ev_hy_f_w1[j], ev_hy_f_b1[j], ev_hy_f_freq[j],
                              ev_hy_f_w2[j], ev_hy_f_b2[j], ev_hy_f_w3[j], ev_hy_skip[j], ev_w_out[j])
        else:
            x2d = _odd_layer(x2d, b, s, mix_norm[layer], od_w_in[j], od_conv_w[j], od_conv_b[j], od_gate_b[j],
                             od_out_norm[j], od_w_out[j])
        x2d = hier_moe_residual(x2d, ffn_norm[layer], moe_wg[layer], moe_bg[layer], moe_we[layer],
                                moe_be[layer], moe_w1, moe_w3, moe_w2, layer)
    return x2d.reshape(b, s, d)
```

```python
import functools
import math

import jax
import jax.numpy as jnp
import numpy as np
from jax import lax
from jax.experimental import pallas as pl
from jax.experimental.pallas import tpu as pltpu

F32 = jnp.float32
BF16 = jnp.bfloat16
I32 = jnp.int32

EPS = 1e-6
ROPE_THETA = 500000.0
ATT_HEADS = 4
ML_HEADS = 4
ML_CHUNK = 128
ML_BATCH_PER_STEP = 2
N_GROUPS = 4
EXPERTS_PER_GROUP = 8
N_EXPERTS = N_GROUPS * EXPERTS_PER_GROUP
HY_EMB_BANDS = 16
HY_MIN_DECAY = math.log(1e-2) / 1.5
HY_MAX_DECAY = math.log(1e-2) / 0.3

V7X_VMEM_BYTES = 64 * 1024 * 1024
VMEM_LIMIT = V7X_VMEM_BYTES - 8 * 1024 * 1024
NEG_BIG = -1e30


def _params(*sem, **kw):
    return pltpu.CompilerParams(dimension_semantics=sem, vmem_limit_bytes=VMEM_LIMIT, **kw)


def _sigmoid(x):
    return 1.0 / (1.0 + jnp.exp(-x))


def _norm_matmul_body(x_ref, g_ref, *refs, n_out, col_chunk):
    w_refs, o_refs = refs[:n_out], refs[n_out:]
    x = x_ref[...]
    ms = jnp.mean(x * x, axis=-1, keepdims=True)
    hn = (x * lax.rsqrt(ms + EPS) * g_ref[...]).astype(BF16)
    for w_ref, o_ref in zip(w_refs, o_refs):
        n = w_ref.shape[1]
        for c in range(0, n, col_chunk):
            ce = min(n, c + col_chunk)
            o_ref[:, c:ce] = jnp.dot(hn, w_ref[:, c:ce], preferred_element_type=F32).astype(o_ref.dtype)


def norm_matmul(x2d, g, ws, out_dtypes, tm=1024):
    t, d = x2d.shape
    in_specs = [pl.BlockSpec((tm, d), lambda i: (i, 0)), pl.BlockSpec((1, d), lambda i: (0, 0))]
    in_specs += [pl.BlockSpec(w.shape, lambda i: (0, 0)) for w in ws]
    out_specs = [pl.BlockSpec((tm, w.shape[1]), lambda i: (i, 0)) for w in ws]
    out_shape = [jax.ShapeDtypeStruct((t, w.shape[1]), dt) for w, dt in zip(ws, out_dtypes)]
    return pl.pallas_call(
        functools.partial(_norm_matmul_body, n_out=len(ws), col_chunk=1024),
        grid=(t // tm,), in_specs=in_specs, out_specs=out_specs, out_shape=out_shape,
        compiler_params=_params("parallel"), name="norm_matmul",
    )(x2d, g.reshape(1, d), *ws)


def _matmul_res_body(res_ref, *refs, n_in):
    a_refs, w_refs, o_ref = refs[:n_in], refs[n_in:2 * n_in], refs[2 * n_in]
    acc = res_ref[...]
    for a_ref, w_ref in zip(a_refs, w_refs):
        acc = acc + jnp.dot(a_ref[...], w_ref[...], preferred_element_type=F32)
    o_ref[...] = acc


def matmul_residual(res, a_list, w_list, tm=1024):
    t, d = res.shape
    in_specs = [pl.BlockSpec((tm, d), lambda i: (i, 0))]
    in_specs += [pl.BlockSpec((tm, a.shape[1]), lambda i: (i, 0)) for a in a_list]
    in_specs += [pl.BlockSpec(w.shape, lambda i: (0, 0)) for w in w_list]
    return pl.pallas_call(
        functools.partial(_matmul_res_body, n_in=len(a_list)),
        grid=(t // tm,), in_specs=in_specs, out_specs=pl.BlockSpec((tm, d), lambda i: (i, 0)),
        out_shape=jax.ShapeDtypeStruct((t, d), F32),
        compiler_params=_params("parallel"), name="matmul_residual",
    )(res, *a_list, *w_list)


def _rope_lane_tables(seq, dk, rope_dim, scale):
    half = rope_dim // 2
    f32 = np.float32
    inv_freq = (f32(1.0) / (f32(ROPE_THETA) ** (np.arange(0, rope_dim, 2, dtype=f32) / f32(rope_dim)))).astype(f32)
    ang = np.arange(seq, dtype=f32)[:, None] * inv_freq[None, :]
    cos, sin = np.cos(ang), np.sin(ang)
    d = np.arange(2 * dk) % dk
    fi = d % half
    c_tab = np.where(d[None, :] < rope_dim, cos[:, fi], 1.0)
    s1_tab = np.where(d[None, :] < half, -sin[:, fi], 0.0)
    s2_tab = np.where((d[None, :] >= half) & (d[None, :] < rope_dim), sin[:, fi], 0.0)
    return jnp.asarray((np.stack([c_tab, s1_tab, s2_tab]) * scale).astype(f32))


def _rope_block(x, g, c_tab, s1_tab, s2_tab, dk):
    lane = lax.broadcasted_iota(I32, x.shape, 1)
    lo = lane < dk
    x2 = x * x
    s_lo = jnp.sum(jnp.where(lo, x2, 0.0), axis=-1, keepdims=True)
    s_hi = jnp.sum(jnp.where(lo, 0.0, x2), axis=-1, keepdims=True)
    ms = jnp.where(lo, s_lo, s_hi) * (1.0 / dk)
    y = x * lax.rsqrt(ms + EPS) * g
    return y * c_tab + pltpu.roll(y, 120, 1) * s1_tab + pltpu.roll(y, 8, 1) * s2_tab


def _even_in_body(x_ref, g_ref, w_ref, gq_ref, gk_ref, tq_ref, tk_ref, qt_ref, k_ref, vt_ref, hy_ref, *, heads, dk):
    x = x_ref[...]
    ms = jnp.mean(x * x, axis=-1, keepdims=True)
    hn = (x * lax.rsqrt(ms + EPS) * g_ref[...]).astype(BF16)
    hw = heads * 128
    qk = jnp.dot(hn, w_ref[:, :2 * hw], preferred_element_type=F32)
    for h in range(heads):
        q = _rope_block(qk[:, h * 128:(h + 1) * 128], gq_ref[...], tq_ref[0], tq_ref[1], tq_ref[2], dk)
        qt_ref[h * 128:(h + 1) * 128, :] = q.T.astype(qt_ref.dtype)
        k = _rope_block(qk[:, hw + h * 128:hw + (h + 1) * 128], gk_ref[...], tk_ref[0], tk_ref[1], tk_ref[2], dk)
        k_ref[:, h * 128:(h + 1) * 128] = k.astype(k_ref.dtype)
    v = jnp.dot(hn, w_ref[:, 2 * hw:3 * hw], preferred_element_type=F32)
    for h in range(heads):
        vt_ref[h * 128:(h + 1) * 128, :] = v[:, h * 128:(h + 1) * 128].T.astype(vt_ref.dtype)
    for c in range(0, hy_ref.shape[1], hw):
        hy_ref[:, c:c + hw] = jnp.dot(hn, w_ref[:, 3 * hw + c:3 * hw + c + hw],
                                      preferred_element_type=F32).astype(hy_ref.dtype)


def even_in_proj(x2d, b, s, g, w, q_norm, k_norm, q_scale, tm=512):
    t, d = x2d.shape
    heads = ATT_HEADS
    hw = heads * 128
    dk = q_norm.shape[0]
    assert 2 * dk == 128 and dk // 4 == 16, "rope roll shifts assume 64-wide components, 16 rotary dims"
    nj = s // tm
    n_hy = w.shape[1] - 3 * hw
    tab_q = _rope_lane_tables(s, dk, dk // 4, q_scale)
    tab_k = _rope_lane_tables(s, dk, dk // 4, 1.0)
    row = lambda i: (i, 0)
    const = lambda i: (0, 0)
    tab_spec = pl.BlockSpec((3, tm, 128), lambda i: (0, i % nj, 0))
    return pl.pallas_call(
        functools.partial(_even_in_body, heads=heads, dk=dk), grid=(t // tm,),
        in_specs=[pl.BlockSpec((tm, d), row), pl.BlockSpec((1, d), const), pl.BlockSpec(w.shape, const),
                  pl.BlockSpec((1, 128), const), pl.BlockSpec((1, 128), const), tab_spec, tab_spec],
        out_specs=[pl.BlockSpec((None, hw, tm), lambda i: (i // nj, 0, i % nj)),
                   pl.BlockSpec((None, tm, hw), lambda i: (i // nj, i % nj, 0)),
                   pl.BlockSpec((None, hw, tm), lambda i: (i // nj, 0, i % nj)),
                   pl.BlockSpec((tm, n_hy), row)],
        out_shape=[jax.ShapeDtypeStruct((b, hw, s), BF16), jax.ShapeDtypeStruct((b, s, hw), BF16),
                   jax.ShapeDtypeStruct((b, hw, s), BF16), jax.ShapeDtypeStruct((t, n_hy), BF16)],
        compiler_params=_params("parallel"), name="even_in_proj",
    )(x2d, g.reshape(1, d), w, jnp.tile(q_norm, 2).reshape(1, 128).astype(F32),
      jnp.tile(k_norm, 2).reshape(1, 128).astype(F32), tab_q, tab_k)


def _attn_body(lam_ref, qt_ref, k_ref, vt_ref, g_ref, o_ref, *, tq, dk, post_scale):
    lam = lam_ref[0, 0]
    qt = qt_ref[...]
    row = lax.broadcasted_iota(I32, qt.shape, 0)
    zero = jnp.zeros_like(qt)
    qq = jnp.concatenate([jnp.where(row < dk, qt, zero), jnp.where(row < dk, zero, qt)], axis=1)
    st = jnp.dot(k_ref[...], qq, preferred_element_type=F32)
    m = jnp.max(st, axis=0, keepdims=True)
    p = jnp.exp2(st - m)
    r = 1.0 / jnp.sum(p, axis=0, keepdims=True)
    ot = jnp.dot(vt_ref[...], p.astype(BF16), preferred_element_type=F32)
    o = (ot[:, :tq] * r[:, :tq] - ot[:, tq:] * (lam * r[:, tq:])).T
    ms = jnp.mean(o * o, axis=-1, keepdims=True)
    o_ref[...] = (o * lax.rsqrt(ms + EPS) * g_ref[...] * post_scale).astype(o_ref.dtype)


def diff_attention(qt, k, vt, lam, subln, lambda_init, tq=512):
    b, s, _ = k.shape
    h = ATT_HEADS
    return pl.pallas_call(
        functools.partial(_attn_body, tq=tq, dk=64, post_scale=1.0 - lambda_init),
        grid=(b, h, s // tq),
        in_specs=[pl.BlockSpec(memory_space=pltpu.SMEM),
                  pl.BlockSpec((None, 128, tq), lambda bi, hi, i: (bi, hi, i)),
                  pl.BlockSpec((None, s, 128), lambda bi, hi, i: (bi, 0, hi)),
                  pl.BlockSpec((None, 128, s), lambda bi, hi, i: (bi, hi, 0)),
                  pl.BlockSpec((1, 128), lambda bi, hi, i: (0, 0))],
        out_specs=pl.BlockSpec((None, tq, 128), lambda bi, hi, i: (bi, i, hi)),
        out_shape=jax.ShapeDtypeStruct((b, s, h * 128), BF16),
        compiler_params=_params("parallel", "parallel", "parallel"), name="diff_attention",
    )(lam.reshape(1, 1).astype(F32), qt, k, vt, subln.reshape(1, 128).astype(F32))


def _conv3(u_ref, w_ref, b_ref):
    x = u_ref[...].astype(F32)
    s = x.shape[0]
    row = lax.broadcasted_iota(I32, x.shape, 0)
    x_prev = jnp.where(row == 0, 0.0, pltpu.roll(x, 1, 0))
    x_next = jnp.where(row == s - 1, 0.0, pltpu.roll(x, s - 1, 0))
    w = w_ref[...]
    return b_ref[...] + x_prev * w[0:1] + x * w[1:2] + x_next * w[2:3]


FFT_N1 = 64
FFT_UNROLL = 64
FFT_PAD = 8


def _hy_prep_body(x1_ref, x2_ref, v_ref, w1_ref, w2_ref, wv_ref, b1_ref, b2_ref, bv_ref, z_ref, x1c_ref, zp_ref):
    x1c_ref[...] = _conv3(x1_ref, w1_ref, b1_ref).astype(x1c_ref.dtype)
    z = _conv3(v_ref, wv_ref, bv_ref) * _conv3(x2_ref, w2_ref, b2_ref)
    z_ref[...] = z.astype(z_ref.dtype)
    nb = z.shape[0] // FFT_N1
    zp_ref[...] = jnp.zeros_like(zp_ref)
    for n2 in range(nb):
        for ci in range(z.shape[1] // 128):
            zp_ref[ci, pl.ds(n2, FFT_N1, stride=nb + FFT_PAD), :] = (
                z[n2 * FFT_N1:(n2 + 1) * FFT_N1, ci * 128:(ci + 1) * 128])


def hy_prep(u, col0, conv_w, conv_b, tc=256):
    b, s, _ = u.shape
    d_hy = conv_w.shape[1] // 3
    nct = d_hy // tc
    blk0 = col0 // tc
    sp = FFT_N1 * (s // FFT_N1 + FFT_PAD)

    def uspec(part):
        return pl.BlockSpec((None, s, tc), lambda bi, c: (bi, 0, blk0 + part * nct + c))

    def wspec(part, rows):
        return pl.BlockSpec((rows, tc), lambda bi, c: (0, part * nct + c))

    ospec = pl.BlockSpec((None, s, tc), lambda bi, c: (bi, 0, c))
    return pl.pallas_call(
        _hy_prep_body, grid=(b, nct),
        in_specs=[uspec(0), uspec(1), uspec(2), wspec(0, 3), wspec(1, 3), wspec(2, 3),
                  wspec(0, 1), wspec(1, 1), wspec(2, 1)],
        out_specs=[ospec, ospec, pl.BlockSpec((None, tc // 128, sp, 128), lambda bi, c: (bi, c, 0, 0))],
        out_shape=[jax.ShapeDtypeStruct((b, s, d_hy), BF16), jax.ShapeDtypeStruct((b, s, d_hy), BF16),
                   jax.ShapeDtypeStruct((b, d_hy // 128, sp, 128), F32)],
        compiler_params=_params("parallel", "parallel"), name="hy_prep",
    )(u, u, u, conv_w, conv_w, conv_w, conv_b.reshape(1, -1), conv_b.reshape(1, -1), conv_b.reshape(1, -1))


def hyena_filter(length, w1, b1, freq, w2, b2, w3):
    d_hy = w3.shape[1] // 2
    f32 = np.float32
    t = np.linspace(0.0, 1.0, length, dtype=f32)[:, None]
    bands = np.linspace(1e-4, HY_EMB_BANDS - 1, HY_EMB_BANDS, dtype=f32)[None, :]
    ang = (f32(2.0 * math.pi / length) * np.arange(length, dtype=f32)[:, None] * bands).astype(f32)
    z = jnp.asarray(np.concatenate([t, np.cos(ang), -np.sin(ang)], axis=-1).astype(f32))
    deltas = np.abs(np.linspace(HY_MIN_DECAY, HY_MAX_DECAY, d_hy, dtype=f32))
    decay = jnp.asarray(np.exp(-t * deltas[None, :]).astype(f32))
    hp = lax.Precision.HIGHEST
    hdn = jnp.sin(freq * (jnp.dot(z, w1, precision=hp) + b1))
    hdn = jnp.sin(freq * (jnp.dot(hdn, w2, precision=hp) + b2))
    filt = jnp.dot(hdn, w3, precision=hp)
    h_fwd = filt[:, :d_hy] * decay
    h_bwd = filt[:, d_hy:] * decay
    h_fwd = h_fwd.at[0].add(h_bwd[0])
    h_bwd = h_bwd.at[0].set(0.0)
    norm = jnp.sum(jnp.abs(h_fwd), axis=0, keepdims=True) + jnp.sum(jnp.abs(h_bwd), axis=0, keepdims=True) + EPS
    return jnp.concatenate([h_fwd / norm, h_bwd / norm], axis=1)


def fft_tables(length):
    n = 2 * length
    n1c, n2c, nb = FFT_N1, n // FFT_N1, length // FFT_N1
    unit = 2.0 * math.pi / n
    i1 = np.arange(n1c, dtype=np.int64)
    i2 = np.arange(n2c, dtype=np.int64)
    ib = np.arange(nb, dtype=np.int64)
    samp = i1[:, None, None] + n1c * ib[None, None, :]
    ang = ((i2[None, :, None] * samp) % n) * unit
    m1 = np.concatenate([np.cos(ang), -np.sin(ang)], axis=1)
    ang = ((i1[:, None] * i1[None, :]) % n1c) * (2.0 * math.pi / n1c)
    c, s = np.cos(ang), np.sin(ang)
    f1 = np.concatenate([np.concatenate([c, s], axis=1), np.concatenate([-s, c], axis=1)], axis=0)
    freq = n2c * i1[None, None, :] + i2[:, None, None]
    ang = ((i1[None, :, None] * freq) % n) * unit
    c, s = np.cos(ang), np.sin(ang)
    g1 = np.concatenate([np.concatenate([c, -s], axis=2), np.concatenate([s, c], axis=2)], axis=1)
    ang = ((ib[:, None] * i2[None, :]) % n2c) * (2.0 * math.pi / n2c)
    g2 = np.concatenate([np.cos(ang), -np.sin(ang)], axis=1) * (1.0 / n)
    return tuple(jnp.asarray(t.astype(np.float32).astype(BF16)) for t in (m1, f1, g1, g2))


def _fft_stage1(xp_ref, m1_ref, p_ref):
    nb = m1_ref.shape[2]
    n2c = m1_ref.shape[1] // 2

    def body(n1, c):
        x = xp_ref[pl.ds(pl.multiple_of(n1 * (nb + FFT_PAD), 8), nb), :].astype(BF16)
        a = jnp.dot(m1_ref[n1], x, preferred_element_type=F32)
        p_ref[0, pl.ds(n1, n2c, stride=FFT_N1 + FFT_PAD), :] = a[:n2c]
        p_ref[1, pl.ds(n1, n2c, stride=FFT_N1 + FFT_PAD), :] = a[n2c:]
        return c
    lax.fori_loop(0, FFT_N1, body, 0, unroll=FFT_UNROLL)


def _fft_stage2(p_ref, f1_ref, k2):
    r0 = pl.multiple_of(k2 * (FFT_N1 + FFT_PAD), 8)
    slab = jnp.concatenate([p_ref[0, pl.ds(r0, FFT_N1), :], p_ref[1, pl.ds(r0, FFT_N1), :]], axis=0)
    return jnp.dot(f1_ref[...], slab.astype(BF16), preferred_element_type=F32)


def _spectrum_body(xp_ref, m1_ref, f1_ref, o_ref, p_ref):
    _fft_stage1(xp_ref, m1_ref, p_ref)

    def body(k2, c):
        o_ref[k2] = _fft_stage2(p_ref, f1_ref, k2)
        return c
    lax.fori_loop(0, o_ref.shape[0], body, 0, unroll=FFT_UNROLL)


def filter_spectrum(ab, m1, f1):
    length, c2 = ab.shape
    nb = length // FFT_N1
    n2c = m1.shape[1] // 2
    nch = c2 // 128
    abp = jnp.pad(ab.reshape(nb, FFT_N1, nch, 128).transpose(2, 1, 0, 3), ((0, 0), (0, 0), (0, FFT_PAD), (0, 0)))
    abp = abp.reshape(nch, FFT_N1 * (nb + FFT_PAD), 128)
    spec = pl.pallas_call(
        _spectrum_body, grid=(nch,),
        in_specs=[pl.BlockSpec((None,) + abp.shape[1:], lambda c: (c, 0, 0)),
                  pl.BlockSpec(m1.shape, lambda c: (0, 0, 0)), pl.BlockSpec(f1.shape, lambda c: (0, 0))],
        out_specs=pl.BlockSpec((None, n2c, 2 * FFT_N1, 128), lambda c: (c, 0, 0, 0)),
        out_shape=jax.ShapeDtypeStruct((nch, n2c, 2 * FFT_N1, 128), F32),
        scratch_shapes=[pltpu.VMEM((2, n2c * (FFT_N1 + FFT_PAD), 128), F32)],
        compiler_params=_params("parallel"), name="filter_spectrum",
    )(abp, m1, f1)
    fa, fb = spec[:nch // 2], spec[nch // 2:]
    h = FFT_N1
    return jnp.concatenate([fa[:, :, :h] + fb[:, :, :h], fa[:, :, h:] - fb[:, :, h:]], axis=2).astype(BF16)


def _hy_fft_body(zp_ref, z_ref, x1c_ref, kf_ref, m1_ref, f1_ref, g1_ref, g2_ref, skip_ref, o_ref,
                 p_ref, q_ref, y_ref):
    h = FFT_N1
    n2c = g1_ref.shape[0]
    nb = g2_ref.shape[0]
    _fft_stage1(zp_ref, m1_ref, p_ref)

    def mid(k2, c):
        xf = _fft_stage2(p_ref, f1_ref, k2)
        kf = kf_ref[k2].astype(F32)
        xr, xi, kr, ki = xf[:h], xf[h:], kf[:h], kf[h:]
        y = jnp.concatenate([xr * kr - xi * ki, xr * ki + xi * kr], axis=0).astype(BF16)
        d = jnp.dot(g1_ref[k2], y, preferred_element_type=F32)
        q_ref[0, pl.ds(k2, h, stride=n2c + FFT_PAD), :] = d[:h]
        q_ref[1, pl.ds(k2, h, stride=n2c + FFT_PAD), :] = d[h:]
        return c
    lax.fori_loop(0, n2c, mid, 0, unroll=FFT_UNROLL)

    def last(t1, c):
        r0 = pl.multiple_of(t1 * (n2c + FFT_PAD), 8)
        slab = jnp.concatenate([q_ref[0, pl.ds(r0, n2c), :], q_ref[1, pl.ds(r0, n2c), :]], axis=0)
        y_ref[pl.ds(t1, nb, stride=h), :] = jnp.dot(g2_ref[...], slab.astype(BF16), preferred_element_type=F32)
        return c
    lax.fori_loop(0, h, last, 0, unroll=FFT_UNROLL)

    z = z_ref[...].astype(F32)
    o_ref[...] = ((y_ref[...] + z * skip_ref[...]) * x1c_ref[...].astype(F32)).astype(o_ref.dtype)


def hy_fft_conv(zp, z, x1c, kf, tables, skip):
    m1, f1, g1, g2 = tables
    b, s, c = z.shape
    nch = c // 128

    def const(shape):
        return pl.BlockSpec(shape, lambda ci, bi: (0,) * len(shape))

    nat = pl.BlockSpec((None, s, 128), lambda ci, bi: (bi, 0, ci))
    return pl.pallas_call(
        _hy_fft_body, grid=(nch, b),
        in_specs=[pl.BlockSpec((None, None) + zp.shape[2:], lambda ci, bi: (bi, ci, 0, 0)), nat, nat,
                  pl.BlockSpec((None,) + kf.shape[1:], lambda ci, bi: (ci, 0, 0, 0)),
                  const(m1.shape), const(f1.shape), const(g1.shape), const(g2.shape),
                  pl.BlockSpec((1, 128), lambda ci, bi: (0, ci))],
        out_specs=nat,
        out_shape=jax.ShapeDtypeStruct((b, s, c), BF16),
        scratch_shapes=[pltpu.VMEM((2, g1.shape[0] * (FFT_N1 + FFT_PAD), 128), F32),
                        pltpu.VMEM((2, FFT_N1 * (g1.shape[0] + FFT_PAD), 128), F32),
                        pltpu.VMEM((s, 128), F32)],
        compiler_params=_params("parallel", "parallel"), name="hy_fft_conv",
    )(zp, z, x1c, kf, m1, f1, g1, g2, skip.reshape(1, c).astype(F32))


def _ml_prep_body(u_ref, w_ref, b_ref, o_ref, *, scale, transpose):
    y = _conv3(u_ref, w_ref, b_ref)
    y = y * _sigmoid(y) * scale
    o_ref[...] = (y.T if transpose else y).astype(o_ref.dtype)


def ml_prep(u, conv_w, conv_b, col0, ncols, scale, transpose, tc=256):
    b, s, _ = u.shape
    w = conv_w.shape[1]
    c0 = col0 // tc
    if transpose:
        out_spec = pl.BlockSpec((None, tc, s), lambda bi, c: (bi, c, 0))
        out_shape = jax.ShapeDtypeStruct((b, ncols, s), BF16)
    else:
        out_spec = pl.BlockSpec((None, s, tc), lambda bi, c: (bi, 0, c))
        out_shape = jax.ShapeDtypeStruct((b, s, ncols), BF16)
    return pl.pallas_call(
        functools.partial(_ml_prep_body, scale=scale, transpose=transpose), grid=(b, ncols // tc),
        in_specs=[pl.BlockSpec((None, s, tc), lambda bi, c: (bi, 0, c0 + c)),
                  pl.BlockSpec((3, tc), lambda bi, c: (0, c0 + c)),
                  pl.BlockSpec((1, tc), lambda bi, c: (0, c0 + c))],
        out_specs=out_spec, out_shape=out_shape,
        compiler_params=_params("parallel", "parallel"), name="ml_prep",
    )(u, conv_w, conv_b.reshape(1, w))


def _log_sigmoid(x):
    return jnp.minimum(x, 0.0) - jnp.log(1.0 + jnp.exp(-jnp.abs(x)))


def _dot_split(a, b, a_is_f32):
    x = a if a_is_f32 else b
    hi = x.astype(BF16)
    lo = (x - hi.astype(F32)).astype(BF16)
    if a_is_f32:
        return (jnp.dot(hi, b, preferred_element_type=F32) + jnp.dot(lo, b, preferred_element_type=F32))
    return (jnp.dot(a, hi, preferred_element_type=F32) + jnp.dot(a, lo, preferred_element_type=F32))


def _mlstm_chain(q, kt, v, bc, br, li_r, total, mask, c_ref, m_ref, idx):
    dv = v.shape[1] - 128
    c_st = c_ref[idx]
    m_st = m_ref[idx:idx + 1, 0:1]
    key_row = li_r - br
    inter = bc + m_st
    m_t = jnp.maximum(inter, bc + jnp.max(jnp.where(mask, key_row, NEG_BIG), axis=-1, keepdims=True))
    w_intra = jnp.exp(jnp.where(mask, (bc - m_t) + key_row, NEG_BIG))
    w_inter = jnp.exp(inter - m_t)
    sc = jnp.dot(q, kt, preferred_element_type=F32) * w_intra
    q_inter = (q.astype(F32) * w_inter).astype(BF16)
    both = jnp.dot(jnp.concatenate([q_inter, sc.astype(BF16)], axis=1),
                   jnp.concatenate([c_st.astype(BF16), v], axis=0),
                   preferred_element_type=F32)
    den = both[:, dv:dv + 1]
    h = both[:, :dv] / jnp.maximum(jnp.abs(den), jnp.exp(-m_t))
    g_row = total + key_row
    m_next = jnp.maximum(total + m_st, jnp.max(g_row, axis=-1, keepdims=True))
    a_prev = jnp.exp(total + m_st - m_next)
    kwt = (kt.astype(F32) * jnp.exp(g_row - m_next)).astype(BF16)
    c_ref[idx] = a_prev * c_st + jnp.dot(kwt, v, preferred_element_type=F32)
    m_ref[idx:idx + 1, :] = jnp.broadcast_to(m_next, (1, m_ref.shape[1]))
    return h


def _mlstm_body(qf_ref, ktf_ref, vf_ref, gcf_ref, grf_ref, qb_ref, ktb_ref, vb_ref, gcb_ref, grb_ref,
                hf_ref, hb_ref, c_ref, m_ref, *, heads, dk, dv):
    @pl.when(pl.program_id(1) == 0)
    def _():
        c_ref[...] = jnp.zeros_like(c_ref)
        m_ref[...] = jnp.zeros_like(m_ref)

    lc = qf_ref.shape[1]
    ones_blk = jnp.where(lax.broadcasted_iota(I32, (lc, 128), 1) == 0, 1.0, 0.0).astype(BF16)
    t_i = lax.broadcasted_iota(I32, (lc, lc), 0)
    s_i = lax.broadcasted_iota(I32, (lc, lc), 1)
    lower = s_i <= t_i
    upper = s_i >= t_i
    ltri = jnp.where(lower, 1.0, 0.0).astype(BF16)
    utri = jnp.where(upper, 1.0, 0.0).astype(BF16)

    for bb in range(qf_ref.shape[0]):
        for direction, (q_ref, kt_ref, v_ref, gc_ref, gr_ref, h_ref) in enumerate(
                ((qf_ref, ktf_ref, vf_ref, gcf_ref, grf_ref, hf_ref),
                 (qb_ref, ktb_ref, vb_ref, gcb_ref, grb_ref, hb_ref))):
            fwd = direction == 0
            gc = gc_ref[bb]
            gr = gr_ref[bb]
            lfc, lfr = _log_sigmoid(gc), _log_sigmoid(gr)
            cum_c = _dot_split(ltri if fwd else utri, lfc, a_is_f32=False)
            cum_r = _dot_split(lfr, utri if fwd else ltri, a_is_f32=True)
            for hd in range(heads):
                gi = (0 if fwd else 2) * heads + hd
                gf = (1 if fwd else 3) * heads + hd
                bc, br = cum_c[:, gf:gf + 1], cum_r[gf:gf + 1, :]
                total = br[:, lc - 1:lc] if fwd else br[:, 0:1]
                q = q_ref[bb, :, hd * dk:(hd + 1) * dk]
                kt = kt_ref[bb, hd * dk:(hd + 1) * dk, :]
                v = jnp.concatenate([v_ref[bb, :, hd * dv:(hd + 1) * dv], ones_blk], axis=1)
                h = _mlstm_chain(q, kt, v, bc, br, gr[gi:gi + 1, :], total,
                                 lower if fwd else upper, c_ref, m_ref, (bb * 2 + direction) * heads + hd)
                h_ref[bb, :, hd * dv:(hd + 1) * dv] = h.astype(h_ref.dtype)


def mlstm_scan(q, kt, u, v_blk, gcol, grow):
    b, s, w = q.shape
    heads = ML_HEADS
    dk = w // heads
    dv = 2 * dk
    lc = ML_CHUNK
    nc = s // lc
    ng = gcol.shape[-1]

    def fw(bi, j):
        return j

    def bw(bi, j):
        return nc - 1 - j

    nbs = ML_BATCH_PER_STEP if b % ML_BATCH_PER_STEP == 0 else 1

    def specs(pos):
        return [pl.BlockSpec((nbs, lc, w), lambda bi, j: (bi, pos(bi, j), 0)),
                pl.BlockSpec((nbs, w, lc), lambda bi, j: (bi, 0, pos(bi, j))),
                pl.BlockSpec((nbs, lc, heads * dv), lambda bi, j: (bi, pos(bi, j), v_blk)),
                pl.BlockSpec((nbs, lc, ng), lambda bi, j: (bi, pos(bi, j), 0)),
                pl.BlockSpec((nbs, ng, lc), lambda bi, j: (bi, 0, pos(bi, j)))]

    hshape = jax.ShapeDtypeStruct((b, s, heads * dv), BF16)
    return pl.pallas_call(
        functools.partial(_mlstm_body, heads=heads, dk=dk, dv=dv), grid=(b // nbs, nc),
        in_specs=specs(fw) + specs(bw),
        out_specs=[pl.BlockSpec((nbs, lc, heads * dv), lambda bi, j: (bi, j, 0)),
                   pl.BlockSpec((nbs, lc, heads * dv), lambda bi, j: (bi, nc - 1 - j, 0))],
        out_shape=[hshape, hshape],
        scratch_shapes=[pltpu.VMEM((nbs * 2 * heads, dk, dv + 128), F32),
                        pltpu.VMEM((nbs * 2 * heads, 128), F32)],
        compiler_params=_params("parallel", "arbitrary"), name="mlstm_scan",
    )(q, kt, u, gcol, grow, q, kt, u, gcol, grow)


def _mlstm_out_body(res_ref, hf_ref, hb_ref, o_ref, g_ref, w_ref, out_ref, *, heads):
    hs = hf_ref[...].astype(F32) + hb_ref[...].astype(F32)
    dv = hs.shape[1] // heads
    g = g_ref[...]
    parts = []
    for hd in range(heads):
        seg = hs[:, hd * dv:(hd + 1) * dv]
        ms = jnp.mean(seg * seg, axis=-1, keepdims=True)
        parts.append(seg * lax.rsqrt(ms + EPS) * g[:, hd * dv:(hd + 1) * dv])
    a = jnp.concatenate(parts, axis=-1) * _sigmoid(o_ref[...].astype(F32))
    out_ref[...] = res_ref[...] + jnp.dot(a.astype(BF16), w_ref[...], preferred_element_type=F32)


def mlstm_out(res, hf, hb, u2d, o_blk, gain, w_out, tm=1024):
    t, d = res.shape
    row = lambda i: (i, 0)
    return pl.pallas_call(
        functools.partial(_mlstm_out_body, heads=ML_HEADS), grid=(t // tm,),
        in_specs=[pl.BlockSpec((tm, d), row), pl.BlockSpec((tm, d), row), pl.BlockSpec((tm, d), row),
                  pl.BlockSpec((tm, d), lambda i: (i, o_blk)), pl.BlockSpec((1, d), lambda i: (0, 0)),
                  pl.BlockSpec((d, d), lambda i: (0, 0))],
        out_specs=pl.BlockSpec((tm, d), row), out_shape=jax.ShapeDtypeStruct((t, d), F32),
        compiler_params=_params("parallel"), name="mlstm_out",
    )(res, hf, hb, u2d, gain.reshape(1, d), w_out)


ROUTE_ROWS = 128
EXPERT_ROW0 = 8


def _router_body(x_ref, g_ref, wt_ref, b_ref, o_ref, cnt_ref, run_ref):
    @pl.when(pl.program_id(0) == 0)
    def _():
        run_ref[...] = jnp.zeros_like(run_ref)

    x = x_ref[...]
    ms = jnp.mean(x * x, axis=-1, keepdims=True)
    xn = x * lax.rsqrt(ms + EPS) * g_ref[...]
    wt = wt_ref[...]
    w_hi, x_hi = wt.astype(BF16), xn.astype(BF16)
    w_lo, x_lo = (wt - w_hi.astype(F32)).astype(BF16), (xn - x_hi.astype(F32)).astype(BF16)
    nt = (((1,), (1,)), ((), ()))
    logit = (lax.dot_general(w_hi, x_hi, nt, preferred_element_type=F32)
             + lax.dot_general(w_lo, x_hi, nt, preferred_element_type=F32)
             + lax.dot_general(w_hi, x_lo, nt, preferred_element_type=F32)) + b_ref[...]
    rows = [logit[r:r + 1, :] for r in range(EXPERT_ROW0 + N_EXPERTS)]
    g_best, g_idx = rows[0], jnp.zeros_like(rows[0])
    for gi in range(1, N_GROUPS):
        better = rows[gi] > g_best
        g_best = jnp.where(better, rows[gi], g_best)
        g_idx = jnp.where(better, float(gi), g_idx)
    g_den = sum(jnp.exp(rows[gi] - g_best) for gi in range(N_GROUPS))
    g_w = 1.0 / g_den
    sel = []
    for e in range(EXPERTS_PER_GROUP):
        v = rows[EXPERT_ROW0 + e]
        for gi in range(1, N_GROUPS):
            v = jnp.where(g_idx == float(gi), rows[EXPERT_ROW0 + gi * EXPERTS_PER_GROUP + e], v)
        sel.append(v)
    v1, i1 = sel[0], jnp.zeros_like(sel[0])
    for e in range(1, EXPERTS_PER_GROUP):
        better = sel[e] > v1
        v1 = jnp.where(better, sel[e], v1)
        i1 = jnp.where(better, float(e), i1)
    v2, i2 = jnp.full_like(v1, -jnp.inf), jnp.zeros_like(v1)
    for e in range(EXPERTS_PER_GROUP):
        better = (sel[e] > v2) & (i1 != float(e))
        v2 = jnp.where(better, sel[e], v2)
        i2 = jnp.where(better, float(e), i2)
    e21 = jnp.exp(v2 - v1)
    gate1 = g_w / (1.0 + e21)
    gate2 = gate1 * e21
    base = g_idx * float(EXPERTS_PER_GROUP)
    e1, e2 = base + i1, base + i2
    tm = e1.shape[1]
    erow = lax.broadcasted_iota(I32, (N_EXPERTS, tm), 0).astype(F32)
    oh1 = jnp.where(erow == e1, 1.0, 0.0)
    oh2 = jnp.where(erow == e2, 1.0, 0.0)
    cnt = oh1 + oh2
    earlier = jnp.where(lax.broadcasted_iota(I32, (tm, tm), 0) < lax.broadcasted_iota(I32, (tm, tm), 1),
                        1.0, 0.0).astype(BF16)
    run = run_ref[...]
    pos = run[:, 0:1] + jnp.dot(cnt.astype(BF16), earlier, preferred_element_type=F32)
    rank1 = jnp.sum(oh1 * pos, axis=0, keepdims=True)
    rank2 = jnp.sum(oh2 * pos, axis=0, keepdims=True)
    run = run + jnp.sum(cnt, axis=1, keepdims=True)
    run_ref[...] = run
    cnt_ref[...] = run
    zero = jnp.zeros_like(v1)
    o_ref[...] = jnp.concatenate([e1, e2, gate1, gate2, rank1, rank2, zero, zero], axis=0)


def moe_router(x2d, g, wg, bg, we, be, tm=512):
    t, d = x2d.shape
    wt = jnp.zeros((ROUTE_ROWS, d), F32).at[:N_GROUPS].set(wg.T).at[EXPERT_ROW0:EXPERT_ROW0 + N_EXPERTS].set(we.T)
    bias = jnp.zeros((ROUTE_ROWS, 1), F32).at[:N_GROUPS, 0].set(bg).at[EXPERT_ROW0:EXPERT_ROW0 + N_EXPERTS, 0].set(be)
    return pl.pallas_call(
        _router_body, grid=(t // tm,),
        in_specs=[pl.BlockSpec((tm, d), lambda i: (i, 0)), pl.BlockSpec((1, d), lambda i: (0, 0)),
                  pl.BlockSpec((ROUTE_ROWS, d), lambda i: (0, 0)), pl.BlockSpec((ROUTE_ROWS, 1), lambda i: (0, 0))],
        out_specs=[pl.BlockSpec((8, tm), lambda i: (0, i)), pl.BlockSpec((N_EXPERTS, 128), lambda i: (0, 0))],
        out_shape=[jax.ShapeDtypeStruct((8, t), F32), jax.ShapeDtypeStruct((N_EXPERTS, 128), F32)],
        scratch_shapes=[pltpu.VMEM((N_EXPERTS, 128), F32)],
        compiler_params=_params("arbitrary"), name="moe_router",
    )(x2d, g.reshape(1, d), wt, bias)


def _dispatch_body(dest_ref, zblk_ref, x_ref, g_ref, xs_ref, buf_ref, sem_ref, *, td):
    i = pl.program_id(0)
    n = pl.num_programs(0)
    slot = i % 2

    @pl.when(i == 0)
    def _():
        buf_ref[1] = jnp.zeros(buf_ref.shape[1:], buf_ref.dtype)

        def zero_copy(j):
            return pltpu.make_async_copy(buf_ref.at[1], xs_ref.at[pl.ds(zblk_ref[j] * td, td), :], sem_ref.at[1])

        def start(j, c):
            @pl.when(zblk_ref[j] >= 0)
            def _():
                zero_copy(j).start()
            return c

        def wait(j, c):
            @pl.when(zblk_ref[j] >= 0)
            def _():
                zero_copy(j).wait()
            return c
        lax.fori_loop(0, zblk_ref.shape[0], start, 0)
        lax.fori_loop(0, zblk_ref.shape[0], wait, 0)

    def row_copy(r, kk):
        return pltpu.make_async_copy(buf_ref.at[slot, pl.ds(r, 1), :],
                                     xs_ref.at[pl.ds(dest_ref[(i * td + r) * 2 + kk], 1), :],
                                     sem_ref.at[slot])

    def wait_buffer(sl):
        for _ in range(2):
            pltpu.make_async_copy(buf_ref.at[sl], xs_ref.at[pl.ds(0, td), :], sem_ref.at[sl]).wait()

    @pl.when(i >= 2)
    def _():
        wait_buffer(slot)

    x = x_ref[...]
    ms = jnp.mean(x * x, axis=-1, keepdims=True)
    buf_ref[slot] = x * lax.rsqrt(ms + EPS) * g_ref[...]

    for r in range(td):
        row_copy(r, 0).start(priority=0)
        row_copy(r, 1).start(priority=1)

    @pl.when(i == n - 1)
    def _():
        @pl.when(n >= 2)
        def _():
            wait_buffer(1 - slot)
        wait_buffer(slot)


def moe_dispatch(x2d, g, dest, zero_blk, cap, td):
    t, d = x2d.shape
    grid_spec = pltpu.PrefetchScalarGridSpec(
        num_scalar_prefetch=2, grid=(t // td,),
        in_specs=[pl.BlockSpec((td, d), lambda i, dest, zb: (i, 0)),
                  pl.BlockSpec((1, d), lambda i, dest, zb: (0, 0))],
        out_specs=pl.BlockSpec(memory_space=pl.ANY),
        scratch_shapes=[pltpu.VMEM((2, td, d), F32), pltpu.SemaphoreType.DMA((2,))])
    return pl.pallas_call(
        functools.partial(_dispatch_body, td=td), grid_spec=grid_spec,
        out_shape=jax.ShapeDtypeStruct((cap, d), F32),
        compiler_params=_params("arbitrary", disable_bounds_checks=True),
        name="moe_dispatch",
    )(dest, zero_blk, x2d, g.reshape(1, d))


def _expert_body(blk_e_ref, nused_ref, xs_ref, w1_ref, w3_ref, w2_ref, ys_ref, w1b_ref, w3b_ref, w2b_ref):
    i = pl.program_id(0)
    nused = nused_ref[0]
    last = nused - 1
    cur = blk_e_ref[jnp.minimum(i, last)]
    prev = blk_e_ref[jnp.minimum(jnp.maximum(i - 1, 0), last)]

    @pl.when((i == 0) | (cur != prev))
    def _():
        w1b_ref[...] = w1_ref[...].astype(BF16)
        w3b_ref[...] = w3_ref[...].astype(BF16)
        w2b_ref[...] = w2_ref[...].astype(BF16)

    @pl.when(i < nused)
    def _():
        x = xs_ref[...].astype(BF16)
        h1 = jnp.dot(x, w1b_ref[...], preferred_element_type=F32)
        h3 = jnp.dot(x, w3b_ref[...], preferred_element_type=F32)
        hid = (h1 * _sigmoid(h1) * h3).astype(BF16)
        ys_ref[...] = jnp.dot(hid, w2b_ref[...], preferred_element_type=F32)

    @pl.when(i >= nused)
    def _():
        ys_ref[...] = jnp.zeros_like(ys_ref)


def moe_experts(xs, blk_e, nused, w1, w3, w2, layer, tm):
    cap, d = xs.shape
    de = w1.shape[-1]

    def blk(i, be, nu):
        return jnp.minimum(i, nu[0] - 1)

    grid_spec = pltpu.PrefetchScalarGridSpec(
        num_scalar_prefetch=2, grid=(cap // tm,),
        in_specs=[pl.BlockSpec((tm, d), lambda i, be, nu: (blk(i, be, nu), 0)),
                  pl.BlockSpec((None, None, d, de), lambda i, be, nu: (layer, be[blk(i, be, nu)], 0, 0)),
                  pl.BlockSpec((None, None, d, de), lambda i, be, nu: (layer, be[blk(i, be, nu)], 0, 0)),
                  pl.BlockSpec((None, None, de, d), lambda i, be, nu: (layer, be[blk(i, be, nu)], 0, 0))],
        out_specs=pl.BlockSpec((tm, d), lambda i, be, nu: (i, 0)),
        scratch_shapes=[pltpu.VMEM((d, de), BF16), pltpu.VMEM((d, de), BF16), pltpu.VMEM((de, d), BF16)])
    return pl.pallas_call(
        _expert_body, grid_spec=grid_spec, out_shape=jax.ShapeDtypeStruct((cap, d), F32),
        compiler_params=_params("arbitrary"), name="moe_experts",
    )(blk_e, nused, xs, w1, w3, w2)


def _combine_body(dest_ref, x_ref, gate_ref, ys_ref, o_ref, buf_ref, sem_ref, *, tc):
    i = pl.program_id(0)
    n = pl.num_programs(0)
    slot = i % 2

    def row_copy(step, sl, r, kk):
        return pltpu.make_async_copy(ys_ref.at[pl.ds(dest_ref[(step * tc + r) * 2 + kk], 1), :],
                                     buf_ref.at[sl, kk, pl.ds(r, 1), :], sem_ref.at[sl])

    def issue_step(step, sl):
        for r in range(tc):
            row_copy(step, sl, r, 0).start(priority=0)
            row_copy(step, sl, r, 1).start(priority=1)

    @pl.when(i == 0)
    def _():
        issue_step(0, 0)

    @pl.when(i + 1 < n)
    def _():
        issue_step(i + 1, 1 - slot)

    for kk in range(2):
        pltpu.make_async_copy(ys_ref.at[pl.ds(0, tc), :], buf_ref.at[slot, kk], sem_ref.at[slot]).wait()

    gate = gate_ref[...]
    o_ref[...] = x_ref[...] + gate[:, 0:1] * buf_ref[slot, 0] + gate[:, 1:2] * buf_ref[slot, 1]


def moe_combine(x2d, gates, ys, dest, tc=256):
    t, d = x2d.shape
    grid_spec = pltpu.PrefetchScalarGridSpec(
        num_scalar_prefetch=1, grid=(t // tc,),
        in_specs=[pl.BlockSpec((tc, d), lambda i, dest: (i, 0)), pl.BlockSpec((tc, 2), lambda i, dest: (i, 0)),
                  pl.BlockSpec(memory_space=pl.ANY)],
        out_specs=pl.BlockSpec((tc, d), lambda i, dest: (i, 0)),
        scratch_shapes=[pltpu.VMEM((2, 2, tc, d), F32), pltpu.SemaphoreType.DMA((2,))])
    return pl.pallas_call(
        functools.partial(_combine_body, tc=tc), grid_spec=grid_spec,
        out_shape=jax.ShapeDtypeStruct((t, d), F32),
        compiler_params=_params("arbitrary", disable_bounds_checks=True), name="moe_combine",
    )(dest, x2d, gates, ys)


def hier_moe_residual(x2d, g, wg, bg, we, be, w1, w3, w2, layer, tm=512):
    t, d = x2d.shape
    route, counts_b = moe_router(x2d, g, wg, bg, we, be)
    eid = route[0:2].T.astype(I32).reshape(-1)
    rank = route[4:6].T.astype(I32).reshape(-1)
    gates = route[2:4].T
    counts = counts_b[:, 0].astype(I32)
    padded = (counts + tm - 1) // tm * tm
    pad_end = jnp.cumsum(padded)
    pad_start = pad_end - padded
    experts = jnp.arange(N_EXPERTS, dtype=I32)
    dest = rank + jnp.sum(jnp.where(eid[:, None] == experts[None, :], pad_start[None, :], 0), axis=1)
    cap = t * 2 + N_EXPERTS * tm
    nblk = cap // tm
    blk_row0 = jnp.arange(nblk, dtype=I32) * tm
    blk_e = jnp.minimum(jnp.sum((pad_end[None, :] <= blk_row0[:, None]).astype(I32), axis=1), N_EXPERTS - 1)
    nused = (pad_end[-1:] // tm).astype(I32)
    last_blk = jnp.where(padded > 0, pad_end // tm - 1, -1)
    tail_blk = jnp.where(nused[0] + experts < nblk, nused[0] + experts, -1)
    zero_blk = jnp.concatenate([last_blk, tail_blk]).astype(I32)
    xs = moe_dispatch(x2d, g, dest.astype(I32), zero_blk, cap, tm)
    ys = moe_experts(xs, blk_e, nused, w1, w3, w2, layer, tm)
    return moe_combine(x2d, gates, ys, dest.astype(I32))


def _even_layer(x2d, b, s, layer, mix_g, w_in, q_norm, k_norm, lam_q1, lam_k1, lam_q2, lam_k2, subln,
                hy_conv_w, hy_conv_b, f_w1, f_b1, f_freq, f_w2, f_b2, f_w3, hy_skip, w_out):
    d = x2d.shape[1]
    d_att = d // 2
    lambda_init = 0.8 - 0.6 * math.exp(-0.3 * layer)
    lam = jnp.exp(jnp.sum(lam_q1 * lam_k1)) - jnp.exp(jnp.sum(lam_q2 * lam_k2)) + lambda_init
    dk = q_norm.shape[0]
    qt, kp, vt, u_hy = even_in_proj(x2d, b, s, mix_g, w_in.astype(BF16), q_norm, k_norm,
                                    dk ** -0.5 * math.log2(math.e))
    y_att = diff_attention(qt, kp, vt, lam, subln, lambda_init)
    z, x1c, zp = hy_prep(u_hy.reshape(b, s, -1), 0, hy_conv_w, hy_conv_b)
    tables = fft_tables(s)
    kf = filter_spectrum(hyena_filter(s, f_w1, f_b1, f_freq, f_w2, f_b2, f_w3), tables[0], tables[1])
    y_hy = hy_fft_conv(zp, z, x1c, kf, tables, hy_skip)
    w_out = w_out.astype(BF16)
    return matmul_residual(x2d, [y_att.reshape(b * s, -1), y_hy.reshape(b * s, -1)],
                           [w_out[:d_att], w_out[d_att:]])


def _odd_layer(x2d, b, s, mix_g, w_in, conv_w, conv_b, gate_b, out_norm, w_out):
    d = x2d.shape[1]
    qk_w = conv_w.shape[1]
    main_w = qk_w + 2 * d
    ng = 4 * ML_HEADS
    w_gate = jnp.zeros((d, 128), F32).at[:, :ng].set(w_in[:, main_w:]).astype(BF16)
    u2d, ug = norm_matmul(x2d, mix_g, [w_in[:, :main_w].astype(BF16), w_gate], [BF16, F32])
    u = u2d.reshape(b, s, main_w)
    gcol = (ug[:, :ng] + gate_b).reshape(b, s, ng)
    grow = jnp.swapaxes(gcol, 1, 2)
    dk = qk_w // (2 * ML_HEADS)
    q = ml_prep(u, conv_w, conv_b, 0, qk_w // 2, dk ** -0.5, transpose=False)
    kt = ml_prep(u, conv_w, conv_b, qk_w // 2, qk_w // 2, 1.0, transpose=True)
    hf, hb = mlstm_scan(q, kt, u, qk_w // d, gcol, grow)
    return mlstm_out(x2d, hf.reshape(b * s, d), hb.reshape(b * s, d), u2d, (qk_w + d) // d, out_norm,
                     w_out.astype(BF16))


def kernel(x, mix_norm, ffn_norm, ev_w_in, ev_q_norm, ev_k_norm, ev_lam_q1, ev_lam_k1, ev_lam_q2, ev_lam_k2, ev_subln, ev_hy_conv_w, ev_hy_conv_b, ev_hy_f_w1, ev_hy_f_b1, ev_hy_f_freq, ev_hy_f_w2, ev_hy_f_b2, ev_hy_f_w3, ev_hy_skip, ev_w_out, od_w_in, od_conv_w, od_conv_b, od_gate_b, od_out_norm, od_w_out, moe_wg, moe_bg, moe_we, moe_be, moe_w1, moe_w3, moe_w2):
    b, s, d = x.shape
    depth = mix_norm.shape[0]
    x2d = x.reshape(b * s, d)
    for layer in range(depth):
        j = layer // 2
        if layer % 2 == 0:
            x2d = _even_layer(x2d, b, s, layer, mix_norm[layer], ev_w_in[j], ev_q_norm[j], ev_k_norm[j],
                              ev_lam_q1[j], ev_lam_k1[j], ev_lam_q2[j], ev_lam_k2[j], ev_subln[j],
                              ev_hy_conv_w[j], ev_hy_conv_b[j], ev_hy_f_w1[j], ev_hy_f_b1[j], ev_hy_f_freq[j],
                              ev_hy_f_w2[j], ev_hy_f_b2[j], ev_hy_f_w3[j], ev_hy_skip[j], ev_w_out[j])
        else:
            x2d = _odd_layer(x2d, b, s, mix_norm[layer], od_w_in[j], od_conv_w[j], od_conv_b[j], od_gate_b[j],
                             od_out_norm[j], od_w_out[j])
        x2d = hier_moe_residual(x2d, ffn_norm[layer], moe_wg[layer], moe_bg[layer], moe_we[layer],
                                moe_be[layer], moe_w1, moe_w3, moe_w2, layer)
    return x2d.reshape(b, s, d)
```

```python
import functools
import math

import jax
import jax.numpy as jnp
import numpy as np
from jax import lax
from jax.experimental import pallas as pl
from jax.experimental.pallas import tpu as pltpu

F32 = jnp.float32
BF16 = jnp.bfloat16
I32 = jnp.int32

EPS = 1e-6
ROPE_THETA = 500000.0
ATT_HEADS = 4
ML_HEADS = 4
ML_CHUNK = 128
ML_BATCH_PER_STEP = 2
N_GROUPS = 4
EXPERTS_PER_GROUP = 8
N_EXPERTS = N_GROUPS * EXPERTS_PER_GROUP
HY_EMB_BANDS = 16
HY_MIN_DECAY = math.log(1e-2) / 1.5
HY_MAX_DECAY = math.log(1e-2) / 0.3

V7X_VMEM_BYTES = 64 * 1024 * 1024
VMEM_LIMIT = V7X_VMEM_BYTES - 8 * 1024 * 1024
NEG_BIG = -1e30


def _params(*sem, **kw):
    return pltpu.CompilerParams(dimension_semantics=sem, vmem_limit_bytes=VMEM_LIMIT, **kw)


def _sigmoid(x):
    return 1.0 / (1.0 + jnp.exp(-x))


def _norm_matmul_body(x_ref, g_ref, *refs, n_out, col_chunk):
    w_refs, o_refs = refs[:n_out], refs[n_out:]
    x = x_ref[...]
    ms = jnp.mean(x * x, axis=-1, keepdims=True)
    hn = (x * lax.rsqrt(ms + EPS) * g_ref[...]).astype(BF16)
    for w_ref, o_ref in zip(w_refs, o_refs):
        n = w_ref.shape[1]
        for c in range(0, n, col_chunk):
            ce = min(n, c + col_chunk)
            o_ref[:, c:ce] = jnp.dot(hn, w_ref[:, c:ce], preferred_element_type=F32).astype(o_ref.dtype)


def norm_matmul(x2d, g, ws, out_dtypes, tm=1024):
    t, d = x2d.shape
    in_specs = [pl.BlockSpec((tm, d), lambda i: (i, 0)), pl.BlockSpec((1, d), lambda i: (0, 0))]
    in_specs += [pl.BlockSpec(w.shape, lambda i: (0, 0)) for w in ws]
    out_specs = [pl.BlockSpec((tm, w.shape[1]), lambda i: (i, 0)) for w in ws]
    out_shape = [jax.ShapeDtypeStruct((t, w.shape[1]), dt) for w, dt in zip(ws, out_dtypes)]
    return pl.pallas_call(
        functools.partial(_norm_matmul_body, n_out=len(ws), col_chunk=1024),
        grid=(t // tm,), in_specs=in_specs, out_specs=out_specs, out_shape=out_shape,
        compiler_params=_params("parallel"), name="norm_matmul",
    )(x2d, g.reshape(1, d), *ws)


def _matmul_res_body(res_ref, *refs, n_in):
    a_refs, w_refs = refs[:n_in], refs[n_in:2 * n_in]
    g_ref, wt_ref, b_ref, o_ref, route_ref, cnt_ref, run_ref = refs[2 * n_in:]
    acc = res_ref[...]
    for a_ref, w_ref in zip(a_refs, w_refs):
        acc = acc + jnp.dot(a_ref[...], w_ref[...], preferred_element_type=F32)
    o_ref[...] = acc
    _route_epilogue(acc, g_ref, wt_ref, b_ref, route_ref, cnt_ref, run_ref)


def matmul_residual(res, a_list, w_list, route_params, tm=1024):
    t, d = res.shape
    r_in, r_args, r_out, r_shapes, r_scratch = _route_plumbing(t, d, tm, *route_params)
    in_specs = [pl.BlockSpec((tm, d), lambda i: (i, 0))]
    in_specs += [pl.BlockSpec((tm, a.shape[1]), lambda i: (i, 0)) for a in a_list]
    in_specs += [pl.BlockSpec(w.shape, lambda i: (0, 0)) for w in w_list]
    return pl.pallas_call(
        functools.partial(_matmul_res_body, n_in=len(a_list)),
        grid=(t // tm,), in_specs=in_specs + r_in,
        out_specs=[pl.BlockSpec((tm, d), lambda i: (i, 0))] + r_out,
        out_shape=[jax.ShapeDtypeStruct((t, d), F32)] + r_shapes, scratch_shapes=r_scratch,
        compiler_params=_params("arbitrary"), name="matmul_residual",
    )(res, *a_list, *w_list, *r_args)


def _rope_lane_tables(seq, dk, rope_dim, scale):
    half = rope_dim // 2
    f32 = np.float32
    inv_freq = (f32(1.0) / (f32(ROPE_THETA) ** (np.arange(0, rope_dim, 2, dtype=f32) / f32(rope_dim)))).astype(f32)
    ang = np.arange(seq, dtype=f32)[:, None] * inv_freq[None, :]
    cos, sin = np.cos(ang), np.sin(ang)
    d = np.arange(2 * dk) % dk
    fi = d % half
    c_tab = np.where(d[None, :] < rope_dim, cos[:, fi], 1.0)
    s1_tab = np.where(d[None, :] < half, -sin[:, fi], 0.0)
    s2_tab = np.where((d[None, :] >= half) & (d[None, :] < rope_dim), sin[:, fi], 0.0)
    return jnp.asarray((np.stack([c_tab, s1_tab, s2_tab]) * scale).astype(f32))


def _rope_block(x, g, c_tab, s1_tab, s2_tab, dk):
    lane = lax.broadcasted_iota(I32, x.shape, 1)
    lo = lane < dk
    x2 = x * x
    s_lo = jnp.sum(jnp.where(lo, x2, 0.0), axis=-1, keepdims=True)
    s_hi = jnp.sum(jnp.where(lo, 0.0, x2), axis=-1, keepdims=True)
    ms = jnp.where(lo, s_lo, s_hi) * (1.0 / dk)
    y = x * lax.rsqrt(ms + EPS) * g
    return y * c_tab + pltpu.roll(y, 120, 1) * s1_tab + pltpu.roll(y, 8, 1) * s2_tab


def _even_in_body(x_ref, g_ref, w_ref, gq_ref, gk_ref, tq_ref, tk_ref, qt_ref, k_ref, vt_ref, hy_ref, *, heads, dk):
    x = x_ref[...]
    ms = jnp.mean(x * x, axis=-1, keepdims=True)
    hn = (x * lax.rsqrt(ms + EPS) * g_ref[...]).astype(BF16)
    hw = heads * 128
    qk = jnp.dot(hn, w_ref[:, :2 * hw], preferred_element_type=F32)
    for h in range(heads):
        q = _rope_block(qk[:, h * 128:(h + 1) * 128], gq_ref[...], tq_ref[0], tq_ref[1], tq_ref[2], dk)
        qt_ref[h * 128:(h + 1) * 128, :] = q.T.astype(qt_ref.dtype)
        k = _rope_block(qk[:, hw + h * 128:hw + (h + 1) * 128], gk_ref[...], tk_ref[0], tk_ref[1], tk_ref[2], dk)
        k_ref[:, h * 128:(h + 1) * 128] = k.astype(k_ref.dtype)
    v = jnp.dot(hn, w_ref[:, 2 * hw:3 * hw], preferred_element_type=F32)
    for h in range(heads):
        vt_ref[h * 128:(h + 1) * 128, :] = v[:, h * 128:(h + 1) * 128].T.astype(vt_ref.dtype)
    for c in range(0, hy_ref.shape[1], hw):
        hy_ref[:, c:c + hw] = jnp.dot(hn, w_ref[:, 3 * hw + c:3 * hw + c + hw],
                                      preferred_element_type=F32).astype(hy_ref.dtype)


def even_in_proj(x2d, b, s, g, w, q_norm, k_norm, q_scale, tm=512):
    t, d = x2d.shape
    heads = ATT_HEADS
    hw = heads * 128
    dk = q_norm.shape[0]
    assert 2 * dk == 128 and dk // 4 == 16, "rope roll shifts assume 64-wide components, 16 rotary dims"
    nj = s // tm
    n_hy = w.shape[1] - 3 * hw
    tab_q = _rope_lane_tables(s, dk, dk // 4, q_scale)
    tab_k = _rope_lane_tables(s, dk, dk // 4, 1.0)
    row = lambda i: (i, 0)
    const = lambda i: (0, 0)
    tab_spec = pl.BlockSpec((3, tm, 128), lambda i: (0, i % nj, 0))
    return pl.pallas_call(
        functools.partial(_even_in_body, heads=heads, dk=dk), grid=(t // tm,),
        in_specs=[pl.BlockSpec((tm, d), row), pl.BlockSpec((1, d), const), pl.BlockSpec(w.shape, const),
                  pl.BlockSpec((1, 128), const), pl.BlockSpec((1, 128), const), tab_spec, tab_spec],
        out_specs=[pl.BlockSpec((None, hw, tm), lambda i: (i // nj, 0, i % nj)),
                   pl.BlockSpec((None, tm, hw), lambda i: (i // nj, i % nj, 0)),
                   pl.BlockSpec((None, hw, tm), lambda i: (i // nj, 0, i % nj)),
                   pl.BlockSpec((tm, n_hy), row)],
        out_shape=[jax.ShapeDtypeStruct((b, hw, s), BF16), jax.ShapeDtypeStruct((b, s, hw), BF16),
                   jax.ShapeDtypeStruct((b, hw, s), BF16), jax.ShapeDtypeStruct((t, n_hy), BF16)],
        compiler_params=_params("parallel"), name="even_in_proj",
    )(x2d, g.reshape(1, d), w, jnp.tile(q_norm, 2).reshape(1, 128).astype(F32),
      jnp.tile(k_norm, 2).reshape(1, 128).astype(F32), tab_q, tab_k)


def _attn_body(lam_ref, qt_ref, k_ref, vt_ref, g_ref, o_ref, *, tq, dk, post_scale):
    lam = lam_ref[0, 0]
    qt = qt_ref[...]
    row = lax.broadcasted_iota(I32, qt.shape, 0)
    zero = jnp.zeros_like(qt)
    qq = jnp.concatenate([jnp.where(row < dk, qt, zero), jnp.where(row < dk, zero, qt)], axis=1)
    st = jnp.dot(k_ref[...], qq, preferred_element_type=F32)
    m = jnp.max(st, axis=0, keepdims=True)
    p = jnp.exp2(st - m)
    r = 1.0 / jnp.sum(p, axis=0, keepdims=True)
    ot = jnp.dot(vt_ref[...], p.astype(BF16), preferred_element_type=F32)
    o = (ot[:, :tq] * r[:, :tq] - ot[:, tq:] * (lam * r[:, tq:])).T
    ms = jnp.mean(o * o, axis=-1, keepdims=True)
    o_ref[...] = (o * lax.rsqrt(ms + EPS) * g_ref[...] * post_scale).astype(o_ref.dtype)


def diff_attention(qt, k, vt, lam, subln, lambda_init, tq=512):
    b, s, _ = k.shape
    h = ATT_HEADS
    return pl.pallas_call(
        functools.partial(_attn_body, tq=tq, dk=64, post_scale=1.0 - lambda_init),
        grid=(b, h, s // tq),
        in_specs=[pl.BlockSpec(memory_space=pltpu.SMEM),
                  pl.BlockSpec((None, 128, tq), lambda bi, hi, i: (bi, hi, i)),
                  pl.BlockSpec((None, s, 128), lambda bi, hi, i: (bi, 0, hi)),
                  pl.BlockSpec((None, 128, s), lambda bi, hi, i: (bi, hi, 0)),
                  pl.BlockSpec((1, 128), lambda bi, hi, i: (0, 0))],
        out_specs=pl.BlockSpec((None, tq, 128), lambda bi, hi, i: (bi, i, hi)),
        out_shape=jax.ShapeDtypeStruct((b, s, h * 128), BF16),
        compiler_params=_params("parallel", "parallel", "parallel"), name="diff_attention",
    )(lam.reshape(1, 1).astype(F32), qt, k, vt, subln.reshape(1, 128).astype(F32))


def _conv3(u_ref, w_ref, b_ref):
    x = u_ref[...].astype(F32)
    s = x.shape[0]
    row = lax.broadcasted_iota(I32, x.shape, 0)
    x_prev = jnp.where(row == 0, 0.0, pltpu.roll(x, 1, 0))
    x_next = jnp.where(row == s - 1, 0.0, pltpu.roll(x, s - 1, 0))
    w = w_ref[...]
    return b_ref[...] + x_prev * w[0:1] + x * w[1:2] + x_next * w[2:3]


FFT_N1 = 64
FFT_UNROLL = 64
FFT_PAD = 8


def _hy_prep_body(x1_ref, x2_ref, v_ref, w1_ref, w2_ref, wv_ref, b1_ref, b2_ref, bv_ref, z_ref, x1c_ref, zp_ref):
    x1c_ref[...] = _conv3(x1_ref, w1_ref, b1_ref).astype(x1c_ref.dtype)
    z = _conv3(v_ref, wv_ref, bv_ref) * _conv3(x2_ref, w2_ref, b2_ref)
    z_ref[...] = z.astype(z_ref.dtype)
    nb = z.shape[0] // FFT_N1
    zp_ref[...] = jnp.zeros_like(zp_ref)
    for n2 in range(nb):
        for ci in range(z.shape[1] // 128):
            zp_ref[ci, pl.ds(n2, FFT_N1, stride=nb + FFT_PAD), :] = (
                z[n2 * FFT_N1:(n2 + 1) * FFT_N1, ci * 128:(ci + 1) * 128])


def hy_prep(u, col0, conv_w, conv_b, tc=256):
    b, s, _ = u.shape
    d_hy = conv_w.shape[1] // 3
    nct = d_hy // tc
    blk0 = col0 // tc
    sp = FFT_N1 * (s // FFT_N1 + FFT_PAD)

    def uspec(part):
        return pl.BlockSpec((None, s, tc), lambda bi, c: (bi, 0, blk0 + part * nct + c))

    def wspec(part, rows):
        return pl.BlockSpec((rows, tc), lambda bi, c: (0, part * nct + c))

    ospec = pl.BlockSpec((None, s, tc), lambda bi, c: (bi, 0, c))
    return pl.pallas_call(
        _hy_prep_body, grid=(b, nct),
        in_specs=[uspec(0), uspec(1), uspec(2), wspec(0, 3), wspec(1, 3), wspec(2, 3),
                  wspec(0, 1), wspec(1, 1), wspec(2, 1)],
        out_specs=[ospec, ospec, pl.BlockSpec((None, tc // 128, sp, 128), lambda bi, c: (bi, c, 0, 0))],
        out_shape=[jax.ShapeDtypeStruct((b, s, d_hy), BF16), jax.ShapeDtypeStruct((b, s, d_hy), BF16),
                   jax.ShapeDtypeStruct((b, d_hy // 128, sp, 128), F32)],
        compiler_params=_params("parallel", "parallel"), name="hy_prep",
    )(u, u, u, conv_w, conv_w, conv_w, conv_b.reshape(1, -1), conv_b.reshape(1, -1), conv_b.reshape(1, -1))


def hyena_filter(length, w1, b1, freq, w2, b2, w3):
    d_hy = w3.shape[1] // 2
    f32 = np.float32
    t = np.linspace(0.0, 1.0, length, dtype=f32)[:, None]
    bands = np.linspace(1e-4, HY_EMB_BANDS - 1, HY_EMB_BANDS, dtype=f32)[None, :]
    ang = (f32(2.0 * math.pi / length) * np.arange(length, dtype=f32)[:, None] * bands).astype(f32)
    z = jnp.asarray(np.concatenate([t, np.cos(ang), -np.sin(ang)], axis=-1).astype(f32))
    deltas = np.abs(np.linspace(HY_MIN_DECAY, HY_MAX_DECAY, d_hy, dtype=f32))
    decay = jnp.asarray(np.exp(-t * deltas[None, :]).astype(f32))
    hp = lax.Precision.HIGHEST
    hdn = jnp.sin(freq * (jnp.dot(z, w1, precision=hp) + b1))
    hdn = jnp.sin(freq * (jnp.dot(hdn, w2, precision=hp) + b2))
    filt = jnp.dot(hdn, w3, precision=hp)
    h_fwd = filt[:, :d_hy] * decay
    h_bwd = filt[:, d_hy:] * decay
    h_fwd = h_fwd.at[0].add(h_bwd[0])
    h_bwd = h_bwd.at[0].set(0.0)
    norm = jnp.sum(jnp.abs(h_fwd), axis=0, keepdims=True) + jnp.sum(jnp.abs(h_bwd), axis=0, keepdims=True) + EPS
    return jnp.concatenate([h_fwd / norm, h_bwd / norm], axis=1)


def fft_tables(length):
    n = 2 * length
    n1c, n2c, nb = FFT_N1, n // FFT_N1, length // FFT_N1
    unit = 2.0 * math.pi / n
    i1 = np.arange(n1c, dtype=np.int64)
    i2 = np.arange(n2c, dtype=np.int64)
    ib = np.arange(nb, dtype=np.int64)
    samp = i1[:, None, None] + n1c * ib[None, None, :]
    ang = ((i2[None, :, None] * samp) % n) * unit
    m1 = np.concatenate([np.cos(ang), -np.sin(ang)], axis=1)
    ang = ((i1[:, None] * i1[None, :]) % n1c) * (2.0 * math.pi / n1c)
    c, s = np.cos(ang), np.sin(ang)
    f1 = np.concatenate([np.concatenate([c, s], axis=1), np.concatenate([-s, c], axis=1)], axis=0)
    freq = n2c * i1[None, None, :] + i2[:, None, None]
    ang = ((i1[None, :, None] * freq) % n) * unit
    c, s = np.cos(ang), np.sin(ang)
    g1 = np.concatenate([np.concatenate([c, -s], axis=2), np.concatenate([s, c], axis=2)], axis=1)
    ang = ((ib[:, None] * i2[None, :]) % n2c) * (2.0 * math.pi / n2c)
    g2 = np.concatenate([np.cos(ang), -np.sin(ang)], axis=1) * (1.0 / n)
    return tuple(jnp.asarray(t.astype(np.float32).astype(BF16)) for t in (m1, f1, g1, g2))


def _fft_stage1(xp_ref, m1_ref, p_ref):
    nb = m1_ref.shape[2]
    n2c = m1_ref.shape[1] // 2

    def body(n1, c):
        x = xp_ref[pl.ds(pl.multiple_of(n1 * (nb + FFT_PAD), 8), nb), :].astype(BF16)
        a = jnp.dot(m1_ref[n1], x, preferred_element_type=F32)
        p_ref[0, pl.ds(n1, n2c, stride=FFT_N1 + FFT_PAD), :] = a[:n2c]
        p_ref[1, pl.ds(n1, n2c, stride=FFT_N1 + FFT_PAD), :] = a[n2c:]
        return c
    lax.fori_loop(0, FFT_N1, body, 0, unroll=FFT_UNROLL)


def _fft_stage2(p_ref, f1_ref, k2):
    r0 = pl.multiple_of(k2 * (FFT_N1 + FFT_PAD), 8)
    slab = jnp.concatenate([p_ref[0, pl.ds(r0, FFT_N1), :], p_ref[1, pl.ds(r0, FFT_N1), :]], axis=0)
    return jnp.dot(f1_ref[...], slab.astype(BF16), preferred_element_type=F32)


def _spectrum_body(xp_ref, m1_ref, f1_ref, o_ref, p_ref):
    _fft_stage1(xp_ref, m1_ref, p_ref)

    def body(k2, c):
        o_ref[k2] = _fft_stage2(p_ref, f1_ref, k2)
        return c
    lax.fori_loop(0, o_ref.shape[0], body, 0, unroll=FFT_UNROLL)


def filter_spectrum(ab, m1, f1):
    length, c2 = ab.shape
    nb = length // FFT_N1
    n2c = m1.shape[1] // 2
    nch = c2 // 128
    abp = jnp.pad(ab.reshape(nb, FFT_N1, nch, 128).transpose(2, 1, 0, 3), ((0, 0), (0, 0), (0, FFT_PAD), (0, 0)))
    abp = abp.reshape(nch, FFT_N1 * (nb + FFT_PAD), 128)
    spec = pl.pallas_call(
        _spectrum_body, grid=(nch,),
        in_specs=[pl.BlockSpec((None,) + abp.shape[1:], lambda c: (c, 0, 0)),
                  pl.BlockSpec(m1.shape, lambda c: (0, 0, 0)), pl.BlockSpec(f1.shape, lambda c: (0, 0))],
        out_specs=pl.BlockSpec((None, n2c, 2 * FFT_N1, 128), lambda c: (c, 0, 0, 0)),
        out_shape=jax.ShapeDtypeStruct((nch, n2c, 2 * FFT_N1, 128), F32),
        scratch_shapes=[pltpu.VMEM((2, n2c * (FFT_N1 + FFT_PAD), 128), F32)],
        compiler_params=_params("parallel"), name="filter_spectrum",
    )(abp, m1, f1)
    fa, fb = spec[:nch // 2], spec[nch // 2:]
    h = FFT_N1
    return jnp.concatenate([fa[:, :, :h] + fb[:, :, :h], fa[:, :, h:] - fb[:, :, h:]], axis=2).astype(BF16)


def _hy_fft_body(zp_ref, z_ref, x1c_ref, kf_ref, m1_ref, f1_ref, g1_ref, g2_ref, skip_ref, o_ref,
                 p_ref, q_ref, y_ref):
    h = FFT_N1
    n2c = g1_ref.shape[0]
    nb = g2_ref.shape[0]
    _fft_stage1(zp_ref, m1_ref, p_ref)

    def mid(k2, c):
        xf = _fft_stage2(p_ref, f1_ref, k2)
        kf = kf_ref[k2].astype(F32)
        xr, xi, kr, ki = xf[:h], xf[h:], kf[:h], kf[h:]
        y = jnp.concatenate([xr * kr - xi * ki, xr * ki + xi * kr], axis=0).astype(BF16)
        d = jnp.dot(g1_ref[k2], y, preferred_element_type=F32)
        q_ref[0, pl.ds(k2, h, stride=n2c + FFT_PAD), :] = d[:h]
        q_ref[1, pl.ds(k2, h, stride=n2c + FFT_PAD), :] = d[h:]
        return c
    lax.fori_loop(0, n2c, mid, 0, unroll=FFT_UNROLL)

    def last(t1, c):
        r0 = pl.multiple_of(t1 * (n2c + FFT_PAD), 8)
        slab = jnp.concatenate([q_ref[0, pl.ds(r0, n2c), :], q_ref[1, pl.ds(r0, n2c), :]], axis=0)
        y_ref[pl.ds(t1, nb, stride=h), :] = jnp.dot(g2_ref[...], slab.astype(BF16), preferred_element_type=F32)
        return c
    lax.fori_loop(0, h, last, 0, unroll=FFT_UNROLL)

    z = z_ref[...].astype(F32)
    o_ref[...] = ((y_ref[...] + z * skip_ref[...]) * x1c_ref[...].astype(F32)).astype(o_ref.dtype)


def hy_fft_conv(zp, z, x1c, kf, tables, skip):
    m1, f1, g1, g2 = tables
    b, s, c = z.shape
    nch = c // 128

    def const(shape):
        return pl.BlockSpec(shape, lambda ci, bi: (0,) * len(shape))

    nat = pl.BlockSpec((None, s, 128), lambda ci, bi: (bi, 0, ci))
    return pl.pallas_call(
        _hy_fft_body, grid=(nch, b),
        in_specs=[pl.BlockSpec((None, None) + zp.shape[2:], lambda ci, bi: (bi, ci, 0, 0)), nat, nat,
                  pl.BlockSpec((None,) + kf.shape[1:], lambda ci, bi: (ci, 0, 0, 0)),
                  const(m1.shape), const(f1.shape), const(g1.shape), const(g2.shape),
                  pl.BlockSpec((1, 128), lambda ci, bi: (0, ci))],
        out_specs=nat,
        out_shape=jax.ShapeDtypeStruct((b, s, c), BF16),
        scratch_shapes=[pltpu.VMEM((2, g1.shape[0] * (FFT_N1 + FFT_PAD), 128), F32),
                        pltpu.VMEM((2, FFT_N1 * (g1.shape[0] + FFT_PAD), 128), F32),
                        pltpu.VMEM((s, 128), F32)],
        compiler_params=_params("parallel", "parallel"), name="hy_fft_conv",
    )(zp, z, x1c, kf, m1, f1, g1, g2, skip.reshape(1, c).astype(F32))


def _ml_prep_body(u_ref, w_ref, b_ref, o_ref, *, scale, transpose):
    y = _conv3(u_ref, w_ref, b_ref)
    y = y * _sigmoid(y) * scale
    o_ref[...] = (y.T if transpose else y).astype(o_ref.dtype)


def ml_prep(u, conv_w, conv_b, col0, ncols, scale, transpose, tc=256):
    b, s, _ = u.shape
    w = conv_w.shape[1]
    c0 = col0 // tc
    if transpose:
        out_spec = pl.BlockSpec((None, tc, s), lambda bi, c: (bi, c, 0))
        out_shape = jax.ShapeDtypeStruct((b, ncols, s), BF16)
    else:
        out_spec = pl.BlockSpec((None, s, tc), lambda bi, c: (bi, 0, c))
        out_shape = jax.ShapeDtypeStruct((b, s, ncols), BF16)
    return pl.pallas_call(
        functools.partial(_ml_prep_body, scale=scale, transpose=transpose), grid=(b, ncols // tc),
        in_specs=[pl.BlockSpec((None, s, tc), lambda bi, c: (bi, 0, c0 + c)),
                  pl.BlockSpec((3, tc), lambda bi, c: (0, c0 + c)),
                  pl.BlockSpec((1, tc), lambda bi, c: (0, c0 + c))],
        out_specs=out_spec, out_shape=out_shape,
        compiler_params=_params("parallel", "parallel"), name="ml_prep",
    )(u, conv_w, conv_b.reshape(1, w))


def _log_sigmoid(x):
    return jnp.minimum(x, 0.0) - jnp.log(1.0 + jnp.exp(-jnp.abs(x)))


def _dot_split(a, b, a_is_f32):
    x = a if a_is_f32 else b
    hi = x.astype(BF16)
    lo = (x - hi.astype(F32)).astype(BF16)
    if a_is_f32:
        return (jnp.dot(hi, b, preferred_element_type=F32) + jnp.dot(lo, b, preferred_element_type=F32))
    return (jnp.dot(a, hi, preferred_element_type=F32) + jnp.dot(a, lo, preferred_element_type=F32))


def _mlstm_chain(q, kt, v, bc, br, li_r, total, mask, c_ref, m_ref, idx):
    dv = v.shape[1] - 128
    c_st = c_ref[idx]
    m_st = m_ref[idx:idx + 1, 0:1]
    key_row = li_r - br
    inter = bc + m_st
    m_t = jnp.maximum(inter, bc + jnp.max(jnp.where(mask, key_row, NEG_BIG), axis=-1, keepdims=True))
    w_intra = jnp.exp(jnp.where(mask, (bc - m_t) + key_row, NEG_BIG))
    w_inter = jnp.exp(inter - m_t)
    sc = jnp.dot(q, kt, preferred_element_type=F32) * w_intra
    q_inter = (q.astype(F32) * w_inter).astype(BF16)
    both = jnp.dot(jnp.concatenate([q_inter, sc.astype(BF16)], axis=1),
                   jnp.concatenate([c_st.astype(BF16), v], axis=0),
                   preferred_element_type=F32)
    den = both[:, dv:dv + 1]
    h = both[:, :dv] / jnp.maximum(jnp.abs(den), jnp.exp(-m_t))
    g_row = total + key_row
    m_next = jnp.maximum(total + m_st, jnp.max(g_row, axis=-1, keepdims=True))
    a_prev = jnp.exp(total + m_st - m_next)
    kwt = (kt.astype(F32) * jnp.exp(g_row - m_next)).astype(BF16)
    c_ref[idx] = a_prev * c_st + jnp.dot(kwt, v, preferred_element_type=F32)
    m_ref[idx:idx + 1, :] = jnp.broadcast_to(m_next, (1, m_ref.shape[1]))
    return h


def _mlstm_body(qf_ref, ktf_ref, vf_ref, gcf_ref, grf_ref, qb_ref, ktb_ref, vb_ref, gcb_ref, grb_ref,
                hf_ref, hb_ref, c_ref, m_ref, *, heads, dk, dv):
    @pl.when(pl.program_id(1) == 0)
    def _():
        c_ref[...] = jnp.zeros_like(c_ref)
        m_ref[...] = jnp.zeros_like(m_ref)

    lc = qf_ref.shape[1]
    ones_blk = jnp.where(lax.broadcasted_iota(I32, (lc, 128), 1) == 0, 1.0, 0.0).astype(BF16)
    t_i = lax.broadcasted_iota(I32, (lc, lc), 0)
    s_i = lax.broadcasted_iota(I32, (lc, lc), 1)
    lower = s_i <= t_i
    upper = s_i >= t_i
    ltri = jnp.where(lower, 1.0, 0.0).astype(BF16)
    utri = jnp.where(upper, 1.0, 0.0).astype(BF16)

    for bb in range(qf_ref.shape[0]):
        for direction, (q_ref, kt_ref, v_ref, gc_ref, gr_ref, h_ref) in enumerate(
                ((qf_ref, ktf_ref, vf_ref, gcf_ref, grf_ref, hf_ref),
                 (qb_ref, ktb_ref, vb_ref, gcb_ref, grb_ref, hb_ref))):
            fwd = direction == 0
            gc = gc_ref[bb]
            gr = gr_ref[bb]
            lfc, lfr = _log_sigmoid(gc), _log_sigmoid(gr)
            cum_c = _dot_split(ltri if fwd else utri, lfc, a_is_f32=False)
            cum_r = _dot_split(lfr, utri if fwd else ltri, a_is_f32=True)
            for hd in range(heads):
                gi = (0 if fwd else 2) * heads + hd
                gf = (1 if fwd else 3) * heads + hd
                bc, br = cum_c[:, gf:gf + 1], cum_r[gf:gf + 1, :]
                total = br[:, lc - 1:lc] if fwd else br[:, 0:1]
                q = q_ref[bb, :, hd * dk:(hd + 1) * dk]
                kt = kt_ref[bb, hd * dk:(hd + 1) * dk, :]
                v = jnp.concatenate([v_ref[bb, :, hd * dv:(hd + 1) * dv], ones_blk], axis=1)
                h = _mlstm_chain(q, kt, v, bc, br, gr[gi:gi + 1, :], total,
                                 lower if fwd else upper, c_ref, m_ref, (bb * 2 + direction) * heads + hd)
                h_ref[bb, :, hd * dv:(hd + 1) * dv] = h.astype(h_ref.dtype)


def mlstm_scan(q, kt, u, v_blk, gcol, grow):
    b, s, w = q.shape
    heads = ML_HEADS
    dk = w // heads
    dv = 2 * dk
    lc = ML_CHUNK
    nc = s // lc
    ng = gcol.shape[-1]

    def fw(bi, j):
        return j

    def bw(bi, j):
        return nc - 1 - j

    nbs = ML_BATCH_PER_STEP if b % ML_BATCH_PER_STEP == 0 else 1

    def specs(pos):
        return [pl.BlockSpec((nbs, lc, w), lambda bi, j: (bi, pos(bi, j), 0)),
                pl.BlockSpec((nbs, w, lc), lambda bi, j: (bi, 0, pos(bi, j))),
                pl.BlockSpec((nbs, lc, heads * dv), lambda bi, j: (bi, pos(bi, j), v_blk)),
                pl.BlockSpec((nbs, lc, ng), lambda bi, j: (bi, pos(bi, j), 0)),
                pl.BlockSpec((nbs, ng, lc), lambda bi, j: (bi, 0, pos(bi, j)))]

    hshape = jax.ShapeDtypeStruct((b, s, heads * dv), BF16)
    return pl.pallas_call(
        functools.partial(_mlstm_body, heads=heads, dk=dk, dv=dv), grid=(b // nbs, nc),
        in_specs=specs(fw) + specs(bw),
        out_specs=[pl.BlockSpec((nbs, lc, heads * dv), lambda bi, j: (bi, j, 0)),
                   pl.BlockSpec((nbs, lc, heads * dv), lambda bi, j: (bi, nc - 1 - j, 0))],
        out_shape=[hshape, hshape],
        scratch_shapes=[pltpu.VMEM((nbs * 2 * heads, dk, dv + 128), F32),
                        pltpu.VMEM((nbs * 2 * heads, 128), F32)],
        compiler_params=_params("parallel", "arbitrary"), name="mlstm_scan",
    )(q, kt, u, gcol, grow, q, kt, u, gcol, grow)


def _mlstm_out_body(res_ref, hf_ref, hb_ref, o_ref, g_ref, w_ref, gf_ref, wt_ref, rb_ref,
                    out_ref, route_ref, cnt_ref, run_ref, *, heads):
    hs = hf_ref[...].astype(F32) + hb_ref[...].astype(F32)
    dv = hs.shape[1] // heads
    g = g_ref[...]
    parts = []
    for hd in range(heads):
        seg = hs[:, hd * dv:(hd + 1) * dv]
        ms = jnp.mean(seg * seg, axis=-1, keepdims=True)
        parts.append(seg * lax.rsqrt(ms + EPS) * g[:, hd * dv:(hd + 1) * dv])
    a = jnp.concatenate(parts, axis=-1) * _sigmoid(o_ref[...].astype(F32))
    x_new = res_ref[...] + jnp.dot(a.astype(BF16), w_ref[...], preferred_element_type=F32)
    out_ref[...] = x_new
    _route_epilogue(x_new, gf_ref, wt_ref, rb_ref, route_ref, cnt_ref, run_ref)


def mlstm_out(res, hf, hb, u2d, o_blk, gain, w_out, route_params, tm=1024):
    t, d = res.shape
    row = lambda i: (i, 0)
    r_in, r_args, r_out, r_shapes, r_scratch = _route_plumbing(t, d, tm, *route_params)
    return pl.pallas_call(
        functools.partial(_mlstm_out_body, heads=ML_HEADS), grid=(t // tm,),
        in_specs=[pl.BlockSpec((tm, d), row), pl.BlockSpec((tm, d), row), pl.BlockSpec((tm, d), row),
                  pl.BlockSpec((tm, d), lambda i: (i, o_blk)), pl.BlockSpec((1, d), lambda i: (0, 0)),
                  pl.BlockSpec((d, d), lambda i: (0, 0))] + r_in,
        out_specs=[pl.BlockSpec((tm, d), row)] + r_out,
        out_shape=[jax.ShapeDtypeStruct((t, d), F32)] + r_shapes, scratch_shapes=r_scratch,
        compiler_params=_params("arbitrary"), name="mlstm_out",
    )(res, hf, hb, u2d, gain.reshape(1, d), w_out, *r_args)


ROUTE_TILE = 512
ROUTE_ROWS = 128
EXPERT_ROW0 = 8


def _route_tile(x, g, wt, bias, run_ref):
    ms = jnp.mean(x * x, axis=-1, keepdims=True)
    xn = x * lax.rsqrt(ms + EPS) * g
    w_hi, x_hi = wt.astype(BF16), xn.astype(BF16)
    w_lo, x_lo = (wt - w_hi.astype(F32)).astype(BF16), (xn - x_hi.astype(F32)).astype(BF16)
    nt = (((1,), (1,)), ((), ()))
    logit = (lax.dot_general(w_hi, x_hi, nt, preferred_element_type=F32)
             + lax.dot_general(w_lo, x_hi, nt, preferred_element_type=F32)
             + lax.dot_general(w_hi, x_lo, nt, preferred_element_type=F32)) + bias
    rows = [logit[r:r + 1, :] for r in range(EXPERT_ROW0 + N_EXPERTS)]
    g_best, g_idx = rows[0], jnp.zeros_like(rows[0])
    for gi in range(1, N_GROUPS):
        better = rows[gi] > g_best
        g_best = jnp.where(better, rows[gi], g_best)
        g_idx = jnp.where(better, float(gi), g_idx)
    g_den = sum(jnp.exp(rows[gi] - g_best) for gi in range(N_GROUPS))
    g_w = 1.0 / g_den
    sel = []
    for e in range(EXPERTS_PER_GROUP):
        v = rows[EXPERT_ROW0 + e]
        for gi in range(1, N_GROUPS):
            v = jnp.where(g_idx == float(gi), rows[EXPERT_ROW0 + gi * EXPERTS_PER_GROUP + e], v)
        sel.append(v)
    v1, i1 = sel[0], jnp.zeros_like(sel[0])
    for e in range(1, EXPERTS_PER_GROUP):
        better = sel[e] > v1
        v1 = jnp.where(better, sel[e], v1)
        i1 = jnp.where(better, float(e), i1)
    v2, i2 = jnp.full_like(v1, -jnp.inf), jnp.zeros_like(v1)
    for e in range(EXPERTS_PER_GROUP):
        better = (sel[e] > v2) & (i1 != float(e))
        v2 = jnp.where(better, sel[e], v2)
        i2 = jnp.where(better, float(e), i2)
    e21 = jnp.exp(v2 - v1)
    gate1 = g_w / (1.0 + e21)
    gate2 = gate1 * e21
    base = g_idx * float(EXPERTS_PER_GROUP)
    e1, e2 = base + i1, base + i2
    tm = e1.shape[1]
    erow = lax.broadcasted_iota(I32, (N_EXPERTS, tm), 0).astype(F32)
    oh1 = jnp.where(erow == e1, 1.0, 0.0)
    oh2 = jnp.where(erow == e2, 1.0, 0.0)
    cnt = oh1 + oh2
    earlier = jnp.where(lax.broadcasted_iota(I32, (tm, tm), 0) < lax.broadcasted_iota(I32, (tm, tm), 1),
                        1.0, 0.0).astype(BF16)
    run = run_ref[...]
    pos = run[:, 0:1] + jnp.dot(cnt.astype(BF16), earlier, preferred_element_type=F32)
    rank1 = jnp.sum(oh1 * pos, axis=0, keepdims=True)
    rank2 = jnp.sum(oh2 * pos, axis=0, keepdims=True)
    run = run + jnp.sum(cnt, axis=1, keepdims=True)
    run_ref[...] = run
    zero = jnp.zeros_like(v1)
    return jnp.concatenate([e1, e2, gate1, gate2, rank1, rank2, zero, zero], axis=0)


def _route_epilogue(x_new, g_ref, wt_ref, b_ref, route_ref, cnt_ref, run_ref):
    @pl.when(pl.program_id(0) == 0)
    def _():
        run_ref[...] = jnp.zeros_like(run_ref)

    for r0 in range(0, x_new.shape[0], ROUTE_TILE):
        route_ref[:, r0:r0 + ROUTE_TILE] = _route_tile(x_new[r0:r0 + ROUTE_TILE], g_ref[...], wt_ref[...],
                                                       b_ref[...], run_ref)
    cnt_ref[...] = run_ref[...]


def _route_plumbing(t, d, tm, g, wg, bg, we, be):
    wt = jnp.zeros((ROUTE_ROWS, d), F32).at[:N_GROUPS].set(wg.T).at[EXPERT_ROW0:EXPERT_ROW0 + N_EXPERTS].set(we.T)
    bias = jnp.zeros((ROUTE_ROWS, 1), F32).at[:N_GROUPS, 0].set(bg).at[EXPERT_ROW0:EXPERT_ROW0 + N_EXPERTS, 0].set(be)
    in_specs = [pl.BlockSpec((1, d), lambda i: (0, 0)), pl.BlockSpec((ROUTE_ROWS, d), lambda i: (0, 0)),
                pl.BlockSpec((ROUTE_ROWS, 1), lambda i: (0, 0))]
    out_specs = [pl.BlockSpec((8, tm), lambda i: (0, i)), pl.BlockSpec((N_EXPERTS, 128), lambda i: (0, 0))]
    out_shapes = [jax.ShapeDtypeStruct((8, t), F32), jax.ShapeDtypeStruct((N_EXPERTS, 128), F32)]
    return in_specs, [g.reshape(1, d), wt, bias], out_specs, out_shapes, [pltpu.VMEM((N_EXPERTS, 128), F32)]


def _dispatch_body(dest_ref, zblk_ref, x_ref, g_ref, xs_ref, buf_ref, sem_ref, *, td):
    i = pl.program_id(0)
    n = pl.num_programs(0)
    slot = i % 2

    @pl.when(i == 0)
    def _():
        buf_ref[1] = jnp.zeros(buf_ref.shape[1:], buf_ref.dtype)

        def zero_copy(j):
            return pltpu.make_async_copy(buf_ref.at[1], xs_ref.at[pl.ds(zblk_ref[j] * td, td), :], sem_ref.at[1])

        def start(j, c):
            @pl.when(zblk_ref[j] >= 0)
            def _():
                zero_copy(j).start()
            return c

        def wait(j, c):
            @pl.when(zblk_ref[j] >= 0)
            def _():
                zero_copy(j).wait()
            return c
        lax.fori_loop(0, zblk_ref.shape[0], start, 0)
        lax.fori_loop(0, zblk_ref.shape[0], wait, 0)

    def row_copy(r, kk):
        return pltpu.make_async_copy(buf_ref.at[slot, pl.ds(r, 1), :],
                                     xs_ref.at[pl.ds(dest_ref[(i * td + r) * 2 + kk], 1), :],
                                     sem_ref.at[slot])

    def wait_buffer(sl):
        for _ in range(2):
            pltpu.make_async_copy(buf_ref.at[sl], xs_ref.at[pl.ds(0, td), :], sem_ref.at[sl]).wait()

    @pl.when(i >= 2)
    def _():
        wait_buffer(slot)

    x = x_ref[...]
    ms = jnp.mean(x * x, axis=-1, keepdims=True)
    buf_ref[slot] = x * lax.rsqrt(ms + EPS) * g_ref[...]

    for r in range(td):
        row_copy(r, 0).start(priority=0)
        row_copy(r, 1).start(priority=1)

    @pl.when(i == n - 1)
    def _():
        @pl.when(n >= 2)
        def _():
            wait_buffer(1 - slot)
        wait_buffer(slot)


def moe_dispatch(x2d, g, dest, zero_blk, cap, td):
    t, d = x2d.shape
    grid_spec = pltpu.PrefetchScalarGridSpec(
        num_scalar_prefetch=2, grid=(t // td,),
        in_specs=[pl.BlockSpec((td, d), lambda i, dest, zb: (i, 0)),
                  pl.BlockSpec((1, d), lambda i, dest, zb: (0, 0))],
        out_specs=pl.BlockSpec(memory_space=pl.ANY),
        scratch_shapes=[pltpu.VMEM((2, td, d), F32), pltpu.SemaphoreType.DMA((2,))])
    return pl.pallas_call(
        functools.partial(_dispatch_body, td=td), grid_spec=grid_spec,
        out_shape=jax.ShapeDtypeStruct((cap, d), F32),
        compiler_params=_params("arbitrary", disable_bounds_checks=True),
        name="moe_dispatch",
    )(dest, zero_blk, x2d, g.reshape(1, d))


def _expert_body(blk_e_ref, nused_ref, xs_ref, w1_ref, w3_ref, w2_ref, ys_ref, w1b_ref, w3b_ref, w2b_ref):
    i = pl.program_id(0)
    nused = nused_ref[0]
    last = nused - 1
    cur = blk_e_ref[jnp.minimum(i, last)]
    prev = blk_e_ref[jnp.minimum(jnp.maximum(i - 1, 0), last)]

    @pl.when((i == 0) | (cur != prev))
    def _():
        w1b_ref[...] = w1_ref[...].astype(BF16)
        w3b_ref[...] = w3_ref[...].astype(BF16)
        w2b_ref[...] = w2_ref[...].astype(BF16)

    @pl.when(i < nused)
    def _():
        x = xs_ref[...].astype(BF16)
        h1 = jnp.dot(x, w1b_ref[...], preferred_element_type=F32)
        h3 = jnp.dot(x, w3b_ref[...], preferred_element_type=F32)
        hid = (h1 * _sigmoid(h1) * h3).astype(BF16)
        ys_ref[...] = jnp.dot(hid, w2b_ref[...], preferred_element_type=F32)

    @pl.when(i >= nused)
    def _():
        ys_ref[...] = jnp.zeros_like(ys_ref)


def moe_experts(xs, blk_e, nused, w1, w3, w2, layer, tm):
    cap, d = xs.shape
    de = w1.shape[-1]

    def blk(i, be, nu):
        return jnp.minimum(i, nu[0] - 1)

    grid_spec = pltpu.PrefetchScalarGridSpec(
        num_scalar_prefetch=2, grid=(cap // tm,),
        in_specs=[pl.BlockSpec((tm, d), lambda i, be, nu: (blk(i, be, nu), 0)),
                  pl.BlockSpec((None, None, d, de), lambda i, be, nu: (layer, be[blk(i, be, nu)], 0, 0)),
                  pl.BlockSpec((None, None, d, de), lambda i, be, nu: (layer, be[blk(i, be, nu)], 0, 0)),
                  pl.BlockSpec((None, None, de, d), lambda i, be, nu: (layer, be[blk(i, be, nu)], 0, 0))],
        out_specs=pl.BlockSpec((tm, d), lambda i, be, nu: (i, 0)),
        scratch_shapes=[pltpu.VMEM((d, de), BF16), pltpu.VMEM((d, de), BF16), pltpu.VMEM((de, d), BF16)])
    return pl.pallas_call(
        _expert_body, grid_spec=grid_spec, out_shape=jax.ShapeDtypeStruct((cap, d), F32),
        compiler_params=_params("arbitrary"), name="moe_experts",
    )(blk_e, nused, xs, w1, w3, w2)


def _combine_body(dest_ref, x_ref, gate_ref, ys_ref, o_ref, buf_ref, sem_ref, *, tc):
    i = pl.program_id(0)
    n = pl.num_programs(0)
    slot = i % 2

    def row_copy(step, sl, r, kk):
        return pltpu.make_async_copy(ys_ref.at[pl.ds(dest_ref[(step * tc + r) * 2 + kk], 1), :],
                                     buf_ref.at[sl, kk, pl.ds(r, 1), :], sem_ref.at[sl])

    def issue_step(step, sl):
        for r in range(tc):
            row_copy(step, sl, r, 0).start(priority=0)
            row_copy(step, sl, r, 1).start(priority=1)

    @pl.when(i == 0)
    def _():
        issue_step(0, 0)

    @pl.when(i + 1 < n)
    def _():
        issue_step(i + 1, 1 - slot)

    for kk in range(2):
        pltpu.make_async_copy(ys_ref.at[pl.ds(0, tc), :], buf_ref.at[slot, kk], sem_ref.at[slot]).wait()

    gate = gate_ref[...]
    o_ref[...] = x_ref[...] + gate[:, 0:1] * buf_ref[slot, 0] + gate[:, 1:2] * buf_ref[slot, 1]


def moe_combine(x2d, gates, ys, dest, tc=256):
    t, d = x2d.shape
    grid_spec = pltpu.PrefetchScalarGridSpec(
        num_scalar_prefetch=1, grid=(t // tc,),
        in_specs=[pl.BlockSpec((tc, d), lambda i, dest: (i, 0)), pl.BlockSpec((tc, 2), lambda i, dest: (i, 0)),
                  pl.BlockSpec(memory_space=pl.ANY)],
        out_specs=pl.BlockSpec((tc, d), lambda i, dest: (i, 0)),
        scratch_shapes=[pltpu.VMEM((2, 2, tc, d), F32), pltpu.SemaphoreType.DMA((2,))])
    return pl.pallas_call(
        functools.partial(_combine_body, tc=tc), grid_spec=grid_spec,
        out_shape=jax.ShapeDtypeStruct((t, d), F32),
        compiler_params=_params("arbitrary", disable_bounds_checks=True), name="moe_combine",
    )(dest, x2d, gates, ys)


def hier_moe_residual(x2d, route, counts_b, g, w1, w3, w2, layer, tm=512):
    t, d = x2d.shape
    eid = route[0:2].T.astype(I32).reshape(-1)
    rank = route[4:6].T.astype(I32).reshape(-1)
    gates = route[2:4].T
    counts = counts_b[:, 0].astype(I32)
    padded = (counts + tm - 1) // tm * tm
    pad_end = jnp.cumsum(padded)
    pad_start = pad_end - padded
    experts = jnp.arange(N_EXPERTS, dtype=I32)
    dest = rank + jnp.sum(jnp.where(eid[:, None] == experts[None, :], pad_start[None, :], 0), axis=1)
    cap = t * 2 + N_EXPERTS * tm
    nblk = cap // tm
    blk_row0 = jnp.arange(nblk, dtype=I32) * tm
    blk_e = jnp.minimum(jnp.sum((pad_end[None, :] <= blk_row0[:, None]).astype(I32), axis=1), N_EXPERTS - 1)
    nused = (pad_end[-1:] // tm).astype(I32)
    last_blk = jnp.where(padded > 0, pad_end // tm - 1, -1)
    tail_blk = jnp.where(nused[0] + experts < nblk, nused[0] + experts, -1)
    zero_blk = jnp.concatenate([last_blk, tail_blk]).astype(I32)
    xs = moe_dispatch(x2d, g, dest.astype(I32), zero_blk, cap, tm)
    ys = moe_experts(xs, blk_e, nused, w1, w3, w2, layer, tm)
    return moe_combine(x2d, gates, ys, dest.astype(I32))


def _even_layer(x2d, b, s, layer, mix_g, w_in, q_norm, k_norm, lam_q1, lam_k1, lam_q2, lam_k2, subln,
                hy_conv_w, hy_conv_b, f_w1, f_b1, f_freq, f_w2, f_b2, f_w3, hy_skip, w_out, route_params):
    d = x2d.shape[1]
    d_att = d // 2
    lambda_init = 0.8 - 0.6 * math.exp(-0.3 * layer)
    lam = jnp.exp(jnp.sum(lam_q1 * lam_k1)) - jnp.exp(jnp.sum(lam_q2 * lam_k2)) + lambda_init
    dk = q_norm.shape[0]
    qt, kp, vt, u_hy = even_in_proj(x2d, b, s, mix_g, w_in.astype(BF16), q_norm, k_norm,
                                    dk ** -0.5 * math.log2(math.e))
    y_att = diff_attention(qt, kp, vt, lam, subln, lambda_init)
    z, x1c, zp = hy_prep(u_hy.reshape(b, s, -1), 0, hy_conv_w, hy_conv_b)
    tables = fft_tables(s)
    kf = filter_spectrum(hyena_filter(s, f_w1, f_b1, f_freq, f_w2, f_b2, f_w3), tables[0], tables[1])
    y_hy = hy_fft_conv(zp, z, x1c, kf, tables, hy_skip)
    w_out = w_out.astype(BF16)
    return matmul_residual(x2d, [y_att.reshape(b * s, -1), y_hy.reshape(b * s, -1)],
                           [w_out[:d_att], w_out[d_att:]], route_params)


def _odd_layer(x2d, b, s, mix_g, w_in, conv_w, conv_b, gate_b, out_norm, w_out, route_params):
    d = x2d.shape[1]
    qk_w = conv_w.shape[1]
    main_w = qk_w + 2 * d
    ng = 4 * ML_HEADS
    w_gate = jnp.zeros((d, 128), F32).at[:, :ng].set(w_in[:, main_w:]).astype(BF16)
    u2d, ug = norm_matmul(x2d, mix_g, [w_in[:, :main_w].astype(BF16), w_gate], [BF16, F32])
    u = u2d.reshape(b, s, main_w)
    gcol = (ug[:, :ng] + gate_b).reshape(b, s, ng)
    grow = jnp.swapaxes(gcol, 1, 2)
    dk = qk_w // (2 * ML_HEADS)
    q = ml_prep(u, conv_w, conv_b, 0, qk_w // 2, dk ** -0.5, transpose=False)
    kt = ml_prep(u, conv_w, conv_b, qk_w // 2, qk_w // 2, 1.0, transpose=True)
    hf, hb = mlstm_scan(q, kt, u, qk_w // d, gcol, grow)
    return mlstm_out(x2d, hf.reshape(b * s, d), hb.reshape(b * s, d), u2d, (qk_w + d) // d, out_norm,
                     w_out.astype(BF16), route_params)


def kernel(x, mix_norm, ffn_norm, ev_w_in, ev_q_norm, ev_k_norm, ev_lam_q1, ev_lam_k1, ev_lam_q2, ev_lam_k2, ev_subln, ev_hy_conv_w, ev_hy_conv_b, ev_hy_f_w1, ev_hy_f_b1, ev_hy_f_freq, ev_hy_f_w2, ev_hy_f_b2, ev_hy_f_w3, ev_hy_skip, ev_w_out, od_w_in, od_conv_w, od_conv_b, od_gate_b, od_out_norm, od_w_out, moe_wg, moe_bg, moe_we, moe_be, moe_w1, moe_w3, moe_w2):
    b, s, d = x.shape
    depth = mix_norm.shape[0]
    x2d = x.reshape(b * s, d)
    for layer in range(depth):
        j = layer // 2
        route_params = (ffn_norm[layer], moe_wg[layer], moe_bg[layer], moe_we[layer], moe_be[layer])
        if layer % 2 == 0:
            x2d, route, counts = _even_layer(
                x2d, b, s, layer, mix_norm[layer], ev_w_in[j], ev_q_norm[j], ev_k_norm[j],
                ev_lam_q1[j], ev_lam_k1[j], ev_lam_q2[j], ev_lam_k2[j], ev_subln[j],
                ev_hy_conv_w[j], ev_hy_conv_b[j], ev_hy_f_w1[j], ev_hy_f_b1[j], ev_hy_f_freq[j],
                ev_hy_f_w2[j], ev_hy_f_b2[j], ev_hy_f_w3[j], ev_hy_skip[j], ev_w_out[j], route_params)
        else:
            x2d, route, counts = _odd_layer(x2d, b, s, mix_norm[layer], od_w_in[j], od_conv_w[j], od_conv_b[j],
                                            od_gate_b[j], od_out_norm[j], od_w_out[j], route_params)
        x2d = hier_moe_residual(x2d, route, counts, ffn_norm[layer], moe_w1, moe_w3, moe_w2, layer)
    return x2d.reshape(b, s, d)
```

```python
import functools
import math

import jax
import jax.numpy as jnp
import numpy as np
from jax import lax
from jax.experimental import pallas as pl
from jax.experimental.pallas import tpu as pltpu

F32 = jnp.float32
BF16 = jnp.bfloat16
I32 = jnp.int32

EPS = 1e-6
ROPE_THETA = 500000.0
ATT_HEADS = 4
ML_HEADS = 4
ML_CHUNK = 128
ML_BATCH_PER_STEP = 2
N_GROUPS = 4
EXPERTS_PER_GROUP = 8
N_EXPERTS = N_GROUPS * EXPERTS_PER_GROUP
HY_EMB_BANDS = 16
HY_MIN_DECAY = math.log(1e-2) / 1.5
HY_MAX_DECAY = math.log(1e-2) / 0.3

V7X_VMEM_BYTES = 64 * 1024 * 1024
VMEM_LIMIT = V7X_VMEM_BYTES - 8 * 1024 * 1024
NEG_BIG = -1e30


def _params(*sem, **kw):
    return pltpu.CompilerParams(dimension_semantics=sem, vmem_limit_bytes=VMEM_LIMIT, **kw)


def _sigmoid(x):
    return 1.0 / (1.0 + jnp.exp(-x))


def _norm_matmul_body(x_ref, g_ref, *refs, n_out, col_chunk):
    w_refs, o_refs = refs[:n_out], refs[n_out:]
    x = x_ref[...]
    ms = jnp.mean(x * x, axis=-1, keepdims=True)
    hn = (x * lax.rsqrt(ms + EPS) * g_ref[...]).astype(BF16)
    for w_ref, o_ref in zip(w_refs, o_refs):
        n = w_ref.shape[1]
        for c in range(0, n, col_chunk):
            ce = min(n, c + col_chunk)
            o_ref[:, c:ce] = jnp.dot(hn, w_ref[:, c:ce], preferred_element_type=F32).astype(o_ref.dtype)


def norm_matmul(x2d, g, ws, out_dtypes, tm=1024):
    t, d = x2d.shape
    in_specs = [pl.BlockSpec((tm, d), lambda i: (i, 0)), pl.BlockSpec((1, d), lambda i: (0, 0))]
    in_specs += [pl.BlockSpec(w.shape, lambda i: (0, 0)) for w in ws]
    out_specs = [pl.BlockSpec((tm, w.shape[1]), lambda i: (i, 0)) for w in ws]
    out_shape = [jax.ShapeDtypeStruct((t, w.shape[1]), dt) for w, dt in zip(ws, out_dtypes)]
    return pl.pallas_call(
        functools.partial(_norm_matmul_body, n_out=len(ws), col_chunk=1024),
        grid=(t // tm,), in_specs=in_specs, out_specs=out_specs, out_shape=out_shape,
        compiler_params=_params("parallel"), name="norm_matmul",
    )(x2d, g.reshape(1, d), *ws)


def _matmul_res_body(res_ref, *refs, n_in):
    a_refs, w_refs = refs[:n_in], refs[n_in:2 * n_in]
    g_ref, wt_ref, b_ref, o_ref, route_ref, cnt_ref, run_ref = refs[2 * n_in:]
    acc = res_ref[...]
    for a_ref, w_ref in zip(a_refs, w_refs):
        acc = acc + jnp.dot(a_ref[...], w_ref[...], preferred_element_type=F32)
    o_ref[...] = acc
    _route_epilogue(acc, g_ref, wt_ref, b_ref, route_ref, cnt_ref, run_ref)


def matmul_residual(res, a_list, w_list, route_params, tm=1024):
    t, d = res.shape
    r_in, r_args, r_out, r_shapes, r_scratch = _route_plumbing(t, d, tm, *route_params)
    in_specs = [pl.BlockSpec((tm, d), lambda i: (i, 0))]
    in_specs += [pl.BlockSpec((tm, a.shape[1]), lambda i: (i, 0)) for a in a_list]
    in_specs += [pl.BlockSpec(w.shape, lambda i: (0, 0)) for w in w_list]
    return pl.pallas_call(
        functools.partial(_matmul_res_body, n_in=len(a_list)),
        grid=(t // tm,), in_specs=in_specs + r_in,
        out_specs=[pl.BlockSpec((tm, d), lambda i: (i, 0))] + r_out,
        out_shape=[jax.ShapeDtypeStruct((t, d), F32)] + r_shapes, scratch_shapes=r_scratch,
        compiler_params=_params("arbitrary"), name="matmul_residual",
    )(res, *a_list, *w_list, *r_args)


def _rope_lane_tables(seq, dk, rope_dim, scale):
    half = rope_dim // 2
    f32 = np.float32
    inv_freq = (f32(1.0) / (f32(ROPE_THETA) ** (np.arange(0, rope_dim, 2, dtype=f32) / f32(rope_dim)))).astype(f32)
    ang = np.arange(seq, dtype=f32)[:, None] * inv_freq[None, :]
    cos, sin = np.cos(ang), np.sin(ang)
    d = np.arange(2 * dk) % dk
    fi = d % half
    c_tab = np.where(d[None, :] < rope_dim, cos[:, fi], 1.0)
    s1_tab = np.where(d[None, :] < half, -sin[:, fi], 0.0)
    s2_tab = np.where((d[None, :] >= half) & (d[None, :] < rope_dim), sin[:, fi], 0.0)
    return jnp.asarray((np.stack([c_tab, s1_tab, s2_tab]) * scale).astype(f32))


def _rope_block(x, g, c_tab, s1_tab, s2_tab, dk):
    lane = lax.broadcasted_iota(I32, x.shape, 1)
    lo = lane < dk
    x2 = x * x
    s_lo = jnp.sum(jnp.where(lo, x2, 0.0), axis=-1, keepdims=True)
    s_hi = jnp.sum(jnp.where(lo, 0.0, x2), axis=-1, keepdims=True)
    ms = jnp.where(lo, s_lo, s_hi) * (1.0 / dk)
    y = x * lax.rsqrt(ms + EPS) * g
    return y * c_tab + pltpu.roll(y, 120, 1) * s1_tab + pltpu.roll(y, 8, 1) * s2_tab


def _even_in_body(x_ref, g_ref, w_ref, gq_ref, gk_ref, tq_ref, tk_ref, qt_ref, k_ref, vt_ref, hy_ref, *, heads, dk):
    x = x_ref[...]
    ms = jnp.mean(x * x, axis=-1, keepdims=True)
    hn = (x * lax.rsqrt(ms + EPS) * g_ref[...]).astype(BF16)
    hw = heads * 128
    qk = jnp.dot(hn, w_ref[:, :2 * hw], preferred_element_type=F32)
    for h in range(heads):
        q = _rope_block(qk[:, h * 128:(h + 1) * 128], gq_ref[...], tq_ref[0], tq_ref[1], tq_ref[2], dk)
        qt_ref[h * 128:(h + 1) * 128, :] = q.T.astype(qt_ref.dtype)
        k = _rope_block(qk[:, hw + h * 128:hw + (h + 1) * 128], gk_ref[...], tk_ref[0], tk_ref[1], tk_ref[2], dk)
        k_ref[:, h * 128:(h + 1) * 128] = k.astype(k_ref.dtype)
    v = jnp.dot(hn, w_ref[:, 2 * hw:3 * hw], preferred_element_type=F32)
    for h in range(heads):
        vt_ref[h * 128:(h + 1) * 128, :] = v[:, h * 128:(h + 1) * 128].T.astype(vt_ref.dtype)
    for c in range(0, hy_ref.shape[1], hw):
        hy_ref[:, c:c + hw] = jnp.dot(hn, w_ref[:, 3 * hw + c:3 * hw + c + hw],
                                      preferred_element_type=F32).astype(hy_ref.dtype)


def even_in_proj(x2d, b, s, g, w, q_norm, k_norm, q_scale, tm=512):
    t, d = x2d.shape
    heads = ATT_HEADS
    hw = heads * 128
    dk = q_norm.shape[0]
    assert 2 * dk == 128 and dk // 4 == 16, "rope roll shifts assume 64-wide components, 16 rotary dims"
    nj = s // tm
    n_hy = w.shape[1] - 3 * hw
    tab_q = _rope_lane_tables(s, dk, dk // 4, q_scale)
    tab_k = _rope_lane_tables(s, dk, dk // 4, 1.0)
    row = lambda i: (i, 0)
    const = lambda i: (0, 0)
    tab_spec = pl.BlockSpec((3, tm, 128), lambda i: (0, i % nj, 0))
    return pl.pallas_call(
        functools.partial(_even_in_body, heads=heads, dk=dk), grid=(t // tm,),
        in_specs=[pl.BlockSpec((tm, d), row), pl.BlockSpec((1, d), const), pl.BlockSpec(w.shape, const),
                  pl.BlockSpec((1, 128), const), pl.BlockSpec((1, 128), const), tab_spec, tab_spec],
        out_specs=[pl.BlockSpec((None, hw, tm), lambda i: (i // nj, 0, i % nj)),
                   pl.BlockSpec((None, tm, hw), lambda i: (i // nj, i % nj, 0)),
                   pl.BlockSpec((None, hw, tm), lambda i: (i // nj, 0, i % nj)),
                   pl.BlockSpec((tm, n_hy), row)],
        out_shape=[jax.ShapeDtypeStruct((b, hw, s), BF16), jax.ShapeDtypeStruct((b, s, hw), BF16),
                   jax.ShapeDtypeStruct((b, hw, s), BF16), jax.ShapeDtypeStruct((t, n_hy), BF16)],
        compiler_params=_params("parallel"), name="even_in_proj",
    )(x2d, g.reshape(1, d), w, jnp.tile(q_norm, 2).reshape(1, 128).astype(F32),
      jnp.tile(k_norm, 2).reshape(1, 128).astype(F32), tab_q, tab_k)


def _attn_body(lam_ref, qt_ref, k_ref, vt_ref, g_ref, o_ref, *, tq, dk, post_scale):
    lam = lam_ref[0, 0]
    qt = qt_ref[...]
    row = lax.broadcasted_iota(I32, qt.shape, 0)
    zero = jnp.zeros_like(qt)
    qq = jnp.concatenate([jnp.where(row < dk, qt, zero), jnp.where(row < dk, zero, qt)], axis=1)
    st = jnp.dot(k_ref[...], qq, preferred_element_type=F32)
    m = jnp.max(st, axis=0, keepdims=True)
    p = jnp.exp2(st - m)
    r = 1.0 / jnp.sum(p, axis=0, keepdims=True)
    ot = jnp.dot(vt_ref[...], p.astype(BF16), preferred_element_type=F32)
    o = (ot[:, :tq] * r[:, :tq] - ot[:, tq:] * (lam * r[:, tq:])).T
    ms = jnp.mean(o * o, axis=-1, keepdims=True)
    o_ref[...] = (o * lax.rsqrt(ms + EPS) * g_ref[...] * post_scale).astype(o_ref.dtype)


def diff_attention(qt, k, vt, lam, subln, lambda_init, tq=512):
    b, s, _ = k.shape
    h = ATT_HEADS
    return pl.pallas_call(
        functools.partial(_attn_body, tq=tq, dk=64, post_scale=1.0 - lambda_init),
        grid=(b, h, s // tq),
        in_specs=[pl.BlockSpec(memory_space=pltpu.SMEM),
                  pl.BlockSpec((None, 128, tq), lambda bi, hi, i: (bi, hi, i)),
                  pl.BlockSpec((None, s, 128), lambda bi, hi, i: (bi, 0, hi)),
                  pl.BlockSpec((None, 128, s), lambda bi, hi, i: (bi, hi, 0)),
                  pl.BlockSpec((1, 128), lambda bi, hi, i: (0, 0))],
        out_specs=pl.BlockSpec((None, tq, 128), lambda bi, hi, i: (bi, i, hi)),
        out_shape=jax.ShapeDtypeStruct((b, s, h * 128), BF16),
        compiler_params=_params("parallel", "parallel", "parallel"), name="diff_attention",
    )(lam.reshape(1, 1).astype(F32), qt, k, vt, subln.reshape(1, 128).astype(F32))


def _conv3(u_ref, w_ref, b_ref):
    x = u_ref[...].astype(F32)
    s = x.shape[0]
    row = lax.broadcasted_iota(I32, x.shape, 0)
    x_prev = jnp.where(row == 0, 0.0, pltpu.roll(x, 1, 0))
    x_next = jnp.where(row == s - 1, 0.0, pltpu.roll(x, s - 1, 0))
    w = w_ref[...]
    return b_ref[...] + x_prev * w[0:1] + x * w[1:2] + x_next * w[2:3]


FFT_N1 = 64
FFT_UNROLL = 128
FFT_PAD = 8


def _hy_prep_body(x1_ref, x2_ref, v_ref, w1_ref, w2_ref, wv_ref, b1_ref, b2_ref, bv_ref, z_ref, x1c_ref, zp_ref):
    x1c_ref[...] = _conv3(x1_ref, w1_ref, b1_ref).astype(x1c_ref.dtype)
    z = _conv3(v_ref, wv_ref, bv_ref) * _conv3(x2_ref, w2_ref, b2_ref)
    z_ref[...] = z.astype(z_ref.dtype)
    nb = z.shape[0] // FFT_N1
    zp_ref[...] = jnp.zeros_like(zp_ref)
    for n2 in range(nb):
        for ci in range(z.shape[1] // 128):
            zp_ref[ci, pl.ds(n2, FFT_N1, stride=nb + FFT_PAD), :] = (
                z[n2 * FFT_N1:(n2 + 1) * FFT_N1, ci * 128:(ci + 1) * 128])


def hy_prep(u, col0, conv_w, conv_b, tc=256):
    b, s, _ = u.shape
    d_hy = conv_w.shape[1] // 3
    nct = d_hy // tc
    blk0 = col0 // tc
    sp = FFT_N1 * (s // FFT_N1 + FFT_PAD)

    def uspec(part):
        return pl.BlockSpec((None, s, tc), lambda bi, c: (bi, 0, blk0 + part * nct + c))

    def wspec(part, rows):
        return pl.BlockSpec((rows, tc), lambda bi, c: (0, part * nct + c))

    ospec = pl.BlockSpec((None, s, tc), lambda bi, c: (bi, 0, c))
    return pl.pallas_call(
        _hy_prep_body, grid=(b, nct),
        in_specs=[uspec(0), uspec(1), uspec(2), wspec(0, 3), wspec(1, 3), wspec(2, 3),
                  wspec(0, 1), wspec(1, 1), wspec(2, 1)],
        out_specs=[ospec, ospec, pl.BlockSpec((None, tc // 128, sp, 128), lambda bi, c: (bi, c, 0, 0))],
        out_shape=[jax.ShapeDtypeStruct((b, s, d_hy), BF16), jax.ShapeDtypeStruct((b, s, d_hy), BF16),
                   jax.ShapeDtypeStruct((b, d_hy // 128, sp, 128), F32)],
        compiler_params=_params("parallel", "parallel"), name="hy_prep",
    )(u, u, u, conv_w, conv_w, conv_w, conv_b.reshape(1, -1), conv_b.reshape(1, -1), conv_b.reshape(1, -1))


def _hy_filter_body(z_ref, w1_ref, b1_ref, fr_ref, w2_ref, b2_ref, w3f_ref, w3b_ref, dec_ref, hf_ref, hb_ref):
    hp = lax.Precision.HIGHEST
    fr = fr_ref[...]
    hdn = jnp.sin(fr * (jnp.dot(z_ref[...], w1_ref[...], precision=hp, preferred_element_type=F32) + b1_ref[...]))
    hdn = jnp.sin(fr * (jnp.dot(hdn, w2_ref[...], precision=hp, preferred_element_type=F32) + b2_ref[...]))
    dec = dec_ref[...]
    h_fwd = jnp.dot(hdn, w3f_ref[...], precision=hp, preferred_element_type=F32) * dec
    h_bwd = jnp.dot(hdn, w3b_ref[...], precision=hp, preferred_element_type=F32) * dec
    row0 = lax.broadcasted_iota(I32, h_fwd.shape, 0) == 0
    h_fwd = h_fwd + jnp.where(row0, h_bwd, 0.0)
    h_bwd = jnp.where(row0, 0.0, h_bwd)
    norm = (jnp.sum(jnp.abs(h_fwd), axis=0, keepdims=True) + jnp.sum(jnp.abs(h_bwd), axis=0, keepdims=True) + EPS)
    hf_ref[...] = h_fwd / norm
    hb_ref[...] = h_bwd / norm


def hyena_filter(length, w1, b1, freq, w2, b2, w3, tc=256):
    d_hy = w3.shape[1] // 2
    ff = w1.shape[1]
    f32 = np.float32
    t = np.linspace(0.0, 1.0, length, dtype=f32)[:, None]
    bands = np.linspace(1e-4, HY_EMB_BANDS - 1, HY_EMB_BANDS, dtype=f32)[None, :]
    ang = (f32(2.0 * math.pi / length) * np.arange(length, dtype=f32)[:, None] * bands).astype(f32)
    feat = np.concatenate([t, np.cos(ang), -np.sin(ang)], axis=-1).astype(f32)
    emb = feat.shape[1]
    z = jnp.asarray(np.pad(feat, ((0, 0), (0, 128 - emb))))
    w1p = jnp.zeros((128, ff), F32).at[:emb].set(w1)
    deltas = np.abs(np.linspace(HY_MIN_DECAY, HY_MAX_DECAY, d_hy, dtype=f32))
    decay = jnp.asarray(np.exp(-t * deltas[None, :]).astype(f32))
    const = lambda c: (0, 0)
    col = lambda c: (0, c)
    out = jax.ShapeDtypeStruct((length, d_hy), F32)
    h_fwd, h_bwd = pl.pallas_call(
        _hy_filter_body, grid=(d_hy // tc,),
        in_specs=[pl.BlockSpec((length, 128), const), pl.BlockSpec((128, ff), const), pl.BlockSpec((1, ff), const),
                  pl.BlockSpec((1, ff), const), pl.BlockSpec((ff, ff), const), pl.BlockSpec((1, ff), const),
                  pl.BlockSpec((ff, tc), col), pl.BlockSpec((ff, tc), lambda c: (0, d_hy // tc + c)),
                  pl.BlockSpec((length, tc), col)],
        out_specs=[pl.BlockSpec((length, tc), col), pl.BlockSpec((length, tc), col)],
        out_shape=[out, out], compiler_params=_params("parallel"), name="hyena_filter",
    )(z, w1p, b1.reshape(1, ff), freq.reshape(1, ff), w2, b2.reshape(1, ff), w3, w3, decay)
    return jnp.concatenate([h_fwd, h_bwd], axis=1)


def fft_tables(length):
    n = 2 * length
    n1c, n2c, nb = FFT_N1, n // FFT_N1, length // FFT_N1
    unit = 2.0 * math.pi / n
    i1 = np.arange(n1c, dtype=np.int64)
    i2 = np.arange(n2c, dtype=np.int64)
    ib = np.arange(nb, dtype=np.int64)
    samp = i1[:, None, None] + n1c * ib[None, None, :]
    ang = ((i2[None, :, None] * samp) % n) * unit
    m1 = np.concatenate([np.cos(ang), -np.sin(ang)], axis=1)
    ang = ((i1[:, None] * i1[None, :]) % n1c) * (2.0 * math.pi / n1c)
    c, s = np.cos(ang), np.sin(ang)
    f1 = np.concatenate([np.concatenate([c, s], axis=1), np.concatenate([-s, c], axis=1)], axis=0)
    freq = n2c * i1[None, None, :] + i2[:, None, None]
    ang = ((i1[None, :, None] * freq) % n) * unit
    c, s = np.cos(ang), np.sin(ang)
    g1 = np.concatenate([np.concatenate([c, -s], axis=2), np.concatenate([s, c], axis=2)], axis=1)
    ang = ((ib[:, None] * i2[None, :]) % n2c) * (2.0 * math.pi / n2c)
    g2 = np.concatenate([np.cos(ang), -np.sin(ang)], axis=1) * (1.0 / n)
    return tuple(jnp.asarray(t.astype(np.float32).astype(BF16)) for t in (m1, f1, g1, g2))


def _fft_stage1(xp_ref, m1_ref, p_ref):
    nb = m1_ref.shape[2]
    n2c = m1_ref.shape[1] // 2

    def body(n1, c):
        x = xp_ref[pl.ds(pl.multiple_of(n1 * (nb + FFT_PAD), 8), nb), :].astype(BF16)
        a = jnp.dot(m1_ref[n1], x, preferred_element_type=F32)
        p_ref[0, pl.ds(n1, n2c, stride=FFT_N1 + FFT_PAD), :] = a[:n2c]
        p_ref[1, pl.ds(n1, n2c, stride=FFT_N1 + FFT_PAD), :] = a[n2c:]
        return c
    lax.fori_loop(0, FFT_N1, body, 0, unroll=FFT_UNROLL)


def _fft_stage2(p_ref, f1_ref, k2):
    r0 = pl.multiple_of(k2 * (FFT_N1 + FFT_PAD), 8)
    slab = jnp.concatenate([p_ref[0, pl.ds(r0, FFT_N1), :], p_ref[1, pl.ds(r0, FFT_N1), :]], axis=0)
    return jnp.dot(f1_ref[...], slab.astype(BF16), preferred_element_type=F32)


def _spectrum_body(xp_ref, m1_ref, f1_ref, o_ref, p_ref):
    _fft_stage1(xp_ref, m1_ref, p_ref)

    def body(k2, c):
        o_ref[k2] = _fft_stage2(p_ref, f1_ref, k2)
        return c
    lax.fori_loop(0, o_ref.shape[0], body, 0, unroll=FFT_UNROLL)


def filter_spectrum(ab, m1, f1):
    length, c2 = ab.shape
    nb = length // FFT_N1
    n2c = m1.shape[1] // 2
    nch = c2 // 128
    abp = jnp.pad(ab.reshape(nb, FFT_N1, nch, 128).transpose(2, 1, 0, 3), ((0, 0), (0, 0), (0, FFT_PAD), (0, 0)))
    abp = abp.reshape(nch, FFT_N1 * (nb + FFT_PAD), 128)
    spec = pl.pallas_call(
        _spectrum_body, grid=(nch,),
        in_specs=[pl.BlockSpec((None,) + abp.shape[1:], lambda c: (c, 0, 0)),
                  pl.BlockSpec(m1.shape, lambda c: (0, 0, 0)), pl.BlockSpec(f1.shape, lambda c: (0, 0))],
        out_specs=pl.BlockSpec((None, n2c, 2 * FFT_N1, 128), lambda c: (c, 0, 0, 0)),
        out_shape=jax.ShapeDtypeStruct((nch, n2c, 2 * FFT_N1, 128), F32),
        scratch_shapes=[pltpu.VMEM((2, n2c * (FFT_N1 + FFT_PAD), 128), F32)],
        compiler_params=_params("parallel"), name="filter_spectrum",
    )(abp, m1, f1)
    fa, fb = spec[:nch // 2], spec[nch // 2:]
    h = FFT_N1
    return jnp.concatenate([fa[:, :, :h] + fb[:, :, :h], fa[:, :, h:] - fb[:, :, h:]], axis=2).astype(BF16)


def _hy_fft_body(zp_ref, z_ref, x1c_ref, kf_ref, m1_ref, f1_ref, g1_ref, g2_ref, skip_ref, o_ref,
                 p_ref, q_ref, y_ref):
    h = FFT_N1
    n2c = g1_ref.shape[0]
    nb = g2_ref.shape[0]
    _fft_stage1(zp_ref, m1_ref, p_ref)

    def mid(k2, c):
        xf = _fft_stage2(p_ref, f1_ref, k2)
        kf = kf_ref[k2].astype(F32)
        xr, xi, kr, ki = xf[:h], xf[h:], kf[:h], kf[h:]
        y = jnp.concatenate([xr * kr - xi * ki, xr * ki + xi * kr], axis=0).astype(BF16)
        d = jnp.dot(g1_ref[k2], y, preferred_element_type=F32)
        q_ref[0, pl.ds(k2, h, stride=n2c + FFT_PAD), :] = d[:h]
        q_ref[1, pl.ds(k2, h, stride=n2c + FFT_PAD), :] = d[h:]
        return c
    lax.fori_loop(0, n2c, mid, 0, unroll=FFT_UNROLL)

    def last(t1, c):
        r0 = pl.multiple_of(t1 * (n2c + FFT_PAD), 8)
        slab = jnp.concatenate([q_ref[0, pl.ds(r0, n2c), :], q_ref[1, pl.ds(r0, n2c), :]], axis=0)
        y_ref[pl.ds(t1, nb, stride=h), :] = jnp.dot(g2_ref[...], slab.astype(BF16), preferred_element_type=F32)
        return c
    lax.fori_loop(0, h, last, 0, unroll=FFT_UNROLL)

    z = z_ref[...].astype(F32)
    o_ref[...] = ((y_ref[...] + z * skip_ref[...]) * x1c_ref[...].astype(F32)).astype(o_ref.dtype)


def hy_fft_conv(zp, z, x1c, kf, tables, skip):
    m1, f1, g1, g2 = tables
    b, s, c = z.shape
    nch = c // 128

    def const(shape):
        return pl.BlockSpec(shape, lambda ci, bi: (0,) * len(shape))

    nat = pl.BlockSpec((None, s, 128), lambda ci, bi: (bi, 0, ci))
    return pl.pallas_call(
        _hy_fft_body, grid=(nch, b),
        in_specs=[pl.BlockSpec((None, None) + zp.shape[2:], lambda ci, bi: (bi, ci, 0, 0)), nat, nat,
                  pl.BlockSpec((None,) + kf.shape[1:], lambda ci, bi: (ci, 0, 0, 0)),
                  const(m1.shape), const(f1.shape), const(g1.shape), const(g2.shape),
                  pl.BlockSpec((1, 128), lambda ci, bi: (0, ci))],
        out_specs=nat,
        out_shape=jax.ShapeDtypeStruct((b, s, c), BF16),
        scratch_shapes=[pltpu.VMEM((2, g1.shape[0] * (FFT_N1 + FFT_PAD), 128), F32),
                        pltpu.VMEM((2, FFT_N1 * (g1.shape[0] + FFT_PAD), 128), F32),
                        pltpu.VMEM((s, 128), F32)],
        compiler_params=_params("parallel", "parallel"), name="hy_fft_conv",
    )(zp, z, x1c, kf, m1, f1, g1, g2, skip.reshape(1, c).astype(F32))


def _ml_prep_body(u_ref, w_ref, b_ref, o_ref, *, scale, transpose):
    y = _conv3(u_ref, w_ref, b_ref)
    y = y * _sigmoid(y) * scale
    o_ref[...] = (y.T if transpose else y).astype(o_ref.dtype)


def ml_prep(u, conv_w, conv_b, col0, ncols, scale, transpose, tc=256):
    b, s, _ = u.shape
    w = conv_w.shape[1]
    c0 = col0 // tc
    if transpose:
        out_spec = pl.BlockSpec((None, tc, s), lambda bi, c: (bi, c, 0))
        out_shape = jax.ShapeDtypeStruct((b, ncols, s), BF16)
    else:
        out_spec = pl.BlockSpec((None, s, tc), lambda bi, c: (bi, 0, c))
        out_shape = jax.ShapeDtypeStruct((b, s, ncols), BF16)
    return pl.pallas_call(
        functools.partial(_ml_prep_body, scale=scale, transpose=transpose), grid=(b, ncols // tc),
        in_specs=[pl.BlockSpec((None, s, tc), lambda bi, c: (bi, 0, c0 + c)),
                  pl.BlockSpec((3, tc), lambda bi, c: (0, c0 + c)),
                  pl.BlockSpec((1, tc), lambda bi, c: (0, c0 + c))],
        out_specs=out_spec, out_shape=out_shape,
        compiler_params=_params("parallel", "parallel"), name="ml_prep",
    )(u, conv_w, conv_b.reshape(1, w))


def _log_sigmoid(x):
    return jnp.minimum(x, 0.0) - jnp.log(1.0 + jnp.exp(-jnp.abs(x)))


def _dot_split(a, b, a_is_f32):
    x = a if a_is_f32 else b
    hi = x.astype(BF16)
    lo = (x - hi.astype(F32)).astype(BF16)
    if a_is_f32:
        return (jnp.dot(hi, b, preferred_element_type=F32) + jnp.dot(lo, b, preferred_element_type=F32))
    return (jnp.dot(a, hi, preferred_element_type=F32) + jnp.dot(a, lo, preferred_element_type=F32))


def _mlstm_chain(q, kt, v, bc, br, li_r, total, mask, c_ref, m_ref, idx):
    dv = v.shape[1] - 128
    c_st = c_ref[idx]
    m_st = m_ref[idx:idx + 1, 0:1]
    key_row = li_r - br
    inter = bc + m_st
    m_t = jnp.maximum(inter, bc + jnp.max(jnp.where(mask, key_row, NEG_BIG), axis=-1, keepdims=True))
    w_intra = jnp.exp(jnp.where(mask, (bc - m_t) + key_row, NEG_BIG))
    w_inter = jnp.exp(inter - m_t)
    sc = jnp.dot(q, kt, preferred_element_type=F32) * w_intra
    q_inter = (q.astype(F32) * w_inter).astype(BF16)
    both = jnp.dot(jnp.concatenate([q_inter, sc.astype(BF16)], axis=1),
                   jnp.concatenate([c_st.astype(BF16), v], axis=0),
                   preferred_element_type=F32)
    den = both[:, dv:dv + 1]
    h = both[:, :dv] / jnp.maximum(jnp.abs(den), jnp.exp(-m_t))
    g_row = total + key_row
    m_next = jnp.maximum(total + m_st, jnp.max(g_row, axis=-1, keepdims=True))
    a_prev = jnp.exp(total + m_st - m_next)
    kwt = (kt.astype(F32) * jnp.exp(g_row - m_next)).astype(BF16)
    c_ref[idx] = a_prev * c_st + jnp.dot(kwt, v, preferred_element_type=F32)
    m_ref[idx:idx + 1, :] = jnp.broadcast_to(m_next, (1, m_ref.shape[1]))
    return h


def _mlstm_body(qf_ref, ktf_ref, vf_ref, gcf_ref, grf_ref, qb_ref, ktb_ref, vb_ref, gcb_ref, grb_ref,
                hf_ref, hb_ref, c_ref, m_ref, *, heads, dk, dv):
    @pl.when(pl.program_id(1) == 0)
    def _():
        c_ref[...] = jnp.zeros_like(c_ref)
        m_ref[...] = jnp.zeros_like(m_ref)

    lc = qf_ref.shape[1]
    ones_blk = jnp.where(lax.broadcasted_iota(I32, (lc, 128), 1) == 0, 1.0, 0.0).astype(BF16)
    t_i = lax.broadcasted_iota(I32, (lc, lc), 0)
    s_i = lax.broadcasted_iota(I32, (lc, lc), 1)
    lower = s_i <= t_i
    upper = s_i >= t_i
    ltri = jnp.where(lower, 1.0, 0.0).astype(BF16)
    utri = jnp.where(upper, 1.0, 0.0).astype(BF16)

    for bb in range(qf_ref.shape[0]):
        for direction, (q_ref, kt_ref, v_ref, gc_ref, gr_ref, h_ref) in enumerate(
                ((qf_ref, ktf_ref, vf_ref, gcf_ref, grf_ref, hf_ref),
                 (qb_ref, ktb_ref, vb_ref, gcb_ref, grb_ref, hb_ref))):
            fwd = direction == 0
            gc = gc_ref[bb]
            gr = gr_ref[bb]
            lfc, lfr = _log_sigmoid(gc), _log_sigmoid(gr)
            cum_c = _dot_split(ltri if fwd else utri, lfc, a_is_f32=False)
            cum_r = _dot_split(lfr, utri if fwd else ltri, a_is_f32=True)
            for hd in range(heads):
                gi = (0 if fwd else 2) * heads + hd
                gf = (1 if fwd else 3) * heads + hd
                bc, br = cum_c[:, gf:gf + 1], cum_r[gf:gf + 1, :]
                total = br[:, lc - 1:lc] if fwd else br[:, 0:1]
                q = q_ref[bb, :, hd * dk:(hd + 1) * dk]
                kt = kt_ref[bb, hd * dk:(hd + 1) * dk, :]
                v = jnp.concatenate([v_ref[bb, :, hd * dv:(hd + 1) * dv], ones_blk], axis=1)
                h = _mlstm_chain(q, kt, v, bc, br, gr[gi:gi + 1, :], total,
                                 lower if fwd else upper, c_ref, m_ref, (bb * 2 + direction) * heads + hd)
                h_ref[bb, :, hd * dv:(hd + 1) * dv] = h.astype(h_ref.dtype)


def mlstm_scan(q, kt, u, v_blk, gcol, grow):
    b, s, w = q.shape
    heads = ML_HEADS
    dk = w // heads
    dv = 2 * dk
    lc = ML_CHUNK
    nc = s // lc
    ng = gcol.shape[-1]

    def fw(bi, j):
        return j

    def bw(bi, j):
        return nc - 1 - j

    nbs = ML_BATCH_PER_STEP if b % ML_BATCH_PER_STEP == 0 else 1

    def specs(pos):
        return [pl.BlockSpec((nbs, lc, w), lambda bi, j: (bi, pos(bi, j), 0)),
                pl.BlockSpec((nbs, w, lc), lambda bi, j: (bi, 0, pos(bi, j))),
                pl.BlockSpec((nbs, lc, heads * dv), lambda bi, j: (bi, pos(bi, j), v_blk)),
                pl.BlockSpec((nbs, lc, ng), lambda bi, j: (bi, pos(bi, j), 0)),
                pl.BlockSpec((nbs, ng, lc), lambda bi, j: (bi, 0, pos(bi, j)))]

    hshape = jax.ShapeDtypeStruct((b, s, heads * dv), BF16)
    return pl.pallas_call(
        functools.partial(_mlstm_body, heads=heads, dk=dk, dv=dv), grid=(b // nbs, nc),
        in_specs=specs(fw) + specs(bw),
        out_specs=[pl.BlockSpec((nbs, lc, heads * dv), lambda bi, j: (bi, j, 0)),
                   pl.BlockSpec((nbs, lc, heads * dv), lambda bi, j: (bi, nc - 1 - j, 0))],
        out_shape=[hshape, hshape],
        scratch_shapes=[pltpu.VMEM((nbs * 2 * heads, dk, dv + 128), F32),
                        pltpu.VMEM((nbs * 2 * heads, 128), F32)],
        compiler_params=_params("parallel", "arbitrary"), name="mlstm_scan",
    )(q, kt, u, gcol, grow, q, kt, u, gcol, grow)


def _mlstm_out_body(res_ref, hf_ref, hb_ref, o_ref, g_ref, w_ref, gf_ref, wt_ref, rb_ref,
                    out_ref, route_ref, cnt_ref, run_ref, *, heads):
    hs = hf_ref[...].astype(F32) + hb_ref[...].astype(F32)
    dv = hs.shape[1] // heads
    g = g_ref[...]
    parts = []
    for hd in range(heads):
        seg = hs[:, hd * dv:(hd + 1) * dv]
        ms = jnp.mean(seg * seg, axis=-1, keepdims=True)
        parts.append(seg * lax.rsqrt(ms + EPS) * g[:, hd * dv:(hd + 1) * dv])
    a = jnp.concatenate(parts, axis=-1) * _sigmoid(o_ref[...].astype(F32))
    x_new = res_ref[...] + jnp.dot(a.astype(BF16), w_ref[...], preferred_element_type=F32)
    out_ref[...] = x_new
    _route_epilogue(x_new, gf_ref, wt_ref, rb_ref, route_ref, cnt_ref, run_ref)


def mlstm_out(res, hf, hb, u2d, o_blk, gain, w_out, route_params, tm=1024):
    t, d = res.shape
    row = lambda i: (i, 0)
    r_in, r_args, r_out, r_shapes, r_scratch = _route_plumbing(t, d, tm, *route_params)
    return pl.pallas_call(
        functools.partial(_mlstm_out_body, heads=ML_HEADS), grid=(t // tm,),
        in_specs=[pl.BlockSpec((tm, d), row), pl.BlockSpec((tm, d), row), pl.BlockSpec((tm, d), row),
                  pl.BlockSpec((tm, d), lambda i: (i, o_blk)), pl.BlockSpec((1, d), lambda i: (0, 0)),
                  pl.BlockSpec((d, d), lambda i: (0, 0))] + r_in,
        out_specs=[pl.BlockSpec((tm, d), row)] + r_out,
        out_shape=[jax.ShapeDtypeStruct((t, d), F32)] + r_shapes, scratch_shapes=r_scratch,
        compiler_params=_params("arbitrary"), name="mlstm_out",
    )(res, hf, hb, u2d, gain.reshape(1, d), w_out, *r_args)


ROUTE_TILE = 512
ROUTE_ROWS = 128
EXPERT_ROW0 = 8


def _route_tile(x, g, wt, bias, run_ref):
    ms = jnp.mean(x * x, axis=-1, keepdims=True)
    xn = x * lax.rsqrt(ms + EPS) * g
    w_hi, x_hi = wt.astype(BF16), xn.astype(BF16)
    w_lo, x_lo = (wt - w_hi.astype(F32)).astype(BF16), (xn - x_hi.astype(F32)).astype(BF16)
    nt = (((1,), (1,)), ((), ()))
    logit = (lax.dot_general(w_hi, x_hi, nt, preferred_element_type=F32)
             + lax.dot_general(w_lo, x_hi, nt, preferred_element_type=F32)
             + lax.dot_general(w_hi, x_lo, nt, preferred_element_type=F32)) + bias
    rows = [logit[r:r + 1, :] for r in range(EXPERT_ROW0 + N_EXPERTS)]
    g_best, g_idx = rows[0], jnp.zeros_like(rows[0])
    for gi in range(1, N_GROUPS):
        better = rows[gi] > g_best
        g_best = jnp.where(better, rows[gi], g_best)
        g_idx = jnp.where(better, float(gi), g_idx)
    g_den = sum(jnp.exp(rows[gi] - g_best) for gi in range(N_GROUPS))
    g_w = 1.0 / g_den
    sel = []
    for e in range(EXPERTS_PER_GROUP):
        v = rows[EXPERT_ROW0 + e]
        for gi in range(1, N_GROUPS):
            v = jnp.where(g_idx == float(gi), rows[EXPERT_ROW0 + gi * EXPERTS_PER_GROUP + e], v)
        sel.append(v)
    v1, i1 = sel[0], jnp.zeros_like(sel[0])
    for e in range(1, EXPERTS_PER_GROUP):
        better = sel[e] > v1
        v1 = jnp.where(better, sel[e], v1)
        i1 = jnp.where(better, float(e), i1)
    v2, i2 = jnp.full_like(v1, -jnp.inf), jnp.zeros_like(v1)
    for e in range(EXPERTS_PER_GROUP):
        better = (sel[e] > v2) & (i1 != float(e))
        v2 = jnp.where(better, sel[e], v2)
        i2 = jnp.where(better, float(e), i2)
    e21 = jnp.exp(v2 - v1)
    gate1 = g_w / (1.0 + e21)
    gate2 = gate1 * e21
    base = g_idx * float(EXPERTS_PER_GROUP)
    e1, e2 = base + i1, base + i2
    tm = e1.shape[1]
    erow = lax.broadcasted_iota(I32, (N_EXPERTS, tm), 0).astype(F32)
    oh1 = jnp.where(erow == e1, 1.0, 0.0)
    oh2 = jnp.where(erow == e2, 1.0, 0.0)
    cnt = oh1 + oh2
    earlier = jnp.where(lax.broadcasted_iota(I32, (tm, tm), 0) < lax.broadcasted_iota(I32, (tm, tm), 1),
                        1.0, 0.0).astype(BF16)
    run = run_ref[...]
    pos = run[:, 0:1] + jnp.dot(cnt.astype(BF16), earlier, preferred_element_type=F32)
    rank1 = jnp.sum(oh1 * pos, axis=0, keepdims=True)
    rank2 = jnp.sum(oh2 * pos, axis=0, keepdims=True)
    run = run + jnp.sum(cnt, axis=1, keepdims=True)
    run_ref[...] = run
    zero = jnp.zeros_like(v1)
    return jnp.concatenate([e1, e2, gate1, gate2, rank1, rank2, zero, zero], axis=0)


def _route_epilogue(x_new, g_ref, wt_ref, b_ref, route_ref, cnt_ref, run_ref):
    @pl.when(pl.program_id(0) == 0)
    def _():
        run_ref[...] = jnp.zeros_like(run_ref)

    for r0 in range(0, x_new.shape[0], ROUTE_TILE):
        route_ref[:, r0:r0 + ROUTE_TILE] = _route_tile(x_new[r0:r0 + ROUTE_TILE], g_ref[...], wt_ref[...],
                                                       b_ref[...], run_ref)
    cnt_ref[...] = run_ref[...]


def _route_plumbing(t, d, tm, g, wg, bg, we, be):
    wt = jnp.zeros((ROUTE_ROWS, d), F32).at[:N_GROUPS].set(wg.T).at[EXPERT_ROW0:EXPERT_ROW0 + N_EXPERTS].set(we.T)
    bias = jnp.zeros((ROUTE_ROWS, 1), F32).at[:N_GROUPS, 0].set(bg).at[EXPERT_ROW0:EXPERT_ROW0 + N_EXPERTS, 0].set(be)
    in_specs = [pl.BlockSpec((1, d), lambda i: (0, 0)), pl.BlockSpec((ROUTE_ROWS, d), lambda i: (0, 0)),
                pl.BlockSpec((ROUTE_ROWS, 1), lambda i: (0, 0))]
    out_specs = [pl.BlockSpec((8, tm), lambda i: (0, i)), pl.BlockSpec((N_EXPERTS, 128), lambda i: (0, 0))]
    out_shapes = [jax.ShapeDtypeStruct((8, t), F32), jax.ShapeDtypeStruct((N_EXPERTS, 128), F32)]
    return in_specs, [g.reshape(1, d), wt, bias], out_specs, out_shapes, [pltpu.VMEM((N_EXPERTS, 128), F32)]


def _dispatch_body(dest_ref, zblk_ref, x_ref, g_ref, xs_ref, buf_ref, sem_ref, *, td, ntok):
    i = pl.program_id(0)
    n = pl.num_programs(0)
    slot = i % 2

    @pl.when(i == 0)
    def _():
        buf_ref[1] = jnp.zeros(buf_ref.shape[1:], buf_ref.dtype)

        def zero_copy(j):
            return pltpu.make_async_copy(buf_ref.at[1], xs_ref.at[pl.ds(zblk_ref[j] * td, td), :], sem_ref.at[1])

        def start(j, c):
            @pl.when(zblk_ref[j] >= 0)
            def _():
                zero_copy(j).start()
            return c

        def wait(j, c):
            @pl.when(zblk_ref[j] >= 0)
            def _():
                zero_copy(j).wait()
            return c
        lax.fori_loop(0, zblk_ref.shape[0], start, 0)
        lax.fori_loop(0, zblk_ref.shape[0], wait, 0)

    def row_copy(r, kk):
        return pltpu.make_async_copy(buf_ref.at[slot, pl.ds(r, 1), :],
                                     xs_ref.at[pl.ds(dest_ref[kk * ntok + i * td + r], 1), :],
                                     sem_ref.at[slot])

    def wait_buffer(sl):
        for _ in range(2):
            pltpu.make_async_copy(buf_ref.at[sl], xs_ref.at[pl.ds(0, td), :], sem_ref.at[sl]).wait()

    @pl.when(i >= 2)
    def _():
        wait_buffer(slot)

    x = x_ref[...]
    ms = jnp.mean(x * x, axis=-1, keepdims=True)
    buf_ref[slot] = x * lax.rsqrt(ms + EPS) * g_ref[...]

    for r in range(td):
        row_copy(r, 0).start(priority=0)
        row_copy(r, 1).start(priority=1)

    @pl.when(i == n - 1)
    def _():
        @pl.when(n >= 2)
        def _():
            wait_buffer(1 - slot)
        wait_buffer(slot)


def moe_dispatch(x2d, g, dest, zero_blk, cap, td):
    t, d = x2d.shape
    grid_spec = pltpu.PrefetchScalarGridSpec(
        num_scalar_prefetch=2, grid=(t // td,),
        in_specs=[pl.BlockSpec((td, d), lambda i, dest, zb: (i, 0)),
                  pl.BlockSpec((1, d), lambda i, dest, zb: (0, 0))],
        out_specs=pl.BlockSpec(memory_space=pl.ANY),
        scratch_shapes=[pltpu.VMEM((2, td, d), F32), pltpu.SemaphoreType.DMA((2,))])
    return pl.pallas_call(
        functools.partial(_dispatch_body, td=td, ntok=t), grid_spec=grid_spec,
        out_shape=jax.ShapeDtypeStruct((cap, d), F32),
        compiler_params=_params("arbitrary", disable_bounds_checks=True),
        name="moe_dispatch",
    )(dest, zero_blk, x2d, g.reshape(1, d))


def _expert_body(blk_e_ref, nused_ref, xs_ref, w1_ref, w3_ref, w2_ref, ys_ref, w1b_ref, w3b_ref, w2b_ref):
    i = pl.program_id(0)
    nused = nused_ref[0]
    last = nused - 1
    cur = blk_e_ref[jnp.minimum(i, last)]
    prev = blk_e_ref[jnp.minimum(jnp.maximum(i - 1, 0), last)]

    @pl.when((i == 0) | (cur != prev))
    def _():
        w1b_ref[...] = w1_ref[...].astype(BF16)
        w3b_ref[...] = w3_ref[...].astype(BF16)
        w2b_ref[...] = w2_ref[...].astype(BF16)

    @pl.when(i < nused)
    def _():
        x = xs_ref[...].astype(BF16)
        h1 = jnp.dot(x, w1b_ref[...], preferred_element_type=F32)
        h3 = jnp.dot(x, w3b_ref[...], preferred_element_type=F32)
        hid = (h1 * _sigmoid(h1) * h3).astype(BF16)
        ys_ref[...] = jnp.dot(hid, w2b_ref[...], preferred_element_type=F32)

    @pl.when(i >= nused)
    def _():
        ys_ref[...] = jnp.zeros_like(ys_ref)


def moe_experts(xs, blk_e, nused, w1, w3, w2, layer, tm):
    cap, d = xs.shape
    de = w1.shape[-1]

    def blk(i, be, nu):
        return jnp.minimum(i, nu[0] - 1)

    grid_spec = pltpu.PrefetchScalarGridSpec(
        num_scalar_prefetch=2, grid=(cap // tm,),
        in_specs=[pl.BlockSpec((tm, d), lambda i, be, nu: (blk(i, be, nu), 0)),
                  pl.BlockSpec((None, None, d, de), lambda i, be, nu: (layer, be[blk(i, be, nu)], 0, 0)),
                  pl.BlockSpec((None, None, d, de), lambda i, be, nu: (layer, be[blk(i, be, nu)], 0, 0)),
                  pl.BlockSpec((None, None, de, d), lambda i, be, nu: (layer, be[blk(i, be, nu)], 0, 0))],
        out_specs=pl.BlockSpec((tm, d), lambda i, be, nu: (i, 0)),
        scratch_shapes=[pltpu.VMEM((d, de), BF16), pltpu.VMEM((d, de), BF16), pltpu.VMEM((de, d), BF16)])
    return pl.pallas_call(
        _expert_body, grid_spec=grid_spec, out_shape=jax.ShapeDtypeStruct((cap, d), F32),
        compiler_params=_params("arbitrary"), name="moe_experts",
    )(blk_e, nused, xs, w1, w3, w2)


def _combine_body(dest_ref, x_ref, gate_ref, ys_ref, o_ref, buf_ref, sem_ref, *, tc, ntok):
    i = pl.program_id(0)
    n = pl.num_programs(0)
    slot = i % 2

    def row_copy(step, sl, r, kk):
        return pltpu.make_async_copy(ys_ref.at[pl.ds(dest_ref[kk * ntok + step * tc + r], 1), :],
                                     buf_ref.at[sl, kk, pl.ds(r, 1), :], sem_ref.at[sl])

    def issue_step(step, sl):
        for r in range(tc):
            row_copy(step, sl, r, 0).start(priority=0)
            row_copy(step, sl, r, 1).start(priority=1)

    @pl.when(i == 0)
    def _():
        issue_step(0, 0)

    @pl.when(i + 1 < n)
    def _():
        issue_step(i + 1, 1 - slot)

    for kk in range(2):
        pltpu.make_async_copy(ys_ref.at[pl.ds(0, tc), :], buf_ref.at[slot, kk], sem_ref.at[slot]).wait()

    gate = gate_ref[...]
    o_ref[...] = x_ref[...] + gate[:, 0:1] * buf_ref[slot, 0] + gate[:, 1:2] * buf_ref[slot, 1]


def moe_combine(x2d, gates, ys, dest, tc=512):
    t, d = x2d.shape
    grid_spec = pltpu.PrefetchScalarGridSpec(
        num_scalar_prefetch=1, grid=(t // tc,),
        in_specs=[pl.BlockSpec((tc, d), lambda i, dest: (i, 0)), pl.BlockSpec((tc, 2), lambda i, dest: (i, 0)),
                  pl.BlockSpec(memory_space=pl.ANY)],
        out_specs=pl.BlockSpec((tc, d), lambda i, dest: (i, 0)),
        scratch_shapes=[pltpu.VMEM((2, 2, tc, d), F32), pltpu.SemaphoreType.DMA((2,))])
    return pl.pallas_call(
        functools.partial(_combine_body, tc=tc, ntok=t), grid_spec=grid_spec,
        out_shape=jax.ShapeDtypeStruct((t, d), F32),
        compiler_params=_params("arbitrary", disable_bounds_checks=True), name="moe_combine",
    )(dest, x2d, gates, ys)


def hier_moe_residual(x2d, route, counts_b, g, w1, w3, w2, layer, tm=512):
    t, d = x2d.shape
    eid = route[0:2].astype(I32).reshape(-1)
    rank = route[4:6].astype(I32).reshape(-1)
    gates = route[2:4].T
    counts = counts_b[:, 0].astype(I32)
    padded = (counts + tm - 1) // tm * tm
    pad_end = jnp.cumsum(padded)
    pad_start = pad_end - padded
    experts = jnp.arange(N_EXPERTS, dtype=I32)
    dest = rank + jnp.sum(jnp.where(eid[:, None] == experts[None, :], pad_start[None, :], 0), axis=1)
    cap = t * 2 + N_EXPERTS * tm
    nblk = cap // tm
    blk_row0 = jnp.arange(nblk, dtype=I32) * tm
    blk_e = jnp.minimum(jnp.sum((pad_end[None, :] <= blk_row0[:, None]).astype(I32), axis=1), N_EXPERTS - 1)
    nused = (pad_end[-1:] // tm).astype(I32)
    last_blk = jnp.where(padded > 0, pad_end // tm - 1, -1)
    tail_blk = jnp.where(nused[0] + experts < nblk, nused[0] + experts, -1)
    zero_blk = jnp.concatenate([last_blk, tail_blk]).astype(I32)
    xs = moe_dispatch(x2d, g, dest.astype(I32), zero_blk, cap, tm)
    ys = moe_experts(xs, blk_e, nused, w1, w3, w2, layer, tm)
    return moe_combine(x2d, gates, ys, dest.astype(I32))


def _even_layer(x2d, b, s, layer, mix_g, w_in, q_norm, k_norm, lam_q1, lam_k1, lam_q2, lam_k2, subln,
                hy_conv_w, hy_conv_b, f_w1, f_b1, f_freq, f_w2, f_b2, f_w3, hy_skip, w_out, route_params):
    d = x2d.shape[1]
    d_att = d // 2
    lambda_init = 0.8 - 0.6 * math.exp(-0.3 * layer)
    lam = jnp.exp(jnp.sum(lam_q1 * lam_k1)) - jnp.exp(jnp.sum(lam_q2 * lam_k2)) + lambda_init
    dk = q_norm.shape[0]
    qt, kp, vt, u_hy = even_in_proj(x2d, b, s, mix_g, w_in.astype(BF16), q_norm, k_norm,
                                    dk ** -0.5 * math.log2(math.e))
    y_att = diff_attention(qt, kp, vt, lam, subln, lambda_init)
    z, x1c, zp = hy_prep(u_hy.reshape(b, s, -1), 0, hy_conv_w, hy_conv_b)
    tables = fft_tables(s)
    kf = filter_spectrum(hyena_filter(s, f_w1, f_b1, f_freq, f_w2, f_b2, f_w3), tables[0], tables[1])
    y_hy = hy_fft_conv(zp, z, x1c, kf, tables, hy_skip)
    w_out = w_out.astype(BF16)
    return matmul_residual(x2d, [y_att.reshape(b * s, -1), y_hy.reshape(b * s, -1)],
                           [w_out[:d_att], w_out[d_att:]], route_params)


def _odd_layer(x2d, b, s, mix_g, w_in, conv_w, conv_b, gate_b, out_norm, w_out, route_params):
    d = x2d.shape[1]
    qk_w = conv_w.shape[1]
    main_w = qk_w + 2 * d
    ng = 4 * ML_HEADS
    w_gate = jnp.zeros((d, 128), F32).at[:, :ng].set(w_in[:, main_w:]).astype(BF16)
    u2d, ug = norm_matmul(x2d, mix_g, [w_in[:, :main_w].astype(BF16), w_gate], [BF16, F32])
    u = u2d.reshape(b, s, main_w)
    gcol = (ug[:, :ng] + gate_b).reshape(b, s, ng)
    grow = jnp.swapaxes(gcol, 1, 2)
    dk = qk_w // (2 * ML_HEADS)
    q = ml_prep(u, conv_w, conv_b, 0, qk_w // 2, dk ** -0.5, transpose=False)
    kt = ml_prep(u, conv_w, conv_b, qk_w // 2, qk_w // 2, 1.0, transpose=True)
    hf, hb = mlstm_scan(q, kt, u, qk_w // d, gcol, grow)
    return mlstm_out(x2d, hf.reshape(b * s, d), hb.reshape(b * s, d), u2d, (qk_w + d) // d, out_norm,
                     w_out.astype(BF16), route_params)


def kernel(x, mix_norm, ffn_norm, ev_w_in, ev_q_norm, ev_k_norm, ev_lam_q1, ev_lam_k1, ev_lam_q2, ev_lam_k2, ev_subln, ev_hy_conv_w, ev_hy_conv_b, ev_hy_f_w1, ev_hy_f_b1, ev_hy_f_freq, ev_hy_f_w2, ev_hy_f_b2, ev_hy_f_w3, ev_hy_skip, ev_w_out, od_w_in, od_conv_w, od_conv_b, od_gate_b, od_out_norm, od_w_out, moe_wg, moe_bg, moe_we, moe_be, moe_w1, moe_w3, moe_w2):
    b, s, d = x.shape
    depth = mix_norm.shape[0]
    x2d = x.reshape(b * s, d)
    for layer in range(depth):
        j = layer // 2
        route_params = (ffn_norm[layer], moe_wg[layer], moe_bg[layer], moe_we[layer], moe_be[layer])
        if layer % 2 == 0:
            x2d, route, counts = _even_layer(
                x2d, b, s, layer, mix_norm[layer], ev_w_in[j], ev_q_norm[j], ev_k_norm[j],
                ev_lam_q1[j], ev_lam_k1[j], ev_lam_q2[j], ev_lam_k2[j], ev_subln[j],
                ev_hy_conv_w[j], ev_hy_conv_b[j], ev_hy_f_w1[j], ev_hy_f_b1[j], ev_hy_f_freq[j],
                ev_hy_f_w2[j], ev_hy_f_b2[j], ev_hy_f_w3[j], ev_hy_skip[j], ev_w_out[j], route_params)
        else:
            x2d, route, counts = _odd_layer(x2d, b, s, mix_norm[layer], od_w_in[j], od_conv_w[j], od_conv_b[j],
                                            od_gate_b[j], od_out_norm[j], od_w_out[j], route_params)
        x2d = hier_moe_residual(x2d, route, counts, ffn_norm[layer], moe_w1, moe_w3, moe_w2, layer)
    return x2d.reshape(b, s, d)
```

```python
import functools
import math

import jax
import jax.numpy as jnp
import numpy as np
from jax import lax
from jax.experimental import pallas as pl
from jax.experimental.pallas import tpu as pltpu

F32 = jnp.float32
BF16 = jnp.bfloat16
I32 = jnp.int32

EPS = 1e-6
ROPE_THETA = 500000.0
ATT_HEADS = 4
ML_HEADS = 4
ML_CHUNK = 128
ML_BATCH_PER_STEP = 2
N_GROUPS = 4
EXPERTS_PER_GROUP = 8
N_EXPERTS = N_GROUPS * EXPERTS_PER_GROUP
HY_EMB_BANDS = 16
HY_MIN_DECAY = math.log(1e-2) / 1.5
HY_MAX_DECAY = math.log(1e-2) / 0.3

V7X_VMEM_BYTES = 64 * 1024 * 1024
VMEM_LIMIT = V7X_VMEM_BYTES - 8 * 1024 * 1024
NEG_BIG = -1e30


def _params(*sem, **kw):
    return pltpu.CompilerParams(dimension_semantics=sem, vmem_limit_bytes=VMEM_LIMIT, **kw)


def _sigmoid(x):
    return 1.0 / (1.0 + jnp.exp(-x))


def _norm_matmul_body(x_ref, g_ref, *refs, n_out, col_chunk):
    w_refs, o_refs = refs[:n_out], refs[n_out:]
    x = x_ref[...]
    ms = jnp.mean(x * x, axis=-1, keepdims=True)
    hn = (x * lax.rsqrt(ms + EPS) * g_ref[...]).astype(BF16)
    for w_ref, o_ref in zip(w_refs, o_refs):
        n = w_ref.shape[1]
        for c in range(0, n, col_chunk):
            ce = min(n, c + col_chunk)
            o_ref[:, c:ce] = jnp.dot(hn, w_ref[:, c:ce], preferred_element_type=F32).astype(o_ref.dtype)


def norm_matmul(x2d, g, ws, out_dtypes, tm=1024):
    t, d = x2d.shape
    in_specs = [pl.BlockSpec((tm, d), lambda i: (i, 0)), pl.BlockSpec((1, d), lambda i: (0, 0))]
    in_specs += [pl.BlockSpec(w.shape, lambda i: (0, 0)) for w in ws]
    out_specs = [pl.BlockSpec((tm, w.shape[1]), lambda i: (i, 0)) for w in ws]
    out_shape = [jax.ShapeDtypeStruct((t, w.shape[1]), dt) for w, dt in zip(ws, out_dtypes)]
    return pl.pallas_call(
        functools.partial(_norm_matmul_body, n_out=len(ws), col_chunk=1024),
        grid=(t // tm,), in_specs=in_specs, out_specs=out_specs, out_shape=out_shape,
        compiler_params=_params("parallel"), name="norm_matmul",
    )(x2d, g.reshape(1, d), *ws)


def _matmul_res_body(res_ref, *refs, n_in):
    a_refs, w_refs = refs[:n_in], refs[n_in:2 * n_in]
    g_ref, wt_ref, b_ref, o_ref, route_ref, cnt_ref, run_ref = refs[2 * n_in:]
    acc = res_ref[...]
    for a_ref, w_ref in zip(a_refs, w_refs):
        acc = acc + jnp.dot(a_ref[...], w_ref[...], preferred_element_type=F32)
    o_ref[...] = acc
    _route_epilogue(acc, g_ref, wt_ref, b_ref, route_ref, cnt_ref, run_ref)


def matmul_residual(res, a_list, w_list, route_params, tm=1024):
    t, d = res.shape
    r_in, r_args, r_out, r_shapes, r_scratch = _route_plumbing(t, d, tm, *route_params)
    in_specs = [pl.BlockSpec((tm, d), lambda i: (i, 0))]
    in_specs += [pl.BlockSpec((tm, a.shape[1]), lambda i: (i, 0)) for a in a_list]
    in_specs += [pl.BlockSpec(w.shape, lambda i: (0, 0)) for w in w_list]
    return pl.pallas_call(
        functools.partial(_matmul_res_body, n_in=len(a_list)),
        grid=(t // tm,), in_specs=in_specs + r_in,
        out_specs=[pl.BlockSpec((tm, d), lambda i: (i, 0))] + r_out,
        out_shape=[jax.ShapeDtypeStruct((t, d), F32)] + r_shapes, scratch_shapes=r_scratch,
        compiler_params=_params("arbitrary"), name="matmul_residual",
    )(res, *a_list, *w_list, *r_args)


def _rope_lane_tables(seq, dk, rope_dim, scale):
    half = rope_dim // 2
    f32 = np.float32
    inv_freq = (f32(1.0) / (f32(ROPE_THETA) ** (np.arange(0, rope_dim, 2, dtype=f32) / f32(rope_dim)))).astype(f32)
    ang = np.arange(seq, dtype=f32)[:, None] * inv_freq[None, :]
    cos, sin = np.cos(ang), np.sin(ang)
    d = np.arange(2 * dk) % dk
    fi = d % half
    c_tab = np.where(d[None, :] < rope_dim, cos[:, fi], 1.0)
    s1_tab = np.where(d[None, :] < half, -sin[:, fi], 0.0)
    s2_tab = np.where((d[None, :] >= half) & (d[None, :] < rope_dim), sin[:, fi], 0.0)
    return jnp.asarray((np.stack([c_tab, s1_tab, s2_tab]) * scale).astype(f32))


def _rope_block(x, g, c_tab, s1_tab, s2_tab, dk):
    lane = lax.broadcasted_iota(I32, x.shape, 1)
    lo = lane < dk
    x2 = x * x
    s_lo = jnp.sum(jnp.where(lo, x2, 0.0), axis=-1, keepdims=True)
    s_hi = jnp.sum(jnp.where(lo, 0.0, x2), axis=-1, keepdims=True)
    ms = jnp.where(lo, s_lo, s_hi) * (1.0 / dk)
    y = x * lax.rsqrt(ms + EPS) * g
    return y * c_tab + pltpu.roll(y, 120, 1) * s1_tab + pltpu.roll(y, 8, 1) * s2_tab


def _even_in_body(x_ref, g_ref, w_ref, gq_ref, gk_ref, tq_ref, tk_ref, qt_ref, k_ref, vt_ref, hy_ref, *, heads, dk):
    x = x_ref[...]
    ms = jnp.mean(x * x, axis=-1, keepdims=True)
    hn = (x * lax.rsqrt(ms + EPS) * g_ref[...]).astype(BF16)
    hw = heads * 128
    qk = jnp.dot(hn, w_ref[:, :2 * hw], preferred_element_type=F32)
    for h in range(heads):
        q = _rope_block(qk[:, h * 128:(h + 1) * 128], gq_ref[...], tq_ref[0], tq_ref[1], tq_ref[2], dk)
        qt_ref[h * 128:(h + 1) * 128, :] = q.T.astype(qt_ref.dtype)
        k = _rope_block(qk[:, hw + h * 128:hw + (h + 1) * 128], gk_ref[...], tk_ref[0], tk_ref[1], tk_ref[2], dk)
        k_ref[:, h * 128:(h + 1) * 128] = k.astype(k_ref.dtype)
    v = jnp.dot(hn, w_ref[:, 2 * hw:3 * hw], preferred_element_type=F32)
    for h in range(heads):
        vt_ref[h * 128:(h + 1) * 128, :] = v[:, h * 128:(h + 1) * 128].T.astype(vt_ref.dtype)
    for c in range(0, hy_ref.shape[1], hw):
        hy_ref[:, c:c + hw] = jnp.dot(hn, w_ref[:, 3 * hw + c:3 * hw + c + hw],
                                      preferred_element_type=F32).astype(hy_ref.dtype)


def even_in_proj(x2d, b, s, g, w, q_norm, k_norm, q_scale, tm=512):
    t, d = x2d.shape
    heads = ATT_HEADS
    hw = heads * 128
    dk = q_norm.shape[0]
    assert 2 * dk == 128 and dk // 4 == 16, "rope roll shifts assume 64-wide components, 16 rotary dims"
    nj = s // tm
    n_hy = w.shape[1] - 3 * hw
    tab_q = _rope_lane_tables(s, dk, dk // 4, q_scale)
    tab_k = _rope_lane_tables(s, dk, dk // 4, 1.0)
    row = lambda i: (i, 0)
    const = lambda i: (0, 0)
    tab_spec = pl.BlockSpec((3, tm, 128), lambda i: (0, i % nj, 0))
    return pl.pallas_call(
        functools.partial(_even_in_body, heads=heads, dk=dk), grid=(t // tm,),
        in_specs=[pl.BlockSpec((tm, d), row), pl.BlockSpec((1, d), const), pl.BlockSpec(w.shape, const),
                  pl.BlockSpec((1, 128), const), pl.BlockSpec((1, 128), const), tab_spec, tab_spec],
        out_specs=[pl.BlockSpec((None, hw, tm), lambda i: (i // nj, 0, i % nj)),
                   pl.BlockSpec((None, tm, hw), lambda i: (i // nj, i % nj, 0)),
                   pl.BlockSpec((None, hw, tm), lambda i: (i // nj, 0, i % nj)),
                   pl.BlockSpec((tm, n_hy), row)],
        out_shape=[jax.ShapeDtypeStruct((b, hw, s), BF16), jax.ShapeDtypeStruct((b, s, hw), BF16),
                   jax.ShapeDtypeStruct((b, hw, s), BF16), jax.ShapeDtypeStruct((t, n_hy), BF16)],
        compiler_params=_params("parallel"), name="even_in_proj",
    )(x2d, g.reshape(1, d), w, jnp.tile(q_norm, 2).reshape(1, 128).astype(F32),
      jnp.tile(k_norm, 2).reshape(1, 128).astype(F32), tab_q, tab_k)


def _attn_body(lam_ref, qt_ref, k_ref, vt_ref, g_ref, o_ref, *, tq, dk, post_scale):
    lam = lam_ref[0, 0]
    qt = qt_ref[...]
    row = lax.broadcasted_iota(I32, qt.shape, 0)
    zero = jnp.zeros_like(qt)
    qq = jnp.concatenate([jnp.where(row < dk, qt, zero), jnp.where(row < dk, zero, qt)], axis=1)
    st = jnp.dot(k_ref[...], qq, preferred_element_type=F32)
    m = jnp.max(st, axis=0, keepdims=True)
    p = jnp.exp2(st - m)
    r = 1.0 / jnp.sum(p, axis=0, keepdims=True)
    ot = jnp.dot(vt_ref[...], p.astype(BF16), preferred_element_type=F32)
    o = (ot[:, :tq] * r[:, :tq] - ot[:, tq:] * (lam * r[:, tq:])).T
    ms = jnp.mean(o * o, axis=-1, keepdims=True)
    o_ref[...] = (o * lax.rsqrt(ms + EPS) * g_ref[...] * post_scale).astype(o_ref.dtype)


def diff_attention(qt, k, vt, lam, subln, lambda_init, tq=512):
    b, s, _ = k.shape
    h = ATT_HEADS
    return pl.pallas_call(
        functools.partial(_attn_body, tq=tq, dk=64, post_scale=1.0 - lambda_init),
        grid=(b, h, s // tq),
        in_specs=[pl.BlockSpec(memory_space=pltpu.SMEM),
                  pl.BlockSpec((None, 128, tq), lambda bi, hi, i: (bi, hi, i)),
                  pl.BlockSpec((None, s, 128), lambda bi, hi, i: (bi, 0, hi)),
                  pl.BlockSpec((None, 128, s), lambda bi, hi, i: (bi, hi, 0)),
                  pl.BlockSpec((1, 128), lambda bi, hi, i: (0, 0))],
        out_specs=pl.BlockSpec((None, tq, 128), lambda bi, hi, i: (bi, i, hi)),
        out_shape=jax.ShapeDtypeStruct((b, s, h * 128), BF16),
        compiler_params=_params("parallel", "parallel", "parallel"), name="diff_attention",
    )(lam.reshape(1, 1).astype(F32), qt, k, vt, subln.reshape(1, 128).astype(F32))


def _conv3(u_ref, w_ref, b_ref):
    x = u_ref[...].astype(F32)
    s = x.shape[0]
    row = lax.broadcasted_iota(I32, x.shape, 0)
    x_prev = jnp.where(row == 0, 0.0, pltpu.roll(x, 1, 0))
    x_next = jnp.where(row == s - 1, 0.0, pltpu.roll(x, s - 1, 0))
    w = w_ref[...]
    return b_ref[...] + x_prev * w[0:1] + x * w[1:2] + x_next * w[2:3]


FFT_N1 = 64
FFT_UNROLL = 128
FFT_PAD = 8


def _hy_prep_body(x1_ref, x2_ref, v_ref, w1_ref, w2_ref, wv_ref, b1_ref, b2_ref, bv_ref, z_ref, x1c_ref, zp_ref):
    x1c_ref[...] = _conv3(x1_ref, w1_ref, b1_ref).astype(x1c_ref.dtype)
    z = _conv3(v_ref, wv_ref, bv_ref) * _conv3(x2_ref, w2_ref, b2_ref)
    z_ref[...] = z.astype(z_ref.dtype)
    nb = z.shape[0] // FFT_N1
    zp_ref[...] = jnp.zeros_like(zp_ref)
    for n2 in range(nb):
        for ci in range(z.shape[1] // 128):
            zp_ref[ci, pl.ds(n2, FFT_N1, stride=nb + FFT_PAD), :] = (
                z[n2 * FFT_N1:(n2 + 1) * FFT_N1, ci * 128:(ci + 1) * 128])


def hy_prep(u, col0, conv_w, conv_b, tc=256):
    b, s, _ = u.shape
    d_hy = conv_w.shape[1] // 3
    nct = d_hy // tc
    blk0 = col0 // tc
    sp = FFT_N1 * (s // FFT_N1 + FFT_PAD)

    def uspec(part):
        return pl.BlockSpec((None, s, tc), lambda bi, c: (bi, 0, blk0 + part * nct + c))

    def wspec(part, rows):
        return pl.BlockSpec((rows, tc), lambda bi, c: (0, part * nct + c))

    ospec = pl.BlockSpec((None, s, tc), lambda bi, c: (bi, 0, c))
    return pl.pallas_call(
        _hy_prep_body, grid=(b, nct),
        in_specs=[uspec(0), uspec(1), uspec(2), wspec(0, 3), wspec(1, 3), wspec(2, 3),
                  wspec(0, 1), wspec(1, 1), wspec(2, 1)],
        out_specs=[ospec, ospec, pl.BlockSpec((None, tc // 128, sp, 128), lambda bi, c: (bi, c, 0, 0))],
        out_shape=[jax.ShapeDtypeStruct((b, s, d_hy), BF16), jax.ShapeDtypeStruct((b, s, d_hy), BF16),
                   jax.ShapeDtypeStruct((b, d_hy // 128, sp, 128), F32)],
        compiler_params=_params("parallel", "parallel"), name="hy_prep",
    )(u, u, u, conv_w, conv_w, conv_w, conv_b.reshape(1, -1), conv_b.reshape(1, -1), conv_b.reshape(1, -1))


def _hy_filter_body(z_ref, w1_ref, b1_ref, fr_ref, w2_ref, b2_ref, w3f_ref, w3b_ref, dec_ref, hf_ref, hb_ref):
    hp = lax.Precision.HIGHEST
    fr = fr_ref[...]
    hdn = jnp.sin(fr * (jnp.dot(z_ref[...], w1_ref[...], precision=hp, preferred_element_type=F32) + b1_ref[...]))
    hdn = jnp.sin(fr * (jnp.dot(hdn, w2_ref[...], precision=hp, preferred_element_type=F32) + b2_ref[...]))
    dec = dec_ref[...]
    h_fwd = jnp.dot(hdn, w3f_ref[...], precision=hp, preferred_element_type=F32) * dec
    h_bwd = jnp.dot(hdn, w3b_ref[...], precision=hp, preferred_element_type=F32) * dec
    row0 = lax.broadcasted_iota(I32, h_fwd.shape, 0) == 0
    h_fwd = h_fwd + jnp.where(row0, h_bwd, 0.0)
    h_bwd = jnp.where(row0, 0.0, h_bwd)
    norm = (jnp.sum(jnp.abs(h_fwd), axis=0, keepdims=True) + jnp.sum(jnp.abs(h_bwd), axis=0, keepdims=True) + EPS)
    hf_ref[...] = h_fwd / norm
    hb_ref[...] = h_bwd / norm


def hyena_filter(length, w1, b1, freq, w2, b2, w3, tc=256):
    d_hy = w3.shape[1] // 2
    ff = w1.shape[1]
    f32 = np.float32
    t = np.linspace(0.0, 1.0, length, dtype=f32)[:, None]
    bands = np.linspace(1e-4, HY_EMB_BANDS - 1, HY_EMB_BANDS, dtype=f32)[None, :]
    ang = (f32(2.0 * math.pi / length) * np.arange(length, dtype=f32)[:, None] * bands).astype(f32)
    feat = np.concatenate([t, np.cos(ang), -np.sin(ang)], axis=-1).astype(f32)
    emb = feat.shape[1]
    z = jnp.asarray(np.pad(feat, ((0, 0), (0, 128 - emb))))
    w1p = jnp.zeros((128, ff), F32).at[:emb].set(w1)
    deltas = np.abs(np.linspace(HY_MIN_DECAY, HY_MAX_DECAY, d_hy, dtype=f32))
    decay = jnp.asarray(np.exp(-t * deltas[None, :]).astype(f32))
    const = lambda c: (0, 0)
    col = lambda c: (0, c)
    out = jax.ShapeDtypeStruct((length, d_hy), F32)
    h_fwd, h_bwd = pl.pallas_call(
        _hy_filter_body, grid=(d_hy // tc,),
        in_specs=[pl.BlockSpec((length, 128), const), pl.BlockSpec((128, ff), const), pl.BlockSpec((1, ff), const),
                  pl.BlockSpec((1, ff), const), pl.BlockSpec((ff, ff), const), pl.BlockSpec((1, ff), const),
                  pl.BlockSpec((ff, tc), col), pl.BlockSpec((ff, tc), lambda c: (0, d_hy // tc + c)),
                  pl.BlockSpec((length, tc), col)],
        out_specs=[pl.BlockSpec((length, tc), col), pl.BlockSpec((length, tc), col)],
        out_shape=[out, out], compiler_params=_params("parallel"), name="hyena_filter",
    )(z, w1p, b1.reshape(1, ff), freq.reshape(1, ff), w2, b2.reshape(1, ff), w3, w3, decay)
    return jnp.concatenate([h_fwd, h_bwd], axis=1)


def fft_tables(length):
    n = 2 * length
    n1c, n2c, nb = FFT_N1, n // FFT_N1, length // FFT_N1
    unit = 2.0 * math.pi / n
    i1 = np.arange(n1c, dtype=np.int64)
    i2 = np.arange(n2c, dtype=np.int64)
    ib = np.arange(nb, dtype=np.int64)
    samp = i1[:, None, None] + n1c * ib[None, None, :]
    ang = ((i2[None, :, None] * samp) % n) * unit
    m1 = np.concatenate([np.cos(ang), -np.sin(ang)], axis=1)
    ang = ((i1[:, None] * i1[None, :]) % n1c) * (2.0 * math.pi / n1c)
    c, s = np.cos(ang), np.sin(ang)
    f1 = np.concatenate([np.concatenate([c, s], axis=1), np.concatenate([-s, c], axis=1)], axis=0)
    freq = n2c * i1[None, None, :] + i2[:, None, None]
    ang = ((i1[None, :, None] * freq) % n) * unit
    c, s = np.cos(ang), np.sin(ang)
    g1 = np.concatenate([np.concatenate([c, -s], axis=2), np.concatenate([s, c], axis=2)], axis=1)
    ang = ((ib[:, None] * i2[None, :]) % n2c) * (2.0 * math.pi / n2c)
    g2 = np.concatenate([np.cos(ang), -np.sin(ang)], axis=1) * (1.0 / n)
    return tuple(jnp.asarray(t.astype(np.float32).astype(BF16)) for t in (m1, f1, g1, g2))


def _fft_stage1(xp_ref, m1_ref, p_ref):
    nb = m1_ref.shape[2]
    n2c = m1_ref.shape[1] // 2

    def body(n1, c):
        x = xp_ref[pl.ds(pl.multiple_of(n1 * (nb + FFT_PAD), 8), nb), :].astype(BF16)
        a = jnp.dot(m1_ref[n1], x, preferred_element_type=F32)
        p_ref[0, pl.ds(n1, n2c, stride=FFT_N1 + FFT_PAD), :] = a[:n2c]
        p_ref[1, pl.ds(n1, n2c, stride=FFT_N1 + FFT_PAD), :] = a[n2c:]
        return c
    lax.fori_loop(0, FFT_N1, body, 0, unroll=FFT_UNROLL)


def _fft_stage2(p_ref, f1_ref, k2):
    r0 = pl.multiple_of(k2 * (FFT_N1 + FFT_PAD), 8)
    slab = jnp.concatenate([p_ref[0, pl.ds(r0, FFT_N1), :], p_ref[1, pl.ds(r0, FFT_N1), :]], axis=0)
    return jnp.dot(f1_ref[...], slab.astype(BF16), preferred_element_type=F32)


def _spectrum_body(xp_ref, m1_ref, f1_ref, o_ref, p_ref):
    _fft_stage1(xp_ref, m1_ref, p_ref)

    def body(k2, c):
        o_ref[k2] = _fft_stage2(p_ref, f1_ref, k2)
        return c
    lax.fori_loop(0, o_ref.shape[0], body, 0, unroll=FFT_UNROLL)


def filter_spectrum(ab, m1, f1):
    length, c2 = ab.shape
    nb = length // FFT_N1
    n2c = m1.shape[1] // 2
    nch = c2 // 128
    abp = jnp.pad(ab.reshape(nb, FFT_N1, nch, 128).transpose(2, 1, 0, 3), ((0, 0), (0, 0), (0, FFT_PAD), (0, 0)))
    abp = abp.reshape(nch, FFT_N1 * (nb + FFT_PAD), 128)
    spec = pl.pallas_call(
        _spectrum_body, grid=(nch,),
        in_specs=[pl.BlockSpec((None,) + abp.shape[1:], lambda c: (c, 0, 0)),
                  pl.BlockSpec(m1.shape, lambda c: (0, 0, 0)), pl.BlockSpec(f1.shape, lambda c: (0, 0))],
        out_specs=pl.BlockSpec((None, n2c, 2 * FFT_N1, 128), lambda c: (c, 0, 0, 0)),
        out_shape=jax.ShapeDtypeStruct((nch, n2c, 2 * FFT_N1, 128), F32),
        scratch_shapes=[pltpu.VMEM((2, n2c * (FFT_N1 + FFT_PAD), 128), F32)],
        compiler_params=_params("parallel"), name="filter_spectrum",
    )(abp, m1, f1)
    fa, fb = spec[:nch // 2], spec[nch // 2:]
    h = FFT_N1
    return jnp.concatenate([fa[:, :, :h] + fb[:, :, :h], fa[:, :, h:] - fb[:, :, h:]], axis=2).astype(BF16)


def _hy_fft_body(zp_ref, z_ref, x1c_ref, kf_ref, m1_ref, f1_ref, g1_ref, g2_ref, skip_ref, o_ref,
                 p_ref, q_ref, y_ref):
    h = FFT_N1
    n2c = g1_ref.shape[0]
    nb = g2_ref.shape[0]
    _fft_stage1(zp_ref, m1_ref, p_ref)

    def mid(k2, c):
        xf = _fft_stage2(p_ref, f1_ref, k2)
        kf = kf_ref[k2].astype(F32)
        xr, xi, kr, ki = xf[:h], xf[h:], kf[:h], kf[h:]
        y = jnp.concatenate([xr * kr - xi * ki, xr * ki + xi * kr], axis=0).astype(BF16)
        d = jnp.dot(g1_ref[k2], y, preferred_element_type=F32)
        q_ref[0, pl.ds(k2, h, stride=n2c + FFT_PAD), :] = d[:h]
        q_ref[1, pl.ds(k2, h, stride=n2c + FFT_PAD), :] = d[h:]
        return c
    lax.fori_loop(0, n2c, mid, 0, unroll=FFT_UNROLL)

    def last(t1, c):
        r0 = pl.multiple_of(t1 * (n2c + FFT_PAD), 8)
        slab = jnp.concatenate([q_ref[0, pl.ds(r0, n2c), :], q_ref[1, pl.ds(r0, n2c), :]], axis=0)
        y_ref[pl.ds(t1, nb, stride=h), :] = jnp.dot(g2_ref[...], slab.astype(BF16), preferred_element_type=F32)
        return c
    lax.fori_loop(0, h, last, 0, unroll=FFT_UNROLL)

    z = z_ref[...].astype(F32)
    o_ref[...] = ((y_ref[...] + z * skip_ref[...]) * x1c_ref[...].astype(F32)).astype(o_ref.dtype)


def hy_fft_conv(zp, z, x1c, kf, tables, skip):
    m1, f1, g1, g2 = tables
    b, s, c = z.shape
    nch = c // 128

    def const(shape):
        return pl.BlockSpec(shape, lambda ci, bi: (0,) * len(shape))

    nat = pl.BlockSpec((None, s, 128), lambda ci, bi: (bi, 0, ci))
    return pl.pallas_call(
        _hy_fft_body, grid=(nch, b),
        in_specs=[pl.BlockSpec((None, None) + zp.shape[2:], lambda ci, bi: (bi, ci, 0, 0)), nat, nat,
                  pl.BlockSpec((None,) + kf.shape[1:], lambda ci, bi: (ci, 0, 0, 0)),
                  const(m1.shape), const(f1.shape), const(g1.shape), const(g2.shape),
                  pl.BlockSpec((1, 128), lambda ci, bi: (0, ci))],
        out_specs=nat,
        out_shape=jax.ShapeDtypeStruct((b, s, c), BF16),
        scratch_shapes=[pltpu.VMEM((2, g1.shape[0] * (FFT_N1 + FFT_PAD), 128), F32),
                        pltpu.VMEM((2, FFT_N1 * (g1.shape[0] + FFT_PAD), 128), F32),
                        pltpu.VMEM((s, 128), F32)],
        compiler_params=_params("parallel", "parallel"), name="hy_fft_conv",
    )(zp, z, x1c, kf, m1, f1, g1, g2, skip.reshape(1, c).astype(F32))


def _ml_prep_body(u_ref, w_ref, b_ref, o_ref, *, scale, transpose):
    y = _conv3(u_ref, w_ref, b_ref)
    y = y * _sigmoid(y) * scale
    o_ref[...] = (y.T if transpose else y).astype(o_ref.dtype)


def ml_prep(u, conv_w, conv_b, col0, ncols, scale, transpose, tc=256):
    b, s, _ = u.shape
    w = conv_w.shape[1]
    c0 = col0 // tc
    if transpose:
        out_spec = pl.BlockSpec((None, tc, s), lambda bi, c: (bi, c, 0))
        out_shape = jax.ShapeDtypeStruct((b, ncols, s), BF16)
    else:
        out_spec = pl.BlockSpec((None, s, tc), lambda bi, c: (bi, 0, c))
        out_shape = jax.ShapeDtypeStruct((b, s, ncols), BF16)
    return pl.pallas_call(
        functools.partial(_ml_prep_body, scale=scale, transpose=transpose), grid=(b, ncols // tc),
        in_specs=[pl.BlockSpec((None, s, tc), lambda bi, c: (bi, 0, c0 + c)),
                  pl.BlockSpec((3, tc), lambda bi, c: (0, c0 + c)),
                  pl.BlockSpec((1, tc), lambda bi, c: (0, c0 + c))],
        out_specs=out_spec, out_shape=out_shape,
        compiler_params=_params("parallel", "parallel"), name="ml_prep",
    )(u, conv_w, conv_b.reshape(1, w))


def _log_sigmoid(x):
    return jnp.minimum(x, 0.0) - jnp.log(1.0 + jnp.exp(-jnp.abs(x)))


def _dot_split(a, b, a_is_f32):
    x = a if a_is_f32 else b
    hi = x.astype(BF16)
    lo = (x - hi.astype(F32)).astype(BF16)
    if a_is_f32:
        return (jnp.dot(hi, b, preferred_element_type=F32) + jnp.dot(lo, b, preferred_element_type=F32))
    return (jnp.dot(a, hi, preferred_element_type=F32) + jnp.dot(a, lo, preferred_element_type=F32))


def _mlstm_chain(q, kt, v, bc, br, li_r, total, mask, c_ref, m_ref, idx):
    dv = v.shape[1] - 128
    c_st = c_ref[idx]
    m_st = m_ref[idx:idx + 1, 0:1]
    key_row = li_r - br
    inter = bc + m_st
    m_t = jnp.maximum(inter, bc + jnp.max(jnp.where(mask, key_row, NEG_BIG), axis=-1, keepdims=True))
    w_intra = jnp.exp(jnp.where(mask, (bc - m_t) + key_row, NEG_BIG))
    w_inter = jnp.exp(inter - m_t)
    sc = jnp.dot(q, kt, preferred_element_type=F32) * w_intra
    q_inter = (q.astype(F32) * w_inter).astype(BF16)
    both = jnp.dot(jnp.concatenate([q_inter, sc.astype(BF16)], axis=1),
                   jnp.concatenate([c_st.astype(BF16), v], axis=0),
                   preferred_element_type=F32)
    den = both[:, dv:dv + 1]
    h = both[:, :dv] / jnp.maximum(jnp.abs(den), jnp.exp(-m_t))
    g_row = total + key_row
    m_next = jnp.maximum(total + m_st, jnp.max(g_row, axis=-1, keepdims=True))
    a_prev = jnp.exp(total + m_st - m_next)
    kwt = (kt.astype(F32) * jnp.exp(g_row - m_next)).astype(BF16)
    c_ref[idx] = a_prev * c_st + jnp.dot(kwt, v, preferred_element_type=F32)
    m_ref[idx:idx + 1, :] = jnp.broadcast_to(m_next, (1, m_ref.shape[1]))
    return h


def _mlstm_body(qf_ref, ktf_ref, vf_ref, gcf_ref, grf_ref, qb_ref, ktb_ref, vb_ref, gcb_ref, grb_ref,
                hf_ref, hb_ref, c_ref, m_ref, *, heads, dk, dv):
    @pl.when(pl.program_id(1) == 0)
    def _():
        c_ref[...] = jnp.zeros_like(c_ref)
        m_ref[...] = jnp.zeros_like(m_ref)

    lc = qf_ref.shape[1]
    ones_blk = jnp.where(lax.broadcasted_iota(I32, (lc, 128), 1) == 0, 1.0, 0.0).astype(BF16)
    t_i = lax.broadcasted_iota(I32, (lc, lc), 0)
    s_i = lax.broadcasted_iota(I32, (lc, lc), 1)
    lower = s_i <= t_i
    upper = s_i >= t_i
    ltri = jnp.where(lower, 1.0, 0.0).astype(BF16)
    utri = jnp.where(upper, 1.0, 0.0).astype(BF16)

    for bb in range(qf_ref.shape[0]):
        for direction, (q_ref, kt_ref, v_ref, gc_ref, gr_ref, h_ref) in enumerate(
                ((qf_ref, ktf_ref, vf_ref, gcf_ref, grf_ref, hf_ref),
                 (qb_ref, ktb_ref, vb_ref, gcb_ref, grb_ref, hb_ref))):
            fwd = direction == 0
            gc = gc_ref[bb]
            gr = gr_ref[bb]
            lfc, lfr = _log_sigmoid(gc), _log_sigmoid(gr)
            cum_c = _dot_split(ltri if fwd else utri, lfc, a_is_f32=False)
            cum_r = _dot_split(lfr, utri if fwd else ltri, a_is_f32=True)
            for hd in range(heads):
                gi = (0 if fwd else 2) * heads + hd
                gf = (1 if fwd else 3) * heads + hd
                bc, br = cum_c[:, gf:gf + 1], cum_r[gf:gf + 1, :]
                total = br[:, lc - 1:lc] if fwd else br[:, 0:1]
                q = q_ref[bb, :, hd * dk:(hd + 1) * dk]
                kt = kt_ref[bb, hd * dk:(hd + 1) * dk, :]
                v = jnp.concatenate([v_ref[bb, :, hd * dv:(hd + 1) * dv], ones_blk], axis=1)
                h = _mlstm_chain(q, kt, v, bc, br, gr[gi:gi + 1, :], total,
                                 lower if fwd else upper, c_ref, m_ref, (bb * 2 + direction) * heads + hd)
                h_ref[bb, :, hd * dv:(hd + 1) * dv] = h.astype(h_ref.dtype)


def mlstm_scan(q, kt, u, v_blk, gcol, grow):
    b, s, w = q.shape
    heads = ML_HEADS
    dk = w // heads
    dv = 2 * dk
    lc = ML_CHUNK
    nc = s // lc
    ng = gcol.shape[-1]

    def fw(bi, j):
        return j

    def bw(bi, j):
        return nc - 1 - j

    nbs = ML_BATCH_PER_STEP if b % ML_BATCH_PER_STEP == 0 else 1

    def specs(pos):
        return [pl.BlockSpec((nbs, lc, w), lambda bi, j: (bi, pos(bi, j), 0)),
                pl.BlockSpec((nbs, w, lc), lambda bi, j: (bi, 0, pos(bi, j))),
                pl.BlockSpec((nbs, lc, heads * dv), lambda bi, j: (bi, pos(bi, j), v_blk)),
                pl.BlockSpec((nbs, lc, ng), lambda bi, j: (bi, pos(bi, j), 0)),
                pl.BlockSpec((nbs, ng, lc), lambda bi, j: (bi, 0, pos(bi, j)))]

    hshape = jax.ShapeDtypeStruct((b, s, heads * dv), BF16)
    return pl.pallas_call(
        functools.partial(_mlstm_body, heads=heads, dk=dk, dv=dv), grid=(b // nbs, nc),
        in_specs=specs(fw) + specs(bw),
        out_specs=[pl.BlockSpec((nbs, lc, heads * dv), lambda bi, j: (bi, j, 0)),
                   pl.BlockSpec((nbs, lc, heads * dv), lambda bi, j: (bi, nc - 1 - j, 0))],
        out_shape=[hshape, hshape],
        scratch_shapes=[pltpu.VMEM((nbs * 2 * heads, dk, dv + 128), F32),
                        pltpu.VMEM((nbs * 2 * heads, 128), F32)],
        compiler_params=_params("parallel", "arbitrary"), name="mlstm_scan",
    )(q, kt, u, gcol, grow, q, kt, u, gcol, grow)


def _mlstm_out_body(res_ref, hf_ref, hb_ref, o_ref, g_ref, w_ref, gf_ref, wt_ref, rb_ref,
                    out_ref, route_ref, cnt_ref, run_ref, *, heads):
    hs = hf_ref[...].astype(F32) + hb_ref[...].astype(F32)
    dv = hs.shape[1] // heads
    g = g_ref[...]
    parts = []
    for hd in range(heads):
        seg = hs[:, hd * dv:(hd + 1) * dv]
        ms = jnp.mean(seg * seg, axis=-1, keepdims=True)
        parts.append(seg * lax.rsqrt(ms + EPS) * g[:, hd * dv:(hd + 1) * dv])
    a = jnp.concatenate(parts, axis=-1) * _sigmoid(o_ref[...].astype(F32))
    x_new = res_ref[...] + jnp.dot(a.astype(BF16), w_ref[...], preferred_element_type=F32)
    out_ref[...] = x_new
    _route_epilogue(x_new, gf_ref, wt_ref, rb_ref, route_ref, cnt_ref, run_ref)


def mlstm_out(res, hf, hb, u2d, o_blk, gain, w_out, route_params, tm=1024):
    t, d = res.shape
    row = lambda i: (i, 0)
    r_in, r_args, r_out, r_shapes, r_scratch = _route_plumbing(t, d, tm, *route_params)
    return pl.pallas_call(
        functools.partial(_mlstm_out_body, heads=ML_HEADS), grid=(t // tm,),
        in_specs=[pl.BlockSpec((tm, d), row), pl.BlockSpec((tm, d), row), pl.BlockSpec((tm, d), row),
                  pl.BlockSpec((tm, d), lambda i: (i, o_blk)), pl.BlockSpec((1, d), lambda i: (0, 0)),
                  pl.BlockSpec((d, d), lambda i: (0, 0))] + r_in,
        out_specs=[pl.BlockSpec((tm, d), row)] + r_out,
        out_shape=[jax.ShapeDtypeStruct((t, d), F32)] + r_shapes, scratch_shapes=r_scratch,
        compiler_params=_params("arbitrary"), name="mlstm_out",
    )(res, hf, hb, u2d, gain.reshape(1, d), w_out, *r_args)


ROUTE_TILE = 512
ROUTE_ROWS = 128
EXPERT_ROW0 = 8


def _route_tile(x, g, wt, bias, run_ref):
    ms = jnp.mean(x * x, axis=-1, keepdims=True)
    xn = x * lax.rsqrt(ms + EPS) * g
    w_hi, x_hi = wt.astype(BF16), xn.astype(BF16)
    w_lo, x_lo = (wt - w_hi.astype(F32)).astype(BF16), (xn - x_hi.astype(F32)).astype(BF16)
    nt = (((1,), (1,)), ((), ()))
    logit = (lax.dot_general(w_hi, x_hi, nt, preferred_element_type=F32)
             + lax.dot_general(w_lo, x_hi, nt, preferred_element_type=F32)
             + lax.dot_general(w_hi, x_lo, nt, preferred_element_type=F32)) + bias
    rows = [logit[r:r + 1, :] for r in range(EXPERT_ROW0 + N_EXPERTS)]
    g_best, g_idx = rows[0], jnp.zeros_like(rows[0])
    for gi in range(1, N_GROUPS):
        better = rows[gi] > g_best
        g_best = jnp.where(better, rows[gi], g_best)
        g_idx = jnp.where(better, float(gi), g_idx)
    g_den = sum(jnp.exp(rows[gi] - g_best) for gi in range(N_GROUPS))
    g_w = 1.0 / g_den
    sel = []
    for e in range(EXPERTS_PER_GROUP):
        v = rows[EXPERT_ROW0 + e]
        for gi in range(1, N_GROUPS):
            v = jnp.where(g_idx == float(gi), rows[EXPERT_ROW0 + gi * EXPERTS_PER_GROUP + e], v)
        sel.append(v)
    v1, i1 = sel[0], jnp.zeros_like(sel[0])
    for e in range(1, EXPERTS_PER_GROUP):
        better = sel[e] > v1
        v1 = jnp.where(better, sel[e], v1)
        i1 = jnp.where(better, float(e), i1)
    v2, i2 = jnp.full_like(v1, -jnp.inf), jnp.zeros_like(v1)
    for e in range(EXPERTS_PER_GROUP):
        better = (sel[e] > v2) & (i1 != float(e))
        v2 = jnp.where(better, sel[e], v2)
        i2 = jnp.where(better, float(e), i2)
    e21 = jnp.exp(v2 - v1)
    gate1 = g_w / (1.0 + e21)
    gate2 = gate1 * e21
    base = g_idx * float(EXPERTS_PER_GROUP)
    e1, e2 = base + i1, base + i2
    tm = e1.shape[1]
    erow = lax.broadcasted_iota(I32, (N_EXPERTS, tm), 0).astype(F32)
    oh1 = jnp.where(erow == e1, 1.0, 0.0)
    oh2 = jnp.where(erow == e2, 1.0, 0.0)
    cnt = oh1 + oh2
    earlier = jnp.where(lax.broadcasted_iota(I32, (tm, tm), 0) < lax.broadcasted_iota(I32, (tm, tm), 1),
                        1.0, 0.0).astype(BF16)
    run = run_ref[...]
    pos = run[:, 0:1] + jnp.dot(cnt.astype(BF16), earlier, preferred_element_type=F32)
    rank1 = jnp.sum(oh1 * pos, axis=0, keepdims=True)
    rank2 = jnp.sum(oh2 * pos, axis=0, keepdims=True)
    run = run + jnp.sum(cnt, axis=1, keepdims=True)
    run_ref[...] = run
    zero = jnp.zeros_like(v1)
    return jnp.concatenate([e1, e2, gate1, gate2, rank1, rank2, zero, zero], axis=0)


def _route_epilogue(x_new, g_ref, wt_ref, b_ref, route_ref, cnt_ref, run_ref):
    @pl.when(pl.program_id(0) == 0)
    def _():
        run_ref[...] = jnp.zeros_like(run_ref)

    for r0 in range(0, x_new.shape[0], ROUTE_TILE):
        route_ref[:, r0:r0 + ROUTE_TILE] = _route_tile(x_new[r0:r0 + ROUTE_TILE], g_ref[...], wt_ref[...],
                                                       b_ref[...], run_ref)
    cnt_ref[...] = run_ref[...]


def _route_plumbing(t, d, tm, g, wg, bg, we, be):
    wt = jnp.zeros((ROUTE_ROWS, d), F32).at[:N_GROUPS].set(wg.T).at[EXPERT_ROW0:EXPERT_ROW0 + N_EXPERTS].set(we.T)
    bias = jnp.zeros((ROUTE_ROWS, 1), F32).at[:N_GROUPS, 0].set(bg).at[EXPERT_ROW0:EXPERT_ROW0 + N_EXPERTS, 0].set(be)
    in_specs = [pl.BlockSpec((1, d), lambda i: (0, 0)), pl.BlockSpec((ROUTE_ROWS, d), lambda i: (0, 0)),
                pl.BlockSpec((ROUTE_ROWS, 1), lambda i: (0, 0))]
    out_specs = [pl.BlockSpec((8, tm), lambda i: (0, i)), pl.BlockSpec((N_EXPERTS, 128), lambda i: (0, 0))]
    out_shapes = [jax.ShapeDtypeStruct((8, t), F32), jax.ShapeDtypeStruct((N_EXPERTS, 128), F32)]
    return in_specs, [g.reshape(1, d), wt, bias], out_specs, out_shapes, [pltpu.VMEM((N_EXPERTS, 128), F32)]


def _dispatch_body(dest_ref, zblk_ref, x_ref, g_ref, xs_ref, buf_ref, sem_ref, *, td, ntok):
    i = pl.program_id(0)
    n = pl.num_programs(0)
    slot = i % 2

    @pl.when(i == 0)
    def _():
        buf_ref[1] = jnp.zeros(buf_ref.shape[1:], buf_ref.dtype)

        def zero_copy(j):
            return pltpu.make_async_copy(buf_ref.at[1], xs_ref.at[pl.ds(zblk_ref[j] * td, td), :], sem_ref.at[1])

        def start(j, c):
            @pl.when(zblk_ref[j] >= 0)
            def _():
                zero_copy(j).start()
            return c

        def wait(j, c):
            @pl.when(zblk_ref[j] >= 0)
            def _():
                zero_copy(j).wait()
            return c
        lax.fori_loop(0, zblk_ref.shape[0], start, 0)
        lax.fori_loop(0, zblk_ref.shape[0], wait, 0)

    def row_copy(r, kk):
        return pltpu.make_async_copy(buf_ref.at[slot, pl.ds(r, 1), :],
                                     xs_ref.at[pl.ds(dest_ref[kk * ntok + i * td + r], 1), :],
                                     sem_ref.at[slot])

    def wait_buffer(sl):
        for _ in range(2):
            pltpu.make_async_copy(buf_ref.at[sl], xs_ref.at[pl.ds(0, td), :], sem_ref.at[sl]).wait()

    @pl.when(i >= 2)
    def _():
        wait_buffer(slot)

    x = x_ref[...]
    ms = jnp.mean(x * x, axis=-1, keepdims=True)
    buf_ref[slot] = x * lax.rsqrt(ms + EPS) * g_ref[...]

    for r in range(td):
        row_copy(r, 0).start(priority=0)
        row_copy(r, 1).start(priority=1)

    @pl.when(i == n - 1)
    def _():
        @pl.when(n >= 2)
        def _():
            wait_buffer(1 - slot)
        wait_buffer(slot)


def moe_dispatch(x2d, g, dest, zero_blk, cap, td):
    t, d = x2d.shape
    grid_spec = pltpu.PrefetchScalarGridSpec(
        num_scalar_prefetch=2, grid=(t // td,),
        in_specs=[pl.BlockSpec((td, d), lambda i, dest, zb: (i, 0)),
                  pl.BlockSpec((1, d), lambda i, dest, zb: (0, 0))],
        out_specs=pl.BlockSpec(memory_space=pl.ANY),
        scratch_shapes=[pltpu.VMEM((2, td, d), F32), pltpu.SemaphoreType.DMA((2,))])
    return pl.pallas_call(
        functools.partial(_dispatch_body, td=td, ntok=t), grid_spec=grid_spec,
        out_shape=jax.ShapeDtypeStruct((cap, d), F32),
        compiler_params=_params("arbitrary", disable_bounds_checks=True),
        name="moe_dispatch",
    )(dest, zero_blk, x2d, g.reshape(1, d))


def _expert_body(blk_e_ref, nused_ref, xs_ref, w1_ref, w3_ref, w2_ref, ys_ref, w1b_ref, w3b_ref, w2b_ref):
    i = pl.program_id(0)
    nused = nused_ref[0]
    last = nused - 1
    cur = blk_e_ref[jnp.minimum(i, last)]
    prev = blk_e_ref[jnp.minimum(jnp.maximum(i - 1, 0), last)]

    @pl.when((i == 0) | (cur != prev))
    def _():
        w1b_ref[...] = w1_ref[...].astype(BF16)
        w3b_ref[...] = w3_ref[...].astype(BF16)
        w2b_ref[...] = w2_ref[...].astype(BF16)

    @pl.when(i < nused)
    def _():
        x = xs_ref[...].astype(BF16)
        hw = w1b_ref.shape[1] // 2
        parts = []
        for c0 in (0, hw):
            h1 = jnp.dot(x, w1b_ref[:, c0:c0 + hw], preferred_element_type=F32)
            h3 = jnp.dot(x, w3b_ref[:, c0:c0 + hw], preferred_element_type=F32)
            parts.append((h1 * _sigmoid(h1) * h3).astype(BF16))
        ys_ref[...] = jnp.dot(jnp.concatenate(parts, axis=1), w2b_ref[...], preferred_element_type=F32)

    @pl.when(i >= nused)
    def _():
        ys_ref[...] = jnp.zeros_like(ys_ref)


def moe_experts(xs, blk_e, nused, w1, w3, w2, layer, tm):
    cap, d = xs.shape
    de = w1.shape[-1]

    def blk(i, be, nu):
        return jnp.minimum(i, nu[0] - 1)

    grid_spec = pltpu.PrefetchScalarGridSpec(
        num_scalar_prefetch=2, grid=(cap // tm,),
        in_specs=[pl.BlockSpec((tm, d), lambda i, be, nu: (blk(i, be, nu), 0)),
                  pl.BlockSpec((None, None, d, de), lambda i, be, nu: (layer, be[blk(i, be, nu)], 0, 0)),
                  pl.BlockSpec((None, None, d, de), lambda i, be, nu: (layer, be[blk(i, be, nu)], 0, 0)),
                  pl.BlockSpec((None, None, de, d), lambda i, be, nu: (layer, be[blk(i, be, nu)], 0, 0))],
        out_specs=pl.BlockSpec((tm, d), lambda i, be, nu: (i, 0)),
        scratch_shapes=[pltpu.VMEM((d, de), BF16), pltpu.VMEM((d, de), BF16), pltpu.VMEM((de, d), BF16)])
    return pl.pallas_call(
        _expert_body, grid_spec=grid_spec, out_shape=jax.ShapeDtypeStruct((cap, d), F32),
        compiler_params=_params("arbitrary"), name="moe_experts",
    )(blk_e, nused, xs, w1, w3, w2)


def _combine_body(dest_ref, x_ref, gate_ref, ys_ref, o_ref, buf_ref, sem_ref, *, tc, ntok):
    i = pl.program_id(0)
    n = pl.num_programs(0)
    slot = i % 2

    def row_copy(step, sl, r, kk):
        return pltpu.make_async_copy(ys_ref.at[pl.ds(dest_ref[kk * ntok + step * tc + r], 1), :],
                                     buf_ref.at[sl, kk, pl.ds(r, 1), :], sem_ref.at[sl])

    def issue_step(step, sl):
        for r in range(tc):
            row_copy(step, sl, r, 0).start(priority=0)
            row_copy(step, sl, r, 1).start(priority=1)

    @pl.when(i == 0)
    def _():
        issue_step(0, 0)

    @pl.when(i + 1 < n)
    def _():
        issue_step(i + 1, 1 - slot)

    for kk in range(2):
        pltpu.make_async_copy(ys_ref.at[pl.ds(0, tc), :], buf_ref.at[slot, kk], sem_ref.at[slot]).wait()

    gate = gate_ref[...]
    o_ref[...] = x_ref[...] + gate[:, 0:1] * buf_ref[slot, 0] + gate[:, 1:2] * buf_ref[slot, 1]


def moe_combine(x2d, gates, ys, dest, tc=512):
    t, d = x2d.shape
    grid_spec = pltpu.PrefetchScalarGridSpec(
        num_scalar_prefetch=1, grid=(t // tc,),
        in_specs=[pl.BlockSpec((tc, d), lambda i, dest: (i, 0)), pl.BlockSpec((tc, 2), lambda i, dest: (i, 0)),
                  pl.BlockSpec(memory_space=pl.ANY)],
        out_specs=pl.BlockSpec((tc, d), lambda i, dest: (i, 0)),
        scratch_shapes=[pltpu.VMEM((2, 2, tc, d), F32), pltpu.SemaphoreType.DMA((2,))])
    return pl.pallas_call(
        functools.partial(_combine_body, tc=tc, ntok=t), grid_spec=grid_spec,
        out_shape=jax.ShapeDtypeStruct((t, d), F32),
        compiler_params=_params("arbitrary", disable_bounds_checks=True), name="moe_combine",
    )(dest, x2d, gates, ys)


def hier_moe_residual(x2d, route, counts_b, g, w1, w3, w2, layer, tm=512):
    t, d = x2d.shape
    eid = route[0:2].astype(I32).reshape(-1)
    rank = route[4:6].astype(I32).reshape(-1)
    gates = route[2:4].T
    counts = counts_b[:, 0].astype(I32)
    padded = (counts + tm - 1) // tm * tm
    pad_end = jnp.cumsum(padded)
    pad_start = pad_end - padded
    experts = jnp.arange(N_EXPERTS, dtype=I32)
    dest = rank + jnp.sum(jnp.where(eid[:, None] == experts[None, :], pad_start[None, :], 0), axis=1)
    cap = t * 2 + N_EXPERTS * tm
    nblk = cap // tm
    blk_row0 = jnp.arange(nblk, dtype=I32) * tm
    blk_e = jnp.minimum(jnp.sum((pad_end[None, :] <= blk_row0[:, None]).astype(I32), axis=1), N_EXPERTS - 1)
    nused = (pad_end[-1:] // tm).astype(I32)
    last_blk = jnp.where(padded > 0, pad_end // tm - 1, -1)
    tail_blk = jnp.where(nused[0] + experts < nblk, nused[0] + experts, -1)
    zero_blk = jnp.concatenate([last_blk, tail_blk]).astype(I32)
    xs = moe_dispatch(x2d, g, dest.astype(I32), zero_blk, cap, tm)
    ys = moe_experts(xs, blk_e, nused, w1, w3, w2, layer, tm)
    return moe_combine(x2d, gates, ys, dest.astype(I32))


def _even_layer(x2d, b, s, layer, mix_g, w_in, q_norm, k_norm, lam_q1, lam_k1, lam_q2, lam_k2, subln,
                hy_conv_w, hy_conv_b, f_w1, f_b1, f_freq, f_w2, f_b2, f_w3, hy_skip, w_out, route_params):
    d = x2d.shape[1]
    d_att = d // 2
    lambda_init = 0.8 - 0.6 * math.exp(-0.3 * layer)
    lam = jnp.exp(jnp.sum(lam_q1 * lam_k1)) - jnp.exp(jnp.sum(lam_q2 * lam_k2)) + lambda_init
    dk = q_norm.shape[0]
    qt, kp, vt, u_hy = even_in_proj(x2d, b, s, mix_g, w_in.astype(BF16), q_norm, k_norm,
                                    dk ** -0.5 * math.log2(math.e))
    y_att = diff_attention(qt, kp, vt, lam, subln, lambda_init)
    z, x1c, zp = hy_prep(u_hy.reshape(b, s, -1), 0, hy_conv_w, hy_conv_b)
    tables = fft_tables(s)
    kf = filter_spectrum(hyena_filter(s, f_w1, f_b1, f_freq, f_w2, f_b2, f_w3), tables[0], tables[1])
    y_hy = hy_fft_conv(zp, z, x1c, kf, tables, hy_skip)
    w_out = w_out.astype(BF16)
    return matmul_residual(x2d, [y_att.reshape(b * s, -1), y_hy.reshape(b * s, -1)],
                           [w_out[:d_att], w_out[d_att:]], route_params)


def _odd_layer(x2d, b, s, mix_g, w_in, conv_w, conv_b, gate_b, out_norm, w_out, route_params):
    d = x2d.shape[1]
    qk_w = conv_w.shape[1]
    main_w = qk_w + 2 * d
    ng = 4 * ML_HEADS
    w_gate = jnp.zeros((d, 128), F32).at[:, :ng].set(w_in[:, main_w:]).astype(BF16)
    u2d, ug = norm_matmul(x2d, mix_g, [w_in[:, :main_w].astype(BF16), w_gate], [BF16, F32])
    u = u2d.reshape(b, s, main_w)
    gcol = (ug[:, :ng] + gate_b).reshape(b, s, ng)
    grow = jnp.swapaxes(gcol, 1, 2)
    dk = qk_w // (2 * ML_HEADS)
    q = ml_prep(u, conv_w, conv_b, 0, qk_w // 2, dk ** -0.5, transpose=False)
    kt = ml_prep(u, conv_w, conv_b, qk_w // 2, qk_w // 2, 1.0, transpose=True)
    hf, hb = mlstm_scan(q, kt, u, qk_w // d, gcol, grow)
    return mlstm_out(x2d, hf.reshape(b * s, d), hb.reshape(b * s, d), u2d, (qk_w + d) // d, out_norm,
                     w_out.astype(BF16), route_params)


def kernel(x, mix_norm, ffn_norm, ev_w_in, ev_q_norm, ev_k_norm, ev_lam_q1, ev_lam_k1, ev_lam_q2, ev_lam_k2, ev_subln, ev_hy_conv_w, ev_hy_conv_b, ev_hy_f_w1, ev_hy_f_b1, ev_hy_f_freq, ev_hy_f_w2, ev_hy_f_b2, ev_hy_f_w3, ev_hy_skip, ev_w_out, od_w_in, od_conv_w, od_conv_b, od_gate_b, od_out_norm, od_w_out, moe_wg, moe_bg, moe_we, moe_be, moe_w1, moe_w3, moe_w2):
    b, s, d = x.shape
    depth = mix_norm.shape[0]
    x2d = x.reshape(b * s, d)
    for layer in range(depth):
        j = layer // 2
        route_params = (ffn_norm[layer], moe_wg[layer], moe_bg[layer], moe_we[layer], moe_be[layer])
        if layer % 2 == 0:
            x2d, route, counts = _even_layer(
                x2d, b, s, layer, mix_norm[layer], ev_w_in[j], ev_q_norm[j], ev_k_norm[j],
                ev_lam_q1[j], ev_lam_k1[j], ev_lam_q2[j], ev_lam_k2[j], ev_subln[j],
                ev_hy_conv_w[j], ev_hy_conv_b[j], ev_hy_f_w1[j], ev_hy_f_b1[j], ev_hy_f_freq[j],
                ev_hy_f_w2[j], ev_hy_f_b2[j], ev_hy_f_w3[j], ev_hy_skip[j], ev_w_out[j], route_params)
        else:
            x2d, route, counts = _odd_layer(x2d, b, s, mix_norm[layer], od_w_in[j], od_conv_w[j], od_conv_b[j],
                                            od_gate_b[j], od_out_norm[j], od_w_out[j], route_params)
        x2d = hier_moe_residual(x2d, route, counts, ffn_norm[layer], moe_w1, moe_w3, moe_w2, layer)
    return x2d.reshape(b, s, d)
```
